```python
import jax, jax.numpy as jnp
from jax import lax
import numpy as np

D_MODEL = 1024
BATCH = 8
SEQ = 8192
DEPTH = 2

CONV_CH = D_MODEL // 2
CONV_K = 3
SGU_WIDTH = D_MODEL // 2
SGU_GROUPS = 4
SGU_GROUP_CH = SGU_WIDTH // SGU_GROUPS
CHUNK = 128
ATT_HEADS = 8
HEAD_DIM = 64
ATT_WIDTH = ATT_HEADS * HEAD_DIM
Q_BLOCK = 128
N_BRANCH = 3
D_FF = ((8 * D_MODEL + 3 * 256 - 1) // (3 * 256)) * 256
IN_COLS = 3 * CONV_CH + 2 * SGU_WIDTH + 3 * ATT_WIDTH + N_BRANCH * D_MODEL
EPS = 1e-6

kernel_name = "hybrid_gated_conv_sgu_stickbreaking"


def _rmsnorm(x, g):
    xf = x.astype(jnp.float32)
    y = xf * lax.rsqrt(jnp.mean(xf * xf, axis=-1, keepdims=True) + EPS)
    return (y * g.astype(jnp.float32)).astype(x.dtype)


def _layernorm(x, g, b):
    xf = x.astype(jnp.float32)
    mu = jnp.mean(xf, axis=-1, keepdims=True)
    xc = xf - mu
    y = xc * lax.rsqrt(jnp.mean(xc * xc, axis=-1, keepdims=True) + EPS)
    return (y * g.astype(jnp.float32) + b.astype(jnp.float32)).astype(x.dtype)


def _split_cols(p):
    widths = (CONV_CH, CONV_CH, CONV_CH, SGU_WIDTH, SGU_WIDTH,
              ATT_WIDTH, ATT_WIDTH, ATT_WIDTH, N_BRANCH * D_MODEL)
    out, off = [], 0
    for w in widths:
        out.append(p[..., off:off + w])
        off += w
    return out


def _short_conv(bg, cg, xa, w, b):
    u = cg * xa
    S = u.shape[1]
    up = jnp.pad(u, ((0, 0), (CONV_K - 1, 0), (0, 0)))
    y = b
    for k in range(CONV_K):
        y = y + w[k] * up[:, k:k + S]
    return bg * y


def _sgu(u, v, ln_g, ln_b, w_s, b_s):
    u = jax.nn.gelu(u, approximate=False)
    v = _layernorm(jax.nn.gelu(v, approximate=False), ln_g, ln_b)
    B, S, _ = v.shape
    vc = v.reshape(B, S // CHUNK, CHUNK, SGU_GROUPS, SGU_GROUP_CH)
    causal = jnp.tril(jnp.ones((CHUNK, CHUNK), dtype=bool))
    w = jnp.where(causal[None], w_s, 0.0).astype(v.dtype)
    mixed = jnp.einsum('gts,bnsgc->bntgc', w, vc) + b_s.T[None, None, :, :, None]
    return u * mixed.reshape(B, S, SGU_WIDTH)


def _stick_breaking(q, k, v):
    B, H, S, d = q.shape
    nb = S // Q_BLOCK
    scale = 1.0 / float(np.sqrt(d))
    qb = q.reshape(B, H, nb, Q_BLOCK, d).transpose(2, 0, 1, 3, 4)
    key_pos = jnp.arange(S)

    def block(args):
        qi, i = args
        z = jnp.einsum('bhqd,bhkd->bhqk', qi, k).astype(jnp.float32) * scale
        qpos = i * Q_BLOCK + jnp.arange(Q_BLOCK)
        mask = key_pos[None, :] < qpos[:, None]
        log_beta = jax.nn.log_sigmoid(z)
        log_1m = jnp.where(mask, jax.nn.log_sigmoid(-z), 0.0)
        after = lax.cumsum(log_1m, axis=3, reverse=True) - log_1m
        a = jnp.where(mask, jnp.exp(log_beta + after), 0.0)
        return jnp.einsum('bhqk,bhkd->bhqd', a.astype(v.dtype), v)

    out = lax.map(block, (qb, jnp.arange(nb)))
    return out.transpose(1, 2, 0, 3, 4).reshape(B, H, S, d)


def _fwd_setup_inputs(seed: int = 0) -> dict:
    key = jax.random.key(seed)
    ks = jax.random.split(key, 17)
    L, D = DEPTH, D_MODEL
    nrm = lambda k, shape, s: jax.random.normal(k, shape, jnp.float32) * s
    return {
        "x": jax.random.normal(ks[0], (BATCH, SEQ, D), jnp.float32),
        "mix_norm_g": 1.0 + nrm(ks[1], (L, D), 0.02),
        "w_in": nrm(ks[2], (L, D, IN_COLS), D ** -0.5),
        "b_gate": nrm(ks[3], (L, N_BRANCH * D), 0.02),
        "conv_w": nrm(ks[4], (L, CONV_K, CONV_CH), CONV_K ** -0.5),
        "conv_b": nrm(ks[5], (L, CONV_CH), 0.02),
        "sgu_ln_g": 1.0 + nrm(ks[6], (L, SGU_WIDTH), 0.02),
        "sgu_ln_b": nrm(ks[7], (L, SGU_WIDTH), 0.02),
        "sgu_w": nrm(ks[8], (L, SGU_GROUPS, CHUNK, CHUNK), CHUNK ** -0.5),
        "sgu_b": nrm(ks[9], (L, SGU_GROUPS, CHUNK), 0.02),
        "q_norm_g": 1.0 + nrm(ks[10], (L, HEAD_DIM), 0.02),
        "k_norm_g": 1.0 + nrm(ks[11], (L, HEAD_DIM), 0.02),
        "w_branch_out": nrm(ks[12], (L, N_BRANCH, CONV_CH, D), CONV_CH ** -0.5),
        "w_o": nrm(ks[13], (L, D, D), D ** -0.5),
        "ffn_norm_g": 1.0 + nrm(ks[14], (L, D), 0.02),
        "w_gate_up": nrm(ks[15], (L, D, 2 * D_FF), D ** -0.5),
        "w_down": nrm(ks[16], (L, D_FF, D), D_FF ** -0.5),
    }


def _fwd_reference(x, mix_norm_g, w_in, b_gate, conv_w, conv_b, sgu_ln_g, sgu_ln_b, sgu_w, sgu_b,
              q_norm_g, k_norm_g, w_branch_out, w_o, ffn_norm_g, w_gate_up, w_down):
    B, S, D = x.shape
    for l in range(DEPTH):
        h = _rmsnorm(x, mix_norm_g[l])
        a_b, a_c, a_x, s_u, s_v, q, k, v, gates = _split_cols(h @ w_in[l])

        ya = _short_conv(a_b, a_c, a_x, conv_w[l], conv_b[l])
        yb = _sgu(s_u, s_v, sgu_ln_g[l], sgu_ln_b[l], sgu_w[l], sgu_b[l])

        qh = _rmsnorm(q.reshape(B, S, ATT_HEADS, HEAD_DIM), q_norm_g[l]).transpose(0, 2, 1, 3)
        kh = _rmsnorm(k.reshape(B, S, ATT_HEADS, HEAD_DIM), k_norm_g[l]).transpose(0, 2, 1, 3)
        vh = v.reshape(B, S, ATT_HEADS, HEAD_DIM).transpose(0, 2, 1, 3)
        yc = _stick_breaking(qh, kh, vh).transpose(0, 2, 1, 3).reshape(B, S, ATT_WIDTH)

        ys = jnp.stack([ya, yb, yc], axis=2)
        yd = jnp.einsum('bsnc,ncd->bsnd', ys, w_branch_out[l])
        g = jax.nn.sigmoid(gates + b_gate[l]).reshape(B, S, N_BRANCH, D)
        merged = jnp.sum(g * yd, axis=2)
        x = x + merged @ w_o[l]

        h2 = _rmsnorm(x, ffn_norm_g[l])
        gu = h2 @ w_gate_up[l]
        x = x + (jax.nn.silu(gu[..., :D_FF]) * gu[..., D_FF:]) @ w_down[l]
    return x


import jax as _jax
import jax.numpy as _jnp

TWIN_FORMAT = 'train_step'
FWD_PARAMS = ['x', 'mix_norm_g', 'w_in', 'b_gate', 'conv_w', 'conv_b', 'sgu_ln_g', 'sgu_ln_b', 'sgu_w', 'sgu_b', 'q_norm_g', 'k_norm_g', 'w_branch_out', 'w_o', 'ffn_norm_g', 'w_gate_up', 'w_down']
TWIN_WEIGHTS = ['mix_norm_g', 'w_in', 'b_gate', 'conv_w', 'conv_b', 'sgu_ln_g', 'sgu_ln_b', 'sgu_w', 'sgu_b', 'q_norm_g', 'k_norm_g', 'w_branch_out', 'w_o', 'ffn_norm_g', 'w_gate_up', 'w_down']
TWIN_DIFF_INPUT = 'x'
TWIN_INPUTS = ['x', 'mix_norm_g', 'w_in', 'b_gate', 'conv_w', 'conv_b', 'sgu_ln_g', 'sgu_ln_b', 'sgu_w', 'sgu_b', 'q_norm_g', 'k_norm_g', 'w_branch_out', 'w_o', 'ffn_norm_g', 'w_gate_up', 'w_down', 'loss_target', 'm_mix_norm_g', 'm_w_in', 'm_b_gate', 'm_conv_w', 'm_conv_b', 'm_sgu_ln_g', 'm_sgu_ln_b', 'm_sgu_w', 'm_sgu_b', 'm_q_norm_g', 'm_k_norm_g', 'm_w_branch_out', 'm_w_o', 'm_ffn_norm_g', 'm_w_gate_up', 'm_w_down', 'v_mix_norm_g', 'v_w_in', 'v_b_gate', 'v_conv_w', 'v_conv_b', 'v_sgu_ln_g', 'v_sgu_ln_b', 'v_sgu_w', 'v_sgu_b', 'v_q_norm_g', 'v_k_norm_g', 'v_w_branch_out', 'v_w_o', 'v_ffn_norm_g', 'v_w_gate_up', 'v_w_down']
TWIN_OUTPUTS = ['loss', 'grad_x', 'grad_mix_norm_g', 'grad_w_in', 'grad_b_gate', 'grad_conv_w', 'grad_conv_b', 'grad_sgu_ln_g', 'grad_sgu_ln_b', 'grad_sgu_w', 'grad_sgu_b', 'grad_q_norm_g', 'grad_k_norm_g', 'grad_w_branch_out', 'grad_w_o', 'grad_ffn_norm_g', 'grad_w_gate_up', 'grad_w_down', 'delta_mix_norm_g', 'delta_w_in', 'delta_b_gate', 'delta_conv_w', 'delta_conv_b', 'delta_sgu_ln_g', 'delta_sgu_ln_b', 'delta_sgu_w', 'delta_sgu_b', 'delta_q_norm_g', 'delta_k_norm_g', 'delta_w_branch_out', 'delta_w_o', 'delta_ffn_norm_g', 'delta_w_gate_up', 'delta_w_down', 'new_m_mix_norm_g', 'new_m_w_in', 'new_m_b_gate', 'new_m_conv_w', 'new_m_conv_b', 'new_m_sgu_ln_g', 'new_m_sgu_ln_b', 'new_m_sgu_w', 'new_m_sgu_b', 'new_m_q_norm_g', 'new_m_k_norm_g', 'new_m_w_branch_out', 'new_m_w_o', 'new_m_ffn_norm_g', 'new_m_w_gate_up', 'new_m_w_down', 'new_v_mix_norm_g', 'new_v_w_in', 'new_v_b_gate', 'new_v_conv_w', 'new_v_conv_b', 'new_v_sgu_ln_g', 'new_v_sgu_ln_b', 'new_v_sgu_w', 'new_v_sgu_b', 'new_v_q_norm_g', 'new_v_k_norm_g', 'new_v_w_branch_out', 'new_v_w_o', 'new_v_ffn_norm_g', 'new_v_w_gate_up', 'new_v_w_down']
TWIN_LEAF_KINDS = {'loss': 'loss', 'grad_x': 'grad_x', 'grad_mix_norm_g': 'grad_w', 'grad_w_in': 'grad_w', 'grad_b_gate': 'grad_w', 'grad_conv_w': 'grad_w', 'grad_conv_b': 'grad_w', 'grad_sgu_ln_g': 'grad_w', 'grad_sgu_ln_b': 'grad_w', 'grad_sgu_w': 'grad_w', 'grad_sgu_b': 'grad_w', 'grad_q_norm_g': 'grad_w', 'grad_k_norm_g': 'grad_w', 'grad_w_branch_out': 'grad_w', 'grad_w_o': 'grad_w', 'grad_ffn_norm_g': 'grad_w', 'grad_w_gate_up': 'grad_w', 'grad_w_down': 'grad_w', 'delta_mix_norm_g': 'delta_w', 'delta_w_in': 'delta_w', 'delta_b_gate': 'delta_w', 'delta_conv_w': 'delta_w', 'delta_conv_b': 'delta_w', 'delta_sgu_ln_g': 'delta_w', 'delta_sgu_ln_b': 'delta_w', 'delta_sgu_w': 'delta_w', 'delta_sgu_b': 'delta_w', 'delta_q_norm_g': 'delta_w', 'delta_k_norm_g': 'delta_w', 'delta_w_branch_out': 'delta_w', 'delta_w_o': 'delta_w', 'delta_ffn_norm_g': 'delta_w', 'delta_w_gate_up': 'delta_w', 'delta_w_down': 'delta_w', 'new_m_mix_norm_g': 'new_m', 'new_m_w_in': 'new_m', 'new_m_b_gate': 'new_m', 'new_m_conv_w': 'new_m', 'new_m_conv_b': 'new_m', 'new_m_sgu_ln_g': 'new_m', 'new_m_sgu_ln_b': 'new_m', 'new_m_sgu_w': 'new_m', 'new_m_sgu_b': 'new_m', 'new_m_q_norm_g': 'new_m', 'new_m_k_norm_g': 'new_m', 'new_m_w_branch_out': 'new_m', 'new_m_w_o': 'new_m', 'new_m_ffn_norm_g': 'new_m', 'new_m_w_gate_up': 'new_m', 'new_m_w_down': 'new_m', 'new_v_mix_norm_g': 'new_v', 'new_v_w_in': 'new_v', 'new_v_b_gate': 'new_v', 'new_v_conv_w': 'new_v', 'new_v_conv_b': 'new_v', 'new_v_sgu_ln_g': 'new_v', 'new_v_sgu_ln_b': 'new_v', 'new_v_sgu_w': 'new_v', 'new_v_sgu_b': 'new_v', 'new_v_q_norm_g': 'new_v', 'new_v_k_norm_g': 'new_v', 'new_v_w_branch_out': 'new_v', 'new_v_w_o': 'new_v', 'new_v_ffn_norm_g': 'new_v', 'new_v_w_gate_up': 'new_v', 'new_v_w_down': 'new_v'}


def _forward(args):
    return _fwd_reference(*[args[k] for k in FWD_PARAMS])


def _output_shape():
    def fwd():
        inp = _fwd_setup_inputs(0)
        return _fwd_reference(*[inp[k] for k in FWD_PARAMS])
    out = _jax.eval_shape(fwd)
    return out.shape, out.dtype

N_MICROBATCH = 1
ADAM_LR = 0.001
ADAM_B1 = 0.9
ADAM_B2 = 0.999
ADAM_EPS = 1e-08
ADAM_WD = 0.01
ADAM_STEP = 10
PER_EXAMPLE_BATCH_AXIS = {'x': 0, 'loss_target': 0}
SHARED_INPUTS = []
_WEIGHT_DTYPES = {'mix_norm_g': _jnp.float32, 'w_in': _jnp.float32, 'b_gate': _jnp.float32, 'conv_w': _jnp.float32, 'conv_b': _jnp.float32, 'sgu_ln_g': _jnp.float32, 'sgu_ln_b': _jnp.float32, 'sgu_w': _jnp.float32, 'sgu_b': _jnp.float32, 'q_norm_g': _jnp.float32, 'k_norm_g': _jnp.float32, 'w_branch_out': _jnp.float32, 'w_o': _jnp.float32, 'ffn_norm_g': _jnp.float32, 'w_gate_up': _jnp.float32, 'w_down': _jnp.float32}
MOMENT_SCALE = {'mix_norm_g': 7.188873e+01, 'w_in': 6.374299e-01, 'b_gate': 4.186200e+00, 'conv_w': 2.127496e+01, 'conv_b': 1.668256e+00, 'sgu_ln_g': 7.872364e+00, 'sgu_ln_b': 3.813892e-01, 'sgu_w': 7.137087e-01, 'sgu_b': 5.168901e-01, 'q_norm_g': 1.857442e+01, 'k_norm_g': 1.853561e+01, 'w_branch_out': 6.813737e-01, 'w_o': 1.135664e+00, 'ffn_norm_g': 4.962822e+01, 'w_gate_up': 3.735885e-01, 'w_down': 6.492140e-01}


def _to_microbatches(a, axis):
    t = _jnp.moveaxis(a, axis, 0)
    t = t.reshape((N_MICROBATCH, t.shape[0] // N_MICROBATCH) + t.shape[1:])
    return _jnp.moveaxis(t, 1, axis + 1)


def setup_inputs(seed: int = 0) -> dict:
    inp = _fwd_setup_inputs(seed)
    key = _jax.random.fold_in(_jax.random.key(seed), 7919)
    shape, _ = _output_shape()
    out = dict(inp)
    out["loss_target"] = _jax.random.normal(_jax.random.fold_in(key, 0), shape, _jnp.float32)
    for i, name in enumerate(TWIN_WEIGHTS):
        w = inp[name].astype(_jnp.float32)
        if MOMENT_SCALE is None:
            s = _jnp.sqrt(_jnp.mean(_jnp.square(w)) + 1e-30)
        else:
            s = MOMENT_SCALE[name]
        km, kv = _jax.random.split(_jax.random.fold_in(key, i + 1))
        out[name] = w
        out["m_" + name] = s * _jax.random.normal(km, w.shape, _jnp.float32)
        out["v_" + name] = (s * s) * _jax.random.uniform(kv, w.shape, _jnp.float32, 0.5, 1.5)
    if N_MICROBATCH > 1:
        for name, axis in PER_EXAMPLE_BATCH_AXIS.items():
            out[name] = _to_microbatches(out[name], axis)
    return {'x': out['x'], 'mix_norm_g': out['mix_norm_g'], 'w_in': out['w_in'], 'b_gate': out['b_gate'], 'conv_w': out['conv_w'], 'conv_b': out['conv_b'], 'sgu_ln_g': out['sgu_ln_g'], 'sgu_ln_b': out['sgu_ln_b'], 'sgu_w': out['sgu_w'], 'sgu_b': out['sgu_b'], 'q_norm_g': out['q_norm_g'], 'k_norm_g': out['k_norm_g'], 'w_branch_out': out['w_branch_out'], 'w_o': out['w_o'], 'ffn_norm_g': out['ffn_norm_g'], 'w_gate_up': out['w_gate_up'], 'w_down': out['w_down'], 'loss_target': out['loss_target'], 'm_mix_norm_g': out['m_mix_norm_g'], 'm_w_in': out['m_w_in'], 'm_b_gate': out['m_b_gate'], 'm_conv_w': out['m_conv_w'], 'm_conv_b': out['m_conv_b'], 'm_sgu_ln_g': out['m_sgu_ln_g'], 'm_sgu_ln_b': out['m_sgu_ln_b'], 'm_sgu_w': out['m_sgu_w'], 'm_sgu_b': out['m_sgu_b'], 'm_q_norm_g': out['m_q_norm_g'], 'm_k_norm_g': out['m_k_norm_g'], 'm_w_branch_out': out['m_w_branch_out'], 'm_w_o': out['m_w_o'], 'm_ffn_norm_g': out['m_ffn_norm_g'], 'm_w_gate_up': out['m_w_gate_up'], 'm_w_down': out['m_w_down'], 'v_mix_norm_g': out['v_mix_norm_g'], 'v_w_in': out['v_w_in'], 'v_b_gate': out['v_b_gate'], 'v_conv_w': out['v_conv_w'], 'v_conv_b': out['v_conv_b'], 'v_sgu_ln_g': out['v_sgu_ln_g'], 'v_sgu_ln_b': out['v_sgu_ln_b'], 'v_sgu_w': out['v_sgu_w'], 'v_sgu_b': out['v_sgu_b'], 'v_q_norm_g': out['v_q_norm_g'], 'v_k_norm_g': out['v_k_norm_g'], 'v_w_branch_out': out['v_w_branch_out'], 'v_w_o': out['v_w_o'], 'v_ffn_norm_g': out['v_ffn_norm_g'], 'v_w_gate_up': out['v_w_gate_up'], 'v_w_down': out['v_w_down']}


def _loss(weights, diff, rest, loss_target):
    with _jax.named_scope("forward"):
        args = {**rest, TWIN_DIFF_INPUT: diff, **{k: w.astype(_WEIGHT_DTYPES[k]) for k, w in weights.items()}}
        y = _forward(args)
    with _jax.named_scope("loss_head"):
        err = _jnp.square(y.astype(_jnp.float32) - loss_target)
        return 0.5 * _jnp.sum(_jnp.mean(err, axis=-1)) if err.ndim else 0.5 * err


def _adamw(w, g, m, v):
    m = ADAM_B1 * m + (1.0 - ADAM_B1) * g
    v = ADAM_B2 * v + (1.0 - ADAM_B2) * _jnp.square(g)
    m_hat = m / (1.0 - ADAM_B1 ** ADAM_STEP)
    v_hat = v / (1.0 - ADAM_B2 ** ADAM_STEP)
    delta = -ADAM_LR * (m_hat / (_jnp.sqrt(v_hat) + ADAM_EPS) + ADAM_WD * w)
    return delta, m, v


def reference(x, mix_norm_g, w_in, b_gate, conv_w, conv_b, sgu_ln_g, sgu_ln_b, sgu_w, sgu_b, q_norm_g, k_norm_g, w_branch_out, w_o, ffn_norm_g, w_gate_up, w_down, loss_target, m_mix_norm_g, m_w_in, m_b_gate, m_conv_w, m_conv_b, m_sgu_ln_g, m_sgu_ln_b, m_sgu_w, m_sgu_b, m_q_norm_g, m_k_norm_g, m_w_branch_out, m_w_o, m_ffn_norm_g, m_w_gate_up, m_w_down, v_mix_norm_g, v_w_in, v_b_gate, v_conv_w, v_conv_b, v_sgu_ln_g, v_sgu_ln_b, v_sgu_w, v_sgu_b, v_q_norm_g, v_k_norm_g, v_w_branch_out, v_w_o, v_ffn_norm_g, v_w_gate_up, v_w_down):
    given = dict(x=x, mix_norm_g=mix_norm_g, w_in=w_in, b_gate=b_gate, conv_w=conv_w, conv_b=conv_b, sgu_ln_g=sgu_ln_g, sgu_ln_b=sgu_ln_b, sgu_w=sgu_w, sgu_b=sgu_b, q_norm_g=q_norm_g, k_norm_g=k_norm_g, w_branch_out=w_branch_out, w_o=w_o, ffn_norm_g=ffn_norm_g, w_gate_up=w_gate_up, w_down=w_down, loss_target=loss_target, m_mix_norm_g=m_mix_norm_g, m_w_in=m_w_in, m_b_gate=m_b_gate, m_conv_w=m_conv_w, m_conv_b=m_conv_b, m_sgu_ln_g=m_sgu_ln_g, m_sgu_ln_b=m_sgu_ln_b, m_sgu_w=m_sgu_w, m_sgu_b=m_sgu_b, m_q_norm_g=m_q_norm_g, m_k_norm_g=m_k_norm_g, m_w_branch_out=m_w_branch_out, m_w_o=m_w_o, m_ffn_norm_g=m_ffn_norm_g, m_w_gate_up=m_w_gate_up, m_w_down=m_w_down, v_mix_norm_g=v_mix_norm_g, v_w_in=v_w_in, v_b_gate=v_b_gate, v_conv_w=v_conv_w, v_conv_b=v_conv_b, v_sgu_ln_g=v_sgu_ln_g, v_sgu_ln_b=v_sgu_ln_b, v_sgu_w=v_sgu_w, v_sgu_b=v_sgu_b, v_q_norm_g=v_q_norm_g, v_k_norm_g=v_k_norm_g, v_w_branch_out=v_w_branch_out, v_w_o=v_w_o, v_ffn_norm_g=v_ffn_norm_g, v_w_gate_up=v_w_gate_up, v_w_down=v_w_down)
    weights = {n: given[n] for n in TWIN_WEIGHTS}
    shared = {n: given[n] for n in SHARED_INPUTS}
    per_example = {n: given[n] for n in ['x']}
    grad_fn = _jax.value_and_grad(_loss, argnums=(0, 1))

    def one_microbatch(ex, loss_target):
        ex = dict(ex)
        diff = ex.pop(TWIN_DIFF_INPUT)
        return grad_fn(weights, diff, {**shared, **ex}, loss_target)

    if N_MICROBATCH == 1:
        loss, (grad_w, grad_x) = one_microbatch(per_example, given["loss_target"])
    else:
        def body(carry, xs):
            loss_sum, grad_sum = carry
            l_k, (gw_k, gx_k) = one_microbatch(xs[0], xs[1])
            with _jax.named_scope("update"):
                return (loss_sum + l_k, _jax.tree.map(_jnp.add, grad_sum, gw_k)), gx_k

        init = (_jnp.zeros((), _jnp.float32), _jax.tree.map(_jnp.zeros_like, weights))
        (loss, grad_w), grad_x = _jax.lax.scan(body, init, (per_example, given["loss_target"]))
    with _jax.named_scope("update"):
        delta_w, new_m, new_v = {}, {}, {}
        for n in TWIN_WEIGHTS:
            delta_w[n], new_m[n], new_v[n] = _adamw(weights[n], grad_w[n], given["m_" + n], given["v_" + n])
    return (loss, grad_x, *[grad_w[n] for n in TWIN_WEIGHTS], *[delta_w[n] for n in TWIN_WEIGHTS],
            *[new_m[n] for n in TWIN_WEIGHTS], *[new_v[n] for n in TWIN_WEIGHTS])
```

```python
import functools
import math

import jax
import jax.numpy as jnp
from jax import lax
from jax.experimental import pallas as pl
from jax.experimental.pallas import tpu as pltpu

F32 = jnp.float32
MXU_DTYPE = jnp.bfloat16

EPS = 1e-6
CONV_K = 3
CHUNK = 128
HEADS = 8
HEAD_DIM = 64
ATT_BLOCK = 128
EXP_UNDERFLOW = -104.0

ADAM_LR = 0.001
ADAM_B1 = 0.9
ADAM_B2 = 0.999
ADAM_EPS = 1e-08
ADAM_WD = 0.01
ADAM_STEP = 10

VMEM_LIMIT = 56 * 1024 * 1024
MESH = pl.DeviceIdType.MESH
N_CHIPS = 4
N_DEV = 8
ANY = pl.BlockSpec(memory_space=pl.ANY)


def _params(**kw):
    return pltpu.CompilerParams(vmem_limit_bytes=VMEM_LIMIT, **kw)


def _mx(v):
    return v.astype(MXU_DTYPE)


def _dot(a, b):
    return lax.dot_general(a, b, (((1,), (0,)), ((), ())), preferred_element_type=F32)


def _dot_nt(a, b):
    return lax.dot_general(a, b, (((1,), (1,)), ((), ())), preferred_element_type=F32)


def _dot_tn(a, b):
    return lax.dot_general(a, b, (((0,), (0,)), ((), ())), preferred_element_type=F32)


def _dot_split(const, v):
    hi = _mx(v)
    lo = _mx(v - hi.astype(F32))
    return _dot(const, hi) + _dot(const, lo)


def _dot_split_r(v, const):
    hi = _mx(v)
    lo = _mx(v - hi.astype(F32))
    return _dot(hi, const) + _dot(lo, const)


def _sigmoid(x):
    return 1.0 / (1.0 + jnp.exp(-x))


_INV_SQRT2 = 1.0 / math.sqrt(2.0)
_INV_SQRT2PI = 1.0 / math.sqrt(2.0 * math.pi)


def _gelu(x):
    return 0.5 * x * (1.0 + lax.erf(x * _INV_SQRT2))


def _gelu_grad(x):
    return 0.5 * (1.0 + lax.erf(x * _INV_SQRT2)) + x * jnp.exp(-0.5 * x * x) * _INV_SQRT2PI


def _row_fold(v):
    m, n = v.shape
    return jnp.sum(v.reshape(m // 8, 8, n), axis=0)


def _tile(m, pref):
    t = min(m, pref)
    while m % t:
        t //= 2
    return t


def _norm_matmul(x, g, w, *, name, tm=512, tn=None):
    s, d = x.shape
    n = w.shape[1]
    tm = _tile(s, tm)
    tn = tn or n
    assert n % tn == 0

    def body(x_ref, g_ref, w_ref, p_ref, h_ref):
        @pl.when(pl.program_id(1) == 0)
        def _():
            xf = x_ref[...]
            r = lax.rsqrt(jnp.mean(xf * xf, axis=-1, keepdims=True) + EPS)
            h_ref[...] = _mx(xf * r * g_ref[...])

        p_ref[...] = _dot(h_ref[...], w_ref[...])

    return pl.pallas_call(
        body,
        name=name,
        grid=(s // tm, n // tn),
        in_specs=[
            pl.BlockSpec((tm, d), lambda i, j: (i, 0)),
            pl.BlockSpec((1, d), lambda i, j: (0, 0)),
            pl.BlockSpec((d, tn), lambda i, j: (0, j)),
        ],
        out_specs=[
            pl.BlockSpec((tm, tn), lambda i, j: (i, j)),
            pl.BlockSpec((tm, d), lambda i, j: (i, 0)),
        ],
        out_shape=[jax.ShapeDtypeStruct((s, n), F32), jax.ShapeDtypeStruct((s, d), MXU_DTYPE)],
        compiler_params=_params(),
    )(x, g, w)


def _conv_taps(u, prev):
    row = lax.broadcasted_iota(jnp.int32, u.shape, 0)
    um1 = jnp.where(row == 0, prev[7:8, :], pltpu.roll(u, 1, 0))
    um2 = pltpu.roll(u, 2, 0)
    um2 = jnp.where(row == 0, prev[6:7, :], jnp.where(row == 1, prev[7:8, :], um2))
    return um1, um2


def _conv_fwd(p, conv_w, conv_b, *, name, tm=512):
    s = p.shape[0]
    c = conv_w.shape[1]
    tm = _tile(s, tm)
    hb = tm // 8

    def body(bg_ref, cg_ref, xa_ref, cgp_ref, xap_ref, w_ref, b_ref, y_ref):
        first = pl.program_id(0) == 0
        u = cg_ref[...] * xa_ref[...]
        prev = jnp.where(first, 0.0, cgp_ref[...] * xap_ref[...])
        um1, um2 = _conv_taps(u, prev)
        w = w_ref[...]
        y = b_ref[...] + w[0:1, :] * um2 + w[1:2, :] * um1 + w[2:3, :] * u
        y_ref[...] = _mx(bg_ref[...] * y)

    halo = lambda col: pl.BlockSpec((8, c), lambda i: (jnp.maximum(i * hb - 1, 0), col))
    return pl.pallas_call(
        body,
        name=name,
        grid=(s // tm,),
        in_specs=[
            pl.BlockSpec((tm, c), lambda i: (i, 0)),
            pl.BlockSpec((tm, c), lambda i: (i, 1)),
            pl.BlockSpec((tm, c), lambda i: (i, 2)),
            halo(1),
            halo(2),
            pl.BlockSpec((CONV_K, c), lambda i: (0, 0)),
            pl.BlockSpec((1, c), lambda i: (0, 0)),
        ],
        out_specs=pl.BlockSpec((tm, c), lambda i: (i, 0)),
        out_shape=jax.ShapeDtypeStruct((s, c), MXU_DTYPE),
        compiler_params=_params(),
    )(p, p, p, p, p, conv_w, conv_b)


def _layernorm_stats(x):
    mu = jnp.mean(x, axis=-1, keepdims=True)
    xc = x - mu
    r = lax.rsqrt(jnp.mean(xc * xc, axis=-1, keepdims=True) + EPS)
    return xc * r, r


def _sgu_fwd(p, ln_g, ln_b, wtril, bias_full, *, col0, name, tm=512):
    s = p.shape[0]
    c = ln_g.shape[1]
    groups = c // CHUNK
    tm = _tile(s, tm)

    def body(u_ref, v_ref, g_ref, b_ref, w_ref, bias_ref, y_ref):
        vn, _ = _layernorm_stats(_gelu(v_ref[...]))
        vb = _mx(vn * g_ref[...] + b_ref[...])
        for n in range(tm // CHUNK):
            rows = slice(n * CHUNK, (n + 1) * CHUNK)
            for gi in range(groups):
                cols = slice(gi * CHUNK, (gi + 1) * CHUNK)
                mixed = _dot(w_ref[gi], vb[rows, cols]) + bias_ref[:, cols]
                y_ref[rows, cols] = _mx(_gelu(u_ref[rows, cols]) * mixed)

    return pl.pallas_call(
        body,
        name=name,
        grid=(s // tm,),
        in_specs=[
            pl.BlockSpec((tm, c), lambda i: (i, col0)),
            pl.BlockSpec((tm, c), lambda i: (i, col0 + 1)),
            pl.BlockSpec((1, c), lambda i: (0, 0)),
            pl.BlockSpec((1, c), lambda i: (0, 0)),
            pl.BlockSpec((groups, CHUNK, CHUNK), lambda i: (0, 0, 0)),
            pl.BlockSpec((CHUNK, c), lambda i: (0, 0)),
        ],
        out_specs=pl.BlockSpec((tm, c), lambda i: (i, 0)),
        out_shape=jax.ShapeDtypeStruct((s, c), MXU_DTYPE),
        compiler_params=_params(),
    )(p, p, ln_g, ln_b, wtril, bias_full)


def _head_mean(v, headavg):
    return _dot_split_r(v, headavg)


def _qkv_prep(p, qg, kg, headavg, *, col0, name, tm=512):
    s = p.shape[0]
    c = qg.shape[1]
    tm = _tile(s, tm)

    def body(q_ref, k_ref, v_ref, qg_ref, kg_ref, avg_ref, qn_ref, kn_ref, vb_ref):
        for src, g_ref, dst in ((q_ref, qg_ref, qn_ref), (k_ref, kg_ref, kn_ref)):
            xf = src[...]
            r = lax.rsqrt(_head_mean(xf * xf, avg_ref[...]) + EPS)
            dst[...] = _mx(xf * r * g_ref[...])
        vb_ref[...] = _mx(v_ref[...])

    blk = lambda col: pl.BlockSpec((tm, c), lambda i: (i, col))
    vec = pl.BlockSpec((1, c), lambda i: (0, 0))
    out = pl.BlockSpec((tm, c), lambda i: (i, 0))
    return pl.pallas_call(
        body,
        name=name,
        grid=(s // tm,),
        in_specs=[blk(col0), blk(col0 + 1), blk(col0 + 2), vec, vec, pl.BlockSpec((c, c), lambda i: (0, 0))],
        out_specs=[out, out, out],
        out_shape=[jax.ShapeDtypeStruct((s, c), MXU_DTYPE)] * 3,
        compiler_params=_params(),
    )(p, p, p, qg, kg, headavg)


def _att_block(k_blk, q_t, scale, key0, qry0):
    t = k_blk.shape[0]
    z = _dot(k_blk, q_t) * scale
    e = jnp.exp(-jnp.abs(z))
    lb = jnp.minimum(z, 0.0) - jnp.log1p(e)
    kpos = key0 + lax.broadcasted_iota(jnp.int32, (t, t), 0)
    qpos = qry0 + lax.broadcasted_iota(jnp.int32, (t, t), 1)
    mask = kpos < qpos
    l1m = jnp.where(mask, lb - z, 0.0)
    return z, e, lb, l1m, mask


def _att_fwd(k_hm, qt_hm, vt_hm, later, *, name):
    h, nb, t, d = k_hm.shape
    s = nb * t
    scale = 1.0 / math.sqrt(d)

    def body(k_ref, qt_ref, vt_ref, later_ref, o_ref):
        i = pl.program_id(1)
        q_t = qt_ref[...]

        def cond(carry):
            j, _, _, rmax = carry
            return jnp.logical_and(j >= 0, rmax > EXP_UNDERFLOW)

        def step(carry):
            j, run, acc, _ = carry
            _, _, lb, l1m, mask = _att_block(k_ref[j], q_t, scale, j * t, i * t)
            after = _dot_split(later_ref[...], l1m) + run
            a = jnp.where(mask, jnp.exp(lb + after), 0.0)
            acc = acc + _dot(vt_ref[j], _mx(a))
            run = run + jnp.sum(l1m, axis=0, keepdims=True)
            return j - 1, run, acc, jnp.max(run)

        init = (i, jnp.zeros((1, t), F32), jnp.zeros((d, t), F32), jnp.float32(0.0))
        _, _, acc, _ = lax.while_loop(cond, step, init)
        o_ref[...] = _mx(acc)

    return pl.pallas_call(
        body,
        name=name,
        grid=(h, nb),
        in_specs=[
            pl.BlockSpec((None, nb, t, d), lambda hh, i: (hh, 0, 0, 0)),
            pl.BlockSpec((None, d, t), lambda hh, i: (hh, 0, i)),
            pl.BlockSpec((None, nb, d, t), lambda hh, i: (hh, 0, 0, 0)),
            pl.BlockSpec((t, t), lambda hh, i: (0, 0)),
        ],
        out_specs=pl.BlockSpec((None, d, t), lambda hh, i: (hh, 0, i)),
        out_shape=jax.ShapeDtypeStruct((h, d, s), MXU_DTYPE),
        compiler_params=_params(),
    )(k_hm, qt_hm, vt_hm, later)


def _merge_fwd(ys, p, b_gate, w_bo, w_o, x, *, gate_col0, name, tm=256):
    s, d = x.shape
    nbr, c, _ = w_bo.shape
    tm = _tile(s, tm)

    def body(ya_ref, yb_ref, yc_ref, g0_ref, g1_ref, g2_ref, bg_ref, wbo_ref, wo_ref, x_ref, x1_ref, m_ref):
        merged = jnp.zeros((tm, d), F32)
        for i, (y_ref, g_ref) in enumerate(((ya_ref, g0_ref), (yb_ref, g1_ref), (yc_ref, g2_ref))):
            gate = _sigmoid(g_ref[...] + bg_ref[:, i * d:(i + 1) * d])
            merged = merged + gate * _dot(y_ref[...], wbo_ref[i])
        mb = _mx(merged)
        m_ref[...] = mb
        x1_ref[...] = x_ref[...] + _dot(mb, wo_ref[...])

    yblk = pl.BlockSpec((tm, c), lambda i: (i, 0))
    gblk = lambda k: pl.BlockSpec((tm, d), lambda i: (i, gate_col0 + k))
    xblk = pl.BlockSpec((tm, d), lambda i: (i, 0))
    return pl.pallas_call(
        body,
        name=name,
        grid=(s // tm,),
        in_specs=[
            yblk, yblk, yblk, gblk(0), gblk(1), gblk(2),
            pl.BlockSpec((1, nbr * d), lambda i: (0, 0)),
            pl.BlockSpec((nbr, c, d), lambda i: (0, 0, 0)),
            pl.BlockSpec((d, d), lambda i: (0, 0)),
            xblk,
        ],
        out_specs=[xblk, xblk],
        out_shape=[jax.ShapeDtypeStruct((s, d), F32), jax.ShapeDtypeStruct((s, d), MXU_DTYPE)],
        compiler_params=_params(),
    )(*ys, p, p, p, b_gate, w_bo, w_o, x)


def _ffn_down(gu, w_down, x, *, name, tm=256):
    s, d = x.shape
    f = w_down.shape[0]
    tm = _tile(s, tm)

    def body(g_ref, u_ref, w_ref, x_ref, o_ref):
        g = g_ref[...]
        act = _mx(g * _sigmoid(g) * u_ref[...])
        o_ref[...] = x_ref[...] + _dot(act, w_ref[...])

    return pl.pallas_call(
        body,
        name=name,
        grid=(s // tm,),
        in_specs=[
            pl.BlockSpec((tm, f), lambda i: (i, 0)),
            pl.BlockSpec((tm, f), lambda i: (i, 1)),
            pl.BlockSpec((f, d), lambda i: (0, 0)),
            pl.BlockSpec((tm, d), lambda i: (i, 0)),
        ],
        out_specs=pl.BlockSpec((tm, d), lambda i: (i, 0)),
        out_shape=jax.ShapeDtypeStruct((s, d), F32),
        compiler_params=_params(),
    )(gu, gu, w_down, x)


def _loss_head(y, target, *, name, tm=512):
    s, d = y.shape
    tm = _tile(s, tm)

    def body(y_ref, t_ref, l_ref, dy_ref):
        @pl.when(pl.program_id(0) == 0)
        def _():
            l_ref[...] = jnp.zeros_like(l_ref)

        err = y_ref[...] - t_ref[...]
        dy_ref[...] = err * (1.0 / d)
        sq = _row_fold(err * err)
        part = sq[:, 0:128]
        for k in range(1, d // 128):
            part = part + sq[:, k * 128:(k + 1) * 128]
        l_ref[...] += part * (0.5 / d)

    blk = pl.BlockSpec((tm, d), lambda i: (i, 0))
    return pl.pallas_call(
        body,
        name=name,
        grid=(s // tm,),
        in_specs=[blk, blk],
        out_specs=[pl.BlockSpec((8, 128), lambda i: (0, 0)), blk],
        out_shape=[jax.ShapeDtypeStruct((8, 128), F32), jax.ShapeDtypeStruct((s, d), F32)],
        compiler_params=_params(),
    )(y, target)


def _matmul_tn(a, b, *, name, t1=512, tn=None, tm=1024):
    m, k1 = a.shape
    n = b.shape[1]
    t1 = _tile(k1, t1)
    tn = tn or n
    tm = _tile(m, tm)
    steps = m // tm
    assert n % tn == 0

    def body(a_ref, b_ref, o_ref):
        @pl.when(pl.program_id(2) == 0)
        def _():
            o_ref[...] = jnp.zeros_like(o_ref)

        o_ref[...] += _dot_tn(a_ref[...], b_ref[...])

    return pl.pallas_call(
        body,
        name=name,
        grid=(k1 // t1, n // tn, steps),
        in_specs=[
            pl.BlockSpec((tm, t1), lambda i, j, k: (k, i)),
            pl.BlockSpec((tm, tn), lambda i, j, k: (k, j)),
        ],
        out_specs=pl.BlockSpec((t1, tn), lambda i, j, k: (i, j)),
        out_shape=jax.ShapeDtypeStruct((k1, n), F32),
        compiler_params=_params(),
    )(a, b)


def _matmul_nt_normbwd(dp, w, x, g, dres, *, name, tm=512, tk=None):
    s, d = x.shape
    k = w.shape[1]
    tm = _tile(s, tm)
    tk = tk or k
    steps = k // tk
    assert k % tk == 0

    def body(dp_ref, w_ref, x_ref, g_ref, dres_ref, dx_ref, dg_ref, acc_ref):
        kk = pl.program_id(1)

        @pl.when(jnp.logical_and(pl.program_id(0) == 0, kk == 0))
        def _():
            dg_ref[...] = jnp.zeros_like(dg_ref)

        @pl.when(kk == 0)
        def _():
            acc_ref[...] = jnp.zeros_like(acc_ref)

        acc_ref[...] += _dot_nt(dp_ref[...], w_ref[...])

        @pl.when(kk == steps - 1)
        def _():
            dh = acc_ref[...]
            xf = x_ref[...]
            r = lax.rsqrt(jnp.mean(xf * xf, axis=-1, keepdims=True) + EPS)
            y = xf * r
            dy = dh * g_ref[...]
            dx_ref[...] = dres_ref[...] + r * (dy - y * jnp.mean(dy * y, axis=-1, keepdims=True))
            dg_ref[...] += _row_fold(dh * y)

    xblk = pl.BlockSpec((tm, d), lambda i, j: (i, 0))
    return pl.pallas_call(
        body,
        name=name,
        grid=(s // tm, steps),
        in_specs=[
            pl.BlockSpec((tm, tk), lambda i, j: (i, j)),
            pl.BlockSpec((d, tk), lambda i, j: (0, j)),
            xblk,
            pl.BlockSpec((1, d), lambda i, j: (0, 0)),
            xblk,
        ],
        out_specs=[xblk, pl.BlockSpec((8, d), lambda i, j: (0, 0))],
        out_shape=[jax.ShapeDtypeStruct((s, d), F32), jax.ShapeDtypeStruct((8, d), F32)],
        scratch_shapes=[pltpu.VMEM((tm, d), F32)],
        compiler_params=_params(),
    )(dp, w, x, g, dres)


def _ffn_bwd(dx, gu, w_down, *, name, tm=256):
    s, d = dx.shape
    f = w_down.shape[0]
    tm = _tile(s, tm)

    def body(dx_ref, g_ref, u_ref, w_ref, dgu_ref, act_ref, dxb_ref):
        dxb = _mx(dx_ref[...])
        dxb_ref[...] = dxb
        dact = _dot_nt(dxb, w_ref[...])
        g = g_ref[...]
        u = u_ref[...]
        sg = _sigmoid(g)
        silu = g * sg
        act_ref[...] = _mx(silu * u)
        dgu_ref[:, 0:f] = _mx(dact * u * (sg * (1.0 + g * (1.0 - sg))))
        dgu_ref[:, f:2 * f] = _mx(dact * silu)

    fblk = lambda col: pl.BlockSpec((tm, f), lambda i: (i, col))
    dblk = pl.BlockSpec((tm, d), lambda i: (i, 0))
    return pl.pallas_call(
        body,
        name=name,
        grid=(s // tm,),
        in_specs=[dblk, fblk(0), fblk(1), pl.BlockSpec((f, d), lambda i: (0, 0))],
        out_specs=[pl.BlockSpec((tm, 2 * f), lambda i: (i, 0)), fblk(0), dblk],
        out_shape=[
            jax.ShapeDtypeStruct((s, 2 * f), MXU_DTYPE),
            jax.ShapeDtypeStruct((s, f), MXU_DTYPE),
            jax.ShapeDtypeStruct((s, d), MXU_DTYPE),
        ],
        compiler_params=_params(),
    )(dx, gu, gu, w_down)


def _merge_bwd(dx, ys, p, b_gate, w_bo, w_o, *, gate_col0, name, tm=256):
    s, d = dx.shape
    nbr, c, _ = w_bo.shape
    tm = _tile(s, tm)

    def body(dx_ref, ya_ref, yb_ref, yc_ref, g0_ref, g1_ref, g2_ref, bg_ref, wbo_ref, wo_ref,
             dgate_ref, dya_ref, dyb_ref, dyc_ref, dd0_ref, dd1_ref, dd2_ref, dxb_ref, dbg_ref):
        @pl.when(pl.program_id(0) == 0)
        def _():
            dbg_ref[...] = jnp.zeros_like(dbg_ref)

        dxb = _mx(dx_ref[...])
        dxb_ref[...] = dxb
        dmerged = _dot_nt(dxb, wo_ref[...])
        branches = ((ya_ref, g0_ref, dya_ref, dd0_ref), (yb_ref, g1_ref, dyb_ref, dd1_ref), (yc_ref, g2_ref, dyc_ref, dd2_ref))
        for i, (y_ref, g_ref, dy_ref, dd_ref) in enumerate(branches):
            cols = slice(i * d, (i + 1) * d)
            gate = _sigmoid(g_ref[...] + bg_ref[:, cols])
            yd = _dot(y_ref[...], wbo_ref[i])
            dyd = _mx(dmerged * gate)
            dd_ref[...] = dyd
            dy_ref[...] = _dot_nt(dyd, wbo_ref[i])
            dpre = dmerged * yd * gate * (1.0 - gate)
            dgate_ref[:, cols] = _mx(dpre)
            dbg_ref[:, cols] += _row_fold(dpre)

    yblk = pl.BlockSpec((tm, c), lambda i: (i, 0))
    gblk = lambda k: pl.BlockSpec((tm, d), lambda i: (i, gate_col0 + k))
    dblk = pl.BlockSpec((tm, d), lambda i: (i, 0))
    return pl.pallas_call(
        body,
        name=name,
        grid=(s // tm,),
        in_specs=[
            dblk, yblk, yblk, yblk, gblk(0), gblk(1), gblk(2),
            pl.BlockSpec((1, nbr * d), lambda i: (0, 0)),
            pl.BlockSpec((nbr, c, d), lambda i: (0, 0, 0)),
            pl.BlockSpec((d, d), lambda i: (0, 0)),
        ],
        out_specs=[pl.BlockSpec((tm, nbr * d), lambda i: (i, 0)), yblk, yblk, yblk, dblk, dblk, dblk, dblk,
                   pl.BlockSpec((8, nbr * d), lambda i: (0, 0))],
        out_shape=[jax.ShapeDtypeStruct((s, nbr * d), MXU_DTYPE)] + [jax.ShapeDtypeStruct((s, c), F32)] * 3
        + [jax.ShapeDtypeStruct((s, d), MXU_DTYPE)] * 4 + [jax.ShapeDtypeStruct((8, nbr * d), F32)],
        compiler_params=_params(),
    )(dx, *ys, p, p, p, b_gate, w_bo, w_o)


def _conv_bwd(p, dya, conv_w, conv_b, *, name, tm=512):
    s = p.shape[0]
    c = conv_w.shape[1]
    tm = _tile(s, tm)
    hb = tm // 8
    last = s // tm - 1

    def body(bg_ref, cg_ref, xa_ref, cgp_ref, xap_ref, dy_ref, dyn_ref, bgn_ref, w_ref, b_ref,
             dp_ref, dw_ref):
        i = pl.program_id(0)

        @pl.when(i == 0)
        def _():
            dw_ref[...] = jnp.zeros_like(dw_ref)

        cg = cg_ref[...]
        xa = xa_ref[...]
        u = cg * xa
        prev = jnp.where(i == 0, 0.0, cgp_ref[...] * xap_ref[...])
        um1, um2 = _conv_taps(u, prev)
        w = w_ref[...]
        y = b_ref[...] + w[0:1, :] * um2 + w[1:2, :] * um1 + w[2:3, :] * u
        dya = dy_ref[...]
        dp_ref[:, 0:c] = _mx(dya * y)
        dyv = dya * bg_ref[...]
        nxt = jnp.where(i == last, 0.0, dyn_ref[...] * bgn_ref[...])
        row = lax.broadcasted_iota(jnp.int32, dyv.shape, 0)
        dp1 = jnp.where(row == tm - 1, nxt[0:1, :], pltpu.roll(dyv, tm - 1, 0))
        dp2 = pltpu.roll(dyv, tm - 2, 0)
        dp2 = jnp.where(row == tm - 2, nxt[0:1, :], jnp.where(row == tm - 1, nxt[1:2, :], dp2))
        du = w[2:3, :] * dyv + w[1:2, :] * dp1 + w[0:1, :] * dp2
        dp_ref[:, c:2 * c] = _mx(du * xa)
        dp_ref[:, 2 * c:3 * c] = _mx(du * cg)
        dw_ref[0] += _row_fold(dyv * um2)
        dw_ref[1] += _row_fold(dyv * um1)
        dw_ref[2] += _row_fold(dyv * u)
        dw_ref[3] += _row_fold(dyv)

    blk = lambda col: pl.BlockSpec((tm, c), lambda i: (i, col))
    halo = lambda col: pl.BlockSpec((8, c), lambda i: (jnp.maximum(i * hb - 1, 0), col))
    nhalo = lambda col: pl.BlockSpec((8, c), lambda i: (jnp.minimum((i + 1) * hb, s // 8 - 1), col))
    return pl.pallas_call(
        body,
        name=name,
        grid=(s // tm,),
        in_specs=[blk(0), blk(1), blk(2), halo(1), halo(2), blk(0), nhalo(0), nhalo(0),
                  pl.BlockSpec((CONV_K, c), lambda i: (0, 0)), pl.BlockSpec((1, c), lambda i: (0, 0))],
        out_specs=[pl.BlockSpec((tm, 3 * c), lambda i: (i, 0)), pl.BlockSpec((4, 8, c), lambda i: (0, 0, 0))],
        out_shape=[jax.ShapeDtypeStruct((s, 3 * c), MXU_DTYPE), jax.ShapeDtypeStruct((4, 8, c), F32)],
        compiler_params=_params(),
    )(p, p, p, p, p, dya, dya, p, conv_w, conv_b)


def _sgu_bwd(p, dyb, ln_g, ln_b, wtril, wtril_t, bias_full, *, col0, name, tm=512):
    s = p.shape[0]
    c = ln_g.shape[1]
    groups = c // CHUNK
    tm = _tile(s, tm)

    def body(u_ref, v_ref, dy_ref, g_ref, b_ref, w_ref, wt_ref, bias_ref, du_ref, dv_ref, dln_ref, dw_ref, dbias_ref,
             dvn_ref):
        @pl.when(pl.program_id(0) == 0)
        def _():
            dln_ref[...] = jnp.zeros_like(dln_ref)
            dw_ref[...] = jnp.zeros_like(dw_ref)
            dbias_ref[...] = jnp.zeros_like(dbias_ref)

        sv = v_ref[...]
        xhat, r = _layernorm_stats(_gelu(sv))
        vb = _mx(xhat * g_ref[...] + b_ref[...])
        for n in range(tm // CHUNK):
            rows = slice(n * CHUNK, (n + 1) * CHUNK)
            for gi in range(groups):
                cols = slice(gi * CHUNK, (gi + 1) * CHUNK)
                su = u_ref[rows, cols]
                dy = dy_ref[rows, cols]
                vblk = vb[rows, cols]
                mixed = _dot(w_ref[gi], vblk) + bias_ref[:, cols]
                du_ref[rows, cols] = _mx(dy * mixed * _gelu_grad(su))
                dmixed = dy * _gelu(su)
                dmb = _mx(dmixed)
                dvn_ref[rows, cols] = _dot(wt_ref[gi], dmb)
                dw_ref[gi] += _dot_nt(dmb, vblk)
                dbias_ref[:, cols] += dmixed
        dvn = dvn_ref[...]
        dln_ref[0] += _row_fold(dvn * xhat)
        dln_ref[1] += _row_fold(dvn)
        dxh = dvn * g_ref[...]
        dgv = r * (dxh - jnp.mean(dxh, axis=-1, keepdims=True) - xhat * jnp.mean(dxh * xhat, axis=-1, keepdims=True))
        dv_ref[...] = _mx(dgv * _gelu_grad(sv))

    blk = lambda col: pl.BlockSpec((tm, c), lambda i: (i, col))
    vec = pl.BlockSpec((1, c), lambda i: (0, 0))
    wspec = pl.BlockSpec((groups, CHUNK, CHUNK), lambda i: (0, 0, 0))
    return pl.pallas_call(
        body,
        name=name,
        grid=(s // tm,),
        in_specs=[blk(col0), blk(col0 + 1), blk(0), vec, vec, wspec, wspec, pl.BlockSpec((CHUNK, c), lambda i: (0, 0))],
        out_specs=[blk(0), blk(0), pl.BlockSpec((2, 8, c), lambda i: (0, 0, 0)), wspec,
                   pl.BlockSpec((CHUNK, c), lambda i: (0, 0))],
        out_shape=[jax.ShapeDtypeStruct((s, c), MXU_DTYPE), jax.ShapeDtypeStruct((s, c), MXU_DTYPE),
                   jax.ShapeDtypeStruct((2, 8, c), F32), jax.ShapeDtypeStruct((groups, CHUNK, CHUNK), F32),
                   jax.ShapeDtypeStruct((CHUNK, c), F32)],
        scratch_shapes=[pltpu.VMEM((tm, c), F32)],
        compiler_params=_params(),
    )(p, p, dyb, ln_g, ln_b, wtril, wtril_t, bias_full)


def _att_bwd(k_hm, kt_hm, v_hm, vt_hm, q_hm, qt_hm, do_hm, dot_hm, later, earlier, *, name):
    h, nb, t, d = k_hm.shape
    s = nb * t
    scale = 1.0 / math.sqrt(d)

    def body(k_ref, kt_ref, v_ref, vt_ref, q_ref, qt_ref, do_ref, dot_ref, later_ref, earlier_ref,
             dqt_ref, dk_ref, dv_ref, run_ref):
        i = pl.program_id(1)

        @pl.when(i == 0)
        def _():
            dk_ref[...] = jnp.zeros_like(dk_ref)
            dv_ref[...] = jnp.zeros_like(dv_ref)

        q_t = qt_ref[...]

        def cond(carry):
            j, _, rmax = carry
            return jnp.logical_and(j >= 0, rmax > EXP_UNDERFLOW)

        def down(carry):
            j, run, _ = carry
            run_ref[j] = run
            _, _, _, l1m, _ = _att_block(k_ref[j], q_t, scale, j * t, i * t)
            run = run + jnp.sum(l1m, axis=0, keepdims=True)
            return j - 1, run, jnp.max(run)

        jlow, _, _ = lax.while_loop(cond, down, (i, jnp.zeros((1, t), F32), jnp.float32(0.0)))

        def up(j, carry):
            gsum, dqt = carry
            z, e, lb, l1m, mask = _att_block(k_ref[j], q_t, scale, j * t, i * t)
            after = _dot_split(later_ref[...], l1m) + run_ref[j]
            a = jnp.where(mask, jnp.exp(lb + after), 0.0)
            g = _dot(v_ref[j], dot_ref[...]) * a
            gbefore = _dot_split(earlier_ref[...], g) + gsum
            inv = 1.0 / (1.0 + e)
            pos = z >= 0.0
            beta = jnp.where(pos, inv, e * inv)
            omb = jnp.where(pos, e * inv, inv)
            dz = _mx(jnp.where(mask, g * omb - gbefore * beta, 0.0) * scale)
            dqt = dqt + _dot(kt_ref[j], dz)
            dk_ref[j] += _dot(dz, q_ref[...])
            dv_ref[j] += _dot(_mx(a), do_ref[...])
            return gsum + jnp.sum(g, axis=0, keepdims=True), dqt

        _, dqt = lax.fori_loop(jlow + 1, i + 1, up, (jnp.zeros((1, t), F32), jnp.zeros((d, t), F32)))
        dqt_ref[...] = dqt

    whole = lambda a, b: pl.BlockSpec((None, nb, a, b), lambda hh, i: (hh, 0, 0, 0))
    rows = pl.BlockSpec((None, t, d), lambda hh, i: (hh, i, 0))
    cols = pl.BlockSpec((None, d, t), lambda hh, i: (hh, 0, i))
    tri = pl.BlockSpec((t, t), lambda hh, i: (0, 0))
    return pl.pallas_call(
        body,
        name=name,
        grid=(h, nb),
        in_specs=[whole(t, d), whole(d, t), whole(t, d), whole(d, t), rows, cols, rows, cols, tri, tri],
        out_specs=[cols, whole(t, d), whole(t, d)],
        out_shape=[jax.ShapeDtypeStruct((h, d, s), F32), jax.ShapeDtypeStruct((h, nb, t, d), F32),
                   jax.ShapeDtypeStruct((h, nb, t, d), F32)],
        scratch_shapes=[pltpu.VMEM((nb, 1, t), F32)],
        compiler_params=_params(),
    )(k_hm, kt_hm, v_hm, vt_hm, q_hm, qt_hm, do_hm, dot_hm, later, earlier)


def _qkv_bwd(p, dqn, dkn, dv, qg, kg, headavg, *, col0, name, tm=512):
    s = p.shape[0]
    c = qg.shape[1]
    tm = _tile(s, tm)

    def body(q_ref, k_ref, dqn_ref, dkn_ref, dv_ref, qg_ref, kg_ref, avg_ref, dq_ref, dk_ref, dvb_ref, dg_ref):
        @pl.when(pl.program_id(0) == 0)
        def _():
            dg_ref[...] = jnp.zeros_like(dg_ref)

        for n, (src, dn_ref, g_ref, dst) in enumerate(((q_ref, dqn_ref, qg_ref, dq_ref), (k_ref, dkn_ref, kg_ref, dk_ref))):
            xf = src[...]
            r = lax.rsqrt(_head_mean(xf * xf, avg_ref[...]) + EPS)
            y = xf * r
            dn = dn_ref[...]
            dy = dn * g_ref[...]
            dst[...] = _mx(r * (dy - y * _head_mean(dy * y, avg_ref[...])))
            dg_ref[n] += _row_fold(dn * y)
        dvb_ref[...] = _mx(dv_ref[...])

    blk = lambda col: pl.BlockSpec((tm, c), lambda i: (i, col))
    vec = pl.BlockSpec((1, c), lambda i: (0, 0))
    out_shape = jax.ShapeDtypeStruct((s, c), MXU_DTYPE)
    return pl.pallas_call(
        body,
        name=name,
        grid=(s // tm,),
        in_specs=[blk(col0), blk(col0 + 1), blk(0), blk(0), blk(0), vec, vec, pl.BlockSpec((c, c), lambda i: (0, 0))],
        out_specs=[blk(0), blk(0), blk(0), pl.BlockSpec((2, 8, c), lambda i: (0, 0, 0))],
        out_shape=[out_shape, out_shape, out_shape, jax.ShapeDtypeStruct((2, 8, c), F32)],
        compiler_params=_params(),
    )(p, p, dqn, dkn, dv, qg, kg, headavg)


def _place():
    x, y, c = lax.axis_index("x"), lax.axis_index("y"), lax.axis_index("c")
    chips = [(1 - x, y), (x, 1 - y), (1 - x, 1 - y)]
    return x, y, c, chips


def _shard_of(ref, axis, chip, width):
    idx = [slice(None)] * len(ref.shape)
    idx[axis] = pl.ds(chip * width, width)
    return ref.at[tuple(idx)]


def _gather_weights(shards, axes, *, name):
    n = len(shards)
    widths = [sh.shape[1 + ax] for sh, ax in zip(shards, axes)]
    out_shapes = []
    for sh, ax in zip(shards, axes):
        full = list(sh.shape)
        full[1 + ax] *= N_CHIPS
        out_shapes.append(jax.ShapeDtypeStruct(tuple(full), sh.dtype))

    def body(*refs):
        ins, outs = refs[:n], refs[n:2 * n]
        local_sem, send_sem, recv_sem, fsend_sem, frecv_sem = refs[2 * n:]
        x, y, c, chips = _place()
        me = 2 * x + y
        sibling = (x, y, 1 - c)

        def slot(a, layer, chip):
            return _shard_of(outs[a].at[layer], axes[a], chip, widths[a])

        def send(a, k):
            return pltpu.make_async_remote_copy(
                src_ref=ins[a].at[c], dst_ref=slot(a, c, me), send_sem=send_sem.at[a, k], recv_sem=recv_sem.at[a, k],
                device_id=(*chips[k], c), device_id_type=MESH)

        def landed(a, k):
            cx, cy = chips[k]
            return pltpu.make_async_remote_copy(
                src_ref=ins[a].at[c], dst_ref=slot(a, c, 2 * cx + cy), send_sem=send_sem.at[a, k],
                recv_sem=recv_sem.at[a, k], device_id=(*chips[k], c), device_id_type=MESH)

        def forward(a, k, layer):
            cx, cy = chips[k]
            blk = slot(a, layer, 2 * cx + cy)
            return pltpu.make_async_remote_copy(
                src_ref=blk, dst_ref=blk, send_sem=fsend_sem.at[a, k], recv_sem=frecv_sem.at[a, k],
                device_id=sibling, device_id_type=MESH)

        own = []
        for a in range(n):
            for layer in range(2):
                cp = pltpu.make_async_copy(ins[a].at[layer], slot(a, layer, me), local_sem.at[a, layer])
                cp.start()
                own.append(cp)
        for a in range(n):
            for k in range(3):
                send(a, k).start()
        for a in range(n):
            for k in range(3):
                landed(a, k).wait_recv()
                forward(a, k, c).start()
        for a in range(n):
            for k in range(3):
                forward(a, k, 1 - c).wait_recv()
        for a in range(n):
            for k in range(3):
                send(a, k).wait_send()
                forward(a, k, c).wait_send()
        for cp in own:
            cp.wait()

    return pl.pallas_call(
        body,
        name=name,
        in_specs=[ANY] * n,
        out_specs=[ANY] * n,
        out_shape=out_shapes,
        scratch_shapes=[pltpu.SemaphoreType.DMA((n, 2))] + [pltpu.SemaphoreType.DMA((n, 3))] * 4,
            )(*shards)


def _pair_exchange(grads, *, name):
    n = len(grads)

    def body(*refs):
        ins, outs = refs[:n], refs[n:2 * n]
        send_sem, recv_sem = refs[2 * n:]
        x, y, c, _ = _place()
        copies = [
            pltpu.make_async_remote_copy(
                src_ref=ins[a].at[1 - c], dst_ref=outs[a], send_sem=send_sem.at[a], recv_sem=recv_sem.at[a],
                device_id=(x, y, 1 - c), device_id_type=MESH)
            for a in range(n)
        ]
        for cp in copies:
            cp.start()
        for cp in copies:
            cp.wait()

    return pl.pallas_call(
        body,
        name=name,
        in_specs=[ANY] * n,
        out_specs=[ANY] * n,
        out_shape=[jax.ShapeDtypeStruct(g.shape[1:], g.dtype) for g in grads],
        scratch_shapes=[pltpu.SemaphoreType.DMA((n,)), pltpu.SemaphoreType.DMA((n,))],
            )(*grads)


def _chip_scatter(sums, axes, *, name):
    n = len(sums)
    widths = [sm.shape[ax] // N_CHIPS for sm, ax in zip(sums, axes)]
    out_shapes = []
    for sm, ax, w in zip(sums, axes, widths):
        shp = list(sm.shape)
        shp[ax] = w
        out_shapes.append(jax.ShapeDtypeStruct((3, *shp), sm.dtype))

    def body(*refs):
        ins, outs = refs[:n], refs[n:2 * n]
        send_sem, recv_sem = refs[2 * n:]
        x, y, c, chips = _place()
        copies = []
        for a in range(n):
            for k in range(3):
                cx, cy = chips[k]
                copies.append(pltpu.make_async_remote_copy(
                    src_ref=_shard_of(ins[a], axes[a], 2 * cx + cy, widths[a]), dst_ref=outs[a].at[k],
                    send_sem=send_sem.at[a, k], recv_sem=recv_sem.at[a, k], device_id=(cx, cy, c), device_id_type=MESH))
        for cp in copies:
            cp.start()
        for cp in copies:
            cp.wait()

    return pl.pallas_call(
        body,
        name=name,
        in_specs=[ANY] * n,
        out_specs=[ANY] * n,
        out_shape=out_shapes,
        scratch_shapes=[pltpu.SemaphoreType.DMA((n, 3)), pltpu.SemaphoreType.DMA((n, 3))],
            )(*sums)


def _pair_share(finals, *, name):
    n = len(finals)

    def body(*refs):
        ins, outs = refs[:n], refs[n:2 * n]
        local_sem, send_sem, recv_sem = refs[2 * n:]
        x, y, c, _ = _place()
        own, copies = [], []
        for a in range(n):
            cp = pltpu.make_async_copy(ins[a], outs[a].at[c], local_sem.at[a])
            cp.start()
            own.append(cp)
            copies.append(pltpu.make_async_remote_copy(
                src_ref=ins[a], dst_ref=outs[a].at[c], send_sem=send_sem.at[a], recv_sem=recv_sem.at[a],
                device_id=(x, y, 1 - c), device_id_type=MESH))
        for cp in copies:
            cp.start()
        for a in range(n):
            pltpu.make_async_remote_copy(
                src_ref=ins[a], dst_ref=outs[a].at[1 - c], send_sem=send_sem.at[a], recv_sem=recv_sem.at[a],
                device_id=(x, y, 1 - c), device_id_type=MESH).wait()
        for cp in own:
            cp.wait()

    return pl.pallas_call(
        body,
        name=name,
        in_specs=[ANY] * n,
        out_specs=[ANY] * n,
        out_shape=[jax.ShapeDtypeStruct((2, *f.shape), f.dtype) for f in finals],
        scratch_shapes=[pltpu.SemaphoreType.DMA((n,))] * 3,
            )(*finals)


def _all_reduce_small(packed, *, name):
    r = packed.shape[0]

    def body(in_ref, out_ref, slots, send_sem, recv_sem):
        x, y, c, _ = _place()
        me = 4 * x + 2 * y + c
        slots[me] = in_ref[...]
        copies = []
        for k in range(1, N_DEV):
            peer = (x ^ (k >> 2), y ^ ((k >> 1) & 1), c ^ (k & 1))
            copies.append(pltpu.make_async_remote_copy(
                src_ref=in_ref, dst_ref=slots.at[me], send_sem=send_sem.at[k - 1], recv_sem=recv_sem.at[k - 1],
                device_id=peer, device_id_type=MESH))
        for cp in copies:
            cp.start()
        for k in range(1, N_DEV):
            pltpu.make_async_remote_copy(
                src_ref=in_ref, dst_ref=slots.at[me ^ k], send_sem=send_sem.at[k - 1], recv_sem=recv_sem.at[k - 1],
                device_id=(x, y, c), device_id_type=MESH).wait()
        total = slots[0]
        for dev in range(1, N_DEV):
            total = total + slots[dev]
        out_ref[...] = total

    return pl.pallas_call(
        body,
        name=name,
        in_specs=[pl.BlockSpec(memory_space=pltpu.VMEM)],
        out_specs=pl.BlockSpec(memory_space=pltpu.VMEM),
        out_shape=jax.ShapeDtypeStruct((r, 128), F32),
        scratch_shapes=[pltpu.VMEM((N_DEV, r, 128), F32), pltpu.SemaphoreType.DMA((N_DEV - 1,)),
                        pltpu.SemaphoreType.DMA((N_DEV - 1,))],
        compiler_params=_params(),
    )(packed)


def _as_rows(shape):
    cols = shape[-1]
    return math.prod(shape[:-1]), cols


ELEMENTWISE_VMEM = 24 * 1024 * 1024


def _row_tile(rows, cols, n_arrays):
    cap = ELEMENTWISE_VMEM // (n_arrays * 2 * 4 * cols)
    best = None
    for t in range(8, min(rows, cap) + 1, 8):
        if rows % t == 0:
            best = t
    assert best is not None, (rows, cols)
    return best


def _pair_sum(g, other, c_idx, *, name):
    rows, cols = _as_rows(other.shape)
    tr = _row_tile(rows, cols, 3)
    g2 = g.reshape(2, rows, cols)

    def body(c_ref, g_ref, o_ref, out_ref):
        out_ref[...] = g_ref[...] + o_ref[...]

    out = pl.pallas_call(
        body,
        name=name,
        grid_spec=pltpu.PrefetchScalarGridSpec(
            num_scalar_prefetch=1,
            grid=(rows // tr,),
            in_specs=[pl.BlockSpec((None, tr, cols), lambda i, c_ref: (c_ref[0], i, 0)),
                      pl.BlockSpec((tr, cols), lambda i, c_ref: (i, 0))],
            out_specs=pl.BlockSpec((tr, cols), lambda i, c_ref: (i, 0)),
        ),
        out_shape=jax.ShapeDtypeStruct((rows, cols), F32),
        compiler_params=_params(),
    )(c_idx, g2, other.reshape(rows, cols))
    return out.reshape(other.shape)


def _chip_sum(mine, got, axis, chip_idx, *, name):
    shard_shape = got.shape[1:]
    rows, cols = _as_rows(shard_shape)
    tr = _row_tile(rows, cols, 5)
    if axis == len(mine.shape) - 1:
        m2 = mine.reshape(rows, cols * N_CHIPS)
        mine_spec = pl.BlockSpec((tr, cols), lambda i, j_ref: (i, j_ref[0]))
    else:
        assert axis == 0
        m2 = mine.reshape(N_CHIPS, rows, cols)
        mine_spec = pl.BlockSpec((None, tr, cols), lambda i, j_ref: (j_ref[0], i, 0))

    def body(j_ref, m_ref, got_ref, out_ref):
        out_ref[...] = ((m_ref[...] + got_ref[0]) + got_ref[1]) + got_ref[2]

    out = pl.pallas_call(
        body,
        name=name,
        grid_spec=pltpu.PrefetchScalarGridSpec(
            num_scalar_prefetch=1,
            grid=(rows // tr,),
            in_specs=[mine_spec, pl.BlockSpec((3, tr, cols), lambda i, j_ref: (0, i, 0))],
            out_specs=pl.BlockSpec((tr, cols), lambda i, j_ref: (i, 0)),
        ),
        out_shape=jax.ShapeDtypeStruct((rows, cols), F32),
        compiler_params=_params(),
    )(chip_idx, m2, got.reshape(3, rows, cols))
    return out.reshape(shard_shape)


def _adamw(w, g, m, v, *, name):
    shape = w.shape
    rows, cols = _as_rows(shape)
    tr = _row_tile(rows, cols, 7)
    c1 = 1.0 / (1.0 - ADAM_B1 ** ADAM_STEP)
    c2 = 1.0 / (1.0 - ADAM_B2 ** ADAM_STEP)

    def body(w_ref, g_ref, m_ref, v_ref, d_ref, nm_ref, nv_ref):
        gg = g_ref[...]
        nm = ADAM_B1 * m_ref[...] + (1.0 - ADAM_B1) * gg
        nv = ADAM_B2 * v_ref[...] + (1.0 - ADAM_B2) * (gg * gg)
        nm_ref[...] = nm
        nv_ref[...] = nv
        d_ref[...] = -ADAM_LR * ((nm * c1) / (jnp.sqrt(nv * c2) + ADAM_EPS) + ADAM_WD * w_ref[...])

    blk = pl.BlockSpec((tr, cols), lambda i: (i, 0))
    flat = lambda a: a.reshape(rows, cols)
    outs = pl.pallas_call(
        body,
        name=name,
        grid=(rows // tr,),
        in_specs=[blk] * 4,
        out_specs=[blk] * 3,
        out_shape=[jax.ShapeDtypeStruct((rows, cols), F32)] * 3,
        compiler_params=_params(),
    )(flat(w), flat(g), flat(m), flat(v))
    return tuple(o.reshape(shape) for o in outs)


def _head_major(a, t):
    s = a.shape[0]
    a4 = a.reshape(s // t, t, HEADS, HEAD_DIM)
    rows = a4.transpose(2, 0, 1, 3)
    cols = a4.transpose(2, 0, 3, 1)
    return (rows.reshape(HEADS, s, HEAD_DIM), cols.transpose(0, 2, 1, 3).reshape(HEADS, HEAD_DIM, s), rows, cols)


def _from_head_major(a):
    h, s, d = a.shape
    return a.transpose(1, 0, 2).reshape(s, h * d)


def _pack(parts):
    flat = []
    for a in parts:
        v = a.reshape(-1)
        flat.append(jnp.pad(v, (0, (-v.shape[0]) % 1024)))
    return jnp.concatenate(flat).reshape(-1, 128)


def _unpack(packed, shapes):
    flat = packed.reshape(-1)
    out, off = [], 0
    for shp in shapes:
        size = math.prod(shp)
        out.append(flat[off:off + size].reshape(shp))
        off += size + (-size) % 1024
    return out


BIG = ("w_in", "w_branch_out", "w_o", "w_gate_up", "w_down")
BIG_AXIS = {"w_in": 1, "w_branch_out": 2, "w_o": 0, "w_gate_up": 1, "w_down": 0}
SMALL = ("mix_norm_g", "b_gate", "conv_w", "conv_b", "sgu_ln_g", "sgu_ln_b", "sgu_w", "sgu_b", "q_norm_g", "k_norm_g",
         "ffn_norm_g")
ORDER = ("mix_norm_g", "w_in", "b_gate", "conv_w", "conv_b", "sgu_ln_g", "sgu_ln_b", "sgu_w", "sgu_b", "q_norm_g",
         "k_norm_g", "w_branch_out", "w_o", "ffn_norm_g", "w_gate_up", "w_down")


def _layer_forward(x, w, l):
    t = ATT_BLOCK
    c = w["conv_b"].shape[1]
    n_in = w["w_in"].shape[2]
    gate_col0 = (n_in - 3 * x.shape[1]) // x.shape[1]
    tag = f"l{l}"
    p, h = _norm_matmul(x, w["mix_norm_g"][l][None], w["w_in"][l], name=f"in_proj_{tag}", tn=n_in // 8)
    ya = _conv_fwd(p, w["conv_w"][l], w["conv_b"][l][None], name=f"conv_{tag}")
    yb = _sgu_fwd(p, w["sgu_ln_g"][l][None], w["sgu_ln_b"][l][None], w["wtril"][l], w["bias_full"][l], col0=3,
                  name=f"sgu_{tag}")
    qn, kn, vb = _qkv_prep(p, w["qg"][l], w["kg"][l], w["headavg"], col0=5, name=f"qkv_{tag}")
    q_r, q_c, _, _ = _head_major(qn, t)
    _, _, k_rb, k_cb = _head_major(kn, t)
    _, _, v_rb, v_cb = _head_major(vb, t)
    out_t = _att_fwd(k_rb, q_c, v_cb, w["later"], name=f"att_{tag}")
    yc = _from_head_major(out_t.transpose(0, 2, 1))
    x1, merged = _merge_fwd((ya, yb, yc), p, w["b_gate"][l][None], w["w_branch_out"][l], w["w_o"][l], x,
                            gate_col0=gate_col0, name=f"merge_{tag}")
    gu, h2 = _norm_matmul(x1, w["ffn_norm_g"][l][None], w["w_gate_up"][l], name=f"gate_up_{tag}",
                          tn=w["w_gate_up"].shape[2] // 4)
    x2 = _ffn_down(gu, w["w_down"][l], x1, name=f"down_{tag}")
    saved = dict(x=x, p=p, h=h, ya=ya, yb=yb, yc=yc, merged=merged, x1=x1, gu=gu, h2=h2,
                 att=(k_rb, k_cb, v_rb, v_cb, q_r, q_c), gate_col0=gate_col0)
    return x2, saved


def _layer_backward(dx2, w, sv, l):
    t = ATT_BLOCK
    tag = f"l{l}"
    d = dx2.shape[1]
    c = w["conv_b"].shape[1]
    n_in = w["w_in"].shape[2]
    n_ff = w["w_gate_up"].shape[2]
    g = {}
    dgu, act, dx2b = _ffn_bwd(dx2, sv["gu"], w["w_down"][l], name=f"down_bwd_{tag}")
    g["w_down"] = _matmul_tn(act, dx2b, name=f"dw_down_{tag}", t1=act.shape[1] // 2)
    dx1, dg2 = _matmul_nt_normbwd(dgu, w["w_gate_up"][l], sv["x1"], w["ffn_norm_g"][l][None], dx2,
                                  name=f"gate_up_bwd_{tag}", tk=n_ff // 4)
    g["ffn_norm_g"] = jnp.sum(dg2, axis=0)
    g["w_gate_up"] = _matmul_tn(sv["h2"], dgu, name=f"dw_gate_up_{tag}", tn=n_ff // 4)
    ys = (sv["ya"], sv["yb"], sv["yc"])
    (dgates, dya, dyb, dyc, dd0, dd1, dd2, dx1b, dbg) = _merge_bwd(
        dx1, ys, sv["p"], w["b_gate"][l][None], w["w_branch_out"][l], w["w_o"][l], gate_col0=sv["gate_col0"],
        name=f"merge_bwd_{tag}")
    g["b_gate"] = jnp.sum(dbg, axis=0)
    g["w_o"] = _matmul_tn(sv["merged"], dx1b, name=f"dw_o_{tag}")
    g["w_branch_out"] = jnp.stack([
        _matmul_tn(y, dd, name=f"dw_bo{i}_{tag}") for i, (y, dd) in enumerate(zip(ys, (dd0, dd1, dd2)))])
    dconv, dwc = _conv_bwd(sv["p"], dya, w["conv_w"][l], w["conv_b"][l][None], name=f"conv_bwd_{tag}")
    dwc = jnp.sum(dwc, axis=1)
    g["conv_w"] = dwc[0:CONV_K]
    g["conv_b"] = dwc[CONV_K]
    dsu, dsv, dln, dws, dbias = _sgu_bwd(sv["p"], dyb, w["sgu_ln_g"][l][None], w["sgu_ln_b"][l][None], w["wtril"][l],
                                         w["wtril_t"][l], w["bias_full"][l], col0=3, name=f"sgu_bwd_{tag}")
    dln = jnp.sum(dln, axis=1)
    g["sgu_ln_g"], g["sgu_ln_b"] = dln[0], dln[1]
    g["sgu_w"] = jnp.where(w["tril"], dws, 0.0)
    g["sgu_b"] = jnp.sum(dbias.reshape(CHUNK, c // CHUNK, CHUNK), axis=2).T
    k_rb, k_cb, v_rb, v_cb, q_r, q_c = sv["att"]
    do_r, do_c, _, _ = _head_major(_mx(dyc), t)
    dqt, dk, dv = _att_bwd(k_rb, k_cb, v_rb, v_cb, q_r, q_c, do_r, do_c, w["later"], w["earlier"], name=f"att_bwd_{tag}")
    s = dx2.shape[0]
    dqn = _from_head_major(dqt.transpose(0, 2, 1))
    dkn = _from_head_major(dk.reshape(HEADS, s, HEAD_DIM))
    dvv = _from_head_major(dv.reshape(HEADS, s, HEAD_DIM))
    dq, dk_, dvb, dqkg = _qkv_bwd(sv["p"], dqn, dkn, dvv, w["qg"][l], w["kg"][l], w["headavg"], col0=5,
                                  name=f"qkv_bwd_{tag}")
    dqkg = jnp.sum(dqkg.reshape(2, 8 * HEADS, HEAD_DIM), axis=1)
    g["q_norm_g"], g["k_norm_g"] = dqkg[0], dqkg[1]
    dp = jnp.concatenate([dconv, dsu, dsv, dq, dk_, dvb, dgates], axis=1)
    dx0, dg1 = _matmul_nt_normbwd(dp, w["w_in"][l], sv["x"], w["mix_norm_g"][l][None], dx1, name=f"in_proj_bwd_{tag}",
                                  tk=n_in // 4)
    g["mix_norm_g"] = jnp.sum(dg1, axis=0)
    g["w_in"] = _matmul_tn(sv["h"], dp, name=f"dw_in_{tag}", tn=n_in // 4)
    return dx0, g


def kernel(x, mix_norm_g, w_in, b_gate, conv_w, conv_b, sgu_ln_g, sgu_ln_b, sgu_w, sgu_b, q_norm_g, k_norm_g, w_branch_out, w_o, ffn_norm_g, w_gate_up, w_down, loss_target, m_mix_norm_g, m_w_in, m_b_gate, m_conv_w, m_conv_b, m_sgu_ln_g, m_sgu_ln_b, m_sgu_w, m_sgu_b, m_q_norm_g, m_k_norm_g, m_w_branch_out, m_w_o, m_ffn_norm_g, m_w_gate_up, m_w_down, v_mix_norm_g, v_w_in, v_b_gate, v_conv_w, v_conv_b, v_sgu_ln_g, v_sgu_ln_b, v_sgu_w, v_sgu_b, v_q_norm_g, v_k_norm_g, v_w_branch_out, v_w_o, v_ffn_norm_g, v_w_gate_up, v_w_down):
    given = dict(locals())
    params = {n: given[n] for n in ORDER}
    moms = {n: (given["m_" + n], given["v_" + n]) for n in ORDER}
    layers = mix_norm_g.shape[0]
    assert layers == 2, "the exchanges split the work of a chip's two cores by layer"
    xs = x[0]
    target = loss_target[0]
    chip = 2 * lax.axis_index("x") + lax.axis_index("y")
    core = lax.axis_index("c")

    gathered = _gather_weights([_mx(params[n]) for n in BIG] + [conv_w], [BIG_AXIS[n] for n in BIG] + [1],
                               name="gather_weights")
    w = dict(zip(BIG + ("conv_w",), gathered))
    for n in ("mix_norm_g", "b_gate", "conv_b", "sgu_ln_g", "sgu_ln_b", "ffn_norm_g"):
        w[n] = params[n]
    groups = sgu_w.shape[1]
    tril = jnp.tril(jnp.ones((CHUNK, CHUNK), dtype=bool))
    w["tril"] = tril
    w["wtril"] = _mx(jnp.where(tril, sgu_w, 0.0))
    w["wtril_t"] = w["wtril"].transpose(0, 1, 3, 2)
    w["bias_full"] = jnp.repeat(sgu_b.transpose(0, 2, 1), CHUNK, axis=2)
    w["qg"] = jnp.tile(q_norm_g, (1, HEADS))[:, None, :]
    w["kg"] = jnp.tile(k_norm_g, (1, HEADS))[:, None, :]
    lane = jnp.arange(HEADS * HEAD_DIM) // HEAD_DIM
    w["headavg"] = _mx(jnp.where(lane[:, None] == lane[None, :], 1.0 / HEAD_DIM, 0.0))
    pos = jnp.arange(ATT_BLOCK)
    w["later"] = _mx(jnp.where(pos[None, :] > pos[:, None], 1.0, 0.0))
    w["earlier"] = _mx(jnp.where(pos[None, :] < pos[:, None], 1.0, 0.0))

    saved = []
    act = xs
    for l in range(layers):
        act, sv = _layer_forward(act, w, l)
        saved.append(sv)
    loss_part, dact = _loss_head(act, target, name="loss_head")
    loss = lax.psum(jnp.sum(loss_part), ("x", "y", "c"))
    grads = [None] * layers
    for l in reversed(range(layers)):
        dact, grads[l] = _layer_backward(dact, w, saved[l], l)
    grad_x = dact[None]
    local = {n: jnp.stack([grads[l][n] for l in range(layers)]) for n in ORDER}

    c_idx = core.reshape(1).astype(jnp.int32)
    j_idx = chip.reshape(1).astype(jnp.int32)
    got = _pair_exchange([local[n] for n in BIG], name="grads_pair_exchange")
    pair = [_pair_sum(local[n], o, c_idx, name=f"pair_sum_{n}") for n, o in zip(BIG, got)]
    landed = _chip_scatter(pair, [BIG_AXIS[n] for n in BIG], name="grads_chip_scatter")
    final = [_chip_sum(pr, ld, BIG_AXIS[n], j_idx, name=f"chip_sum_{n}") for n, pr, ld in zip(BIG, pair, landed)]
    full = dict(zip(BIG, _pair_share(final, name="grads_pair_share")))

    small_shapes = [local[n].shape for n in SMALL]
    summed = _unpack(_all_reduce_small(_pack([local[n] for n in SMALL]), name="grads_all_reduce_small"), small_shapes)
    for n, gsum in zip(SMALL, summed):
        full[n] = gsum
    width = conv_w.shape[2]
    full["conv_w"] = lax.dynamic_slice_in_dim(full["conv_w"], chip * width, width, axis=2)

    out = {}
    for n in BIG:
        out[n] = _adamw(params[n], full[n], *moms[n], name=f"adamw_{n}")
    shapes = [params[n].shape for n in SMALL]
    packed = [_pack([src(n) for n in SMALL]) for src in
              (lambda n: params[n], lambda n: full[n], lambda n: moms[n][0], lambda n: moms[n][1])]
    for n, dl, nm, nv in zip(SMALL, *[_unpack(o, shapes) for o in _adamw(*packed, name="adamw_small")]):
        out[n] = (dl, nm, nv)
    return (loss, grad_x, *[full[n] for n in ORDER], *[out[n][0] for n in ORDER], *[out[n][1] for n in ORDER],
            *[out[n][2] for n in ORDER])
```

```python
import functools
import math

import jax
import jax.numpy as jnp
from jax import lax
from jax.experimental import pallas as pl
from jax.experimental.pallas import tpu as pltpu

F32 = jnp.float32
MXU_DTYPE = jnp.bfloat16

EPS = 1e-6
CONV_K = 3
CHUNK = 128
HEADS = 8
HEAD_DIM = 64
ATT_BLOCK = 128
EXP_UNDERFLOW = -104.0

ADAM_LR = 0.001
ADAM_B1 = 0.9
ADAM_B2 = 0.999
ADAM_EPS = 1e-08
ADAM_WD = 0.01
ADAM_STEP = 10

VMEM_LIMIT = 56 * 1024 * 1024
MESH = pl.DeviceIdType.MESH
N_CHIPS = 4
N_DEV = 8
ANY = pl.BlockSpec(memory_space=pl.ANY)


def _params(**kw):
    return pltpu.CompilerParams(vmem_limit_bytes=VMEM_LIMIT, **kw)


def _mx(v):
    return v.astype(MXU_DTYPE)


def _dot(a, b):
    return lax.dot_general(a, b, (((1,), (0,)), ((), ())), preferred_element_type=F32)


def _dot_nt(a, b):
    return lax.dot_general(a, b, (((1,), (1,)), ((), ())), preferred_element_type=F32)


def _dot_tn(a, b):
    return lax.dot_general(a, b, (((0,), (0,)), ((), ())), preferred_element_type=F32)


def _dot_split(const, v):
    hi = _mx(v)
    lo = _mx(v - hi.astype(F32))
    return _dot(const, hi) + _dot(const, lo)


def _dot_split_r(v, const):
    hi = _mx(v)
    lo = _mx(v - hi.astype(F32))
    return _dot(hi, const) + _dot(lo, const)


def _sigmoid(x):
    return 1.0 / (1.0 + jnp.exp(-x))


_INV_SQRT2 = 1.0 / math.sqrt(2.0)
_INV_SQRT2PI = 1.0 / math.sqrt(2.0 * math.pi)


def _gelu(x):
    return 0.5 * x * (1.0 + lax.erf(x * _INV_SQRT2))


def _gelu_grad(x):
    return 0.5 * (1.0 + lax.erf(x * _INV_SQRT2)) + x * jnp.exp(-0.5 * x * x) * _INV_SQRT2PI


def _row_fold(v):
    m, n = v.shape
    return jnp.sum(v.reshape(m // 8, 8, n), axis=0)


def _tile(m, pref):
    t = min(m, pref)
    while m % t:
        t //= 2
    return t


def _norm_matmul(x, g, w, *, name, tm=512, tn=None):
    s, d = x.shape
    n = w.shape[1]
    tm = _tile(s, tm)
    tn = tn or n
    assert n % tn == 0

    def body(x_ref, g_ref, w_ref, p_ref, h_ref):
        @pl.when(pl.program_id(1) == 0)
        def _():
            xf = x_ref[...]
            r = lax.rsqrt(jnp.mean(xf * xf, axis=-1, keepdims=True) + EPS)
            h_ref[...] = _mx(xf * r * g_ref[...])

        p_ref[...] = _dot(h_ref[...], w_ref[...])

    return pl.pallas_call(
        body,
        name=name,
        grid=(s // tm, n // tn),
        in_specs=[
            pl.BlockSpec((tm, d), lambda i, j: (i, 0)),
            pl.BlockSpec((1, d), lambda i, j: (0, 0)),
            pl.BlockSpec((d, tn), lambda i, j: (0, j)),
        ],
        out_specs=[
            pl.BlockSpec((tm, tn), lambda i, j: (i, j)),
            pl.BlockSpec((tm, d), lambda i, j: (i, 0)),
        ],
        out_shape=[jax.ShapeDtypeStruct((s, n), F32), jax.ShapeDtypeStruct((s, d), MXU_DTYPE)],
        compiler_params=_params(),
    )(x, g, w)


def _conv_taps(u, prev):
    row = lax.broadcasted_iota(jnp.int32, u.shape, 0)
    um1 = jnp.where(row == 0, prev[7:8, :], pltpu.roll(u, 1, 0))
    um2 = pltpu.roll(u, 2, 0)
    um2 = jnp.where(row == 0, prev[6:7, :], jnp.where(row == 1, prev[7:8, :], um2))
    return um1, um2


def _conv_fwd(p, conv_w, conv_b, *, name, tm=512):
    s = p.shape[0]
    c = conv_w.shape[1]
    tm = _tile(s, tm)
    hb = tm // 8

    def body(bg_ref, cg_ref, xa_ref, cgp_ref, xap_ref, w_ref, b_ref, y_ref):
        first = pl.program_id(0) == 0
        u = cg_ref[...] * xa_ref[...]
        prev = jnp.where(first, 0.0, cgp_ref[...] * xap_ref[...])
        um1, um2 = _conv_taps(u, prev)
        w = w_ref[...]
        y = b_ref[...] + w[0:1, :] * um2 + w[1:2, :] * um1 + w[2:3, :] * u
        y_ref[...] = _mx(bg_ref[...] * y)

    halo = lambda col: pl.BlockSpec((8, c), lambda i: (jnp.maximum(i * hb - 1, 0), col))
    return pl.pallas_call(
        body,
        name=name,
        grid=(s // tm,),
        in_specs=[
            pl.BlockSpec((tm, c), lambda i: (i, 0)),
            pl.BlockSpec((tm, c), lambda i: (i, 1)),
            pl.BlockSpec((tm, c), lambda i: (i, 2)),
            halo(1),
            halo(2),
            pl.BlockSpec((CONV_K, c), lambda i: (0, 0)),
            pl.BlockSpec((1, c), lambda i: (0, 0)),
        ],
        out_specs=pl.BlockSpec((tm, c), lambda i: (i, 0)),
        out_shape=jax.ShapeDtypeStruct((s, c), MXU_DTYPE),
        compiler_params=_params(),
    )(p, p, p, p, p, conv_w, conv_b)


def _layernorm_stats(x):
    mu = jnp.mean(x, axis=-1, keepdims=True)
    xc = x - mu
    r = lax.rsqrt(jnp.mean(xc * xc, axis=-1, keepdims=True) + EPS)
    return xc * r, r


def _sgu_fwd(p, ln_g, ln_b, wtril, bias_full, *, col0, name, tm=512):
    s = p.shape[0]
    c = ln_g.shape[1]
    groups = c // CHUNK
    tm = _tile(s, tm)

    def body(u_ref, v_ref, g_ref, b_ref, w_ref, bias_ref, y_ref):
        vn, _ = _layernorm_stats(_gelu(v_ref[...]))
        vb = _mx(vn * g_ref[...] + b_ref[...])
        for n in range(tm // CHUNK):
            rows = slice(n * CHUNK, (n + 1) * CHUNK)
            for gi in range(groups):
                cols = slice(gi * CHUNK, (gi + 1) * CHUNK)
                mixed = _dot(w_ref[gi], vb[rows, cols]) + bias_ref[:, cols]
                y_ref[rows, cols] = _mx(_gelu(u_ref[rows, cols]) * mixed)

    return pl.pallas_call(
        body,
        name=name,
        grid=(s // tm,),
        in_specs=[
            pl.BlockSpec((tm, c), lambda i: (i, col0)),
            pl.BlockSpec((tm, c), lambda i: (i, col0 + 1)),
            pl.BlockSpec((1, c), lambda i: (0, 0)),
            pl.BlockSpec((1, c), lambda i: (0, 0)),
            pl.BlockSpec((groups, CHUNK, CHUNK), lambda i: (0, 0, 0)),
            pl.BlockSpec((CHUNK, c), lambda i: (0, 0)),
        ],
        out_specs=pl.BlockSpec((tm, c), lambda i: (i, 0)),
        out_shape=jax.ShapeDtypeStruct((s, c), MXU_DTYPE),
        compiler_params=_params(),
    )(p, p, ln_g, ln_b, wtril, bias_full)


def _head_mean(v, headavg):
    return _dot_split_r(v, headavg)


def _qkv_prep(p, qg, kg, headavg, *, col0, name, tm=512):
    s = p.shape[0]
    c = qg.shape[1]
    tm = _tile(s, tm)

    def body(q_ref, k_ref, v_ref, qg_ref, kg_ref, avg_ref, qn_ref, kn_ref, vb_ref):
        for src, g_ref, dst in ((q_ref, qg_ref, qn_ref), (k_ref, kg_ref, kn_ref)):
            xf = src[...]
            r = lax.rsqrt(_head_mean(xf * xf, avg_ref[...]) + EPS)
            dst[...] = _mx(xf * r * g_ref[...])
        vb_ref[...] = _mx(v_ref[...])

    blk = lambda col: pl.BlockSpec((tm, c), lambda i: (i, col))
    vec = pl.BlockSpec((1, c), lambda i: (0, 0))
    out = pl.BlockSpec((tm, c), lambda i: (i, 0))
    return pl.pallas_call(
        body,
        name=name,
        grid=(s // tm,),
        in_specs=[blk(col0), blk(col0 + 1), blk(col0 + 2), vec, vec, pl.BlockSpec((c, c), lambda i: (0, 0))],
        out_specs=[out, out, out],
        out_shape=[jax.ShapeDtypeStruct((s, c), MXU_DTYPE)] * 3,
        compiler_params=_params(),
    )(p, p, p, qg, kg, headavg)


def _att_mask(t, key0, qry0):
    kpos = key0 + lax.broadcasted_iota(jnp.int32, (t, t), 0)
    qpos = qry0 + lax.broadcasted_iota(jnp.int32, (t, t), 1)
    return kpos < qpos


def _att_blocks(k_blks, q_ts, scale, mask):
    zs = [_dot(k, q) * scale for k, q in zip(k_blks, q_ts)]
    es = [jnp.exp(-jnp.abs(z)) for z in zs]
    lbs = [jnp.minimum(z, 0.0) - jnp.log1p(e) for z, e in zip(zs, es)]
    l1ms = [jnp.where(mask, lb - z, 0.0) for lb, z in zip(lbs, zs)]
    return zs, es, lbs, l1ms


def _dot_split_each(const, vs):
    his = [_mx(v) for v in vs]
    los = [_mx(v - hi.astype(F32)) for v, hi in zip(vs, his)]
    tops = [_dot(const, hi) for hi in his]
    return [top + _dot(const, lo) for top, lo in zip(tops, los)]


def _max_over(runs):
    m = runs[0]
    for r in runs[1:]:
        m = jnp.maximum(m, r)
    return jnp.max(m)


def _att_fwd(k_hm, qt_hm, vt_hm, later, *, name, hg=4):
    h, nb, t, d = k_hm.shape
    s = nb * t
    scale = 1.0 / math.sqrt(d)
    assert h % hg == 0

    def body(k_ref, qt_ref, vt_ref, later_ref, o_ref):
        i = pl.program_id(1)
        q_ts = [qt_ref[g] for g in range(hg)]

        def cond(carry):
            j, _, _, rmax = carry
            return jnp.logical_and(j >= 0, rmax > EXP_UNDERFLOW)

        def step(carry):
            j, runs, accs, _ = carry
            mask = _att_mask(t, j * t, i * t)
            heads = range(hg)
            _, _, lbs, l1ms = _att_blocks([k_ref[g, j] for g in heads], q_ts, scale, mask)
            afters = _dot_split_each(later_ref[...], l1ms)
            weights = [_mx(jnp.where(mask, jnp.exp(lbs[g] + afters[g] + runs[g]), 0.0)) for g in heads]
            new_accs = [accs[g] + _dot(vt_ref[g, j], weights[g]) for g in heads]
            new_runs = [runs[g] + jnp.sum(l1ms[g], axis=0, keepdims=True) for g in heads]
            return j - 1, tuple(new_runs), tuple(new_accs), _max_over(new_runs)

        init = (i, tuple(jnp.zeros((1, t), F32) for _ in range(hg)), tuple(jnp.zeros((d, t), F32) for _ in range(hg)),
                jnp.float32(0.0))
        _, _, accs, _ = lax.while_loop(cond, step, init)
        for g in range(hg):
            o_ref[g] = _mx(accs[g])

    return pl.pallas_call(
        body,
        name=name,
        grid=(h // hg, nb),
        in_specs=[
            pl.BlockSpec((hg, nb, t, d), lambda hh, i: (hh, 0, 0, 0)),
            pl.BlockSpec((hg, d, t), lambda hh, i: (hh, 0, i)),
            pl.BlockSpec((hg, nb, d, t), lambda hh, i: (hh, 0, 0, 0)),
            pl.BlockSpec((t, t), lambda hh, i: (0, 0)),
        ],
        out_specs=pl.BlockSpec((hg, d, t), lambda hh, i: (hh, 0, i)),
        out_shape=jax.ShapeDtypeStruct((h, d, s), MXU_DTYPE),
        compiler_params=_params(),
    )(k_hm, qt_hm, vt_hm, later)


def _merge_fwd(ys, p, b_gate, w_bo, w_o, x, *, gate_col0, name, tm=256):
    s, d = x.shape
    nbr, c, _ = w_bo.shape
    tm = _tile(s, tm)

    def body(ya_ref, yb_ref, yc_ref, g0_ref, g1_ref, g2_ref, bg_ref, wbo_ref, wo_ref, x_ref, x1_ref, m_ref):
        merged = jnp.zeros((tm, d), F32)
        for i, (y_ref, g_ref) in enumerate(((ya_ref, g0_ref), (yb_ref, g1_ref), (yc_ref, g2_ref))):
            gate = _sigmoid(g_ref[...] + bg_ref[:, i * d:(i + 1) * d])
            merged = merged + gate * _dot(y_ref[...], wbo_ref[i])
        mb = _mx(merged)
        m_ref[...] = mb
        x1_ref[...] = x_ref[...] + _dot(mb, wo_ref[...])

    yblk = pl.BlockSpec((tm, c), lambda i: (i, 0))
    gblk = lambda k: pl.BlockSpec((tm, d), lambda i: (i, gate_col0 + k))
    xblk = pl.BlockSpec((tm, d), lambda i: (i, 0))
    return pl.pallas_call(
        body,
        name=name,
        grid=(s // tm,),
        in_specs=[
            yblk, yblk, yblk, gblk(0), gblk(1), gblk(2),
            pl.BlockSpec((1, nbr * d), lambda i: (0, 0)),
            pl.BlockSpec((nbr, c, d), lambda i: (0, 0, 0)),
            pl.BlockSpec((d, d), lambda i: (0, 0)),
            xblk,
        ],
        out_specs=[xblk, xblk],
        out_shape=[jax.ShapeDtypeStruct((s, d), F32), jax.ShapeDtypeStruct((s, d), MXU_DTYPE)],
        compiler_params=_params(),
    )(*ys, p, p, p, b_gate, w_bo, w_o, x)


def _ffn_down(gu, w_down, x, *, name, tm=256):
    s, d = x.shape
    f = w_down.shape[0]
    tm = _tile(s, tm)

    def body(g_ref, u_ref, w_ref, x_ref, o_ref):
        g = g_ref[...]
        act = _mx(g * _sigmoid(g) * u_ref[...])
        o_ref[...] = x_ref[...] + _dot(act, w_ref[...])

    return pl.pallas_call(
        body,
        name=name,
        grid=(s // tm,),
        in_specs=[
            pl.BlockSpec((tm, f), lambda i: (i, 0)),
            pl.BlockSpec((tm, f), lambda i: (i, 1)),
            pl.BlockSpec((f, d), lambda i: (0, 0)),
            pl.BlockSpec((tm, d), lambda i: (i, 0)),
        ],
        out_specs=pl.BlockSpec((tm, d), lambda i: (i, 0)),
        out_shape=jax.ShapeDtypeStruct((s, d), F32),
        compiler_params=_params(),
    )(gu, gu, w_down, x)


def _loss_head(y, target, *, name, tm=512):
    s, d = y.shape
    tm = _tile(s, tm)

    def body(y_ref, t_ref, l_ref, dy_ref):
        @pl.when(pl.program_id(0) == 0)
        def _():
            l_ref[...] = jnp.zeros_like(l_ref)

        err = y_ref[...] - t_ref[...]
        dy_ref[...] = err * (1.0 / d)
        sq = _row_fold(err * err)
        part = sq[:, 0:128]
        for k in range(1, d // 128):
            part = part + sq[:, k * 128:(k + 1) * 128]
        l_ref[...] += part * (0.5 / d)

    blk = pl.BlockSpec((tm, d), lambda i: (i, 0))
    return pl.pallas_call(
        body,
        name=name,
        grid=(s // tm,),
        in_specs=[blk, blk],
        out_specs=[pl.BlockSpec((8, 128), lambda i: (0, 0)), blk],
        out_shape=[jax.ShapeDtypeStruct((8, 128), F32), jax.ShapeDtypeStruct((s, d), F32)],
        compiler_params=_params(),
    )(y, target)


def _matmul_tn(a, b, *, name, t1=512, tn=None, tm=1024):
    m, k1 = a.shape
    n = b.shape[1]
    t1 = _tile(k1, t1)
    tn = tn or n
    tm = _tile(m, tm)
    steps = m // tm
    assert n % tn == 0

    def body(a_ref, b_ref, o_ref):
        @pl.when(pl.program_id(2) == 0)
        def _():
            o_ref[...] = jnp.zeros_like(o_ref)

        o_ref[...] += _dot_tn(a_ref[...], b_ref[...])

    return pl.pallas_call(
        body,
        name=name,
        grid=(k1 // t1, n // tn, steps),
        in_specs=[
            pl.BlockSpec((tm, t1), lambda i, j, k: (k, i)),
            pl.BlockSpec((tm, tn), lambda i, j, k: (k, j)),
        ],
        out_specs=pl.BlockSpec((t1, tn), lambda i, j, k: (i, j)),
        out_shape=jax.ShapeDtypeStruct((k1, n), F32),
        compiler_params=_params(),
    )(a, b)


def _matmul_nt_normbwd(dp, w, x, g, dres, *, name, tm=512, tk=None):
    s, d = x.shape
    k = w.shape[1]
    tm = _tile(s, tm)
    tk = tk or k
    steps = k // tk
    assert k % tk == 0

    def body(dp_ref, w_ref, x_ref, g_ref, dres_ref, dx_ref, dg_ref, acc_ref):
        kk = pl.program_id(1)

        @pl.when(jnp.logical_and(pl.program_id(0) == 0, kk == 0))
        def _():
            dg_ref[...] = jnp.zeros_like(dg_ref)

        @pl.when(kk == 0)
        def _():
            acc_ref[...] = jnp.zeros_like(acc_ref)

        acc_ref[...] += _dot_nt(dp_ref[...], w_ref[...])

        @pl.when(kk == steps - 1)
        def _():
            dh = acc_ref[...]
            xf = x_ref[...]
            r = lax.rsqrt(jnp.mean(xf * xf, axis=-1, keepdims=True) + EPS)
            y = xf * r
            dy = dh * g_ref[...]
            dx_ref[...] = dres_ref[...] + r * (dy - y * jnp.mean(dy * y, axis=-1, keepdims=True))
            dg_ref[...] += _row_fold(dh * y)

    xblk = pl.BlockSpec((tm, d), lambda i, j: (i, 0))
    return pl.pallas_call(
        body,
        name=name,
        grid=(s // tm, steps),
        in_specs=[
            pl.BlockSpec((tm, tk), lambda i, j: (i, j)),
            pl.BlockSpec((d, tk), lambda i, j: (0, j)),
            xblk,
            pl.BlockSpec((1, d), lambda i, j: (0, 0)),
            xblk,
        ],
        out_specs=[xblk, pl.BlockSpec((8, d), lambda i, j: (0, 0))],
        out_shape=[jax.ShapeDtypeStruct((s, d), F32), jax.ShapeDtypeStruct((8, d), F32)],
        scratch_shapes=[pltpu.VMEM((tm, d), F32)],
        compiler_params=_params(),
    )(dp, w, x, g, dres)


def _ffn_bwd(dx, gu, w_down, *, name, tm=256):
    s, d = dx.shape
    f = w_down.shape[0]
    tm = _tile(s, tm)

    def body(dx_ref, g_ref, u_ref, w_ref, dgu_ref, act_ref, dxb_ref):
        dxb = _mx(dx_ref[...])
        dxb_ref[...] = dxb
        dact = _dot_nt(dxb, w_ref[...])
        g = g_ref[...]
        u = u_ref[...]
        sg = _sigmoid(g)
        silu = g * sg
        act_ref[...] = _mx(silu * u)
        dgu_ref[:, 0:f] = _mx(dact * u * (sg * (1.0 + g * (1.0 - sg))))
        dgu_ref[:, f:2 * f] = _mx(dact * silu)

    fblk = lambda col: pl.BlockSpec((tm, f), lambda i: (i, col))
    dblk = pl.BlockSpec((tm, d), lambda i: (i, 0))
    return pl.pallas_call(
        body,
        name=name,
        grid=(s // tm,),
        in_specs=[dblk, fblk(0), fblk(1), pl.BlockSpec((f, d), lambda i: (0, 0))],
        out_specs=[pl.BlockSpec((tm, 2 * f), lambda i: (i, 0)), fblk(0), dblk],
        out_shape=[
            jax.ShapeDtypeStruct((s, 2 * f), MXU_DTYPE),
            jax.ShapeDtypeStruct((s, f), MXU_DTYPE),
            jax.ShapeDtypeStruct((s, d), MXU_DTYPE),
        ],
        compiler_params=_params(),
    )(dx, gu, gu, w_down)


def _merge_bwd(dx, ys, p, b_gate, w_bo, w_o, *, gate_col0, name, tm=256):
    s, d = dx.shape
    nbr, c, _ = w_bo.shape
    tm = _tile(s, tm)

    def body(dx_ref, ya_ref, yb_ref, yc_ref, g0_ref, g1_ref, g2_ref, bg_ref, wbo_ref, wo_ref,
             dgate_ref, dya_ref, dyb_ref, dyc_ref, dd0_ref, dd1_ref, dd2_ref, dxb_ref, dbg_ref):
        @pl.when(pl.program_id(0) == 0)
        def _():
            dbg_ref[...] = jnp.zeros_like(dbg_ref)

        dxb = _mx(dx_ref[...])
        dxb_ref[...] = dxb
        dmerged = _dot_nt(dxb, wo_ref[...])
        branches = ((ya_ref, g0_ref, dya_ref, dd0_ref), (yb_ref, g1_ref, dyb_ref, dd1_ref), (yc_ref, g2_ref, dyc_ref, dd2_ref))
        for i, (y_ref, g_ref, dy_ref, dd_ref) in enumerate(branches):
            cols = slice(i * d, (i + 1) * d)
            gate = _sigmoid(g_ref[...] + bg_ref[:, cols])
            yd = _dot(y_ref[...], wbo_ref[i])
            dyd = _mx(dmerged * gate)
            dd_ref[...] = dyd
            dy_ref[...] = _dot_nt(dyd, wbo_ref[i])
            dpre = dmerged * yd * gate * (1.0 - gate)
            dgate_ref[:, cols] = _mx(dpre)
            dbg_ref[:, cols] += _row_fold(dpre)

    yblk = pl.BlockSpec((tm, c), lambda i: (i, 0))
    gblk = lambda k: pl.BlockSpec((tm, d), lambda i: (i, gate_col0 + k))
    dblk = pl.BlockSpec((tm, d), lambda i: (i, 0))
    return pl.pallas_call(
        body,
        name=name,
        grid=(s // tm,),
        in_specs=[
            dblk, yblk, yblk, yblk, gblk(0), gblk(1), gblk(2),
            pl.BlockSpec((1, nbr * d), lambda i: (0, 0)),
            pl.BlockSpec((nbr, c, d), lambda i: (0, 0, 0)),
            pl.BlockSpec((d, d), lambda i: (0, 0)),
        ],
        out_specs=[pl.BlockSpec((tm, nbr * d), lambda i: (i, 0)), yblk, yblk, yblk, dblk, dblk, dblk, dblk,
                   pl.BlockSpec((8, nbr * d), lambda i: (0, 0))],
        out_shape=[jax.ShapeDtypeStruct((s, nbr * d), MXU_DTYPE)] + [jax.ShapeDtypeStruct((s, c), F32)] * 3
        + [jax.ShapeDtypeStruct((s, d), MXU_DTYPE)] * 4 + [jax.ShapeDtypeStruct((8, nbr * d), F32)],
        compiler_params=_params(),
    )(dx, *ys, p, p, p, b_gate, w_bo, w_o)


def _conv_bwd(p, dya, conv_w, conv_b, *, name, tm=512):
    s = p.shape[0]
    c = conv_w.shape[1]
    tm = _tile(s, tm)
    hb = tm // 8
    last = s // tm - 1

    def body(bg_ref, cg_ref, xa_ref, cgp_ref, xap_ref, dy_ref, dyn_ref, bgn_ref, w_ref, b_ref,
             dp_ref, dw_ref):
        i = pl.program_id(0)

        @pl.when(i == 0)
        def _():
            dw_ref[...] = jnp.zeros_like(dw_ref)

        cg = cg_ref[...]
        xa = xa_ref[...]
        u = cg * xa
        prev = jnp.where(i == 0, 0.0, cgp_ref[...] * xap_ref[...])
        um1, um2 = _conv_taps(u, prev)
        w = w_ref[...]
        y = b_ref[...] + w[0:1, :] * um2 + w[1:2, :] * um1 + w[2:3, :] * u
        dya = dy_ref[...]
        dp_ref[:, 0:c] = _mx(dya * y)
        dyv = dya * bg_ref[...]
        nxt = jnp.where(i == last, 0.0, dyn_ref[...] * bgn_ref[...])
        row = lax.broadcasted_iota(jnp.int32, dyv.shape, 0)
        dp1 = jnp.where(row == tm - 1, nxt[0:1, :], pltpu.roll(dyv, tm - 1, 0))
        dp2 = pltpu.roll(dyv, tm - 2, 0)
        dp2 = jnp.where(row == tm - 2, nxt[0:1, :], jnp.where(row == tm - 1, nxt[1:2, :], dp2))
        du = w[2:3, :] * dyv + w[1:2, :] * dp1 + w[0:1, :] * dp2
        dp_ref[:, c:2 * c] = _mx(du * xa)
        dp_ref[:, 2 * c:3 * c] = _mx(du * cg)
        dw_ref[0] += _row_fold(dyv * um2)
        dw_ref[1] += _row_fold(dyv * um1)
        dw_ref[2] += _row_fold(dyv * u)
        dw_ref[3] += _row_fold(dyv)

    blk = lambda col: pl.BlockSpec((tm, c), lambda i: (i, col))
    halo = lambda col: pl.BlockSpec((8, c), lambda i: (jnp.maximum(i * hb - 1, 0), col))
    nhalo = lambda col: pl.BlockSpec((8, c), lambda i: (jnp.minimum((i + 1) * hb, s // 8 - 1), col))
    return pl.pallas_call(
        body,
        name=name,
        grid=(s // tm,),
        in_specs=[blk(0), blk(1), blk(2), halo(1), halo(2), blk(0), nhalo(0), nhalo(0),
                  pl.BlockSpec((CONV_K, c), lambda i: (0, 0)), pl.BlockSpec((1, c), lambda i: (0, 0))],
        out_specs=[pl.BlockSpec((tm, 3 * c), lambda i: (i, 0)), pl.BlockSpec((4, 8, c), lambda i: (0, 0, 0))],
        out_shape=[jax.ShapeDtypeStruct((s, 3 * c), MXU_DTYPE), jax.ShapeDtypeStruct((4, 8, c), F32)],
        compiler_params=_params(),
    )(p, p, p, p, p, dya, dya, p, conv_w, conv_b)


def _sgu_bwd(p, dyb, ln_g, ln_b, wtril, wtril_t, bias_full, *, col0, name, tm=512):
    s = p.shape[0]
    c = ln_g.shape[1]
    groups = c // CHUNK
    tm = _tile(s, tm)

    def body(u_ref, v_ref, dy_ref, g_ref, b_ref, w_ref, wt_ref, bias_ref, du_ref, dv_ref, dln_ref, dw_ref, dbias_ref,
             dvn_ref):
        @pl.when(pl.program_id(0) == 0)
        def _():
            dln_ref[...] = jnp.zeros_like(dln_ref)
            dw_ref[...] = jnp.zeros_like(dw_ref)
            dbias_ref[...] = jnp.zeros_like(dbias_ref)

        sv = v_ref[...]
        xhat, r = _layernorm_stats(_gelu(sv))
        vb = _mx(xhat * g_ref[...] + b_ref[...])
        for n in range(tm // CHUNK):
            rows = slice(n * CHUNK, (n + 1) * CHUNK)
            for gi in range(groups):
                cols = slice(gi * CHUNK, (gi + 1) * CHUNK)
                su = u_ref[rows, cols]
                dy = dy_ref[rows, cols]
                vblk = vb[rows, cols]
                mixed = _dot(w_ref[gi], vblk) + bias_ref[:, cols]
                du_ref[rows, cols] = _mx(dy * mixed * _gelu_grad(su))
                dmixed = dy * _gelu(su)
                dmb = _mx(dmixed)
                dvn_ref[rows, cols] = _dot(wt_ref[gi], dmb)
                dw_ref[gi] += _dot_nt(dmb, vblk)
                dbias_ref[:, cols] += dmixed
        dvn = dvn_ref[...]
        dln_ref[0] += _row_fold(dvn * xhat)
        dln_ref[1] += _row_fold(dvn)
        dxh = dvn * g_ref[...]
        dgv = r * (dxh - jnp.mean(dxh, axis=-1, keepdims=True) - xhat * jnp.mean(dxh * xhat, axis=-1, keepdims=True))
        dv_ref[...] = _mx(dgv * _gelu_grad(sv))

    blk = lambda col: pl.BlockSpec((tm, c), lambda i: (i, col))
    vec = pl.BlockSpec((1, c), lambda i: (0, 0))
    wspec = pl.BlockSpec((groups, CHUNK, CHUNK), lambda i: (0, 0, 0))
    return pl.pallas_call(
        body,
        name=name,
        grid=(s // tm,),
        in_specs=[blk(col0), blk(col0 + 1), blk(0), vec, vec, wspec, wspec, pl.BlockSpec((CHUNK, c), lambda i: (0, 0))],
        out_specs=[blk(0), blk(0), pl.BlockSpec((2, 8, c), lambda i: (0, 0, 0)), wspec,
                   pl.BlockSpec((CHUNK, c), lambda i: (0, 0))],
        out_shape=[jax.ShapeDtypeStruct((s, c), MXU_DTYPE), jax.ShapeDtypeStruct((s, c), MXU_DTYPE),
                   jax.ShapeDtypeStruct((2, 8, c), F32), jax.ShapeDtypeStruct((groups, CHUNK, CHUNK), F32),
                   jax.ShapeDtypeStruct((CHUNK, c), F32)],
        scratch_shapes=[pltpu.VMEM((tm, c), F32)],
        compiler_params=_params(),
    )(p, p, dyb, ln_g, ln_b, wtril, wtril_t, bias_full)


def _att_bwd(k_hm, kt_hm, v_hm, qt_hm, dot_hm, later, earlier, *, name, hg=2):
    h, nb, t, d = k_hm.shape
    s = nb * t
    scale = 1.0 / math.sqrt(d)
    assert h % hg == 0

    def body(k_ref, kt_ref, v_ref, qt_ref, dot_ref, later_ref, earlier_ref, dqt_ref, dkt_ref, dvt_ref, run_ref):
        i = pl.program_id(1)

        @pl.when(i == 0)
        def _():
            dkt_ref[...] = jnp.zeros_like(dkt_ref)
            dvt_ref[...] = jnp.zeros_like(dvt_ref)

        q_ts = [qt_ref[g] for g in range(hg)]
        do_ts = [dot_ref[g] for g in range(hg)]

        def cond(carry):
            j, _, rmax = carry
            return jnp.logical_and(j >= 0, rmax > EXP_UNDERFLOW)

        def down(carry):
            j, runs, _ = carry
            mask = _att_mask(t, j * t, i * t)
            for g in range(hg):
                run_ref[g * nb + j] = runs[g]
            _, _, _, l1ms = _att_blocks([k_ref[g, j] for g in range(hg)], q_ts, scale, mask)
            new_runs = [runs[g] + jnp.sum(l1ms[g], axis=0, keepdims=True) for g in range(hg)]
            return j - 1, tuple(new_runs), _max_over(new_runs)

        zeros_row = tuple(jnp.zeros((1, t), F32) for _ in range(hg))
        jlow, _, _ = lax.while_loop(cond, down, (i, zeros_row, jnp.float32(0.0)))

        def up(j, carry):
            gsums, dqts = carry
            mask = _att_mask(t, j * t, i * t)
            heads = range(hg)
            zs, es, lbs, l1ms = _att_blocks([k_ref[g, j] for g in heads], q_ts, scale, mask)
            afters = _dot_split_each(later_ref[...], l1ms)
            das = [_dot(v_ref[g, j], do_ts[g]) for g in heads]
            weights = [jnp.where(mask, jnp.exp(lbs[g] + afters[g] + run_ref[g * nb + j]), 0.0) for g in heads]
            grs = [das[g] * weights[g] for g in heads]
            gbefores = _dot_split_each(earlier_ref[...], grs)
            dzs = []
            for g in heads:
                inv = 1.0 / (1.0 + es[g])
                pos = zs[g] >= 0.0
                beta = jnp.where(pos, inv, es[g] * inv)
                omb = jnp.where(pos, es[g] * inv, inv)
                dzs.append(_mx(jnp.where(mask, grs[g] * omb - (gbefores[g] + gsums[g]) * beta, 0.0) * scale))
            new_dqts = [dqts[g] + _dot(kt_ref[g, j], dzs[g]) for g in heads]
            for g in heads:
                dkt_ref[g, j] += _dot_nt(q_ts[g], dzs[g])
            for g in heads:
                dvt_ref[g, j] += _dot_nt(do_ts[g], _mx(weights[g]))
            new_gsums = [gsums[g] + jnp.sum(grs[g], axis=0, keepdims=True) for g in heads]
            return tuple(new_gsums), tuple(new_dqts)

        _, dqts = lax.fori_loop(jlow + 1, i + 1, up, (zeros_row, tuple(jnp.zeros((d, t), F32) for _ in range(hg))))
        for g in range(hg):
            dqt_ref[g] = dqts[g]

    whole = lambda a, b: pl.BlockSpec((hg, nb, a, b), lambda hh, i: (hh, 0, 0, 0))
    cols = pl.BlockSpec((hg, d, t), lambda hh, i: (hh, 0, i))
    tri = pl.BlockSpec((t, t), lambda hh, i: (0, 0))
    return pl.pallas_call(
        body,
        name=name,
        grid=(h // hg, nb),
        in_specs=[whole(t, d), whole(d, t), whole(t, d), cols, cols, tri, tri],
        out_specs=[cols, whole(d, t), whole(d, t)],
        out_shape=[jax.ShapeDtypeStruct((h, d, s), F32), jax.ShapeDtypeStruct((h, nb, d, t), F32),
                   jax.ShapeDtypeStruct((h, nb, d, t), F32)],
        scratch_shapes=[pltpu.VMEM((hg * nb, 1, t), F32)],
        compiler_params=_params(),
    )(k_hm, kt_hm, v_hm, qt_hm, dot_hm, later, earlier)


def _qkv_bwd(p, dqn, dkn, dv, qg, kg, headavg, *, col0, name, tm=512):
    s = p.shape[0]
    c = qg.shape[1]
    tm = _tile(s, tm)

    def body(q_ref, k_ref, dqn_ref, dkn_ref, dv_ref, qg_ref, kg_ref, avg_ref, dq_ref, dk_ref, dvb_ref, dg_ref):
        @pl.when(pl.program_id(0) == 0)
        def _():
            dg_ref[...] = jnp.zeros_like(dg_ref)

        for n, (src, dn_ref, g_ref, dst) in enumerate(((q_ref, dqn_ref, qg_ref, dq_ref), (k_ref, dkn_ref, kg_ref, dk_ref))):
            xf = src[...]
            r = lax.rsqrt(_head_mean(xf * xf, avg_ref[...]) + EPS)
            y = xf * r
            dn = dn_ref[...]
            dy = dn * g_ref[...]
            dst[...] = _mx(r * (dy - y * _head_mean(dy * y, avg_ref[...])))
            dg_ref[n] += _row_fold(dn * y)
        dvb_ref[...] = _mx(dv_ref[...])

    blk = lambda col: pl.BlockSpec((tm, c), lambda i: (i, col))
    vec = pl.BlockSpec((1, c), lambda i: (0, 0))
    out_shape = jax.ShapeDtypeStruct((s, c), MXU_DTYPE)
    return pl.pallas_call(
        body,
        name=name,
        grid=(s // tm,),
        in_specs=[blk(col0), blk(col0 + 1), blk(0), blk(0), blk(0), vec, vec, pl.BlockSpec((c, c), lambda i: (0, 0))],
        out_specs=[blk(0), blk(0), blk(0), pl.BlockSpec((2, 8, c), lambda i: (0, 0, 0))],
        out_shape=[out_shape, out_shape, out_shape, jax.ShapeDtypeStruct((2, 8, c), F32)],
        compiler_params=_params(),
    )(p, p, dqn, dkn, dv, qg, kg, headavg)


def _place():
    x, y, c = lax.axis_index("x"), lax.axis_index("y"), lax.axis_index("c")
    chips = [(1 - x, y), (x, 1 - y), (1 - x, 1 - y)]
    return x, y, c, chips


def _shard_of(ref, axis, chip, width):
    idx = [slice(None)] * len(ref.shape)
    idx[axis] = pl.ds(chip * width, width)
    return ref.at[tuple(idx)]


def _place_shard(w, axis, chip_idx, dtype, *, name):
    layers = w.shape[0]
    rows, cols = _as_rows(w.shape[1:])
    if axis == len(w.shape) - 2:
        tr = _row_tile(rows, cols, 2)
        out_shape = (layers, rows, cols * N_CHIPS)
        out_spec = pl.BlockSpec((None, tr, cols), lambda l, i, j_ref: (l, i, j_ref[0]))
    else:
        assert axis == 0
        tr = _row_tile(rows, cols, 2)
        per = rows // tr
        out_shape = (layers, rows * N_CHIPS, cols)
        out_spec = pl.BlockSpec((None, tr, cols), lambda l, i, j_ref: (l, j_ref[0] * per + i, 0))
    full = list(w.shape)
    full[1 + axis] *= N_CHIPS

    def body(j_ref, w_ref, o_ref):
        o_ref[...] = w_ref[...].astype(dtype)

    out = pl.pallas_call(
        body,
        name=name,
        grid_spec=pltpu.PrefetchScalarGridSpec(
            num_scalar_prefetch=1,
            grid=(layers, rows // tr),
            in_specs=[pl.BlockSpec((None, tr, cols), lambda l, i, j_ref: (l, i, 0))],
            out_specs=out_spec,
        ),
        out_shape=jax.ShapeDtypeStruct(out_shape, dtype),
        compiler_params=_params(),
    )(chip_idx, w.reshape(layers, rows, cols))
    return out.reshape(full)


def _gather_weights(placed, axes, *, name):
    n = len(placed)
    widths = [pa.shape[1 + ax] // N_CHIPS for pa, ax in zip(placed, axes)]

    def body(*refs):
        outs = refs[n:2 * n]
        send_sem, recv_sem, fsend_sem, frecv_sem = refs[2 * n:]
        x, y, c, chips = _place()
        me = 2 * x + y
        sibling = (x, y, 1 - c)

        def slot(a, layer, chip):
            return _shard_of(outs[a].at[layer], axes[a], chip, widths[a])

        def send(a, k):
            return pltpu.make_async_remote_copy(
                src_ref=slot(a, c, me), dst_ref=slot(a, c, me), send_sem=send_sem.at[a, k], recv_sem=recv_sem.at[a, k],
                device_id=(*chips[k], c), device_id_type=MESH)

        def landed(a, k):
            cx, cy = chips[k]
            blk = slot(a, c, 2 * cx + cy)
            return pltpu.make_async_remote_copy(
                src_ref=blk, dst_ref=blk, send_sem=send_sem.at[a, k], recv_sem=recv_sem.at[a, k],
                device_id=(*chips[k], c), device_id_type=MESH)

        def forward(a, k, layer):
            cx, cy = chips[k]
            blk = slot(a, layer, 2 * cx + cy)
            return pltpu.make_async_remote_copy(
                src_ref=blk, dst_ref=blk, send_sem=fsend_sem.at[a, k], recv_sem=frecv_sem.at[a, k],
                device_id=sibling, device_id_type=MESH)

        for a in range(n):
            for k in range(3):
                send(a, k).start()
        for a in range(n):
            for k in range(3):
                landed(a, k).wait_recv()
                forward(a, k, c).start()
        for a in range(n):
            for k in range(3):
                forward(a, k, 1 - c).wait_recv()
        for a in range(n):
            for k in range(3):
                send(a, k).wait_send()
                forward(a, k, c).wait_send()

    return pl.pallas_call(
        body,
        name=name,
        in_specs=[ANY] * n,
        out_specs=[ANY] * n,
        out_shape=[jax.ShapeDtypeStruct(pa.shape, pa.dtype) for pa in placed],
        input_output_aliases={a: a for a in range(n)},
        scratch_shapes=[pltpu.SemaphoreType.DMA((n, 3))] * 4,
    )(*placed)


def _pair_exchange(grads, *, name):
    n = len(grads)

    def body(*refs):
        ins, outs = refs[:n], refs[n:2 * n]
        send_sem, recv_sem = refs[2 * n:]
        x, y, c, _ = _place()
        copies = [
            pltpu.make_async_remote_copy(
                src_ref=ins[a].at[1 - c], dst_ref=outs[a], send_sem=send_sem.at[a], recv_sem=recv_sem.at[a],
                device_id=(x, y, 1 - c), device_id_type=MESH)
            for a in range(n)
        ]
        for cp in copies:
            cp.start()
        for cp in copies:
            cp.wait()

    return pl.pallas_call(
        body,
        name=name,
        in_specs=[ANY] * n,
        out_specs=[ANY] * n,
        out_shape=[jax.ShapeDtypeStruct(g.shape[1:], g.dtype) for g in grads],
        scratch_shapes=[pltpu.SemaphoreType.DMA((n,)), pltpu.SemaphoreType.DMA((n,))],
    )(*grads)


def _chip_scatter(sums, axes, *, name):
    n = len(sums)
    widths = [sm.shape[ax] // N_CHIPS for sm, ax in zip(sums, axes)]
    out_shapes = []
    for sm, ax, w in zip(sums, axes, widths):
        shp = list(sm.shape)
        shp[ax] = w
        out_shapes.append(jax.ShapeDtypeStruct((3, *shp), sm.dtype))

    def body(*refs):
        ins, outs = refs[:n], refs[n:2 * n]
        send_sem, recv_sem = refs[2 * n:]
        x, y, c, chips = _place()
        copies = []
        for a in range(n):
            for k in range(3):
                cx, cy = chips[k]
                copies.append(pltpu.make_async_remote_copy(
                    src_ref=_shard_of(ins[a], axes[a], 2 * cx + cy, widths[a]), dst_ref=outs[a].at[k],
                    send_sem=send_sem.at[a, k], recv_sem=recv_sem.at[a, k], device_id=(cx, cy, c), device_id_type=MESH))
        for cp in copies:
            cp.start()
        for cp in copies:
            cp.wait()

    return pl.pallas_call(
        body,
        name=name,
        in_specs=[ANY] * n,
        out_specs=[ANY] * n,
        out_shape=out_shapes,
        scratch_shapes=[pltpu.SemaphoreType.DMA((n, 3)), pltpu.SemaphoreType.DMA((n, 3))],
    )(*sums)


def _pair_share(finals, *, name):
    n = len(finals)

    def body(*refs):
        outs = refs[n:2 * n]
        send_sem, recv_sem = refs[2 * n:]
        x, y, c, _ = _place()
        for a in range(n):
            pltpu.make_async_remote_copy(
                src_ref=outs[a].at[c], dst_ref=outs[a].at[c], send_sem=send_sem.at[a], recv_sem=recv_sem.at[a],
                device_id=(x, y, 1 - c), device_id_type=MESH).start()
        for a in range(n):
            pltpu.make_async_remote_copy(
                src_ref=outs[a].at[c], dst_ref=outs[a].at[1 - c], send_sem=send_sem.at[a], recv_sem=recv_sem.at[a],
                device_id=(x, y, 1 - c), device_id_type=MESH).wait()

    return pl.pallas_call(
        body,
        name=name,
        in_specs=[ANY] * n,
        out_specs=[ANY] * n,
        out_shape=[jax.ShapeDtypeStruct(f.shape, f.dtype) for f in finals],
        input_output_aliases={a: a for a in range(n)},
        scratch_shapes=[pltpu.SemaphoreType.DMA((n,))] * 2,
    )(*finals)


def _all_reduce_small(packed, *, name):
    r = packed.shape[0]

    def body(in_ref, out_ref, slots, send_sem, recv_sem):
        x, y, c, _ = _place()
        me = 4 * x + 2 * y + c
        slots[me] = in_ref[...]
        copies = []
        for k in range(1, N_DEV):
            peer = (x ^ (k >> 2), y ^ ((k >> 1) & 1), c ^ (k & 1))
            copies.append(pltpu.make_async_remote_copy(
                src_ref=in_ref, dst_ref=slots.at[me], send_sem=send_sem.at[k - 1], recv_sem=recv_sem.at[k - 1],
                device_id=peer, device_id_type=MESH))
        for cp in copies:
            cp.start()
        for k in range(1, N_DEV):
            pltpu.make_async_remote_copy(
                src_ref=in_ref, dst_ref=slots.at[me ^ k], send_sem=send_sem.at[k - 1], recv_sem=recv_sem.at[k - 1],
                device_id=(x, y, c), device_id_type=MESH).wait()
        total = slots[0]
        for dev in range(1, N_DEV):
            total = total + slots[dev]
        out_ref[...] = total

    return pl.pallas_call(
        body,
        name=name,
        in_specs=[pl.BlockSpec(memory_space=pltpu.VMEM)],
        out_specs=pl.BlockSpec(memory_space=pltpu.VMEM),
        out_shape=jax.ShapeDtypeStruct((r, 128), F32),
        scratch_shapes=[pltpu.VMEM((N_DEV, r, 128), F32), pltpu.SemaphoreType.DMA((N_DEV - 1,)),
                        pltpu.SemaphoreType.DMA((N_DEV - 1,))],
        compiler_params=_params(),
    )(packed)


def _as_rows(shape):
    cols = shape[-1]
    return math.prod(shape[:-1]), cols


ELEMENTWISE_VMEM = 24 * 1024 * 1024


def _row_tile(rows, cols, n_arrays):
    cap = ELEMENTWISE_VMEM // (n_arrays * 2 * 4 * cols)
    best = None
    for t in range(8, min(rows, cap) + 1, 8):
        if rows % t == 0:
            best = t
    assert best is not None, (rows, cols)
    return best


def _pair_sum(g, other, c_idx, *, name):
    rows, cols = _as_rows(other.shape)
    tr = _row_tile(rows, cols, 3)
    g2 = g.reshape(2, rows, cols)

    def body(c_ref, g_ref, o_ref, out_ref):
        out_ref[...] = g_ref[...] + o_ref[...]

    out = pl.pallas_call(
        body,
        name=name,
        grid_spec=pltpu.PrefetchScalarGridSpec(
            num_scalar_prefetch=1,
            grid=(rows // tr,),
            in_specs=[pl.BlockSpec((None, tr, cols), lambda i, c_ref: (c_ref[0], i, 0)),
                      pl.BlockSpec((tr, cols), lambda i, c_ref: (i, 0))],
            out_specs=pl.BlockSpec((tr, cols), lambda i, c_ref: (i, 0)),
        ),
        out_shape=jax.ShapeDtypeStruct((rows, cols), F32),
        compiler_params=_params(),
    )(c_idx, g2, other.reshape(rows, cols))
    return out.reshape(other.shape)


def _chip_sum(mine, got, axis, chip_idx, c_idx, *, name):
    shard_shape = got.shape[1:]
    rows, cols = _as_rows(shard_shape)
    tr = _row_tile(rows, cols, 5)
    if axis == len(mine.shape) - 1:
        m2 = mine.reshape(rows, cols * N_CHIPS)
        mine_spec = pl.BlockSpec((tr, cols), lambda i, j_ref, c_ref: (i, j_ref[0]))
    else:
        assert axis == 0
        m2 = mine.reshape(N_CHIPS, rows, cols)
        mine_spec = pl.BlockSpec((None, tr, cols), lambda i, j_ref, c_ref: (j_ref[0], i, 0))

    def body(j_ref, c_ref, m_ref, got_ref, out_ref):
        out_ref[...] = ((m_ref[...] + got_ref[0]) + got_ref[1]) + got_ref[2]

    out = pl.pallas_call(
        body,
        name=name,
        grid_spec=pltpu.PrefetchScalarGridSpec(
            num_scalar_prefetch=2,
            grid=(rows // tr,),
            in_specs=[mine_spec, pl.BlockSpec((3, tr, cols), lambda i, j_ref, c_ref: (0, i, 0))],
            out_specs=pl.BlockSpec((None, tr, cols), lambda i, j_ref, c_ref: (c_ref[0], i, 0)),
        ),
        out_shape=jax.ShapeDtypeStruct((2, rows, cols), F32),
        compiler_params=_params(),
    )(chip_idx, c_idx, m2, got.reshape(3, rows, cols))
    return out.reshape((2, *shard_shape))


def _adamw(w, g, m, v, *, name):
    shape = w.shape
    rows, cols = _as_rows(shape)
    tr = _row_tile(rows, cols, 7)
    c1 = 1.0 / (1.0 - ADAM_B1 ** ADAM_STEP)
    c2 = 1.0 / (1.0 - ADAM_B2 ** ADAM_STEP)

    def body(w_ref, g_ref, m_ref, v_ref, d_ref, nm_ref, nv_ref):
        gg = g_ref[...]
        nm = ADAM_B1 * m_ref[...] + (1.0 - ADAM_B1) * gg
        nv = ADAM_B2 * v_ref[...] + (1.0 - ADAM_B2) * (gg * gg)
        nm_ref[...] = nm
        nv_ref[...] = nv
        d_ref[...] = -ADAM_LR * ((nm * c1) / (jnp.sqrt(nv * c2) + ADAM_EPS) + ADAM_WD * w_ref[...])

    blk = pl.BlockSpec((tr, cols), lambda i: (i, 0))
    flat = lambda a: a.reshape(rows, cols)
    outs = pl.pallas_call(
        body,
        name=name,
        grid=(rows // tr,),
        in_specs=[blk] * 4,
        out_specs=[blk] * 3,
        out_shape=[jax.ShapeDtypeStruct((rows, cols), F32)] * 3,
        compiler_params=_params(),
    )(flat(w), flat(g), flat(m), flat(v))
    return tuple(o.reshape(shape) for o in outs)


def _rows_blocked(a, t):
    return a.reshape(a.shape[0] // t, t, HEADS, HEAD_DIM).transpose(2, 0, 1, 3)


def _cols_blocked(a, t):
    return a.reshape(a.shape[0] // t, t, HEADS, HEAD_DIM).transpose(2, 0, 3, 1)


def _cols(a):
    return a.reshape(a.shape[0], HEADS, HEAD_DIM).transpose(1, 2, 0)


def _from_cols(a):
    h, d, s = a.shape
    return a.transpose(2, 0, 1).reshape(s, h * d)


def _from_cols_blocked(a):
    h, nb, d, t = a.shape
    return a.transpose(1, 3, 0, 2).reshape(nb * t, h * d)


def _pack(parts):
    flat = []
    for a in parts:
        v = a.reshape(-1)
        flat.append(jnp.pad(v, (0, (-v.shape[0]) % 1024)))
    return jnp.concatenate(flat).reshape(-1, 128)


def _unpack(packed, shapes):
    flat = packed.reshape(-1)
    out, off = [], 0
    for shp in shapes:
        size = math.prod(shp)
        out.append(flat[off:off + size].reshape(shp))
        off += size + (-size) % 1024
    return out


BIG = ("w_in", "w_branch_out", "w_o", "w_gate_up", "w_down")
BIG_AXIS = {"w_in": 1, "w_branch_out": 2, "w_o": 0, "w_gate_up": 1, "w_down": 0}
SMALL = ("mix_norm_g", "b_gate", "conv_w", "conv_b", "sgu_ln_g", "sgu_ln_b", "sgu_w", "sgu_b", "q_norm_g", "k_norm_g",
         "ffn_norm_g")
ORDER = ("mix_norm_g", "w_in", "b_gate", "conv_w", "conv_b", "sgu_ln_g", "sgu_ln_b", "sgu_w", "sgu_b", "q_norm_g",
         "k_norm_g", "w_branch_out", "w_o", "ffn_norm_g", "w_gate_up", "w_down")


def _layer_forward(x, w, l):
    t = ATT_BLOCK
    c = w["conv_b"].shape[1]
    n_in = w["w_in"].shape[2]
    gate_col0 = (n_in - 3 * x.shape[1]) // x.shape[1]
    tag = f"l{l}"
    p, h = _norm_matmul(x, w["mix_norm_g"][l][None], w["w_in"][l], name=f"in_proj_{tag}", tn=n_in // 8)
    ya = _conv_fwd(p, w["conv_w"][l], w["conv_b"][l][None], name=f"conv_{tag}")
    yb = _sgu_fwd(p, w["sgu_ln_g"][l][None], w["sgu_ln_b"][l][None], w["wtril"][l], w["bias_full"][l], col0=3,
                  name=f"sgu_{tag}")
    qn, kn, vb = _qkv_prep(p, w["qg"][l], w["kg"][l], w["headavg"], col0=5, name=f"qkv_{tag}")
    q_c = _cols(qn)
    k_rb, k_cb = _rows_blocked(kn, t), _cols_blocked(kn, t)
    v_rb, v_cb = _rows_blocked(vb, t), _cols_blocked(vb, t)
    out_t = _att_fwd(k_rb, q_c, v_cb, w["later"], name=f"att_{tag}")
    yc = _from_cols(out_t)
    x1, merged = _merge_fwd((ya, yb, yc), p, w["b_gate"][l][None], w["w_branch_out"][l], w["w_o"][l], x,
                            gate_col0=gate_col0, name=f"merge_{tag}")
    gu, h2 = _norm_matmul(x1, w["ffn_norm_g"][l][None], w["w_gate_up"][l], name=f"gate_up_{tag}",
                          tn=w["w_gate_up"].shape[2] // 4)
    x2 = _ffn_down(gu, w["w_down"][l], x1, name=f"down_{tag}")
    saved = dict(x=x, p=p, h=h, ya=ya, yb=yb, yc=yc, merged=merged, x1=x1, gu=gu, h2=h2,
                 att=(k_rb, k_cb, v_rb, q_c), gate_col0=gate_col0)
    return x2, saved


def _layer_backward(dx2, w, sv, l):
    t = ATT_BLOCK
    tag = f"l{l}"
    d = dx2.shape[1]
    c = w["conv_b"].shape[1]
    n_in = w["w_in"].shape[2]
    n_ff = w["w_gate_up"].shape[2]
    g = {}
    dgu, act, dx2b = _ffn_bwd(dx2, sv["gu"], w["w_down"][l], name=f"down_bwd_{tag}")
    g["w_down"] = _matmul_tn(act, dx2b, name=f"dw_down_{tag}", t1=act.shape[1] // 2)
    dx1, dg2 = _matmul_nt_normbwd(dgu, w["w_gate_up"][l], sv["x1"], w["ffn_norm_g"][l][None], dx2,
                                  name=f"gate_up_bwd_{tag}", tk=n_ff // 4)
    g["ffn_norm_g"] = jnp.sum(dg2, axis=0)
    g["w_gate_up"] = _matmul_tn(sv["h2"], dgu, name=f"dw_gate_up_{tag}", tn=n_ff // 4)
    ys = (sv["ya"], sv["yb"], sv["yc"])
    (dgates, dya, dyb, dyc, dd0, dd1, dd2, dx1b, dbg) = _merge_bwd(
        dx1, ys, sv["p"], w["b_gate"][l][None], w["w_branch_out"][l], w["w_o"][l], gate_col0=sv["gate_col0"],
        name=f"merge_bwd_{tag}")
    g["b_gate"] = jnp.sum(dbg, axis=0)
    g["w_o"] = _matmul_tn(sv["merged"], dx1b, name=f"dw_o_{tag}")
    g["w_branch_out"] = jnp.stack([
        _matmul_tn(y, dd, name=f"dw_bo{i}_{tag}") for i, (y, dd) in enumerate(zip(ys, (dd0, dd1, dd2)))])
    dconv, dwc = _conv_bwd(sv["p"], dya, w["conv_w"][l], w["conv_b"][l][None], name=f"conv_bwd_{tag}")
    dwc = jnp.sum(dwc, axis=1)
    g["conv_w"] = dwc[0:CONV_K]
    g["conv_b"] = dwc[CONV_K]
    dsu, dsv, dln, dws, dbias = _sgu_bwd(sv["p"], dyb, w["sgu_ln_g"][l][None], w["sgu_ln_b"][l][None], w["wtril"][l],
                                         w["wtril_t"][l], w["bias_full"][l], col0=3, name=f"sgu_bwd_{tag}")
    dln = jnp.sum(dln, axis=1)
    g["sgu_ln_g"], g["sgu_ln_b"] = dln[0], dln[1]
    g["sgu_w"] = jnp.where(w["tril"], dws, 0.0)
    g["sgu_b"] = jnp.sum(dbias.reshape(CHUNK, c // CHUNK, CHUNK), axis=2).T
    k_rb, k_cb, v_rb, q_c = sv["att"]
    dqt, dkt, dvt = _att_bwd(k_rb, k_cb, v_rb, q_c, _cols(_mx(dyc)), w["later"], w["earlier"], name=f"att_bwd_{tag}")
    dqn, dkn, dvv = _from_cols(dqt), _from_cols_blocked(dkt), _from_cols_blocked(dvt)
    dq, dk_, dvb, dqkg = _qkv_bwd(sv["p"], dqn, dkn, dvv, w["qg"][l], w["kg"][l], w["headavg"], col0=5,
                                  name=f"qkv_bwd_{tag}")
    dqkg = jnp.sum(dqkg.reshape(2, 8 * HEADS, HEAD_DIM), axis=1)
    g["q_norm_g"], g["k_norm_g"] = dqkg[0], dqkg[1]
    dp = jnp.concatenate([dconv, dsu, dsv, dq, dk_, dvb, dgates], axis=1)
    dx0, dg1 = _matmul_nt_normbwd(dp, w["w_in"][l], sv["x"], w["mix_norm_g"][l][None], dx1, name=f"in_proj_bwd_{tag}",
                                  tk=n_in // 4)
    g["mix_norm_g"] = jnp.sum(dg1, axis=0)
    g["w_in"] = _matmul_tn(sv["h"], dp, name=f"dw_in_{tag}", tn=n_in // 4)
    return dx0, g


def kernel(x, mix_norm_g, w_in, b_gate, conv_w, conv_b, sgu_ln_g, sgu_ln_b, sgu_w, sgu_b, q_norm_g, k_norm_g, w_branch_out, w_o, ffn_norm_g, w_gate_up, w_down, loss_target, m_mix_norm_g, m_w_in, m_b_gate, m_conv_w, m_conv_b, m_sgu_ln_g, m_sgu_ln_b, m_sgu_w, m_sgu_b, m_q_norm_g, m_k_norm_g, m_w_branch_out, m_w_o, m_ffn_norm_g, m_w_gate_up, m_w_down, v_mix_norm_g, v_w_in, v_b_gate, v_conv_w, v_conv_b, v_sgu_ln_g, v_sgu_ln_b, v_sgu_w, v_sgu_b, v_q_norm_g, v_k_norm_g, v_w_branch_out, v_w_o, v_ffn_norm_g, v_w_gate_up, v_w_down):
    given = dict(locals())
    params = {n: given[n] for n in ORDER}
    moms = {n: (given["m_" + n], given["v_" + n]) for n in ORDER}
    layers = mix_norm_g.shape[0]
    assert layers == 2, "the exchanges split the work of a chip's two cores by layer"
    xs = x[0]
    target = loss_target[0]
    chip = 2 * lax.axis_index("x") + lax.axis_index("y")
    core = lax.axis_index("c")

    c_idx = core.reshape(1).astype(jnp.int32)
    j_idx = chip.reshape(1).astype(jnp.int32)
    width = conv_w.shape[2]
    placed = [_place_shard(params[n], BIG_AXIS[n], j_idx, MXU_DTYPE, name=f"place_{n}") for n in BIG]
    placed.append(lax.dynamic_update_slice_in_dim(jnp.zeros((layers, CONV_K, width * N_CHIPS), F32), conv_w, chip * width, 2))
    gathered = _gather_weights(placed, [BIG_AXIS[n] for n in BIG] + [1], name="gather_weights")
    w = dict(zip(BIG + ("conv_w",), gathered))
    for n in ("mix_norm_g", "b_gate", "conv_b", "sgu_ln_g", "sgu_ln_b", "ffn_norm_g"):
        w[n] = params[n]
    groups = sgu_w.shape[1]
    tril = jnp.tril(jnp.ones((CHUNK, CHUNK), dtype=bool))
    w["tril"] = tril
    w["wtril"] = _mx(jnp.where(tril, sgu_w, 0.0))
    w["wtril_t"] = w["wtril"].transpose(0, 1, 3, 2)
    w["bias_full"] = jnp.repeat(sgu_b.transpose(0, 2, 1), CHUNK, axis=2)
    w["qg"] = jnp.tile(q_norm_g, (1, HEADS))[:, None, :]
    w["kg"] = jnp.tile(k_norm_g, (1, HEADS))[:, None, :]
    lane = jnp.arange(HEADS * HEAD_DIM) // HEAD_DIM
    w["headavg"] = _mx(jnp.where(lane[:, None] == lane[None, :], 1.0 / HEAD_DIM, 0.0))
    pos = jnp.arange(ATT_BLOCK)
    w["later"] = _mx(jnp.where(pos[None, :] > pos[:, None], 1.0, 0.0))
    w["earlier"] = _mx(jnp.where(pos[None, :] < pos[:, None], 1.0, 0.0))

    saved = []
    act = xs
    for l in range(layers):
        act, sv = _layer_forward(act, w, l)
        saved.append(sv)
    loss_part, dact = _loss_head(act, target, name="loss_head")
    loss = lax.psum(jnp.sum(loss_part), ("x", "y", "c"))
    grads = [None] * layers
    for l in reversed(range(layers)):
        dact, grads[l] = _layer_backward(dact, w, saved[l], l)
    grad_x = dact[None]
    local = {n: jnp.stack([grads[l][n] for l in range(layers)]) for n in ORDER}

    got = _pair_exchange([local[n] for n in BIG], name="grads_pair_exchange")
    pair = [_pair_sum(local[n], o, c_idx, name=f"pair_sum_{n}") for n, o in zip(BIG, got)]
    landed = _chip_scatter(pair, [BIG_AXIS[n] for n in BIG], name="grads_chip_scatter")
    final = [_chip_sum(pr, ld, BIG_AXIS[n], j_idx, c_idx, name=f"chip_sum_{n}") for n, pr, ld in zip(BIG, pair, landed)]
    full = dict(zip(BIG, _pair_share(final, name="grads_pair_share")))

    small_shapes = [local[n].shape for n in SMALL]
    summed = _unpack(_all_reduce_small(_pack([local[n] for n in SMALL]), name="grads_all_reduce_small"), small_shapes)
    for n, gsum in zip(SMALL, summed):
        full[n] = gsum
    full["conv_w"] = lax.dynamic_slice_in_dim(full["conv_w"], chip * width, width, axis=2)

    out = {}
    for n in BIG:
        out[n] = _adamw(params[n], full[n], *moms[n], name=f"adamw_{n}")
    shapes = [params[n].shape for n in SMALL]
    packed = [_pack([src(n) for n in SMALL]) for src in
              (lambda n: params[n], lambda n: full[n], lambda n: moms[n][0], lambda n: moms[n][1])]
    for n, dl, nm, nv in zip(SMALL, *[_unpack(o, shapes) for o in _adamw(*packed, name="adamw_small")]):
        out[n] = (dl, nm, nv)
    return (loss, grad_x, *[full[n] for n in ORDER], *[out[n][0] for n in ORDER], *[out[n][1] for n in ORDER],
            *[out[n][2] for n in ORDER])
```

```python
import functools
import math

import jax
import jax.numpy as jnp
from jax import lax
from jax.experimental import pallas as pl
from jax.experimental.pallas import tpu as pltpu

F32 = jnp.float32
MXU_DTYPE = jnp.bfloat16
WIRE_DTYPE = jnp.bfloat16

EPS = 1e-6
CONV_K = 3
CHUNK = 128
HEADS = 8
HEAD_DIM = 64
ATT_BLOCK = 128
EXP_UNDERFLOW = -104.0

ADAM_LR = 0.001
ADAM_B1 = 0.9
ADAM_B2 = 0.999
ADAM_EPS = 1e-08
ADAM_WD = 0.01
ADAM_STEP = 10

VMEM_LIMIT = 56 * 1024 * 1024
MESH = pl.DeviceIdType.MESH
N_CHIPS = 4
N_DEV = 8
ANY = pl.BlockSpec(memory_space=pl.ANY)


def _params(**kw):
    return pltpu.CompilerParams(vmem_limit_bytes=VMEM_LIMIT, **kw)


def _mx(v):
    return v.astype(MXU_DTYPE)


def _dot(a, b):
    return lax.dot_general(a, b, (((1,), (0,)), ((), ())), preferred_element_type=F32)


def _dot_nt(a, b):
    return lax.dot_general(a, b, (((1,), (1,)), ((), ())), preferred_element_type=F32)


def _dot_tn(a, b):
    return lax.dot_general(a, b, (((0,), (0,)), ((), ())), preferred_element_type=F32)


def _dot_split(const, v):
    hi = _mx(v)
    lo = _mx(v - hi.astype(F32))
    return _dot(const, hi) + _dot(const, lo)


def _dot_split_r(v, const):
    hi = _mx(v)
    lo = _mx(v - hi.astype(F32))
    return _dot(hi, const) + _dot(lo, const)


def _sigmoid(x):
    return 1.0 / (1.0 + jnp.exp(-x))


_INV_SQRT2 = 1.0 / math.sqrt(2.0)
_INV_SQRT2PI = 1.0 / math.sqrt(2.0 * math.pi)


def _gelu(x):
    return 0.5 * x * (1.0 + lax.erf(x * _INV_SQRT2))


def _gelu_grad(x):
    return 0.5 * (1.0 + lax.erf(x * _INV_SQRT2)) + x * jnp.exp(-0.5 * x * x) * _INV_SQRT2PI


def _row_fold(v):
    m, n = v.shape
    return jnp.sum(v.reshape(m // 8, 8, n), axis=0)


def _tile(m, pref):
    t = min(m, pref)
    while m % t:
        t //= 2
    return t


def _norm_matmul(x, g, w, *, name, tm=1024, tn=None):
    s, d = x.shape
    n = w.shape[1]
    tm = _tile(s, tm)
    tn = tn or n
    assert n % tn == 0

    def body(x_ref, g_ref, w_ref, p_ref, h_ref):
        @pl.when(pl.program_id(1) == 0)
        def _():
            xf = x_ref[...]
            r = lax.rsqrt(jnp.mean(xf * xf, axis=-1, keepdims=True) + EPS)
            h_ref[...] = _mx(xf * r * g_ref[...])

        p_ref[...] = _dot(h_ref[...], w_ref[...])

    return pl.pallas_call(
        body,
        name=name,
        grid=(s // tm, n // tn),
        in_specs=[
            pl.BlockSpec((tm, d), lambda i, j: (i, 0)),
            pl.BlockSpec((1, d), lambda i, j: (0, 0)),
            pl.BlockSpec((d, tn), lambda i, j: (0, j)),
        ],
        out_specs=[
            pl.BlockSpec((tm, tn), lambda i, j: (i, j)),
            pl.BlockSpec((tm, d), lambda i, j: (i, 0)),
        ],
        out_shape=[jax.ShapeDtypeStruct((s, n), F32), jax.ShapeDtypeStruct((s, d), MXU_DTYPE)],
        compiler_params=_params(),
    )(x, g, w)


def _conv_taps(u, prev):
    row = lax.broadcasted_iota(jnp.int32, u.shape, 0)
    um1 = jnp.where(row == 0, prev[7:8, :], pltpu.roll(u, 1, 0))
    um2 = pltpu.roll(u, 2, 0)
    um2 = jnp.where(row == 0, prev[6:7, :], jnp.where(row == 1, prev[7:8, :], um2))
    return um1, um2


def _conv_fwd(p, conv_w, conv_b, *, name, tm=512):
    s = p.shape[0]
    c = conv_w.shape[1]
    tm = _tile(s, tm)
    hb = tm // 8

    def body(bg_ref, cg_ref, xa_ref, cgp_ref, xap_ref, w_ref, b_ref, y_ref):
        first = pl.program_id(0) == 0
        u = cg_ref[...] * xa_ref[...]
        prev = jnp.where(first, 0.0, cgp_ref[...] * xap_ref[...])
        um1, um2 = _conv_taps(u, prev)
        w = w_ref[...]
        y = b_ref[...] + w[0:1, :] * um2 + w[1:2, :] * um1 + w[2:3, :] * u
        y_ref[...] = _mx(bg_ref[...] * y)

    halo = lambda col: pl.BlockSpec((8, c), lambda i: (jnp.maximum(i * hb - 1, 0), col))
    return pl.pallas_call(
        body,
        name=name,
        grid=(s // tm,),
        in_specs=[
            pl.BlockSpec((tm, c), lambda i: (i, 0)),
            pl.BlockSpec((tm, c), lambda i: (i, 1)),
            pl.BlockSpec((tm, c), lambda i: (i, 2)),
            halo(1),
            halo(2),
            pl.BlockSpec((CONV_K, c), lambda i: (0, 0)),
            pl.BlockSpec((1, c), lambda i: (0, 0)),
        ],
        out_specs=pl.BlockSpec((tm, c), lambda i: (i, 0)),
        out_shape=jax.ShapeDtypeStruct((s, c), MXU_DTYPE),
        compiler_params=_params(),
    )(p, p, p, p, p, conv_w, conv_b)


def _layernorm_stats(x):
    mu = jnp.mean(x, axis=-1, keepdims=True)
    xc = x - mu
    r = lax.rsqrt(jnp.mean(xc * xc, axis=-1, keepdims=True) + EPS)
    return xc * r, r


def _sgu_fwd(p, ln_g, ln_b, wtril, bias_full, *, col0, name, tm=512):
    s = p.shape[0]
    c = ln_g.shape[1]
    groups = c // CHUNK
    tm = _tile(s, tm)

    def body(u_ref, v_ref, g_ref, b_ref, w_ref, bias_ref, y_ref):
        vn, _ = _layernorm_stats(_gelu(v_ref[...]))
        vb = _mx(vn * g_ref[...] + b_ref[...])
        for n in range(tm // CHUNK):
            rows = slice(n * CHUNK, (n + 1) * CHUNK)
            for gi in range(groups):
                cols = slice(gi * CHUNK, (gi + 1) * CHUNK)
                mixed = _dot(w_ref[gi], vb[rows, cols]) + bias_ref[:, cols]
                y_ref[rows, cols] = _mx(_gelu(u_ref[rows, cols]) * mixed)

    return pl.pallas_call(
        body,
        name=name,
        grid=(s // tm,),
        in_specs=[
            pl.BlockSpec((tm, c), lambda i: (i, col0)),
            pl.BlockSpec((tm, c), lambda i: (i, col0 + 1)),
            pl.BlockSpec((1, c), lambda i: (0, 0)),
            pl.BlockSpec((1, c), lambda i: (0, 0)),
            pl.BlockSpec((groups, CHUNK, CHUNK), lambda i: (0, 0, 0)),
            pl.BlockSpec((CHUNK, c), lambda i: (0, 0)),
        ],
        out_specs=pl.BlockSpec((tm, c), lambda i: (i, 0)),
        out_shape=jax.ShapeDtypeStruct((s, c), MXU_DTYPE),
        compiler_params=_params(),
    )(p, p, ln_g, ln_b, wtril, bias_full)


def _head_mean(v, headavg):
    return _dot_split_r(v, headavg)


def _qkv_prep(p, qg, kg, headavg, *, col0, name, tm=512):
    s = p.shape[0]
    c = qg.shape[1]
    tm = _tile(s, tm)

    def body(q_ref, k_ref, v_ref, qg_ref, kg_ref, avg_ref, qn_ref, kn_ref, vb_ref):
        for src, g_ref, dst in ((q_ref, qg_ref, qn_ref), (k_ref, kg_ref, kn_ref)):
            xf = src[...]
            r = lax.rsqrt(_head_mean(xf * xf, avg_ref[...]) + EPS)
            dst[...] = _mx(xf * r * g_ref[...])
        vb_ref[...] = _mx(v_ref[...])

    blk = lambda col: pl.BlockSpec((tm, c), lambda i: (i, col))
    vec = pl.BlockSpec((1, c), lambda i: (0, 0))
    out = pl.BlockSpec((tm, c), lambda i: (i, 0))
    return pl.pallas_call(
        body,
        name=name,
        grid=(s // tm,),
        in_specs=[blk(col0), blk(col0 + 1), blk(col0 + 2), vec, vec, pl.BlockSpec((c, c), lambda i: (0, 0))],
        out_specs=[out, out, out],
        out_shape=[jax.ShapeDtypeStruct((s, c), MXU_DTYPE)] * 3,
        compiler_params=_params(),
    )(p, p, p, qg, kg, headavg)


def _att_mask(t, key0, qry0):
    kpos = key0 + lax.broadcasted_iota(jnp.int32, (t, t), 0)
    qpos = qry0 + lax.broadcasted_iota(jnp.int32, (t, t), 1)
    return kpos < qpos


def _att_blocks(k_blks, q_ts, scale, mask):
    zs = [_dot(k, q) * scale for k, q in zip(k_blks, q_ts)]
    es = [jnp.exp(-jnp.abs(z)) for z in zs]
    lbs = [jnp.minimum(z, 0.0) - jnp.log1p(e) for z, e in zip(zs, es)]
    l1ms = [jnp.where(mask, lb - z, 0.0) for lb, z in zip(lbs, zs)]
    return zs, es, lbs, l1ms


def _dot_split_each(const, vs):
    his = [_mx(v) for v in vs]
    los = [_mx(v - hi.astype(F32)) for v, hi in zip(vs, his)]
    tops = [_dot(const, hi) for hi in his]
    return [top + _dot(const, lo) for top, lo in zip(tops, los)]


def _max_over(runs):
    m = runs[0]
    for r in runs[1:]:
        m = jnp.maximum(m, r)
    return jnp.max(m)


NOT_SEEN = -1e30


def _row_put(ref, g, j, row):
    j8 = pl.multiple_of((j // 8) * 8, 8)
    blk = ref[g, pl.ds(j8, 8), :]
    sub = lax.broadcasted_iota(jnp.int32, blk.shape, 0)
    ref[g, pl.ds(j8, 8), :] = jnp.where(sub == j - j8, row, blk)


def _row_get(ref, g, j):
    j8 = pl.multiple_of((j // 8) * 8, 8)
    blk = ref[g, pl.ds(j8, 8), :]
    sub = lax.broadcasted_iota(jnp.int32, blk.shape, 0)
    return jnp.sum(jnp.where(sub == j - j8, blk, 0.0), axis=0, keepdims=True)


def _att_fwd(k_hm, qt_hm, vt_hm, later, *, name, hg=4):
    h, nb, t, d = k_hm.shape
    s = nb * t
    nbp = -(-nb // 8) * 8
    scale = 1.0 / math.sqrt(d)
    assert h % hg == 0

    def body(k_ref, qt_ref, vt_ref, later_ref, o_ref, runs_ref):
        i = pl.program_id(1)
        q_ts = [qt_ref[g] for g in range(hg)]
        runs_ref[...] = jnp.full(runs_ref.shape, NOT_SEEN, F32)

        def cond(carry):
            j, _, _, rmax = carry
            return jnp.logical_and(j >= 0, rmax > EXP_UNDERFLOW)

        def step(carry):
            j, runs, accs, _ = carry
            mask = _att_mask(t, j * t, i * t)
            heads = range(hg)
            for g in heads:
                _row_put(runs_ref, g, j, runs[g])
            _, _, lbs, l1ms = _att_blocks([k_ref[g, j] for g in heads], q_ts, scale, mask)
            afters = _dot_split_each(later_ref[...], l1ms)
            weights = [_mx(jnp.where(mask, jnp.exp(lbs[g] + afters[g] + runs[g]), 0.0)) for g in heads]
            new_accs = [accs[g] + _dot(vt_ref[g, j], weights[g]) for g in heads]
            new_runs = [runs[g] + jnp.sum(l1ms[g], axis=0, keepdims=True) for g in heads]
            return j - 1, tuple(new_runs), tuple(new_accs), _max_over(new_runs)

        init = (i, tuple(jnp.zeros((1, t), F32) for _ in range(hg)), tuple(jnp.zeros((d, t), F32) for _ in range(hg)),
                jnp.float32(0.0))
        _, _, accs, _ = lax.while_loop(cond, step, init)
        for g in range(hg):
            o_ref[g] = _mx(accs[g])

    return pl.pallas_call(
        body,
        name=name,
        grid=(h // hg, nb),
        in_specs=[
            pl.BlockSpec((hg, nb, t, d), lambda hh, i: (hh, 0, 0, 0)),
            pl.BlockSpec((hg, d, t), lambda hh, i: (hh, 0, i)),
            pl.BlockSpec((hg, nb, d, t), lambda hh, i: (hh, 0, 0, 0)),
            pl.BlockSpec((t, t), lambda hh, i: (0, 0)),
        ],
        out_specs=[pl.BlockSpec((hg, d, t), lambda hh, i: (hh, 0, i)),
                   pl.BlockSpec((hg, None, nbp, t), lambda hh, i: (hh, i, 0, 0))],
        out_shape=[jax.ShapeDtypeStruct((h, d, s), MXU_DTYPE), jax.ShapeDtypeStruct((h, nb, nbp, t), F32)],
        compiler_params=_params(),
    )(k_hm, qt_hm, vt_hm, later)


def _merge_fwd(ys, p, b_gate, w_bo, w_o, x, *, gate_col0, name, tm=256):
    s, d = x.shape
    nbr, c, _ = w_bo.shape
    tm = _tile(s, tm)

    def body(ya_ref, yb_ref, yc_ref, g0_ref, g1_ref, g2_ref, bg_ref, wbo_ref, wo_ref, x_ref, x1_ref, m_ref):
        merged = jnp.zeros((tm, d), F32)
        for i, (y_ref, g_ref) in enumerate(((ya_ref, g0_ref), (yb_ref, g1_ref), (yc_ref, g2_ref))):
            gate = _sigmoid(g_ref[...] + bg_ref[:, i * d:(i + 1) * d])
            merged = merged + gate * _dot(y_ref[...], wbo_ref[i])
        mb = _mx(merged)
        m_ref[...] = mb
        x1_ref[...] = x_ref[...] + _dot(mb, wo_ref[...])

    yblk = pl.BlockSpec((tm, c), lambda i: (i, 0))
    gblk = lambda k: pl.BlockSpec((tm, d), lambda i: (i, gate_col0 + k))
    xblk = pl.BlockSpec((tm, d), lambda i: (i, 0))
    return pl.pallas_call(
        body,
        name=name,
        grid=(s // tm,),
        in_specs=[
            yblk, yblk, yblk, gblk(0), gblk(1), gblk(2),
            pl.BlockSpec((1, nbr * d), lambda i: (0, 0)),
            pl.BlockSpec((nbr, c, d), lambda i: (0, 0, 0)),
            pl.BlockSpec((d, d), lambda i: (0, 0)),
            xblk,
        ],
        out_specs=[xblk, xblk],
        out_shape=[jax.ShapeDtypeStruct((s, d), F32), jax.ShapeDtypeStruct((s, d), MXU_DTYPE)],
        compiler_params=_params(),
    )(*ys, p, p, p, b_gate, w_bo, w_o, x)


def _ffn_down(gu, w_down, x, *, name, tm=256):
    s, d = x.shape
    f = w_down.shape[0]
    tm = _tile(s, tm)

    def body(g_ref, u_ref, w_ref, x_ref, o_ref):
        g = g_ref[...]
        act = _mx(g * _sigmoid(g) * u_ref[...])
        o_ref[...] = x_ref[...] + _dot(act, w_ref[...])

    return pl.pallas_call(
        body,
        name=name,
        grid=(s // tm,),
        in_specs=[
            pl.BlockSpec((tm, f), lambda i: (i, 0)),
            pl.BlockSpec((tm, f), lambda i: (i, 1)),
            pl.BlockSpec((f, d), lambda i: (0, 0)),
            pl.BlockSpec((tm, d), lambda i: (i, 0)),
        ],
        out_specs=pl.BlockSpec((tm, d), lambda i: (i, 0)),
        out_shape=jax.ShapeDtypeStruct((s, d), F32),
        compiler_params=_params(),
    )(gu, gu, w_down, x)


def _loss_head(y, target, *, name, tm=512):
    s, d = y.shape
    tm = _tile(s, tm)

    def body(y_ref, t_ref, l_ref, dy_ref):
        @pl.when(pl.program_id(0) == 0)
        def _():
            l_ref[...] = jnp.zeros_like(l_ref)

        err = y_ref[...] - t_ref[...]
        dy_ref[...] = err * (1.0 / d)
        sq = _row_fold(err * err)
        part = sq[:, 0:128]
        for k in range(1, d // 128):
            part = part + sq[:, k * 128:(k + 1) * 128]
        l_ref[...] += part * (0.5 / d)

    blk = pl.BlockSpec((tm, d), lambda i: (i, 0))
    return pl.pallas_call(
        body,
        name=name,
        grid=(s // tm,),
        in_specs=[blk, blk],
        out_specs=[pl.BlockSpec((8, 128), lambda i: (0, 0)), blk],
        out_shape=[jax.ShapeDtypeStruct((8, 128), F32), jax.ShapeDtypeStruct((s, d), F32)],
        compiler_params=_params(),
    )(y, target)


def _matmul_tn(a, b, *, name, t1=1024, tn=None, tm=1024):
    m, k1 = a.shape
    n = b.shape[1]
    t1 = _tile(k1, t1)
    tn = tn or n
    tm = _tile(m, tm)
    steps = m // tm
    assert n % tn == 0

    def body(a_ref, b_ref, o_ref):
        @pl.when(pl.program_id(2) == 0)
        def _():
            o_ref[...] = jnp.zeros_like(o_ref)

        o_ref[...] += _dot_tn(a_ref[...], b_ref[...])

    return pl.pallas_call(
        body,
        name=name,
        grid=(k1 // t1, n // tn, steps),
        in_specs=[
            pl.BlockSpec((tm, t1), lambda i, j, k: (k, i)),
            pl.BlockSpec((tm, tn), lambda i, j, k: (k, j)),
        ],
        out_specs=pl.BlockSpec((t1, tn), lambda i, j, k: (i, j)),
        out_shape=jax.ShapeDtypeStruct((k1, n), F32),
        compiler_params=_params(),
    )(a, b)


def _matmul_nt_normbwd(dp, w, x, g, dres, *, name, tm=1024, tk=None):
    s, d = x.shape
    k = w.shape[1]
    tm = _tile(s, tm)
    tk = tk or k
    steps = k // tk
    assert k % tk == 0

    def body(dp_ref, w_ref, x_ref, g_ref, dres_ref, dx_ref, dg_ref, acc_ref):
        kk = pl.program_id(1)

        @pl.when(jnp.logical_and(pl.program_id(0) == 0, kk == 0))
        def _():
            dg_ref[...] = jnp.zeros_like(dg_ref)

        @pl.when(kk == 0)
        def _():
            acc_ref[...] = jnp.zeros_like(acc_ref)

        acc_ref[...] += _dot_nt(dp_ref[...], w_ref[...])

        @pl.when(kk == steps - 1)
        def _():
            dh = acc_ref[...]
            xf = x_ref[...]
            r = lax.rsqrt(jnp.mean(xf * xf, axis=-1, keepdims=True) + EPS)
            y = xf * r
            dy = dh * g_ref[...]
            dx_ref[...] = dres_ref[...] + r * (dy - y * jnp.mean(dy * y, axis=-1, keepdims=True))
            dg_ref[...] += _row_fold(dh * y)

    xblk = pl.BlockSpec((tm, d), lambda i, j: (i, 0))
    return pl.pallas_call(
        body,
        name=name,
        grid=(s // tm, steps),
        in_specs=[
            pl.BlockSpec((tm, tk), lambda i, j: (i, j)),
            pl.BlockSpec((d, tk), lambda i, j: (0, j)),
            xblk,
            pl.BlockSpec((1, d), lambda i, j: (0, 0)),
            xblk,
        ],
        out_specs=[xblk, pl.BlockSpec((8, d), lambda i, j: (0, 0))],
        out_shape=[jax.ShapeDtypeStruct((s, d), F32), jax.ShapeDtypeStruct((8, d), F32)],
        scratch_shapes=[pltpu.VMEM((tm, d), F32)],
        compiler_params=_params(),
    )(dp, w, x, g, dres)


def _ffn_bwd(dx, gu, w_down, *, name, tm=256):
    s, d = dx.shape
    f = w_down.shape[0]
    tm = _tile(s, tm)

    def body(dx_ref, g_ref, u_ref, w_ref, dgu_ref, act_ref, dxb_ref):
        dxb = _mx(dx_ref[...])
        dxb_ref[...] = dxb
        dact = _dot_nt(dxb, w_ref[...])
        g = g_ref[...]
        u = u_ref[...]
        sg = _sigmoid(g)
        silu = g * sg
        act_ref[...] = _mx(silu * u)
        dgu_ref[:, 0:f] = _mx(dact * u * (sg * (1.0 + g * (1.0 - sg))))
        dgu_ref[:, f:2 * f] = _mx(dact * silu)

    fblk = lambda col: pl.BlockSpec((tm, f), lambda i: (i, col))
    dblk = pl.BlockSpec((tm, d), lambda i: (i, 0))
    return pl.pallas_call(
        body,
        name=name,
        grid=(s // tm,),
        in_specs=[dblk, fblk(0), fblk(1), pl.BlockSpec((f, d), lambda i: (0, 0))],
        out_specs=[pl.BlockSpec((tm, 2 * f), lambda i: (i, 0)), fblk(0), dblk],
        out_shape=[
            jax.ShapeDtypeStruct((s, 2 * f), MXU_DTYPE),
            jax.ShapeDtypeStruct((s, f), MXU_DTYPE),
            jax.ShapeDtypeStruct((s, d), MXU_DTYPE),
        ],
        compiler_params=_params(),
    )(dx, gu, gu, w_down)


def _merge_bwd(dx, ys, p, b_gate, w_bo, w_o, *, gate_col0, name, tm=256):
    s, d = dx.shape
    nbr, c, _ = w_bo.shape
    tm = _tile(s, tm)

    def body(dx_ref, ya_ref, yb_ref, yc_ref, g0_ref, g1_ref, g2_ref, bg_ref, wbo_ref, wo_ref,
             dgate_ref, dya_ref, dyb_ref, dyc_ref, dd0_ref, dd1_ref, dd2_ref, dxb_ref, dbg_ref):
        @pl.when(pl.program_id(0) == 0)
        def _():
            dbg_ref[...] = jnp.zeros_like(dbg_ref)

        dxb = _mx(dx_ref[...])
        dxb_ref[...] = dxb
        dmerged = _dot_nt(dxb, wo_ref[...])
        branches = ((ya_ref, g0_ref, dya_ref, dd0_ref), (yb_ref, g1_ref, dyb_ref, dd1_ref), (yc_ref, g2_ref, dyc_ref, dd2_ref))
        for i, (y_ref, g_ref, dy_ref, dd_ref) in enumerate(branches):
            cols = slice(i * d, (i + 1) * d)
            gate = _sigmoid(g_ref[...] + bg_ref[:, cols])
            yd = _dot(y_ref[...], wbo_ref[i])
            dyd = _mx(dmerged * gate)
            dd_ref[...] = dyd
            dy_ref[...] = _dot_nt(dyd, wbo_ref[i])
            dpre = dmerged * yd * gate * (1.0 - gate)
            dgate_ref[:, cols] = _mx(dpre)
            dbg_ref[:, cols] += _row_fold(dpre)

    yblk = pl.BlockSpec((tm, c), lambda i: (i, 0))
    gblk = lambda k: pl.BlockSpec((tm, d), lambda i: (i, gate_col0 + k))
    dblk = pl.BlockSpec((tm, d), lambda i: (i, 0))
    return pl.pallas_call(
        body,
        name=name,
        grid=(s // tm,),
        in_specs=[
            dblk, yblk, yblk, yblk, gblk(0), gblk(1), gblk(2),
            pl.BlockSpec((1, nbr * d), lambda i: (0, 0)),
            pl.BlockSpec((nbr, c, d), lambda i: (0, 0, 0)),
            pl.BlockSpec((d, d), lambda i: (0, 0)),
        ],
        out_specs=[pl.BlockSpec((tm, nbr * d), lambda i: (i, 0)), yblk, yblk, yblk, dblk, dblk, dblk, dblk,
                   pl.BlockSpec((8, nbr * d), lambda i: (0, 0))],
        out_shape=[jax.ShapeDtypeStruct((s, nbr * d), MXU_DTYPE)] + [jax.ShapeDtypeStruct((s, c), F32)] * 3
        + [jax.ShapeDtypeStruct((s, d), MXU_DTYPE)] * 4 + [jax.ShapeDtypeStruct((8, nbr * d), F32)],
        compiler_params=_params(),
    )(dx, *ys, p, p, p, b_gate, w_bo, w_o)


def _conv_bwd(p, dya, conv_w, conv_b, *, name, tm=512):
    s = p.shape[0]
    c = conv_w.shape[1]
    tm = _tile(s, tm)
    hb = tm // 8
    last = s // tm - 1

    def body(bg_ref, cg_ref, xa_ref, cgp_ref, xap_ref, dy_ref, dyn_ref, bgn_ref, w_ref, b_ref,
             dp_ref, dw_ref):
        i = pl.program_id(0)

        @pl.when(i == 0)
        def _():
            dw_ref[...] = jnp.zeros_like(dw_ref)

        cg = cg_ref[...]
        xa = xa_ref[...]
        u = cg * xa
        prev = jnp.where(i == 0, 0.0, cgp_ref[...] * xap_ref[...])
        um1, um2 = _conv_taps(u, prev)
        w = w_ref[...]
        y = b_ref[...] + w[0:1, :] * um2 + w[1:2, :] * um1 + w[2:3, :] * u
        dya = dy_ref[...]
        dp_ref[:, 0:c] = _mx(dya * y)
        dyv = dya * bg_ref[...]
        nxt = jnp.where(i == last, 0.0, dyn_ref[...] * bgn_ref[...])
        row = lax.broadcasted_iota(jnp.int32, dyv.shape, 0)
        dp1 = jnp.where(row == tm - 1, nxt[0:1, :], pltpu.roll(dyv, tm - 1, 0))
        dp2 = pltpu.roll(dyv, tm - 2, 0)
        dp2 = jnp.where(row == tm - 2, nxt[0:1, :], jnp.where(row == tm - 1, nxt[1:2, :], dp2))
        du = w[2:3, :] * dyv + w[1:2, :] * dp1 + w[0:1, :] * dp2
        dp_ref[:, c:2 * c] = _mx(du * xa)
        dp_ref[:, 2 * c:3 * c] = _mx(du * cg)
        dw_ref[0] += _row_fold(dyv * um2)
        dw_ref[1] += _row_fold(dyv * um1)
        dw_ref[2] += _row_fold(dyv * u)
        dw_ref[3] += _row_fold(dyv)

    blk = lambda col: pl.BlockSpec((tm, c), lambda i: (i, col))
    halo = lambda col: pl.BlockSpec((8, c), lambda i: (jnp.maximum(i * hb - 1, 0), col))
    nhalo = lambda col: pl.BlockSpec((8, c), lambda i: (jnp.minimum((i + 1) * hb, s // 8 - 1), col))
    return pl.pallas_call(
        body,
        name=name,
        grid=(s // tm,),
        in_specs=[blk(0), blk(1), blk(2), halo(1), halo(2), blk(0), nhalo(0), nhalo(0),
                  pl.BlockSpec((CONV_K, c), lambda i: (0, 0)), pl.BlockSpec((1, c), lambda i: (0, 0))],
        out_specs=[pl.BlockSpec((tm, 3 * c), lambda i: (i, 0)), pl.BlockSpec((4, 8, c), lambda i: (0, 0, 0))],
        out_shape=[jax.ShapeDtypeStruct((s, 3 * c), MXU_DTYPE), jax.ShapeDtypeStruct((4, 8, c), F32)],
        compiler_params=_params(),
    )(p, p, p, p, p, dya, dya, p, conv_w, conv_b)


def _sgu_bwd(p, dyb, ln_g, ln_b, wtril, wtril_t, bias_full, *, col0, name, tm=512):
    s = p.shape[0]
    c = ln_g.shape[1]
    groups = c // CHUNK
    tm = _tile(s, tm)

    def body(u_ref, v_ref, dy_ref, g_ref, b_ref, w_ref, wt_ref, bias_ref, du_ref, dv_ref, dln_ref, dw_ref, dbias_ref,
             dvn_ref):
        @pl.when(pl.program_id(0) == 0)
        def _():
            dln_ref[...] = jnp.zeros_like(dln_ref)
            dw_ref[...] = jnp.zeros_like(dw_ref)
            dbias_ref[...] = jnp.zeros_like(dbias_ref)

        sv = v_ref[...]
        xhat, r = _layernorm_stats(_gelu(sv))
        vb = _mx(xhat * g_ref[...] + b_ref[...])
        for n in range(tm // CHUNK):
            rows = slice(n * CHUNK, (n + 1) * CHUNK)
            for gi in range(groups):
                cols = slice(gi * CHUNK, (gi + 1) * CHUNK)
                su = u_ref[rows, cols]
                dy = dy_ref[rows, cols]
                vblk = vb[rows, cols]
                mixed = _dot(w_ref[gi], vblk) + bias_ref[:, cols]
                du_ref[rows, cols] = _mx(dy * mixed * _gelu_grad(su))
                dmixed = dy * _gelu(su)
                dmb = _mx(dmixed)
                dvn_ref[rows, cols] = _dot(wt_ref[gi], dmb)
                dw_ref[gi] += _dot_nt(dmb, vblk)
                dbias_ref[:, cols] += dmixed
        dvn = dvn_ref[...]
        dln_ref[0] += _row_fold(dvn * xhat)
        dln_ref[1] += _row_fold(dvn)
        dxh = dvn * g_ref[...]
        dgv = r * (dxh - jnp.mean(dxh, axis=-1, keepdims=True) - xhat * jnp.mean(dxh * xhat, axis=-1, keepdims=True))
        dv_ref[...] = _mx(dgv * _gelu_grad(sv))

    blk = lambda col: pl.BlockSpec((tm, c), lambda i: (i, col))
    vec = pl.BlockSpec((1, c), lambda i: (0, 0))
    wspec = pl.BlockSpec((groups, CHUNK, CHUNK), lambda i: (0, 0, 0))
    return pl.pallas_call(
        body,
        name=name,
        grid=(s // tm,),
        in_specs=[blk(col0), blk(col0 + 1), blk(0), vec, vec, wspec, wspec, pl.BlockSpec((CHUNK, c), lambda i: (0, 0))],
        out_specs=[blk(0), blk(0), pl.BlockSpec((2, 8, c), lambda i: (0, 0, 0)), wspec,
                   pl.BlockSpec((CHUNK, c), lambda i: (0, 0))],
        out_shape=[jax.ShapeDtypeStruct((s, c), MXU_DTYPE), jax.ShapeDtypeStruct((s, c), MXU_DTYPE),
                   jax.ShapeDtypeStruct((2, 8, c), F32), jax.ShapeDtypeStruct((groups, CHUNK, CHUNK), F32),
                   jax.ShapeDtypeStruct((CHUNK, c), F32)],
        scratch_shapes=[pltpu.VMEM((tm, c), F32)],
        compiler_params=_params(),
    )(p, p, dyb, ln_g, ln_b, wtril, wtril_t, bias_full)


def _att_bwd(k_hm, kt_hm, v_hm, qt_hm, dot_hm, runs, later, earlier, *, name, hg=2):
    h, nb, t, d = k_hm.shape
    s = nb * t
    nbp = runs.shape[2]
    scale = 1.0 / math.sqrt(d)
    assert h % hg == 0

    def body(k_ref, kt_ref, v_ref, qt_ref, dot_ref, runs_ref, later_ref, earlier_ref, dqt_ref, dkt_ref, dvt_ref):
        i = pl.program_id(1)

        @pl.when(i == 0)
        def _():
            dkt_ref[...] = jnp.zeros_like(dkt_ref)
            dvt_ref[...] = jnp.zeros_like(dvt_ref)

        q_ts = [qt_ref[g] for g in range(hg)]
        do_ts = [dot_ref[g] for g in range(hg)]

        best = runs_ref[0]
        for g in range(1, hg):
            best = jnp.maximum(best, runs_ref[g])
        row = lax.broadcasted_iota(jnp.int32, (nbp, 1), 0)
        counts = jnp.logical_and(jnp.max(best, axis=1, keepdims=True) > EXP_UNDERFLOW, row < i)
        seen = jnp.sum(counts.astype(jnp.int32))
        zeros_row = tuple(jnp.zeros((1, t), F32) for _ in range(hg))

        def up(j, carry):
            gsums, dqts = carry
            mask = _att_mask(t, j * t, i * t)
            heads = range(hg)
            zs, es, lbs, l1ms = _att_blocks([k_ref[g, j] for g in heads], q_ts, scale, mask)
            afters = _dot_split_each(later_ref[...], l1ms)
            das = [_dot(v_ref[g, j], do_ts[g]) for g in heads]
            weights = [jnp.where(mask, jnp.exp(lbs[g] + afters[g] + _row_get(runs_ref, g, j)), 0.0) for g in heads]
            grs = [das[g] * weights[g] for g in heads]
            gbefores = _dot_split_each(earlier_ref[...], grs)
            dzs = []
            for g in heads:
                inv = 1.0 / (1.0 + es[g])
                pos = zs[g] >= 0.0
                beta = jnp.where(pos, inv, es[g] * inv)
                omb = jnp.where(pos, es[g] * inv, inv)
                dzs.append(_mx(jnp.where(mask, grs[g] * omb - (gbefores[g] + gsums[g]) * beta, 0.0) * scale))
            new_dqts = [dqts[g] + _dot(kt_ref[g, j], dzs[g]) for g in heads]
            for g in heads:
                dkt_ref[g, j] += _dot_nt(q_ts[g], dzs[g])
            for g in heads:
                dvt_ref[g, j] += _dot_nt(do_ts[g], _mx(weights[g]))
            new_gsums = [gsums[g] + jnp.sum(grs[g], axis=0, keepdims=True) for g in heads]
            return tuple(new_gsums), tuple(new_dqts)

        _, dqts = lax.fori_loop(i - seen, i + 1, up, (zeros_row, tuple(jnp.zeros((d, t), F32) for _ in range(hg))))
        for g in range(hg):
            dqt_ref[g] = dqts[g]

    whole = lambda a, b: pl.BlockSpec((hg, nb, a, b), lambda hh, i: (hh, 0, 0, 0))
    cols = pl.BlockSpec((hg, d, t), lambda hh, i: (hh, 0, i))
    tri = pl.BlockSpec((t, t), lambda hh, i: (0, 0))
    return pl.pallas_call(
        body,
        name=name,
        grid=(h // hg, nb),
        in_specs=[whole(t, d), whole(d, t), whole(t, d), cols, cols,
                  pl.BlockSpec((hg, None, nbp, t), lambda hh, i: (hh, i, 0, 0)), tri, tri],
        out_specs=[cols, whole(d, t), whole(d, t)],
        out_shape=[jax.ShapeDtypeStruct((h, d, s), F32), jax.ShapeDtypeStruct((h, nb, d, t), F32),
                   jax.ShapeDtypeStruct((h, nb, d, t), F32)],
        compiler_params=_params(),
    )(k_hm, kt_hm, v_hm, qt_hm, dot_hm, runs, later, earlier)


def _qkv_bwd(p, dqn, dkn, dv, qg, kg, headavg, *, col0, name, tm=512):
    s = p.shape[0]
    c = qg.shape[1]
    tm = _tile(s, tm)

    def body(q_ref, k_ref, dqn_ref, dkn_ref, dv_ref, qg_ref, kg_ref, avg_ref, dq_ref, dk_ref, dvb_ref, dg_ref):
        @pl.when(pl.program_id(0) == 0)
        def _():
            dg_ref[...] = jnp.zeros_like(dg_ref)

        for n, (src, dn_ref, g_ref, dst) in enumerate(((q_ref, dqn_ref, qg_ref, dq_ref), (k_ref, dkn_ref, kg_ref, dk_ref))):
            xf = src[...]
            r = lax.rsqrt(_head_mean(xf * xf, avg_ref[...]) + EPS)
            y = xf * r
            dn = dn_ref[...]
            dy = dn * g_ref[...]
            dst[...] = _mx(r * (dy - y * _head_mean(dy * y, avg_ref[...])))
            dg_ref[n] += _row_fold(dn * y)
        dvb_ref[...] = _mx(dv_ref[...])

    blk = lambda col: pl.BlockSpec((tm, c), lambda i: (i, col))
    vec = pl.BlockSpec((1, c), lambda i: (0, 0))
    out_shape = jax.ShapeDtypeStruct((s, c), MXU_DTYPE)
    return pl.pallas_call(
        body,
        name=name,
        grid=(s // tm,),
        in_specs=[blk(col0), blk(col0 + 1), blk(0), blk(0), blk(0), vec, vec, pl.BlockSpec((c, c), lambda i: (0, 0))],
        out_specs=[blk(0), blk(0), blk(0), pl.BlockSpec((2, 8, c), lambda i: (0, 0, 0))],
        out_shape=[out_shape, out_shape, out_shape, jax.ShapeDtypeStruct((2, 8, c), F32)],
        compiler_params=_params(),
    )(p, p, dqn, dkn, dv, qg, kg, headavg)


def _place():
    x, y, c = lax.axis_index("x"), lax.axis_index("y"), lax.axis_index("c")
    chips = [(1 - x, y), (x, 1 - y), (1 - x, 1 - y)]
    return x, y, c, chips


def _shard_of(ref, axis, chip, width):
    idx = [slice(None)] * len(ref.shape)
    idx[axis] = pl.ds(chip * width, width)
    return ref.at[tuple(idx)]


def _place_shard(w, axis, chip_idx, dtype, *, name):
    layers = w.shape[0]
    rows, cols = _as_rows(w.shape[1:])
    if axis == len(w.shape) - 2:
        tr = _row_tile(rows, cols, 2)
        out_shape = (layers, rows, cols * N_CHIPS)
        out_spec = pl.BlockSpec((None, tr, cols), lambda l, i, j_ref: (l, i, j_ref[0]))
    else:
        assert axis == 0
        tr = _row_tile(rows, cols, 2)
        per = rows // tr
        out_shape = (layers, rows * N_CHIPS, cols)
        out_spec = pl.BlockSpec((None, tr, cols), lambda l, i, j_ref: (l, j_ref[0] * per + i, 0))
    full = list(w.shape)
    full[1 + axis] *= N_CHIPS

    def body(j_ref, w_ref, o_ref):
        o_ref[...] = w_ref[...].astype(dtype)

    out = pl.pallas_call(
        body,
        name=name,
        grid_spec=pltpu.PrefetchScalarGridSpec(
            num_scalar_prefetch=1,
            grid=(layers, rows // tr),
            in_specs=[pl.BlockSpec((None, tr, cols), lambda l, i, j_ref: (l, i, 0))],
            out_specs=out_spec,
        ),
        out_shape=jax.ShapeDtypeStruct(out_shape, dtype),
        compiler_params=_params(),
    )(chip_idx, w.reshape(layers, rows, cols))
    return out.reshape(full)


def _gather_weights(placed, axes, *, name):
    n = len(placed)
    widths = [pa.shape[1 + ax] // N_CHIPS for pa, ax in zip(placed, axes)]

    def body(*refs):
        outs = refs[n:2 * n]
        send_sem, recv_sem, fsend_sem, frecv_sem = refs[2 * n:]
        x, y, c, chips = _place()
        me = 2 * x + y
        sibling = (x, y, 1 - c)

        def slot(a, layer, chip):
            return _shard_of(outs[a].at[layer], axes[a], chip, widths[a])

        def send(a, k):
            return pltpu.make_async_remote_copy(
                src_ref=slot(a, c, me), dst_ref=slot(a, c, me), send_sem=send_sem.at[a, k], recv_sem=recv_sem.at[a, k],
                device_id=(*chips[k], c), device_id_type=MESH)

        def landed(a, k):
            cx, cy = chips[k]
            blk = slot(a, c, 2 * cx + cy)
            return pltpu.make_async_remote_copy(
                src_ref=blk, dst_ref=blk, send_sem=send_sem.at[a, k], recv_sem=recv_sem.at[a, k],
                device_id=(*chips[k], c), device_id_type=MESH)

        def forward(a, k, layer):
            cx, cy = chips[k]
            blk = slot(a, layer, 2 * cx + cy)
            return pltpu.make_async_remote_copy(
                src_ref=blk, dst_ref=blk, send_sem=fsend_sem.at[a, k], recv_sem=frecv_sem.at[a, k],
                device_id=sibling, device_id_type=MESH)

        for a in range(n):
            for k in range(3):
                send(a, k).start()
        for a in range(n):
            for k in range(3):
                landed(a, k).wait_recv()
                forward(a, k, c).start()
        for a in range(n):
            for k in range(3):
                forward(a, k, 1 - c).wait_recv()
        for a in range(n):
            for k in range(3):
                send(a, k).wait_send()
                forward(a, k, c).wait_send()

    return pl.pallas_call(
        body,
        name=name,
        in_specs=[ANY] * n,
        out_specs=[ANY] * n,
        out_shape=[jax.ShapeDtypeStruct(pa.shape, pa.dtype) for pa in placed],
        input_output_aliases={a: a for a in range(n)},
        scratch_shapes=[pltpu.SemaphoreType.DMA((n, 3))] * 4,
    )(*placed)


def _pair_exchange(grads, *, name):
    n = len(grads)

    def body(*refs):
        ins, outs = refs[:n], refs[n:2 * n]
        send_sem, recv_sem = refs[2 * n:]
        x, y, c, _ = _place()
        copies = [
            pltpu.make_async_remote_copy(
                src_ref=ins[a].at[1 - c], dst_ref=outs[a], send_sem=send_sem.at[a], recv_sem=recv_sem.at[a],
                device_id=(x, y, 1 - c), device_id_type=MESH)
            for a in range(n)
        ]
        for cp in copies:
            cp.start()
        for cp in copies:
            cp.wait()

    return pl.pallas_call(
        body,
        name=name,
        in_specs=[ANY] * n,
        out_specs=[ANY] * n,
        out_shape=[jax.ShapeDtypeStruct(g.shape[1:], g.dtype) for g in grads],
        scratch_shapes=[pltpu.SemaphoreType.DMA((n,)), pltpu.SemaphoreType.DMA((n,))],
    )(*grads)


def _chip_scatter(sums, axes, *, name):
    n = len(sums)
    widths = [sm.shape[ax] // N_CHIPS for sm, ax in zip(sums, axes)]
    out_shapes = []
    for sm, ax, w in zip(sums, axes, widths):
        shp = list(sm.shape)
        shp[ax] = w
        out_shapes.append(jax.ShapeDtypeStruct((3, *shp), sm.dtype))

    def body(*refs):
        ins, outs = refs[:n], refs[n:2 * n]
        send_sem, recv_sem = refs[2 * n:]
        x, y, c, chips = _place()
        copies = []
        for a in range(n):
            for k in range(3):
                cx, cy = chips[k]
                copies.append(pltpu.make_async_remote_copy(
                    src_ref=_shard_of(ins[a], axes[a], 2 * cx + cy, widths[a]), dst_ref=outs[a].at[k],
                    send_sem=send_sem.at[a, k], recv_sem=recv_sem.at[a, k], device_id=(cx, cy, c), device_id_type=MESH))
        for cp in copies:
            cp.start()
        for cp in copies:
            cp.wait()

    return pl.pallas_call(
        body,
        name=name,
        in_specs=[ANY] * n,
        out_specs=[ANY] * n,
        out_shape=out_shapes,
        scratch_shapes=[pltpu.SemaphoreType.DMA((n, 3)), pltpu.SemaphoreType.DMA((n, 3))],
    )(*sums)


def _pair_share(finals, *, name):
    n = len(finals)

    def body(*refs):
        outs = refs[n:2 * n]
        send_sem, recv_sem = refs[2 * n:]
        x, y, c, _ = _place()
        for a in range(n):
            pltpu.make_async_remote_copy(
                src_ref=outs[a].at[c], dst_ref=outs[a].at[c], send_sem=send_sem.at[a], recv_sem=recv_sem.at[a],
                device_id=(x, y, 1 - c), device_id_type=MESH).start()
        for a in range(n):
            pltpu.make_async_remote_copy(
                src_ref=outs[a].at[c], dst_ref=outs[a].at[1 - c], send_sem=send_sem.at[a], recv_sem=recv_sem.at[a],
                device_id=(x, y, 1 - c), device_id_type=MESH).wait()

    return pl.pallas_call(
        body,
        name=name,
        in_specs=[ANY] * n,
        out_specs=[ANY] * n,
        out_shape=[jax.ShapeDtypeStruct(f.shape, f.dtype) for f in finals],
        input_output_aliases={a: a for a in range(n)},
        scratch_shapes=[pltpu.SemaphoreType.DMA((n,))] * 2,
    )(*finals)


def _all_reduce_small(packed, *, name):
    r = packed.shape[0]

    def body(in_ref, out_ref, slots, send_sem, recv_sem):
        x, y, c, _ = _place()
        me = 4 * x + 2 * y + c
        slots[me] = in_ref[...]
        copies = []
        for k in range(1, N_DEV):
            peer = (x ^ (k >> 2), y ^ ((k >> 1) & 1), c ^ (k & 1))
            copies.append(pltpu.make_async_remote_copy(
                src_ref=in_ref, dst_ref=slots.at[me], send_sem=send_sem.at[k - 1], recv_sem=recv_sem.at[k - 1],
                device_id=peer, device_id_type=MESH))
        for cp in copies:
            cp.start()
        for k in range(1, N_DEV):
            pltpu.make_async_remote_copy(
                src_ref=in_ref, dst_ref=slots.at[me ^ k], send_sem=send_sem.at[k - 1], recv_sem=recv_sem.at[k - 1],
                device_id=(x, y, c), device_id_type=MESH).wait()
        total = slots[0]
        for dev in range(1, N_DEV):
            total = total + slots[dev]
        out_ref[...] = total

    return pl.pallas_call(
        body,
        name=name,
        in_specs=[pl.BlockSpec(memory_space=pltpu.VMEM)],
        out_specs=pl.BlockSpec(memory_space=pltpu.VMEM),
        out_shape=jax.ShapeDtypeStruct((r, 128), F32),
        scratch_shapes=[pltpu.VMEM((N_DEV, r, 128), F32), pltpu.SemaphoreType.DMA((N_DEV - 1,)),
                        pltpu.SemaphoreType.DMA((N_DEV - 1,))],
        compiler_params=_params(),
    )(packed)


def _as_rows(shape):
    cols = shape[-1]
    return math.prod(shape[:-1]), cols


ELEMENTWISE_VMEM = 24 * 1024 * 1024


def _row_tile(rows, cols, n_arrays, sublanes=8):
    cap = ELEMENTWISE_VMEM // (n_arrays * 2 * 4 * cols)
    best = None
    for t in range(sublanes, min(rows, cap) + 1, sublanes):
        if rows % t == 0:
            best = t
    assert best is not None, (rows, cols)
    return best


def _pair_sum(g, other, c_idx, *, name):
    rows, cols = _as_rows(other.shape)
    tr = _row_tile(rows, cols, 4, sublanes=16)
    g2 = g.reshape(2, rows, cols)

    def body(c_ref, g_ref, o_ref, out_ref, wire_ref):
        total = g_ref[...] + o_ref[...]
        out_ref[...] = total
        wire_ref[...] = total.astype(WIRE_DTYPE)

    blk = pl.BlockSpec((tr, cols), lambda i, c_ref: (i, 0))
    out, wire = pl.pallas_call(
        body,
        name=name,
        grid_spec=pltpu.PrefetchScalarGridSpec(
            num_scalar_prefetch=1,
            grid=(rows // tr,),
            in_specs=[pl.BlockSpec((None, tr, cols), lambda i, c_ref: (c_ref[0], i, 0)), blk],
            out_specs=[blk, blk],
        ),
        out_shape=[jax.ShapeDtypeStruct((rows, cols), F32), jax.ShapeDtypeStruct((rows, cols), WIRE_DTYPE)],
        compiler_params=_params(),
    )(c_idx, g2, other.reshape(rows, cols))
    return out.reshape(other.shape), wire.reshape(other.shape)


def _chip_sum(mine, got, axis, chip_idx, c_idx, *, name):
    shard_shape = got.shape[1:]
    rows, cols = _as_rows(shard_shape)
    tr = _row_tile(rows, cols, 5, sublanes=16)
    if axis == len(mine.shape) - 1:
        m2 = mine.reshape(rows, cols * N_CHIPS)
        mine_spec = pl.BlockSpec((tr, cols), lambda i, j_ref, c_ref: (i, j_ref[0]))
    else:
        assert axis == 0
        m2 = mine.reshape(N_CHIPS, rows, cols)
        mine_spec = pl.BlockSpec((None, tr, cols), lambda i, j_ref, c_ref: (j_ref[0], i, 0))

    def body(j_ref, c_ref, m_ref, got_ref, out_ref):
        out_ref[...] = ((m_ref[...] + got_ref[0].astype(F32)) + got_ref[1].astype(F32)) + got_ref[2].astype(F32)

    out = pl.pallas_call(
        body,
        name=name,
        grid_spec=pltpu.PrefetchScalarGridSpec(
            num_scalar_prefetch=2,
            grid=(rows // tr,),
            in_specs=[mine_spec, pl.BlockSpec((3, tr, cols), lambda i, j_ref, c_ref: (0, i, 0))],
            out_specs=pl.BlockSpec((None, tr, cols), lambda i, j_ref, c_ref: (c_ref[0], i, 0)),
        ),
        out_shape=jax.ShapeDtypeStruct((2, rows, cols), F32),
        compiler_params=_params(),
    )(chip_idx, c_idx, m2, got.reshape(3, rows, cols))
    return out.reshape((2, *shard_shape))


def _adamw(w, g, m, v, *, name):
    shape = w.shape
    rows, cols = _as_rows(shape)
    tr = _row_tile(rows, cols, 7)
    c1 = 1.0 / (1.0 - ADAM_B1 ** ADAM_STEP)
    c2 = 1.0 / (1.0 - ADAM_B2 ** ADAM_STEP)

    def body(w_ref, g_ref, m_ref, v_ref, d_ref, nm_ref, nv_ref):
        gg = g_ref[...]
        nm = ADAM_B1 * m_ref[...] + (1.0 - ADAM_B1) * gg
        nv = ADAM_B2 * v_ref[...] + (1.0 - ADAM_B2) * (gg * gg)
        nm_ref[...] = nm
        nv_ref[...] = nv
        d_ref[...] = -ADAM_LR * ((nm * c1) / (jnp.sqrt(nv * c2) + ADAM_EPS) + ADAM_WD * w_ref[...])

    blk = pl.BlockSpec((tr, cols), lambda i: (i, 0))
    flat = lambda a: a.reshape(rows, cols)
    outs = pl.pallas_call(
        body,
        name=name,
        grid=(rows // tr,),
        in_specs=[blk] * 4,
        out_specs=[blk] * 3,
        out_shape=[jax.ShapeDtypeStruct((rows, cols), F32)] * 3,
        compiler_params=_params(),
    )(flat(w), flat(g), flat(m), flat(v))
    return tuple(o.reshape(shape) for o in outs)


def _rows_blocked(a, t):
    return a.reshape(a.shape[0] // t, t, HEADS, HEAD_DIM).transpose(2, 0, 1, 3)


def _cols_blocked(a, t):
    return a.reshape(a.shape[0] // t, t, HEADS, HEAD_DIM).transpose(2, 0, 3, 1)


def _cols(a):
    return a.reshape(a.shape[0], HEADS, HEAD_DIM).transpose(1, 2, 0)


def _from_cols(a):
    h, d, s = a.shape
    return a.transpose(2, 0, 1).reshape(s, h * d)


def _from_cols_blocked(a):
    h, nb, d, t = a.shape
    return a.transpose(1, 3, 0, 2).reshape(nb * t, h * d)


def _pack(parts):
    flat = []
    for a in parts:
        v = a.reshape(-1)
        flat.append(jnp.pad(v, (0, (-v.shape[0]) % 1024)))
    return jnp.concatenate(flat).reshape(-1, 128)


def _unpack(packed, shapes):
    flat = packed.reshape(-1)
    out, off = [], 0
    for shp in shapes:
        size = math.prod(shp)
        out.append(flat[off:off + size].reshape(shp))
        off += size + (-size) % 1024
    return out


BIG = ("w_in", "w_branch_out", "w_o", "w_gate_up", "w_down")
BIG_AXIS = {"w_in": 1, "w_branch_out": 2, "w_o": 0, "w_gate_up": 1, "w_down": 0}
SMALL = ("mix_norm_g", "b_gate", "conv_w", "conv_b", "sgu_ln_g", "sgu_ln_b", "sgu_w", "sgu_b", "q_norm_g", "k_norm_g",
         "ffn_norm_g")
ORDER = ("mix_norm_g", "w_in", "b_gate", "conv_w", "conv_b", "sgu_ln_g", "sgu_ln_b", "sgu_w", "sgu_b", "q_norm_g",
         "k_norm_g", "w_branch_out", "w_o", "ffn_norm_g", "w_gate_up", "w_down")


def _layer_forward(x, w, l):
    t = ATT_BLOCK
    c = w["conv_b"].shape[1]
    n_in = w["w_in"].shape[2]
    gate_col0 = (n_in - 3 * x.shape[1]) // x.shape[1]
    tag = f"l{l}"
    p, h = _norm_matmul(x, w["mix_norm_g"][l][None], w["w_in"][l], name=f"in_proj_{tag}", tn=n_in // 8)
    ya = _conv_fwd(p, w["conv_w"][l], w["conv_b"][l][None], name=f"conv_{tag}")
    yb = _sgu_fwd(p, w["sgu_ln_g"][l][None], w["sgu_ln_b"][l][None], w["wtril"][l], w["bias_full"][l], col0=3,
                  name=f"sgu_{tag}")
    qn, kn, vb = _qkv_prep(p, w["qg"][l], w["kg"][l], w["headavg"], col0=5, name=f"qkv_{tag}")
    q_c = _cols(qn)
    k_rb, k_cb = _rows_blocked(kn, t), _cols_blocked(kn, t)
    v_rb, v_cb = _rows_blocked(vb, t), _cols_blocked(vb, t)
    out_t, runs = _att_fwd(k_rb, q_c, v_cb, w["later"], name=f"att_{tag}")
    yc = _from_cols(out_t)
    x1, merged = _merge_fwd((ya, yb, yc), p, w["b_gate"][l][None], w["w_branch_out"][l], w["w_o"][l], x,
                            gate_col0=gate_col0, name=f"merge_{tag}")
    gu, h2 = _norm_matmul(x1, w["ffn_norm_g"][l][None], w["w_gate_up"][l], name=f"gate_up_{tag}",
                          tn=w["w_gate_up"].shape[2] // 4)
    x2 = _ffn_down(gu, w["w_down"][l], x1, name=f"down_{tag}")
    saved = dict(x=x, p=p, h=h, ya=ya, yb=yb, yc=yc, merged=merged, x1=x1, gu=gu, h2=h2,
                 att=(k_rb, k_cb, v_rb, q_c, runs), gate_col0=gate_col0)
    return x2, saved


def _layer_backward(dx2, w, sv, l):
    t = ATT_BLOCK
    tag = f"l{l}"
    d = dx2.shape[1]
    c = w["conv_b"].shape[1]
    n_in = w["w_in"].shape[2]
    n_ff = w["w_gate_up"].shape[2]
    g = {}
    dgu, act, dx2b = _ffn_bwd(dx2, sv["gu"], w["w_down"][l], name=f"down_bwd_{tag}")
    g["w_down"] = _matmul_tn(act, dx2b, name=f"dw_down_{tag}", t1=act.shape[1] // 2)
    dx1, dg2 = _matmul_nt_normbwd(dgu, w["w_gate_up"][l], sv["x1"], w["ffn_norm_g"][l][None], dx2,
                                  name=f"gate_up_bwd_{tag}", tk=n_ff // 4)
    g["ffn_norm_g"] = jnp.sum(dg2, axis=0)
    g["w_gate_up"] = _matmul_tn(sv["h2"], dgu, name=f"dw_gate_up_{tag}", tn=n_ff // 4)
    ys = (sv["ya"], sv["yb"], sv["yc"])
    (dgates, dya, dyb, dyc, dd0, dd1, dd2, dx1b, dbg) = _merge_bwd(
        dx1, ys, sv["p"], w["b_gate"][l][None], w["w_branch_out"][l], w["w_o"][l], gate_col0=sv["gate_col0"],
        name=f"merge_bwd_{tag}")
    g["b_gate"] = jnp.sum(dbg, axis=0)
    g["w_o"] = _matmul_tn(sv["merged"], dx1b, name=f"dw_o_{tag}")
    g["w_branch_out"] = jnp.stack([
        _matmul_tn(y, dd, name=f"dw_bo{i}_{tag}") for i, (y, dd) in enumerate(zip(ys, (dd0, dd1, dd2)))])
    dconv, dwc = _conv_bwd(sv["p"], dya, w["conv_w"][l], w["conv_b"][l][None], name=f"conv_bwd_{tag}")
    dwc = jnp.sum(dwc, axis=1)
    g["conv_w"] = dwc[0:CONV_K]
    g["conv_b"] = dwc[CONV_K]
    dsu, dsv, dln, dws, dbias = _sgu_bwd(sv["p"], dyb, w["sgu_ln_g"][l][None], w["sgu_ln_b"][l][None], w["wtril"][l],
                                         w["wtril_t"][l], w["bias_full"][l], col0=3, name=f"sgu_bwd_{tag}")
    dln = jnp.sum(dln, axis=1)
    g["sgu_ln_g"], g["sgu_ln_b"] = dln[0], dln[1]
    g["sgu_w"] = jnp.where(w["tril"], dws, 0.0)
    g["sgu_b"] = jnp.sum(dbias.reshape(CHUNK, c // CHUNK, CHUNK), axis=2).T
    k_rb, k_cb, v_rb, q_c, runs = sv["att"]
    dqt, dkt, dvt = _att_bwd(k_rb, k_cb, v_rb, q_c, _cols(_mx(dyc)), runs, w["later"], w["earlier"],
                             name=f"att_bwd_{tag}")
    dqn, dkn, dvv = _from_cols(dqt), _from_cols_blocked(dkt), _from_cols_blocked(dvt)
    dq, dk_, dvb, dqkg = _qkv_bwd(sv["p"], dqn, dkn, dvv, w["qg"][l], w["kg"][l], w["headavg"], col0=5,
                                  name=f"qkv_bwd_{tag}")
    dqkg = jnp.sum(dqkg.reshape(2, 8 * HEADS, HEAD_DIM), axis=1)
    g["q_norm_g"], g["k_norm_g"] = dqkg[0], dqkg[1]
    dp = jnp.concatenate([dconv, dsu, dsv, dq, dk_, dvb, dgates], axis=1)
    dx0, dg1 = _matmul_nt_normbwd(dp, w["w_in"][l], sv["x"], w["mix_norm_g"][l][None], dx1, name=f"in_proj_bwd_{tag}",
                                  tk=n_in // 8)
    g["mix_norm_g"] = jnp.sum(dg1, axis=0)
    g["w_in"] = _matmul_tn(sv["h"], dp, name=f"dw_in_{tag}", tn=n_in // 4)
    return dx0, g


def kernel(x, mix_norm_g, w_in, b_gate, conv_w, conv_b, sgu_ln_g, sgu_ln_b, sgu_w, sgu_b, q_norm_g, k_norm_g, w_branch_out, w_o, ffn_norm_g, w_gate_up, w_down, loss_target, m_mix_norm_g, m_w_in, m_b_gate, m_conv_w, m_conv_b, m_sgu_ln_g, m_sgu_ln_b, m_sgu_w, m_sgu_b, m_q_norm_g, m_k_norm_g, m_w_branch_out, m_w_o, m_ffn_norm_g, m_w_gate_up, m_w_down, v_mix_norm_g, v_w_in, v_b_gate, v_conv_w, v_conv_b, v_sgu_ln_g, v_sgu_ln_b, v_sgu_w, v_sgu_b, v_q_norm_g, v_k_norm_g, v_w_branch_out, v_w_o, v_ffn_norm_g, v_w_gate_up, v_w_down):
    given = dict(locals())
    params = {n: given[n] for n in ORDER}
    moms = {n: (given["m_" + n], given["v_" + n]) for n in ORDER}
    layers = mix_norm_g.shape[0]
    assert layers == 2, "the exchanges split the work of a chip's two cores by layer"
    xs = x[0]
    target = loss_target[0]
    chip = 2 * lax.axis_index("x") + lax.axis_index("y")
    core = lax.axis_index("c")

    c_idx = core.reshape(1).astype(jnp.int32)
    j_idx = chip.reshape(1).astype(jnp.int32)
    width = conv_w.shape[2]
    placed = [_place_shard(params[n], BIG_AXIS[n], j_idx, MXU_DTYPE, name=f"place_{n}") for n in BIG]
    placed.append(lax.dynamic_update_slice_in_dim(jnp.zeros((layers, CONV_K, width * N_CHIPS), F32), conv_w, chip * width, 2))
    gathered = _gather_weights(placed, [BIG_AXIS[n] for n in BIG] + [1], name="gather_weights")
    w = dict(zip(BIG + ("conv_w",), gathered))
    for n in ("mix_norm_g", "b_gate", "conv_b", "sgu_ln_g", "sgu_ln_b", "ffn_norm_g"):
        w[n] = params[n]
    groups = sgu_w.shape[1]
    tril = jnp.tril(jnp.ones((CHUNK, CHUNK), dtype=bool))
    w["tril"] = tril
    w["wtril"] = _mx(jnp.where(tril, sgu_w, 0.0))
    w["wtril_t"] = w["wtril"].transpose(0, 1, 3, 2)
    w["bias_full"] = jnp.repeat(sgu_b.transpose(0, 2, 1), CHUNK, axis=2)
    w["qg"] = jnp.tile(q_norm_g, (1, HEADS))[:, None, :]
    w["kg"] = jnp.tile(k_norm_g, (1, HEADS))[:, None, :]
    lane = jnp.arange(HEADS * HEAD_DIM) // HEAD_DIM
    w["headavg"] = _mx(jnp.where(lane[:, None] == lane[None, :], 1.0 / HEAD_DIM, 0.0))
    pos = jnp.arange(ATT_BLOCK)
    w["later"] = _mx(jnp.where(pos[None, :] > pos[:, None], 1.0, 0.0))
    w["earlier"] = _mx(jnp.where(pos[None, :] < pos[:, None], 1.0, 0.0))

    saved = []
    act = xs
    for l in range(layers):
        act, sv = _layer_forward(act, w, l)
        saved.append(sv)
    loss_part, dact = _loss_head(act, target, name="loss_head")
    loss = lax.psum(jnp.sum(loss_part), ("x", "y", "c"))
    grads = [None] * layers
    for l in reversed(range(layers)):
        dact, grads[l] = _layer_backward(dact, w, saved[l], l)
    grad_x = dact[None]
    local = {n: jnp.stack([grads[l][n] for l in range(layers)]) for n in ORDER}

    got = _pair_exchange([local[n] for n in BIG], name="grads_pair_exchange")
    pair = [_pair_sum(local[n], o, c_idx, name=f"pair_sum_{n}") for n, o in zip(BIG, got)]
    landed = _chip_scatter([wire for _, wire in pair], [BIG_AXIS[n] for n in BIG], name="grads_chip_scatter")
    final = [_chip_sum(pr, ld, BIG_AXIS[n], j_idx, c_idx, name=f"chip_sum_{n}") for n, (pr, _), ld in zip(BIG, pair, landed)]
    full = dict(zip(BIG, _pair_share(final, name="grads_pair_share")))

    small_shapes = [local[n].shape for n in SMALL]
    summed = _unpack(_all_reduce_small(_pack([local[n] for n in SMALL]), name="grads_all_reduce_small"), small_shapes)
    for n, gsum in zip(SMALL, summed):
        full[n] = gsum
    full["conv_w"] = lax.dynamic_slice_in_dim(full["conv_w"], chip * width, width, axis=2)

    out = {}
    for n in BIG:
        out[n] = _adamw(params[n], full[n], *moms[n], name=f"adamw_{n}")
    shapes = [params[n].shape for n in SMALL]
    packed = [_pack([src(n) for n in SMALL]) for src in
              (lambda n: params[n], lambda n: full[n], lambda n: moms[n][0], lambda n: moms[n][1])]
    for n, dl, nm, nv in zip(SMALL, *[_unpack(o, shapes) for o in _adamw(*packed, name="adamw_small")]):
        out[n] = (dl, nm, nv)
    return (loss, grad_x, *[full[n] for n in ORDER], *[out[n][0] for n in ORDER], *[out[n][1] for n in ORDER],
            *[out[n][2] for n in ORDER])
```

```python
import functools
import math

import jax
import jax.numpy as jnp
from jax import lax
from jax.experimental import pallas as pl
from jax.experimental.pallas import tpu as pltpu

F32 = jnp.float32
MXU_DTYPE = jnp.bfloat16
WIRE_DTYPE = jnp.bfloat16

EPS = 1e-6
CONV_K = 3
CHUNK = 128
HEADS = 8
HEAD_DIM = 64
ATT_BLOCK = 128
EXP_UNDERFLOW = -104.0

ADAM_LR = 0.001
ADAM_B1 = 0.9
ADAM_B2 = 0.999
ADAM_EPS = 1e-08
ADAM_WD = 0.01
ADAM_STEP = 10

VMEM_LIMIT = 56 * 1024 * 1024
MESH = pl.DeviceIdType.MESH
N_CHIPS = 4
N_DEV = 8
ANY = pl.BlockSpec(memory_space=pl.ANY)


def _params(**kw):
    return pltpu.CompilerParams(vmem_limit_bytes=VMEM_LIMIT, **kw)


def _mx(v):
    return v.astype(MXU_DTYPE)


def _dot(a, b):
    return lax.dot_general(a, b, (((1,), (0,)), ((), ())), preferred_element_type=F32)


def _dot_nt(a, b):
    return lax.dot_general(a, b, (((1,), (1,)), ((), ())), preferred_element_type=F32)


def _dot_tn(a, b):
    return lax.dot_general(a, b, (((0,), (0,)), ((), ())), preferred_element_type=F32)


def _dot_split(const, v):
    hi = _mx(v)
    lo = _mx(v - hi.astype(F32))
    return _dot(const, hi) + _dot(const, lo)


def _dot_split_r(v, const):
    hi = _mx(v)
    lo = _mx(v - hi.astype(F32))
    return _dot(hi, const) + _dot(lo, const)


def _sigmoid(x):
    return 1.0 / (1.0 + jnp.exp(-x))


_INV_SQRT2 = 1.0 / math.sqrt(2.0)
_INV_SQRT2PI = 1.0 / math.sqrt(2.0 * math.pi)


def _gelu(x):
    return 0.5 * x * (1.0 + lax.erf(x * _INV_SQRT2))


def _gelu_grad(x):
    return 0.5 * (1.0 + lax.erf(x * _INV_SQRT2)) + x * jnp.exp(-0.5 * x * x) * _INV_SQRT2PI


def _row_fold(v):
    m, n = v.shape
    return jnp.sum(v.reshape(m // 8, 8, n), axis=0)


def _tile(m, pref):
    t = min(m, pref)
    while m % t:
        t //= 2
    return t


def _resident(block_shape, index_map):
    return pl.BlockSpec(block_shape, index_map, pipeline_mode=pl.Buffered(1))


def _norm_matmul(x, g, w, l, *, name, tm=1024, tn=None):
    s, d = x.shape
    n = w.shape[2]
    tm = _tile(s, tm)
    tn = tn or n
    assert n % tn == 0

    def body(x_ref, g_ref, w_ref, p_ref, h_ref):
        @pl.when(pl.program_id(1) == 0)
        def _():
            xf = x_ref[...]
            r = lax.rsqrt(jnp.mean(xf * xf, axis=-1, keepdims=True) + EPS)
            h_ref[...] = _mx(xf * r * g_ref[...])

        p_ref[...] = _dot(h_ref[...], w_ref[...])

    return pl.pallas_call(
        body,
        name=name,
        grid=(s // tm, n // tn),
        in_specs=[
            pl.BlockSpec((tm, d), lambda i, j: (i, 0)),
            pl.BlockSpec((1, d), lambda i, j: (0, 0)),
            pl.BlockSpec((None, d, tn), lambda i, j: (l, 0, j)),
        ],
        out_specs=[
            pl.BlockSpec((tm, tn), lambda i, j: (i, j)),
            pl.BlockSpec((tm, d), lambda i, j: (i, 0)),
        ],
        out_shape=[jax.ShapeDtypeStruct((s, n), F32), jax.ShapeDtypeStruct((s, d), MXU_DTYPE)],
        compiler_params=_params(),
    )(x, g, w)


def _conv_taps(u, prev):
    row = lax.broadcasted_iota(jnp.int32, u.shape, 0)
    um1 = jnp.where(row == 0, prev[7:8, :], pltpu.roll(u, 1, 0))
    um2 = pltpu.roll(u, 2, 0)
    um2 = jnp.where(row == 0, prev[6:7, :], jnp.where(row == 1, prev[7:8, :], um2))
    return um1, um2


def _conv_fwd(p, conv_w, conv_b, *, name, tm=512):
    s = p.shape[0]
    c = conv_w.shape[1]
    tm = _tile(s, tm)
    hb = tm // 8

    def body(bg_ref, cg_ref, xa_ref, cgp_ref, xap_ref, w_ref, b_ref, y_ref):
        first = pl.program_id(0) == 0
        u = cg_ref[...] * xa_ref[...]
        prev = jnp.where(first, 0.0, cgp_ref[...] * xap_ref[...])
        um1, um2 = _conv_taps(u, prev)
        w = w_ref[...]
        y = b_ref[...] + w[0:1, :] * um2 + w[1:2, :] * um1 + w[2:3, :] * u
        y_ref[...] = _mx(bg_ref[...] * y)

    halo = lambda col: pl.BlockSpec((8, c), lambda i: (jnp.maximum(i * hb - 1, 0), col))
    return pl.pallas_call(
        body,
        name=name,
        grid=(s // tm,),
        in_specs=[
            pl.BlockSpec((tm, c), lambda i: (i, 0)),
            pl.BlockSpec((tm, c), lambda i: (i, 1)),
            pl.BlockSpec((tm, c), lambda i: (i, 2)),
            halo(1),
            halo(2),
            pl.BlockSpec((CONV_K, c), lambda i: (0, 0)),
            pl.BlockSpec((1, c), lambda i: (0, 0)),
        ],
        out_specs=pl.BlockSpec((tm, c), lambda i: (i, 0)),
        out_shape=jax.ShapeDtypeStruct((s, c), MXU_DTYPE),
        compiler_params=_params(),
    )(p, p, p, p, p, conv_w, conv_b)


def _layernorm_stats(x):
    mu = jnp.mean(x, axis=-1, keepdims=True)
    xc = x - mu
    r = lax.rsqrt(jnp.mean(xc * xc, axis=-1, keepdims=True) + EPS)
    return xc * r, r


def _sgu_fwd(p, ln_g, ln_b, wtril, bias_full, *, col0, name, tm=512):
    s = p.shape[0]
    c = ln_g.shape[1]
    groups = c // CHUNK
    tm = _tile(s, tm)

    def body(u_ref, v_ref, g_ref, b_ref, w_ref, bias_ref, y_ref):
        vn, _ = _layernorm_stats(_gelu(v_ref[...]))
        vb = _mx(vn * g_ref[...] + b_ref[...])
        for n in range(tm // CHUNK):
            rows = slice(n * CHUNK, (n + 1) * CHUNK)
            for gi in range(groups):
                cols = slice(gi * CHUNK, (gi + 1) * CHUNK)
                mixed = _dot(w_ref[gi], vb[rows, cols]) + bias_ref[:, cols]
                y_ref[rows, cols] = _mx(_gelu(u_ref[rows, cols]) * mixed)

    return pl.pallas_call(
        body,
        name=name,
        grid=(s // tm,),
        in_specs=[
            pl.BlockSpec((tm, c), lambda i: (i, col0)),
            pl.BlockSpec((tm, c), lambda i: (i, col0 + 1)),
            pl.BlockSpec((1, c), lambda i: (0, 0)),
            pl.BlockSpec((1, c), lambda i: (0, 0)),
            pl.BlockSpec((groups, CHUNK, CHUNK), lambda i: (0, 0, 0)),
            pl.BlockSpec((CHUNK, c), lambda i: (0, 0)),
        ],
        out_specs=pl.BlockSpec((tm, c), lambda i: (i, 0)),
        out_shape=jax.ShapeDtypeStruct((s, c), MXU_DTYPE),
        compiler_params=_params(),
    )(p, p, ln_g, ln_b, wtril, bias_full)


def _head_mean(v, headavg):
    return _dot_split_r(v, headavg)


def _qkv_prep(p, qg, kg, headavg, *, col0, name, tm=512):
    s = p.shape[0]
    c = qg.shape[1]
    tm = _tile(s, tm)

    def body(q_ref, k_ref, v_ref, qg_ref, kg_ref, avg_ref, qn_ref, kn_ref, vb_ref):
        for src, g_ref, dst in ((q_ref, qg_ref, qn_ref), (k_ref, kg_ref, kn_ref)):
            xf = src[...]
            r = lax.rsqrt(_head_mean(xf * xf, avg_ref[...]) + EPS)
            dst[...] = _mx(xf * r * g_ref[...])
        vb_ref[...] = _mx(v_ref[...])

    blk = lambda col: pl.BlockSpec((tm, c), lambda i: (i, col))
    vec = pl.BlockSpec((1, c), lambda i: (0, 0))
    out = pl.BlockSpec((tm, c), lambda i: (i, 0))
    return pl.pallas_call(
        body,
        name=name,
        grid=(s // tm,),
        in_specs=[blk(col0), blk(col0 + 1), blk(col0 + 2), vec, vec, pl.BlockSpec((c, c), lambda i: (0, 0))],
        out_specs=[out, out, out],
        out_shape=[jax.ShapeDtypeStruct((s, c), MXU_DTYPE)] * 3,
        compiler_params=_params(),
    )(p, p, p, qg, kg, headavg)


def _att_mask(t, key0, qry0):
    kpos = key0 + lax.broadcasted_iota(jnp.int32, (t, t), 0)
    qpos = qry0 + lax.broadcasted_iota(jnp.int32, (t, t), 1)
    return kpos < qpos


def _att_blocks(k_blks, q_ts, scale, mask):
    zs = [_dot(k, q) * scale for k, q in zip(k_blks, q_ts)]
    es = [jnp.exp(-jnp.abs(z)) for z in zs]
    lbs = [jnp.minimum(z, 0.0) - jnp.log1p(e) for z, e in zip(zs, es)]
    l1ms = [jnp.where(mask, lb - z, 0.0) for lb, z in zip(lbs, zs)]
    return zs, es, lbs, l1ms


def _dot_split_each(const, vs):
    his = [_mx(v) for v in vs]
    los = [_mx(v - hi.astype(F32)) for v, hi in zip(vs, his)]
    tops = [_dot(const, hi) for hi in his]
    return [top + _dot(const, lo) for top, lo in zip(tops, los)]


def _max_over(runs):
    m = runs[0]
    for r in runs[1:]:
        m = jnp.maximum(m, r)
    return jnp.max(m)


NOT_SEEN = -1e30


def _row_put(ref, g, j, row):
    j8 = pl.multiple_of((j // 8) * 8, 8)
    blk = ref[g, pl.ds(j8, 8), :]
    sub = lax.broadcasted_iota(jnp.int32, blk.shape, 0)
    ref[g, pl.ds(j8, 8), :] = jnp.where(sub == j - j8, row, blk)


def _row_get(ref, g, j):
    j8 = pl.multiple_of((j // 8) * 8, 8)
    blk = ref[g, pl.ds(j8, 8), :]
    sub = lax.broadcasted_iota(jnp.int32, blk.shape, 0)
    return jnp.sum(jnp.where(sub == j - j8, blk, 0.0), axis=0, keepdims=True)


def _att_fwd(k_hm, qt_hm, vt_hm, later, *, name, hg=4):
    h, nb, t, d = k_hm.shape
    s = nb * t
    nbp = -(-nb // 8) * 8
    scale = 1.0 / math.sqrt(d)
    assert h % hg == 0

    def body(k_ref, qt_ref, vt_ref, later_ref, o_ref, runs_ref):
        i = pl.program_id(1)
        q_ts = [qt_ref[g] for g in range(hg)]
        runs_ref[...] = jnp.full(runs_ref.shape, NOT_SEEN, F32)

        def cond(carry):
            j, _, _, rmax = carry
            return jnp.logical_and(j >= 0, rmax > EXP_UNDERFLOW)

        def step(carry):
            j, runs, accs, _ = carry
            mask = _att_mask(t, j * t, i * t)
            heads = range(hg)
            for g in heads:
                _row_put(runs_ref, g, j, runs[g])
            _, _, lbs, l1ms = _att_blocks([k_ref[g, j] for g in heads], q_ts, scale, mask)
            afters = _dot_split_each(later_ref[...], l1ms)
            weights = [_mx(jnp.where(mask, jnp.exp(lbs[g] + afters[g] + runs[g]), 0.0)) for g in heads]
            new_accs = [accs[g] + _dot(vt_ref[g, j], weights[g]) for g in heads]
            new_runs = [runs[g] + jnp.sum(l1ms[g], axis=0, keepdims=True) for g in heads]
            return j - 1, tuple(new_runs), tuple(new_accs), _max_over(new_runs)

        init = (i, tuple(jnp.zeros((1, t), F32) for _ in range(hg)), tuple(jnp.zeros((d, t), F32) for _ in range(hg)),
                jnp.float32(0.0))
        _, _, accs, _ = lax.while_loop(cond, step, init)
        for g in range(hg):
            o_ref[g] = _mx(accs[g])

    return pl.pallas_call(
        body,
        name=name,
        grid=(h // hg, nb),
        in_specs=[
            pl.BlockSpec((hg, nb, t, d), lambda hh, i: (hh, 0, 0, 0)),
            pl.BlockSpec((hg, d, t), lambda hh, i: (hh, 0, i)),
            pl.BlockSpec((hg, nb, d, t), lambda hh, i: (hh, 0, 0, 0)),
            pl.BlockSpec((t, t), lambda hh, i: (0, 0)),
        ],
        out_specs=[pl.BlockSpec((hg, d, t), lambda hh, i: (hh, 0, i)),
                   pl.BlockSpec((hg, None, nbp, t), lambda hh, i: (hh, i, 0, 0))],
        out_shape=[jax.ShapeDtypeStruct((h, d, s), MXU_DTYPE), jax.ShapeDtypeStruct((h, nb, nbp, t), F32)],
        compiler_params=_params(),
    )(k_hm, qt_hm, vt_hm, later)


def _merge_fwd(ys, p, b_gate, w_bo, w_o, l, x, *, gate_col0, name, tm=512):
    s, d = x.shape
    _, nbr, c, _ = w_bo.shape
    tm = _tile(s, tm)

    def body(ya_ref, yb_ref, yc_ref, g0_ref, g1_ref, g2_ref, bg_ref, wbo_ref, wo_ref, x_ref, x1_ref, m_ref):
        merged = jnp.zeros((tm, d), F32)
        for i, (y_ref, g_ref) in enumerate(((ya_ref, g0_ref), (yb_ref, g1_ref), (yc_ref, g2_ref))):
            gate = _sigmoid(g_ref[...] + bg_ref[:, i * d:(i + 1) * d])
            merged = merged + gate * _dot(y_ref[...], wbo_ref[i])
        mb = _mx(merged)
        m_ref[...] = mb
        x1_ref[...] = x_ref[...] + _dot(mb, wo_ref[...])

    yblk = pl.BlockSpec((tm, c), lambda i: (i, 0))
    gblk = lambda k: pl.BlockSpec((tm, d), lambda i: (i, gate_col0 + k))
    xblk = pl.BlockSpec((tm, d), lambda i: (i, 0))
    return pl.pallas_call(
        body,
        name=name,
        grid=(s // tm,),
        in_specs=[
            yblk, yblk, yblk, gblk(0), gblk(1), gblk(2),
            pl.BlockSpec((1, nbr * d), lambda i: (0, 0)),
            _resident((None, nbr, c, d), lambda i: (l, 0, 0, 0)),
            _resident((None, d, d), lambda i: (l, 0, 0)),
            xblk,
        ],
        out_specs=[xblk, xblk],
        out_shape=[jax.ShapeDtypeStruct((s, d), F32), jax.ShapeDtypeStruct((s, d), MXU_DTYPE)],
        compiler_params=_params(),
    )(*ys, p, p, p, b_gate, w_bo, w_o, x)


def _ffn_down(gu, w_down, l, x, *, name, tm=512):
    s, d = x.shape
    f = w_down.shape[1]
    tm = _tile(s, tm)

    def body(g_ref, u_ref, w_ref, x_ref, o_ref):
        g = g_ref[...]
        act = _mx(g * _sigmoid(g) * u_ref[...])
        o_ref[...] = x_ref[...] + _dot(act, w_ref[...])

    return pl.pallas_call(
        body,
        name=name,
        grid=(s // tm,),
        in_specs=[
            pl.BlockSpec((tm, f), lambda i: (i, 0)),
            pl.BlockSpec((tm, f), lambda i: (i, 1)),
            _resident((None, f, d), lambda i: (l, 0, 0)),
            pl.BlockSpec((tm, d), lambda i: (i, 0)),
        ],
        out_specs=pl.BlockSpec((tm, d), lambda i: (i, 0)),
        out_shape=jax.ShapeDtypeStruct((s, d), F32),
        compiler_params=_params(),
    )(gu, gu, w_down, x)


def _loss_head(y, target, *, name, tm=512):
    s, d = y.shape
    tm = _tile(s, tm)

    def body(y_ref, t_ref, l_ref, dy_ref):
        @pl.when(pl.program_id(0) == 0)
        def _():
            l_ref[...] = jnp.zeros_like(l_ref)

        err = y_ref[...] - t_ref[...]
        dy_ref[...] = err * (1.0 / d)
        sq = _row_fold(err * err)
        part = sq[:, 0:128]
        for k in range(1, d // 128):
            part = part + sq[:, k * 128:(k + 1) * 128]
        l_ref[...] += part * (0.5 / d)

    blk = pl.BlockSpec((tm, d), lambda i: (i, 0))
    return pl.pallas_call(
        body,
        name=name,
        grid=(s // tm,),
        in_specs=[blk, blk],
        out_specs=[pl.BlockSpec((8, 128), lambda i: (0, 0)), blk],
        out_shape=[jax.ShapeDtypeStruct((8, 128), F32), jax.ShapeDtypeStruct((s, d), F32)],
        compiler_params=_params(),
    )(y, target)


def _matmul_tn(a, b, slot, n_slots, into, *, name, t1=1024, tn=None, tm=1024):
    m, k1 = a.shape
    n = b.shape[1]
    t1 = _tile(k1, t1)
    tn = tn or n
    tm = _tile(m, tm)
    steps = m // tm
    assert n % tn == 0

    def body(a_ref, b_ref, *refs):
        o_ref = refs[-1]

        @pl.when(pl.program_id(2) == 0)
        def _():
            o_ref[...] = jnp.zeros_like(o_ref)

        o_ref[...] += _dot_tn(a_ref[...], b_ref[...])

    return pl.pallas_call(
        body,
        name=name,
        grid=(k1 // t1, n // tn, steps),
        in_specs=[
            pl.BlockSpec((tm, t1), lambda i, j, k: (k, i)),
            pl.BlockSpec((tm, tn), lambda i, j, k: (k, j)),
        ] + ([] if into is None else [ANY]),
        out_specs=pl.BlockSpec((None, t1, tn), lambda i, j, k: (slot, i, j)),
        out_shape=jax.ShapeDtypeStruct((n_slots, k1, n), F32),
        input_output_aliases={} if into is None else {2: 0},
        compiler_params=_params(),
    )(a, b, *([] if into is None else [into]))


def _matmul_nt_normbwd(dp, w, l, x, g, dres, *, name, tm=1024, tk=None):
    s, d = x.shape
    k = w.shape[2]
    tm = _tile(s, tm)
    tk = tk or k
    steps = k // tk
    assert k % tk == 0

    def body(dp_ref, w_ref, x_ref, g_ref, dres_ref, dx_ref, dg_ref, acc_ref):
        kk = pl.program_id(1)

        @pl.when(jnp.logical_and(pl.program_id(0) == 0, kk == 0))
        def _():
            dg_ref[...] = jnp.zeros_like(dg_ref)

        @pl.when(kk == 0)
        def _():
            acc_ref[...] = jnp.zeros_like(acc_ref)

        acc_ref[...] += _dot_nt(dp_ref[...], w_ref[...])

        @pl.when(kk == steps - 1)
        def _():
            dh = acc_ref[...]
            xf = x_ref[...]
            r = lax.rsqrt(jnp.mean(xf * xf, axis=-1, keepdims=True) + EPS)
            y = xf * r
            dy = dh * g_ref[...]
            dx_ref[...] = dres_ref[...] + r * (dy - y * jnp.mean(dy * y, axis=-1, keepdims=True))
            dg_ref[...] += _row_fold(dh * y)

    xblk = pl.BlockSpec((tm, d), lambda i, j: (i, 0))
    return pl.pallas_call(
        body,
        name=name,
        grid=(s // tm, steps),
        in_specs=[
            pl.BlockSpec((tm, tk), lambda i, j: (i, j)),
            pl.BlockSpec((None, d, tk), lambda i, j: (l, 0, j)),
            xblk,
            pl.BlockSpec((1, d), lambda i, j: (0, 0)),
            xblk,
        ],
        out_specs=[xblk, pl.BlockSpec((8, d), lambda i, j: (0, 0))],
        out_shape=[jax.ShapeDtypeStruct((s, d), F32), jax.ShapeDtypeStruct((8, d), F32)],
        scratch_shapes=[pltpu.VMEM((tm, d), F32)],
        compiler_params=_params(),
    )(dp, w, x, g, dres)


def _ffn_bwd(dx, gu, w_down, l, *, name, tm=512):
    s, d = dx.shape
    f = w_down.shape[1]
    tm = _tile(s, tm)

    def body(dx_ref, g_ref, u_ref, w_ref, dgu_ref, act_ref, dxb_ref):
        dxb = _mx(dx_ref[...])
        dxb_ref[...] = dxb
        dact = _dot_nt(dxb, w_ref[...])
        g = g_ref[...]
        u = u_ref[...]
        sg = _sigmoid(g)
        silu = g * sg
        act_ref[...] = _mx(silu * u)
        dgu_ref[:, 0:f] = _mx(dact * u * (sg * (1.0 + g * (1.0 - sg))))
        dgu_ref[:, f:2 * f] = _mx(dact * silu)

    fblk = lambda col: pl.BlockSpec((tm, f), lambda i: (i, col))
    dblk = pl.BlockSpec((tm, d), lambda i: (i, 0))
    return pl.pallas_call(
        body,
        name=name,
        grid=(s // tm,),
        in_specs=[dblk, fblk(0), fblk(1), _resident((None, f, d), lambda i: (l, 0, 0))],
        out_specs=[pl.BlockSpec((tm, 2 * f), lambda i: (i, 0)), fblk(0), dblk],
        out_shape=[
            jax.ShapeDtypeStruct((s, 2 * f), MXU_DTYPE),
            jax.ShapeDtypeStruct((s, f), MXU_DTYPE),
            jax.ShapeDtypeStruct((s, d), MXU_DTYPE),
        ],
        compiler_params=_params(),
    )(dx, gu, gu, w_down)


def _merge_bwd(dx, ys, p, b_gate, w_bo, w_o, l, *, gate_col0, name, tm=512):
    s, d = dx.shape
    _, nbr, c, _ = w_bo.shape
    tm = _tile(s, tm)

    def body(dx_ref, ya_ref, yb_ref, yc_ref, g0_ref, g1_ref, g2_ref, bg_ref, wbo_ref, wo_ref,
             dgate_ref, dya_ref, dyb_ref, dyc_ref, dd0_ref, dd1_ref, dd2_ref, dxb_ref, dbg_ref):
        @pl.when(pl.program_id(0) == 0)
        def _():
            dbg_ref[...] = jnp.zeros_like(dbg_ref)

        dxb = _mx(dx_ref[...])
        dxb_ref[...] = dxb
        dmerged = _dot_nt(dxb, wo_ref[...])
        branches = ((ya_ref, g0_ref, dya_ref, dd0_ref), (yb_ref, g1_ref, dyb_ref, dd1_ref), (yc_ref, g2_ref, dyc_ref, dd2_ref))
        for i, (y_ref, g_ref, dy_ref, dd_ref) in enumerate(branches):
            cols = slice(i * d, (i + 1) * d)
            gate = _sigmoid(g_ref[...] + bg_ref[:, cols])
            yd = _dot(y_ref[...], wbo_ref[i])
            dyd = _mx(dmerged * gate)
            dd_ref[...] = dyd
            dy_ref[...] = _dot_nt(dyd, wbo_ref[i])
            dpre = dmerged * yd * gate * (1.0 - gate)
            dgate_ref[:, cols] = _mx(dpre)
            dbg_ref[:, cols] += _row_fold(dpre)

    yblk = pl.BlockSpec((tm, c), lambda i: (i, 0))
    gblk = lambda k: pl.BlockSpec((tm, d), lambda i: (i, gate_col0 + k))
    dblk = pl.BlockSpec((tm, d), lambda i: (i, 0))
    return pl.pallas_call(
        body,
        name=name,
        grid=(s // tm,),
        in_specs=[
            dblk, yblk, yblk, yblk, gblk(0), gblk(1), gblk(2),
            pl.BlockSpec((1, nbr * d), lambda i: (0, 0)),
            _resident((None, nbr, c, d), lambda i: (l, 0, 0, 0)),
            _resident((None, d, d), lambda i: (l, 0, 0)),
        ],
        out_specs=[pl.BlockSpec((tm, nbr * d), lambda i: (i, 0)), yblk, yblk, yblk, dblk, dblk, dblk, dblk,
                   pl.BlockSpec((8, nbr * d), lambda i: (0, 0))],
        out_shape=[jax.ShapeDtypeStruct((s, nbr * d), MXU_DTYPE)] + [jax.ShapeDtypeStruct((s, c), F32)] * 3
        + [jax.ShapeDtypeStruct((s, d), MXU_DTYPE)] * 4 + [jax.ShapeDtypeStruct((8, nbr * d), F32)],
        compiler_params=_params(),
    )(dx, *ys, p, p, p, b_gate, w_bo, w_o)


def _conv_bwd(p, dya, conv_w, conv_b, *, name, tm=512):
    s = p.shape[0]
    c = conv_w.shape[1]
    tm = _tile(s, tm)
    hb = tm // 8
    last = s // tm - 1

    def body(bg_ref, cg_ref, xa_ref, cgp_ref, xap_ref, dy_ref, dyn_ref, bgn_ref, w_ref, b_ref,
             dp_ref, dw_ref):
        i = pl.program_id(0)

        @pl.when(i == 0)
        def _():
            dw_ref[...] = jnp.zeros_like(dw_ref)

        cg = cg_ref[...]
        xa = xa_ref[...]
        u = cg * xa
        prev = jnp.where(i == 0, 0.0, cgp_ref[...] * xap_ref[...])
        um1, um2 = _conv_taps(u, prev)
        w = w_ref[...]
        y = b_ref[...] + w[0:1, :] * um2 + w[1:2, :] * um1 + w[2:3, :] * u
        dya = dy_ref[...]
        dp_ref[:, 0:c] = _mx(dya * y)
        dyv = dya * bg_ref[...]
        nxt = jnp.where(i == last, 0.0, dyn_ref[...] * bgn_ref[...])
        row = lax.broadcasted_iota(jnp.int32, dyv.shape, 0)
        dp1 = jnp.where(row == tm - 1, nxt[0:1, :], pltpu.roll(dyv, tm - 1, 0))
        dp2 = pltpu.roll(dyv, tm - 2, 0)
        dp2 = jnp.where(row == tm - 2, nxt[0:1, :], jnp.where(row == tm - 1, nxt[1:2, :], dp2))
        du = w[2:3, :] * dyv + w[1:2, :] * dp1 + w[0:1, :] * dp2
        dp_ref[:, c:2 * c] = _mx(du * xa)
        dp_ref[:, 2 * c:3 * c] = _mx(du * cg)
        dw_ref[0] += _row_fold(dyv * um2)
        dw_ref[1] += _row_fold(dyv * um1)
        dw_ref[2] += _row_fold(dyv * u)
        dw_ref[3] += _row_fold(dyv)

    blk = lambda col: pl.BlockSpec((tm, c), lambda i: (i, col))
    halo = lambda col: pl.BlockSpec((8, c), lambda i: (jnp.maximum(i * hb - 1, 0), col))
    nhalo = lambda col: pl.BlockSpec((8, c), lambda i: (jnp.minimum((i + 1) * hb, s // 8 - 1), col))
    return pl.pallas_call(
        body,
        name=name,
        grid=(s // tm,),
        in_specs=[blk(0), blk(1), blk(2), halo(1), halo(2), blk(0), nhalo(0), nhalo(0),
                  pl.BlockSpec((CONV_K, c), lambda i: (0, 0)), pl.BlockSpec((1, c), lambda i: (0, 0))],
        out_specs=[pl.BlockSpec((tm, 3 * c), lambda i: (i, 0)), pl.BlockSpec((4, 8, c), lambda i: (0, 0, 0))],
        out_shape=[jax.ShapeDtypeStruct((s, 3 * c), MXU_DTYPE), jax.ShapeDtypeStruct((4, 8, c), F32)],
        compiler_params=_params(),
    )(p, p, p, p, p, dya, dya, p, conv_w, conv_b)


def _sgu_bwd(p, dyb, ln_g, ln_b, wtril, wtril_t, bias_full, *, col0, name, tm=512):
    s = p.shape[0]
    c = ln_g.shape[1]
    groups = c // CHUNK
    tm = _tile(s, tm)

    def body(u_ref, v_ref, dy_ref, g_ref, b_ref, w_ref, wt_ref, bias_ref, du_ref, dv_ref, dln_ref, dw_ref, dbias_ref,
             dvn_ref):
        @pl.when(pl.program_id(0) == 0)
        def _():
            dln_ref[...] = jnp.zeros_like(dln_ref)
            dw_ref[...] = jnp.zeros_like(dw_ref)
            dbias_ref[...] = jnp.zeros_like(dbias_ref)

        sv = v_ref[...]
        xhat, r = _layernorm_stats(_gelu(sv))
        vb = _mx(xhat * g_ref[...] + b_ref[...])
        for n in range(tm // CHUNK):
            rows = slice(n * CHUNK, (n + 1) * CHUNK)
            for gi in range(groups):
                cols = slice(gi * CHUNK, (gi + 1) * CHUNK)
                su = u_ref[rows, cols]
                dy = dy_ref[rows, cols]
                vblk = vb[rows, cols]
                mixed = _dot(w_ref[gi], vblk) + bias_ref[:, cols]
                du_ref[rows, cols] = _mx(dy * mixed * _gelu_grad(su))
                dmixed = dy * _gelu(su)
                dmb = _mx(dmixed)
                dvn_ref[rows, cols] = _dot(wt_ref[gi], dmb)
                dw_ref[gi] += _dot_nt(dmb, vblk)
                dbias_ref[:, cols] += dmixed
        dvn = dvn_ref[...]
        dln_ref[0] += _row_fold(dvn * xhat)
        dln_ref[1] += _row_fold(dvn)
        dxh = dvn * g_ref[...]
        dgv = r * (dxh - jnp.mean(dxh, axis=-1, keepdims=True) - xhat * jnp.mean(dxh * xhat, axis=-1, keepdims=True))
        dv_ref[...] = _mx(dgv * _gelu_grad(sv))

    blk = lambda col: pl.BlockSpec((tm, c), lambda i: (i, col))
    vec = pl.BlockSpec((1, c), lambda i: (0, 0))
    wspec = pl.BlockSpec((groups, CHUNK, CHUNK), lambda i: (0, 0, 0))
    return pl.pallas_call(
        body,
        name=name,
        grid=(s // tm,),
        in_specs=[blk(col0), blk(col0 + 1), blk(0), vec, vec, wspec, wspec, pl.BlockSpec((CHUNK, c), lambda i: (0, 0))],
        out_specs=[blk(0), blk(0), pl.BlockSpec((2, 8, c), lambda i: (0, 0, 0)), wspec,
                   pl.BlockSpec((CHUNK, c), lambda i: (0, 0))],
        out_shape=[jax.ShapeDtypeStruct((s, c), MXU_DTYPE), jax.ShapeDtypeStruct((s, c), MXU_DTYPE),
                   jax.ShapeDtypeStruct((2, 8, c), F32), jax.ShapeDtypeStruct((groups, CHUNK, CHUNK), F32),
                   jax.ShapeDtypeStruct((CHUNK, c), F32)],
        scratch_shapes=[pltpu.VMEM((tm, c), F32)],
        compiler_params=_params(),
    )(p, p, dyb, ln_g, ln_b, wtril, wtril_t, bias_full)


def _att_bwd(k_hm, kt_hm, v_hm, qt_hm, dot_hm, runs, later, earlier, *, name, hg=2):
    h, nb, t, d = k_hm.shape
    s = nb * t
    nbp = runs.shape[2]
    scale = 1.0 / math.sqrt(d)
    assert h % hg == 0

    def body(k_ref, kt_ref, v_ref, qt_ref, dot_ref, runs_ref, later_ref, earlier_ref, dqt_ref, dkt_ref, dvt_ref):
        i = pl.program_id(1)

        @pl.when(i == 0)
        def _():
            dkt_ref[...] = jnp.zeros_like(dkt_ref)
            dvt_ref[...] = jnp.zeros_like(dvt_ref)

        q_ts = [qt_ref[g] for g in range(hg)]
        do_ts = [dot_ref[g] for g in range(hg)]

        best = runs_ref[0]
        for g in range(1, hg):
            best = jnp.maximum(best, runs_ref[g])
        row = lax.broadcasted_iota(jnp.int32, (nbp, 1), 0)
        counts = jnp.logical_and(jnp.max(best, axis=1, keepdims=True) > EXP_UNDERFLOW, row < i)
        seen = jnp.sum(counts.astype(jnp.int32))
        zeros_row = tuple(jnp.zeros((1, t), F32) for _ in range(hg))

        def up(j, carry):
            gsums, dqts = carry
            mask = _att_mask(t, j * t, i * t)
            heads = range(hg)
            zs, es, lbs, l1ms = _att_blocks([k_ref[g, j] for g in heads], q_ts, scale, mask)
            afters = _dot_split_each(later_ref[...], l1ms)
            das = [_dot(v_ref[g, j], do_ts[g]) for g in heads]
            weights = [jnp.where(mask, jnp.exp(lbs[g] + afters[g] + _row_get(runs_ref, g, j)), 0.0) for g in heads]
            grs = [das[g] * weights[g] for g in heads]
            gbefores = _dot_split_each(earlier_ref[...], grs)
            dzs = []
            for g in heads:
                inv = 1.0 / (1.0 + es[g])
                pos = zs[g] >= 0.0
                beta = jnp.where(pos, inv, es[g] * inv)
                omb = jnp.where(pos, es[g] * inv, inv)
                dzs.append(_mx(jnp.where(mask, grs[g] * omb - (gbefores[g] + gsums[g]) * beta, 0.0) * scale))
            new_dqts = [dqts[g] + _dot(kt_ref[g, j], dzs[g]) for g in heads]
            for g in heads:
                dkt_ref[g, j] += _dot_nt(q_ts[g], dzs[g])
            for g in heads:
                dvt_ref[g, j] += _dot_nt(do_ts[g], _mx(weights[g]))
            new_gsums = [gsums[g] + jnp.sum(grs[g], axis=0, keepdims=True) for g in heads]
            return tuple(new_gsums), tuple(new_dqts)

        _, dqts = lax.fori_loop(i - seen, i + 1, up, (zeros_row, tuple(jnp.zeros((d, t), F32) for _ in range(hg))))
        for g in range(hg):
            dqt_ref[g] = dqts[g]

    whole = lambda a, b: pl.BlockSpec((hg, nb, a, b), lambda hh, i: (hh, 0, 0, 0))
    cols = pl.BlockSpec((hg, d, t), lambda hh, i: (hh, 0, i))
    tri = pl.BlockSpec((t, t), lambda hh, i: (0, 0))
    return pl.pallas_call(
        body,
        name=name,
        grid=(h // hg, nb),
        in_specs=[whole(t, d), whole(d, t), whole(t, d), cols, cols,
                  pl.BlockSpec((hg, None, nbp, t), lambda hh, i: (hh, i, 0, 0)), tri, tri],
        out_specs=[cols, whole(d, t), whole(d, t)],
        out_shape=[jax.ShapeDtypeStruct((h, d, s), F32), jax.ShapeDtypeStruct((h, nb, d, t), F32),
                   jax.ShapeDtypeStruct((h, nb, d, t), F32)],
        compiler_params=_params(),
    )(k_hm, kt_hm, v_hm, qt_hm, dot_hm, runs, later, earlier)


def _qkv_bwd(p, dqn, dkn, dv, qg, kg, headavg, *, col0, name, tm=512):
    s = p.shape[0]
    c = qg.shape[1]
    tm = _tile(s, tm)

    def body(q_ref, k_ref, dqn_ref, dkn_ref, dv_ref, qg_ref, kg_ref, avg_ref, dq_ref, dk_ref, dvb_ref, dg_ref):
        @pl.when(pl.program_id(0) == 0)
        def _():
            dg_ref[...] = jnp.zeros_like(dg_ref)

        for n, (src, dn_ref, g_ref, dst) in enumerate(((q_ref, dqn_ref, qg_ref, dq_ref), (k_ref, dkn_ref, kg_ref, dk_ref))):
            xf = src[...]
            r = lax.rsqrt(_head_mean(xf * xf, avg_ref[...]) + EPS)
            y = xf * r
            dn = dn_ref[...]
            dy = dn * g_ref[...]
            dst[...] = _mx(r * (dy - y * _head_mean(dy * y, avg_ref[...])))
            dg_ref[n] += _row_fold(dn * y)
        dvb_ref[...] = _mx(dv_ref[...])

    blk = lambda col: pl.BlockSpec((tm, c), lambda i: (i, col))
    vec = pl.BlockSpec((1, c), lambda i: (0, 0))
    out_shape = jax.ShapeDtypeStruct((s, c), MXU_DTYPE)
    return pl.pallas_call(
        body,
        name=name,
        grid=(s // tm,),
        in_specs=[blk(col0), blk(col0 + 1), blk(0), blk(0), blk(0), vec, vec, pl.BlockSpec((c, c), lambda i: (0, 0))],
        out_specs=[blk(0), blk(0), blk(0), pl.BlockSpec((2, 8, c), lambda i: (0, 0, 0))],
        out_shape=[out_shape, out_shape, out_shape, jax.ShapeDtypeStruct((2, 8, c), F32)],
        compiler_params=_params(),
    )(p, p, dqn, dkn, dv, qg, kg, headavg)


def _place():
    x, y, c = lax.axis_index("x"), lax.axis_index("y"), lax.axis_index("c")
    chips = [(1 - x, y), (x, 1 - y), (1 - x, 1 - y)]
    return x, y, c, chips


def _shard_of(ref, axis, chip, width):
    idx = [slice(None)] * len(ref.shape)
    idx[axis] = pl.ds(chip * width, width)
    return ref.at[tuple(idx)]


def _place_shard(w, axis, chip_idx, dtype, *, name):
    layers = w.shape[0]
    rows, cols = _as_rows(w.shape[1:])
    if axis == len(w.shape) - 2:
        tr = _row_tile(rows, cols, 2)
        out_shape = (layers, rows, cols * N_CHIPS)
        out_spec = pl.BlockSpec((None, tr, cols), lambda l, i, j_ref: (l, i, j_ref[0]))
    else:
        assert axis == 0
        tr = _row_tile(rows, cols, 2)
        per = rows // tr
        out_shape = (layers, rows * N_CHIPS, cols)
        out_spec = pl.BlockSpec((None, tr, cols), lambda l, i, j_ref: (l, j_ref[0] * per + i, 0))
    full = list(w.shape)
    full[1 + axis] *= N_CHIPS

    def body(j_ref, w_ref, o_ref):
        o_ref[...] = w_ref[...].astype(dtype)

    out = pl.pallas_call(
        body,
        name=name,
        grid_spec=pltpu.PrefetchScalarGridSpec(
            num_scalar_prefetch=1,
            grid=(layers, rows // tr),
            in_specs=[pl.BlockSpec((None, tr, cols), lambda l, i, j_ref: (l, i, 0))],
            out_specs=out_spec,
        ),
        out_shape=jax.ShapeDtypeStruct(out_shape, dtype),
        compiler_params=_params(),
    )(chip_idx, w.reshape(layers, rows, cols))
    return out.reshape(full)


def _gather_weights(placed, axes, *, name):
    n = len(placed)
    widths = [pa.shape[1 + ax] // N_CHIPS for pa, ax in zip(placed, axes)]

    def body(*refs):
        outs = refs[n:2 * n]
        send_sem, recv_sem, fsend_sem, frecv_sem = refs[2 * n:]
        x, y, c, chips = _place()
        me = 2 * x + y
        sibling = (x, y, 1 - c)

        def slot(a, layer, chip):
            return _shard_of(outs[a].at[layer], axes[a], chip, widths[a])

        def send(a, k):
            return pltpu.make_async_remote_copy(
                src_ref=slot(a, c, me), dst_ref=slot(a, c, me), send_sem=send_sem.at[a, k], recv_sem=recv_sem.at[a, k],
                device_id=(*chips[k], c), device_id_type=MESH)

        def landed(a, k):
            cx, cy = chips[k]
            blk = slot(a, c, 2 * cx + cy)
            return pltpu.make_async_remote_copy(
                src_ref=blk, dst_ref=blk, send_sem=send_sem.at[a, k], recv_sem=recv_sem.at[a, k],
                device_id=(*chips[k], c), device_id_type=MESH)

        def forward(a, k, layer):
            cx, cy = chips[k]
            blk = slot(a, layer, 2 * cx + cy)
            return pltpu.make_async_remote_copy(
                src_ref=blk, dst_ref=blk, send_sem=fsend_sem.at[a, k], recv_sem=frecv_sem.at[a, k],
                device_id=sibling, device_id_type=MESH)

        for a in range(n):
            for k in range(3):
                send(a, k).start()
        for a in range(n):
            for k in range(3):
                landed(a, k).wait_recv()
                forward(a, k, c).start()
        for a in range(n):
            for k in range(3):
                forward(a, k, 1 - c).wait_recv()
        for a in range(n):
            for k in range(3):
                send(a, k).wait_send()
                forward(a, k, c).wait_send()

    return pl.pallas_call(
        body,
        name=name,
        in_specs=[ANY] * n,
        out_specs=[ANY] * n,
        out_shape=[jax.ShapeDtypeStruct(pa.shape, pa.dtype) for pa in placed],
        input_output_aliases={a: a for a in range(n)},
        scratch_shapes=[pltpu.SemaphoreType.DMA((n, 3))] * 4,
    )(*placed)


def _pair_exchange(grads, *, name):
    n = len(grads)

    def body(*refs):
        ins, outs = refs[:n], refs[n:2 * n]
        send_sem, recv_sem = refs[2 * n:]
        x, y, c, _ = _place()
        copies = [
            pltpu.make_async_remote_copy(
                src_ref=ins[a].at[1 - c], dst_ref=outs[a], send_sem=send_sem.at[a], recv_sem=recv_sem.at[a],
                device_id=(x, y, 1 - c), device_id_type=MESH)
            for a in range(n)
        ]
        for cp in copies:
            cp.start()
        for cp in copies:
            cp.wait()

    return pl.pallas_call(
        body,
        name=name,
        in_specs=[ANY] * n,
        out_specs=[ANY] * n,
        out_shape=[jax.ShapeDtypeStruct(g.shape[1:], g.dtype) for g in grads],
        scratch_shapes=[pltpu.SemaphoreType.DMA((n,)), pltpu.SemaphoreType.DMA((n,))],
    )(*grads)


def _chip_scatter(sums, axes, *, name):
    n = len(sums)
    widths = [sm.shape[ax] // N_CHIPS for sm, ax in zip(sums, axes)]
    out_shapes = []
    for sm, ax, w in zip(sums, axes, widths):
        shp = list(sm.shape)
        shp[ax] = w
        out_shapes.append(jax.ShapeDtypeStruct((3, *shp), sm.dtype))

    def body(*refs):
        ins, outs = refs[:n], refs[n:2 * n]
        send_sem, recv_sem = refs[2 * n:]
        x, y, c, chips = _place()
        copies = []
        for a in range(n):
            for k in range(3):
                cx, cy = chips[k]
                copies.append(pltpu.make_async_remote_copy(
                    src_ref=_shard_of(ins[a], axes[a], 2 * cx + cy, widths[a]), dst_ref=outs[a].at[k],
                    send_sem=send_sem.at[a, k], recv_sem=recv_sem.at[a, k], device_id=(cx, cy, c), device_id_type=MESH))
        for cp in copies:
            cp.start()
        for cp in copies:
            cp.wait()

    return pl.pallas_call(
        body,
        name=name,
        in_specs=[ANY] * n,
        out_specs=[ANY] * n,
        out_shape=out_shapes,
        scratch_shapes=[pltpu.SemaphoreType.DMA((n, 3)), pltpu.SemaphoreType.DMA((n, 3))],
    )(*sums)


def _pair_share(finals, *, name):
    n = len(finals)

    def body(*refs):
        outs = refs[n:2 * n]
        send_sem, recv_sem = refs[2 * n:]
        x, y, c, _ = _place()
        for a in range(n):
            pltpu.make_async_remote_copy(
                src_ref=outs[a].at[c], dst_ref=outs[a].at[c], send_sem=send_sem.at[a], recv_sem=recv_sem.at[a],
                device_id=(x, y, 1 - c), device_id_type=MESH).start()
        for a in range(n):
            pltpu.make_async_remote_copy(
                src_ref=outs[a].at[c], dst_ref=outs[a].at[1 - c], send_sem=send_sem.at[a], recv_sem=recv_sem.at[a],
                device_id=(x, y, 1 - c), device_id_type=MESH).wait()

    return pl.pallas_call(
        body,
        name=name,
        in_specs=[ANY] * n,
        out_specs=[ANY] * n,
        out_shape=[jax.ShapeDtypeStruct(f.shape, f.dtype) for f in finals],
        input_output_aliases={a: a for a in range(n)},
        scratch_shapes=[pltpu.SemaphoreType.DMA((n,))] * 2,
    )(*finals)


def _all_reduce_small(packed, *, name):
    r = packed.shape[0]

    def body(in_ref, out_ref, slots, send_sem, recv_sem):
        x, y, c, _ = _place()
        me = 4 * x + 2 * y + c
        slots[me] = in_ref[...]
        copies = []
        for k in range(1, N_DEV):
            peer = (x ^ (k >> 2), y ^ ((k >> 1) & 1), c ^ (k & 1))
            copies.append(pltpu.make_async_remote_copy(
                src_ref=in_ref, dst_ref=slots.at[me], send_sem=send_sem.at[k - 1], recv_sem=recv_sem.at[k - 1],
                device_id=peer, device_id_type=MESH))
        for cp in copies:
            cp.start()
        for k in range(1, N_DEV):
            pltpu.make_async_remote_copy(
                src_ref=in_ref, dst_ref=slots.at[me ^ k], send_sem=send_sem.at[k - 1], recv_sem=recv_sem.at[k - 1],
                device_id=(x, y, c), device_id_type=MESH).wait()
        total = slots[0]
        for dev in range(1, N_DEV):
            total = total + slots[dev]
        out_ref[...] = total

    return pl.pallas_call(
        body,
        name=name,
        in_specs=[pl.BlockSpec(memory_space=pltpu.VMEM)],
        out_specs=pl.BlockSpec(memory_space=pltpu.VMEM),
        out_shape=jax.ShapeDtypeStruct((r, 128), F32),
        scratch_shapes=[pltpu.VMEM((N_DEV, r, 128), F32), pltpu.SemaphoreType.DMA((N_DEV - 1,)),
                        pltpu.SemaphoreType.DMA((N_DEV - 1,))],
        compiler_params=_params(),
    )(packed)


def _as_rows(shape):
    cols = shape[-1]
    return math.prod(shape[:-1]), cols


ELEMENTWISE_VMEM = 24 * 1024 * 1024


def _row_tile(rows, cols, n_arrays, sublanes=8):
    cap = ELEMENTWISE_VMEM // (n_arrays * 2 * 4 * cols)
    best = None
    for t in range(sublanes, min(rows, cap) + 1, sublanes):
        if rows % t == 0:
            best = t
    assert best is not None, (rows, cols)
    return best


def _pair_sum(g, other, c_idx, *, name):
    rows, cols = _as_rows(other.shape)
    tr = _row_tile(rows, cols, 4, sublanes=16)
    g2 = g.reshape(2, rows, cols)

    def body(c_ref, g_ref, o_ref, out_ref, wire_ref):
        total = g_ref[...] + o_ref[...]
        out_ref[...] = total
        wire_ref[...] = total.astype(WIRE_DTYPE)

    blk = pl.BlockSpec((tr, cols), lambda i, c_ref: (i, 0))
    out, wire = pl.pallas_call(
        body,
        name=name,
        grid_spec=pltpu.PrefetchScalarGridSpec(
            num_scalar_prefetch=1,
            grid=(rows // tr,),
            in_specs=[pl.BlockSpec((None, tr, cols), lambda i, c_ref: (c_ref[0], i, 0)), blk],
            out_specs=[blk, blk],
        ),
        out_shape=[jax.ShapeDtypeStruct((rows, cols), F32), jax.ShapeDtypeStruct((rows, cols), WIRE_DTYPE)],
        compiler_params=_params(),
    )(c_idx, g2, other.reshape(rows, cols))
    return out.reshape(other.shape), wire.reshape(other.shape)


def _chip_sum(mine, got, axis, chip_idx, c_idx, *, name):
    shard_shape = got.shape[1:]
    rows, cols = _as_rows(shard_shape)
    tr = _row_tile(rows, cols, 5, sublanes=16)
    if axis == len(mine.shape) - 1:
        m2 = mine.reshape(rows, cols * N_CHIPS)
        mine_spec = pl.BlockSpec((tr, cols), lambda i, j_ref, c_ref: (i, j_ref[0]))
    else:
        assert axis == 0
        m2 = mine.reshape(N_CHIPS, rows, cols)
        mine_spec = pl.BlockSpec((None, tr, cols), lambda i, j_ref, c_ref: (j_ref[0], i, 0))

    def body(j_ref, c_ref, m_ref, got_ref, out_ref):
        out_ref[...] = ((m_ref[...] + got_ref[0].astype(F32)) + got_ref[1].astype(F32)) + got_ref[2].astype(F32)

    out = pl.pallas_call(
        body,
        name=name,
        grid_spec=pltpu.PrefetchScalarGridSpec(
            num_scalar_prefetch=2,
            grid=(rows // tr,),
            in_specs=[mine_spec, pl.BlockSpec((3, tr, cols), lambda i, j_ref, c_ref: (0, i, 0))],
            out_specs=pl.BlockSpec((None, tr, cols), lambda i, j_ref, c_ref: (c_ref[0], i, 0)),
        ),
        out_shape=jax.ShapeDtypeStruct((2, rows, cols), F32),
        compiler_params=_params(),
    )(chip_idx, c_idx, m2, got.reshape(3, rows, cols))
    return out.reshape((2, *shard_shape))


def _adamw_update(w_ref, g_ref, m_ref, v_ref, d_ref, nm_ref, nv_ref):
    c1 = 1.0 / (1.0 - ADAM_B1 ** ADAM_STEP)
    c2 = 1.0 / (1.0 - ADAM_B2 ** ADAM_STEP)
    gg = g_ref[...]
    nm = ADAM_B1 * m_ref[...] + (1.0 - ADAM_B1) * gg
    nv = ADAM_B2 * v_ref[...] + (1.0 - ADAM_B2) * (gg * gg)
    nm_ref[...] = nm
    nv_ref[...] = nv
    d_ref[...] = -ADAM_LR * ((nm * c1) / (jnp.sqrt(nv * c2) + ADAM_EPS) + ADAM_WD * w_ref[...])


def _adamw_small(ws, gs, ms, vs, *, name):
    n = len(ws)
    views = [(a.shape[0], math.prod(a.shape[1:])) for a in ws]

    def body(*refs):
        w_r, g_r, m_r, v_r = (refs[k * n:(k + 1) * n] for k in range(4))
        d_r, nm_r, nv_r = (refs[(4 + k) * n:(5 + k) * n] for k in range(3))
        for a in range(n):
            _adamw_update(w_r[a], g_r[a], m_r[a], v_r[a], d_r[a], nm_r[a], nv_r[a])

    vmem = pl.BlockSpec(memory_space=pltpu.VMEM)
    flat = lambda arrs: [a.reshape(view) for a, view in zip(arrs, views)]
    outs = pl.pallas_call(
        body,
        name=name,
        in_specs=[vmem] * (4 * n),
        out_specs=[vmem] * (3 * n),
        out_shape=[jax.ShapeDtypeStruct(view, F32) for view in views] * 3,
        compiler_params=_params(),
    )(*flat(ws), *flat(gs), *flat(ms), *flat(vs))
    return [tuple(outs[k * n + a].reshape(ws[a].shape) for k in range(3)) for a in range(n)]


def _adamw(w, g, m, v, *, name):
    shape = w.shape
    rows, cols = _as_rows(shape)
    tr = _row_tile(rows, cols, 7)
    body = functools.partial(_adamw_update)

    blk = pl.BlockSpec((tr, cols), lambda i: (i, 0))
    flat = lambda a: a.reshape(rows, cols)
    outs = pl.pallas_call(
        body,
        name=name,
        grid=(rows // tr,),
        in_specs=[blk] * 4,
        out_specs=[blk] * 3,
        out_shape=[jax.ShapeDtypeStruct((rows, cols), F32)] * 3,
        compiler_params=_params(),
    )(flat(w), flat(g), flat(m), flat(v))
    return tuple(o.reshape(shape) for o in outs)


def _rows_blocked(a, t):
    return a.reshape(a.shape[0] // t, t, HEADS, HEAD_DIM).transpose(2, 0, 1, 3)


def _cols_blocked(a, t):
    return a.reshape(a.shape[0] // t, t, HEADS, HEAD_DIM).transpose(2, 0, 3, 1)


def _cols(a):
    return a.reshape(a.shape[0], HEADS, HEAD_DIM).transpose(1, 2, 0)


def _from_cols(a):
    h, d, s = a.shape
    return a.transpose(2, 0, 1).reshape(s, h * d)


def _from_cols_blocked(a):
    h, nb, d, t = a.shape
    return a.transpose(1, 3, 0, 2).reshape(nb * t, h * d)


def _pack(parts):
    flat = []
    for a in parts:
        v = a.reshape(-1)
        flat.append(jnp.pad(v, (0, (-v.shape[0]) % 1024)))
    return jnp.concatenate(flat).reshape(-1, 128)


def _unpack(packed, shapes):
    flat = packed.reshape(-1)
    out, off = [], 0
    for shp in shapes:
        size = math.prod(shp)
        out.append(flat[off:off + size].reshape(shp))
        off += size + (-size) % 1024
    return out


BIG = ("w_in", "w_branch_out", "w_o", "w_gate_up", "w_down")
BIG_AXIS = {"w_in": 1, "w_branch_out": 2, "w_o": 0, "w_gate_up": 1, "w_down": 0}
SMALL = ("mix_norm_g", "b_gate", "conv_w", "conv_b", "sgu_ln_g", "sgu_ln_b", "sgu_w", "sgu_b", "q_norm_g", "k_norm_g",
         "ffn_norm_g")
ORDER = ("mix_norm_g", "w_in", "b_gate", "conv_w", "conv_b", "sgu_ln_g", "sgu_ln_b", "sgu_w", "sgu_b", "q_norm_g",
         "k_norm_g", "w_branch_out", "w_o", "ffn_norm_g", "w_gate_up", "w_down")


def _layer_forward(x, w, l):
    t = ATT_BLOCK
    c = w["conv_b"].shape[1]
    n_in = w["w_in"].shape[2]
    gate_col0 = (n_in - 3 * x.shape[1]) // x.shape[1]
    tag = f"l{l}"
    p, h = _norm_matmul(x, w["mix_norm_g"][l][None], w["w_in"], l, name=f"in_proj_{tag}", tn=n_in // 8)
    ya = _conv_fwd(p, w["conv_w"][l], w["conv_b"][l][None], name=f"conv_{tag}")
    yb = _sgu_fwd(p, w["sgu_ln_g"][l][None], w["sgu_ln_b"][l][None], w["wtril"][l], w["bias_full"][l], col0=3,
                  name=f"sgu_{tag}")
    qn, kn, vb = _qkv_prep(p, w["qg"][l], w["kg"][l], w["headavg"], col0=5, name=f"qkv_{tag}")
    q_c = _cols(qn)
    k_rb, k_cb = _rows_blocked(kn, t), _cols_blocked(kn, t)
    v_rb, v_cb = _rows_blocked(vb, t), _cols_blocked(vb, t)
    out_t, runs = _att_fwd(k_rb, q_c, v_cb, w["later"], name=f"att_{tag}")
    yc = _from_cols(out_t)
    x1, merged = _merge_fwd((ya, yb, yc), p, w["b_gate"][l][None], w["w_branch_out"], w["w_o"], l, x,
                            gate_col0=gate_col0, name=f"merge_{tag}")
    gu, h2 = _norm_matmul(x1, w["ffn_norm_g"][l][None], w["w_gate_up"], l, name=f"gate_up_{tag}",
                          tn=w["w_gate_up"].shape[2] // 4)
    x2 = _ffn_down(gu, w["w_down"], l, x1, name=f"down_{tag}")
    saved = dict(x=x, p=p, h=h, ya=ya, yb=yb, yc=yc, merged=merged, x1=x1, gu=gu, h2=h2,
                 att=(k_rb, k_cb, v_rb, q_c, runs), gate_col0=gate_col0)
    return x2, saved


def _layer_backward(dx2, w, sv, l, big):
    layers = w["w_in"].shape[0]
    t = ATT_BLOCK
    tag = f"l{l}"
    d = dx2.shape[1]
    c = w["conv_b"].shape[1]
    n_in = w["w_in"].shape[2]
    n_ff = w["w_gate_up"].shape[2]
    g = {}
    dgu, act, dx2b = _ffn_bwd(dx2, sv["gu"], w["w_down"], l, name=f"down_bwd_{tag}")
    big["w_down"] = _matmul_tn(act, dx2b, l, layers, big.get("w_down"), name=f"dw_down_{tag}", t1=act.shape[1] // 2)
    dx1, dg2 = _matmul_nt_normbwd(dgu, w["w_gate_up"], l, sv["x1"], w["ffn_norm_g"][l][None], dx2,
                                  name=f"gate_up_bwd_{tag}", tk=n_ff // 4)
    g["ffn_norm_g"] = jnp.sum(dg2, axis=0)
    big["w_gate_up"] = _matmul_tn(sv["h2"], dgu, l, layers, big.get("w_gate_up"), name=f"dw_gate_up_{tag}", tn=n_ff // 4)
    ys = (sv["ya"], sv["yb"], sv["yc"])
    (dgates, dya, dyb, dyc, dd0, dd1, dd2, dx1b, dbg) = _merge_bwd(
        dx1, ys, sv["p"], w["b_gate"][l][None], w["w_branch_out"], w["w_o"], l, gate_col0=sv["gate_col0"],
        name=f"merge_bwd_{tag}")
    g["b_gate"] = jnp.sum(dbg, axis=0)
    big["w_o"] = _matmul_tn(sv["merged"], dx1b, l, layers, big.get("w_o"), name=f"dw_o_{tag}")
    for i, (y, dd) in enumerate(zip(ys, (dd0, dd1, dd2))):
        big["w_branch_out"] = _matmul_tn(y, dd, len(ys) * l + i, len(ys) * layers, big.get("w_branch_out"),
                                         name=f"dw_bo{i}_{tag}")
    dconv, dwc = _conv_bwd(sv["p"], dya, w["conv_w"][l], w["conv_b"][l][None], name=f"conv_bwd_{tag}")
    dwc = jnp.sum(dwc, axis=1)
    g["conv_w"] = dwc[0:CONV_K]
    g["conv_b"] = dwc[CONV_K]
    dsu, dsv, dln, dws, dbias = _sgu_bwd(sv["p"], dyb, w["sgu_ln_g"][l][None], w["sgu_ln_b"][l][None], w["wtril"][l],
                                         w["wtril_t"][l], w["bias_full"][l], col0=3, name=f"sgu_bwd_{tag}")
    dln = jnp.sum(dln, axis=1)
    g["sgu_ln_g"], g["sgu_ln_b"] = dln[0], dln[1]
    g["sgu_w"] = jnp.where(w["tril"], dws, 0.0)
    g["sgu_b"] = jnp.sum(dbias.reshape(CHUNK, c // CHUNK, CHUNK), axis=2).T
    k_rb, k_cb, v_rb, q_c, runs = sv["att"]
    dqt, dkt, dvt = _att_bwd(k_rb, k_cb, v_rb, q_c, _cols(_mx(dyc)), runs, w["later"], w["earlier"],
                             name=f"att_bwd_{tag}")
    dqn, dkn, dvv = _from_cols(dqt), _from_cols_blocked(dkt), _from_cols_blocked(dvt)
    dq, dk_, dvb, dqkg = _qkv_bwd(sv["p"], dqn, dkn, dvv, w["qg"][l], w["kg"][l], w["headavg"], col0=5,
                                  name=f"qkv_bwd_{tag}")
    dqkg = jnp.sum(dqkg.reshape(2, 8 * HEADS, HEAD_DIM), axis=1)
    g["q_norm_g"], g["k_norm_g"] = dqkg[0], dqkg[1]
    dp = jnp.concatenate([dconv, dsu, dsv, dq, dk_, dvb, dgates], axis=1)
    dx0, dg1 = _matmul_nt_normbwd(dp, w["w_in"], l, sv["x"], w["mix_norm_g"][l][None], dx1, name=f"in_proj_bwd_{tag}",
                                  tk=n_in // 8)
    g["mix_norm_g"] = jnp.sum(dg1, axis=0)
    big["w_in"] = _matmul_tn(sv["h"], dp, l, layers, big.get("w_in"), name=f"dw_in_{tag}", tn=n_in // 4)
    return dx0, g


def kernel(x, mix_norm_g, w_in, b_gate, conv_w, conv_b, sgu_ln_g, sgu_ln_b, sgu_w, sgu_b, q_norm_g, k_norm_g, w_branch_out, w_o, ffn_norm_g, w_gate_up, w_down, loss_target, m_mix_norm_g, m_w_in, m_b_gate, m_conv_w, m_conv_b, m_sgu_ln_g, m_sgu_ln_b, m_sgu_w, m_sgu_b, m_q_norm_g, m_k_norm_g, m_w_branch_out, m_w_o, m_ffn_norm_g, m_w_gate_up, m_w_down, v_mix_norm_g, v_w_in, v_b_gate, v_conv_w, v_conv_b, v_sgu_ln_g, v_sgu_ln_b, v_sgu_w, v_sgu_b, v_q_norm_g, v_k_norm_g, v_w_branch_out, v_w_o, v_ffn_norm_g, v_w_gate_up, v_w_down):
    given = dict(locals())
    params = {n: given[n] for n in ORDER}
    moms = {n: (given["m_" + n], given["v_" + n]) for n in ORDER}
    layers = mix_norm_g.shape[0]
    assert layers == 2, "the exchanges split the work of a chip's two cores by layer"
    xs = x[0]
    target = loss_target[0]
    chip = 2 * lax.axis_index("x") + lax.axis_index("y")
    core = lax.axis_index("c")

    c_idx = core.reshape(1).astype(jnp.int32)
    j_idx = chip.reshape(1).astype(jnp.int32)
    width = conv_w.shape[2]
    placed = [_place_shard(params[n], BIG_AXIS[n], j_idx, MXU_DTYPE, name=f"place_{n}") for n in BIG]
    placed.append(lax.dynamic_update_slice_in_dim(jnp.zeros((layers, CONV_K, width * N_CHIPS), F32), conv_w, chip * width, 2))
    gathered = _gather_weights(placed, [BIG_AXIS[n] for n in BIG] + [1], name="gather_weights")
    w = dict(zip(BIG + ("conv_w",), gathered))
    for n in ("mix_norm_g", "b_gate", "conv_b", "sgu_ln_g", "sgu_ln_b", "ffn_norm_g"):
        w[n] = params[n]
    groups = sgu_w.shape[1]
    tril = jnp.tril(jnp.ones((CHUNK, CHUNK), dtype=bool))
    w["tril"] = tril
    w["wtril"] = _mx(jnp.where(tril, sgu_w, 0.0))
    w["wtril_t"] = w["wtril"].transpose(0, 1, 3, 2)
    w["bias_full"] = jnp.repeat(sgu_b.transpose(0, 2, 1), CHUNK, axis=2)
    w["qg"] = jnp.tile(q_norm_g, (1, HEADS))[:, None, :]
    w["kg"] = jnp.tile(k_norm_g, (1, HEADS))[:, None, :]
    lane = jnp.arange(HEADS * HEAD_DIM) // HEAD_DIM
    w["headavg"] = _mx(jnp.where(lane[:, None] == lane[None, :], 1.0 / HEAD_DIM, 0.0))
    pos = jnp.arange(ATT_BLOCK)
    w["later"] = _mx(jnp.where(pos[None, :] > pos[:, None], 1.0, 0.0))
    w["earlier"] = _mx(jnp.where(pos[None, :] < pos[:, None], 1.0, 0.0))

    saved = []
    act = xs
    for l in range(layers):
        act, sv = _layer_forward(act, w, l)
        saved.append(sv)
    loss_part, dact = _loss_head(act, target, name="loss_head")
    loss = lax.psum(jnp.sum(loss_part), ("x", "y", "c"))
    grads = [None] * layers
    big = {}
    for l in reversed(range(layers)):
        dact, grads[l] = _layer_backward(dact, w, saved[l], l, big)
    grad_x = dact[None]
    local = {n: jnp.stack([grads[l][n] for l in range(layers)]) for n in SMALL}
    for n in BIG:
        local[n] = big[n].reshape(w[n].shape)

    got = _pair_exchange([local[n] for n in BIG], name="grads_pair_exchange")
    pair = [_pair_sum(local[n], o, c_idx, name=f"pair_sum_{n}") for n, o in zip(BIG, got)]
    landed = _chip_scatter([wire for _, wire in pair], [BIG_AXIS[n] for n in BIG], name="grads_chip_scatter")
    final = [_chip_sum(pr, ld, BIG_AXIS[n], j_idx, c_idx, name=f"chip_sum_{n}") for n, (pr, _), ld in zip(BIG, pair, landed)]
    full = dict(zip(BIG, _pair_share(final, name="grads_pair_share")))

    small_shapes = [local[n].shape for n in SMALL]
    summed = _unpack(_all_reduce_small(_pack([local[n] for n in SMALL]), name="grads_all_reduce_small"), small_shapes)
    for n, gsum in zip(SMALL, summed):
        full[n] = gsum
    full["conv_w"] = lax.dynamic_slice_in_dim(full["conv_w"], chip * width, width, axis=2)

    out = {}
    for n in BIG:
        out[n] = _adamw(params[n], full[n], *moms[n], name=f"adamw_{n}")
    small = _adamw_small([params[n] for n in SMALL], [full[n] for n in SMALL], [moms[n][0] for n in SMALL],
                         [moms[n][1] for n in SMALL], name="adamw_small")
    for n, triple in zip(SMALL, small):
        out[n] = triple
    return (loss, grad_x, *[full[n] for n in ORDER], *[out[n][0] for n in ORDER], *[out[n][1] for n in ORDER],
            *[out[n][2] for n in ORDER])
```

```python
import functools
import math

import jax
import jax.numpy as jnp
from jax import lax
from jax.experimental import pallas as pl
from jax.experimental.pallas import tpu as pltpu

F32 = jnp.float32
MXU_DTYPE = jnp.bfloat16
WIRE_DTYPE = jnp.bfloat16

EPS = 1e-6
CONV_K = 3
CHUNK = 128
HEADS = 8
HEAD_DIM = 64
ATT_BLOCK = 128
EXP_UNDERFLOW = -104.0

ADAM_LR = 0.001
ADAM_B1 = 0.9
ADAM_B2 = 0.999
ADAM_EPS = 1e-08
ADAM_WD = 0.01
ADAM_STEP = 10

VMEM_LIMIT = 56 * 1024 * 1024
MESH = pl.DeviceIdType.MESH
N_CHIPS = 4
N_DEV = 8
ANY = pl.BlockSpec(memory_space=pl.ANY)


def _params(**kw):
    return pltpu.CompilerParams(vmem_limit_bytes=VMEM_LIMIT, **kw)


def _mx(v):
    return v.astype(MXU_DTYPE)


def _dot(a, b):
    return lax.dot_general(a, b, (((1,), (0,)), ((), ())), preferred_element_type=F32)


def _dot_nt(a, b):
    return lax.dot_general(a, b, (((1,), (1,)), ((), ())), preferred_element_type=F32)


def _dot_tn(a, b):
    return lax.dot_general(a, b, (((0,), (0,)), ((), ())), preferred_element_type=F32)


def _dot_split(const, v):
    hi = _mx(v)
    lo = _mx(v - hi.astype(F32))
    return _dot(const, hi) + _dot(const, lo)


def _dot_split_r(v, const):
    hi = _mx(v)
    lo = _mx(v - hi.astype(F32))
    return _dot(hi, const) + _dot(lo, const)


def _sigmoid(x):
    return 1.0 / (1.0 + jnp.exp(-x))


_INV_SQRT2 = 1.0 / math.sqrt(2.0)
_INV_SQRT2PI = 1.0 / math.sqrt(2.0 * math.pi)


def _gelu(x):
    return 0.5 * x * (1.0 + lax.erf(x * _INV_SQRT2))


def _gelu_grad(x):
    return 0.5 * (1.0 + lax.erf(x * _INV_SQRT2)) + x * jnp.exp(-0.5 * x * x) * _INV_SQRT2PI


def _row_fold(v):
    m, n = v.shape
    return jnp.sum(v.reshape(m // 8, 8, n), axis=0)


def _tile(m, pref):
    t = min(m, pref)
    while m % t:
        t //= 2
    return t


def _resident(block_shape, index_map):
    return pl.BlockSpec(block_shape, index_map, pipeline_mode=pl.Buffered(1))


def _norm_matmul(x, g, w, l, *, name, tm=1024, tn=None):
    s, d = x.shape
    n = w.shape[2]
    tm = _tile(s, tm)
    tn = tn or n
    assert n % tn == 0

    def body(x_ref, g_ref, w_ref, p_ref, h_ref):
        @pl.when(pl.program_id(1) == 0)
        def _():
            xf = x_ref[...]
            r = lax.rsqrt(jnp.mean(xf * xf, axis=-1, keepdims=True) + EPS)
            h_ref[...] = _mx(xf * r * g_ref[...])

        p_ref[...] = _dot(h_ref[...], w_ref[...])

    return pl.pallas_call(
        body,
        name=name,
        grid=(s // tm, n // tn),
        in_specs=[
            pl.BlockSpec((tm, d), lambda i, j: (i, 0)),
            pl.BlockSpec((1, d), lambda i, j: (0, 0)),
            pl.BlockSpec((None, d, tn), lambda i, j: (l, 0, j)),
        ],
        out_specs=[
            pl.BlockSpec((tm, tn), lambda i, j: (i, j)),
            pl.BlockSpec((tm, d), lambda i, j: (i, 0)),
        ],
        out_shape=[jax.ShapeDtypeStruct((s, n), F32), jax.ShapeDtypeStruct((s, d), MXU_DTYPE)],
        compiler_params=_params(),
    )(x, g, w)


def _conv_taps(u, prev):
    row = lax.broadcasted_iota(jnp.int32, u.shape, 0)
    um1 = jnp.where(row == 0, prev[7:8, :], pltpu.roll(u, 1, 0))
    um2 = pltpu.roll(u, 2, 0)
    um2 = jnp.where(row == 0, prev[6:7, :], jnp.where(row == 1, prev[7:8, :], um2))
    return um1, um2


def _conv_fwd(p, conv_w, conv_b, *, name, tm=512):
    s = p.shape[0]
    c = conv_w.shape[1]
    tm = _tile(s, tm)
    hb = tm // 8

    def body(bg_ref, cg_ref, xa_ref, cgp_ref, xap_ref, w_ref, b_ref, y_ref):
        first = pl.program_id(0) == 0
        u = cg_ref[...] * xa_ref[...]
        prev = jnp.where(first, 0.0, cgp_ref[...] * xap_ref[...])
        um1, um2 = _conv_taps(u, prev)
        w = w_ref[...]
        y = b_ref[...] + w[0:1, :] * um2 + w[1:2, :] * um1 + w[2:3, :] * u
        y_ref[...] = _mx(bg_ref[...] * y)

    halo = lambda col: pl.BlockSpec((8, c), lambda i: (jnp.maximum(i * hb - 1, 0), col))
    return pl.pallas_call(
        body,
        name=name,
        grid=(s // tm,),
        in_specs=[
            pl.BlockSpec((tm, c), lambda i: (i, 0)),
            pl.BlockSpec((tm, c), lambda i: (i, 1)),
            pl.BlockSpec((tm, c), lambda i: (i, 2)),
            halo(1),
            halo(2),
            pl.BlockSpec((CONV_K, c), lambda i: (0, 0)),
            pl.BlockSpec((1, c), lambda i: (0, 0)),
        ],
        out_specs=pl.BlockSpec((tm, c), lambda i: (i, 0)),
        out_shape=jax.ShapeDtypeStruct((s, c), MXU_DTYPE),
        compiler_params=_params(),
    )(p, p, p, p, p, conv_w, conv_b)


def _layernorm_stats(x):
    mu = jnp.mean(x, axis=-1, keepdims=True)
    xc = x - mu
    r = lax.rsqrt(jnp.mean(xc * xc, axis=-1, keepdims=True) + EPS)
    return xc * r, r


def _sgu_fwd(p, ln_g, ln_b, wtril, bias_full, *, col0, name, tm=512):
    s = p.shape[0]
    c = ln_g.shape[1]
    groups = c // CHUNK
    tm = _tile(s, tm)

    def body(u_ref, v_ref, g_ref, b_ref, w_ref, bias_ref, y_ref):
        vn, _ = _layernorm_stats(_gelu(v_ref[...]))
        vb = _mx(vn * g_ref[...] + b_ref[...])
        for n in range(tm // CHUNK):
            rows = slice(n * CHUNK, (n + 1) * CHUNK)
            for gi in range(groups):
                cols = slice(gi * CHUNK, (gi + 1) * CHUNK)
                mixed = _dot(w_ref[gi], vb[rows, cols]) + bias_ref[:, cols]
                y_ref[rows, cols] = _mx(_gelu(u_ref[rows, cols]) * mixed)

    return pl.pallas_call(
        body,
        name=name,
        grid=(s // tm,),
        in_specs=[
            pl.BlockSpec((tm, c), lambda i: (i, col0)),
            pl.BlockSpec((tm, c), lambda i: (i, col0 + 1)),
            pl.BlockSpec((1, c), lambda i: (0, 0)),
            pl.BlockSpec((1, c), lambda i: (0, 0)),
            pl.BlockSpec((groups, CHUNK, CHUNK), lambda i: (0, 0, 0)),
            pl.BlockSpec((CHUNK, c), lambda i: (0, 0)),
        ],
        out_specs=pl.BlockSpec((tm, c), lambda i: (i, 0)),
        out_shape=jax.ShapeDtypeStruct((s, c), MXU_DTYPE),
        compiler_params=_params(),
    )(p, p, ln_g, ln_b, wtril, bias_full)


def _head_mean(v, headavg):
    return _dot_split_r(v, headavg)


def _qkv_prep(p, qg, kg, headavg, *, col0, name, tm=512):
    s = p.shape[0]
    c = qg.shape[1]
    tm = _tile(s, tm)

    def body(q_ref, k_ref, v_ref, qg_ref, kg_ref, avg_ref, qn_ref, kn_ref, vb_ref):
        for src, g_ref, dst in ((q_ref, qg_ref, qn_ref), (k_ref, kg_ref, kn_ref)):
            xf = src[...]
            r = lax.rsqrt(_head_mean(xf * xf, avg_ref[...]) + EPS)
            dst[...] = _mx(xf * r * g_ref[...])
        vb_ref[...] = _mx(v_ref[...])

    blk = lambda col: pl.BlockSpec((tm, c), lambda i: (i, col))
    vec = pl.BlockSpec((1, c), lambda i: (0, 0))
    out = pl.BlockSpec((tm, c), lambda i: (i, 0))
    return pl.pallas_call(
        body,
        name=name,
        grid=(s // tm,),
        in_specs=[blk(col0), blk(col0 + 1), blk(col0 + 2), vec, vec, pl.BlockSpec((c, c), lambda i: (0, 0))],
        out_specs=[out, out, out],
        out_shape=[jax.ShapeDtypeStruct((s, c), MXU_DTYPE)] * 3,
        compiler_params=_params(),
    )(p, p, p, qg, kg, headavg)


def _att_mask(t, key0, qry0):
    kpos = key0 + lax.broadcasted_iota(jnp.int32, (t, t), 0)
    qpos = qry0 + lax.broadcasted_iota(jnp.int32, (t, t), 1)
    return kpos < qpos


def _att_blocks(k_blks, q_ts, scale, mask):
    zs = [(_dot_tn(k, q) if k.shape[0] == q.shape[0] else _dot(k, q)) * scale for k, q in zip(k_blks, q_ts)]
    es = [jnp.exp(-jnp.abs(z)) for z in zs]
    lbs = [jnp.minimum(z, 0.0) - jnp.log1p(e) for z, e in zip(zs, es)]
    l1ms = [jnp.where(mask, lb - z, 0.0) for lb, z in zip(lbs, zs)]
    return zs, es, lbs, l1ms


def _dot_split_each(const, vs):
    his = [_mx(v) for v in vs]
    los = [_mx(v - hi.astype(F32)) for v, hi in zip(vs, his)]
    tops = [_dot(const, hi) for hi in his]
    return [top + _dot(const, lo) for top, lo in zip(tops, los)]


def _max_over(runs):
    m = runs[0]
    for r in runs[1:]:
        m = jnp.maximum(m, r)
    return jnp.max(m)


NOT_SEEN = -1e30


def _row_put(ref, g, j, row):
    j8 = pl.multiple_of((j // 8) * 8, 8)
    blk = ref[g, pl.ds(j8, 8), :]
    sub = lax.broadcasted_iota(jnp.int32, blk.shape, 0)
    ref[g, pl.ds(j8, 8), :] = jnp.where(sub == j - j8, row, blk)


def _row_get(ref, g, j):
    j8 = pl.multiple_of((j // 8) * 8, 8)
    blk = ref[g, pl.ds(j8, 8), :]
    sub = lax.broadcasted_iota(jnp.int32, blk.shape, 0)
    return jnp.sum(jnp.where(sub == j - j8, blk, 0.0), axis=0, keepdims=True)


def _att_fwd(k_hm, qt_hm, vt_hm, later, *, name, hg=8):
    h, nb, d, t = k_hm.shape
    s = nb * t
    nbp = -(-nb // 8) * 8
    scale = 1.0 / math.sqrt(d)
    assert h % hg == 0

    def body(k_ref, qt_ref, vt_ref, later_ref, o_ref, runs_ref):
        i = pl.program_id(1)
        q_ts = [qt_ref[g] for g in range(hg)]
        runs_ref[...] = jnp.full(runs_ref.shape, NOT_SEEN, F32)

        def cond(carry):
            j, _, _, rmax = carry
            return jnp.logical_and(j >= 0, rmax > EXP_UNDERFLOW)

        def step(carry):
            j, runs, accs, _ = carry
            mask = _att_mask(t, j * t, i * t)
            heads = range(hg)
            for g in heads:
                _row_put(runs_ref, g, j, runs[g])
            _, _, lbs, l1ms = _att_blocks([k_ref[g, j] for g in heads], q_ts, scale, mask)
            afters = _dot_split_each(later_ref[...], l1ms)
            weights = [_mx(jnp.where(mask, jnp.exp(lbs[g] + afters[g] + runs[g]), 0.0)) for g in heads]
            new_accs = [accs[g] + _dot(vt_ref[g, j], weights[g]) for g in heads]
            new_runs = [runs[g] + jnp.sum(l1ms[g], axis=0, keepdims=True) for g in heads]
            return j - 1, tuple(new_runs), tuple(new_accs), _max_over(new_runs)

        init = (i, tuple(jnp.zeros((1, t), F32) for _ in range(hg)), tuple(jnp.zeros((d, t), F32) for _ in range(hg)),
                jnp.float32(0.0))
        _, _, accs, _ = lax.while_loop(cond, step, init)
        for g in range(hg):
            o_ref[g] = _mx(accs[g])

    return pl.pallas_call(
        body,
        name=name,
        grid=(h // hg, nb),
        in_specs=[
            pl.BlockSpec((hg, nb, d, t), lambda hh, i: (hh, 0, 0, 0)),
            pl.BlockSpec((hg, d, t), lambda hh, i: (hh, 0, i)),
            pl.BlockSpec((hg, nb, d, t), lambda hh, i: (hh, 0, 0, 0)),
            pl.BlockSpec((t, t), lambda hh, i: (0, 0)),
        ],
        out_specs=[pl.BlockSpec((hg, d, t), lambda hh, i: (hh, 0, i)),
                   pl.BlockSpec((hg, None, nbp, t), lambda hh, i: (hh, i, 0, 0))],
        out_shape=[jax.ShapeDtypeStruct((h, d, s), MXU_DTYPE), jax.ShapeDtypeStruct((h, nb, nbp, t), F32)],
        compiler_params=_params(),
    )(k_hm, qt_hm, vt_hm, later)


def _merge_fwd(ys, p, b_gate, w_bo, w_o, l, x, *, gate_col0, name, tm=512):
    s, d = x.shape
    _, nbr, c, _ = w_bo.shape
    tm = _tile(s, tm)

    def body(ya_ref, yb_ref, yc_ref, g0_ref, g1_ref, g2_ref, bg_ref, wbo_ref, wo_ref, x_ref, x1_ref, m_ref):
        merged = jnp.zeros((tm, d), F32)
        for i, (y_ref, g_ref) in enumerate(((ya_ref, g0_ref), (yb_ref, g1_ref), (yc_ref, g2_ref))):
            gate = _sigmoid(g_ref[...] + bg_ref[:, i * d:(i + 1) * d])
            merged = merged + gate * _dot(y_ref[...], wbo_ref[i])
        mb = _mx(merged)
        m_ref[...] = mb
        x1_ref[...] = x_ref[...] + _dot(mb, wo_ref[...])

    yblk = pl.BlockSpec((tm, c), lambda i: (i, 0))
    gblk = lambda k: pl.BlockSpec((tm, d), lambda i: (i, gate_col0 + k))
    xblk = pl.BlockSpec((tm, d), lambda i: (i, 0))
    return pl.pallas_call(
        body,
        name=name,
        grid=(s // tm,),
        in_specs=[
            yblk, yblk, yblk, gblk(0), gblk(1), gblk(2),
            pl.BlockSpec((1, nbr * d), lambda i: (0, 0)),
            _resident((None, nbr, c, d), lambda i: (l, 0, 0, 0)),
            _resident((None, d, d), lambda i: (l, 0, 0)),
            xblk,
        ],
        out_specs=[xblk, xblk],
        out_shape=[jax.ShapeDtypeStruct((s, d), F32), jax.ShapeDtypeStruct((s, d), MXU_DTYPE)],
        compiler_params=_params(),
    )(*ys, p, p, p, b_gate, w_bo, w_o, x)


def _ffn_down(gu, w_down, l, x, *, name, tm=512):
    s, d = x.shape
    f = w_down.shape[1]
    tm = _tile(s, tm)

    def body(g_ref, u_ref, w_ref, x_ref, o_ref):
        g = g_ref[...]
        act = _mx(g * _sigmoid(g) * u_ref[...])
        o_ref[...] = x_ref[...] + _dot(act, w_ref[...])

    return pl.pallas_call(
        body,
        name=name,
        grid=(s // tm,),
        in_specs=[
            pl.BlockSpec((tm, f), lambda i: (i, 0)),
            pl.BlockSpec((tm, f), lambda i: (i, 1)),
            _resident((None, f, d), lambda i: (l, 0, 0)),
            pl.BlockSpec((tm, d), lambda i: (i, 0)),
        ],
        out_specs=pl.BlockSpec((tm, d), lambda i: (i, 0)),
        out_shape=jax.ShapeDtypeStruct((s, d), F32),
        compiler_params=_params(),
    )(gu, gu, w_down, x)


def _loss_head(y, target, *, name, tm=512):
    s, d = y.shape
    tm = _tile(s, tm)

    def body(y_ref, t_ref, l_ref, dy_ref):
        @pl.when(pl.program_id(0) == 0)
        def _():
            l_ref[...] = jnp.zeros_like(l_ref)

        err = y_ref[...] - t_ref[...]
        dy_ref[...] = err * (1.0 / d)
        sq = _row_fold(err * err)
        part = sq[:, 0:128]
        for k in range(1, d // 128):
            part = part + sq[:, k * 128:(k + 1) * 128]
        l_ref[...] += part * (0.5 / d)

    blk = pl.BlockSpec((tm, d), lambda i: (i, 0))
    return pl.pallas_call(
        body,
        name=name,
        grid=(s // tm,),
        in_specs=[blk, blk],
        out_specs=[pl.BlockSpec((8, 128), lambda i: (0, 0)), blk],
        out_shape=[jax.ShapeDtypeStruct((8, 128), F32), jax.ShapeDtypeStruct((s, d), F32)],
        compiler_params=_params(),
    )(y, target)


def _matmul_tn(a, b, slot, n_slots, into, *, name, t1=1024, tn=None, tm=1024):
    m, k1 = a.shape
    n = b.shape[1]
    t1 = _tile(k1, t1)
    tn = tn or n
    tm = _tile(m, tm)
    steps = m // tm
    assert n % tn == 0

    def body(a_ref, b_ref, *refs):
        o_ref = refs[-1]

        @pl.when(pl.program_id(2) == 0)
        def _():
            o_ref[...] = jnp.zeros_like(o_ref)

        o_ref[...] += _dot_tn(a_ref[...], b_ref[...])

    return pl.pallas_call(
        body,
        name=name,
        grid=(k1 // t1, n // tn, steps),
        in_specs=[
            pl.BlockSpec((tm, t1), lambda i, j, k: (k, i)),
            pl.BlockSpec((tm, tn), lambda i, j, k: (k, j)),
        ] + ([] if into is None else [ANY]),
        out_specs=pl.BlockSpec((None, t1, tn), lambda i, j, k: (slot, i, j)),
        out_shape=jax.ShapeDtypeStruct((n_slots, k1, n), F32),
        input_output_aliases={} if into is None else {2: 0},
        compiler_params=_params(),
    )(a, b, *([] if into is None else [into]))


def _matmul_nt_normbwd(dp, w, l, x, g, dres, *, name, tm=1024, tk=None):
    s, d = x.shape
    k = w.shape[2]
    tm = _tile(s, tm)
    tk = tk or k
    steps = k // tk
    assert k % tk == 0

    def body(dp_ref, w_ref, x_ref, g_ref, dres_ref, dx_ref, dg_ref, acc_ref):
        kk = pl.program_id(1)

        @pl.when(jnp.logical_and(pl.program_id(0) == 0, kk == 0))
        def _():
            dg_ref[...] = jnp.zeros_like(dg_ref)

        @pl.when(kk == 0)
        def _():
            acc_ref[...] = jnp.zeros_like(acc_ref)

        acc_ref[...] += _dot_nt(dp_ref[...], w_ref[...])

        @pl.when(kk == steps - 1)
        def _():
            dh = acc_ref[...]
            xf = x_ref[...]
            r = lax.rsqrt(jnp.mean(xf * xf, axis=-1, keepdims=True) + EPS)
            y = xf * r
            dy = dh * g_ref[...]
            dx_ref[...] = dres_ref[...] + r * (dy - y * jnp.mean(dy * y, axis=-1, keepdims=True))
            dg_ref[...] += _row_fold(dh * y)

    xblk = pl.BlockSpec((tm, d), lambda i, j: (i, 0))
    return pl.pallas_call(
        body,
        name=name,
        grid=(s // tm, steps),
        in_specs=[
            pl.BlockSpec((tm, tk), lambda i, j: (i, j)),
            pl.BlockSpec((None, d, tk), lambda i, j: (l, 0, j)),
            xblk,
            pl.BlockSpec((1, d), lambda i, j: (0, 0)),
            xblk,
        ],
        out_specs=[xblk, pl.BlockSpec((8, d), lambda i, j: (0, 0))],
        out_shape=[jax.ShapeDtypeStruct((s, d), F32), jax.ShapeDtypeStruct((8, d), F32)],
        scratch_shapes=[pltpu.VMEM((tm, d), F32)],
        compiler_params=_params(),
    )(dp, w, x, g, dres)


def _ffn_bwd(dx, gu, w_down, l, *, name, tm=512):
    s, d = dx.shape
    f = w_down.shape[1]
    tm = _tile(s, tm)

    def body(dx_ref, g_ref, u_ref, w_ref, dgu_ref, act_ref, dxb_ref):
        dxb = _mx(dx_ref[...])
        dxb_ref[...] = dxb
        dact = _dot_nt(dxb, w_ref[...])
        g = g_ref[...]
        u = u_ref[...]
        sg = _sigmoid(g)
        silu = g * sg
        act_ref[...] = _mx(silu * u)
        dgu_ref[:, 0:f] = _mx(dact * u * (sg * (1.0 + g * (1.0 - sg))))
        dgu_ref[:, f:2 * f] = _mx(dact * silu)

    fblk = lambda col: pl.BlockSpec((tm, f), lambda i: (i, col))
    dblk = pl.BlockSpec((tm, d), lambda i: (i, 0))
    return pl.pallas_call(
        body,
        name=name,
        grid=(s // tm,),
        in_specs=[dblk, fblk(0), fblk(1), _resident((None, f, d), lambda i: (l, 0, 0))],
        out_specs=[pl.BlockSpec((tm, 2 * f), lambda i: (i, 0)), fblk(0), dblk],
        out_shape=[
            jax.ShapeDtypeStruct((s, 2 * f), MXU_DTYPE),
            jax.ShapeDtypeStruct((s, f), MXU_DTYPE),
            jax.ShapeDtypeStruct((s, d), MXU_DTYPE),
        ],
        compiler_params=_params(),
    )(dx, gu, gu, w_down)


def _merge_bwd(dx, ys, p, b_gate, w_bo, w_o, l, *, gate_col0, name, tm=512):
    s, d = dx.shape
    _, nbr, c, _ = w_bo.shape
    tm = _tile(s, tm)

    def body(dx_ref, ya_ref, yb_ref, yc_ref, g0_ref, g1_ref, g2_ref, bg_ref, wbo_ref, wo_ref,
             dgate_ref, dya_ref, dyb_ref, dyc_ref, dd0_ref, dd1_ref, dd2_ref, dxb_ref, dbg_ref):
        @pl.when(pl.program_id(0) == 0)
        def _():
            dbg_ref[...] = jnp.zeros_like(dbg_ref)

        dxb = _mx(dx_ref[...])
        dxb_ref[...] = dxb
        dmerged = _dot_nt(dxb, wo_ref[...])
        branches = ((ya_ref, g0_ref, dya_ref, dd0_ref), (yb_ref, g1_ref, dyb_ref, dd1_ref), (yc_ref, g2_ref, dyc_ref, dd2_ref))
        for i, (y_ref, g_ref, dy_ref, dd_ref) in enumerate(branches):
            cols = slice(i * d, (i + 1) * d)
            gate = _sigmoid(g_ref[...] + bg_ref[:, cols])
            yd = _dot(y_ref[...], wbo_ref[i])
            dyd = _mx(dmerged * gate)
            dd_ref[...] = dyd
            dy_ref[...] = _dot_nt(dyd, wbo_ref[i])
            dpre = dmerged * yd * gate * (1.0 - gate)
            dgate_ref[:, cols] = _mx(dpre)
            dbg_ref[:, cols] += _row_fold(dpre)

    yblk = pl.BlockSpec((tm, c), lambda i: (i, 0))
    gblk = lambda k: pl.BlockSpec((tm, d), lambda i: (i, gate_col0 + k))
    dblk = pl.BlockSpec((tm, d), lambda i: (i, 0))
    return pl.pallas_call(
        body,
        name=name,
        grid=(s // tm,),
        in_specs=[
            dblk, yblk, yblk, yblk, gblk(0), gblk(1), gblk(2),
            pl.BlockSpec((1, nbr * d), lambda i: (0, 0)),
            _resident((None, nbr, c, d), lambda i: (l, 0, 0, 0)),
            _resident((None, d, d), lambda i: (l, 0, 0)),
        ],
        out_specs=[pl.BlockSpec((tm, nbr * d), lambda i: (i, 0)), yblk, yblk, yblk, dblk, dblk, dblk, dblk,
                   pl.BlockSpec((8, nbr * d), lambda i: (0, 0))],
        out_shape=[jax.ShapeDtypeStruct((s, nbr * d), MXU_DTYPE)] + [jax.ShapeDtypeStruct((s, c), F32)] * 3
        + [jax.ShapeDtypeStruct((s, d), MXU_DTYPE)] * 4 + [jax.ShapeDtypeStruct((8, nbr * d), F32)],
        compiler_params=_params(),
    )(dx, *ys, p, p, p, b_gate, w_bo, w_o)


def _conv_bwd(p, dya, conv_w, conv_b, *, name, tm=512):
    s = p.shape[0]
    c = conv_w.shape[1]
    tm = _tile(s, tm)
    hb = tm // 8
    last = s // tm - 1

    def body(bg_ref, cg_ref, xa_ref, cgp_ref, xap_ref, dy_ref, dyn_ref, bgn_ref, w_ref, b_ref,
             dp_ref, dw_ref):
        i = pl.program_id(0)

        @pl.when(i == 0)
        def _():
            dw_ref[...] = jnp.zeros_like(dw_ref)

        cg = cg_ref[...]
        xa = xa_ref[...]
        u = cg * xa
        prev = jnp.where(i == 0, 0.0, cgp_ref[...] * xap_ref[...])
        um1, um2 = _conv_taps(u, prev)
        w = w_ref[...]
        y = b_ref[...] + w[0:1, :] * um2 + w[1:2, :] * um1 + w[2:3, :] * u
        dya = dy_ref[...]
        dp_ref[:, 0:c] = _mx(dya * y)
        dyv = dya * bg_ref[...]
        nxt = jnp.where(i == last, 0.0, dyn_ref[...] * bgn_ref[...])
        row = lax.broadcasted_iota(jnp.int32, dyv.shape, 0)
        dp1 = jnp.where(row == tm - 1, nxt[0:1, :], pltpu.roll(dyv, tm - 1, 0))
        dp2 = pltpu.roll(dyv, tm - 2, 0)
        dp2 = jnp.where(row == tm - 2, nxt[0:1, :], jnp.where(row == tm - 1, nxt[1:2, :], dp2))
        du = w[2:3, :] * dyv + w[1:2, :] * dp1 + w[0:1, :] * dp2
        dp_ref[:, c:2 * c] = _mx(du * xa)
        dp_ref[:, 2 * c:3 * c] = _mx(du * cg)
        dw_ref[0] += _row_fold(dyv * um2)
        dw_ref[1] += _row_fold(dyv * um1)
        dw_ref[2] += _row_fold(dyv * u)
        dw_ref[3] += _row_fold(dyv)

    blk = lambda col: pl.BlockSpec((tm, c), lambda i: (i, col))
    halo = lambda col: pl.BlockSpec((8, c), lambda i: (jnp.maximum(i * hb - 1, 0), col))
    nhalo = lambda col: pl.BlockSpec((8, c), lambda i: (jnp.minimum((i + 1) * hb, s // 8 - 1), col))
    return pl.pallas_call(
        body,
        name=name,
        grid=(s // tm,),
        in_specs=[blk(0), blk(1), blk(2), halo(1), halo(2), blk(0), nhalo(0), nhalo(0),
                  pl.BlockSpec((CONV_K, c), lambda i: (0, 0)), pl.BlockSpec((1, c), lambda i: (0, 0))],
        out_specs=[pl.BlockSpec((tm, 3 * c), lambda i: (i, 0)), pl.BlockSpec((4, 8, c), lambda i: (0, 0, 0))],
        out_shape=[jax.ShapeDtypeStruct((s, 3 * c), MXU_DTYPE), jax.ShapeDtypeStruct((4, 8, c), F32)],
        compiler_params=_params(),
    )(p, p, p, p, p, dya, dya, p, conv_w, conv_b)


def _sgu_bwd(p, dyb, ln_g, ln_b, wtril, wtril_t, bias_full, *, col0, name, tm=512):
    s = p.shape[0]
    c = ln_g.shape[1]
    groups = c // CHUNK
    tm = _tile(s, tm)

    def body(u_ref, v_ref, dy_ref, g_ref, b_ref, w_ref, wt_ref, bias_ref, du_ref, dv_ref, dln_ref, dw_ref, dbias_ref,
             dvn_ref):
        @pl.when(pl.program_id(0) == 0)
        def _():
            dln_ref[...] = jnp.zeros_like(dln_ref)
            dw_ref[...] = jnp.zeros_like(dw_ref)
            dbias_ref[...] = jnp.zeros_like(dbias_ref)

        sv = v_ref[...]
        xhat, r = _layernorm_stats(_gelu(sv))
        vb = _mx(xhat * g_ref[...] + b_ref[...])
        for n in range(tm // CHUNK):
            rows = slice(n * CHUNK, (n + 1) * CHUNK)
            for gi in range(groups):
                cols = slice(gi * CHUNK, (gi + 1) * CHUNK)
                su = u_ref[rows, cols]
                dy = dy_ref[rows, cols]
                vblk = vb[rows, cols]
                mixed = _dot(w_ref[gi], vblk) + bias_ref[:, cols]
                du_ref[rows, cols] = _mx(dy * mixed * _gelu_grad(su))
                dmixed = dy * _gelu(su)
                dmb = _mx(dmixed)
                dvn_ref[rows, cols] = _dot(wt_ref[gi], dmb)
                dw_ref[gi] += _dot_nt(dmb, vblk)
                dbias_ref[:, cols] += dmixed
        dvn = dvn_ref[...]
        dln_ref[0] += _row_fold(dvn * xhat)
        dln_ref[1] += _row_fold(dvn)
        dxh = dvn * g_ref[...]
        dgv = r * (dxh - jnp.mean(dxh, axis=-1, keepdims=True) - xhat * jnp.mean(dxh * xhat, axis=-1, keepdims=True))
        dv_ref[...] = _mx(dgv * _gelu_grad(sv))

    blk = lambda col: pl.BlockSpec((tm, c), lambda i: (i, col))
    vec = pl.BlockSpec((1, c), lambda i: (0, 0))
    wspec = pl.BlockSpec((groups, CHUNK, CHUNK), lambda i: (0, 0, 0))
    return pl.pallas_call(
        body,
        name=name,
        grid=(s // tm,),
        in_specs=[blk(col0), blk(col0 + 1), blk(0), vec, vec, wspec, wspec, pl.BlockSpec((CHUNK, c), lambda i: (0, 0))],
        out_specs=[blk(0), blk(0), pl.BlockSpec((2, 8, c), lambda i: (0, 0, 0)), wspec,
                   pl.BlockSpec((CHUNK, c), lambda i: (0, 0))],
        out_shape=[jax.ShapeDtypeStruct((s, c), MXU_DTYPE), jax.ShapeDtypeStruct((s, c), MXU_DTYPE),
                   jax.ShapeDtypeStruct((2, 8, c), F32), jax.ShapeDtypeStruct((groups, CHUNK, CHUNK), F32),
                   jax.ShapeDtypeStruct((CHUNK, c), F32)],
        scratch_shapes=[pltpu.VMEM((tm, c), F32)],
        compiler_params=_params(),
    )(p, p, dyb, ln_g, ln_b, wtril, wtril_t, bias_full)


def _att_bwd(kt_hm, vt_hm, qt_hm, dot_hm, runs, later, earlier, *, name, hg=4):
    h, nb, d, t = kt_hm.shape
    s = nb * t
    nbp = runs.shape[2]
    scale = 1.0 / math.sqrt(d)
    assert h % hg == 0

    def body(kt_ref, vt_ref, qt_ref, dot_ref, runs_ref, later_ref, earlier_ref, dqt_ref, dkt_ref, dvt_ref):
        i = pl.program_id(1)

        @pl.when(i == 0)
        def _():
            dkt_ref[...] = jnp.zeros_like(dkt_ref)
            dvt_ref[...] = jnp.zeros_like(dvt_ref)

        q_ts = [qt_ref[g] for g in range(hg)]
        do_ts = [dot_ref[g] for g in range(hg)]

        best = runs_ref[0]
        for g in range(1, hg):
            best = jnp.maximum(best, runs_ref[g])
        row = lax.broadcasted_iota(jnp.int32, (nbp, 1), 0)
        counts = jnp.logical_and(jnp.max(best, axis=1, keepdims=True) > EXP_UNDERFLOW, row < i)
        seen = jnp.sum(counts.astype(jnp.int32))
        zeros_row = tuple(jnp.zeros((1, t), F32) for _ in range(hg))

        def up(j, carry):
            gsums, dqts = carry
            mask = _att_mask(t, j * t, i * t)
            heads = range(hg)
            zs, es, lbs, l1ms = _att_blocks([kt_ref[g, j] for g in heads], q_ts, scale, mask)
            afters = _dot_split_each(later_ref[...], l1ms)
            das = [_dot_tn(vt_ref[g, j], do_ts[g]) for g in heads]
            weights = [jnp.where(mask, jnp.exp(lbs[g] + afters[g] + _row_get(runs_ref, g, j)), 0.0) for g in heads]
            grs = [das[g] * weights[g] for g in heads]
            gbefores = _dot_split_each(earlier_ref[...], grs)
            dzs = []
            for g in heads:
                inv = 1.0 / (1.0 + es[g])
                pos = zs[g] >= 0.0
                beta = jnp.where(pos, inv, es[g] * inv)
                omb = jnp.where(pos, es[g] * inv, inv)
                dzs.append(_mx(jnp.where(mask, grs[g] * omb - (gbefores[g] + gsums[g]) * beta, 0.0) * scale))
            new_dqts = [dqts[g] + _dot(kt_ref[g, j], dzs[g]) for g in heads]
            for g in heads:
                dkt_ref[g, j] += _dot_nt(q_ts[g], dzs[g])
            for g in heads:
                dvt_ref[g, j] += _dot_nt(do_ts[g], _mx(weights[g]))
            new_gsums = [gsums[g] + jnp.sum(grs[g], axis=0, keepdims=True) for g in heads]
            return tuple(new_gsums), tuple(new_dqts)

        _, dqts = lax.fori_loop(i - seen, i + 1, up, (zeros_row, tuple(jnp.zeros((d, t), F32) for _ in range(hg))))
        for g in range(hg):
            dqt_ref[g] = dqts[g]

    whole = pl.BlockSpec((hg, nb, d, t), lambda hh, i: (hh, 0, 0, 0))
    cols = pl.BlockSpec((hg, d, t), lambda hh, i: (hh, 0, i))
    tri = pl.BlockSpec((t, t), lambda hh, i: (0, 0))
    return pl.pallas_call(
        body,
        name=name,
        grid=(h // hg, nb),
        in_specs=[whole, whole, cols, cols, pl.BlockSpec((hg, None, nbp, t), lambda hh, i: (hh, i, 0, 0)), tri, tri],
        out_specs=[cols, whole, whole],
        out_shape=[jax.ShapeDtypeStruct((h, d, s), F32), jax.ShapeDtypeStruct((h, nb, d, t), F32),
                   jax.ShapeDtypeStruct((h, nb, d, t), F32)],
        compiler_params=_params(),
    )(kt_hm, vt_hm, qt_hm, dot_hm, runs, later, earlier)


def _qkv_bwd(p, dqn, dkn, dv, qg, kg, headavg, *, col0, name, tm=512):
    s = p.shape[0]
    c = qg.shape[1]
    tm = _tile(s, tm)

    def body(q_ref, k_ref, dqn_ref, dkn_ref, dv_ref, qg_ref, kg_ref, avg_ref, dq_ref, dk_ref, dvb_ref, dg_ref):
        @pl.when(pl.program_id(0) == 0)
        def _():
            dg_ref[...] = jnp.zeros_like(dg_ref)

        for n, (src, dn_ref, g_ref, dst) in enumerate(((q_ref, dqn_ref, qg_ref, dq_ref), (k_ref, dkn_ref, kg_ref, dk_ref))):
            xf = src[...]
            r = lax.rsqrt(_head_mean(xf * xf, avg_ref[...]) + EPS)
            y = xf * r
            dn = dn_ref[...]
            dy = dn * g_ref[...]
            dst[...] = _mx(r * (dy - y * _head_mean(dy * y, avg_ref[...])))
            dg_ref[n] += _row_fold(dn * y)
        dvb_ref[...] = _mx(dv_ref[...])

    blk = lambda col: pl.BlockSpec((tm, c), lambda i: (i, col))
    vec = pl.BlockSpec((1, c), lambda i: (0, 0))
    out_shape = jax.ShapeDtypeStruct((s, c), MXU_DTYPE)
    return pl.pallas_call(
        body,
        name=name,
        grid=(s // tm,),
        in_specs=[blk(col0), blk(col0 + 1), blk(0), blk(0), blk(0), vec, vec, pl.BlockSpec((c, c), lambda i: (0, 0))],
        out_specs=[blk(0), blk(0), blk(0), pl.BlockSpec((2, 8, c), lambda i: (0, 0, 0))],
        out_shape=[out_shape, out_shape, out_shape, jax.ShapeDtypeStruct((2, 8, c), F32)],
        compiler_params=_params(),
    )(p, p, dqn, dkn, dv, qg, kg, headavg)


def _place():
    x, y, c = lax.axis_index("x"), lax.axis_index("y"), lax.axis_index("c")
    chips = [(1 - x, y), (x, 1 - y), (1 - x, 1 - y)]
    return x, y, c, chips


def _shard_of(ref, axis, chip, width):
    idx = [slice(None)] * len(ref.shape)
    idx[axis] = pl.ds(chip * width, width)
    return ref.at[tuple(idx)]


def _place_shard(w, axis, chip_idx, dtype, *, name):
    layers = w.shape[0]
    rows, cols = _as_rows(w.shape[1:])
    if axis == len(w.shape) - 2:
        tr = _row_tile(rows, cols, 2)
        out_shape = (layers, rows, cols * N_CHIPS)
        out_spec = pl.BlockSpec((None, tr, cols), lambda l, i, j_ref: (l, i, j_ref[0]))
    else:
        assert axis == 0
        tr = _row_tile(rows, cols, 2)
        per = rows // tr
        out_shape = (layers, rows * N_CHIPS, cols)
        out_spec = pl.BlockSpec((None, tr, cols), lambda l, i, j_ref: (l, j_ref[0] * per + i, 0))
    full = list(w.shape)
    full[1 + axis] *= N_CHIPS

    def body(j_ref, w_ref, o_ref):
        o_ref[...] = w_ref[...].astype(dtype)

    out = pl.pallas_call(
        body,
        name=name,
        grid_spec=pltpu.PrefetchScalarGridSpec(
            num_scalar_prefetch=1,
            grid=(layers, rows // tr),
            in_specs=[pl.BlockSpec((None, tr, cols), lambda l, i, j_ref: (l, i, 0))],
            out_specs=out_spec,
        ),
        out_shape=jax.ShapeDtypeStruct(out_shape, dtype),
        compiler_params=_params(),
    )(chip_idx, w.reshape(layers, rows, cols))
    return out.reshape(full)


def _gather_weights(placed, axes, *, name):
    n = len(placed)
    widths = [pa.shape[1 + ax] // N_CHIPS for pa, ax in zip(placed, axes)]

    def body(*refs):
        outs = refs[n:2 * n]
        send_sem, recv_sem, fsend_sem, frecv_sem = refs[2 * n:]
        x, y, c, chips = _place()
        me = 2 * x + y
        sibling = (x, y, 1 - c)

        def slot(a, layer, chip):
            return _shard_of(outs[a].at[layer], axes[a], chip, widths[a])

        def send(a, k):
            return pltpu.make_async_remote_copy(
                src_ref=slot(a, c, me), dst_ref=slot(a, c, me), send_sem=send_sem.at[a, k], recv_sem=recv_sem.at[a, k],
                device_id=(*chips[k], c), device_id_type=MESH)

        def landed(a, k):
            cx, cy = chips[k]
            blk = slot(a, c, 2 * cx + cy)
            return pltpu.make_async_remote_copy(
                src_ref=blk, dst_ref=blk, send_sem=send_sem.at[a, k], recv_sem=recv_sem.at[a, k],
                device_id=(*chips[k], c), device_id_type=MESH)

        def forward(a, k, layer):
            cx, cy = chips[k]
            blk = slot(a, layer, 2 * cx + cy)
            return pltpu.make_async_remote_copy(
                src_ref=blk, dst_ref=blk, send_sem=fsend_sem.at[a, k], recv_sem=frecv_sem.at[a, k],
                device_id=sibling, device_id_type=MESH)

        for a in range(n):
            for k in range(3):
                send(a, k).start()
        for a in range(n):
            for k in range(3):
                landed(a, k).wait_recv()
                forward(a, k, c).start()
        for a in range(n):
            for k in range(3):
                forward(a, k, 1 - c).wait_recv()
        for a in range(n):
            for k in range(3):
                send(a, k).wait_send()
                forward(a, k, c).wait_send()

    return pl.pallas_call(
        body,
        name=name,
        in_specs=[ANY] * n,
        out_specs=[ANY] * n,
        out_shape=[jax.ShapeDtypeStruct(pa.shape, pa.dtype) for pa in placed],
        input_output_aliases={a: a for a in range(n)},
        scratch_shapes=[pltpu.SemaphoreType.DMA((n, 3))] * 4,
    )(*placed)


def _pair_exchange(grads, *, name):
    n = len(grads)

    def body(*refs):
        ins, outs = refs[:n], refs[n:2 * n]
        send_sem, recv_sem = refs[2 * n:]
        x, y, c, _ = _place()
        copies = [
            pltpu.make_async_remote_copy(
                src_ref=ins[a].at[1 - c], dst_ref=outs[a], send_sem=send_sem.at[a], recv_sem=recv_sem.at[a],
                device_id=(x, y, 1 - c), device_id_type=MESH)
            for a in range(n)
        ]
        for cp in copies:
            cp.start()
        for cp in copies:
            cp.wait()

    return pl.pallas_call(
        body,
        name=name,
        in_specs=[ANY] * n,
        out_specs=[ANY] * n,
        out_shape=[jax.ShapeDtypeStruct(g.shape[1:], g.dtype) for g in grads],
        scratch_shapes=[pltpu.SemaphoreType.DMA((n,)), pltpu.SemaphoreType.DMA((n,))],
    )(*grads)


def _chip_scatter(sums, axes, *, name):
    n = len(sums)
    widths = [sm.shape[ax] // N_CHIPS for sm, ax in zip(sums, axes)]
    out_shapes = []
    for sm, ax, w in zip(sums, axes, widths):
        shp = list(sm.shape)
        shp[ax] = w
        out_shapes.append(jax.ShapeDtypeStruct((3, *shp), sm.dtype))

    def body(*refs):
        ins, outs = refs[:n], refs[n:2 * n]
        send_sem, recv_sem = refs[2 * n:]
        x, y, c, chips = _place()
        copies = []
        for a in range(n):
            for k in range(3):
                cx, cy = chips[k]
                copies.append(pltpu.make_async_remote_copy(
                    src_ref=_shard_of(ins[a], axes[a], 2 * cx + cy, widths[a]), dst_ref=outs[a].at[k],
                    send_sem=send_sem.at[a, k], recv_sem=recv_sem.at[a, k], device_id=(cx, cy, c), device_id_type=MESH))
        for cp in copies:
            cp.start()
        for cp in copies:
            cp.wait()

    return pl.pallas_call(
        body,
        name=name,
        in_specs=[ANY] * n,
        out_specs=[ANY] * n,
        out_shape=out_shapes,
        scratch_shapes=[pltpu.SemaphoreType.DMA((n, 3)), pltpu.SemaphoreType.DMA((n, 3))],
    )(*sums)


def _pair_share(finals, *, name):
    n = len(finals)

    def body(*refs):
        outs = refs[n:2 * n]
        send_sem, recv_sem = refs[2 * n:]
        x, y, c, _ = _place()
        for a in range(n):
            pltpu.make_async_remote_copy(
                src_ref=outs[a].at[c], dst_ref=outs[a].at[c], send_sem=send_sem.at[a], recv_sem=recv_sem.at[a],
                device_id=(x, y, 1 - c), device_id_type=MESH).start()
        for a in range(n):
            pltpu.make_async_remote_copy(
                src_ref=outs[a].at[c], dst_ref=outs[a].at[1 - c], send_sem=send_sem.at[a], recv_sem=recv_sem.at[a],
                device_id=(x, y, 1 - c), device_id_type=MESH).wait()

    return pl.pallas_call(
        body,
        name=name,
        in_specs=[ANY] * n,
        out_specs=[ANY] * n,
        out_shape=[jax.ShapeDtypeStruct(f.shape, f.dtype) for f in finals],
        input_output_aliases={a: a for a in range(n)},
        scratch_shapes=[pltpu.SemaphoreType.DMA((n,))] * 2,
    )(*finals)


def _all_reduce_small(packed, *, name):
    r = packed.shape[0]

    def body(in_ref, out_ref, slots, send_sem, recv_sem):
        x, y, c, _ = _place()
        me = 4 * x + 2 * y + c
        slots[me] = in_ref[...]
        copies = []
        for k in range(1, N_DEV):
            peer = (x ^ (k >> 2), y ^ ((k >> 1) & 1), c ^ (k & 1))
            copies.append(pltpu.make_async_remote_copy(
                src_ref=in_ref, dst_ref=slots.at[me], send_sem=send_sem.at[k - 1], recv_sem=recv_sem.at[k - 1],
                device_id=peer, device_id_type=MESH))
        for cp in copies:
            cp.start()
        for k in range(1, N_DEV):
            pltpu.make_async_remote_copy(
                src_ref=in_ref, dst_ref=slots.at[me ^ k], send_sem=send_sem.at[k - 1], recv_sem=recv_sem.at[k - 1],
                device_id=(x, y, c), device_id_type=MESH).wait()
        total = slots[0]
        for dev in range(1, N_DEV):
            total = total + slots[dev]
        out_ref[...] = total

    return pl.pallas_call(
        body,
        name=name,
        in_specs=[pl.BlockSpec(memory_space=pltpu.VMEM)],
        out_specs=pl.BlockSpec(memory_space=pltpu.VMEM),
        out_shape=jax.ShapeDtypeStruct((r, 128), F32),
        scratch_shapes=[pltpu.VMEM((N_DEV, r, 128), F32), pltpu.SemaphoreType.DMA((N_DEV - 1,)),
                        pltpu.SemaphoreType.DMA((N_DEV - 1,))],
        compiler_params=_params(),
    )(packed)


def _as_rows(shape):
    cols = shape[-1]
    return math.prod(shape[:-1]), cols


ELEMENTWISE_VMEM = 24 * 1024 * 1024


def _row_tile(rows, cols, n_arrays, sublanes=8):
    cap = ELEMENTWISE_VMEM // (n_arrays * 2 * 4 * cols)
    best = None
    for t in range(sublanes, min(rows, cap) + 1, sublanes):
        if rows % t == 0:
            best = t
    assert best is not None, (rows, cols)
    return best


def _pair_sum(g, other, c_idx, *, name):
    rows, cols = _as_rows(other.shape)
    tr = _row_tile(rows, cols, 4, sublanes=16)
    g2 = g.reshape(2, rows, cols)

    def body(c_ref, g_ref, o_ref, out_ref, wire_ref):
        total = g_ref[...] + o_ref[...]
        out_ref[...] = total
        wire_ref[...] = total.astype(WIRE_DTYPE)

    blk = pl.BlockSpec((tr, cols), lambda i, c_ref: (i, 0))
    out, wire = pl.pallas_call(
        body,
        name=name,
        grid_spec=pltpu.PrefetchScalarGridSpec(
            num_scalar_prefetch=1,
            grid=(rows // tr,),
            in_specs=[pl.BlockSpec((None, tr, cols), lambda i, c_ref: (c_ref[0], i, 0)), blk],
            out_specs=[blk, blk],
        ),
        out_shape=[jax.ShapeDtypeStruct((rows, cols), F32), jax.ShapeDtypeStruct((rows, cols), WIRE_DTYPE)],
        compiler_params=_params(),
    )(c_idx, g2, other.reshape(rows, cols))
    return out.reshape(other.shape), wire.reshape(other.shape)


def _chip_sum(mine, got, axis, chip_idx, c_idx, *, name):
    shard_shape = got.shape[1:]
    rows, cols = _as_rows(shard_shape)
    tr = _row_tile(rows, cols, 5, sublanes=16)
    if axis == len(mine.shape) - 1:
        m2 = mine.reshape(rows, cols * N_CHIPS)
        mine_spec = pl.BlockSpec((tr, cols), lambda i, j_ref, c_ref: (i, j_ref[0]))
    else:
        assert axis == 0
        m2 = mine.reshape(N_CHIPS, rows, cols)
        mine_spec = pl.BlockSpec((None, tr, cols), lambda i, j_ref, c_ref: (j_ref[0], i, 0))

    def body(j_ref, c_ref, m_ref, got_ref, out_ref):
        out_ref[...] = ((m_ref[...] + got_ref[0].astype(F32)) + got_ref[1].astype(F32)) + got_ref[2].astype(F32)

    out = pl.pallas_call(
        body,
        name=name,
        grid_spec=pltpu.PrefetchScalarGridSpec(
            num_scalar_prefetch=2,
            grid=(rows // tr,),
            in_specs=[mine_spec, pl.BlockSpec((3, tr, cols), lambda i, j_ref, c_ref: (0, i, 0))],
            out_specs=pl.BlockSpec((None, tr, cols), lambda i, j_ref, c_ref: (c_ref[0], i, 0)),
        ),
        out_shape=jax.ShapeDtypeStruct((2, rows, cols), F32),
        compiler_params=_params(),
    )(chip_idx, c_idx, m2, got.reshape(3, rows, cols))
    return out.reshape((2, *shard_shape))


def _adamw_update(w_ref, g_ref, m_ref, v_ref, d_ref, nm_ref, nv_ref):
    c1 = 1.0 / (1.0 - ADAM_B1 ** ADAM_STEP)
    c2 = 1.0 / (1.0 - ADAM_B2 ** ADAM_STEP)
    gg = g_ref[...]
    nm = ADAM_B1 * m_ref[...] + (1.0 - ADAM_B1) * gg
    nv = ADAM_B2 * v_ref[...] + (1.0 - ADAM_B2) * (gg * gg)
    nm_ref[...] = nm
    nv_ref[...] = nv
    d_ref[...] = -ADAM_LR * ((nm * c1) / (jnp.sqrt(nv * c2) + ADAM_EPS) + ADAM_WD * w_ref[...])


def _adamw_small(ws, gs, ms, vs, *, name):
    n = len(ws)
    views = [(a.shape[0], math.prod(a.shape[1:])) for a in ws]

    def body(*refs):
        w_r, g_r, m_r, v_r = (refs[k * n:(k + 1) * n] for k in range(4))
        d_r, nm_r, nv_r = (refs[(4 + k) * n:(5 + k) * n] for k in range(3))
        for a in range(n):
            _adamw_update(w_r[a], g_r[a], m_r[a], v_r[a], d_r[a], nm_r[a], nv_r[a])

    vmem = pl.BlockSpec(memory_space=pltpu.VMEM)
    flat = lambda arrs: [a.reshape(view) for a, view in zip(arrs, views)]
    outs = pl.pallas_call(
        body,
        name=name,
        in_specs=[vmem] * (4 * n),
        out_specs=[vmem] * (3 * n),
        out_shape=[jax.ShapeDtypeStruct(view, F32) for view in views] * 3,
        compiler_params=_params(),
    )(*flat(ws), *flat(gs), *flat(ms), *flat(vs))
    return [tuple(outs[k * n + a].reshape(ws[a].shape) for k in range(3)) for a in range(n)]


def _adamw(w, g, m, v, *, name):
    shape = w.shape
    rows, cols = _as_rows(shape)
    tr = _row_tile(rows, cols, 7)
    body = functools.partial(_adamw_update)

    blk = pl.BlockSpec((tr, cols), lambda i: (i, 0))
    flat = lambda a: a.reshape(rows, cols)
    outs = pl.pallas_call(
        body,
        name=name,
        grid=(rows // tr,),
        in_specs=[blk] * 4,
        out_specs=[blk] * 3,
        out_shape=[jax.ShapeDtypeStruct((rows, cols), F32)] * 3,
        compiler_params=_params(),
    )(flat(w), flat(g), flat(m), flat(v))
    return tuple(o.reshape(shape) for o in outs)


def _cols_blocked(a, t):
    return a.reshape(a.shape[0] // t, t, HEADS, HEAD_DIM).transpose(2, 0, 3, 1)


def _cols(a):
    return a.reshape(a.shape[0], HEADS, HEAD_DIM).transpose(1, 2, 0)


def _from_cols(a):
    h, d, s = a.shape
    return a.transpose(2, 0, 1).reshape(s, h * d)


def _from_cols_blocked(a):
    h, nb, d, t = a.shape
    return a.transpose(1, 3, 0, 2).reshape(nb * t, h * d)


def _pack(parts):
    flat = []
    for a in parts:
        v = a.reshape(-1)
        flat.append(jnp.pad(v, (0, (-v.shape[0]) % 1024)))
    return jnp.concatenate(flat).reshape(-1, 128)


def _unpack(packed, shapes):
    flat = packed.reshape(-1)
    out, off = [], 0
    for shp in shapes:
        size = math.prod(shp)
        out.append(flat[off:off + size].reshape(shp))
        off += size + (-size) % 1024
    return out


BIG = ("w_in", "w_branch_out", "w_o", "w_gate_up", "w_down")
BIG_AXIS = {"w_in": 1, "w_branch_out": 2, "w_o": 0, "w_gate_up": 1, "w_down": 0}
SMALL = ("mix_norm_g", "b_gate", "conv_w", "conv_b", "sgu_ln_g", "sgu_ln_b", "sgu_w", "sgu_b", "q_norm_g", "k_norm_g",
         "ffn_norm_g")
ORDER = ("mix_norm_g", "w_in", "b_gate", "conv_w", "conv_b", "sgu_ln_g", "sgu_ln_b", "sgu_w", "sgu_b", "q_norm_g",
         "k_norm_g", "w_branch_out", "w_o", "ffn_norm_g", "w_gate_up", "w_down")


def _layer_forward(x, w, l):
    t = ATT_BLOCK
    c = w["conv_b"].shape[1]
    n_in = w["w_in"].shape[2]
    gate_col0 = (n_in - 3 * x.shape[1]) // x.shape[1]
    tag = f"l{l}"
    p, h = _norm_matmul(x, w["mix_norm_g"][l][None], w["w_in"], l, name=f"in_proj_{tag}", tn=n_in // 8)
    ya = _conv_fwd(p, w["conv_w"][l], w["conv_b"][l][None], name=f"conv_{tag}")
    yb = _sgu_fwd(p, w["sgu_ln_g"][l][None], w["sgu_ln_b"][l][None], w["wtril"][l], w["bias_full"][l], col0=3,
                  name=f"sgu_{tag}")
    qn, kn, vb = _qkv_prep(p, w["qg"][l], w["kg"][l], w["headavg"], col0=5, name=f"qkv_{tag}")
    q_c, k_cb, v_cb = _cols(qn), _cols_blocked(kn, t), _cols_blocked(vb, t)
    out_t, runs = _att_fwd(k_cb, q_c, v_cb, w["later"], name=f"att_{tag}")
    yc = _from_cols(out_t)
    x1, merged = _merge_fwd((ya, yb, yc), p, w["b_gate"][l][None], w["w_branch_out"], w["w_o"], l, x,
                            gate_col0=gate_col0, name=f"merge_{tag}")
    gu, h2 = _norm_matmul(x1, w["ffn_norm_g"][l][None], w["w_gate_up"], l, name=f"gate_up_{tag}",
                          tn=w["w_gate_up"].shape[2] // 4)
    x2 = _ffn_down(gu, w["w_down"], l, x1, name=f"down_{tag}")
    saved = dict(x=x, p=p, h=h, ya=ya, yb=yb, yc=yc, merged=merged, x1=x1, gu=gu, h2=h2,
                 att=(k_cb, v_cb, q_c, runs), gate_col0=gate_col0)
    return x2, saved


def _layer_backward(dx2, w, sv, l, big):
    layers = w["w_in"].shape[0]
    t = ATT_BLOCK
    tag = f"l{l}"
    d = dx2.shape[1]
    c = w["conv_b"].shape[1]
    n_in = w["w_in"].shape[2]
    n_ff = w["w_gate_up"].shape[2]
    g = {}
    dgu, act, dx2b = _ffn_bwd(dx2, sv["gu"], w["w_down"], l, name=f"down_bwd_{tag}")
    big["w_down"] = _matmul_tn(act, dx2b, l, layers, big.get("w_down"), name=f"dw_down_{tag}", t1=act.shape[1] // 2)
    dx1, dg2 = _matmul_nt_normbwd(dgu, w["w_gate_up"], l, sv["x1"], w["ffn_norm_g"][l][None], dx2,
                                  name=f"gate_up_bwd_{tag}", tk=n_ff // 4)
    g["ffn_norm_g"] = jnp.sum(dg2, axis=0)
    big["w_gate_up"] = _matmul_tn(sv["h2"], dgu, l, layers, big.get("w_gate_up"), name=f"dw_gate_up_{tag}", tn=n_ff // 4)
    ys = (sv["ya"], sv["yb"], sv["yc"])
    (dgates, dya, dyb, dyc, dd0, dd1, dd2, dx1b, dbg) = _merge_bwd(
        dx1, ys, sv["p"], w["b_gate"][l][None], w["w_branch_out"], w["w_o"], l, gate_col0=sv["gate_col0"],
        name=f"merge_bwd_{tag}")
    g["b_gate"] = jnp.sum(dbg, axis=0)
    big["w_o"] = _matmul_tn(sv["merged"], dx1b, l, layers, big.get("w_o"), name=f"dw_o_{tag}")
    for i, (y, dd) in enumerate(zip(ys, (dd0, dd1, dd2))):
        big["w_branch_out"] = _matmul_tn(y, dd, len(ys) * l + i, len(ys) * layers, big.get("w_branch_out"),
                                         name=f"dw_bo{i}_{tag}")
    dconv, dwc = _conv_bwd(sv["p"], dya, w["conv_w"][l], w["conv_b"][l][None], name=f"conv_bwd_{tag}")
    dwc = jnp.sum(dwc, axis=1)
    g["conv_w"] = dwc[0:CONV_K]
    g["conv_b"] = dwc[CONV_K]
    dsu, dsv, dln, dws, dbias = _sgu_bwd(sv["p"], dyb, w["sgu_ln_g"][l][None], w["sgu_ln_b"][l][None], w["wtril"][l],
                                         w["wtril_t"][l], w["bias_full"][l], col0=3, name=f"sgu_bwd_{tag}")
    dln = jnp.sum(dln, axis=1)
    g["sgu_ln_g"], g["sgu_ln_b"] = dln[0], dln[1]
    g["sgu_w"] = jnp.where(w["tril"], dws, 0.0)
    g["sgu_b"] = jnp.sum(dbias.reshape(CHUNK, c // CHUNK, CHUNK), axis=2).T
    k_cb, v_cb, q_c, runs = sv["att"]
    dqt, dkt, dvt = _att_bwd(k_cb, v_cb, q_c, _cols(_mx(dyc)), runs, w["later"], w["earlier"], name=f"att_bwd_{tag}")
    dqn, dkn, dvv = _from_cols(dqt), _from_cols_blocked(dkt), _from_cols_blocked(dvt)
    dq, dk_, dvb, dqkg = _qkv_bwd(sv["p"], dqn, dkn, dvv, w["qg"][l], w["kg"][l], w["headavg"], col0=5,
                                  name=f"qkv_bwd_{tag}")
    dqkg = jnp.sum(dqkg.reshape(2, 8 * HEADS, HEAD_DIM), axis=1)
    g["q_norm_g"], g["k_norm_g"] = dqkg[0], dqkg[1]
    dp = jnp.concatenate([dconv, dsu, dsv, dq, dk_, dvb, dgates], axis=1)
    dx0, dg1 = _matmul_nt_normbwd(dp, w["w_in"], l, sv["x"], w["mix_norm_g"][l][None], dx1, name=f"in_proj_bwd_{tag}",
                                  tk=n_in // 8)
    g["mix_norm_g"] = jnp.sum(dg1, axis=0)
    big["w_in"] = _matmul_tn(sv["h"], dp, l, layers, big.get("w_in"), name=f"dw_in_{tag}", tn=n_in // 4)
    return dx0, g


def kernel(x, mix_norm_g, w_in, b_gate, conv_w, conv_b, sgu_ln_g, sgu_ln_b, sgu_w, sgu_b, q_norm_g, k_norm_g, w_branch_out, w_o, ffn_norm_g, w_gate_up, w_down, loss_target, m_mix_norm_g, m_w_in, m_b_gate, m_conv_w, m_conv_b, m_sgu_ln_g, m_sgu_ln_b, m_sgu_w, m_sgu_b, m_q_norm_g, m_k_norm_g, m_w_branch_out, m_w_o, m_ffn_norm_g, m_w_gate_up, m_w_down, v_mix_norm_g, v_w_in, v_b_gate, v_conv_w, v_conv_b, v_sgu_ln_g, v_sgu_ln_b, v_sgu_w, v_sgu_b, v_q_norm_g, v_k_norm_g, v_w_branch_out, v_w_o, v_ffn_norm_g, v_w_gate_up, v_w_down):
    given = dict(locals())
    params = {n: given[n] for n in ORDER}
    moms = {n: (given["m_" + n], given["v_" + n]) for n in ORDER}
    layers = mix_norm_g.shape[0]
    assert layers == 2, "the exchanges split the work of a chip's two cores by layer"
    xs = x[0]
    target = loss_target[0]
    chip = 2 * lax.axis_index("x") + lax.axis_index("y")
    core = lax.axis_index("c")

    c_idx = core.reshape(1).astype(jnp.int32)
    j_idx = chip.reshape(1).astype(jnp.int32)
    width = conv_w.shape[2]
    placed = [_place_shard(params[n], BIG_AXIS[n], j_idx, MXU_DTYPE, name=f"place_{n}") for n in BIG]
    placed.append(lax.dynamic_update_slice_in_dim(jnp.zeros((layers, CONV_K, width * N_CHIPS), F32), conv_w, chip * width, 2))
    gathered = _gather_weights(placed, [BIG_AXIS[n] for n in BIG] + [1], name="gather_weights")
    w = dict(zip(BIG + ("conv_w",), gathered))
    for n in ("mix_norm_g", "b_gate", "conv_b", "sgu_ln_g", "sgu_ln_b", "ffn_norm_g"):
        w[n] = params[n]
    groups = sgu_w.shape[1]
    tril = jnp.tril(jnp.ones((CHUNK, CHUNK), dtype=bool))
    w["tril"] = tril
    w["wtril"] = _mx(jnp.where(tril, sgu_w, 0.0))
    w["wtril_t"] = w["wtril"].transpose(0, 1, 3, 2)
    w["bias_full"] = jnp.repeat(sgu_b.transpose(0, 2, 1), CHUNK, axis=2)
    w["qg"] = jnp.tile(q_norm_g, (1, HEADS))[:, None, :]
    w["kg"] = jnp.tile(k_norm_g, (1, HEADS))[:, None, :]
    lane = jnp.arange(HEADS * HEAD_DIM) // HEAD_DIM
    w["headavg"] = _mx(jnp.where(lane[:, None] == lane[None, :], 1.0 / HEAD_DIM, 0.0))
    pos = jnp.arange(ATT_BLOCK)
    w["later"] = _mx(jnp.where(pos[None, :] > pos[:, None], 1.0, 0.0))
    w["earlier"] = _mx(jnp.where(pos[None, :] < pos[:, None], 1.0, 0.0))

    saved = []
    act = xs
    for l in range(layers):
        act, sv = _layer_forward(act, w, l)
        saved.append(sv)
    loss_part, dact = _loss_head(act, target, name="loss_head")
    loss = lax.psum(jnp.sum(loss_part), ("x", "y", "c"))
    grads = [None] * layers
    big = {}
    for l in reversed(range(layers)):
        dact, grads[l] = _layer_backward(dact, w, saved[l], l, big)
    grad_x = dact[None]
    local = {n: jnp.stack([grads[l][n] for l in range(layers)]) for n in SMALL}
    for n in BIG:
        local[n] = big[n].reshape(w[n].shape)

    got = _pair_exchange([local[n] for n in BIG], name="grads_pair_exchange")
    pair = [_pair_sum(local[n], o, c_idx, name=f"pair_sum_{n}") for n, o in zip(BIG, got)]
    landed = _chip_scatter([wire for _, wire in pair], [BIG_AXIS[n] for n in BIG], name="grads_chip_scatter")
    final = [_chip_sum(pr, ld, BIG_AXIS[n], j_idx, c_idx, name=f"chip_sum_{n}") for n, (pr, _), ld in zip(BIG, pair, landed)]
    full = dict(zip(BIG, _pair_share(final, name="grads_pair_share")))

    small_shapes = [local[n].shape for n in SMALL]
    summed = _unpack(_all_reduce_small(_pack([local[n] for n in SMALL]), name="grads_all_reduce_small"), small_shapes)
    for n, gsum in zip(SMALL, summed):
        full[n] = gsum
    full["conv_w"] = lax.dynamic_slice_in_dim(full["conv_w"], chip * width, width, axis=2)

    out = {}
    for n in BIG:
        out[n] = _adamw(params[n], full[n], *moms[n], name=f"adamw_{n}")
    small = _adamw_small([params[n] for n in SMALL], [full[n] for n in SMALL], [moms[n][0] for n in SMALL],
                         [moms[n][1] for n in SMALL], name="adamw_small")
    for n, triple in zip(SMALL, small):
        out[n] = triple
    return (loss, grad_x, *[full[n] for n in ORDER], *[out[n][0] for n in ORDER], *[out[n][1] for n in ORDER],
            *[out[n][2] for n in ORDER])
```

```python
import functools
import math

import jax
import jax.numpy as jnp
from jax import lax
from jax.experimental import pallas as pl
from jax.experimental.pallas import tpu as pltpu

F32 = jnp.float32
MXU_DTYPE = jnp.bfloat16
WIRE_DTYPE = jnp.bfloat16
ACT_DTYPE = jnp.bfloat16
HALO = 16

EPS = 1e-6
CONV_K = 3
CHUNK = 128
HEADS = 8
HEAD_DIM = 64
ATT_BLOCK = 128
EXP_UNDERFLOW = -104.0

ADAM_LR = 0.001
ADAM_B1 = 0.9
ADAM_B2 = 0.999
ADAM_EPS = 1e-08
ADAM_WD = 0.01
ADAM_STEP = 10

VMEM_LIMIT = 56 * 1024 * 1024
MESH = pl.DeviceIdType.MESH
N_CHIPS = 4
N_DEV = 8
ANY = pl.BlockSpec(memory_space=pl.ANY)


def _params(**kw):
    return pltpu.CompilerParams(vmem_limit_bytes=VMEM_LIMIT, **kw)


def _mx(v):
    return v.astype(MXU_DTYPE)


def _dot(a, b):
    return lax.dot_general(a, b, (((1,), (0,)), ((), ())), preferred_element_type=F32)


def _dot_nt(a, b):
    return lax.dot_general(a, b, (((1,), (1,)), ((), ())), preferred_element_type=F32)


def _dot_tn(a, b):
    return lax.dot_general(a, b, (((0,), (0,)), ((), ())), preferred_element_type=F32)


def _dot_split(const, v):
    hi = _mx(v)
    lo = _mx(v - hi.astype(F32))
    return _dot(const, hi) + _dot(const, lo)


def _dot_split_r(v, const):
    hi = _mx(v)
    lo = _mx(v - hi.astype(F32))
    return _dot(hi, const) + _dot(lo, const)


def _sigmoid(x):
    return 1.0 / (1.0 + jnp.exp(-x))


_INV_SQRT2 = 1.0 / math.sqrt(2.0)
_INV_SQRT2PI = 1.0 / math.sqrt(2.0 * math.pi)


def _gelu(x):
    return 0.5 * x * (1.0 + lax.erf(x * _INV_SQRT2))


def _gelu_grad(x):
    return 0.5 * (1.0 + lax.erf(x * _INV_SQRT2)) + x * jnp.exp(-0.5 * x * x) * _INV_SQRT2PI


def _row_fold(v):
    m, n = v.shape
    return jnp.sum(v.reshape(m // 8, 8, n), axis=0)


def _tile(m, pref):
    t = min(m, pref)
    while m % t:
        t //= 2
    return t


def _resident(block_shape, index_map):
    return pl.BlockSpec(block_shape, index_map, pipeline_mode=pl.Buffered(1))


def _norm_matmul(x, g, w, l, *, name, tm=1024, tn=None):
    s, d = x.shape
    n = w.shape[2]
    tm = _tile(s, tm)
    tn = tn or n
    assert n % tn == 0

    def body(x_ref, g_ref, w_ref, p_ref, h_ref):
        @pl.when(pl.program_id(1) == 0)
        def _():
            xf = x_ref[...]
            r = lax.rsqrt(jnp.mean(xf * xf, axis=-1, keepdims=True) + EPS)
            h_ref[...] = _mx(xf * r * g_ref[...])

        p_ref[...] = _dot(h_ref[...], w_ref[...]).astype(ACT_DTYPE)

    return pl.pallas_call(
        body,
        name=name,
        grid=(s // tm, n // tn),
        in_specs=[
            pl.BlockSpec((tm, d), lambda i, j: (i, 0)),
            pl.BlockSpec((1, d), lambda i, j: (0, 0)),
            pl.BlockSpec((None, d, tn), lambda i, j: (l, 0, j)),
        ],
        out_specs=[
            pl.BlockSpec((tm, tn), lambda i, j: (i, j)),
            pl.BlockSpec((tm, d), lambda i, j: (i, 0)),
        ],
        out_shape=[jax.ShapeDtypeStruct((s, n), ACT_DTYPE), jax.ShapeDtypeStruct((s, d), MXU_DTYPE)],
        compiler_params=_params(),
    )(x, g, w)


def _conv_taps(u, prev):
    row = lax.broadcasted_iota(jnp.int32, u.shape, 0)
    last, before = prev[HALO - 1:HALO, :], prev[HALO - 2:HALO - 1, :]
    um1 = jnp.where(row == 0, last, pltpu.roll(u, 1, 0))
    um2 = pltpu.roll(u, 2, 0)
    um2 = jnp.where(row == 0, before, jnp.where(row == 1, last, um2))
    return um1, um2


def _f32(ref):
    return ref[...].astype(F32)


def _conv_fwd(p, conv_w, conv_b, *, name, tm=512):
    s = p.shape[0]
    c = conv_w.shape[1]
    tm = _tile(s, tm)
    hb = tm // HALO

    def body(bg_ref, cg_ref, xa_ref, cgp_ref, xap_ref, w_ref, b_ref, y_ref):
        first = pl.program_id(0) == 0
        u = _f32(cg_ref) * _f32(xa_ref)
        prev = jnp.where(first, 0.0, _f32(cgp_ref) * _f32(xap_ref))
        um1, um2 = _conv_taps(u, prev)
        w = w_ref[...]
        y = b_ref[...] + w[0:1, :] * um2 + w[1:2, :] * um1 + w[2:3, :] * u
        y_ref[...] = _mx(_f32(bg_ref) * y)

    halo = lambda col: pl.BlockSpec((HALO, c), lambda i: (jnp.maximum(i * hb - 1, 0), col))
    return pl.pallas_call(
        body,
        name=name,
        grid=(s // tm,),
        in_specs=[
            pl.BlockSpec((tm, c), lambda i: (i, 0)),
            pl.BlockSpec((tm, c), lambda i: (i, 1)),
            pl.BlockSpec((tm, c), lambda i: (i, 2)),
            halo(1),
            halo(2),
            pl.BlockSpec((CONV_K, c), lambda i: (0, 0)),
            pl.BlockSpec((1, c), lambda i: (0, 0)),
        ],
        out_specs=pl.BlockSpec((tm, c), lambda i: (i, 0)),
        out_shape=jax.ShapeDtypeStruct((s, c), MXU_DTYPE),
        compiler_params=_params(),
    )(p, p, p, p, p, conv_w, conv_b)


def _layernorm_stats(x):
    mu = jnp.mean(x, axis=-1, keepdims=True)
    xc = x - mu
    r = lax.rsqrt(jnp.mean(xc * xc, axis=-1, keepdims=True) + EPS)
    return xc * r, r


def _sgu_fwd(p, ln_g, ln_b, wtril, bias_full, *, col0, name, tm=512):
    s = p.shape[0]
    c = ln_g.shape[1]
    groups = c // CHUNK
    tm = _tile(s, tm)

    def body(u_ref, v_ref, g_ref, b_ref, w_ref, bias_ref, y_ref):
        vn, _ = _layernorm_stats(_gelu(_f32(v_ref)))
        vb = _mx(vn * g_ref[...] + b_ref[...])
        for n in range(tm // CHUNK):
            rows = slice(n * CHUNK, (n + 1) * CHUNK)
            for gi in range(groups):
                cols = slice(gi * CHUNK, (gi + 1) * CHUNK)
                mixed = _dot(w_ref[gi], vb[rows, cols]) + bias_ref[:, cols]
                y_ref[rows, cols] = _mx(_gelu(u_ref[rows, cols].astype(F32)) * mixed)

    return pl.pallas_call(
        body,
        name=name,
        grid=(s // tm,),
        in_specs=[
            pl.BlockSpec((tm, c), lambda i: (i, col0)),
            pl.BlockSpec((tm, c), lambda i: (i, col0 + 1)),
            pl.BlockSpec((1, c), lambda i: (0, 0)),
            pl.BlockSpec((1, c), lambda i: (0, 0)),
            pl.BlockSpec((groups, CHUNK, CHUNK), lambda i: (0, 0, 0)),
            pl.BlockSpec((CHUNK, c), lambda i: (0, 0)),
        ],
        out_specs=pl.BlockSpec((tm, c), lambda i: (i, 0)),
        out_shape=jax.ShapeDtypeStruct((s, c), MXU_DTYPE),
        compiler_params=_params(),
    )(p, p, ln_g, ln_b, wtril, bias_full)


def _head_mean(v, headavg):
    return _dot_split_r(v, headavg)


def _qkv_prep(p, qg, kg, headavg, *, col0, name, tm=512):
    s = p.shape[0]
    c = qg.shape[1]
    tm = _tile(s, tm)

    def body(q_ref, k_ref, v_ref, qg_ref, kg_ref, avg_ref, qn_ref, kn_ref, vb_ref):
        for src, g_ref, dst in ((q_ref, qg_ref, qn_ref), (k_ref, kg_ref, kn_ref)):
            xf = _f32(src)
            r = lax.rsqrt(_head_mean(xf * xf, avg_ref[...]) + EPS)
            dst[...] = _mx(xf * r * g_ref[...])
        vb_ref[...] = _mx(v_ref[...])

    blk = lambda col: pl.BlockSpec((tm, c), lambda i: (i, col))
    vec = pl.BlockSpec((1, c), lambda i: (0, 0))
    out = pl.BlockSpec((tm, c), lambda i: (i, 0))
    return pl.pallas_call(
        body,
        name=name,
        grid=(s // tm,),
        in_specs=[blk(col0), blk(col0 + 1), blk(col0 + 2), vec, vec, pl.BlockSpec((c, c), lambda i: (0, 0))],
        out_specs=[out, out, out],
        out_shape=[jax.ShapeDtypeStruct((s, c), MXU_DTYPE)] * 3,
        compiler_params=_params(),
    )(p, p, p, qg, kg, headavg)


def _att_mask(t, key0, qry0):
    kpos = key0 + lax.broadcasted_iota(jnp.int32, (t, t), 0)
    qpos = qry0 + lax.broadcasted_iota(jnp.int32, (t, t), 1)
    return kpos < qpos


def _att_blocks(k_blks, q_ts, scale, mask):
    zs = [(_dot_tn(k, q) if k.shape[0] == q.shape[0] else _dot(k, q)) * scale for k, q in zip(k_blks, q_ts)]
    es = [jnp.exp(-jnp.abs(z)) for z in zs]
    lbs = [jnp.minimum(z, 0.0) - jnp.log1p(e) for z, e in zip(zs, es)]
    l1ms = [jnp.where(mask, lb - z, 0.0) for lb, z in zip(lbs, zs)]
    return zs, es, lbs, l1ms


def _dot_split_each(const, vs):
    his = [_mx(v) for v in vs]
    los = [_mx(v - hi.astype(F32)) for v, hi in zip(vs, his)]
    tops = [_dot(const, hi) for hi in his]
    return [top + _dot(const, lo) for top, lo in zip(tops, los)]


def _max_over(runs):
    m = runs[0]
    for r in runs[1:]:
        m = jnp.maximum(m, r)
    return jnp.max(m)


NOT_SEEN = -1e30


def _row_put(ref, g, j, row):
    j8 = pl.multiple_of((j // 8) * 8, 8)
    blk = ref[g, pl.ds(j8, 8), :]
    sub = lax.broadcasted_iota(jnp.int32, blk.shape, 0)
    ref[g, pl.ds(j8, 8), :] = jnp.where(sub == j - j8, row, blk)


def _row_get(ref, g, j):
    j8 = pl.multiple_of((j // 8) * 8, 8)
    blk = ref[g, pl.ds(j8, 8), :]
    sub = lax.broadcasted_iota(jnp.int32, blk.shape, 0)
    return jnp.sum(jnp.where(sub == j - j8, blk, 0.0), axis=0, keepdims=True)


def _att_fwd(k_hm, qt_hm, vt_hm, later, *, name, hg=8):
    h, nb, d, t = k_hm.shape
    s = nb * t
    nbp = -(-nb // 8) * 8
    scale = 1.0 / math.sqrt(d)
    assert h % hg == 0

    def body(k_ref, qt_ref, vt_ref, later_ref, o_ref, runs_ref):
        i = pl.program_id(1)
        q_ts = [qt_ref[g] for g in range(hg)]
        runs_ref[...] = jnp.full(runs_ref.shape, NOT_SEEN, F32)

        def cond(carry):
            j, _, _, rmax = carry
            return jnp.logical_and(j >= 0, rmax > EXP_UNDERFLOW)

        def step(carry):
            j, runs, accs, _ = carry
            mask = _att_mask(t, j * t, i * t)
            heads = range(hg)
            for g in heads:
                _row_put(runs_ref, g, j, runs[g])
            _, _, lbs, l1ms = _att_blocks([k_ref[g, j] for g in heads], q_ts, scale, mask)
            afters = _dot_split_each(later_ref[...], l1ms)
            weights = [_mx(jnp.where(mask, jnp.exp(lbs[g] + afters[g] + runs[g]), 0.0)) for g in heads]
            new_accs = [accs[g] + _dot(vt_ref[g, j], weights[g]) for g in heads]
            new_runs = [runs[g] + jnp.sum(l1ms[g], axis=0, keepdims=True) for g in heads]
            return j - 1, tuple(new_runs), tuple(new_accs), _max_over(new_runs)

        init = (i, tuple(jnp.zeros((1, t), F32) for _ in range(hg)), tuple(jnp.zeros((d, t), F32) for _ in range(hg)),
                jnp.float32(0.0))
        _, _, accs, _ = lax.while_loop(cond, step, init)
        for g in range(hg):
            o_ref[g] = _mx(accs[g])

    return pl.pallas_call(
        body,
        name=name,
        grid=(h // hg, nb),
        in_specs=[
            pl.BlockSpec((hg, nb, d, t), lambda hh, i: (hh, 0, 0, 0)),
            pl.BlockSpec((hg, d, t), lambda hh, i: (hh, 0, i)),
            pl.BlockSpec((hg, nb, d, t), lambda hh, i: (hh, 0, 0, 0)),
            pl.BlockSpec((t, t), lambda hh, i: (0, 0)),
        ],
        out_specs=[pl.BlockSpec((hg, d, t), lambda hh, i: (hh, 0, i)),
                   pl.BlockSpec((hg, None, nbp, t), lambda hh, i: (hh, i, 0, 0))],
        out_shape=[jax.ShapeDtypeStruct((h, d, s), MXU_DTYPE), jax.ShapeDtypeStruct((h, nb, nbp, t), F32)],
        compiler_params=_params(),
    )(k_hm, qt_hm, vt_hm, later)


def _merge_fwd(ys, p, b_gate, w_bo, w_o, l, x, *, gate_col0, name, tm=512):
    s, d = x.shape
    _, nbr, c, _ = w_bo.shape
    tm = _tile(s, tm)

    def body(ya_ref, yb_ref, yc_ref, g0_ref, g1_ref, g2_ref, bg_ref, wbo_ref, wo_ref, x_ref, x1_ref, m_ref):
        merged = jnp.zeros((tm, d), F32)
        for i, (y_ref, g_ref) in enumerate(((ya_ref, g0_ref), (yb_ref, g1_ref), (yc_ref, g2_ref))):
            gate = _sigmoid(_f32(g_ref) + bg_ref[:, i * d:(i + 1) * d])
            merged = merged + gate * _dot(y_ref[...], wbo_ref[i])
        mb = _mx(merged)
        m_ref[...] = mb
        x1_ref[...] = x_ref[...] + _dot(mb, wo_ref[...])

    yblk = pl.BlockSpec((tm, c), lambda i: (i, 0))
    gblk = lambda k: pl.BlockSpec((tm, d), lambda i: (i, gate_col0 + k))
    xblk = pl.BlockSpec((tm, d), lambda i: (i, 0))
    return pl.pallas_call(
        body,
        name=name,
        grid=(s // tm,),
        in_specs=[
            yblk, yblk, yblk, gblk(0), gblk(1), gblk(2),
            pl.BlockSpec((1, nbr * d), lambda i: (0, 0)),
            _resident((None, nbr, c, d), lambda i: (l, 0, 0, 0)),
            _resident((None, d, d), lambda i: (l, 0, 0)),
            xblk,
        ],
        out_specs=[xblk, xblk],
        out_shape=[jax.ShapeDtypeStruct((s, d), F32), jax.ShapeDtypeStruct((s, d), MXU_DTYPE)],
        compiler_params=_params(),
    )(*ys, p, p, p, b_gate, w_bo, w_o, x)


def _ffn_down(gu, w_down, l, x, *, name, tm=512):
    s, d = x.shape
    f = w_down.shape[1]
    tm = _tile(s, tm)

    def body(g_ref, u_ref, w_ref, x_ref, o_ref):
        g = _f32(g_ref)
        act = _mx(g * _sigmoid(g) * _f32(u_ref))
        o_ref[...] = x_ref[...] + _dot(act, w_ref[...])

    return pl.pallas_call(
        body,
        name=name,
        grid=(s // tm,),
        in_specs=[
            pl.BlockSpec((tm, f), lambda i: (i, 0)),
            pl.BlockSpec((tm, f), lambda i: (i, 1)),
            _resident((None, f, d), lambda i: (l, 0, 0)),
            pl.BlockSpec((tm, d), lambda i: (i, 0)),
        ],
        out_specs=pl.BlockSpec((tm, d), lambda i: (i, 0)),
        out_shape=jax.ShapeDtypeStruct((s, d), F32),
        compiler_params=_params(),
    )(gu, gu, w_down, x)


def _loss_head(y, target, *, name, tm=512):
    s, d = y.shape
    tm = _tile(s, tm)

    def body(y_ref, t_ref, l_ref, dy_ref):
        @pl.when(pl.program_id(0) == 0)
        def _():
            l_ref[...] = jnp.zeros_like(l_ref)

        err = y_ref[...] - t_ref[...]
        dy_ref[...] = err * (1.0 / d)
        sq = _row_fold(err * err)
        part = sq[:, 0:128]
        for k in range(1, d // 128):
            part = part + sq[:, k * 128:(k + 1) * 128]
        l_ref[...] += part * (0.5 / d)

    blk = pl.BlockSpec((tm, d), lambda i: (i, 0))
    return pl.pallas_call(
        body,
        name=name,
        grid=(s // tm,),
        in_specs=[blk, blk],
        out_specs=[pl.BlockSpec((8, 128), lambda i: (0, 0)), blk],
        out_shape=[jax.ShapeDtypeStruct((8, 128), F32), jax.ShapeDtypeStruct((s, d), F32)],
        compiler_params=_params(),
    )(y, target)


def _matmul_tn(a, b, slot, n_slots, into, *, name, t1=1024, tn=None, tm=1024):
    m, k1 = a.shape
    n = b.shape[1]
    t1 = _tile(k1, t1)
    tn = tn or n
    tm = _tile(m, tm)
    steps = m // tm
    assert n % tn == 0

    def body(a_ref, b_ref, *refs):
        o_ref = refs[-1]

        @pl.when(pl.program_id(2) == 0)
        def _():
            o_ref[...] = jnp.zeros_like(o_ref)

        o_ref[...] += _dot_tn(a_ref[...], b_ref[...])

    return pl.pallas_call(
        body,
        name=name,
        grid=(k1 // t1, n // tn, steps),
        in_specs=[
            pl.BlockSpec((tm, t1), lambda i, j, k: (k, i)),
            pl.BlockSpec((tm, tn), lambda i, j, k: (k, j)),
        ] + ([] if into is None else [ANY]),
        out_specs=pl.BlockSpec((None, t1, tn), lambda i, j, k: (slot, i, j)),
        out_shape=jax.ShapeDtypeStruct((n_slots, k1, n), F32),
        input_output_aliases={} if into is None else {2: 0},
        compiler_params=_params(),
    )(a, b, *([] if into is None else [into]))


def _matmul_nt_normbwd(dp, w, l, x, g, dres, *, name, tm=1024, tk=None):
    s, d = x.shape
    k = w.shape[2]
    tm = _tile(s, tm)
    tk = tk or k
    steps = k // tk
    assert k % tk == 0

    def body(dp_ref, w_ref, x_ref, g_ref, dres_ref, dx_ref, dg_ref, acc_ref):
        kk = pl.program_id(1)

        @pl.when(jnp.logical_and(pl.program_id(0) == 0, kk == 0))
        def _():
            dg_ref[...] = jnp.zeros_like(dg_ref)

        @pl.when(kk == 0)
        def _():
            acc_ref[...] = jnp.zeros_like(acc_ref)

        acc_ref[...] += _dot_nt(dp_ref[...], w_ref[...])

        @pl.when(kk == steps - 1)
        def _():
            dh = acc_ref[...]
            xf = x_ref[...]
            r = lax.rsqrt(jnp.mean(xf * xf, axis=-1, keepdims=True) + EPS)
            y = xf * r
            dy = dh * g_ref[...]
            dx_ref[...] = dres_ref[...] + r * (dy - y * jnp.mean(dy * y, axis=-1, keepdims=True))
            dg_ref[...] += _row_fold(dh * y)

    xblk = pl.BlockSpec((tm, d), lambda i, j: (i, 0))
    return pl.pallas_call(
        body,
        name=name,
        grid=(s // tm, steps),
        in_specs=[
            pl.BlockSpec((tm, tk), lambda i, j: (i, j)),
            pl.BlockSpec((None, d, tk), lambda i, j: (l, 0, j)),
            xblk,
            pl.BlockSpec((1, d), lambda i, j: (0, 0)),
            xblk,
        ],
        out_specs=[xblk, pl.BlockSpec((8, d), lambda i, j: (0, 0))],
        out_shape=[jax.ShapeDtypeStruct((s, d), F32), jax.ShapeDtypeStruct((8, d), F32)],
        scratch_shapes=[pltpu.VMEM((tm, d), F32)],
        compiler_params=_params(),
    )(dp, w, x, g, dres)


def _ffn_bwd(dx, gu, w_down, l, *, name, tm=512):
    s, d = dx.shape
    f = w_down.shape[1]
    tm = _tile(s, tm)

    def body(dx_ref, g_ref, u_ref, w_ref, dgu_ref, act_ref, dxb_ref):
        dxb = _mx(dx_ref[...])
        dxb_ref[...] = dxb
        dact = _dot_nt(dxb, w_ref[...])
        g = _f32(g_ref)
        u = _f32(u_ref)
        sg = _sigmoid(g)
        silu = g * sg
        act_ref[...] = _mx(silu * u)
        dgu_ref[:, 0:f] = _mx(dact * u * (sg * (1.0 + g * (1.0 - sg))))
        dgu_ref[:, f:2 * f] = _mx(dact * silu)

    fblk = lambda col: pl.BlockSpec((tm, f), lambda i: (i, col))
    dblk = pl.BlockSpec((tm, d), lambda i: (i, 0))
    return pl.pallas_call(
        body,
        name=name,
        grid=(s // tm,),
        in_specs=[dblk, fblk(0), fblk(1), _resident((None, f, d), lambda i: (l, 0, 0))],
        out_specs=[pl.BlockSpec((tm, 2 * f), lambda i: (i, 0)), fblk(0), dblk],
        out_shape=[
            jax.ShapeDtypeStruct((s, 2 * f), MXU_DTYPE),
            jax.ShapeDtypeStruct((s, f), MXU_DTYPE),
            jax.ShapeDtypeStruct((s, d), MXU_DTYPE),
        ],
        compiler_params=_params(),
    )(dx, gu, gu, w_down)


def _merge_bwd(dx, ys, p, b_gate, w_bo, w_o, l, *, gate_col0, name, tm=512):
    s, d = dx.shape
    _, nbr, c, _ = w_bo.shape
    tm = _tile(s, tm)

    def body(dx_ref, ya_ref, yb_ref, yc_ref, g0_ref, g1_ref, g2_ref, bg_ref, wbo_ref, wo_ref,
             dgate_ref, dya_ref, dyb_ref, dyc_ref, dd0_ref, dd1_ref, dd2_ref, dxb_ref, dbg_ref):
        @pl.when(pl.program_id(0) == 0)
        def _():
            dbg_ref[...] = jnp.zeros_like(dbg_ref)

        dxb = _mx(dx_ref[...])
        dxb_ref[...] = dxb
        dmerged = _dot_nt(dxb, wo_ref[...])
        branches = ((ya_ref, g0_ref, dya_ref, dd0_ref), (yb_ref, g1_ref, dyb_ref, dd1_ref), (yc_ref, g2_ref, dyc_ref, dd2_ref))
        for i, (y_ref, g_ref, dy_ref, dd_ref) in enumerate(branches):
            cols = slice(i * d, (i + 1) * d)
            gate = _sigmoid(_f32(g_ref) + bg_ref[:, cols])
            yd = _dot(y_ref[...], wbo_ref[i])
            dyd = _mx(dmerged * gate)
            dd_ref[...] = dyd
            dy_ref[...] = _dot_nt(dyd, wbo_ref[i])
            dpre = dmerged * yd * gate * (1.0 - gate)
            dgate_ref[:, cols] = _mx(dpre)
            dbg_ref[:, cols] += _row_fold(dpre)

    yblk = pl.BlockSpec((tm, c), lambda i: (i, 0))
    gblk = lambda k: pl.BlockSpec((tm, d), lambda i: (i, gate_col0 + k))
    dblk = pl.BlockSpec((tm, d), lambda i: (i, 0))
    return pl.pallas_call(
        body,
        name=name,
        grid=(s // tm,),
        in_specs=[
            dblk, yblk, yblk, yblk, gblk(0), gblk(1), gblk(2),
            pl.BlockSpec((1, nbr * d), lambda i: (0, 0)),
            _resident((None, nbr, c, d), lambda i: (l, 0, 0, 0)),
            _resident((None, d, d), lambda i: (l, 0, 0)),
        ],
        out_specs=[pl.BlockSpec((tm, nbr * d), lambda i: (i, 0)), yblk, yblk, yblk, dblk, dblk, dblk, dblk,
                   pl.BlockSpec((8, nbr * d), lambda i: (0, 0))],
        out_shape=[jax.ShapeDtypeStruct((s, nbr * d), MXU_DTYPE)] + [jax.ShapeDtypeStruct((s, c), F32)] * 3
        + [jax.ShapeDtypeStruct((s, d), MXU_DTYPE)] * 4 + [jax.ShapeDtypeStruct((8, nbr * d), F32)],
        compiler_params=_params(),
    )(dx, *ys, p, p, p, b_gate, w_bo, w_o)


def _conv_bwd(p, dya, conv_w, conv_b, *, name, tm=512):
    s = p.shape[0]
    c = conv_w.shape[1]
    tm = _tile(s, tm)
    hb = tm // HALO
    last = s // tm - 1

    def body(bg_ref, cg_ref, xa_ref, cgp_ref, xap_ref, dy_ref, dyn_ref, bgn_ref, w_ref, b_ref,
             dp_ref, dw_ref):
        i = pl.program_id(0)

        @pl.when(i == 0)
        def _():
            dw_ref[...] = jnp.zeros_like(dw_ref)

        cg = _f32(cg_ref)
        xa = _f32(xa_ref)
        u = cg * xa
        prev = jnp.where(i == 0, 0.0, _f32(cgp_ref) * _f32(xap_ref))
        um1, um2 = _conv_taps(u, prev)
        w = w_ref[...]
        y = b_ref[...] + w[0:1, :] * um2 + w[1:2, :] * um1 + w[2:3, :] * u
        dya = dy_ref[...]
        dp_ref[:, 0:c] = _mx(dya * y)
        dyv = dya * _f32(bg_ref)
        nxt = jnp.where(i == last, 0.0, dyn_ref[...] * _f32(bgn_ref))
        row = lax.broadcasted_iota(jnp.int32, dyv.shape, 0)
        dp1 = jnp.where(row == tm - 1, nxt[0:1, :], pltpu.roll(dyv, tm - 1, 0))
        dp2 = pltpu.roll(dyv, tm - 2, 0)
        dp2 = jnp.where(row == tm - 2, nxt[0:1, :], jnp.where(row == tm - 1, nxt[1:2, :], dp2))
        du = w[2:3, :] * dyv + w[1:2, :] * dp1 + w[0:1, :] * dp2
        dp_ref[:, c:2 * c] = _mx(du * xa)
        dp_ref[:, 2 * c:3 * c] = _mx(du * cg)
        dw_ref[0] += _row_fold(dyv * um2)
        dw_ref[1] += _row_fold(dyv * um1)
        dw_ref[2] += _row_fold(dyv * u)
        dw_ref[3] += _row_fold(dyv)

    blk = lambda col: pl.BlockSpec((tm, c), lambda i: (i, col))
    halo = lambda col: pl.BlockSpec((HALO, c), lambda i: (jnp.maximum(i * hb - 1, 0), col))
    nhalo = lambda col: pl.BlockSpec((HALO, c), lambda i: (jnp.minimum((i + 1) * hb, s // HALO - 1), col))
    return pl.pallas_call(
        body,
        name=name,
        grid=(s // tm,),
        in_specs=[blk(0), blk(1), blk(2), halo(1), halo(2), blk(0), nhalo(0), nhalo(0),
                  pl.BlockSpec((CONV_K, c), lambda i: (0, 0)), pl.BlockSpec((1, c), lambda i: (0, 0))],
        out_specs=[pl.BlockSpec((tm, 3 * c), lambda i: (i, 0)), pl.BlockSpec((4, 8, c), lambda i: (0, 0, 0))],
        out_shape=[jax.ShapeDtypeStruct((s, 3 * c), MXU_DTYPE), jax.ShapeDtypeStruct((4, 8, c), F32)],
        compiler_params=_params(),
    )(p, p, p, p, p, dya, dya, p, conv_w, conv_b)


def _sgu_bwd(p, dyb, ln_g, ln_b, wtril, wtril_t, bias_full, *, col0, name, tm=512):
    s = p.shape[0]
    c = ln_g.shape[1]
    groups = c // CHUNK
    tm = _tile(s, tm)

    def body(u_ref, v_ref, dy_ref, g_ref, b_ref, w_ref, wt_ref, bias_ref, du_ref, dv_ref, dln_ref, dw_ref, dbias_ref,
             dvn_ref):
        @pl.when(pl.program_id(0) == 0)
        def _():
            dln_ref[...] = jnp.zeros_like(dln_ref)
            dw_ref[...] = jnp.zeros_like(dw_ref)
            dbias_ref[...] = jnp.zeros_like(dbias_ref)

        sv = _f32(v_ref)
        xhat, r = _layernorm_stats(_gelu(sv))
        vb = _mx(xhat * g_ref[...] + b_ref[...])
        for n in range(tm // CHUNK):
            rows = slice(n * CHUNK, (n + 1) * CHUNK)
            for gi in range(groups):
                cols = slice(gi * CHUNK, (gi + 1) * CHUNK)
                su = u_ref[rows, cols].astype(F32)
                dy = dy_ref[rows, cols]
                vblk = vb[rows, cols]
                mixed = _dot(w_ref[gi], vblk) + bias_ref[:, cols]
                du_ref[rows, cols] = _mx(dy * mixed * _gelu_grad(su))
                dmixed = dy * _gelu(su)
                dmb = _mx(dmixed)
                dvn_ref[rows, cols] = _dot(wt_ref[gi], dmb)
                dw_ref[gi] += _dot_nt(dmb, vblk)
                dbias_ref[:, cols] += dmixed
        dvn = dvn_ref[...]
        dln_ref[0] += _row_fold(dvn * xhat)
        dln_ref[1] += _row_fold(dvn)
        dxh = dvn * g_ref[...]
        dgv = r * (dxh - jnp.mean(dxh, axis=-1, keepdims=True) - xhat * jnp.mean(dxh * xhat, axis=-1, keepdims=True))
        dv_ref[...] = _mx(dgv * _gelu_grad(sv))

    blk = lambda col: pl.BlockSpec((tm, c), lambda i: (i, col))
    vec = pl.BlockSpec((1, c), lambda i: (0, 0))
    wspec = pl.BlockSpec((groups, CHUNK, CHUNK), lambda i: (0, 0, 0))
    return pl.pallas_call(
        body,
        name=name,
        grid=(s // tm,),
        in_specs=[blk(col0), blk(col0 + 1), blk(0), vec, vec, wspec, wspec, pl.BlockSpec((CHUNK, c), lambda i: (0, 0))],
        out_specs=[blk(0), blk(0), pl.BlockSpec((2, 8, c), lambda i: (0, 0, 0)), wspec,
                   pl.BlockSpec((CHUNK, c), lambda i: (0, 0))],
        out_shape=[jax.ShapeDtypeStruct((s, c), MXU_DTYPE), jax.ShapeDtypeStruct((s, c), MXU_DTYPE),
                   jax.ShapeDtypeStruct((2, 8, c), F32), jax.ShapeDtypeStruct((groups, CHUNK, CHUNK), F32),
                   jax.ShapeDtypeStruct((CHUNK, c), F32)],
        scratch_shapes=[pltpu.VMEM((tm, c), F32)],
        compiler_params=_params(),
    )(p, p, dyb, ln_g, ln_b, wtril, wtril_t, bias_full)


def _att_bwd(kt_hm, vt_hm, qt_hm, dot_hm, runs, later, earlier, *, name, hg=4):
    h, nb, d, t = kt_hm.shape
    s = nb * t
    nbp = runs.shape[2]
    scale = 1.0 / math.sqrt(d)
    assert h % hg == 0

    def body(kt_ref, vt_ref, qt_ref, dot_ref, runs_ref, later_ref, earlier_ref, dqt_ref, dkt_ref, dvt_ref):
        i = pl.program_id(1)

        @pl.when(i == 0)
        def _():
            dkt_ref[...] = jnp.zeros_like(dkt_ref)
            dvt_ref[...] = jnp.zeros_like(dvt_ref)

        q_ts = [qt_ref[g] for g in range(hg)]
        do_ts = [dot_ref[g] for g in range(hg)]

        best = runs_ref[0]
        for g in range(1, hg):
            best = jnp.maximum(best, runs_ref[g])
        row = lax.broadcasted_iota(jnp.int32, (nbp, 1), 0)
        counts = jnp.logical_and(jnp.max(best, axis=1, keepdims=True) > EXP_UNDERFLOW, row < i)
        seen = jnp.sum(counts.astype(jnp.int32))
        zeros_row = tuple(jnp.zeros((1, t), F32) for _ in range(hg))

        def up(j, carry):
            gsums, dqts = carry
            mask = _att_mask(t, j * t, i * t)
            heads = range(hg)
            zs, es, lbs, l1ms = _att_blocks([kt_ref[g, j] for g in heads], q_ts, scale, mask)
            afters = _dot_split_each(later_ref[...], l1ms)
            das = [_dot_tn(vt_ref[g, j], do_ts[g]) for g in heads]
            weights = [jnp.where(mask, jnp.exp(lbs[g] + afters[g] + _row_get(runs_ref, g, j)), 0.0) for g in heads]
            grs = [das[g] * weights[g] for g in heads]
            gbefores = _dot_split_each(earlier_ref[...], grs)
            dzs = []
            for g in heads:
                inv = 1.0 / (1.0 + es[g])
                pos = zs[g] >= 0.0
                beta = jnp.where(pos, inv, es[g] * inv)
                omb = jnp.where(pos, es[g] * inv, inv)
                dzs.append(_mx(jnp.where(mask, grs[g] * omb - (gbefores[g] + gsums[g]) * beta, 0.0) * scale))
            new_dqts = [dqts[g] + _dot(kt_ref[g, j], dzs[g]) for g in heads]
            for g in heads:
                dkt_ref[g, j] += _dot_nt(q_ts[g], dzs[g])
            for g in heads:
                dvt_ref[g, j] += _dot_nt(do_ts[g], _mx(weights[g]))
            new_gsums = [gsums[g] + jnp.sum(grs[g], axis=0, keepdims=True) for g in heads]
            return tuple(new_gsums), tuple(new_dqts)

        _, dqts = lax.fori_loop(i - seen, i + 1, up, (zeros_row, tuple(jnp.zeros((d, t), F32) for _ in range(hg))))
        for g in range(hg):
            dqt_ref[g] = dqts[g]

    whole = pl.BlockSpec((hg, nb, d, t), lambda hh, i: (hh, 0, 0, 0))
    cols = pl.BlockSpec((hg, d, t), lambda hh, i: (hh, 0, i))
    tri = pl.BlockSpec((t, t), lambda hh, i: (0, 0))
    return pl.pallas_call(
        body,
        name=name,
        grid=(h // hg, nb),
        in_specs=[whole, whole, cols, cols, pl.BlockSpec((hg, None, nbp, t), lambda hh, i: (hh, i, 0, 0)), tri, tri],
        out_specs=[cols, whole, whole],
        out_shape=[jax.ShapeDtypeStruct((h, d, s), F32), jax.ShapeDtypeStruct((h, nb, d, t), F32),
                   jax.ShapeDtypeStruct((h, nb, d, t), F32)],
        compiler_params=_params(),
    )(kt_hm, vt_hm, qt_hm, dot_hm, runs, later, earlier)


def _qkv_bwd(p, dqn, dkn, dv, qg, kg, headavg, *, col0, name, tm=512):
    s = p.shape[0]
    c = qg.shape[1]
    tm = _tile(s, tm)

    def body(q_ref, k_ref, dqn_ref, dkn_ref, dv_ref, qg_ref, kg_ref, avg_ref, dq_ref, dk_ref, dvb_ref, dg_ref):
        @pl.when(pl.program_id(0) == 0)
        def _():
            dg_ref[...] = jnp.zeros_like(dg_ref)

        for n, (src, dn_ref, g_ref, dst) in enumerate(((q_ref, dqn_ref, qg_ref, dq_ref), (k_ref, dkn_ref, kg_ref, dk_ref))):
            xf = _f32(src)
            r = lax.rsqrt(_head_mean(xf * xf, avg_ref[...]) + EPS)
            y = xf * r
            dn = dn_ref[...]
            dy = dn * g_ref[...]
            dst[...] = _mx(r * (dy - y * _head_mean(dy * y, avg_ref[...])))
            dg_ref[n] += _row_fold(dn * y)
        dvb_ref[...] = _mx(dv_ref[...])

    blk = lambda col: pl.BlockSpec((tm, c), lambda i: (i, col))
    vec = pl.BlockSpec((1, c), lambda i: (0, 0))
    out_shape = jax.ShapeDtypeStruct((s, c), MXU_DTYPE)
    return pl.pallas_call(
        body,
        name=name,
        grid=(s // tm,),
        in_specs=[blk(col0), blk(col0 + 1), blk(0), blk(0), blk(0), vec, vec, pl.BlockSpec((c, c), lambda i: (0, 0))],
        out_specs=[blk(0), blk(0), blk(0), pl.BlockSpec((2, 8, c), lambda i: (0, 0, 0))],
        out_shape=[out_shape, out_shape, out_shape, jax.ShapeDtypeStruct((2, 8, c), F32)],
        compiler_params=_params(),
    )(p, p, dqn, dkn, dv, qg, kg, headavg)


def _place():
    x, y, c = lax.axis_index("x"), lax.axis_index("y"), lax.axis_index("c")
    chips = [(1 - x, y), (x, 1 - y), (1 - x, 1 - y)]
    return x, y, c, chips


def _shard_of(ref, axis, chip, width):
    idx = [slice(None)] * len(ref.shape)
    idx[axis] = pl.ds(chip * width, width)
    return ref.at[tuple(idx)]


def _place_shard(w, axis, chip_idx, dtype, *, name):
    layers = w.shape[0]
    rows, cols = _as_rows(w.shape[1:])
    if axis == len(w.shape) - 2:
        tr = _row_tile(rows, cols, 2)
        out_shape = (layers, rows, cols * N_CHIPS)
        out_spec = pl.BlockSpec((None, tr, cols), lambda l, i, j_ref: (l, i, j_ref[0]))
    else:
        assert axis == 0
        tr = _row_tile(rows, cols, 2)
        per = rows // tr
        out_shape = (layers, rows * N_CHIPS, cols)
        out_spec = pl.BlockSpec((None, tr, cols), lambda l, i, j_ref: (l, j_ref[0] * per + i, 0))
    full = list(w.shape)
    full[1 + axis] *= N_CHIPS

    def body(j_ref, w_ref, o_ref):
        o_ref[...] = w_ref[...].astype(dtype)

    out = pl.pallas_call(
        body,
        name=name,
        grid_spec=pltpu.PrefetchScalarGridSpec(
            num_scalar_prefetch=1,
            grid=(layers, rows // tr),
            in_specs=[pl.BlockSpec((None, tr, cols), lambda l, i, j_ref: (l, i, 0))],
            out_specs=out_spec,
        ),
        out_shape=jax.ShapeDtypeStruct(out_shape, dtype),
        compiler_params=_params(),
    )(chip_idx, w.reshape(layers, rows, cols))
    return out.reshape(full)


def _gather_weights(placed, axes, *, name):
    n = len(placed)
    widths = [pa.shape[1 + ax] // N_CHIPS for pa, ax in zip(placed, axes)]

    def body(*refs):
        outs = refs[n:2 * n]
        send_sem, recv_sem, fsend_sem, frecv_sem = refs[2 * n:]
        x, y, c, chips = _place()
        me = 2 * x + y
        sibling = (x, y, 1 - c)

        def slot(a, layer, chip):
            return _shard_of(outs[a].at[layer], axes[a], chip, widths[a])

        def send(a, k):
            return pltpu.make_async_remote_copy(
                src_ref=slot(a, c, me), dst_ref=slot(a, c, me), send_sem=send_sem.at[a, k], recv_sem=recv_sem.at[a, k],
                device_id=(*chips[k], c), device_id_type=MESH)

        def landed(a, k):
            cx, cy = chips[k]
            blk = slot(a, c, 2 * cx + cy)
            return pltpu.make_async_remote_copy(
                src_ref=blk, dst_ref=blk, send_sem=send_sem.at[a, k], recv_sem=recv_sem.at[a, k],
                device_id=(*chips[k], c), device_id_type=MESH)

        def forward(a, k, layer):
            cx, cy = chips[k]
            blk = slot(a, layer, 2 * cx + cy)
            return pltpu.make_async_remote_copy(
                src_ref=blk, dst_ref=blk, send_sem=fsend_sem.at[a, k], recv_sem=frecv_sem.at[a, k],
                device_id=sibling, device_id_type=MESH)

        for a in range(n):
            for k in range(3):
                send(a, k).start()
        for a in range(n):
            for k in range(3):
                landed(a, k).wait_recv()
                forward(a, k, c).start()
        for a in range(n):
            for k in range(3):
                forward(a, k, 1 - c).wait_recv()
        for a in range(n):
            for k in range(3):
                send(a, k).wait_send()
                forward(a, k, c).wait_send()

    return pl.pallas_call(
        body,
        name=name,
        in_specs=[ANY] * n,
        out_specs=[ANY] * n,
        out_shape=[jax.ShapeDtypeStruct(pa.shape, pa.dtype) for pa in placed],
        input_output_aliases={a: a for a in range(n)},
        scratch_shapes=[pltpu.SemaphoreType.DMA((n, 3))] * 4,
    )(*placed)


def _pair_exchange(grads, *, name):
    n = len(grads)

    def body(*refs):
        ins, outs = refs[:n], refs[n:2 * n]
        send_sem, recv_sem = refs[2 * n:]
        x, y, c, _ = _place()
        copies = [
            pltpu.make_async_remote_copy(
                src_ref=ins[a].at[1 - c], dst_ref=outs[a], send_sem=send_sem.at[a], recv_sem=recv_sem.at[a],
                device_id=(x, y, 1 - c), device_id_type=MESH)
            for a in range(n)
        ]
        for cp in copies:
            cp.start()
        for cp in copies:
            cp.wait()

    return pl.pallas_call(
        body,
        name=name,
        in_specs=[ANY] * n,
        out_specs=[ANY] * n,
        out_shape=[jax.ShapeDtypeStruct(g.shape[1:], g.dtype) for g in grads],
        scratch_shapes=[pltpu.SemaphoreType.DMA((n,)), pltpu.SemaphoreType.DMA((n,))],
    )(*grads)


def _chip_scatter(sums, axes, *, name):
    n = len(sums)
    widths = [sm.shape[ax] // N_CHIPS for sm, ax in zip(sums, axes)]
    out_shapes = []
    for sm, ax, w in zip(sums, axes, widths):
        shp = list(sm.shape)
        shp[ax] = w
        out_shapes.append(jax.ShapeDtypeStruct((3, *shp), sm.dtype))

    def body(*refs):
        ins, outs = refs[:n], refs[n:2 * n]
        send_sem, recv_sem = refs[2 * n:]
        x, y, c, chips = _place()
        copies = []
        for a in range(n):
            for k in range(3):
                cx, cy = chips[k]
                copies.append(pltpu.make_async_remote_copy(
                    src_ref=_shard_of(ins[a], axes[a], 2 * cx + cy, widths[a]), dst_ref=outs[a].at[k],
                    send_sem=send_sem.at[a, k], recv_sem=recv_sem.at[a, k], device_id=(cx, cy, c), device_id_type=MESH))
        for cp in copies:
            cp.start()
        for cp in copies:
            cp.wait()

    return pl.pallas_call(
        body,
        name=name,
        in_specs=[ANY] * n,
        out_specs=[ANY] * n,
        out_shape=out_shapes,
        scratch_shapes=[pltpu.SemaphoreType.DMA((n, 3)), pltpu.SemaphoreType.DMA((n, 3))],
    )(*sums)


def _pair_share(finals, *, name):
    n = len(finals)

    def body(*refs):
        outs = refs[n:2 * n]
        send_sem, recv_sem = refs[2 * n:]
        x, y, c, _ = _place()
        for a in range(n):
            pltpu.make_async_remote_copy(
                src_ref=outs[a].at[c], dst_ref=outs[a].at[c], send_sem=send_sem.at[a], recv_sem=recv_sem.at[a],
                device_id=(x, y, 1 - c), device_id_type=MESH).start()
        for a in range(n):
            pltpu.make_async_remote_copy(
                src_ref=outs[a].at[c], dst_ref=outs[a].at[1 - c], send_sem=send_sem.at[a], recv_sem=recv_sem.at[a],
                device_id=(x, y, 1 - c), device_id_type=MESH).wait()

    return pl.pallas_call(
        body,
        name=name,
        in_specs=[ANY] * n,
        out_specs=[ANY] * n,
        out_shape=[jax.ShapeDtypeStruct(f.shape, f.dtype) for f in finals],
        input_output_aliases={a: a for a in range(n)},
        scratch_shapes=[pltpu.SemaphoreType.DMA((n,))] * 2,
    )(*finals)


def _small_view(shape):
    size = math.prod(shape)
    return (size // 128, 128) if size % 1024 == 0 else (shape[0], size // shape[0])


def _all_reduce_small(parts, *, name):
    n = len(parts)
    views = [_small_view(a.shape) for a in parts]

    def body(*refs):
        ins, outs, slots = refs[:n], refs[n:2 * n], refs[2 * n:3 * n]
        send_sem, recv_sem = refs[3 * n:]
        x, y, c, _ = _place()
        me = 4 * x + 2 * y + c
        copies = []
        for a in range(n):
            slots[a][me] = ins[a][...]
            for k in range(1, N_DEV):
                peer = (x ^ (k >> 2), y ^ ((k >> 1) & 1), c ^ (k & 1))
                copies.append(pltpu.make_async_remote_copy(
                    src_ref=ins[a], dst_ref=slots[a].at[me], send_sem=send_sem.at[a, k - 1],
                    recv_sem=recv_sem.at[a, k - 1], device_id=peer, device_id_type=MESH))
        for cp in copies:
            cp.start()
        for a in range(n):
            for k in range(1, N_DEV):
                pltpu.make_async_remote_copy(
                    src_ref=ins[a], dst_ref=slots[a].at[me ^ k], send_sem=send_sem.at[a, k - 1],
                    recv_sem=recv_sem.at[a, k - 1], device_id=(x, y, c), device_id_type=MESH).wait()
        for a in range(n):
            total = slots[a][0]
            for dev in range(1, N_DEV):
                total = total + slots[a][dev]
            outs[a][...] = total

    vmem = pl.BlockSpec(memory_space=pltpu.VMEM)
    outs = pl.pallas_call(
        body,
        name=name,
        in_specs=[vmem] * n,
        out_specs=[vmem] * n,
        out_shape=[jax.ShapeDtypeStruct(view, F32) for view in views],
        scratch_shapes=[pltpu.VMEM((N_DEV, *view), F32) for view in views]
        + [pltpu.SemaphoreType.DMA((n, N_DEV - 1)), pltpu.SemaphoreType.DMA((n, N_DEV - 1))],
        compiler_params=_params(),
    )(*[a.reshape(view) for a, view in zip(parts, views)])
    return [o.reshape(a.shape) for o, a in zip(outs, parts)]


def _as_rows(shape):
    cols = shape[-1]
    return math.prod(shape[:-1]), cols


ELEMENTWISE_VMEM = 24 * 1024 * 1024


def _row_tile(rows, cols, n_arrays, sublanes=8):
    cap = ELEMENTWISE_VMEM // (n_arrays * 2 * 4 * cols)
    best = None
    for t in range(sublanes, min(rows, cap) + 1, sublanes):
        if rows % t == 0:
            best = t
    assert best is not None, (rows, cols)
    return best


def _pair_sum(g, other, c_idx, *, name):
    rows, cols = _as_rows(other.shape)
    tr = _row_tile(rows, cols, 4, sublanes=16)
    g2 = g.reshape(2, rows, cols)

    def body(c_ref, g_ref, o_ref, out_ref, wire_ref):
        total = g_ref[...] + o_ref[...]
        out_ref[...] = total
        wire_ref[...] = total.astype(WIRE_DTYPE)

    blk = pl.BlockSpec((tr, cols), lambda i, c_ref: (i, 0))
    out, wire = pl.pallas_call(
        body,
        name=name,
        grid_spec=pltpu.PrefetchScalarGridSpec(
            num_scalar_prefetch=1,
            grid=(rows // tr,),
            in_specs=[pl.BlockSpec((None, tr, cols), lambda i, c_ref: (c_ref[0], i, 0)), blk],
            out_specs=[blk, blk],
        ),
        out_shape=[jax.ShapeDtypeStruct((rows, cols), F32), jax.ShapeDtypeStruct((rows, cols), WIRE_DTYPE)],
        compiler_params=_params(),
    )(c_idx, g2, other.reshape(rows, cols))
    return out.reshape(other.shape), wire.reshape(other.shape)


def _chip_sum(mine, got, axis, chip_idx, c_idx, *, name):
    shard_shape = got.shape[1:]
    rows, cols = _as_rows(shard_shape)
    tr = _row_tile(rows, cols, 5, sublanes=16)
    if axis == len(mine.shape) - 1:
        m2 = mine.reshape(rows, cols * N_CHIPS)
        mine_spec = pl.BlockSpec((tr, cols), lambda i, j_ref, c_ref: (i, j_ref[0]))
    else:
        assert axis == 0
        m2 = mine.reshape(N_CHIPS, rows, cols)
        mine_spec = pl.BlockSpec((None, tr, cols), lambda i, j_ref, c_ref: (j_ref[0], i, 0))

    def body(j_ref, c_ref, m_ref, got_ref, out_ref):
        out_ref[...] = ((m_ref[...] + got_ref[0].astype(F32)) + got_ref[1].astype(F32)) + got_ref[2].astype(F32)

    out = pl.pallas_call(
        body,
        name=name,
        grid_spec=pltpu.PrefetchScalarGridSpec(
            num_scalar_prefetch=2,
            grid=(rows // tr,),
            in_specs=[mine_spec, pl.BlockSpec((3, tr, cols), lambda i, j_ref, c_ref: (0, i, 0))],
            out_specs=pl.BlockSpec((None, tr, cols), lambda i, j_ref, c_ref: (c_ref[0], i, 0)),
        ),
        out_shape=jax.ShapeDtypeStruct((2, rows, cols), F32),
        compiler_params=_params(),
    )(chip_idx, c_idx, m2, got.reshape(3, rows, cols))
    return out.reshape((2, *shard_shape))


def _adamw_update(w_ref, g_ref, m_ref, v_ref, d_ref, nm_ref, nv_ref):
    c1 = 1.0 / (1.0 - ADAM_B1 ** ADAM_STEP)
    c2 = 1.0 / (1.0 - ADAM_B2 ** ADAM_STEP)
    gg = g_ref[...]
    nm = ADAM_B1 * m_ref[...] + (1.0 - ADAM_B1) * gg
    nv = ADAM_B2 * v_ref[...] + (1.0 - ADAM_B2) * (gg * gg)
    nm_ref[...] = nm
    nv_ref[...] = nv
    d_ref[...] = -ADAM_LR * ((nm * c1) / (jnp.sqrt(nv * c2) + ADAM_EPS) + ADAM_WD * w_ref[...])


def _adamw_small(ws, gs, ms, vs, *, name):
    n = len(ws)
    views = [_small_view(a.shape) for a in ws]

    def body(*refs):
        w_r, g_r, m_r, v_r = (refs[k * n:(k + 1) * n] for k in range(4))
        d_r, nm_r, nv_r = (refs[(4 + k) * n:(5 + k) * n] for k in range(3))
        for a in range(n):
            _adamw_update(w_r[a], g_r[a], m_r[a], v_r[a], d_r[a], nm_r[a], nv_r[a])

    vmem = pl.BlockSpec(memory_space=pltpu.VMEM)
    flat = lambda arrs: [a.reshape(view) for a, view in zip(arrs, views)]
    outs = pl.pallas_call(
        body,
        name=name,
        in_specs=[vmem] * (4 * n),
        out_specs=[vmem] * (3 * n),
        out_shape=[jax.ShapeDtypeStruct(view, F32) for view in views] * 3,
        compiler_params=_params(),
    )(*flat(ws), *flat(gs), *flat(ms), *flat(vs))
    return [tuple(outs[k * n + a].reshape(ws[a].shape) for k in range(3)) for a in range(n)]


def _adamw(w, g, m, v, *, name):
    shape = w.shape
    rows, cols = _as_rows(shape)
    tr = _row_tile(rows, cols, 7)
    body = functools.partial(_adamw_update)

    blk = pl.BlockSpec((tr, cols), lambda i: (i, 0))
    flat = lambda a: a.reshape(rows, cols)
    outs = pl.pallas_call(
        body,
        name=name,
        grid=(rows // tr,),
        in_specs=[blk] * 4,
        out_specs=[blk] * 3,
        out_shape=[jax.ShapeDtypeStruct((rows, cols), F32)] * 3,
        compiler_params=_params(),
    )(flat(w), flat(g), flat(m), flat(v))
    return tuple(o.reshape(shape) for o in outs)


def _cols_blocked(a, t):
    return a.reshape(a.shape[0] // t, t, HEADS, HEAD_DIM).transpose(2, 0, 3, 1)


def _cols(a):
    return a.reshape(a.shape[0], HEADS, HEAD_DIM).transpose(1, 2, 0)


def _from_cols(a):
    h, d, s = a.shape
    return a.transpose(2, 0, 1).reshape(s, h * d)


def _from_cols_blocked(a):
    h, nb, d, t = a.shape
    return a.transpose(1, 3, 0, 2).reshape(nb * t, h * d)


BIG = ("w_in", "w_branch_out", "w_o", "w_gate_up", "w_down")
BIG_AXIS = {"w_in": 1, "w_branch_out": 2, "w_o": 0, "w_gate_up": 1, "w_down": 0}
SMALL = ("mix_norm_g", "b_gate", "conv_w", "conv_b", "sgu_ln_g", "sgu_ln_b", "sgu_w", "sgu_b", "q_norm_g", "k_norm_g",
         "ffn_norm_g")
ORDER = ("mix_norm_g", "w_in", "b_gate", "conv_w", "conv_b", "sgu_ln_g", "sgu_ln_b", "sgu_w", "sgu_b", "q_norm_g",
         "k_norm_g", "w_branch_out", "w_o", "ffn_norm_g", "w_gate_up", "w_down")


def _layer_forward(x, w, l):
    t = ATT_BLOCK
    c = w["conv_b"].shape[1]
    n_in = w["w_in"].shape[2]
    gate_col0 = (n_in - 3 * x.shape[1]) // x.shape[1]
    tag = f"l{l}"
    p, h = _norm_matmul(x, w["mix_norm_g"][l][None], w["w_in"], l, name=f"in_proj_{tag}", tn=n_in // 8)
    ya = _conv_fwd(p, w["conv_w"][l], w["conv_b"][l][None], name=f"conv_{tag}")
    yb = _sgu_fwd(p, w["sgu_ln_g"][l][None], w["sgu_ln_b"][l][None], w["wtril"][l], w["bias_full"][l], col0=3,
                  name=f"sgu_{tag}")
    qn, kn, vb = _qkv_prep(p, w["qg"][l], w["kg"][l], w["headavg"], col0=5, name=f"qkv_{tag}")
    q_c, k_cb, v_cb = _cols(qn), _cols_blocked(kn, t), _cols_blocked(vb, t)
    out_t, runs = _att_fwd(k_cb, q_c, v_cb, w["later"], name=f"att_{tag}")
    yc = _from_cols(out_t)
    x1, merged = _merge_fwd((ya, yb, yc), p, w["b_gate"][l][None], w["w_branch_out"], w["w_o"], l, x,
                            gate_col0=gate_col0, name=f"merge_{tag}")
    gu, h2 = _norm_matmul(x1, w["ffn_norm_g"][l][None], w["w_gate_up"], l, name=f"gate_up_{tag}",
                          tn=w["w_gate_up"].shape[2] // 4)
    x2 = _ffn_down(gu, w["w_down"], l, x1, name=f"down_{tag}")
    saved = dict(x=x, p=p, h=h, ya=ya, yb=yb, yc=yc, merged=merged, x1=x1, gu=gu, h2=h2,
                 att=(k_cb, v_cb, q_c, runs), gate_col0=gate_col0)
    return x2, saved


def _layer_backward(dx2, w, sv, l, big):
    layers = w["w_in"].shape[0]
    t = ATT_BLOCK
    tag = f"l{l}"
    d = dx2.shape[1]
    c = w["conv_b"].shape[1]
    n_in = w["w_in"].shape[2]
    n_ff = w["w_gate_up"].shape[2]
    g = {}
    dgu, act, dx2b = _ffn_bwd(dx2, sv["gu"], w["w_down"], l, name=f"down_bwd_{tag}")
    big["w_down"] = _matmul_tn(act, dx2b, l, layers, big.get("w_down"), name=f"dw_down_{tag}", t1=act.shape[1] // 2)
    dx1, dg2 = _matmul_nt_normbwd(dgu, w["w_gate_up"], l, sv["x1"], w["ffn_norm_g"][l][None], dx2,
                                  name=f"gate_up_bwd_{tag}", tk=n_ff // 4)
    g["ffn_norm_g"] = jnp.sum(dg2, axis=0)
    big["w_gate_up"] = _matmul_tn(sv["h2"], dgu, l, layers, big.get("w_gate_up"), name=f"dw_gate_up_{tag}", tn=n_ff // 4)
    ys = (sv["ya"], sv["yb"], sv["yc"])
    (dgates, dya, dyb, dyc, dd0, dd1, dd2, dx1b, dbg) = _merge_bwd(
        dx1, ys, sv["p"], w["b_gate"][l][None], w["w_branch_out"], w["w_o"], l, gate_col0=sv["gate_col0"],
        name=f"merge_bwd_{tag}")
    g["b_gate"] = jnp.sum(dbg, axis=0)
    big["w_o"] = _matmul_tn(sv["merged"], dx1b, l, layers, big.get("w_o"), name=f"dw_o_{tag}")
    for i, (y, dd) in enumerate(zip(ys, (dd0, dd1, dd2))):
        big["w_branch_out"] = _matmul_tn(y, dd, len(ys) * l + i, len(ys) * layers, big.get("w_branch_out"),
                                         name=f"dw_bo{i}_{tag}")
    dconv, dwc = _conv_bwd(sv["p"], dya, w["conv_w"][l], w["conv_b"][l][None], name=f"conv_bwd_{tag}")
    dwc = jnp.sum(dwc, axis=1)
    g["conv_w"] = dwc[0:CONV_K]
    g["conv_b"] = dwc[CONV_K]
    dsu, dsv, dln, dws, dbias = _sgu_bwd(sv["p"], dyb, w["sgu_ln_g"][l][None], w["sgu_ln_b"][l][None], w["wtril"][l],
                                         w["wtril_t"][l], w["bias_full"][l], col0=3, name=f"sgu_bwd_{tag}")
    dln = jnp.sum(dln, axis=1)
    g["sgu_ln_g"], g["sgu_ln_b"] = dln[0], dln[1]
    g["sgu_w"] = jnp.where(w["tril"], dws, 0.0)
    g["sgu_b"] = jnp.sum(dbias.reshape(CHUNK, c // CHUNK, CHUNK), axis=2).T
    k_cb, v_cb, q_c, runs = sv["att"]
    dqt, dkt, dvt = _att_bwd(k_cb, v_cb, q_c, _cols(_mx(dyc)), runs, w["later"], w["earlier"], name=f"att_bwd_{tag}")
    dqn, dkn, dvv = _from_cols(dqt), _from_cols_blocked(dkt), _from_cols_blocked(dvt)
    dq, dk_, dvb, dqkg = _qkv_bwd(sv["p"], dqn, dkn, dvv, w["qg"][l], w["kg"][l], w["headavg"], col0=5,
                                  name=f"qkv_bwd_{tag}")
    dqkg = jnp.sum(dqkg.reshape(2, 8 * HEADS, HEAD_DIM), axis=1)
    g["q_norm_g"], g["k_norm_g"] = dqkg[0], dqkg[1]
    dp = jnp.concatenate([dconv, dsu, dsv, dq, dk_, dvb, dgates], axis=1)
    dx0, dg1 = _matmul_nt_normbwd(dp, w["w_in"], l, sv["x"], w["mix_norm_g"][l][None], dx1, name=f"in_proj_bwd_{tag}",
                                  tk=n_in // 8)
    g["mix_norm_g"] = jnp.sum(dg1, axis=0)
    big["w_in"] = _matmul_tn(sv["h"], dp, l, layers, big.get("w_in"), name=f"dw_in_{tag}", tn=n_in // 4)
    return dx0, g


def kernel(x, mix_norm_g, w_in, b_gate, conv_w, conv_b, sgu_ln_g, sgu_ln_b, sgu_w, sgu_b, q_norm_g, k_norm_g, w_branch_out, w_o, ffn_norm_g, w_gate_up, w_down, loss_target, m_mix_norm_g, m_w_in, m_b_gate, m_conv_w, m_conv_b, m_sgu_ln_g, m_sgu_ln_b, m_sgu_w, m_sgu_b, m_q_norm_g, m_k_norm_g, m_w_branch_out, m_w_o, m_ffn_norm_g, m_w_gate_up, m_w_down, v_mix_norm_g, v_w_in, v_b_gate, v_conv_w, v_conv_b, v_sgu_ln_g, v_sgu_ln_b, v_sgu_w, v_sgu_b, v_q_norm_g, v_k_norm_g, v_w_branch_out, v_w_o, v_ffn_norm_g, v_w_gate_up, v_w_down):
    given = dict(locals())
    params = {n: given[n] for n in ORDER}
    moms = {n: (given["m_" + n], given["v_" + n]) for n in ORDER}
    layers = mix_norm_g.shape[0]
    assert layers == 2, "the exchanges split the work of a chip's two cores by layer"
    xs = x[0]
    target = loss_target[0]
    chip = 2 * lax.axis_index("x") + lax.axis_index("y")
    core = lax.axis_index("c")

    c_idx = core.reshape(1).astype(jnp.int32)
    j_idx = chip.reshape(1).astype(jnp.int32)
    width = conv_w.shape[2]
    placed = [_place_shard(params[n], BIG_AXIS[n], j_idx, MXU_DTYPE, name=f"place_{n}") for n in BIG]
    placed.append(lax.dynamic_update_slice_in_dim(jnp.zeros((layers, CONV_K, width * N_CHIPS), F32), conv_w, chip * width, 2))
    gathered = _gather_weights(placed, [BIG_AXIS[n] for n in BIG] + [1], name="gather_weights")
    w = dict(zip(BIG + ("conv_w",), gathered))
    for n in ("mix_norm_g", "b_gate", "conv_b", "sgu_ln_g", "sgu_ln_b", "ffn_norm_g"):
        w[n] = params[n]
    groups = sgu_w.shape[1]
    tril = jnp.tril(jnp.ones((CHUNK, CHUNK), dtype=bool))
    w["tril"] = tril
    w["wtril"] = _mx(jnp.where(tril, sgu_w, 0.0))
    w["wtril_t"] = w["wtril"].transpose(0, 1, 3, 2)
    w["bias_full"] = jnp.repeat(sgu_b.transpose(0, 2, 1), CHUNK, axis=2)
    w["qg"] = jnp.tile(q_norm_g, (1, HEADS))[:, None, :]
    w["kg"] = jnp.tile(k_norm_g, (1, HEADS))[:, None, :]
    lane = jnp.arange(HEADS * HEAD_DIM) // HEAD_DIM
    w["headavg"] = _mx(jnp.where(lane[:, None] == lane[None, :], 1.0 / HEAD_DIM, 0.0))
    pos = jnp.arange(ATT_BLOCK)
    w["later"] = _mx(jnp.where(pos[None, :] > pos[:, None], 1.0, 0.0))
    w["earlier"] = _mx(jnp.where(pos[None, :] < pos[:, None], 1.0, 0.0))

    saved = []
    act = xs
    for l in range(layers):
        act, sv = _layer_forward(act, w, l)
        saved.append(sv)
    loss_part, dact = _loss_head(act, target, name="loss_head")
    loss = lax.psum(jnp.sum(loss_part), ("x", "y", "c"))
    grads = [None] * layers
    big = {}
    for l in reversed(range(layers)):
        dact, grads[l] = _layer_backward(dact, w, saved[l], l, big)
    grad_x = dact[None]
    local = {n: jnp.stack([grads[l][n] for l in range(layers)]) for n in SMALL}
    for n in BIG:
        local[n] = big[n].reshape(w[n].shape)

    got = _pair_exchange([local[n] for n in BIG], name="grads_pair_exchange")
    pair = [_pair_sum(local[n], o, c_idx, name=f"pair_sum_{n}") for n, o in zip(BIG, got)]
    landed = _chip_scatter([wire for _, wire in pair], [BIG_AXIS[n] for n in BIG], name="grads_chip_scatter")
    final = [_chip_sum(pr, ld, BIG_AXIS[n], j_idx, c_idx, name=f"chip_sum_{n}") for n, (pr, _), ld in zip(BIG, pair, landed)]
    full = dict(zip(BIG, _pair_share(final, name="grads_pair_share")))

    summed = _all_reduce_small([local[n] for n in SMALL], name="grads_all_reduce_small")
    for n, gsum in zip(SMALL, summed):
        full[n] = gsum
    full["conv_w"] = lax.dynamic_slice_in_dim(full["conv_w"], chip * width, width, axis=2)

    out = {}
    for n in BIG:
        out[n] = _adamw(params[n], full[n], *moms[n], name=f"adamw_{n}")
    small = _adamw_small([params[n] for n in SMALL], [full[n] for n in SMALL], [moms[n][0] for n in SMALL],
                         [moms[n][1] for n in SMALL], name="adamw_small")
    for n, triple in zip(SMALL, small):
        out[n] = triple
    return (loss, grad_x, *[full[n] for n in ORDER], *[out[n][0] for n in ORDER], *[out[n][1] for n in ORDER],
            *[out[n][2] for n in ORDER])
```

```python
import functools
import math

import jax
import jax.numpy as jnp
from jax import lax
from jax.experimental import pallas as pl
from jax.experimental.pallas import tpu as pltpu

F32 = jnp.float32
MXU_DTYPE = jnp.bfloat16
WIRE_DTYPE = jnp.bfloat16
ACT_DTYPE = jnp.bfloat16
HALO = 16

EPS = 1e-6
CONV_K = 3
CHUNK = 128
HEADS = 8
HEAD_DIM = 64
ATT_BLOCK = 128
EXP_UNDERFLOW = -104.0

ADAM_LR = 0.001
ADAM_B1 = 0.9
ADAM_B2 = 0.999
ADAM_EPS = 1e-08
ADAM_WD = 0.01
ADAM_STEP = 10

VMEM_LIMIT = 56 * 1024 * 1024
MESH = pl.DeviceIdType.MESH
N_CHIPS = 4
N_DEV = 8
ANY = pl.BlockSpec(memory_space=pl.ANY)


def _params(**kw):
    return pltpu.CompilerParams(vmem_limit_bytes=VMEM_LIMIT, **kw)


def _mx(v):
    return v.astype(MXU_DTYPE)


def _dot(a, b):
    return lax.dot_general(a, b, (((1,), (0,)), ((), ())), preferred_element_type=F32)


def _dot_nt(a, b):
    return lax.dot_general(a, b, (((1,), (1,)), ((), ())), preferred_element_type=F32)


def _dot_tn(a, b):
    return lax.dot_general(a, b, (((0,), (0,)), ((), ())), preferred_element_type=F32)


def _dot_split(const, v):
    hi = _mx(v)
    lo = _mx(v - hi.astype(F32))
    return _dot(const, hi) + _dot(const, lo)


def _dot_split_r(v, const):
    hi = _mx(v)
    lo = _mx(v - hi.astype(F32))
    return _dot(hi, const) + _dot(lo, const)


def _sigmoid(x):
    return 1.0 / (1.0 + jnp.exp(-x))


_INV_SQRT2 = 1.0 / math.sqrt(2.0)
_INV_SQRT2PI = 1.0 / math.sqrt(2.0 * math.pi)


def _gelu(x):
    return 0.5 * x * (1.0 + lax.erf(x * _INV_SQRT2))


def _gelu_grad(x):
    return 0.5 * (1.0 + lax.erf(x * _INV_SQRT2)) + x * jnp.exp(-0.5 * x * x) * _INV_SQRT2PI


def _row_fold(v):
    m, n = v.shape
    return jnp.sum(v.reshape(m // 8, 8, n), axis=0)


def _tile(m, pref):
    t = min(m, pref)
    while m % t:
        t //= 2
    return t


def _resident(block_shape, index_map):
    return pl.BlockSpec(block_shape, index_map, pipeline_mode=pl.Buffered(1))


def _norm_matmul(x, g, w, l, *, name, tm=512, tn=None):
    s, d = x.shape
    n = w.shape[2]
    tm = _tile(s, tm)
    tn = tn or n
    assert n % tn == 0

    def body(x_ref, g_ref, w_ref, p_ref, h_ref):
        xf = x_ref[...]
        r = lax.rsqrt(jnp.mean(xf * xf, axis=-1, keepdims=True) + EPS)
        h = _mx(xf * r * g_ref[...])
        h_ref[...] = h
        for j in range(n // tn):
            cols = slice(j * tn, (j + 1) * tn)
            p_ref[:, cols] = _dot(h, w_ref[:, cols]).astype(ACT_DTYPE)

    return pl.pallas_call(
        body,
        name=name,
        grid=(s // tm,),
        in_specs=[
            pl.BlockSpec((tm, d), lambda i: (i, 0)),
            pl.BlockSpec((1, d), lambda i: (0, 0)),
            _resident((None, d, n), lambda i: (l, 0, 0)),
        ],
        out_specs=[
            pl.BlockSpec((tm, n), lambda i: (i, 0)),
            pl.BlockSpec((tm, d), lambda i: (i, 0)),
        ],
        out_shape=[jax.ShapeDtypeStruct((s, n), ACT_DTYPE), jax.ShapeDtypeStruct((s, d), MXU_DTYPE)],
        compiler_params=_params(),
    )(x, g, w)


def _conv_taps(u, prev):
    row = lax.broadcasted_iota(jnp.int32, u.shape, 0)
    last, before = prev[HALO - 1:HALO, :], prev[HALO - 2:HALO - 1, :]
    um1 = jnp.where(row == 0, last, pltpu.roll(u, 1, 0))
    um2 = pltpu.roll(u, 2, 0)
    um2 = jnp.where(row == 0, before, jnp.where(row == 1, last, um2))
    return um1, um2


def _f32(ref):
    return ref[...].astype(F32)


def _conv_fwd(p, conv_w, conv_b, *, name, tm=512):
    s = p.shape[0]
    c = conv_w.shape[1]
    tm = _tile(s, tm)
    hb = tm // HALO

    def body(bg_ref, cg_ref, xa_ref, cgp_ref, xap_ref, w_ref, b_ref, y_ref):
        first = pl.program_id(0) == 0
        u = _f32(cg_ref) * _f32(xa_ref)
        prev = jnp.where(first, 0.0, _f32(cgp_ref) * _f32(xap_ref))
        um1, um2 = _conv_taps(u, prev)
        w = w_ref[...]
        y = b_ref[...] + w[0:1, :] * um2 + w[1:2, :] * um1 + w[2:3, :] * u
        y_ref[...] = _mx(_f32(bg_ref) * y)

    halo = lambda col: pl.BlockSpec((HALO, c), lambda i: (jnp.maximum(i * hb - 1, 0), col))
    return pl.pallas_call(
        body,
        name=name,
        grid=(s // tm,),
        in_specs=[
            pl.BlockSpec((tm, c), lambda i: (i, 0)),
            pl.BlockSpec((tm, c), lambda i: (i, 1)),
            pl.BlockSpec((tm, c), lambda i: (i, 2)),
            halo(1),
            halo(2),
            pl.BlockSpec((CONV_K, c), lambda i: (0, 0)),
            pl.BlockSpec((1, c), lambda i: (0, 0)),
        ],
        out_specs=pl.BlockSpec((tm, c), lambda i: (i, 0)),
        out_shape=jax.ShapeDtypeStruct((s, c), MXU_DTYPE),
        compiler_params=_params(),
    )(p, p, p, p, p, conv_w, conv_b)


def _layernorm_stats(x):
    mu = jnp.mean(x, axis=-1, keepdims=True)
    xc = x - mu
    r = lax.rsqrt(jnp.mean(xc * xc, axis=-1, keepdims=True) + EPS)
    return xc * r, r


def _sgu_fwd(p, ln_g, ln_b, wtril, bias_full, *, col0, name, tm=512):
    s = p.shape[0]
    c = ln_g.shape[1]
    groups = c // CHUNK
    tm = _tile(s, tm)

    def body(u_ref, v_ref, g_ref, b_ref, w_ref, bias_ref, y_ref):
        vn, _ = _layernorm_stats(_gelu(_f32(v_ref)))
        vb = _mx(vn * g_ref[...] + b_ref[...])
        for n in range(tm // CHUNK):
            rows = slice(n * CHUNK, (n + 1) * CHUNK)
            for gi in range(groups):
                cols = slice(gi * CHUNK, (gi + 1) * CHUNK)
                mixed = _dot(w_ref[gi], vb[rows, cols]) + bias_ref[:, cols]
                y_ref[rows, cols] = _mx(_gelu(u_ref[rows, cols].astype(F32)) * mixed)

    return pl.pallas_call(
        body,
        name=name,
        grid=(s // tm,),
        in_specs=[
            pl.BlockSpec((tm, c), lambda i: (i, col0)),
            pl.BlockSpec((tm, c), lambda i: (i, col0 + 1)),
            pl.BlockSpec((1, c), lambda i: (0, 0)),
            pl.BlockSpec((1, c), lambda i: (0, 0)),
            pl.BlockSpec((groups, CHUNK, CHUNK), lambda i: (0, 0, 0)),
            pl.BlockSpec((CHUNK, c), lambda i: (0, 0)),
        ],
        out_specs=pl.BlockSpec((tm, c), lambda i: (i, 0)),
        out_shape=jax.ShapeDtypeStruct((s, c), MXU_DTYPE),
        compiler_params=_params(),
    )(p, p, ln_g, ln_b, wtril, bias_full)


def _head_mean(v, headavg):
    return _dot_split_r(v, headavg)


def _qkv_prep(p, qg, kg, headavg, *, col0, name, tm=512):
    s = p.shape[0]
    c = qg.shape[1]
    tm = _tile(s, tm)

    def body(q_ref, k_ref, v_ref, qg_ref, kg_ref, avg_ref, qn_ref, kn_ref, vb_ref):
        for src, g_ref, dst in ((q_ref, qg_ref, qn_ref), (k_ref, kg_ref, kn_ref)):
            xf = _f32(src)
            r = lax.rsqrt(_head_mean(xf * xf, avg_ref[...]) + EPS)
            dst[...] = _mx(xf * r * g_ref[...])
        vb_ref[...] = _mx(v_ref[...])

    blk = lambda col: pl.BlockSpec((tm, c), lambda i: (i, col))
    vec = pl.BlockSpec((1, c), lambda i: (0, 0))
    out = pl.BlockSpec((tm, c), lambda i: (i, 0))
    return pl.pallas_call(
        body,
        name=name,
        grid=(s // tm,),
        in_specs=[blk(col0), blk(col0 + 1), blk(col0 + 2), vec, vec, pl.BlockSpec((c, c), lambda i: (0, 0))],
        out_specs=[out, out, out],
        out_shape=[jax.ShapeDtypeStruct((s, c), MXU_DTYPE)] * 3,
        compiler_params=_params(),
    )(p, p, p, qg, kg, headavg)


def _att_mask(t, key0, qry0):
    kpos = key0 + lax.broadcasted_iota(jnp.int32, (t, t), 0)
    qpos = qry0 + lax.broadcasted_iota(jnp.int32, (t, t), 1)
    return kpos < qpos


def _att_blocks(k_blks, q_ts, scale, mask):
    zs = [(_dot_tn(k, q) if k.shape[0] == q.shape[0] else _dot(k, q)) * scale for k, q in zip(k_blks, q_ts)]
    es = [jnp.exp(-jnp.abs(z)) for z in zs]
    lbs = [jnp.minimum(z, 0.0) - jnp.log1p(e) for z, e in zip(zs, es)]
    l1ms = [jnp.where(mask, lb - z, 0.0) for lb, z in zip(lbs, zs)]
    return zs, es, lbs, l1ms


def _dot_split_each(const, vs):
    his = [_mx(v) for v in vs]
    los = [_mx(v - hi.astype(F32)) for v, hi in zip(vs, his)]
    tops = [_dot(const, hi) for hi in his]
    return [top + _dot(const, lo) for top, lo in zip(tops, los)]


def _max_over(runs):
    m = runs[0]
    for r in runs[1:]:
        m = jnp.maximum(m, r)
    return jnp.max(m)


NOT_SEEN = -1e30


def _row_put(ref, g, j, row):
    j8 = pl.multiple_of((j // 8) * 8, 8)
    blk = ref[g, pl.ds(j8, 8), :]
    sub = lax.broadcasted_iota(jnp.int32, blk.shape, 0)
    ref[g, pl.ds(j8, 8), :] = jnp.where(sub == j - j8, row, blk)


def _row_get(ref, g, j):
    j8 = pl.multiple_of((j // 8) * 8, 8)
    blk = ref[g, pl.ds(j8, 8), :]
    sub = lax.broadcasted_iota(jnp.int32, blk.shape, 0)
    return jnp.sum(jnp.where(sub == j - j8, blk, 0.0), axis=0, keepdims=True)


def _att_fwd(k_hm, qt_hm, vt_hm, later, *, name, hg=8):
    h, nb, d, t = k_hm.shape
    s = nb * t
    nbp = -(-nb // 8) * 8
    scale = 1.0 / math.sqrt(d)
    assert h % hg == 0

    def body(k_ref, qt_ref, vt_ref, later_ref, o_ref, runs_ref):
        i = pl.program_id(1)
        q_ts = [qt_ref[g] for g in range(hg)]
        runs_ref[...] = jnp.full(runs_ref.shape, NOT_SEEN, F32)

        def cond(carry):
            j, _, _, rmax = carry
            return jnp.logical_and(j >= 0, rmax > EXP_UNDERFLOW)

        def step(carry):
            j, runs, accs, _ = carry
            mask = _att_mask(t, j * t, i * t)
            heads = range(hg)
            for g in heads:
                _row_put(runs_ref, g, j, runs[g])
            _, _, lbs, l1ms = _att_blocks([k_ref[g, j] for g in heads], q_ts, scale, mask)
            afters = _dot_split_each(later_ref[...], l1ms)
            weights = [_mx(jnp.where(mask, jnp.exp(lbs[g] + afters[g] + runs[g]), 0.0)) for g in heads]
            new_accs = [accs[g] + _dot(vt_ref[g, j], weights[g]) for g in heads]
            new_runs = [runs[g] + jnp.sum(l1ms[g], axis=0, keepdims=True) for g in heads]
            return j - 1, tuple(new_runs), tuple(new_accs), _max_over(new_runs)

        init = (i, tuple(jnp.zeros((1, t), F32) for _ in range(hg)), tuple(jnp.zeros((d, t), F32) for _ in range(hg)),
                jnp.float32(0.0))
        _, _, accs, _ = lax.while_loop(cond, step, init)
        for g in range(hg):
            o_ref[g] = _mx(accs[g])

    return pl.pallas_call(
        body,
        name=name,
        grid=(h // hg, nb),
        in_specs=[
            pl.BlockSpec((hg, nb, d, t), lambda hh, i: (hh, 0, 0, 0)),
            pl.BlockSpec((hg, d, t), lambda hh, i: (hh, 0, i)),
            pl.BlockSpec((hg, nb, d, t), lambda hh, i: (hh, 0, 0, 0)),
            pl.BlockSpec((t, t), lambda hh, i: (0, 0)),
        ],
        out_specs=[pl.BlockSpec((hg, d, t), lambda hh, i: (hh, 0, i)),
                   pl.BlockSpec((hg, None, nbp, t), lambda hh, i: (hh, i, 0, 0))],
        out_shape=[jax.ShapeDtypeStruct((h, d, s), MXU_DTYPE), jax.ShapeDtypeStruct((h, nb, nbp, t), F32)],
        compiler_params=_params(),
    )(k_hm, qt_hm, vt_hm, later)


def _merge_fwd(ys, p, b_gate, w_bo, w_o, l, x, *, gate_col0, name, tm=512):
    s, d = x.shape
    _, nbr, c, _ = w_bo.shape
    tm = _tile(s, tm)

    def body(ya_ref, yb_ref, yc_ref, g0_ref, g1_ref, g2_ref, bg_ref, wbo_ref, wo_ref, x_ref, x1_ref, m_ref):
        merged = jnp.zeros((tm, d), F32)
        for i, (y_ref, g_ref) in enumerate(((ya_ref, g0_ref), (yb_ref, g1_ref), (yc_ref, g2_ref))):
            gate = _sigmoid(_f32(g_ref) + bg_ref[:, i * d:(i + 1) * d])
            merged = merged + gate * _dot(y_ref[...], wbo_ref[i])
        mb = _mx(merged)
        m_ref[...] = mb
        x1_ref[...] = x_ref[...] + _dot(mb, wo_ref[...])

    yblk = pl.BlockSpec((tm, c), lambda i: (i, 0))
    gblk = lambda k: pl.BlockSpec((tm, d), lambda i: (i, gate_col0 + k))
    xblk = pl.BlockSpec((tm, d), lambda i: (i, 0))
    return pl.pallas_call(
        body,
        name=name,
        grid=(s // tm,),
        in_specs=[
            yblk, yblk, yblk, gblk(0), gblk(1), gblk(2),
            pl.BlockSpec((1, nbr * d), lambda i: (0, 0)),
            _resident((None, nbr, c, d), lambda i: (l, 0, 0, 0)),
            _resident((None, d, d), lambda i: (l, 0, 0)),
            xblk,
        ],
        out_specs=[xblk, xblk],
        out_shape=[jax.ShapeDtypeStruct((s, d), F32), jax.ShapeDtypeStruct((s, d), MXU_DTYPE)],
        compiler_params=_params(),
    )(*ys, p, p, p, b_gate, w_bo, w_o, x)


def _ffn_down(gu, w_down, l, x, *, name, tm=512):
    s, d = x.shape
    f = w_down.shape[1]
    tm = _tile(s, tm)

    def body(g_ref, u_ref, w_ref, x_ref, o_ref):
        g = _f32(g_ref)
        act = _mx(g * _sigmoid(g) * _f32(u_ref))
        o_ref[...] = x_ref[...] + _dot(act, w_ref[...])

    return pl.pallas_call(
        body,
        name=name,
        grid=(s // tm,),
        in_specs=[
            pl.BlockSpec((tm, f), lambda i: (i, 0)),
            pl.BlockSpec((tm, f), lambda i: (i, 1)),
            _resident((None, f, d), lambda i: (l, 0, 0)),
            pl.BlockSpec((tm, d), lambda i: (i, 0)),
        ],
        out_specs=pl.BlockSpec((tm, d), lambda i: (i, 0)),
        out_shape=jax.ShapeDtypeStruct((s, d), F32),
        compiler_params=_params(),
    )(gu, gu, w_down, x)


def _loss_head(y, target, *, name, tm=512):
    s, d = y.shape
    tm = _tile(s, tm)

    def body(y_ref, t_ref, l_ref, dy_ref):
        @pl.when(pl.program_id(0) == 0)
        def _():
            l_ref[...] = jnp.zeros_like(l_ref)

        err = y_ref[...] - t_ref[...]
        dy_ref[...] = err * (1.0 / d)
        sq = _row_fold(err * err)
        part = sq[:, 0:128]
        for k in range(1, d // 128):
            part = part + sq[:, k * 128:(k + 1) * 128]
        l_ref[...] += part * (0.5 / d)

    blk = pl.BlockSpec((tm, d), lambda i: (i, 0))
    return pl.pallas_call(
        body,
        name=name,
        grid=(s // tm,),
        in_specs=[blk, blk],
        out_specs=[pl.BlockSpec((8, 128), lambda i: (0, 0)), blk],
        out_shape=[jax.ShapeDtypeStruct((8, 128), F32), jax.ShapeDtypeStruct((s, d), F32)],
        compiler_params=_params(),
    )(y, target)


def _matmul_tn(a, b, slot, n_slots, into, *, name, t1=1024, tn=None, tm=1024, col0=0, n_total=None):
    m, k1 = a.shape
    n = b.shape[1]
    n_total = n_total or n
    t1 = _tile(k1, t1)
    tn = tn or n
    tm = _tile(m, tm)
    steps = m // tm
    assert n % tn == 0 and col0 % tn == 0
    cb0 = col0 // tn

    def body(a_ref, b_ref, *refs):
        o_ref = refs[-1]

        @pl.when(pl.program_id(2) == 0)
        def _():
            o_ref[...] = jnp.zeros_like(o_ref)

        o_ref[...] += _dot_tn(a_ref[...], b_ref[...])

    return pl.pallas_call(
        body,
        name=name,
        grid=(k1 // t1, n // tn, steps),
        in_specs=[
            pl.BlockSpec((tm, t1), lambda i, j, k: (k, i)),
            pl.BlockSpec((tm, tn), lambda i, j, k: (k, j)),
        ] + ([] if into is None else [ANY]),
        out_specs=pl.BlockSpec((None, t1, tn), lambda i, j, k: (slot, i, cb0 + j)),
        out_shape=jax.ShapeDtypeStruct((n_slots, k1, n_total), F32),
        input_output_aliases={} if into is None else {2: 0},
        compiler_params=_params(),
    )(a, b, *([] if into is None else [into]))


def _matmul_nt_normbwd(pieces, w, l, x, g, dres, *, name, tm=512, tk=1024):
    s, d = x.shape
    k = w.shape[2]
    tm = _tile(s, tm)
    n = len(pieces)
    assert sum(pc.shape[1] for pc in pieces) == k

    def body(*refs):
        piece_refs = refs[:n]
        w_ref, x_ref, g_ref, dres_ref, dx_ref, dg_ref = refs[n:]

        @pl.when(pl.program_id(0) == 0)
        def _():
            dg_ref[...] = jnp.zeros_like(dg_ref)

        dh = jnp.zeros((tm, d), F32)
        off = 0
        for pc, pc_ref in zip(pieces, piece_refs):
            width = pc.shape[1]
            step = _tile(width, tk)
            for c0 in range(0, width, step):
                dh = dh + _dot_nt(pc_ref[:, c0:c0 + step], w_ref[:, off + c0:off + c0 + step])
            off += width
        xf = x_ref[...]
        r = lax.rsqrt(jnp.mean(xf * xf, axis=-1, keepdims=True) + EPS)
        y = xf * r
        dy = dh * g_ref[...]
        dx_ref[...] = dres_ref[...] + r * (dy - y * jnp.mean(dy * y, axis=-1, keepdims=True))
        dg_ref[...] += _row_fold(dh * y)

    xblk = pl.BlockSpec((tm, d), lambda i: (i, 0))
    return pl.pallas_call(
        body,
        name=name,
        grid=(s // tm,),
        in_specs=[pl.BlockSpec((tm, pc.shape[1]), lambda i: (i, 0)) for pc in pieces] + [
            _resident((None, d, k), lambda i: (l, 0, 0)),
            xblk,
            pl.BlockSpec((1, d), lambda i: (0, 0)),
            xblk,
        ],
        out_specs=[xblk, pl.BlockSpec((8, d), lambda i: (0, 0))],
        out_shape=[jax.ShapeDtypeStruct((s, d), F32), jax.ShapeDtypeStruct((8, d), F32)],
        compiler_params=_params(),
    )(*pieces, w, x, g, dres)


def _ffn_bwd(dx, gu, w_down, l, *, name, tm=512):
    s, d = dx.shape
    f = w_down.shape[1]
    tm = _tile(s, tm)

    def body(dx_ref, g_ref, u_ref, w_ref, dgu_ref, act_ref, dxb_ref):
        dxb = _mx(dx_ref[...])
        dxb_ref[...] = dxb
        dact = _dot_nt(dxb, w_ref[...])
        g = _f32(g_ref)
        u = _f32(u_ref)
        sg = _sigmoid(g)
        silu = g * sg
        act_ref[...] = _mx(silu * u)
        dgu_ref[:, 0:f] = _mx(dact * u * (sg * (1.0 + g * (1.0 - sg))))
        dgu_ref[:, f:2 * f] = _mx(dact * silu)

    fblk = lambda col: pl.BlockSpec((tm, f), lambda i: (i, col))
    dblk = pl.BlockSpec((tm, d), lambda i: (i, 0))
    return pl.pallas_call(
        body,
        name=name,
        grid=(s // tm,),
        in_specs=[dblk, fblk(0), fblk(1), _resident((None, f, d), lambda i: (l, 0, 0))],
        out_specs=[pl.BlockSpec((tm, 2 * f), lambda i: (i, 0)), fblk(0), dblk],
        out_shape=[
            jax.ShapeDtypeStruct((s, 2 * f), MXU_DTYPE),
            jax.ShapeDtypeStruct((s, f), MXU_DTYPE),
            jax.ShapeDtypeStruct((s, d), MXU_DTYPE),
        ],
        compiler_params=_params(),
    )(dx, gu, gu, w_down)


def _merge_bwd(dx, ys, p, b_gate, w_bo, w_o, l, *, gate_col0, name, tm=512):
    s, d = dx.shape
    _, nbr, c, _ = w_bo.shape
    tm = _tile(s, tm)

    def body(dx_ref, ya_ref, yb_ref, yc_ref, g0_ref, g1_ref, g2_ref, bg_ref, wbo_ref, wo_ref,
             dgate_ref, dya_ref, dyb_ref, dyc_ref, dd0_ref, dd1_ref, dd2_ref, dxb_ref, dbg_ref):
        @pl.when(pl.program_id(0) == 0)
        def _():
            dbg_ref[...] = jnp.zeros_like(dbg_ref)

        dxb = _mx(dx_ref[...])
        dxb_ref[...] = dxb
        dmerged = _dot_nt(dxb, wo_ref[...])
        branches = ((ya_ref, g0_ref, dya_ref, dd0_ref), (yb_ref, g1_ref, dyb_ref, dd1_ref), (yc_ref, g2_ref, dyc_ref, dd2_ref))
        for i, (y_ref, g_ref, dy_ref, dd_ref) in enumerate(branches):
            cols = slice(i * d, (i + 1) * d)
            gate = _sigmoid(_f32(g_ref) + bg_ref[:, cols])
            yd = _dot(y_ref[...], wbo_ref[i])
            dyd = _mx(dmerged * gate)
            dd_ref[...] = dyd
            dy_ref[...] = _dot_nt(dyd, wbo_ref[i])
            dpre = dmerged * yd * gate * (1.0 - gate)
            dgate_ref[:, cols] = _mx(dpre)
            dbg_ref[:, cols] += _row_fold(dpre)

    yblk = pl.BlockSpec((tm, c), lambda i: (i, 0))
    gblk = lambda k: pl.BlockSpec((tm, d), lambda i: (i, gate_col0 + k))
    dblk = pl.BlockSpec((tm, d), lambda i: (i, 0))
    return pl.pallas_call(
        body,
        name=name,
        grid=(s // tm,),
        in_specs=[
            dblk, yblk, yblk, yblk, gblk(0), gblk(1), gblk(2),
            pl.BlockSpec((1, nbr * d), lambda i: (0, 0)),
            _resident((None, nbr, c, d), lambda i: (l, 0, 0, 0)),
            _resident((None, d, d), lambda i: (l, 0, 0)),
        ],
        out_specs=[pl.BlockSpec((tm, nbr * d), lambda i: (i, 0)), yblk, yblk, yblk, dblk, dblk, dblk, dblk,
                   pl.BlockSpec((8, nbr * d), lambda i: (0, 0))],
        out_shape=[jax.ShapeDtypeStruct((s, nbr * d), MXU_DTYPE)] + [jax.ShapeDtypeStruct((s, c), F32)] * 3
        + [jax.ShapeDtypeStruct((s, d), MXU_DTYPE)] * 4 + [jax.ShapeDtypeStruct((8, nbr * d), F32)],
        compiler_params=_params(),
    )(dx, *ys, p, p, p, b_gate, w_bo, w_o)


def _conv_bwd(p, dya, conv_w, conv_b, *, name, tm=512):
    s = p.shape[0]
    c = conv_w.shape[1]
    tm = _tile(s, tm)
    hb = tm // HALO
    last = s // tm - 1

    def body(bg_ref, cg_ref, xa_ref, cgp_ref, xap_ref, dy_ref, dyn_ref, bgn_ref, w_ref, b_ref,
             dp_ref, dw_ref):
        i = pl.program_id(0)

        @pl.when(i == 0)
        def _():
            dw_ref[...] = jnp.zeros_like(dw_ref)

        cg = _f32(cg_ref)
        xa = _f32(xa_ref)
        u = cg * xa
        prev = jnp.where(i == 0, 0.0, _f32(cgp_ref) * _f32(xap_ref))
        um1, um2 = _conv_taps(u, prev)
        w = w_ref[...]
        y = b_ref[...] + w[0:1, :] * um2 + w[1:2, :] * um1 + w[2:3, :] * u
        dya = dy_ref[...]
        dp_ref[:, 0:c] = _mx(dya * y)
        dyv = dya * _f32(bg_ref)
        nxt = jnp.where(i == last, 0.0, dyn_ref[...] * _f32(bgn_ref))
        row = lax.broadcasted_iota(jnp.int32, dyv.shape, 0)
        dp1 = jnp.where(row == tm - 1, nxt[0:1, :], pltpu.roll(dyv, tm - 1, 0))
        dp2 = pltpu.roll(dyv, tm - 2, 0)
        dp2 = jnp.where(row == tm - 2, nxt[0:1, :], jnp.where(row == tm - 1, nxt[1:2, :], dp2))
        du = w[2:3, :] * dyv + w[1:2, :] * dp1 + w[0:1, :] * dp2
        dp_ref[:, c:2 * c] = _mx(du * xa)
        dp_ref[:, 2 * c:3 * c] = _mx(du * cg)
        dw_ref[0] += _row_fold(dyv * um2)
        dw_ref[1] += _row_fold(dyv * um1)
        dw_ref[2] += _row_fold(dyv * u)
        dw_ref[3] += _row_fold(dyv)

    blk = lambda col: pl.BlockSpec((tm, c), lambda i: (i, col))
    halo = lambda col: pl.BlockSpec((HALO, c), lambda i: (jnp.maximum(i * hb - 1, 0), col))
    nhalo = lambda col: pl.BlockSpec((HALO, c), lambda i: (jnp.minimum((i + 1) * hb, s // HALO - 1), col))
    return pl.pallas_call(
        body,
        name=name,
        grid=(s // tm,),
        in_specs=[blk(0), blk(1), blk(2), halo(1), halo(2), blk(0), nhalo(0), nhalo(0),
                  pl.BlockSpec((CONV_K, c), lambda i: (0, 0)), pl.BlockSpec((1, c), lambda i: (0, 0))],
        out_specs=[pl.BlockSpec((tm, 3 * c), lambda i: (i, 0)), pl.BlockSpec((4, 8, c), lambda i: (0, 0, 0))],
        out_shape=[jax.ShapeDtypeStruct((s, 3 * c), MXU_DTYPE), jax.ShapeDtypeStruct((4, 8, c), F32)],
        compiler_params=_params(),
    )(p, p, p, p, p, dya, dya, p, conv_w, conv_b)


def _sgu_bwd(p, dyb, ln_g, ln_b, wtril, wtril_t, bias_full, *, col0, name, tm=512):
    s = p.shape[0]
    c = ln_g.shape[1]
    groups = c // CHUNK
    tm = _tile(s, tm)

    def body(u_ref, v_ref, dy_ref, g_ref, b_ref, w_ref, wt_ref, bias_ref, duv_ref, dln_ref, dw_ref, dbias_ref, dvn_ref):
        @pl.when(pl.program_id(0) == 0)
        def _():
            dln_ref[...] = jnp.zeros_like(dln_ref)
            dw_ref[...] = jnp.zeros_like(dw_ref)
            dbias_ref[...] = jnp.zeros_like(dbias_ref)

        sv = _f32(v_ref)
        xhat, r = _layernorm_stats(_gelu(sv))
        vb = _mx(xhat * g_ref[...] + b_ref[...])
        for n in range(tm // CHUNK):
            rows = slice(n * CHUNK, (n + 1) * CHUNK)
            for gi in range(groups):
                cols = slice(gi * CHUNK, (gi + 1) * CHUNK)
                su = u_ref[rows, cols].astype(F32)
                dy = dy_ref[rows, cols]
                vblk = vb[rows, cols]
                mixed = _dot(w_ref[gi], vblk) + bias_ref[:, cols]
                duv_ref[rows, cols] = _mx(dy * mixed * _gelu_grad(su))
                dmixed = dy * _gelu(su)
                dmb = _mx(dmixed)
                dvn_ref[rows, cols] = _dot(wt_ref[gi], dmb)
                dw_ref[gi] += _dot_nt(dmb, vblk)
                dbias_ref[:, cols] += dmixed
        dvn = dvn_ref[...]
        dln_ref[0] += _row_fold(dvn * xhat)
        dln_ref[1] += _row_fold(dvn)
        dxh = dvn * g_ref[...]
        dgv = r * (dxh - jnp.mean(dxh, axis=-1, keepdims=True) - xhat * jnp.mean(dxh * xhat, axis=-1, keepdims=True))
        duv_ref[:, c:2 * c] = _mx(dgv * _gelu_grad(sv))

    blk = lambda col: pl.BlockSpec((tm, c), lambda i: (i, col))
    vec = pl.BlockSpec((1, c), lambda i: (0, 0))
    wspec = pl.BlockSpec((groups, CHUNK, CHUNK), lambda i: (0, 0, 0))
    return pl.pallas_call(
        body,
        name=name,
        grid=(s // tm,),
        in_specs=[blk(col0), blk(col0 + 1), blk(0), vec, vec, wspec, wspec, pl.BlockSpec((CHUNK, c), lambda i: (0, 0))],
        out_specs=[pl.BlockSpec((tm, 2 * c), lambda i: (i, 0)), pl.BlockSpec((2, 8, c), lambda i: (0, 0, 0)), wspec,
                   pl.BlockSpec((CHUNK, c), lambda i: (0, 0))],
        out_shape=[jax.ShapeDtypeStruct((s, 2 * c), MXU_DTYPE),
                   jax.ShapeDtypeStruct((2, 8, c), F32), jax.ShapeDtypeStruct((groups, CHUNK, CHUNK), F32),
                   jax.ShapeDtypeStruct((CHUNK, c), F32)],
        scratch_shapes=[pltpu.VMEM((tm, c), F32)],
        compiler_params=_params(),
    )(p, p, dyb, ln_g, ln_b, wtril, wtril_t, bias_full)


def _att_bwd(kt_hm, vt_hm, qt_hm, dot_hm, runs, later, earlier, *, name, hg=4):
    h, nb, d, t = kt_hm.shape
    s = nb * t
    nbp = runs.shape[2]
    scale = 1.0 / math.sqrt(d)
    assert h % hg == 0

    def body(kt_ref, vt_ref, qt_ref, dot_ref, runs_ref, later_ref, earlier_ref, dqt_ref, dkt_ref, dvt_ref):
        i = pl.program_id(1)

        @pl.when(i == 0)
        def _():
            dkt_ref[...] = jnp.zeros_like(dkt_ref)
            dvt_ref[...] = jnp.zeros_like(dvt_ref)

        q_ts = [qt_ref[g] for g in range(hg)]
        do_ts = [dot_ref[g] for g in range(hg)]

        best = runs_ref[0]
        for g in range(1, hg):
            best = jnp.maximum(best, runs_ref[g])
        row = lax.broadcasted_iota(jnp.int32, (nbp, 1), 0)
        counts = jnp.logical_and(jnp.max(best, axis=1, keepdims=True) > EXP_UNDERFLOW, row < i)
        seen = jnp.sum(counts.astype(jnp.int32))
        zeros_row = tuple(jnp.zeros((1, t), F32) for _ in range(hg))

        def up(j, carry):
            gsums, dqts = carry
            mask = _att_mask(t, j * t, i * t)
            heads = range(hg)
            zs, es, lbs, l1ms = _att_blocks([kt_ref[g, j] for g in heads], q_ts, scale, mask)
            afters = _dot_split_each(later_ref[...], l1ms)
            das = [_dot_tn(vt_ref[g, j], do_ts[g]) for g in heads]
            weights = [jnp.where(mask, jnp.exp(lbs[g] + afters[g] + _row_get(runs_ref, g, j)), 0.0) for g in heads]
            grs = [das[g] * weights[g] for g in heads]
            gbefores = _dot_split_each(earlier_ref[...], grs)
            dzs = []
            for g in heads:
                inv = 1.0 / (1.0 + es[g])
                pos = zs[g] >= 0.0
                beta = jnp.where(pos, inv, es[g] * inv)
                omb = jnp.where(pos, es[g] * inv, inv)
                dzs.append(_mx(jnp.where(mask, grs[g] * omb - (gbefores[g] + gsums[g]) * beta, 0.0) * scale))
            new_dqts = [dqts[g] + _dot(kt_ref[g, j], dzs[g]) for g in heads]
            for g in heads:
                dkt_ref[g, j] += _dot_nt(q_ts[g], dzs[g])
            for g in heads:
                dvt_ref[g, j] += _dot_nt(do_ts[g], _mx(weights[g]))
            new_gsums = [gsums[g] + jnp.sum(grs[g], axis=0, keepdims=True) for g in heads]
            return tuple(new_gsums), tuple(new_dqts)

        _, dqts = lax.fori_loop(i - seen, i + 1, up, (zeros_row, tuple(jnp.zeros((d, t), F32) for _ in range(hg))))
        for g in range(hg):
            dqt_ref[g] = dqts[g]

    whole = pl.BlockSpec((hg, nb, d, t), lambda hh, i: (hh, 0, 0, 0))
    cols = pl.BlockSpec((hg, d, t), lambda hh, i: (hh, 0, i))
    tri = pl.BlockSpec((t, t), lambda hh, i: (0, 0))
    return pl.pallas_call(
        body,
        name=name,
        grid=(h // hg, nb),
        in_specs=[whole, whole, cols, cols, pl.BlockSpec((hg, None, nbp, t), lambda hh, i: (hh, i, 0, 0)), tri, tri],
        out_specs=[cols, whole, whole],
        out_shape=[jax.ShapeDtypeStruct((h, d, s), F32), jax.ShapeDtypeStruct((h, nb, d, t), F32),
                   jax.ShapeDtypeStruct((h, nb, d, t), F32)],
        compiler_params=_params(),
    )(kt_hm, vt_hm, qt_hm, dot_hm, runs, later, earlier)


def _qkv_bwd(p, dqn, dkn, dv, qg, kg, headavg, *, col0, name, tm=512):
    s = p.shape[0]
    c = qg.shape[1]
    tm = _tile(s, tm)

    def body(q_ref, k_ref, dqn_ref, dkn_ref, dv_ref, qg_ref, kg_ref, avg_ref, dqkv_ref, dg_ref):
        @pl.when(pl.program_id(0) == 0)
        def _():
            dg_ref[...] = jnp.zeros_like(dg_ref)

        for n, (src, dn_ref, g_ref) in enumerate(((q_ref, dqn_ref, qg_ref), (k_ref, dkn_ref, kg_ref))):
            xf = _f32(src)
            r = lax.rsqrt(_head_mean(xf * xf, avg_ref[...]) + EPS)
            y = xf * r
            dn = dn_ref[...]
            dy = dn * g_ref[...]
            dqkv_ref[:, n * c:(n + 1) * c] = _mx(r * (dy - y * _head_mean(dy * y, avg_ref[...])))
            dg_ref[n] += _row_fold(dn * y)
        dqkv_ref[:, 2 * c:3 * c] = _mx(dv_ref[...])

    blk = lambda col: pl.BlockSpec((tm, c), lambda i: (i, col))
    vec = pl.BlockSpec((1, c), lambda i: (0, 0))
    return pl.pallas_call(
        body,
        name=name,
        grid=(s // tm,),
        in_specs=[blk(col0), blk(col0 + 1), blk(0), blk(0), blk(0), vec, vec, pl.BlockSpec((c, c), lambda i: (0, 0))],
        out_specs=[pl.BlockSpec((tm, 3 * c), lambda i: (i, 0)), pl.BlockSpec((2, 8, c), lambda i: (0, 0, 0))],
        out_shape=[jax.ShapeDtypeStruct((s, 3 * c), MXU_DTYPE), jax.ShapeDtypeStruct((2, 8, c), F32)],
        compiler_params=_params(),
    )(p, p, dqn, dkn, dv, qg, kg, headavg)


def _place():
    x, y, c = lax.axis_index("x"), lax.axis_index("y"), lax.axis_index("c")
    chips = [(1 - x, y), (x, 1 - y), (1 - x, 1 - y)]
    return x, y, c, chips


def _shard_of(ref, axis, chip, width):
    idx = [slice(None)] * len(ref.shape)
    idx[axis] = pl.ds(chip * width, width)
    return ref.at[tuple(idx)]


def _place_shard(w, axis, chip_idx, dtype, *, name):
    layers = w.shape[0]
    rows, cols = _as_rows(w.shape[1:])
    if axis == len(w.shape) - 2:
        tr = _row_tile(rows, cols, 2)
        out_shape = (layers, rows, cols * N_CHIPS)
        out_spec = pl.BlockSpec((None, tr, cols), lambda l, i, j_ref: (l, i, j_ref[0]))
    else:
        assert axis == 0
        tr = _row_tile(rows, cols, 2)
        per = rows // tr
        out_shape = (layers, rows * N_CHIPS, cols)
        out_spec = pl.BlockSpec((None, tr, cols), lambda l, i, j_ref: (l, j_ref[0] * per + i, 0))
    full = list(w.shape)
    full[1 + axis] *= N_CHIPS

    def body(j_ref, w_ref, o_ref):
        o_ref[...] = w_ref[...].astype(dtype)

    out = pl.pallas_call(
        body,
        name=name,
        grid_spec=pltpu.PrefetchScalarGridSpec(
            num_scalar_prefetch=1,
            grid=(layers, rows // tr),
            in_specs=[pl.BlockSpec((None, tr, cols), lambda l, i, j_ref: (l, i, 0))],
            out_specs=out_spec,
        ),
        out_shape=jax.ShapeDtypeStruct(out_shape, dtype),
        compiler_params=_params(),
    )(chip_idx, w.reshape(layers, rows, cols))
    return out.reshape(full)


def _gather_weights(placed, axes, *, name):
    n = len(placed)
    widths = [pa.shape[1 + ax] // N_CHIPS for pa, ax in zip(placed, axes)]

    def body(*refs):
        outs = refs[n:2 * n]
        send_sem, recv_sem, fsend_sem, frecv_sem = refs[2 * n:]
        x, y, c, chips = _place()
        me = 2 * x + y
        sibling = (x, y, 1 - c)

        def slot(a, layer, chip):
            return _shard_of(outs[a].at[layer], axes[a], chip, widths[a])

        def send(a, k):
            return pltpu.make_async_remote_copy(
                src_ref=slot(a, c, me), dst_ref=slot(a, c, me), send_sem=send_sem.at[a, k], recv_sem=recv_sem.at[a, k],
                device_id=(*chips[k], c), device_id_type=MESH)

        def landed(a, k):
            cx, cy = chips[k]
            blk = slot(a, c, 2 * cx + cy)
            return pltpu.make_async_remote_copy(
                src_ref=blk, dst_ref=blk, send_sem=send_sem.at[a, k], recv_sem=recv_sem.at[a, k],
                device_id=(*chips[k], c), device_id_type=MESH)

        def forward(a, k, layer):
            cx, cy = chips[k]
            blk = slot(a, layer, 2 * cx + cy)
            return pltpu.make_async_remote_copy(
                src_ref=blk, dst_ref=blk, send_sem=fsend_sem.at[a, k], recv_sem=frecv_sem.at[a, k],
                device_id=sibling, device_id_type=MESH)

        for a in range(n):
            for k in range(3):
                send(a, k).start()
        for a in range(n):
            for k in range(3):
                landed(a, k).wait_recv()
                forward(a, k, c).start()
        for a in range(n):
            for k in range(3):
                forward(a, k, 1 - c).wait_recv()
        for a in range(n):
            for k in range(3):
                send(a, k).wait_send()
                forward(a, k, c).wait_send()

    return pl.pallas_call(
        body,
        name=name,
        in_specs=[ANY] * n,
        out_specs=[ANY] * n,
        out_shape=[jax.ShapeDtypeStruct(pa.shape, pa.dtype) for pa in placed],
        input_output_aliases={a: a for a in range(n)},
        scratch_shapes=[pltpu.SemaphoreType.DMA((n, 3))] * 4,
    )(*placed)


def _pair_exchange(grads, *, name):
    n = len(grads)

    def body(*refs):
        ins, outs = refs[:n], refs[n:2 * n]
        send_sem, recv_sem = refs[2 * n:]
        x, y, c, _ = _place()
        copies = [
            pltpu.make_async_remote_copy(
                src_ref=ins[a].at[1 - c], dst_ref=outs[a], send_sem=send_sem.at[a], recv_sem=recv_sem.at[a],
                device_id=(x, y, 1 - c), device_id_type=MESH)
            for a in range(n)
        ]
        for cp in copies:
            cp.start()
        for cp in copies:
            cp.wait()

    return pl.pallas_call(
        body,
        name=name,
        in_specs=[ANY] * n,
        out_specs=[ANY] * n,
        out_shape=[jax.ShapeDtypeStruct(g.shape[1:], g.dtype) for g in grads],
        scratch_shapes=[pltpu.SemaphoreType.DMA((n,)), pltpu.SemaphoreType.DMA((n,))],
    )(*grads)


def _chip_scatter(sums, axes, *, name):
    n = len(sums)
    widths = [sm.shape[ax] // N_CHIPS for sm, ax in zip(sums, axes)]
    out_shapes = []
    for sm, ax, w in zip(sums, axes, widths):
        shp = list(sm.shape)
        shp[ax] = w
        out_shapes.append(jax.ShapeDtypeStruct((3, *shp), sm.dtype))

    def body(*refs):
        ins, outs = refs[:n], refs[n:2 * n]
        send_sem, recv_sem = refs[2 * n:]
        x, y, c, chips = _place()
        copies = []
        for a in range(n):
            for k in range(3):
                cx, cy = chips[k]
                copies.append(pltpu.make_async_remote_copy(
                    src_ref=_shard_of(ins[a], axes[a], 2 * cx + cy, widths[a]), dst_ref=outs[a].at[k],
                    send_sem=send_sem.at[a, k], recv_sem=recv_sem.at[a, k], device_id=(cx, cy, c), device_id_type=MESH))
        for cp in copies:
            cp.start()
        for cp in copies:
            cp.wait()

    return pl.pallas_call(
        body,
        name=name,
        in_specs=[ANY] * n,
        out_specs=[ANY] * n,
        out_shape=out_shapes,
        scratch_shapes=[pltpu.SemaphoreType.DMA((n, 3)), pltpu.SemaphoreType.DMA((n, 3))],
    )(*sums)


def _pair_share(finals, *, name):
    n = len(finals)

    def body(*refs):
        outs = refs[n:2 * n]
        send_sem, recv_sem = refs[2 * n:]
        x, y, c, _ = _place()
        for a in range(n):
            pltpu.make_async_remote_copy(
                src_ref=outs[a].at[c], dst_ref=outs[a].at[c], send_sem=send_sem.at[a], recv_sem=recv_sem.at[a],
                device_id=(x, y, 1 - c), device_id_type=MESH).start()
        for a in range(n):
            pltpu.make_async_remote_copy(
                src_ref=outs[a].at[c], dst_ref=outs[a].at[1 - c], send_sem=send_sem.at[a], recv_sem=recv_sem.at[a],
                device_id=(x, y, 1 - c), device_id_type=MESH).wait()

    return pl.pallas_call(
        body,
        name=name,
        in_specs=[ANY] * n,
        out_specs=[ANY] * n,
        out_shape=[jax.ShapeDtypeStruct(f.shape, f.dtype) for f in finals],
        input_output_aliases={a: a for a in range(n)},
        scratch_shapes=[pltpu.SemaphoreType.DMA((n,))] * 2,
    )(*finals)


def _small_view(shape):
    size = math.prod(shape)
    return (size // 128, 128) if size % 1024 == 0 else (shape[0], size // shape[0])


def _all_reduce_small(parts, *, name):
    n = len(parts)
    views = [_small_view(a.shape) for a in parts]

    def body(*refs):
        ins, outs, slots = refs[:n], refs[n:2 * n], refs[2 * n:3 * n]
        send_sem, recv_sem = refs[3 * n:]
        x, y, c, _ = _place()
        me = 4 * x + 2 * y + c
        copies = []
        for a in range(n):
            slots[a][me] = ins[a][...]
            for k in range(1, N_DEV):
                peer = (x ^ (k >> 2), y ^ ((k >> 1) & 1), c ^ (k & 1))
                copies.append(pltpu.make_async_remote_copy(
                    src_ref=ins[a], dst_ref=slots[a].at[me], send_sem=send_sem.at[a, k - 1],
                    recv_sem=recv_sem.at[a, k - 1], device_id=peer, device_id_type=MESH))
        for cp in copies:
            cp.start()
        for a in range(n):
            for k in range(1, N_DEV):
                pltpu.make_async_remote_copy(
                    src_ref=ins[a], dst_ref=slots[a].at[me ^ k], send_sem=send_sem.at[a, k - 1],
                    recv_sem=recv_sem.at[a, k - 1], device_id=(x, y, c), device_id_type=MESH).wait()
        for a in range(n):
            total = slots[a][0]
            for dev in range(1, N_DEV):
                total = total + slots[a][dev]
            outs[a][...] = total

    vmem = pl.BlockSpec(memory_space=pltpu.VMEM)
    outs = pl.pallas_call(
        body,
        name=name,
        in_specs=[vmem] * n,
        out_specs=[vmem] * n,
        out_shape=[jax.ShapeDtypeStruct(view, F32) for view in views],
        scratch_shapes=[pltpu.VMEM((N_DEV, *view), F32) for view in views]
        + [pltpu.SemaphoreType.DMA((n, N_DEV - 1)), pltpu.SemaphoreType.DMA((n, N_DEV - 1))],
        compiler_params=_params(),
    )(*[a.reshape(view) for a, view in zip(parts, views)])
    return [o.reshape(a.shape) for o, a in zip(outs, parts)]


def _as_rows(shape):
    cols = shape[-1]
    return math.prod(shape[:-1]), cols


ELEMENTWISE_VMEM = 24 * 1024 * 1024


def _row_tile(rows, cols, n_arrays, sublanes=8):
    cap = ELEMENTWISE_VMEM // (n_arrays * 2 * 4 * cols)
    best = None
    for t in range(sublanes, min(rows, cap) + 1, sublanes):
        if rows % t == 0:
            best = t
    assert best is not None, (rows, cols)
    return best


def _pair_sum(g, other, c_idx, *, name):
    rows, cols = _as_rows(other.shape)
    tr = _row_tile(rows, cols, 4, sublanes=16)
    g2 = g.reshape(2, rows, cols)

    def body(c_ref, g_ref, o_ref, out_ref, wire_ref):
        total = g_ref[...] + o_ref[...]
        out_ref[...] = total
        wire_ref[...] = total.astype(WIRE_DTYPE)

    blk = pl.BlockSpec((tr, cols), lambda i, c_ref: (i, 0))
    out, wire = pl.pallas_call(
        body,
        name=name,
        grid_spec=pltpu.PrefetchScalarGridSpec(
            num_scalar_prefetch=1,
            grid=(rows // tr,),
            in_specs=[pl.BlockSpec((None, tr, cols), lambda i, c_ref: (c_ref[0], i, 0)), blk],
            out_specs=[blk, blk],
        ),
        out_shape=[jax.ShapeDtypeStruct((rows, cols), F32), jax.ShapeDtypeStruct((rows, cols), WIRE_DTYPE)],
        compiler_params=_params(),
    )(c_idx, g2, other.reshape(rows, cols))
    return out.reshape(other.shape), wire.reshape(other.shape)


def _chip_sum(mine, got, axis, chip_idx, c_idx, *, name):
    shard_shape = got.shape[1:]
    rows, cols = _as_rows(shard_shape)
    tr = _row_tile(rows, cols, 5, sublanes=16)
    if axis == len(mine.shape) - 1:
        m2 = mine.reshape(rows, cols * N_CHIPS)
        mine_spec = pl.BlockSpec((tr, cols), lambda i, j_ref, c_ref: (i, j_ref[0]))
    else:
        assert axis == 0
        m2 = mine.reshape(N_CHIPS, rows, cols)
        mine_spec = pl.BlockSpec((None, tr, cols), lambda i, j_ref, c_ref: (j_ref[0], i, 0))

    def body(j_ref, c_ref, m_ref, got_ref, out_ref):
        out_ref[...] = ((m_ref[...] + got_ref[0].astype(F32)) + got_ref[1].astype(F32)) + got_ref[2].astype(F32)

    out = pl.pallas_call(
        body,
        name=name,
        grid_spec=pltpu.PrefetchScalarGridSpec(
            num_scalar_prefetch=2,
            grid=(rows // tr,),
            in_specs=[mine_spec, pl.BlockSpec((3, tr, cols), lambda i, j_ref, c_ref: (0, i, 0))],
            out_specs=pl.BlockSpec((None, tr, cols), lambda i, j_ref, c_ref: (c_ref[0], i, 0)),
        ),
        out_shape=jax.ShapeDtypeStruct((2, rows, cols), F32),
        compiler_params=_params(),
    )(chip_idx, c_idx, m2, got.reshape(3, rows, cols))
    return out.reshape((2, *shard_shape))


def _adamw_update(w_ref, g_ref, m_ref, v_ref, d_ref, nm_ref, nv_ref):
    c1 = 1.0 / (1.0 - ADAM_B1 ** ADAM_STEP)
    c2 = 1.0 / (1.0 - ADAM_B2 ** ADAM_STEP)
    gg = g_ref[...]
    nm = ADAM_B1 * m_ref[...] + (1.0 - ADAM_B1) * gg
    nv = ADAM_B2 * v_ref[...] + (1.0 - ADAM_B2) * (gg * gg)
    nm_ref[...] = nm
    nv_ref[...] = nv
    d_ref[...] = -ADAM_LR * ((nm * c1) / (jnp.sqrt(nv * c2) + ADAM_EPS) + ADAM_WD * w_ref[...])


def _adamw_small(ws, gs, ms, vs, *, name):
    n = len(ws)
    views = [_small_view(a.shape) for a in ws]

    def body(*refs):
        w_r, g_r, m_r, v_r = (refs[k * n:(k + 1) * n] for k in range(4))
        d_r, nm_r, nv_r = (refs[(4 + k) * n:(5 + k) * n] for k in range(3))
        for a in range(n):
            _adamw_update(w_r[a], g_r[a], m_r[a], v_r[a], d_r[a], nm_r[a], nv_r[a])

    vmem = pl.BlockSpec(memory_space=pltpu.VMEM)
    flat = lambda arrs: [a.reshape(view) for a, view in zip(arrs, views)]
    outs = pl.pallas_call(
        body,
        name=name,
        in_specs=[vmem] * (4 * n),
        out_specs=[vmem] * (3 * n),
        out_shape=[jax.ShapeDtypeStruct(view, F32) for view in views] * 3,
        compiler_params=_params(),
    )(*flat(ws), *flat(gs), *flat(ms), *flat(vs))
    return [tuple(outs[k * n + a].reshape(ws[a].shape) for k in range(3)) for a in range(n)]


def _adamw(w, g, m, v, *, name):
    shape = w.shape
    rows, cols = _as_rows(shape)
    tr = _row_tile(rows, cols, 7)
    body = functools.partial(_adamw_update)

    blk = pl.BlockSpec((tr, cols), lambda i: (i, 0))
    flat = lambda a: a.reshape(rows, cols)
    outs = pl.pallas_call(
        body,
        name=name,
        grid=(rows // tr,),
        in_specs=[blk] * 4,
        out_specs=[blk] * 3,
        out_shape=[jax.ShapeDtypeStruct((rows, cols), F32)] * 3,
        compiler_params=_params(),
    )(flat(w), flat(g), flat(m), flat(v))
    return tuple(o.reshape(shape) for o in outs)


def _cols_blocked(a, t):
    return a.reshape(a.shape[0] // t, t, HEADS, HEAD_DIM).transpose(2, 0, 3, 1)


def _cols(a):
    return a.reshape(a.shape[0], HEADS, HEAD_DIM).transpose(1, 2, 0)


def _from_cols(a):
    h, d, s = a.shape
    return a.transpose(2, 0, 1).reshape(s, h * d)


def _from_cols_blocked(a):
    h, nb, d, t = a.shape
    return a.transpose(1, 3, 0, 2).reshape(nb * t, h * d)


BIG = ("w_in", "w_branch_out", "w_o", "w_gate_up", "w_down")
BIG_AXIS = {"w_in": 1, "w_branch_out": 2, "w_o": 0, "w_gate_up": 1, "w_down": 0}
SMALL = ("mix_norm_g", "b_gate", "conv_w", "conv_b", "sgu_ln_g", "sgu_ln_b", "sgu_w", "sgu_b", "q_norm_g", "k_norm_g",
         "ffn_norm_g")
ORDER = ("mix_norm_g", "w_in", "b_gate", "conv_w", "conv_b", "sgu_ln_g", "sgu_ln_b", "sgu_w", "sgu_b", "q_norm_g",
         "k_norm_g", "w_branch_out", "w_o", "ffn_norm_g", "w_gate_up", "w_down")


def _layer_forward(x, w, l):
    t = ATT_BLOCK
    c = w["conv_b"].shape[1]
    n_in = w["w_in"].shape[2]
    gate_col0 = (n_in - 3 * x.shape[1]) // x.shape[1]
    tag = f"l{l}"
    p, h = _norm_matmul(x, w["mix_norm_g"][l][None], w["w_in"], l, name=f"in_proj_{tag}", tn=n_in // 7)
    ya = _conv_fwd(p, w["conv_w"][l], w["conv_b"][l][None], name=f"conv_{tag}")
    yb = _sgu_fwd(p, w["sgu_ln_g"][l][None], w["sgu_ln_b"][l][None], w["wtril"][l], w["bias_full"][l], col0=3,
                  name=f"sgu_{tag}")
    qn, kn, vb = _qkv_prep(p, w["qg"][l], w["kg"][l], w["headavg"], col0=5, name=f"qkv_{tag}")
    q_c, k_cb, v_cb = _cols(qn), _cols_blocked(kn, t), _cols_blocked(vb, t)
    out_t, runs = _att_fwd(k_cb, q_c, v_cb, w["later"], name=f"att_{tag}")
    yc = _from_cols(out_t)
    x1, merged = _merge_fwd((ya, yb, yc), p, w["b_gate"][l][None], w["w_branch_out"], w["w_o"], l, x,
                            gate_col0=gate_col0, name=f"merge_{tag}")
    gu, h2 = _norm_matmul(x1, w["ffn_norm_g"][l][None], w["w_gate_up"], l, name=f"gate_up_{tag}",
                          tn=w["w_gate_up"].shape[2] // 4)
    x2 = _ffn_down(gu, w["w_down"], l, x1, name=f"down_{tag}")
    saved = dict(x=x, p=p, h=h, ya=ya, yb=yb, yc=yc, merged=merged, x1=x1, gu=gu, h2=h2,
                 att=(k_cb, v_cb, q_c, runs), gate_col0=gate_col0)
    return x2, saved


def _layer_backward(dx2, w, sv, l, big):
    layers = w["w_in"].shape[0]
    t = ATT_BLOCK
    tag = f"l{l}"
    d = dx2.shape[1]
    c = w["conv_b"].shape[1]
    n_in = w["w_in"].shape[2]
    n_ff = w["w_gate_up"].shape[2]
    g = {}
    dgu, act, dx2b = _ffn_bwd(dx2, sv["gu"], w["w_down"], l, name=f"down_bwd_{tag}")
    big["w_down"] = _matmul_tn(act, dx2b, l, layers, big.get("w_down"), name=f"dw_down_{tag}", t1=act.shape[1] // 2)
    dx1, dg2 = _matmul_nt_normbwd([dgu], w["w_gate_up"], l, sv["x1"], w["ffn_norm_g"][l][None], dx2,
                                  name=f"gate_up_bwd_{tag}")
    g["ffn_norm_g"] = jnp.sum(dg2, axis=0)
    big["w_gate_up"] = _matmul_tn(sv["h2"], dgu, l, layers, big.get("w_gate_up"), name=f"dw_gate_up_{tag}", tn=n_ff // 4)
    ys = (sv["ya"], sv["yb"], sv["yc"])
    (dgates, dya, dyb, dyc, dd0, dd1, dd2, dx1b, dbg) = _merge_bwd(
        dx1, ys, sv["p"], w["b_gate"][l][None], w["w_branch_out"], w["w_o"], l, gate_col0=sv["gate_col0"],
        name=f"merge_bwd_{tag}")
    g["b_gate"] = jnp.sum(dbg, axis=0)
    big["w_o"] = _matmul_tn(sv["merged"], dx1b, l, layers, big.get("w_o"), name=f"dw_o_{tag}")
    for i, (y, dd) in enumerate(zip(ys, (dd0, dd1, dd2))):
        big["w_branch_out"] = _matmul_tn(y, dd, len(ys) * l + i, len(ys) * layers, big.get("w_branch_out"),
                                         name=f"dw_bo{i}_{tag}")
    dconv, dwc = _conv_bwd(sv["p"], dya, w["conv_w"][l], w["conv_b"][l][None], name=f"conv_bwd_{tag}")
    dwc = jnp.sum(dwc, axis=1)
    g["conv_w"] = dwc[0:CONV_K]
    g["conv_b"] = dwc[CONV_K]
    dsgu, dln, dws, dbias = _sgu_bwd(sv["p"], dyb, w["sgu_ln_g"][l][None], w["sgu_ln_b"][l][None], w["wtril"][l],
                                     w["wtril_t"][l], w["bias_full"][l], col0=3, name=f"sgu_bwd_{tag}")
    dln = jnp.sum(dln, axis=1)
    g["sgu_ln_g"], g["sgu_ln_b"] = dln[0], dln[1]
    g["sgu_w"] = jnp.where(w["tril"], dws, 0.0)
    g["sgu_b"] = jnp.sum(dbias.reshape(CHUNK, c // CHUNK, CHUNK), axis=2).T
    k_cb, v_cb, q_c, runs = sv["att"]
    dqt, dkt, dvt = _att_bwd(k_cb, v_cb, q_c, _cols(_mx(dyc)), runs, w["later"], w["earlier"], name=f"att_bwd_{tag}")
    dqn, dkn, dvv = _from_cols(dqt), _from_cols_blocked(dkt), _from_cols_blocked(dvt)
    dqkv, dqkg = _qkv_bwd(sv["p"], dqn, dkn, dvv, w["qg"][l], w["kg"][l], w["headavg"], col0=5, name=f"qkv_bwd_{tag}")
    dqkg = jnp.sum(dqkg.reshape(2, 8 * HEADS, HEAD_DIM), axis=1)
    g["q_norm_g"], g["k_norm_g"] = dqkg[0], dqkg[1]
    pieces = [dconv, dsgu, dqkv, dgates]
    dx0, dg1 = _matmul_nt_normbwd(pieces, w["w_in"], l, sv["x"], w["mix_norm_g"][l][None], dx1, name=f"in_proj_bwd_{tag}")
    g["mix_norm_g"] = jnp.sum(dg1, axis=0)
    col0 = 0
    for k, pc in enumerate(pieces):
        width = pc.shape[1]
        big["w_in"] = _matmul_tn(sv["h"], pc, l, layers, big.get("w_in"), name=f"dw_in{k}_{tag}", col0=col0, n_total=n_in,
                                 tn=math.gcd(col0, width) if col0 else width)
        col0 += width
    return dx0, g


def kernel(x, mix_norm_g, w_in, b_gate, conv_w, conv_b, sgu_ln_g, sgu_ln_b, sgu_w, sgu_b, q_norm_g, k_norm_g, w_branch_out, w_o, ffn_norm_g, w_gate_up, w_down, loss_target, m_mix_norm_g, m_w_in, m_b_gate, m_conv_w, m_conv_b, m_sgu_ln_g, m_sgu_ln_b, m_sgu_w, m_sgu_b, m_q_norm_g, m_k_norm_g, m_w_branch_out, m_w_o, m_ffn_norm_g, m_w_gate_up, m_w_down, v_mix_norm_g, v_w_in, v_b_gate, v_conv_w, v_conv_b, v_sgu_ln_g, v_sgu_ln_b, v_sgu_w, v_sgu_b, v_q_norm_g, v_k_norm_g, v_w_branch_out, v_w_o, v_ffn_norm_g, v_w_gate_up, v_w_down):
    given = dict(locals())
    params = {n: given[n] for n in ORDER}
    moms = {n: (given["m_" + n], given["v_" + n]) for n in ORDER}
    layers = mix_norm_g.shape[0]
    assert layers == 2, "the exchanges split the work of a chip's two cores by layer"
    xs = x[0]
    target = loss_target[0]
    chip = 2 * lax.axis_index("x") + lax.axis_index("y")
    core = lax.axis_index("c")

    c_idx = core.reshape(1).astype(jnp.int32)
    j_idx = chip.reshape(1).astype(jnp.int32)
    width = conv_w.shape[2]
    placed = [_place_shard(params[n], BIG_AXIS[n], j_idx, MXU_DTYPE, name=f"place_{n}") for n in BIG]
    placed.append(lax.dynamic_update_slice_in_dim(jnp.zeros((layers, CONV_K, width * N_CHIPS), F32), conv_w, chip * width, 2))
    gathered = _gather_weights(placed, [BIG_AXIS[n] for n in BIG] + [1], name="gather_weights")
    w = dict(zip(BIG + ("conv_w",), gathered))
    for n in ("mix_norm_g", "b_gate", "conv_b", "sgu_ln_g", "sgu_ln_b", "ffn_norm_g"):
        w[n] = params[n]
    groups = sgu_w.shape[1]
    tril = jnp.tril(jnp.ones((CHUNK, CHUNK), dtype=bool))
    w["tril"] = tril
    w["wtril"] = _mx(jnp.where(tril, sgu_w, 0.0))
    w["wtril_t"] = w["wtril"].transpose(0, 1, 3, 2)
    w["bias_full"] = jnp.repeat(sgu_b.transpose(0, 2, 1), CHUNK, axis=2)
    w["qg"] = jnp.tile(q_norm_g, (1, HEADS))[:, None, :]
    w["kg"] = jnp.tile(k_norm_g, (1, HEADS))[:, None, :]
    lane = jnp.arange(HEADS * HEAD_DIM) // HEAD_DIM
    w["headavg"] = _mx(jnp.where(lane[:, None] == lane[None, :], 1.0 / HEAD_DIM, 0.0))
    pos = jnp.arange(ATT_BLOCK)
    w["later"] = _mx(jnp.where(pos[None, :] > pos[:, None], 1.0, 0.0))
    w["earlier"] = _mx(jnp.where(pos[None, :] < pos[:, None], 1.0, 0.0))

    saved = []
    act = xs
    for l in range(layers):
        act, sv = _layer_forward(act, w, l)
        saved.append(sv)
    loss_part, dact = _loss_head(act, target, name="loss_head")
    loss = lax.psum(jnp.sum(loss_part), ("x", "y", "c"))
    grads = [None] * layers
    big = {}
    for l in reversed(range(layers)):
        dact, grads[l] = _layer_backward(dact, w, saved[l], l, big)
    grad_x = dact[None]
    local = {n: jnp.stack([grads[l][n] for l in range(layers)]) for n in SMALL}
    for n in BIG:
        local[n] = big[n].reshape(w[n].shape)

    got = _pair_exchange([local[n] for n in BIG], name="grads_pair_exchange")
    pair = [_pair_sum(local[n], o, c_idx, name=f"pair_sum_{n}") for n, o in zip(BIG, got)]
    landed = _chip_scatter([wire for _, wire in pair], [BIG_AXIS[n] for n in BIG], name="grads_chip_scatter")
    final = [_chip_sum(pr, ld, BIG_AXIS[n], j_idx, c_idx, name=f"chip_sum_{n}") for n, (pr, _), ld in zip(BIG, pair, landed)]
    full = dict(zip(BIG, _pair_share(final, name="grads_pair_share")))

    summed = _all_reduce_small([local[n] for n in SMALL], name="grads_all_reduce_small")
    for n, gsum in zip(SMALL, summed):
        full[n] = gsum
    full["conv_w"] = lax.dynamic_slice_in_dim(full["conv_w"], chip * width, width, axis=2)

    out = {}
    for n in BIG:
        out[n] = _adamw(params[n], full[n], *moms[n], name=f"adamw_{n}")
    small = _adamw_small([params[n] for n in SMALL], [full[n] for n in SMALL], [moms[n][0] for n in SMALL],
                         [moms[n][1] for n in SMALL], name="adamw_small")
    for n, triple in zip(SMALL, small):
        out[n] = triple
    return (loss, grad_x, *[full[n] for n in ORDER], *[out[n][0] for n in ORDER], *[out[n][1] for n in ORDER],
            *[out[n][2] for n in ORDER])
```

```python
import functools
import math

import jax
import jax.numpy as jnp
from jax import lax
from jax.experimental import pallas as pl
from jax.experimental.pallas import tpu as pltpu

F32 = jnp.float32
MXU_DTYPE = jnp.bfloat16
WIRE_DTYPE = jnp.bfloat16
ACT_DTYPE = jnp.bfloat16
HALO = 16

EPS = 1e-6
CONV_K = 3
CHUNK = 128
HEADS = 8
HEAD_DIM = 64
ATT_BLOCK = 128
EXP_UNDERFLOW = -88.0

ADAM_LR = 0.001
ADAM_B1 = 0.9
ADAM_B2 = 0.999
ADAM_EPS = 1e-08
ADAM_WD = 0.01
ADAM_STEP = 10

VMEM_LIMIT = 56 * 1024 * 1024
MESH = pl.DeviceIdType.MESH
N_CHIPS = 4
N_DEV = 8
ANY = pl.BlockSpec(memory_space=pl.ANY)


def _params(**kw):
    return pltpu.CompilerParams(vmem_limit_bytes=VMEM_LIMIT, **kw)


def _mx(v):
    return v.astype(MXU_DTYPE)


def _dot(a, b):
    return lax.dot_general(a, b, (((1,), (0,)), ((), ())), preferred_element_type=F32)


def _dot_nt(a, b):
    return lax.dot_general(a, b, (((1,), (1,)), ((), ())), preferred_element_type=F32)


def _dot_tn(a, b):
    return lax.dot_general(a, b, (((0,), (0,)), ((), ())), preferred_element_type=F32)


def _dot_split(const, v):
    hi = _mx(v)
    lo = _mx(v - hi.astype(F32))
    return _dot(const, hi) + _dot(const, lo)


def _dot_split_r(v, const):
    hi = _mx(v)
    lo = _mx(v - hi.astype(F32))
    return _dot(hi, const) + _dot(lo, const)


def _sigmoid(x):
    return 1.0 / (1.0 + jnp.exp(-x))


_INV_SQRT2 = 1.0 / math.sqrt(2.0)
_INV_SQRT2PI = 1.0 / math.sqrt(2.0 * math.pi)


def _gelu(x):
    return 0.5 * x * (1.0 + lax.erf(x * _INV_SQRT2))


def _gelu_grad(x):
    return 0.5 * (1.0 + lax.erf(x * _INV_SQRT2)) + x * jnp.exp(-0.5 * x * x) * _INV_SQRT2PI


def _row_fold(v):
    m, n = v.shape
    return jnp.sum(v.reshape(m // 8, 8, n), axis=0)


def _tile(m, pref):
    t = min(m, pref)
    while m % t:
        t //= 2
    return t


def _resident(block_shape, index_map):
    return pl.BlockSpec(block_shape, index_map, pipeline_mode=pl.Buffered(1))


def _norm_matmul(x, g, w, l, *, name, tm=512, tn=None):
    s, d = x.shape
    n = w.shape[2]
    tm = _tile(s, tm)
    tn = tn or n
    assert n % tn == 0

    def body(x_ref, g_ref, w_ref, p_ref, h_ref):
        xf = x_ref[...]
        r = lax.rsqrt(jnp.mean(xf * xf, axis=-1, keepdims=True) + EPS)
        h = _mx(xf * r * g_ref[...])
        h_ref[...] = h
        for j in range(n // tn):
            cols = slice(j * tn, (j + 1) * tn)
            p_ref[:, cols] = _dot(h, w_ref[:, cols]).astype(ACT_DTYPE)

    return pl.pallas_call(
        body,
        name=name,
        grid=(s // tm,),
        in_specs=[
            pl.BlockSpec((tm, d), lambda i: (i, 0)),
            pl.BlockSpec((1, d), lambda i: (0, 0)),
            _resident((None, d, n), lambda i: (l, 0, 0)),
        ],
        out_specs=[
            pl.BlockSpec((tm, n), lambda i: (i, 0)),
            pl.BlockSpec((tm, d), lambda i: (i, 0)),
        ],
        out_shape=[jax.ShapeDtypeStruct((s, n), ACT_DTYPE), jax.ShapeDtypeStruct((s, d), MXU_DTYPE)],
        compiler_params=_params(),
    )(x, g, w)


def _conv_taps(u, prev):
    row = lax.broadcasted_iota(jnp.int32, u.shape, 0)
    last, before = prev[HALO - 1:HALO, :], prev[HALO - 2:HALO - 1, :]
    um1 = jnp.where(row == 0, last, pltpu.roll(u, 1, 0))
    um2 = pltpu.roll(u, 2, 0)
    um2 = jnp.where(row == 0, before, jnp.where(row == 1, last, um2))
    return um1, um2


def _f32(ref):
    return ref[...].astype(F32)


def _conv_fwd(p, conv_w, conv_b, *, name, tm=512):
    s = p.shape[0]
    c = conv_w.shape[1]
    tm = _tile(s, tm)
    hb = tm // HALO

    def body(bg_ref, cg_ref, xa_ref, cgp_ref, xap_ref, w_ref, b_ref, y_ref):
        first = pl.program_id(0) == 0
        u = _f32(cg_ref) * _f32(xa_ref)
        prev = jnp.where(first, 0.0, _f32(cgp_ref) * _f32(xap_ref))
        um1, um2 = _conv_taps(u, prev)
        w = w_ref[...]
        y = b_ref[...] + w[0:1, :] * um2 + w[1:2, :] * um1 + w[2:3, :] * u
        y_ref[...] = _mx(_f32(bg_ref) * y)

    halo = lambda col: pl.BlockSpec((HALO, c), lambda i: (jnp.maximum(i * hb - 1, 0), col))
    return pl.pallas_call(
        body,
        name=name,
        grid=(s // tm,),
        in_specs=[
            pl.BlockSpec((tm, c), lambda i: (i, 0)),
            pl.BlockSpec((tm, c), lambda i: (i, 1)),
            pl.BlockSpec((tm, c), lambda i: (i, 2)),
            halo(1),
            halo(2),
            pl.BlockSpec((CONV_K, c), lambda i: (0, 0)),
            pl.BlockSpec((1, c), lambda i: (0, 0)),
        ],
        out_specs=pl.BlockSpec((tm, c), lambda i: (i, 0)),
        out_shape=jax.ShapeDtypeStruct((s, c), MXU_DTYPE),
        compiler_params=_params(),
    )(p, p, p, p, p, conv_w, conv_b)


def _layernorm_stats(x):
    mu = jnp.mean(x, axis=-1, keepdims=True)
    xc = x - mu
    r = lax.rsqrt(jnp.mean(xc * xc, axis=-1, keepdims=True) + EPS)
    return xc * r, r


def _sgu_fwd(p, ln_g, ln_b, wtril, bias_full, *, col0, name, tm=512):
    s = p.shape[0]
    c = ln_g.shape[1]
    groups = c // CHUNK
    tm = _tile(s, tm)

    def body(u_ref, v_ref, g_ref, b_ref, w_ref, bias_ref, y_ref):
        vn, _ = _layernorm_stats(_gelu(_f32(v_ref)))
        vb = _mx(vn * g_ref[...] + b_ref[...])
        for n in range(tm // CHUNK):
            rows = slice(n * CHUNK, (n + 1) * CHUNK)
            for gi in range(groups):
                cols = slice(gi * CHUNK, (gi + 1) * CHUNK)
                mixed = _dot(w_ref[gi], vb[rows, cols]) + bias_ref[:, cols]
                y_ref[rows, cols] = _mx(_gelu(u_ref[rows, cols].astype(F32)) * mixed)

    return pl.pallas_call(
        body,
        name=name,
        grid=(s // tm,),
        in_specs=[
            pl.BlockSpec((tm, c), lambda i: (i, col0)),
            pl.BlockSpec((tm, c), lambda i: (i, col0 + 1)),
            pl.BlockSpec((1, c), lambda i: (0, 0)),
            pl.BlockSpec((1, c), lambda i: (0, 0)),
            pl.BlockSpec((groups, CHUNK, CHUNK), lambda i: (0, 0, 0)),
            pl.BlockSpec((CHUNK, c), lambda i: (0, 0)),
        ],
        out_specs=pl.BlockSpec((tm, c), lambda i: (i, 0)),
        out_shape=jax.ShapeDtypeStruct((s, c), MXU_DTYPE),
        compiler_params=_params(),
    )(p, p, ln_g, ln_b, wtril, bias_full)


def _head_mean(v, headavg):
    return _dot_split_r(v, headavg)


def _qkv_prep(p, qg, kg, headavg, *, col0, name, tm=512):
    s = p.shape[0]
    c = qg.shape[1]
    tm = _tile(s, tm)

    def body(q_ref, k_ref, v_ref, qg_ref, kg_ref, avg_ref, qn_ref, kn_ref, vb_ref):
        for src, g_ref, dst in ((q_ref, qg_ref, qn_ref), (k_ref, kg_ref, kn_ref)):
            xf = _f32(src)
            r = lax.rsqrt(_head_mean(xf * xf, avg_ref[...]) + EPS)
            dst[...] = _mx(xf * r * g_ref[...])
        vb_ref[...] = _mx(v_ref[...])

    blk = lambda col: pl.BlockSpec((tm, c), lambda i: (i, col))
    vec = pl.BlockSpec((1, c), lambda i: (0, 0))
    out = pl.BlockSpec((tm, c), lambda i: (i, 0))
    return pl.pallas_call(
        body,
        name=name,
        grid=(s // tm,),
        in_specs=[blk(col0), blk(col0 + 1), blk(col0 + 2), vec, vec, pl.BlockSpec((c, c), lambda i: (0, 0))],
        out_specs=[out, out, out],
        out_shape=[jax.ShapeDtypeStruct((s, c), MXU_DTYPE)] * 3,
        compiler_params=_params(),
    )(p, p, p, qg, kg, headavg)


def _att_mask(t, key0, qry0):
    kpos = key0 + lax.broadcasted_iota(jnp.int32, (t, t), 0)
    qpos = qry0 + lax.broadcasted_iota(jnp.int32, (t, t), 1)
    return kpos < qpos


def _att_blocks(k_blks, q_ts, scale, mask):
    zs = [(_dot_tn(k, q) if k.shape[0] == q.shape[0] else _dot(k, q)) * scale for k, q in zip(k_blks, q_ts)]
    es = [jnp.exp(-jnp.abs(z)) for z in zs]
    lbs = [jnp.minimum(z, 0.0) - jnp.log1p(e) for z, e in zip(zs, es)]
    l1ms = [jnp.where(mask, lb - z, 0.0) for lb, z in zip(lbs, zs)]
    return zs, es, lbs, l1ms


def _dot_split_each(const, vs):
    his = [_mx(v) for v in vs]
    los = [_mx(v - hi.astype(F32)) for v, hi in zip(vs, his)]
    tops = [_dot(const, hi) for hi in his]
    return [top + _dot(const, lo) for top, lo in zip(tops, los)]


def _max_over(runs):
    m = runs[0]
    for r in runs[1:]:
        m = jnp.maximum(m, r)
    return jnp.max(m)


NOT_SEEN = -1e30


def _row_put(ref, g, j, row):
    j8 = pl.multiple_of((j // 8) * 8, 8)
    blk = ref[g, pl.ds(j8, 8), :]
    sub = lax.broadcasted_iota(jnp.int32, blk.shape, 0)
    ref[g, pl.ds(j8, 8), :] = jnp.where(sub == j - j8, row, blk)


def _row_get(ref, g, j):
    j8 = pl.multiple_of((j // 8) * 8, 8)
    blk = ref[g, pl.ds(j8, 8), :]
    sub = lax.broadcasted_iota(jnp.int32, blk.shape, 0)
    return jnp.sum(jnp.where(sub == j - j8, blk, 0.0), axis=0, keepdims=True)


def _att_fwd(k_hm, qt_hm, vt_hm, later, *, name, hg=8):
    h, nb, d, t = k_hm.shape
    s = nb * t
    nbp = -(-nb // 8) * 8
    scale = 1.0 / math.sqrt(d)
    assert h % hg == 0

    def body(k_ref, qt_ref, vt_ref, later_ref, o_ref, runs_ref):
        i = pl.program_id(1)
        q_ts = [qt_ref[g] for g in range(hg)]
        runs_ref[...] = jnp.full(runs_ref.shape, NOT_SEEN, F32)

        def cond(carry):
            j, _, _, rmax = carry
            return jnp.logical_and(j >= 0, rmax > EXP_UNDERFLOW)

        def step(carry):
            j, runs, accs, _ = carry
            mask = _att_mask(t, j * t, i * t)
            heads = range(hg)
            for g in heads:
                _row_put(runs_ref, g, j, runs[g])
            _, _, lbs, l1ms = _att_blocks([k_ref[g, j] for g in heads], q_ts, scale, mask)
            afters = _dot_split_each(later_ref[...], l1ms)
            weights = [_mx(jnp.where(mask, jnp.exp(lbs[g] + afters[g] + runs[g]), 0.0)) for g in heads]
            new_accs = [accs[g] + _dot(vt_ref[g, j], weights[g]) for g in heads]
            new_runs = [runs[g] + jnp.sum(l1ms[g], axis=0, keepdims=True) for g in heads]
            return j - 1, tuple(new_runs), tuple(new_accs), _max_over(new_runs)

        init = (i, tuple(jnp.zeros((1, t), F32) for _ in range(hg)), tuple(jnp.zeros((d, t), F32) for _ in range(hg)),
                jnp.float32(0.0))
        _, _, accs, _ = lax.while_loop(cond, step, init)
        for g in range(hg):
            o_ref[g] = _mx(accs[g])

    return pl.pallas_call(
        body,
        name=name,
        grid=(h // hg, nb),
        in_specs=[
            pl.BlockSpec((hg, nb, d, t), lambda hh, i: (hh, 0, 0, 0)),
            pl.BlockSpec((hg, d, t), lambda hh, i: (hh, 0, i)),
            pl.BlockSpec((hg, nb, d, t), lambda hh, i: (hh, 0, 0, 0)),
            pl.BlockSpec((t, t), lambda hh, i: (0, 0)),
        ],
        out_specs=[pl.BlockSpec((hg, d, t), lambda hh, i: (hh, 0, i)),
                   pl.BlockSpec((hg, None, nbp, t), lambda hh, i: (hh, i, 0, 0))],
        out_shape=[jax.ShapeDtypeStruct((h, d, s), MXU_DTYPE), jax.ShapeDtypeStruct((h, nb, nbp, t), F32)],
        compiler_params=_params(),
    )(k_hm, qt_hm, vt_hm, later)


def _merge_fwd(ys, p, b_gate, w_bo, w_o, l, x, *, gate_col0, name, tm=512):
    s, d = x.shape
    _, nbr, c, _ = w_bo.shape
    tm = _tile(s, tm)

    def body(ya_ref, yb_ref, yc_ref, g0_ref, g1_ref, g2_ref, bg_ref, wbo_ref, wo_ref, x_ref, x1_ref, m_ref):
        merged = jnp.zeros((tm, d), F32)
        for i, (y_ref, g_ref) in enumerate(((ya_ref, g0_ref), (yb_ref, g1_ref), (yc_ref, g2_ref))):
            gate = _sigmoid(_f32(g_ref) + bg_ref[:, i * d:(i + 1) * d])
            merged = merged + gate * _dot(y_ref[...], wbo_ref[i])
        mb = _mx(merged)
        m_ref[...] = mb
        x1_ref[...] = x_ref[...] + _dot(mb, wo_ref[...])

    yblk = pl.BlockSpec((tm, c), lambda i: (i, 0))
    gblk = lambda k: pl.BlockSpec((tm, d), lambda i: (i, gate_col0 + k))
    xblk = pl.BlockSpec((tm, d), lambda i: (i, 0))
    return pl.pallas_call(
        body,
        name=name,
        grid=(s // tm,),
        in_specs=[
            yblk, yblk, yblk, gblk(0), gblk(1), gblk(2),
            pl.BlockSpec((1, nbr * d), lambda i: (0, 0)),
            _resident((None, nbr, c, d), lambda i: (l, 0, 0, 0)),
            _resident((None, d, d), lambda i: (l, 0, 0)),
            xblk,
        ],
        out_specs=[xblk, xblk],
        out_shape=[jax.ShapeDtypeStruct((s, d), F32), jax.ShapeDtypeStruct((s, d), MXU_DTYPE)],
        compiler_params=_params(),
    )(*ys, p, p, p, b_gate, w_bo, w_o, x)


def _ffn_down(gu, w_down, l, x, *, name, tm=512):
    s, d = x.shape
    f = w_down.shape[1]
    tm = _tile(s, tm)

    def body(g_ref, u_ref, w_ref, x_ref, o_ref):
        g = _f32(g_ref)
        act = _mx(g * _sigmoid(g) * _f32(u_ref))
        o_ref[...] = x_ref[...] + _dot(act, w_ref[...])

    return pl.pallas_call(
        body,
        name=name,
        grid=(s // tm,),
        in_specs=[
            pl.BlockSpec((tm, f), lambda i: (i, 0)),
            pl.BlockSpec((tm, f), lambda i: (i, 1)),
            _resident((None, f, d), lambda i: (l, 0, 0)),
            pl.BlockSpec((tm, d), lambda i: (i, 0)),
        ],
        out_specs=pl.BlockSpec((tm, d), lambda i: (i, 0)),
        out_shape=jax.ShapeDtypeStruct((s, d), F32),
        compiler_params=_params(),
    )(gu, gu, w_down, x)


def _loss_head(y, target, *, name, tm=512):
    s, d = y.shape
    tm = _tile(s, tm)

    def body(y_ref, t_ref, l_ref, dy_ref):
        @pl.when(pl.program_id(0) == 0)
        def _():
            l_ref[...] = jnp.zeros_like(l_ref)

        err = y_ref[...] - t_ref[...]
        dy_ref[...] = err * (1.0 / d)
        sq = _row_fold(err * err)
        part = sq[:, 0:128]
        for k in range(1, d // 128):
            part = part + sq[:, k * 128:(k + 1) * 128]
        l_ref[...] += part * (0.5 / d)

    blk = pl.BlockSpec((tm, d), lambda i: (i, 0))
    return pl.pallas_call(
        body,
        name=name,
        grid=(s // tm,),
        in_specs=[blk, blk],
        out_specs=[pl.BlockSpec((8, 128), lambda i: (0, 0)), blk],
        out_shape=[jax.ShapeDtypeStruct((8, 128), F32), jax.ShapeDtypeStruct((s, d), F32)],
        compiler_params=_params(),
    )(y, target)


def _matmul_tn(a, b, slot, n_slots, into, *, name, t1=1024, tn=None, tm=1024, col0=0, n_total=None):
    m, k1 = a.shape
    n = b.shape[1]
    n_total = n_total or n
    t1 = _tile(k1, t1)
    tn = tn or n
    tm = _tile(m, tm)
    steps = m // tm
    assert n % tn == 0 and col0 % tn == 0
    cb0 = col0 // tn

    def body(a_ref, b_ref, *refs):
        o_ref = refs[-1]

        @pl.when(pl.program_id(2) == 0)
        def _():
            o_ref[...] = jnp.zeros_like(o_ref)

        o_ref[...] += _dot_tn(a_ref[...], b_ref[...])

    return pl.pallas_call(
        body,
        name=name,
        grid=(k1 // t1, n // tn, steps),
        in_specs=[
            pl.BlockSpec((tm, t1), lambda i, j, k: (k, i)),
            pl.BlockSpec((tm, tn), lambda i, j, k: (k, j)),
        ] + ([] if into is None else [ANY]),
        out_specs=pl.BlockSpec((None, t1, tn), lambda i, j, k: (slot, i, cb0 + j)),
        out_shape=jax.ShapeDtypeStruct((n_slots, k1, n_total), F32),
        input_output_aliases={} if into is None else {2: 0},
        compiler_params=_params(),
    )(a, b, *([] if into is None else [into]))


def _matmul_nt_normbwd(pieces, w, l, x, g, dres, *, name, tm=512, tk=1024):
    s, d = x.shape
    k = w.shape[2]
    tm = _tile(s, tm)
    n = len(pieces)
    assert sum(pc.shape[1] for pc in pieces) == k

    def body(*refs):
        piece_refs = refs[:n]
        w_ref, x_ref, g_ref, dres_ref, dx_ref, dg_ref = refs[n:]

        @pl.when(pl.program_id(0) == 0)
        def _():
            dg_ref[...] = jnp.zeros_like(dg_ref)

        dh = jnp.zeros((tm, d), F32)
        off = 0
        for pc, pc_ref in zip(pieces, piece_refs):
            width = pc.shape[1]
            step = _tile(width, tk)
            for c0 in range(0, width, step):
                dh = dh + _dot_nt(pc_ref[:, c0:c0 + step], w_ref[:, off + c0:off + c0 + step])
            off += width
        xf = x_ref[...]
        r = lax.rsqrt(jnp.mean(xf * xf, axis=-1, keepdims=True) + EPS)
        y = xf * r
        dy = dh * g_ref[...]
        dx_ref[...] = dres_ref[...] + r * (dy - y * jnp.mean(dy * y, axis=-1, keepdims=True))
        dg_ref[...] += _row_fold(dh * y)

    xblk = pl.BlockSpec((tm, d), lambda i: (i, 0))
    return pl.pallas_call(
        body,
        name=name,
        grid=(s // tm,),
        in_specs=[pl.BlockSpec((tm, pc.shape[1]), lambda i: (i, 0)) for pc in pieces] + [
            _resident((None, d, k), lambda i: (l, 0, 0)),
            xblk,
            pl.BlockSpec((1, d), lambda i: (0, 0)),
            xblk,
        ],
        out_specs=[xblk, pl.BlockSpec((8, d), lambda i: (0, 0))],
        out_shape=[jax.ShapeDtypeStruct((s, d), F32), jax.ShapeDtypeStruct((8, d), F32)],
        compiler_params=_params(),
    )(*pieces, w, x, g, dres)


def _ffn_bwd(dx, gu, w_down, l, *, name, tm=512):
    s, d = dx.shape
    f = w_down.shape[1]
    tm = _tile(s, tm)

    def body(dx_ref, g_ref, u_ref, w_ref, dgu_ref, act_ref, dxb_ref):
        dxb = _mx(dx_ref[...])
        dxb_ref[...] = dxb
        dact = _dot_nt(dxb, w_ref[...])
        g = _f32(g_ref)
        u = _f32(u_ref)
        sg = _sigmoid(g)
        silu = g * sg
        act_ref[...] = _mx(silu * u)
        dgu_ref[:, 0:f] = _mx(dact * u * (sg * (1.0 + g * (1.0 - sg))))
        dgu_ref[:, f:2 * f] = _mx(dact * silu)

    fblk = lambda col: pl.BlockSpec((tm, f), lambda i: (i, col))
    dblk = pl.BlockSpec((tm, d), lambda i: (i, 0))
    return pl.pallas_call(
        body,
        name=name,
        grid=(s // tm,),
        in_specs=[dblk, fblk(0), fblk(1), _resident((None, f, d), lambda i: (l, 0, 0))],
        out_specs=[pl.BlockSpec((tm, 2 * f), lambda i: (i, 0)), fblk(0), dblk],
        out_shape=[
            jax.ShapeDtypeStruct((s, 2 * f), MXU_DTYPE),
            jax.ShapeDtypeStruct((s, f), MXU_DTYPE),
            jax.ShapeDtypeStruct((s, d), MXU_DTYPE),
        ],
        compiler_params=_params(),
    )(dx, gu, gu, w_down)


def _merge_bwd(dx, ys, p, b_gate, w_bo, w_o, l, *, gate_col0, name, tm=512):
    s, d = dx.shape
    _, nbr, c, _ = w_bo.shape
    tm = _tile(s, tm)

    def body(dx_ref, ya_ref, yb_ref, yc_ref, g0_ref, g1_ref, g2_ref, bg_ref, wbo_ref, wo_ref,
             dgate_ref, dya_ref, dyb_ref, dyc_ref, dd0_ref, dd1_ref, dd2_ref, dxb_ref, dbg_ref):
        @pl.when(pl.program_id(0) == 0)
        def _():
            dbg_ref[...] = jnp.zeros_like(dbg_ref)

        dxb = _mx(dx_ref[...])
        dxb_ref[...] = dxb
        dmerged = _dot_nt(dxb, wo_ref[...])
        branches = ((ya_ref, g0_ref, dya_ref, dd0_ref), (yb_ref, g1_ref, dyb_ref, dd1_ref), (yc_ref, g2_ref, dyc_ref, dd2_ref))
        for i, (y_ref, g_ref, dy_ref, dd_ref) in enumerate(branches):
            cols = slice(i * d, (i + 1) * d)
            gate = _sigmoid(_f32(g_ref) + bg_ref[:, cols])
            yd = _dot(y_ref[...], wbo_ref[i])
            dyd = _mx(dmerged * gate)
            dd_ref[...] = dyd
            dy_ref[...] = _dot_nt(dyd, wbo_ref[i])
            dpre = dmerged * yd * gate * (1.0 - gate)
            dgate_ref[:, cols] = _mx(dpre)
            dbg_ref[:, cols] += _row_fold(dpre)

    yblk = pl.BlockSpec((tm, c), lambda i: (i, 0))
    gblk = lambda k: pl.BlockSpec((tm, d), lambda i: (i, gate_col0 + k))
    dblk = pl.BlockSpec((tm, d), lambda i: (i, 0))
    return pl.pallas_call(
        body,
        name=name,
        grid=(s // tm,),
        in_specs=[
            dblk, yblk, yblk, yblk, gblk(0), gblk(1), gblk(2),
            pl.BlockSpec((1, nbr * d), lambda i: (0, 0)),
            _resident((None, nbr, c, d), lambda i: (l, 0, 0, 0)),
            _resident((None, d, d), lambda i: (l, 0, 0)),
        ],
        out_specs=[pl.BlockSpec((tm, nbr * d), lambda i: (i, 0)), yblk, yblk, yblk, dblk, dblk, dblk, dblk,
                   pl.BlockSpec((8, nbr * d), lambda i: (0, 0))],
        out_shape=[jax.ShapeDtypeStruct((s, nbr * d), MXU_DTYPE)] + [jax.ShapeDtypeStruct((s, c), F32)] * 3
        + [jax.ShapeDtypeStruct((s, d), MXU_DTYPE)] * 4 + [jax.ShapeDtypeStruct((8, nbr * d), F32)],
        compiler_params=_params(),
    )(dx, *ys, p, p, p, b_gate, w_bo, w_o)


def _conv_bwd(p, dya, conv_w, conv_b, *, name, tm=512):
    s = p.shape[0]
    c = conv_w.shape[1]
    tm = _tile(s, tm)
    hb = tm // HALO
    last = s // tm - 1

    def body(bg_ref, cg_ref, xa_ref, cgp_ref, xap_ref, dy_ref, dyn_ref, bgn_ref, w_ref, b_ref,
             dp_ref, dw_ref):
        i = pl.program_id(0)

        @pl.when(i == 0)
        def _():
            dw_ref[...] = jnp.zeros_like(dw_ref)

        cg = _f32(cg_ref)
        xa = _f32(xa_ref)
        u = cg * xa
        prev = jnp.where(i == 0, 0.0, _f32(cgp_ref) * _f32(xap_ref))
        um1, um2 = _conv_taps(u, prev)
        w = w_ref[...]
        y = b_ref[...] + w[0:1, :] * um2 + w[1:2, :] * um1 + w[2:3, :] * u
        dya = dy_ref[...]
        dp_ref[:, 0:c] = _mx(dya * y)
        dyv = dya * _f32(bg_ref)
        nxt = jnp.where(i == last, 0.0, dyn_ref[...] * _f32(bgn_ref))
        row = lax.broadcasted_iota(jnp.int32, dyv.shape, 0)
        dp1 = jnp.where(row == tm - 1, nxt[0:1, :], pltpu.roll(dyv, tm - 1, 0))
        dp2 = pltpu.roll(dyv, tm - 2, 0)
        dp2 = jnp.where(row == tm - 2, nxt[0:1, :], jnp.where(row == tm - 1, nxt[1:2, :], dp2))
        du = w[2:3, :] * dyv + w[1:2, :] * dp1 + w[0:1, :] * dp2
        dp_ref[:, c:2 * c] = _mx(du * xa)
        dp_ref[:, 2 * c:3 * c] = _mx(du * cg)
        dw_ref[0] += _row_fold(dyv * um2)
        dw_ref[1] += _row_fold(dyv * um1)
        dw_ref[2] += _row_fold(dyv * u)
        dw_ref[3] += _row_fold(dyv)

    blk = lambda col: pl.BlockSpec((tm, c), lambda i: (i, col))
    halo = lambda col: pl.BlockSpec((HALO, c), lambda i: (jnp.maximum(i * hb - 1, 0), col))
    nhalo = lambda col: pl.BlockSpec((HALO, c), lambda i: (jnp.minimum((i + 1) * hb, s // HALO - 1), col))
    return pl.pallas_call(
        body,
        name=name,
        grid=(s // tm,),
        in_specs=[blk(0), blk(1), blk(2), halo(1), halo(2), blk(0), nhalo(0), nhalo(0),
                  pl.BlockSpec((CONV_K, c), lambda i: (0, 0)), pl.BlockSpec((1, c), lambda i: (0, 0))],
        out_specs=[pl.BlockSpec((tm, 3 * c), lambda i: (i, 0)), pl.BlockSpec((4, 8, c), lambda i: (0, 0, 0))],
        out_shape=[jax.ShapeDtypeStruct((s, 3 * c), MXU_DTYPE), jax.ShapeDtypeStruct((4, 8, c), F32)],
        compiler_params=_params(),
    )(p, p, p, p, p, dya, dya, p, conv_w, conv_b)


def _sgu_bwd(p, dyb, ln_g, ln_b, wtril, wtril_t, bias_full, *, col0, name, tm=512):
    s = p.shape[0]
    c = ln_g.shape[1]
    groups = c // CHUNK
    tm = _tile(s, tm)

    def body(u_ref, v_ref, dy_ref, g_ref, b_ref, w_ref, wt_ref, bias_ref, duv_ref, dln_ref, dw_ref, dbias_ref, dvn_ref):
        @pl.when(pl.program_id(0) == 0)
        def _():
            dln_ref[...] = jnp.zeros_like(dln_ref)
            dw_ref[...] = jnp.zeros_like(dw_ref)
            dbias_ref[...] = jnp.zeros_like(dbias_ref)

        sv = _f32(v_ref)
        xhat, r = _layernorm_stats(_gelu(sv))
        vb = _mx(xhat * g_ref[...] + b_ref[...])
        for n in range(tm // CHUNK):
            rows = slice(n * CHUNK, (n + 1) * CHUNK)
            for gi in range(groups):
                cols = slice(gi * CHUNK, (gi + 1) * CHUNK)
                su = u_ref[rows, cols].astype(F32)
                dy = dy_ref[rows, cols]
                vblk = vb[rows, cols]
                mixed = _dot(w_ref[gi], vblk) + bias_ref[:, cols]
                duv_ref[rows, cols] = _mx(dy * mixed * _gelu_grad(su))
                dmixed = dy * _gelu(su)
                dmb = _mx(dmixed)
                dvn_ref[rows, cols] = _dot(wt_ref[gi], dmb)
                dw_ref[gi] += _dot_nt(dmb, vblk)
                dbias_ref[:, cols] += dmixed
        dvn = dvn_ref[...]
        dln_ref[0] += _row_fold(dvn * xhat)
        dln_ref[1] += _row_fold(dvn)
        dxh = dvn * g_ref[...]
        dgv = r * (dxh - jnp.mean(dxh, axis=-1, keepdims=True) - xhat * jnp.mean(dxh * xhat, axis=-1, keepdims=True))
        duv_ref[:, c:2 * c] = _mx(dgv * _gelu_grad(sv))

    blk = lambda col: pl.BlockSpec((tm, c), lambda i: (i, col))
    vec = pl.BlockSpec((1, c), lambda i: (0, 0))
    wspec = pl.BlockSpec((groups, CHUNK, CHUNK), lambda i: (0, 0, 0))
    return pl.pallas_call(
        body,
        name=name,
        grid=(s // tm,),
        in_specs=[blk(col0), blk(col0 + 1), blk(0), vec, vec, wspec, wspec, pl.BlockSpec((CHUNK, c), lambda i: (0, 0))],
        out_specs=[pl.BlockSpec((tm, 2 * c), lambda i: (i, 0)), pl.BlockSpec((2, 8, c), lambda i: (0, 0, 0)), wspec,
                   pl.BlockSpec((CHUNK, c), lambda i: (0, 0))],
        out_shape=[jax.ShapeDtypeStruct((s, 2 * c), MXU_DTYPE),
                   jax.ShapeDtypeStruct((2, 8, c), F32), jax.ShapeDtypeStruct((groups, CHUNK, CHUNK), F32),
                   jax.ShapeDtypeStruct((CHUNK, c), F32)],
        scratch_shapes=[pltpu.VMEM((tm, c), F32)],
        compiler_params=_params(),
    )(p, p, dyb, ln_g, ln_b, wtril, wtril_t, bias_full)


def _att_bwd(kt_hm, vt_hm, qt_hm, dot_hm, runs, later, earlier, *, name, hg=4):
    h, nb, d, t = kt_hm.shape
    s = nb * t
    nbp = runs.shape[2]
    scale = 1.0 / math.sqrt(d)
    assert h % hg == 0

    def body(kt_ref, vt_ref, qt_ref, dot_ref, runs_ref, later_ref, earlier_ref, dqt_ref, dkt_ref, dvt_ref):
        i = pl.program_id(1)

        @pl.when(i == 0)
        def _():
            dkt_ref[...] = jnp.zeros_like(dkt_ref)
            dvt_ref[...] = jnp.zeros_like(dvt_ref)

        q_ts = [qt_ref[g] for g in range(hg)]
        do_ts = [dot_ref[g] for g in range(hg)]

        best = runs_ref[0]
        for g in range(1, hg):
            best = jnp.maximum(best, runs_ref[g])
        row = lax.broadcasted_iota(jnp.int32, (nbp, 1), 0)
        counts = jnp.logical_and(jnp.max(best, axis=1, keepdims=True) > EXP_UNDERFLOW, row < i)
        seen = jnp.sum(counts.astype(jnp.int32))
        zeros_row = tuple(jnp.zeros((1, t), F32) for _ in range(hg))

        def up(j, carry):
            gsums, dqts = carry
            mask = _att_mask(t, j * t, i * t)
            heads = range(hg)
            zs, es, lbs, l1ms = _att_blocks([kt_ref[g, j] for g in heads], q_ts, scale, mask)
            afters = _dot_split_each(later_ref[...], l1ms)
            das = [_dot_tn(vt_ref[g, j], do_ts[g]) for g in heads]
            weights = [jnp.where(mask, jnp.exp(lbs[g] + afters[g] + _row_get(runs_ref, g, j)), 0.0) for g in heads]
            grs = [das[g] * weights[g] for g in heads]
            gbefores = _dot_split_each(earlier_ref[...], grs)
            dzs = []
            for g in heads:
                inv = 1.0 / (1.0 + es[g])
                pos = zs[g] >= 0.0
                beta = jnp.where(pos, inv, es[g] * inv)
                omb = jnp.where(pos, es[g] * inv, inv)
                dzs.append(_mx(jnp.where(mask, grs[g] * omb - (gbefores[g] + gsums[g]) * beta, 0.0) * scale))
            new_dqts = [dqts[g] + _dot(kt_ref[g, j], dzs[g]) for g in heads]
            for g in heads:
                dkt_ref[g, j] += _dot_nt(q_ts[g], dzs[g])
            for g in heads:
                dvt_ref[g, j] += _dot_nt(do_ts[g], _mx(weights[g]))
            new_gsums = [gsums[g] + jnp.sum(grs[g], axis=0, keepdims=True) for g in heads]
            return tuple(new_gsums), tuple(new_dqts)

        _, dqts = lax.fori_loop(i - seen, i + 1, up, (zeros_row, tuple(jnp.zeros((d, t), F32) for _ in range(hg))))
        for g in range(hg):
            dqt_ref[g] = dqts[g]

    whole = pl.BlockSpec((hg, nb, d, t), lambda hh, i: (hh, 0, 0, 0))
    cols = pl.BlockSpec((hg, d, t), lambda hh, i: (hh, 0, i))
    tri = pl.BlockSpec((t, t), lambda hh, i: (0, 0))
    return pl.pallas_call(
        body,
        name=name,
        grid=(h // hg, nb),
        in_specs=[whole, whole, cols, cols, pl.BlockSpec((hg, None, nbp, t), lambda hh, i: (hh, i, 0, 0)), tri, tri],
        out_specs=[cols, whole, whole],
        out_shape=[jax.ShapeDtypeStruct((h, d, s), F32), jax.ShapeDtypeStruct((h, nb, d, t), F32),
                   jax.ShapeDtypeStruct((h, nb, d, t), F32)],
        compiler_params=_params(),
    )(kt_hm, vt_hm, qt_hm, dot_hm, runs, later, earlier)


def _qkv_bwd(p, dqn, dkn, dv, qg, kg, headavg, *, col0, name, tm=512):
    s = p.shape[0]
    c = qg.shape[1]
    tm = _tile(s, tm)

    def body(q_ref, k_ref, dqn_ref, dkn_ref, dv_ref, qg_ref, kg_ref, avg_ref, dqkv_ref, dg_ref):
        @pl.when(pl.program_id(0) == 0)
        def _():
            dg_ref[...] = jnp.zeros_like(dg_ref)

        for n, (src, dn_ref, g_ref) in enumerate(((q_ref, dqn_ref, qg_ref), (k_ref, dkn_ref, kg_ref))):
            xf = _f32(src)
            r = lax.rsqrt(_head_mean(xf * xf, avg_ref[...]) + EPS)
            y = xf * r
            dn = dn_ref[...]
            dy = dn * g_ref[...]
            dqkv_ref[:, n * c:(n + 1) * c] = _mx(r * (dy - y * _head_mean(dy * y, avg_ref[...])))
            dg_ref[n] += _row_fold(dn * y)
        dqkv_ref[:, 2 * c:3 * c] = _mx(dv_ref[...])

    blk = lambda col: pl.BlockSpec((tm, c), lambda i: (i, col))
    vec = pl.BlockSpec((1, c), lambda i: (0, 0))
    return pl.pallas_call(
        body,
        name=name,
        grid=(s // tm,),
        in_specs=[blk(col0), blk(col0 + 1), blk(0), blk(0), blk(0), vec, vec, pl.BlockSpec((c, c), lambda i: (0, 0))],
        out_specs=[pl.BlockSpec((tm, 3 * c), lambda i: (i, 0)), pl.BlockSpec((2, 8, c), lambda i: (0, 0, 0))],
        out_shape=[jax.ShapeDtypeStruct((s, 3 * c), MXU_DTYPE), jax.ShapeDtypeStruct((2, 8, c), F32)],
        compiler_params=_params(),
    )(p, p, dqn, dkn, dv, qg, kg, headavg)


def _place():
    x, y, c = lax.axis_index("x"), lax.axis_index("y"), lax.axis_index("c")
    chips = [(1 - x, y), (x, 1 - y), (1 - x, 1 - y)]
    return x, y, c, chips


def _shard_of(ref, axis, chip, width):
    idx = [slice(None)] * len(ref.shape)
    idx[axis] = pl.ds(chip * width, width)
    return ref.at[tuple(idx)]


def _place_shard(w, axis, chip_idx, dtype, *, name):
    layers = w.shape[0]
    rows, cols = _as_rows(w.shape[1:])
    if axis == len(w.shape) - 2:
        tr = _row_tile(rows, cols, 2)
        out_shape = (layers, rows, cols * N_CHIPS)
        out_spec = pl.BlockSpec((None, tr, cols), lambda l, i, j_ref: (l, i, j_ref[0]))
    else:
        assert axis == 0
        tr = _row_tile(rows, cols, 2)
        per = rows // tr
        out_shape = (layers, rows * N_CHIPS, cols)
        out_spec = pl.BlockSpec((None, tr, cols), lambda l, i, j_ref: (l, j_ref[0] * per + i, 0))
    full = list(w.shape)
    full[1 + axis] *= N_CHIPS

    def body(j_ref, w_ref, o_ref):
        o_ref[...] = w_ref[...].astype(dtype)

    out = pl.pallas_call(
        body,
        name=name,
        grid_spec=pltpu.PrefetchScalarGridSpec(
            num_scalar_prefetch=1,
            grid=(layers, rows // tr),
            in_specs=[pl.BlockSpec((None, tr, cols), lambda l, i, j_ref: (l, i, 0))],
            out_specs=out_spec,
        ),
        out_shape=jax.ShapeDtypeStruct(out_shape, dtype),
        compiler_params=_params(),
    )(chip_idx, w.reshape(layers, rows, cols))
    return out.reshape(full)


def _gather_weights(placed, axes, *, name):
    n = len(placed)
    widths = [pa.shape[1 + ax] // N_CHIPS for pa, ax in zip(placed, axes)]

    def body(*refs):
        outs = refs[n:2 * n]
        send_sem, recv_sem, fsend_sem, frecv_sem = refs[2 * n:]
        x, y, c, chips = _place()
        me = 2 * x + y
        sibling = (x, y, 1 - c)

        def slot(a, layer, chip):
            return _shard_of(outs[a].at[layer], axes[a], chip, widths[a])

        def send(a, k):
            return pltpu.make_async_remote_copy(
                src_ref=slot(a, c, me), dst_ref=slot(a, c, me), send_sem=send_sem.at[a, k], recv_sem=recv_sem.at[a, k],
                device_id=(*chips[k], c), device_id_type=MESH)

        def landed(a, k):
            cx, cy = chips[k]
            blk = slot(a, c, 2 * cx + cy)
            return pltpu.make_async_remote_copy(
                src_ref=blk, dst_ref=blk, send_sem=send_sem.at[a, k], recv_sem=recv_sem.at[a, k],
                device_id=(*chips[k], c), device_id_type=MESH)

        def forward(a, k, layer):
            cx, cy = chips[k]
            blk = slot(a, layer, 2 * cx + cy)
            return pltpu.make_async_remote_copy(
                src_ref=blk, dst_ref=blk, send_sem=fsend_sem.at[a, k], recv_sem=frecv_sem.at[a, k],
                device_id=sibling, device_id_type=MESH)

        for a in range(n):
            for k in range(3):
                send(a, k).start()
        for a in range(n):
            for k in range(3):
                landed(a, k).wait_recv()
                forward(a, k, c).start()
        for a in range(n):
            for k in range(3):
                forward(a, k, 1 - c).wait_recv()
        for a in range(n):
            for k in range(3):
                send(a, k).wait_send()
                forward(a, k, c).wait_send()

    return pl.pallas_call(
        body,
        name=name,
        in_specs=[ANY] * n,
        out_specs=[ANY] * n,
        out_shape=[jax.ShapeDtypeStruct(pa.shape, pa.dtype) for pa in placed],
        input_output_aliases={a: a for a in range(n)},
        scratch_shapes=[pltpu.SemaphoreType.DMA((n, 3))] * 4,
    )(*placed)


def _pair_exchange(grads, *, name):
    n = len(grads)

    def body(*refs):
        ins, outs = refs[:n], refs[n:2 * n]
        send_sem, recv_sem = refs[2 * n:]
        x, y, c, _ = _place()
        copies = [
            pltpu.make_async_remote_copy(
                src_ref=ins[a].at[1 - c], dst_ref=outs[a], send_sem=send_sem.at[a], recv_sem=recv_sem.at[a],
                device_id=(x, y, 1 - c), device_id_type=MESH)
            for a in range(n)
        ]
        for cp in copies:
            cp.start()
        for cp in copies:
            cp.wait()

    return pl.pallas_call(
        body,
        name=name,
        in_specs=[ANY] * n,
        out_specs=[ANY] * n,
        out_shape=[jax.ShapeDtypeStruct(g.shape[1:], g.dtype) for g in grads],
        scratch_shapes=[pltpu.SemaphoreType.DMA((n,)), pltpu.SemaphoreType.DMA((n,))],
    )(*grads)


def _chip_scatter(sums, axes, *, name):
    n = len(sums)
    widths = [sm.shape[ax] // N_CHIPS for sm, ax in zip(sums, axes)]
    out_shapes = []
    for sm, ax, w in zip(sums, axes, widths):
        shp = list(sm.shape)
        shp[ax] = w
        out_shapes.append(jax.ShapeDtypeStruct((3, *shp), sm.dtype))

    def body(*refs):
        ins, outs = refs[:n], refs[n:2 * n]
        send_sem, recv_sem = refs[2 * n:]
        x, y, c, chips = _place()
        copies = []
        for a in range(n):
            for k in range(3):
                cx, cy = chips[k]
                copies.append(pltpu.make_async_remote_copy(
                    src_ref=_shard_of(ins[a], axes[a], 2 * cx + cy, widths[a]), dst_ref=outs[a].at[k],
                    send_sem=send_sem.at[a, k], recv_sem=recv_sem.at[a, k], device_id=(cx, cy, c), device_id_type=MESH))
        for cp in copies:
            cp.start()
        for cp in copies:
            cp.wait()

    return pl.pallas_call(
        body,
        name=name,
        in_specs=[ANY] * n,
        out_specs=[ANY] * n,
        out_shape=out_shapes,
        scratch_shapes=[pltpu.SemaphoreType.DMA((n, 3)), pltpu.SemaphoreType.DMA((n, 3))],
    )(*sums)


def _pair_share(finals, *, name):
    n = len(finals)

    def body(*refs):
        outs = refs[n:2 * n]
        send_sem, recv_sem = refs[2 * n:]
        x, y, c, _ = _place()
        for a in range(n):
            pltpu.make_async_remote_copy(
                src_ref=outs[a].at[c], dst_ref=outs[a].at[c], send_sem=send_sem.at[a], recv_sem=recv_sem.at[a],
                device_id=(x, y, 1 - c), device_id_type=MESH).start()
        for a in range(n):
            pltpu.make_async_remote_copy(
                src_ref=outs[a].at[c], dst_ref=outs[a].at[1 - c], send_sem=send_sem.at[a], recv_sem=recv_sem.at[a],
                device_id=(x, y, 1 - c), device_id_type=MESH).wait()

    return pl.pallas_call(
        body,
        name=name,
        in_specs=[ANY] * n,
        out_specs=[ANY] * n,
        out_shape=[jax.ShapeDtypeStruct(f.shape, f.dtype) for f in finals],
        input_output_aliases={a: a for a in range(n)},
        scratch_shapes=[pltpu.SemaphoreType.DMA((n,))] * 2,
    )(*finals)


def _small_view(shape):
    size = math.prod(shape)
    return (size // 128, 128) if size % 1024 == 0 else (shape[0], size // shape[0])


def _all_reduce_small(parts, *, name):
    n = len(parts)
    views = [_small_view(a.shape) for a in parts]

    def body(*refs):
        ins, outs, slots = refs[:n], refs[n:2 * n], refs[2 * n:3 * n]
        send_sem, recv_sem = refs[3 * n:]
        x, y, c, _ = _place()
        me = 4 * x + 2 * y + c
        copies = []
        for a in range(n):
            slots[a][me] = ins[a][...]
            for k in range(1, N_DEV):
                peer = (x ^ (k >> 2), y ^ ((k >> 1) & 1), c ^ (k & 1))
                copies.append(pltpu.make_async_remote_copy(
                    src_ref=ins[a], dst_ref=slots[a].at[me], send_sem=send_sem.at[a, k - 1],
                    recv_sem=recv_sem.at[a, k - 1], device_id=peer, device_id_type=MESH))
        for cp in copies:
            cp.start()
        for a in range(n):
            for k in range(1, N_DEV):
                pltpu.make_async_remote_copy(
                    src_ref=ins[a], dst_ref=slots[a].at[me ^ k], send_sem=send_sem.at[a, k - 1],
                    recv_sem=recv_sem.at[a, k - 1], device_id=(x, y, c), device_id_type=MESH).wait()
        for a in range(n):
            total = slots[a][0]
            for dev in range(1, N_DEV):
                total = total + slots[a][dev]
            outs[a][...] = total

    vmem = pl.BlockSpec(memory_space=pltpu.VMEM)
    outs = pl.pallas_call(
        body,
        name=name,
        in_specs=[vmem] * n,
        out_specs=[vmem] * n,
        out_shape=[jax.ShapeDtypeStruct(view, F32) for view in views],
        scratch_shapes=[pltpu.VMEM((N_DEV, *view), F32) for view in views]
        + [pltpu.SemaphoreType.DMA((n, N_DEV - 1)), pltpu.SemaphoreType.DMA((n, N_DEV - 1))],
        compiler_params=_params(),
    )(*[a.reshape(view) for a, view in zip(parts, views)])
    return [o.reshape(a.shape) for o, a in zip(outs, parts)]


def _as_rows(shape):
    cols = shape[-1]
    return math.prod(shape[:-1]), cols


ELEMENTWISE_VMEM = 24 * 1024 * 1024


def _row_tile(rows, cols, n_arrays, sublanes=8):
    cap = ELEMENTWISE_VMEM // (n_arrays * 2 * 4 * cols)
    best = None
    for t in range(sublanes, min(rows, cap) + 1, sublanes):
        if rows % t == 0:
            best = t
    assert best is not None, (rows, cols)
    return best


def _pair_sum(g, other, c_idx, *, name):
    rows, cols = _as_rows(other.shape)
    tr = _row_tile(rows, cols, 4, sublanes=16)
    g2 = g.reshape(2, rows, cols)

    def body(c_ref, g_ref, o_ref, out_ref, wire_ref):
        total = g_ref[...] + o_ref[...]
        out_ref[...] = total
        wire_ref[...] = total.astype(WIRE_DTYPE)

    blk = pl.BlockSpec((tr, cols), lambda i, c_ref: (i, 0))
    out, wire = pl.pallas_call(
        body,
        name=name,
        grid_spec=pltpu.PrefetchScalarGridSpec(
            num_scalar_prefetch=1,
            grid=(rows // tr,),
            in_specs=[pl.BlockSpec((None, tr, cols), lambda i, c_ref: (c_ref[0], i, 0)), blk],
            out_specs=[blk, blk],
        ),
        out_shape=[jax.ShapeDtypeStruct((rows, cols), F32), jax.ShapeDtypeStruct((rows, cols), WIRE_DTYPE)],
        compiler_params=_params(),
    )(c_idx, g2, other.reshape(rows, cols))
    return out.reshape(other.shape), wire.reshape(other.shape)


def _chip_sum(mine, got, axis, chip_idx, c_idx, *, name):
    shard_shape = got.shape[1:]
    rows, cols = _as_rows(shard_shape)
    tr = _row_tile(rows, cols, 5, sublanes=16)
    if axis == len(mine.shape) - 1:
        m2 = mine.reshape(rows, cols * N_CHIPS)
        mine_spec = pl.BlockSpec((tr, cols), lambda i, j_ref, c_ref: (i, j_ref[0]))
    else:
        assert axis == 0
        m2 = mine.reshape(N_CHIPS, rows, cols)
        mine_spec = pl.BlockSpec((None, tr, cols), lambda i, j_ref, c_ref: (j_ref[0], i, 0))

    def body(j_ref, c_ref, m_ref, got_ref, out_ref):
        out_ref[...] = ((m_ref[...] + got_ref[0].astype(F32)) + got_ref[1].astype(F32)) + got_ref[2].astype(F32)

    out = pl.pallas_call(
        body,
        name=name,
        grid_spec=pltpu.PrefetchScalarGridSpec(
            num_scalar_prefetch=2,
            grid=(rows // tr,),
            in_specs=[mine_spec, pl.BlockSpec((3, tr, cols), lambda i, j_ref, c_ref: (0, i, 0))],
            out_specs=pl.BlockSpec((None, tr, cols), lambda i, j_ref, c_ref: (c_ref[0], i, 0)),
        ),
        out_shape=jax.ShapeDtypeStruct((2, rows, cols), F32),
        compiler_params=_params(),
    )(chip_idx, c_idx, m2, got.reshape(3, rows, cols))
    return out.reshape((2, *shard_shape))


def _adamw_update(w_ref, g_ref, m_ref, v_ref, d_ref, nm_ref, nv_ref):
    c1 = 1.0 / (1.0 - ADAM_B1 ** ADAM_STEP)
    c2 = 1.0 / (1.0 - ADAM_B2 ** ADAM_STEP)
    gg = g_ref[...]
    nm = ADAM_B1 * m_ref[...] + (1.0 - ADAM_B1) * gg
    nv = ADAM_B2 * v_ref[...] + (1.0 - ADAM_B2) * (gg * gg)
    nm_ref[...] = nm
    nv_ref[...] = nv
    d_ref[...] = -ADAM_LR * ((nm * c1) / (jnp.sqrt(nv * c2) + ADAM_EPS) + ADAM_WD * w_ref[...])


def _adamw_small(ws, gs, ms, vs, *, name):
    n = len(ws)
    views = [_small_view(a.shape) for a in ws]

    def body(*refs):
        w_r, g_r, m_r, v_r = (refs[k * n:(k + 1) * n] for k in range(4))
        d_r, nm_r, nv_r = (refs[(4 + k) * n:(5 + k) * n] for k in range(3))
        for a in range(n):
            _adamw_update(w_r[a], g_r[a], m_r[a], v_r[a], d_r[a], nm_r[a], nv_r[a])

    vmem = pl.BlockSpec(memory_space=pltpu.VMEM)
    flat = lambda arrs: [a.reshape(view) for a, view in zip(arrs, views)]
    outs = pl.pallas_call(
        body,
        name=name,
        in_specs=[vmem] * (4 * n),
        out_specs=[vmem] * (3 * n),
        out_shape=[jax.ShapeDtypeStruct(view, F32) for view in views] * 3,
        compiler_params=_params(),
    )(*flat(ws), *flat(gs), *flat(ms), *flat(vs))
    return [tuple(outs[k * n + a].reshape(ws[a].shape) for k in range(3)) for a in range(n)]


def _adamw(w, g, m, v, *, name):
    shape = w.shape
    rows, cols = _as_rows(shape)
    tr = _row_tile(rows, cols, 7)
    body = functools.partial(_adamw_update)

    blk = pl.BlockSpec((tr, cols), lambda i: (i, 0))
    flat = lambda a: a.reshape(rows, cols)
    outs = pl.pallas_call(
        body,
        name=name,
        grid=(rows // tr,),
        in_specs=[blk] * 4,
        out_specs=[blk] * 3,
        out_shape=[jax.ShapeDtypeStruct((rows, cols), F32)] * 3,
        compiler_params=_params(),
    )(flat(w), flat(g), flat(m), flat(v))
    return tuple(o.reshape(shape) for o in outs)


def _cols_blocked(a, t):
    return a.reshape(a.shape[0] // t, t, HEADS, HEAD_DIM).transpose(2, 0, 3, 1)


def _cols(a):
    return a.reshape(a.shape[0], HEADS, HEAD_DIM).transpose(1, 2, 0)


def _from_cols(a):
    h, d, s = a.shape
    return a.transpose(2, 0, 1).reshape(s, h * d)


def _from_cols_blocked(a):
    h, nb, d, t = a.shape
    return a.transpose(1, 3, 0, 2).reshape(nb * t, h * d)


BIG = ("w_in", "w_branch_out", "w_o", "w_gate_up", "w_down")
BIG_AXIS = {"w_in": 1, "w_branch_out": 2, "w_o": 0, "w_gate_up": 1, "w_down": 0}
SMALL = ("mix_norm_g", "b_gate", "conv_w", "conv_b", "sgu_ln_g", "sgu_ln_b", "sgu_w", "sgu_b", "q_norm_g", "k_norm_g",
         "ffn_norm_g")
ORDER = ("mix_norm_g", "w_in", "b_gate", "conv_w", "conv_b", "sgu_ln_g", "sgu_ln_b", "sgu_w", "sgu_b", "q_norm_g",
         "k_norm_g", "w_branch_out", "w_o", "ffn_norm_g", "w_gate_up", "w_down")


def _layer_forward(x, w, l):
    t = ATT_BLOCK
    c = w["conv_b"].shape[1]
    n_in = w["w_in"].shape[2]
    gate_col0 = (n_in - 3 * x.shape[1]) // x.shape[1]
    tag = f"l{l}"
    p, h = _norm_matmul(x, w["mix_norm_g"][l][None], w["w_in"], l, name=f"in_proj_{tag}", tn=n_in // 7)
    ya = _conv_fwd(p, w["conv_w"][l], w["conv_b"][l][None], name=f"conv_{tag}")
    yb = _sgu_fwd(p, w["sgu_ln_g"][l][None], w["sgu_ln_b"][l][None], w["wtril"][l], w["bias_full"][l], col0=3,
                  name=f"sgu_{tag}")
    qn, kn, vb = _qkv_prep(p, w["qg"][l], w["kg"][l], w["headavg"], col0=5, name=f"qkv_{tag}")
    q_c, k_cb, v_cb = _cols(qn), _cols_blocked(kn, t), _cols_blocked(vb, t)
    out_t, runs = _att_fwd(k_cb, q_c, v_cb, w["later"], name=f"att_{tag}")
    yc = _from_cols(out_t)
    x1, merged = _merge_fwd((ya, yb, yc), p, w["b_gate"][l][None], w["w_branch_out"], w["w_o"], l, x,
                            gate_col0=gate_col0, name=f"merge_{tag}")
    gu, h2 = _norm_matmul(x1, w["ffn_norm_g"][l][None], w["w_gate_up"], l, name=f"gate_up_{tag}",
                          tn=w["w_gate_up"].shape[2] // 4)
    x2 = _ffn_down(gu, w["w_down"], l, x1, name=f"down_{tag}")
    saved = dict(x=x, p=p, h=h, ya=ya, yb=yb, yc=yc, merged=merged, x1=x1, gu=gu, h2=h2,
                 att=(k_cb, v_cb, q_c, runs), gate_col0=gate_col0)
    return x2, saved


def _layer_backward(dx2, w, sv, l, big):
    layers = w["w_in"].shape[0]
    t = ATT_BLOCK
    tag = f"l{l}"
    d = dx2.shape[1]
    c = w["conv_b"].shape[1]
    n_in = w["w_in"].shape[2]
    n_ff = w["w_gate_up"].shape[2]
    g = {}
    dgu, act, dx2b = _ffn_bwd(dx2, sv["gu"], w["w_down"], l, name=f"down_bwd_{tag}")
    big["w_down"] = _matmul_tn(act, dx2b, l, layers, big.get("w_down"), name=f"dw_down_{tag}", t1=act.shape[1] // 2)
    dx1, dg2 = _matmul_nt_normbwd([dgu], w["w_gate_up"], l, sv["x1"], w["ffn_norm_g"][l][None], dx2,
                                  name=f"gate_up_bwd_{tag}")
    g["ffn_norm_g"] = jnp.sum(dg2, axis=0)
    big["w_gate_up"] = _matmul_tn(sv["h2"], dgu, l, layers, big.get("w_gate_up"), name=f"dw_gate_up_{tag}", tn=n_ff // 4)
    ys = (sv["ya"], sv["yb"], sv["yc"])
    (dgates, dya, dyb, dyc, dd0, dd1, dd2, dx1b, dbg) = _merge_bwd(
        dx1, ys, sv["p"], w["b_gate"][l][None], w["w_branch_out"], w["w_o"], l, gate_col0=sv["gate_col0"],
        name=f"merge_bwd_{tag}")
    g["b_gate"] = jnp.sum(dbg, axis=0)
    big["w_o"] = _matmul_tn(sv["merged"], dx1b, l, layers, big.get("w_o"), name=f"dw_o_{tag}")
    for i, (y, dd) in enumerate(zip(ys, (dd0, dd1, dd2))):
        big["w_branch_out"] = _matmul_tn(y, dd, len(ys) * l + i, len(ys) * layers, big.get("w_branch_out"),
                                         name=f"dw_bo{i}_{tag}")
    dconv, dwc = _conv_bwd(sv["p"], dya, w["conv_w"][l], w["conv_b"][l][None], name=f"conv_bwd_{tag}")
    dwc = jnp.sum(dwc, axis=1)
    g["conv_w"] = dwc[0:CONV_K]
    g["conv_b"] = dwc[CONV_K]
    dsgu, dln, dws, dbias = _sgu_bwd(sv["p"], dyb, w["sgu_ln_g"][l][None], w["sgu_ln_b"][l][None], w["wtril"][l],
                                     w["wtril_t"][l], w["bias_full"][l], col0=3, name=f"sgu_bwd_{tag}")
    dln = jnp.sum(dln, axis=1)
    g["sgu_ln_g"], g["sgu_ln_b"] = dln[0], dln[1]
    g["sgu_w"] = jnp.where(w["tril"], dws, 0.0)
    g["sgu_b"] = jnp.sum(dbias.reshape(CHUNK, c // CHUNK, CHUNK), axis=2).T
    k_cb, v_cb, q_c, runs = sv["att"]
    dqt, dkt, dvt = _att_bwd(k_cb, v_cb, q_c, _cols(_mx(dyc)), runs, w["later"], w["earlier"], name=f"att_bwd_{tag}")
    dqn, dkn, dvv = _from_cols(dqt), _from_cols_blocked(dkt), _from_cols_blocked(dvt)
    dqkv, dqkg = _qkv_bwd(sv["p"], dqn, dkn, dvv, w["qg"][l], w["kg"][l], w["headavg"], col0=5, name=f"qkv_bwd_{tag}")
    dqkg = jnp.sum(dqkg.reshape(2, 8 * HEADS, HEAD_DIM), axis=1)
    g["q_norm_g"], g["k_norm_g"] = dqkg[0], dqkg[1]
    pieces = [dconv, dsgu, dqkv, dgates]
    dx0, dg1 = _matmul_nt_normbwd(pieces, w["w_in"], l, sv["x"], w["mix_norm_g"][l][None], dx1, name=f"in_proj_bwd_{tag}")
    g["mix_norm_g"] = jnp.sum(dg1, axis=0)
    col0 = 0
    for k, pc in enumerate(pieces):
        width = pc.shape[1]
        big["w_in"] = _matmul_tn(sv["h"], pc, l, layers, big.get("w_in"), name=f"dw_in{k}_{tag}", col0=col0, n_total=n_in,
                                 tn=math.gcd(col0, width) if col0 else width)
        col0 += width
    return dx0, g


def kernel(x, mix_norm_g, w_in, b_gate, conv_w, conv_b, sgu_ln_g, sgu_ln_b, sgu_w, sgu_b, q_norm_g, k_norm_g, w_branch_out, w_o, ffn_norm_g, w_gate_up, w_down, loss_target, m_mix_norm_g, m_w_in, m_b_gate, m_conv_w, m_conv_b, m_sgu_ln_g, m_sgu_ln_b, m_sgu_w, m_sgu_b, m_q_norm_g, m_k_norm_g, m_w_branch_out, m_w_o, m_ffn_norm_g, m_w_gate_up, m_w_down, v_mix_norm_g, v_w_in, v_b_gate, v_conv_w, v_conv_b, v_sgu_ln_g, v_sgu_ln_b, v_sgu_w, v_sgu_b, v_q_norm_g, v_k_norm_g, v_w_branch_out, v_w_o, v_ffn_norm_g, v_w_gate_up, v_w_down):
    given = dict(locals())
    params = {n: given[n] for n in ORDER}
    moms = {n: (given["m_" + n], given["v_" + n]) for n in ORDER}
    layers = mix_norm_g.shape[0]
    assert layers == 2, "the exchanges split the work of a chip's two cores by layer"
    xs = x[0]
    target = loss_target[0]
    chip = 2 * lax.axis_index("x") + lax.axis_index("y")
    core = lax.axis_index("c")

    c_idx = core.reshape(1).astype(jnp.int32)
    j_idx = chip.reshape(1).astype(jnp.int32)
    width = conv_w.shape[2]
    placed = [_place_shard(params[n], BIG_AXIS[n], j_idx, MXU_DTYPE, name=f"place_{n}") for n in BIG]
    placed.append(lax.dynamic_update_slice_in_dim(jnp.zeros((layers, CONV_K, width * N_CHIPS), F32), conv_w, chip * width, 2))
    gathered = _gather_weights(placed, [BIG_AXIS[n] for n in BIG] + [1], name="gather_weights")
    w = dict(zip(BIG + ("conv_w",), gathered))
    for n in ("mix_norm_g", "b_gate", "conv_b", "sgu_ln_g", "sgu_ln_b", "ffn_norm_g"):
        w[n] = params[n]
    groups = sgu_w.shape[1]
    tril = jnp.tril(jnp.ones((CHUNK, CHUNK), dtype=bool))
    w["tril"] = tril
    w["wtril"] = _mx(jnp.where(tril, sgu_w, 0.0))
    w["wtril_t"] = w["wtril"].transpose(0, 1, 3, 2)
    w["bias_full"] = jnp.repeat(sgu_b.transpose(0, 2, 1), CHUNK, axis=2)
    w["qg"] = jnp.tile(q_norm_g, (1, HEADS))[:, None, :]
    w["kg"] = jnp.tile(k_norm_g, (1, HEADS))[:, None, :]
    lane = jnp.arange(HEADS * HEAD_DIM) // HEAD_DIM
    w["headavg"] = _mx(jnp.where(lane[:, None] == lane[None, :], 1.0 / HEAD_DIM, 0.0))
    pos = jnp.arange(ATT_BLOCK)
    w["later"] = _mx(jnp.where(pos[None, :] > pos[:, None], 1.0, 0.0))
    w["earlier"] = _mx(jnp.where(pos[None, :] < pos[:, None], 1.0, 0.0))

    saved = []
    act = xs
    for l in range(layers):
        act, sv = _layer_forward(act, w, l)
        saved.append(sv)
    loss_part, dact = _loss_head(act, target, name="loss_head")
    loss = lax.psum(jnp.sum(loss_part), ("x", "y", "c"))
    grads = [None] * layers
    big = {}
    for l in reversed(range(layers)):
        dact, grads[l] = _layer_backward(dact, w, saved[l], l, big)
    grad_x = dact[None]
    local = {n: jnp.stack([grads[l][n] for l in range(layers)]) for n in SMALL}
    for n in BIG:
        local[n] = big[n].reshape(w[n].shape)

    got = _pair_exchange([local[n] for n in BIG], name="grads_pair_exchange")
    pair = [_pair_sum(local[n], o, c_idx, name=f"pair_sum_{n}") for n, o in zip(BIG, got)]
    landed = _chip_scatter([wire for _, wire in pair], [BIG_AXIS[n] for n in BIG], name="grads_chip_scatter")
    final = [_chip_sum(pr, ld, BIG_AXIS[n], j_idx, c_idx, name=f"chip_sum_{n}") for n, (pr, _), ld in zip(BIG, pair, landed)]
    full = dict(zip(BIG, _pair_share(final, name="grads_pair_share")))

    summed = _all_reduce_small([local[n] for n in SMALL], name="grads_all_reduce_small")
    for n, gsum in zip(SMALL, summed):
        full[n] = gsum
    full["conv_w"] = lax.dynamic_slice_in_dim(full["conv_w"], chip * width, width, axis=2)

    out = {}
    for n in BIG:
        out[n] = _adamw(params[n], full[n], *moms[n], name=f"adamw_{n}")
    small = _adamw_small([params[n] for n in SMALL], [full[n] for n in SMALL], [moms[n][0] for n in SMALL],
                         [moms[n][1] for n in SMALL], name="adamw_small")
    for n, triple in zip(SMALL, small):
        out[n] = triple
    return (loss, grad_x, *[full[n] for n in ORDER], *[out[n][0] for n in ORDER], *[out[n][1] for n in ORDER],
            *[out[n][2] for n in ORDER])
```

```python
import functools
import math

import jax
import jax.numpy as jnp
from jax import lax
from jax.experimental import pallas as pl
from jax.experimental.pallas import tpu as pltpu

F32 = jnp.float32
MXU_DTYPE = jnp.bfloat16
WIRE_DTYPE = jnp.bfloat16
ACT_DTYPE = jnp.bfloat16
HALO = 16

EPS = 1e-6
CONV_K = 3
CHUNK = 128
HEADS = 8
HEAD_DIM = 64
ATT_BLOCK = 128
EXP_UNDERFLOW = -88.0

ADAM_LR = 0.001
ADAM_B1 = 0.9
ADAM_B2 = 0.999
ADAM_EPS = 1e-08
ADAM_WD = 0.01
ADAM_STEP = 10

VMEM_LIMIT = 56 * 1024 * 1024
MESH = pl.DeviceIdType.MESH
N_CHIPS = 4
N_DEV = 8
ANY = pl.BlockSpec(memory_space=pl.ANY)


def _params(**kw):
    return pltpu.CompilerParams(vmem_limit_bytes=VMEM_LIMIT, **kw)


def _mx(v):
    return v.astype(MXU_DTYPE)


def _dot(a, b):
    return lax.dot_general(a, b, (((1,), (0,)), ((), ())), preferred_element_type=F32)


def _dot_nt(a, b):
    return lax.dot_general(a, b, (((1,), (1,)), ((), ())), preferred_element_type=F32)


def _dot_tn(a, b):
    return lax.dot_general(a, b, (((0,), (0,)), ((), ())), preferred_element_type=F32)


def _dot_split(const, v):
    hi = _mx(v)
    lo = _mx(v - hi.astype(F32))
    return _dot(const, hi) + _dot(const, lo)


def _dot_split_r(v, const):
    hi = _mx(v)
    lo = _mx(v - hi.astype(F32))
    return _dot(hi, const) + _dot(lo, const)


def _sigmoid(x):
    return 1.0 / (1.0 + jnp.exp(-x))


_INV_SQRT2 = 1.0 / math.sqrt(2.0)
_INV_SQRT2PI = 1.0 / math.sqrt(2.0 * math.pi)


def _gelu(x):
    return 0.5 * x * (1.0 + lax.erf(x * _INV_SQRT2))


def _gelu_grad(x):
    return 0.5 * (1.0 + lax.erf(x * _INV_SQRT2)) + x * jnp.exp(-0.5 * x * x) * _INV_SQRT2PI


def _row_fold(v):
    m, n = v.shape
    return jnp.sum(v.reshape(m // 8, 8, n), axis=0)


def _tile(m, pref):
    t = min(m, pref)
    while m % t:
        t //= 2
    return t


def _resident(block_shape, index_map):
    return pl.BlockSpec(block_shape, index_map, pipeline_mode=pl.Buffered(1))


def _norm_matmul(x, g, w, l, *, name, tm=512, tn=None):
    s, d = x.shape
    n = w.shape[2]
    tm = _tile(s, tm)
    tn = tn or n
    assert n % tn == 0

    def body(x_ref, g_ref, w_ref, p_ref, h_ref):
        xf = x_ref[...]
        r = lax.rsqrt(jnp.mean(xf * xf, axis=-1, keepdims=True) + EPS)
        h = _mx(xf * r * g_ref[...])
        h_ref[...] = h
        for j in range(n // tn):
            cols = slice(j * tn, (j + 1) * tn)
            p_ref[:, cols] = _dot(h, w_ref[:, cols]).astype(ACT_DTYPE)

    return pl.pallas_call(
        body,
        name=name,
        grid=(s // tm,),
        in_specs=[
            pl.BlockSpec((tm, d), lambda i: (i, 0)),
            pl.BlockSpec((1, d), lambda i: (0, 0)),
            _resident((None, d, n), lambda i: (l, 0, 0)),
        ],
        out_specs=[
            pl.BlockSpec((tm, n), lambda i: (i, 0)),
            pl.BlockSpec((tm, d), lambda i: (i, 0)),
        ],
        out_shape=[jax.ShapeDtypeStruct((s, n), ACT_DTYPE), jax.ShapeDtypeStruct((s, d), MXU_DTYPE)],
        compiler_params=_params(),
    )(x, g, w)


def _conv_taps(u, prev):
    row = lax.broadcasted_iota(jnp.int32, u.shape, 0)
    last, before = prev[HALO - 1:HALO, :], prev[HALO - 2:HALO - 1, :]
    um1 = jnp.where(row == 0, last, pltpu.roll(u, 1, 0))
    um2 = pltpu.roll(u, 2, 0)
    um2 = jnp.where(row == 0, before, jnp.where(row == 1, last, um2))
    return um1, um2


def _f32(ref):
    return ref[...].astype(F32)


def _conv_fwd(p, conv_w, conv_b, *, name, tm=512):
    s = p.shape[0]
    c = conv_w.shape[1]
    tm = _tile(s, tm)
    hb = tm // HALO

    def body(bg_ref, cg_ref, xa_ref, cgp_ref, xap_ref, w_ref, b_ref, y_ref):
        first = pl.program_id(0) == 0
        u = _f32(cg_ref) * _f32(xa_ref)
        prev = jnp.where(first, 0.0, _f32(cgp_ref) * _f32(xap_ref))
        um1, um2 = _conv_taps(u, prev)
        w = w_ref[...]
        y = b_ref[...] + w[0:1, :] * um2 + w[1:2, :] * um1 + w[2:3, :] * u
        y_ref[...] = _mx(_f32(bg_ref) * y)

    halo = lambda col: pl.BlockSpec((HALO, c), lambda i: (jnp.maximum(i * hb - 1, 0), col))
    return pl.pallas_call(
        body,
        name=name,
        grid=(s // tm,),
        in_specs=[
            pl.BlockSpec((tm, c), lambda i: (i, 0)),
            pl.BlockSpec((tm, c), lambda i: (i, 1)),
            pl.BlockSpec((tm, c), lambda i: (i, 2)),
            halo(1),
            halo(2),
            pl.BlockSpec((CONV_K, c), lambda i: (0, 0)),
            pl.BlockSpec((1, c), lambda i: (0, 0)),
        ],
        out_specs=pl.BlockSpec((tm, c), lambda i: (i, 0)),
        out_shape=jax.ShapeDtypeStruct((s, c), MXU_DTYPE),
        compiler_params=_params(),
    )(p, p, p, p, p, conv_w, conv_b)


def _layernorm_stats(x):
    mu = jnp.mean(x, axis=-1, keepdims=True)
    xc = x - mu
    r = lax.rsqrt(jnp.mean(xc * xc, axis=-1, keepdims=True) + EPS)
    return xc * r, r


def _sgu_fwd(p, ln_g, ln_b, wtril, bias_full, *, col0, name, tm=512):
    s = p.shape[0]
    c = ln_g.shape[1]
    groups = c // CHUNK
    tm = _tile(s, tm)

    def body(u_ref, v_ref, g_ref, b_ref, w_ref, bias_ref, y_ref):
        vn, _ = _layernorm_stats(_gelu(_f32(v_ref)))
        vb = _mx(vn * g_ref[...] + b_ref[...])
        for n in range(tm // CHUNK):
            rows = slice(n * CHUNK, (n + 1) * CHUNK)
            for gi in range(groups):
                cols = slice(gi * CHUNK, (gi + 1) * CHUNK)
                mixed = _dot(w_ref[gi], vb[rows, cols]) + bias_ref[:, cols]
                y_ref[rows, cols] = _mx(_gelu(u_ref[rows, cols].astype(F32)) * mixed)

    return pl.pallas_call(
        body,
        name=name,
        grid=(s // tm,),
        in_specs=[
            pl.BlockSpec((tm, c), lambda i: (i, col0)),
            pl.BlockSpec((tm, c), lambda i: (i, col0 + 1)),
            pl.BlockSpec((1, c), lambda i: (0, 0)),
            pl.BlockSpec((1, c), lambda i: (0, 0)),
            pl.BlockSpec((groups, CHUNK, CHUNK), lambda i: (0, 0, 0)),
            pl.BlockSpec((CHUNK, c), lambda i: (0, 0)),
        ],
        out_specs=pl.BlockSpec((tm, c), lambda i: (i, 0)),
        out_shape=jax.ShapeDtypeStruct((s, c), MXU_DTYPE),
        compiler_params=_params(),
    )(p, p, ln_g, ln_b, wtril, bias_full)


def _head_mean(v, headavg):
    return _dot_split_r(v, headavg)


def _qkv_prep(p, qg, kg, headavg, *, col0, name, tm=512):
    s = p.shape[0]
    c = qg.shape[1]
    t = ATT_BLOCK
    tm = _tile(s, tm)
    assert tm % t == 0

    def body(q_ref, k_ref, v_ref, qg_ref, kg_ref, avg_ref, qc_ref, kcb_ref, vcb_ref):
        normed = []
        for src, g_ref in ((q_ref, qg_ref), (k_ref, kg_ref)):
            xf = _f32(src)
            r = lax.rsqrt(_head_mean(xf * xf, avg_ref[...]) + EPS)
            normed.append(xf * r * g_ref[...])
        q_t, k_t, v_t = (_mx(a.T) for a in (*normed, _f32(v_ref)))
        for hh in range(HEADS):
            rows = slice(hh * HEAD_DIM, (hh + 1) * HEAD_DIM)
            qc_ref[hh] = q_t[rows, :]
            for b in range(tm // t):
                kcb_ref[hh, b] = k_t[rows, b * t:(b + 1) * t]
                vcb_ref[hh, b] = v_t[rows, b * t:(b + 1) * t]

    blk = lambda col: pl.BlockSpec((tm, c), lambda i: (i, col))
    vec = pl.BlockSpec((1, c), lambda i: (0, 0))
    blocked = pl.BlockSpec((HEADS, tm // t, HEAD_DIM, t), lambda i: (0, i, 0, 0))
    return pl.pallas_call(
        body,
        name=name,
        grid=(s // tm,),
        in_specs=[blk(col0), blk(col0 + 1), blk(col0 + 2), vec, vec, pl.BlockSpec((c, c), lambda i: (0, 0))],
        out_specs=[pl.BlockSpec((HEADS, HEAD_DIM, tm), lambda i: (0, 0, i)), blocked, blocked],
        out_shape=[jax.ShapeDtypeStruct((HEADS, HEAD_DIM, s), MXU_DTYPE)]
        + [jax.ShapeDtypeStruct((HEADS, s // t, HEAD_DIM, t), MXU_DTYPE)] * 2,
        compiler_params=_params(),
    )(p, p, p, qg, kg, headavg)


def _att_mask(t, key0, qry0):
    kpos = key0 + lax.broadcasted_iota(jnp.int32, (t, t), 0)
    qpos = qry0 + lax.broadcasted_iota(jnp.int32, (t, t), 1)
    return kpos < qpos


def _att_blocks(k_blks, q_ts, scale, mask):
    zs = [(_dot_tn(k, q) if k.shape[0] == q.shape[0] else _dot(k, q)) * scale for k, q in zip(k_blks, q_ts)]
    es = [jnp.exp(-jnp.abs(z)) for z in zs]
    lbs = [jnp.minimum(z, 0.0) - jnp.log1p(e) for z, e in zip(zs, es)]
    l1ms = [jnp.where(mask, lb - z, 0.0) for lb, z in zip(lbs, zs)]
    return zs, es, lbs, l1ms


def _dot_split_each(const, vs):
    his = [_mx(v) for v in vs]
    los = [_mx(v - hi.astype(F32)) for v, hi in zip(vs, his)]
    tops = [_dot(const, hi) for hi in his]
    return [top + _dot(const, lo) for top, lo in zip(tops, los)]


def _max_over(runs):
    m = runs[0]
    for r in runs[1:]:
        m = jnp.maximum(m, r)
    return jnp.max(m)


NOT_SEEN = -1e30


def _row_put(ref, g, j, row):
    j8 = pl.multiple_of((j // 8) * 8, 8)
    blk = ref[g, pl.ds(j8, 8), :]
    sub = lax.broadcasted_iota(jnp.int32, blk.shape, 0)
    ref[g, pl.ds(j8, 8), :] = jnp.where(sub == j - j8, row, blk)


def _row_get(ref, g, j):
    j8 = pl.multiple_of((j // 8) * 8, 8)
    blk = ref[g, pl.ds(j8, 8), :]
    sub = lax.broadcasted_iota(jnp.int32, blk.shape, 0)
    return jnp.sum(jnp.where(sub == j - j8, blk, 0.0), axis=0, keepdims=True)


def _att_fwd(k_hm, qt_hm, vt_hm, later, *, name, hg=8):
    h, nb, d, t = k_hm.shape
    s = nb * t
    nbp = -(-nb // 8) * 8
    scale = 1.0 / math.sqrt(d)
    assert h % hg == 0

    def body(k_ref, qt_ref, vt_ref, later_ref, o_ref, runs_ref):
        i = pl.program_id(1)
        q_ts = [qt_ref[g] for g in range(hg)]
        runs_ref[...] = jnp.full(runs_ref.shape, NOT_SEEN, F32)

        def cond(carry):
            j, _, _, rmax = carry
            return jnp.logical_and(j >= 0, rmax > EXP_UNDERFLOW)

        def step(carry):
            j, runs, accs, _ = carry
            mask = _att_mask(t, j * t, i * t)
            heads = range(hg)
            for g in heads:
                _row_put(runs_ref, g, j, runs[g])
            _, _, lbs, l1ms = _att_blocks([k_ref[g, j] for g in heads], q_ts, scale, mask)
            afters = _dot_split_each(later_ref[...], l1ms)
            weights = [_mx(jnp.where(mask, jnp.exp(lbs[g] + afters[g] + runs[g]), 0.0)) for g in heads]
            new_accs = [accs[g] + _dot(vt_ref[g, j], weights[g]) for g in heads]
            new_runs = [runs[g] + jnp.sum(l1ms[g], axis=0, keepdims=True) for g in heads]
            return j - 1, tuple(new_runs), tuple(new_accs), _max_over(new_runs)

        init = (i, tuple(jnp.zeros((1, t), F32) for _ in range(hg)), tuple(jnp.zeros((d, t), F32) for _ in range(hg)),
                jnp.float32(0.0))
        _, _, accs, _ = lax.while_loop(cond, step, init)
        for g in range(hg):
            o_ref[g] = _mx(accs[g])

    return pl.pallas_call(
        body,
        name=name,
        grid=(h // hg, nb),
        in_specs=[
            pl.BlockSpec((hg, nb, d, t), lambda hh, i: (hh, 0, 0, 0)),
            pl.BlockSpec((hg, d, t), lambda hh, i: (hh, 0, i)),
            pl.BlockSpec((hg, nb, d, t), lambda hh, i: (hh, 0, 0, 0)),
            pl.BlockSpec((t, t), lambda hh, i: (0, 0)),
        ],
        out_specs=[pl.BlockSpec((hg, d, t), lambda hh, i: (hh, 0, i)),
                   pl.BlockSpec((hg, None, nbp, t), lambda hh, i: (hh, i, 0, 0))],
        out_shape=[jax.ShapeDtypeStruct((h, d, s), MXU_DTYPE), jax.ShapeDtypeStruct((h, nb, nbp, t), F32)],
        compiler_params=_params(),
    )(k_hm, qt_hm, vt_hm, later)


def _merge_fwd(ys, p, b_gate, w_bo, w_o, l, x, *, gate_col0, name, tm=512):
    s, d = x.shape
    _, nbr, c, _ = w_bo.shape
    tm = _tile(s, tm)

    def body(ya_ref, yb_ref, yc_ref, g0_ref, g1_ref, g2_ref, bg_ref, wbo_ref, wo_ref, x_ref, x1_ref, m_ref):
        merged = jnp.zeros((tm, d), F32)
        for i, (y_ref, g_ref) in enumerate(((ya_ref, g0_ref), (yb_ref, g1_ref), (yc_ref, g2_ref))):
            gate = _sigmoid(_f32(g_ref) + bg_ref[:, i * d:(i + 1) * d])
            merged = merged + gate * _dot(y_ref[...], wbo_ref[i])
        mb = _mx(merged)
        m_ref[...] = mb
        x1_ref[...] = x_ref[...] + _dot(mb, wo_ref[...])

    yblk = pl.BlockSpec((tm, c), lambda i: (i, 0))
    gblk = lambda k: pl.BlockSpec((tm, d), lambda i: (i, gate_col0 + k))
    xblk = pl.BlockSpec((tm, d), lambda i: (i, 0))
    return pl.pallas_call(
        body,
        name=name,
        grid=(s // tm,),
        in_specs=[
            yblk, yblk, yblk, gblk(0), gblk(1), gblk(2),
            pl.BlockSpec((1, nbr * d), lambda i: (0, 0)),
            _resident((None, nbr, c, d), lambda i: (l, 0, 0, 0)),
            _resident((None, d, d), lambda i: (l, 0, 0)),
            xblk,
        ],
        out_specs=[xblk, xblk],
        out_shape=[jax.ShapeDtypeStruct((s, d), F32), jax.ShapeDtypeStruct((s, d), MXU_DTYPE)],
        compiler_params=_params(),
    )(*ys, p, p, p, b_gate, w_bo, w_o, x)


def _ffn_down(gu, w_down, l, x, *, name, tm=512):
    s, d = x.shape
    f = w_down.shape[1]
    tm = _tile(s, tm)

    def body(g_ref, u_ref, w_ref, x_ref, o_ref):
        g = _f32(g_ref)
        act = _mx(g * _sigmoid(g) * _f32(u_ref))
        o_ref[...] = x_ref[...] + _dot(act, w_ref[...])

    return pl.pallas_call(
        body,
        name=name,
        grid=(s // tm,),
        in_specs=[
            pl.BlockSpec((tm, f), lambda i: (i, 0)),
            pl.BlockSpec((tm, f), lambda i: (i, 1)),
            _resident((None, f, d), lambda i: (l, 0, 0)),
            pl.BlockSpec((tm, d), lambda i: (i, 0)),
        ],
        out_specs=pl.BlockSpec((tm, d), lambda i: (i, 0)),
        out_shape=jax.ShapeDtypeStruct((s, d), F32),
        compiler_params=_params(),
    )(gu, gu, w_down, x)


def _loss_head(y, target, *, name, tm=512):
    s, d = y.shape
    tm = _tile(s, tm)

    def body(y_ref, t_ref, l_ref, dy_ref):
        @pl.when(pl.program_id(0) == 0)
        def _():
            l_ref[...] = jnp.zeros_like(l_ref)

        err = y_ref[...] - t_ref[...]
        dy_ref[...] = err * (1.0 / d)
        sq = _row_fold(err * err)
        part = sq[:, 0:128]
        for k in range(1, d // 128):
            part = part + sq[:, k * 128:(k + 1) * 128]
        l_ref[...] += part * (0.5 / d)

    blk = pl.BlockSpec((tm, d), lambda i: (i, 0))
    return pl.pallas_call(
        body,
        name=name,
        grid=(s // tm,),
        in_specs=[blk, blk],
        out_specs=[pl.BlockSpec((8, 128), lambda i: (0, 0)), blk],
        out_shape=[jax.ShapeDtypeStruct((8, 128), F32), jax.ShapeDtypeStruct((s, d), F32)],
        compiler_params=_params(),
    )(y, target)


def _matmul_tn(a, b, slot, n_slots, into, *, name, t1=1024, tn=None, tm=1024, col0=0, n_total=None):
    m, k1 = a.shape
    n = b.shape[1]
    n_total = n_total or n
    t1 = _tile(k1, t1)
    tn = tn or n
    tm = _tile(m, tm)
    steps = m // tm
    assert n % tn == 0 and col0 % tn == 0
    cb0 = col0 // tn

    def body(a_ref, b_ref, *refs):
        o_ref = refs[-1]

        @pl.when(pl.program_id(2) == 0)
        def _():
            o_ref[...] = jnp.zeros_like(o_ref)

        o_ref[...] += _dot_tn(a_ref[...], b_ref[...])

    return pl.pallas_call(
        body,
        name=name,
        grid=(k1 // t1, n // tn, steps),
        in_specs=[
            pl.BlockSpec((tm, t1), lambda i, j, k: (k, i)),
            pl.BlockSpec((tm, tn), lambda i, j, k: (k, j)),
        ] + ([] if into is None else [ANY]),
        out_specs=pl.BlockSpec((None, t1, tn), lambda i, j, k: (slot, i, cb0 + j)),
        out_shape=jax.ShapeDtypeStruct((n_slots, k1, n_total), F32),
        input_output_aliases={} if into is None else {2: 0},
        compiler_params=_params(),
    )(a, b, *([] if into is None else [into]))


def _matmul_nt_normbwd(pieces, w, l, x, g, dres, *, name, tm=512, tk=1024):
    s, d = x.shape
    k = w.shape[2]
    tm = _tile(s, tm)
    n = len(pieces)
    assert sum(pc.shape[1] for pc in pieces) == k

    def body(*refs):
        piece_refs = refs[:n]
        w_ref, x_ref, g_ref, dres_ref, dx_ref, dg_ref = refs[n:]

        @pl.when(pl.program_id(0) == 0)
        def _():
            dg_ref[...] = jnp.zeros_like(dg_ref)

        dh = jnp.zeros((tm, d), F32)
        off = 0
        for pc, pc_ref in zip(pieces, piece_refs):
            width = pc.shape[1]
            step = _tile(width, tk)
            for c0 in range(0, width, step):
                dh = dh + _dot_nt(pc_ref[:, c0:c0 + step], w_ref[:, off + c0:off + c0 + step])
            off += width
        xf = x_ref[...]
        r = lax.rsqrt(jnp.mean(xf * xf, axis=-1, keepdims=True) + EPS)
        y = xf * r
        dy = dh * g_ref[...]
        dx_ref[...] = dres_ref[...] + r * (dy - y * jnp.mean(dy * y, axis=-1, keepdims=True))
        dg_ref[...] += _row_fold(dh * y)

    xblk = pl.BlockSpec((tm, d), lambda i: (i, 0))
    return pl.pallas_call(
        body,
        name=name,
        grid=(s // tm,),
        in_specs=[pl.BlockSpec((tm, pc.shape[1]), lambda i: (i, 0)) for pc in pieces] + [
            _resident((None, d, k), lambda i: (l, 0, 0)),
            xblk,
            pl.BlockSpec((1, d), lambda i: (0, 0)),
            xblk,
        ],
        out_specs=[xblk, pl.BlockSpec((8, d), lambda i: (0, 0))],
        out_shape=[jax.ShapeDtypeStruct((s, d), F32), jax.ShapeDtypeStruct((8, d), F32)],
        compiler_params=_params(),
    )(*pieces, w, x, g, dres)


def _ffn_bwd(dx, gu, w_down, l, *, name, tm=512):
    s, d = dx.shape
    f = w_down.shape[1]
    tm = _tile(s, tm)

    def body(dx_ref, g_ref, u_ref, w_ref, dgu_ref, act_ref, dxb_ref):
        dxb = _mx(dx_ref[...])
        dxb_ref[...] = dxb
        dact = _dot_nt(dxb, w_ref[...])
        g = _f32(g_ref)
        u = _f32(u_ref)
        sg = _sigmoid(g)
        silu = g * sg
        act_ref[...] = _mx(silu * u)
        dgu_ref[:, 0:f] = _mx(dact * u * (sg * (1.0 + g * (1.0 - sg))))
        dgu_ref[:, f:2 * f] = _mx(dact * silu)

    fblk = lambda col: pl.BlockSpec((tm, f), lambda i: (i, col))
    dblk = pl.BlockSpec((tm, d), lambda i: (i, 0))
    return pl.pallas_call(
        body,
        name=name,
        grid=(s // tm,),
        in_specs=[dblk, fblk(0), fblk(1), _resident((None, f, d), lambda i: (l, 0, 0))],
        out_specs=[pl.BlockSpec((tm, 2 * f), lambda i: (i, 0)), fblk(0), dblk],
        out_shape=[
            jax.ShapeDtypeStruct((s, 2 * f), MXU_DTYPE),
            jax.ShapeDtypeStruct((s, f), MXU_DTYPE),
            jax.ShapeDtypeStruct((s, d), MXU_DTYPE),
        ],
        compiler_params=_params(),
    )(dx, gu, gu, w_down)


def _merge_bwd(dx, ys, p, b_gate, w_bo, w_o, l, *, gate_col0, name, tm=512):
    s, d = dx.shape
    _, nbr, c, _ = w_bo.shape
    tm = _tile(s, tm)

    def body(dx_ref, ya_ref, yb_ref, yc_ref, g0_ref, g1_ref, g2_ref, bg_ref, wbo_ref, wo_ref,
             dgate_ref, dya_ref, dyb_ref, dyc_ref, dd0_ref, dd1_ref, dd2_ref, dxb_ref, dbg_ref):
        @pl.when(pl.program_id(0) == 0)
        def _():
            dbg_ref[...] = jnp.zeros_like(dbg_ref)

        dxb = _mx(dx_ref[...])
        dxb_ref[...] = dxb
        dmerged = _dot_nt(dxb, wo_ref[...])
        branches = ((ya_ref, g0_ref, dya_ref, dd0_ref), (yb_ref, g1_ref, dyb_ref, dd1_ref), (yc_ref, g2_ref, dyc_ref, dd2_ref))
        for i, (y_ref, g_ref, dy_ref, dd_ref) in enumerate(branches):
            cols = slice(i * d, (i + 1) * d)
            gate = _sigmoid(_f32(g_ref) + bg_ref[:, cols])
            yd = _dot(y_ref[...], wbo_ref[i])
            dyd = _mx(dmerged * gate)
            dd_ref[...] = dyd
            dy_ref[...] = _dot_nt(dyd, wbo_ref[i])
            dpre = dmerged * yd * gate * (1.0 - gate)
            dgate_ref[:, cols] = _mx(dpre)
            dbg_ref[:, cols] += _row_fold(dpre)

    yblk = pl.BlockSpec((tm, c), lambda i: (i, 0))
    gblk = lambda k: pl.BlockSpec((tm, d), lambda i: (i, gate_col0 + k))
    dblk = pl.BlockSpec((tm, d), lambda i: (i, 0))
    return pl.pallas_call(
        body,
        name=name,
        grid=(s // tm,),
        in_specs=[
            dblk, yblk, yblk, yblk, gblk(0), gblk(1), gblk(2),
            pl.BlockSpec((1, nbr * d), lambda i: (0, 0)),
            _resident((None, nbr, c, d), lambda i: (l, 0, 0, 0)),
            _resident((None, d, d), lambda i: (l, 0, 0)),
        ],
        out_specs=[pl.BlockSpec((tm, nbr * d), lambda i: (i, 0)), yblk, yblk, yblk, dblk, dblk, dblk, dblk,
                   pl.BlockSpec((8, nbr * d), lambda i: (0, 0))],
        out_shape=[jax.ShapeDtypeStruct((s, nbr * d), MXU_DTYPE)] + [jax.ShapeDtypeStruct((s, c), F32)] * 3
        + [jax.ShapeDtypeStruct((s, d), MXU_DTYPE)] * 4 + [jax.ShapeDtypeStruct((8, nbr * d), F32)],
        compiler_params=_params(),
    )(dx, *ys, p, p, p, b_gate, w_bo, w_o)


def _conv_bwd(p, dya, conv_w, conv_b, *, name, tm=512):
    s = p.shape[0]
    c = conv_w.shape[1]
    tm = _tile(s, tm)
    hb = tm // HALO
    last = s // tm - 1

    def body(bg_ref, cg_ref, xa_ref, cgp_ref, xap_ref, dy_ref, dyn_ref, bgn_ref, w_ref, b_ref,
             dp_ref, dw_ref):
        i = pl.program_id(0)

        @pl.when(i == 0)
        def _():
            dw_ref[...] = jnp.zeros_like(dw_ref)

        cg = _f32(cg_ref)
        xa = _f32(xa_ref)
        u = cg * xa
        prev = jnp.where(i == 0, 0.0, _f32(cgp_ref) * _f32(xap_ref))
        um1, um2 = _conv_taps(u, prev)
        w = w_ref[...]
        y = b_ref[...] + w[0:1, :] * um2 + w[1:2, :] * um1 + w[2:3, :] * u
        dya = dy_ref[...]
        dp_ref[:, 0:c] = _mx(dya * y)
        dyv = dya * _f32(bg_ref)
        nxt = jnp.where(i == last, 0.0, dyn_ref[...] * _f32(bgn_ref))
        row = lax.broadcasted_iota(jnp.int32, dyv.shape, 0)
        dp1 = jnp.where(row == tm - 1, nxt[0:1, :], pltpu.roll(dyv, tm - 1, 0))
        dp2 = pltpu.roll(dyv, tm - 2, 0)
        dp2 = jnp.where(row == tm - 2, nxt[0:1, :], jnp.where(row == tm - 1, nxt[1:2, :], dp2))
        du = w[2:3, :] * dyv + w[1:2, :] * dp1 + w[0:1, :] * dp2
        dp_ref[:, c:2 * c] = _mx(du * xa)
        dp_ref[:, 2 * c:3 * c] = _mx(du * cg)
        dw_ref[0] += _row_fold(dyv * um2)
        dw_ref[1] += _row_fold(dyv * um1)
        dw_ref[2] += _row_fold(dyv * u)
        dw_ref[3] += _row_fold(dyv)

    blk = lambda col: pl.BlockSpec((tm, c), lambda i: (i, col))
    halo = lambda col: pl.BlockSpec((HALO, c), lambda i: (jnp.maximum(i * hb - 1, 0), col))
    nhalo = lambda col: pl.BlockSpec((HALO, c), lambda i: (jnp.minimum((i + 1) * hb, s // HALO - 1), col))
    return pl.pallas_call(
        body,
        name=name,
        grid=(s // tm,),
        in_specs=[blk(0), blk(1), blk(2), halo(1), halo(2), blk(0), nhalo(0), nhalo(0),
                  pl.BlockSpec((CONV_K, c), lambda i: (0, 0)), pl.BlockSpec((1, c), lambda i: (0, 0))],
        out_specs=[pl.BlockSpec((tm, 3 * c), lambda i: (i, 0)), pl.BlockSpec((4, 8, c), lambda i: (0, 0, 0))],
        out_shape=[jax.ShapeDtypeStruct((s, 3 * c), MXU_DTYPE), jax.ShapeDtypeStruct((4, 8, c), F32)],
        compiler_params=_params(),
    )(p, p, p, p, p, dya, dya, p, conv_w, conv_b)


def _sgu_bwd(p, dyb, ln_g, ln_b, wtril, wtril_t, bias_full, *, col0, name, tm=512):
    s = p.shape[0]
    c = ln_g.shape[1]
    groups = c // CHUNK
    tm = _tile(s, tm)

    def body(u_ref, v_ref, dy_ref, g_ref, b_ref, w_ref, wt_ref, bias_ref, duv_ref, dln_ref, dw_ref, dbias_ref, dvn_ref):
        @pl.when(pl.program_id(0) == 0)
        def _():
            dln_ref[...] = jnp.zeros_like(dln_ref)
            dw_ref[...] = jnp.zeros_like(dw_ref)
            dbias_ref[...] = jnp.zeros_like(dbias_ref)

        sv = _f32(v_ref)
        xhat, r = _layernorm_stats(_gelu(sv))
        vb = _mx(xhat * g_ref[...] + b_ref[...])
        for n in range(tm // CHUNK):
            rows = slice(n * CHUNK, (n + 1) * CHUNK)
            for gi in range(groups):
                cols = slice(gi * CHUNK, (gi + 1) * CHUNK)
                su = u_ref[rows, cols].astype(F32)
                dy = dy_ref[rows, cols]
                vblk = vb[rows, cols]
                mixed = _dot(w_ref[gi], vblk) + bias_ref[:, cols]
                duv_ref[rows, cols] = _mx(dy * mixed * _gelu_grad(su))
                dmixed = dy * _gelu(su)
                dmb = _mx(dmixed)
                dvn_ref[rows, cols] = _dot(wt_ref[gi], dmb)
                dw_ref[gi] += _dot_nt(dmb, vblk)
                dbias_ref[:, cols] += dmixed
        dvn = dvn_ref[...]
        dln_ref[0] += _row_fold(dvn * xhat)
        dln_ref[1] += _row_fold(dvn)
        dxh = dvn * g_ref[...]
        dgv = r * (dxh - jnp.mean(dxh, axis=-1, keepdims=True) - xhat * jnp.mean(dxh * xhat, axis=-1, keepdims=True))
        duv_ref[:, c:2 * c] = _mx(dgv * _gelu_grad(sv))

    blk = lambda col: pl.BlockSpec((tm, c), lambda i: (i, col))
    vec = pl.BlockSpec((1, c), lambda i: (0, 0))
    wspec = pl.BlockSpec((groups, CHUNK, CHUNK), lambda i: (0, 0, 0))
    return pl.pallas_call(
        body,
        name=name,
        grid=(s // tm,),
        in_specs=[blk(col0), blk(col0 + 1), blk(0), vec, vec, wspec, wspec, pl.BlockSpec((CHUNK, c), lambda i: (0, 0))],
        out_specs=[pl.BlockSpec((tm, 2 * c), lambda i: (i, 0)), pl.BlockSpec((2, 8, c), lambda i: (0, 0, 0)), wspec,
                   pl.BlockSpec((CHUNK, c), lambda i: (0, 0))],
        out_shape=[jax.ShapeDtypeStruct((s, 2 * c), MXU_DTYPE),
                   jax.ShapeDtypeStruct((2, 8, c), F32), jax.ShapeDtypeStruct((groups, CHUNK, CHUNK), F32),
                   jax.ShapeDtypeStruct((CHUNK, c), F32)],
        scratch_shapes=[pltpu.VMEM((tm, c), F32)],
        compiler_params=_params(),
    )(p, p, dyb, ln_g, ln_b, wtril, wtril_t, bias_full)


def _att_bwd(kt_hm, vt_hm, qt_hm, dot_hm, runs, later, earlier, *, name, hg=4):
    h, nb, d, t = kt_hm.shape
    s = nb * t
    nbp = runs.shape[2]
    scale = 1.0 / math.sqrt(d)
    assert h % hg == 0

    def body(kt_ref, vt_ref, qt_ref, dot_ref, runs_ref, later_ref, earlier_ref, dqt_ref, dkt_ref, dvt_ref):
        i = pl.program_id(1)

        @pl.when(i == 0)
        def _():
            dkt_ref[...] = jnp.zeros_like(dkt_ref)
            dvt_ref[...] = jnp.zeros_like(dvt_ref)

        q_ts = [qt_ref[g] for g in range(hg)]
        do_ts = [dot_ref[g] for g in range(hg)]

        best = runs_ref[0]
        for g in range(1, hg):
            best = jnp.maximum(best, runs_ref[g])
        row = lax.broadcasted_iota(jnp.int32, (nbp, 1), 0)
        counts = jnp.logical_and(jnp.max(best, axis=1, keepdims=True) > EXP_UNDERFLOW, row < i)
        seen = jnp.sum(counts.astype(jnp.int32))
        zeros_row = tuple(jnp.zeros((1, t), F32) for _ in range(hg))

        def up(j, carry):
            gsums, dqts = carry
            mask = _att_mask(t, j * t, i * t)
            heads = range(hg)
            zs, es, lbs, l1ms = _att_blocks([kt_ref[g, j] for g in heads], q_ts, scale, mask)
            afters = _dot_split_each(later_ref[...], l1ms)
            das = [_dot_tn(vt_ref[g, j], do_ts[g]) for g in heads]
            weights = [jnp.where(mask, jnp.exp(lbs[g] + afters[g] + _row_get(runs_ref, g, j)), 0.0) for g in heads]
            grs = [das[g] * weights[g] for g in heads]
            gbefores = _dot_split_each(earlier_ref[...], grs)
            dzs = []
            for g in heads:
                inv = 1.0 / (1.0 + es[g])
                pos = zs[g] >= 0.0
                beta = jnp.where(pos, inv, es[g] * inv)
                omb = jnp.where(pos, es[g] * inv, inv)
                dzs.append(_mx(jnp.where(mask, grs[g] * omb - (gbefores[g] + gsums[g]) * beta, 0.0) * scale))
            new_dqts = [dqts[g] + _dot(kt_ref[g, j], dzs[g]) for g in heads]
            for g in heads:
                dkt_ref[g, j] += _dot_nt(q_ts[g], dzs[g])
            for g in heads:
                dvt_ref[g, j] += _dot_nt(do_ts[g], _mx(weights[g]))
            new_gsums = [gsums[g] + jnp.sum(grs[g], axis=0, keepdims=True) for g in heads]
            return tuple(new_gsums), tuple(new_dqts)

        _, dqts = lax.fori_loop(i - seen, i + 1, up, (zeros_row, tuple(jnp.zeros((d, t), F32) for _ in range(hg))))
        for g in range(hg):
            dqt_ref[g] = dqts[g]

    whole = pl.BlockSpec((hg, nb, d, t), lambda hh, i: (hh, 0, 0, 0))
    cols = pl.BlockSpec((hg, d, t), lambda hh, i: (hh, 0, i))
    tri = pl.BlockSpec((t, t), lambda hh, i: (0, 0))
    return pl.pallas_call(
        body,
        name=name,
        grid=(h // hg, nb),
        in_specs=[whole, whole, cols, cols, pl.BlockSpec((hg, None, nbp, t), lambda hh, i: (hh, i, 0, 0)), tri, tri],
        out_specs=[cols, whole, whole],
        out_shape=[jax.ShapeDtypeStruct((h, d, s), F32), jax.ShapeDtypeStruct((h, nb, d, t), F32),
                   jax.ShapeDtypeStruct((h, nb, d, t), F32)],
        compiler_params=_params(),
    )(kt_hm, vt_hm, qt_hm, dot_hm, runs, later, earlier)


def _qkv_bwd(p, dqt, dkt, dvt, qg, kg, headavg, *, col0, name, tm=512):
    s = p.shape[0]
    c = qg.shape[1]
    tm = _tile(s, tm)

    t = ATT_BLOCK
    assert tm % t == 0

    def rows_of(blocked_ref):
        parts = []
        for b in range(tm // t):
            cols = jnp.concatenate([blocked_ref[hh, b] for hh in range(HEADS)], axis=0)
            parts.append(cols.T)
        return jnp.concatenate(parts, axis=0)

    def body(q_ref, k_ref, dqt_ref, dkt_ref, dvt_ref, qg_ref, kg_ref, avg_ref, dqkv_ref, dg_ref):
        @pl.when(pl.program_id(0) == 0)
        def _():
            dg_ref[...] = jnp.zeros_like(dg_ref)

        dqn = jnp.concatenate([dqt_ref[hh] for hh in range(HEADS)], axis=0).T
        for n, (src, dn, g_ref) in enumerate(((q_ref, dqn, qg_ref), (k_ref, rows_of(dkt_ref), kg_ref))):
            xf = _f32(src)
            r = lax.rsqrt(_head_mean(xf * xf, avg_ref[...]) + EPS)
            y = xf * r
            dy = dn * g_ref[...]
            dqkv_ref[:, n * c:(n + 1) * c] = _mx(r * (dy - y * _head_mean(dy * y, avg_ref[...])))
            dg_ref[n] += _row_fold(dn * y)
        dqkv_ref[:, 2 * c:3 * c] = _mx(rows_of(dvt_ref))

    blk = lambda col: pl.BlockSpec((tm, c), lambda i: (i, col))
    vec = pl.BlockSpec((1, c), lambda i: (0, 0))
    blocked = pl.BlockSpec((HEADS, tm // t, HEAD_DIM, t), lambda i: (0, i, 0, 0))
    return pl.pallas_call(
        body,
        name=name,
        grid=(s // tm,),
        in_specs=[blk(col0), blk(col0 + 1), pl.BlockSpec((HEADS, HEAD_DIM, tm), lambda i: (0, 0, i)), blocked, blocked,
                  vec, vec, pl.BlockSpec((c, c), lambda i: (0, 0))],
        out_specs=[pl.BlockSpec((tm, 3 * c), lambda i: (i, 0)), pl.BlockSpec((2, 8, c), lambda i: (0, 0, 0))],
        out_shape=[jax.ShapeDtypeStruct((s, 3 * c), MXU_DTYPE), jax.ShapeDtypeStruct((2, 8, c), F32)],
        compiler_params=_params(),
    )(p, p, dqt, dkt, dvt, qg, kg, headavg)


def _place():
    x, y, c = lax.axis_index("x"), lax.axis_index("y"), lax.axis_index("c")
    chips = [(1 - x, y), (x, 1 - y), (1 - x, 1 - y)]
    return x, y, c, chips


def _shard_of(ref, axis, chip, width):
    idx = [slice(None)] * len(ref.shape)
    idx[axis] = pl.ds(chip * width, width)
    return ref.at[tuple(idx)]


def _place_shard(w, axis, chip_idx, dtype, *, name):
    layers = w.shape[0]
    rows, cols = _as_rows(w.shape[1:])
    if axis == len(w.shape) - 2:
        tr = _row_tile(rows, cols, 2)
        out_shape = (layers, rows, cols * N_CHIPS)
        out_spec = pl.BlockSpec((None, tr, cols), lambda l, i, j_ref: (l, i, j_ref[0]))
    else:
        assert axis == 0
        tr = _row_tile(rows, cols, 2)
        per = rows // tr
        out_shape = (layers, rows * N_CHIPS, cols)
        out_spec = pl.BlockSpec((None, tr, cols), lambda l, i, j_ref: (l, j_ref[0] * per + i, 0))
    full = list(w.shape)
    full[1 + axis] *= N_CHIPS

    def body(j_ref, w_ref, o_ref):
        o_ref[...] = w_ref[...].astype(dtype)

    out = pl.pallas_call(
        body,
        name=name,
        grid_spec=pltpu.PrefetchScalarGridSpec(
            num_scalar_prefetch=1,
            grid=(layers, rows // tr),
            in_specs=[pl.BlockSpec((None, tr, cols), lambda l, i, j_ref: (l, i, 0))],
            out_specs=out_spec,
        ),
        out_shape=jax.ShapeDtypeStruct(out_shape, dtype),
        compiler_params=_params(),
    )(chip_idx, w.reshape(layers, rows, cols))
    return out.reshape(full)


def _gather_weights(placed, axes, *, name):
    n = len(placed)
    widths = [pa.shape[1 + ax] // N_CHIPS for pa, ax in zip(placed, axes)]

    def body(*refs):
        outs = refs[n:2 * n]
        send_sem, recv_sem, fsend_sem, frecv_sem = refs[2 * n:]
        x, y, c, chips = _place()
        me = 2 * x + y
        sibling = (x, y, 1 - c)

        def slot(a, layer, chip):
            return _shard_of(outs[a].at[layer], axes[a], chip, widths[a])

        def send(a, k):
            return pltpu.make_async_remote_copy(
                src_ref=slot(a, c, me), dst_ref=slot(a, c, me), send_sem=send_sem.at[a, k], recv_sem=recv_sem.at[a, k],
                device_id=(*chips[k], c), device_id_type=MESH)

        def landed(a, k):
            cx, cy = chips[k]
            blk = slot(a, c, 2 * cx + cy)
            return pltpu.make_async_remote_copy(
                src_ref=blk, dst_ref=blk, send_sem=send_sem.at[a, k], recv_sem=recv_sem.at[a, k],
                device_id=(*chips[k], c), device_id_type=MESH)

        def forward(a, k, layer):
            cx, cy = chips[k]
            blk = slot(a, layer, 2 * cx + cy)
            return pltpu.make_async_remote_copy(
                src_ref=blk, dst_ref=blk, send_sem=fsend_sem.at[a, k], recv_sem=frecv_sem.at[a, k],
                device_id=sibling, device_id_type=MESH)

        for a in range(n):
            for k in range(3):
                send(a, k).start()
        for a in range(n):
            for k in range(3):
                landed(a, k).wait_recv()
                forward(a, k, c).start()
        for a in range(n):
            for k in range(3):
                forward(a, k, 1 - c).wait_recv()
        for a in range(n):
            for k in range(3):
                send(a, k).wait_send()
                forward(a, k, c).wait_send()

    return pl.pallas_call(
        body,
        name=name,
        in_specs=[ANY] * n,
        out_specs=[ANY] * n,
        out_shape=[jax.ShapeDtypeStruct(pa.shape, pa.dtype) for pa in placed],
        input_output_aliases={a: a for a in range(n)},
        scratch_shapes=[pltpu.SemaphoreType.DMA((n, 3))] * 4,
    )(*placed)


def _pair_exchange(grads, *, name):
    n = len(grads)

    def body(*refs):
        ins, outs = refs[:n], refs[n:2 * n]
        send_sem, recv_sem = refs[2 * n:]
        x, y, c, _ = _place()
        copies = [
            pltpu.make_async_remote_copy(
                src_ref=ins[a].at[1 - c], dst_ref=outs[a], send_sem=send_sem.at[a], recv_sem=recv_sem.at[a],
                device_id=(x, y, 1 - c), device_id_type=MESH)
            for a in range(n)
        ]
        for cp in copies:
            cp.start()
        for cp in copies:
            cp.wait()

    return pl.pallas_call(
        body,
        name=name,
        in_specs=[ANY] * n,
        out_specs=[ANY] * n,
        out_shape=[jax.ShapeDtypeStruct(g.shape[1:], g.dtype) for g in grads],
        scratch_shapes=[pltpu.SemaphoreType.DMA((n,)), pltpu.SemaphoreType.DMA((n,))],
    )(*grads)


def _chip_scatter(sums, axes, *, name):
    n = len(sums)
    widths = [sm.shape[ax] // N_CHIPS for sm, ax in zip(sums, axes)]
    out_shapes = []
    for sm, ax, w in zip(sums, axes, widths):
        shp = list(sm.shape)
        shp[ax] = w
        out_shapes.append(jax.ShapeDtypeStruct((3, *shp), sm.dtype))

    def body(*refs):
        ins, outs = refs[:n], refs[n:2 * n]
        send_sem, recv_sem = refs[2 * n:]
        x, y, c, chips = _place()
        copies = []
        for a in range(n):
            for k in range(3):
                cx, cy = chips[k]
                copies.append(pltpu.make_async_remote_copy(
                    src_ref=_shard_of(ins[a], axes[a], 2 * cx + cy, widths[a]), dst_ref=outs[a].at[k],
                    send_sem=send_sem.at[a, k], recv_sem=recv_sem.at[a, k], device_id=(cx, cy, c), device_id_type=MESH))
        for cp in copies:
            cp.start()
        for cp in copies:
            cp.wait()

    return pl.pallas_call(
        body,
        name=name,
        in_specs=[ANY] * n,
        out_specs=[ANY] * n,
        out_shape=out_shapes,
        scratch_shapes=[pltpu.SemaphoreType.DMA((n, 3)), pltpu.SemaphoreType.DMA((n, 3))],
    )(*sums)


def _pair_share(finals, *, name):
    n = len(finals)

    def body(*refs):
        outs = refs[n:2 * n]
        send_sem, recv_sem = refs[2 * n:]
        x, y, c, _ = _place()
        for a in range(n):
            pltpu.make_async_remote_copy(
                src_ref=outs[a].at[c], dst_ref=outs[a].at[c], send_sem=send_sem.at[a], recv_sem=recv_sem.at[a],
                device_id=(x, y, 1 - c), device_id_type=MESH).start()
        for a in range(n):
            pltpu.make_async_remote_copy(
                src_ref=outs[a].at[c], dst_ref=outs[a].at[1 - c], send_sem=send_sem.at[a], recv_sem=recv_sem.at[a],
                device_id=(x, y, 1 - c), device_id_type=MESH).wait()

    return pl.pallas_call(
        body,
        name=name,
        in_specs=[ANY] * n,
        out_specs=[ANY] * n,
        out_shape=[jax.ShapeDtypeStruct(f.shape, f.dtype) for f in finals],
        input_output_aliases={a: a for a in range(n)},
        scratch_shapes=[pltpu.SemaphoreType.DMA((n,))] * 2,
    )(*finals)


def _small_view(shape):
    size = math.prod(shape)
    return (size // 128, 128) if size % 1024 == 0 else (shape[0], size // shape[0])


def _all_reduce_small(parts, *, name):
    n = len(parts)
    views = [_small_view(a.shape) for a in parts]

    def body(*refs):
        ins, outs, slots = refs[:n], refs[n:2 * n], refs[2 * n:3 * n]
        send_sem, recv_sem = refs[3 * n:]
        x, y, c, _ = _place()
        me = 4 * x + 2 * y + c
        copies = []
        for a in range(n):
            slots[a][me] = ins[a][...]
            for k in range(1, N_DEV):
                peer = (x ^ (k >> 2), y ^ ((k >> 1) & 1), c ^ (k & 1))
                copies.append(pltpu.make_async_remote_copy(
                    src_ref=ins[a], dst_ref=slots[a].at[me], send_sem=send_sem.at[a, k - 1],
                    recv_sem=recv_sem.at[a, k - 1], device_id=peer, device_id_type=MESH))
        for cp in copies:
            cp.start()
        for a in range(n):
            for k in range(1, N_DEV):
                pltpu.make_async_remote_copy(
                    src_ref=ins[a], dst_ref=slots[a].at[me ^ k], send_sem=send_sem.at[a, k - 1],
                    recv_sem=recv_sem.at[a, k - 1], device_id=(x, y, c), device_id_type=MESH).wait()
        for a in range(n):
            total = slots[a][0]
            for dev in range(1, N_DEV):
                total = total + slots[a][dev]
            outs[a][...] = total

    vmem = pl.BlockSpec(memory_space=pltpu.VMEM)
    outs = pl.pallas_call(
        body,
        name=name,
        in_specs=[vmem] * n,
        out_specs=[vmem] * n,
        out_shape=[jax.ShapeDtypeStruct(view, F32) for view in views],
        scratch_shapes=[pltpu.VMEM((N_DEV, *view), F32) for view in views]
        + [pltpu.SemaphoreType.DMA((n, N_DEV - 1)), pltpu.SemaphoreType.DMA((n, N_DEV - 1))],
        compiler_params=_params(),
    )(*[a.reshape(view) for a, view in zip(parts, views)])
    return [o.reshape(a.shape) for o, a in zip(outs, parts)]


def _as_rows(shape):
    cols = shape[-1]
    return math.prod(shape[:-1]), cols


ELEMENTWISE_VMEM = 24 * 1024 * 1024


def _row_tile(rows, cols, n_arrays, sublanes=8):
    cap = ELEMENTWISE_VMEM // (n_arrays * 2 * 4 * cols)
    best = None
    for t in range(sublanes, min(rows, cap) + 1, sublanes):
        if rows % t == 0:
            best = t
    assert best is not None, (rows, cols)
    return best


def _pair_sum(g, other, c_idx, *, name):
    rows, cols = _as_rows(other.shape)
    tr = _row_tile(rows, cols, 4, sublanes=16)
    g2 = g.reshape(2, rows, cols)

    def body(c_ref, g_ref, o_ref, out_ref, wire_ref):
        total = g_ref[...] + o_ref[...]
        out_ref[...] = total
        wire_ref[...] = total.astype(WIRE_DTYPE)

    blk = pl.BlockSpec((tr, cols), lambda i, c_ref: (i, 0))
    out, wire = pl.pallas_call(
        body,
        name=name,
        grid_spec=pltpu.PrefetchScalarGridSpec(
            num_scalar_prefetch=1,
            grid=(rows // tr,),
            in_specs=[pl.BlockSpec((None, tr, cols), lambda i, c_ref: (c_ref[0], i, 0)), blk],
            out_specs=[blk, blk],
        ),
        out_shape=[jax.ShapeDtypeStruct((rows, cols), F32), jax.ShapeDtypeStruct((rows, cols), WIRE_DTYPE)],
        compiler_params=_params(),
    )(c_idx, g2, other.reshape(rows, cols))
    return out.reshape(other.shape), wire.reshape(other.shape)


def _chip_sum(mine, got, axis, chip_idx, c_idx, *, name):
    shard_shape = got.shape[1:]
    rows, cols = _as_rows(shard_shape)
    tr = _row_tile(rows, cols, 5, sublanes=16)
    if axis == len(mine.shape) - 1:
        m2 = mine.reshape(rows, cols * N_CHIPS)
        mine_spec = pl.BlockSpec((tr, cols), lambda i, j_ref, c_ref: (i, j_ref[0]))
    else:
        assert axis == 0
        m2 = mine.reshape(N_CHIPS, rows, cols)
        mine_spec = pl.BlockSpec((None, tr, cols), lambda i, j_ref, c_ref: (j_ref[0], i, 0))

    def body(j_ref, c_ref, m_ref, got_ref, out_ref):
        out_ref[...] = ((m_ref[...] + got_ref[0].astype(F32)) + got_ref[1].astype(F32)) + got_ref[2].astype(F32)

    out = pl.pallas_call(
        body,
        name=name,
        grid_spec=pltpu.PrefetchScalarGridSpec(
            num_scalar_prefetch=2,
            grid=(rows // tr,),
            in_specs=[mine_spec, pl.BlockSpec((3, tr, cols), lambda i, j_ref, c_ref: (0, i, 0))],
            out_specs=pl.BlockSpec((None, tr, cols), lambda i, j_ref, c_ref: (c_ref[0], i, 0)),
        ),
        out_shape=jax.ShapeDtypeStruct((2, rows, cols), F32),
        compiler_params=_params(),
    )(chip_idx, c_idx, m2, got.reshape(3, rows, cols))
    return out.reshape((2, *shard_shape))


def _adamw_update(w_ref, g_ref, m_ref, v_ref, d_ref, nm_ref, nv_ref):
    c1 = 1.0 / (1.0 - ADAM_B1 ** ADAM_STEP)
    c2 = 1.0 / (1.0 - ADAM_B2 ** ADAM_STEP)
    gg = g_ref[...]
    nm = ADAM_B1 * m_ref[...] + (1.0 - ADAM_B1) * gg
    nv = ADAM_B2 * v_ref[...] + (1.0 - ADAM_B2) * (gg * gg)
    nm_ref[...] = nm
    nv_ref[...] = nv
    d_ref[...] = -ADAM_LR * ((nm * c1) / (jnp.sqrt(nv * c2) + ADAM_EPS) + ADAM_WD * w_ref[...])


def _adamw_small(ws, gs, ms, vs, *, name):
    n = len(ws)
    views = [_small_view(a.shape) for a in ws]

    def body(*refs):
        w_r, g_r, m_r, v_r = (refs[k * n:(k + 1) * n] for k in range(4))
        d_r, nm_r, nv_r = (refs[(4 + k) * n:(5 + k) * n] for k in range(3))
        for a in range(n):
            _adamw_update(w_r[a], g_r[a], m_r[a], v_r[a], d_r[a], nm_r[a], nv_r[a])

    vmem = pl.BlockSpec(memory_space=pltpu.VMEM)
    flat = lambda arrs: [a.reshape(view) for a, view in zip(arrs, views)]
    outs = pl.pallas_call(
        body,
        name=name,
        in_specs=[vmem] * (4 * n),
        out_specs=[vmem] * (3 * n),
        out_shape=[jax.ShapeDtypeStruct(view, F32) for view in views] * 3,
        compiler_params=_params(),
    )(*flat(ws), *flat(gs), *flat(ms), *flat(vs))
    return [tuple(outs[k * n + a].reshape(ws[a].shape) for k in range(3)) for a in range(n)]


def _adamw(w, g, m, v, *, name):
    shape = w.shape
    rows, cols = _as_rows(shape)
    tr = _row_tile(rows, cols, 7)
    body = functools.partial(_adamw_update)

    blk = pl.BlockSpec((tr, cols), lambda i: (i, 0))
    flat = lambda a: a.reshape(rows, cols)
    outs = pl.pallas_call(
        body,
        name=name,
        grid=(rows // tr,),
        in_specs=[blk] * 4,
        out_specs=[blk] * 3,
        out_shape=[jax.ShapeDtypeStruct((rows, cols), F32)] * 3,
        compiler_params=_params(),
    )(flat(w), flat(g), flat(m), flat(v))
    return tuple(o.reshape(shape) for o in outs)


def _cols(a):
    return a.reshape(a.shape[0], HEADS, HEAD_DIM).transpose(1, 2, 0)


def _from_cols(a):
    h, d, s = a.shape
    return a.transpose(2, 0, 1).reshape(s, h * d)


BIG = ("w_in", "w_branch_out", "w_o", "w_gate_up", "w_down")
BIG_AXIS = {"w_in": 1, "w_branch_out": 2, "w_o": 0, "w_gate_up": 1, "w_down": 0}
SMALL = ("mix_norm_g", "b_gate", "conv_w", "conv_b", "sgu_ln_g", "sgu_ln_b", "sgu_w", "sgu_b", "q_norm_g", "k_norm_g",
         "ffn_norm_g")
ORDER = ("mix_norm_g", "w_in", "b_gate", "conv_w", "conv_b", "sgu_ln_g", "sgu_ln_b", "sgu_w", "sgu_b", "q_norm_g",
         "k_norm_g", "w_branch_out", "w_o", "ffn_norm_g", "w_gate_up", "w_down")


def _layer_forward(x, w, l):
    t = ATT_BLOCK
    c = w["conv_b"].shape[1]
    n_in = w["w_in"].shape[2]
    gate_col0 = (n_in - 3 * x.shape[1]) // x.shape[1]
    tag = f"l{l}"
    p, h = _norm_matmul(x, w["mix_norm_g"][l][None], w["w_in"], l, name=f"in_proj_{tag}", tn=n_in // 7)
    ya = _conv_fwd(p, w["conv_w"][l], w["conv_b"][l][None], name=f"conv_{tag}")
    yb = _sgu_fwd(p, w["sgu_ln_g"][l][None], w["sgu_ln_b"][l][None], w["wtril"][l], w["bias_full"][l], col0=3,
                  name=f"sgu_{tag}")
    q_c, k_cb, v_cb = _qkv_prep(p, w["qg"][l], w["kg"][l], w["headavg"], col0=5, name=f"qkv_{tag}")
    out_t, runs = _att_fwd(k_cb, q_c, v_cb, w["later"], name=f"att_{tag}")
    yc = _from_cols(out_t)
    x1, merged = _merge_fwd((ya, yb, yc), p, w["b_gate"][l][None], w["w_branch_out"], w["w_o"], l, x,
                            gate_col0=gate_col0, name=f"merge_{tag}")
    gu, h2 = _norm_matmul(x1, w["ffn_norm_g"][l][None], w["w_gate_up"], l, name=f"gate_up_{tag}",
                          tn=w["w_gate_up"].shape[2] // 4)
    x2 = _ffn_down(gu, w["w_down"], l, x1, name=f"down_{tag}")
    saved = dict(x=x, p=p, h=h, ya=ya, yb=yb, yc=yc, merged=merged, x1=x1, gu=gu, h2=h2,
                 att=(k_cb, v_cb, q_c, runs), gate_col0=gate_col0)
    return x2, saved


def _layer_backward(dx2, w, sv, l, big):
    layers = w["w_in"].shape[0]
    t = ATT_BLOCK
    tag = f"l{l}"
    d = dx2.shape[1]
    c = w["conv_b"].shape[1]
    n_in = w["w_in"].shape[2]
    n_ff = w["w_gate_up"].shape[2]
    g = {}
    dgu, act, dx2b = _ffn_bwd(dx2, sv["gu"], w["w_down"], l, name=f"down_bwd_{tag}")
    big["w_down"] = _matmul_tn(act, dx2b, l, layers, big.get("w_down"), name=f"dw_down_{tag}", t1=act.shape[1] // 2)
    dx1, dg2 = _matmul_nt_normbwd([dgu], w["w_gate_up"], l, sv["x1"], w["ffn_norm_g"][l][None], dx2,
                                  name=f"gate_up_bwd_{tag}")
    g["ffn_norm_g"] = jnp.sum(dg2, axis=0)
    big["w_gate_up"] = _matmul_tn(sv["h2"], dgu, l, layers, big.get("w_gate_up"), name=f"dw_gate_up_{tag}", tn=n_ff // 4)
    ys = (sv["ya"], sv["yb"], sv["yc"])
    (dgates, dya, dyb, dyc, dd0, dd1, dd2, dx1b, dbg) = _merge_bwd(
        dx1, ys, sv["p"], w["b_gate"][l][None], w["w_branch_out"], w["w_o"], l, gate_col0=sv["gate_col0"],
        name=f"merge_bwd_{tag}")
    g["b_gate"] = jnp.sum(dbg, axis=0)
    big["w_o"] = _matmul_tn(sv["merged"], dx1b, l, layers, big.get("w_o"), name=f"dw_o_{tag}")
    for i, (y, dd) in enumerate(zip(ys, (dd0, dd1, dd2))):
        big["w_branch_out"] = _matmul_tn(y, dd, len(ys) * l + i, len(ys) * layers, big.get("w_branch_out"),
                                         name=f"dw_bo{i}_{tag}")
    dconv, dwc = _conv_bwd(sv["p"], dya, w["conv_w"][l], w["conv_b"][l][None], name=f"conv_bwd_{tag}")
    dwc = jnp.sum(dwc, axis=1)
    g["conv_w"] = dwc[0:CONV_K]
    g["conv_b"] = dwc[CONV_K]
    dsgu, dln, dws, dbias = _sgu_bwd(sv["p"], dyb, w["sgu_ln_g"][l][None], w["sgu_ln_b"][l][None], w["wtril"][l],
                                     w["wtril_t"][l], w["bias_full"][l], col0=3, name=f"sgu_bwd_{tag}")
    dln = jnp.sum(dln, axis=1)
    g["sgu_ln_g"], g["sgu_ln_b"] = dln[0], dln[1]
    g["sgu_w"] = jnp.where(w["tril"], dws, 0.0)
    g["sgu_b"] = jnp.sum(dbias.reshape(CHUNK, c // CHUNK, CHUNK), axis=2).T
    k_cb, v_cb, q_c, runs = sv["att"]
    dqt, dkt, dvt = _att_bwd(k_cb, v_cb, q_c, _cols(_mx(dyc)), runs, w["later"], w["earlier"], name=f"att_bwd_{tag}")
    dqkv, dqkg = _qkv_bwd(sv["p"], dqt, dkt, dvt, w["qg"][l], w["kg"][l], w["headavg"], col0=5, name=f"qkv_bwd_{tag}")
    dqkg = jnp.sum(dqkg.reshape(2, 8 * HEADS, HEAD_DIM), axis=1)
    g["q_norm_g"], g["k_norm_g"] = dqkg[0], dqkg[1]
    pieces = [dconv, dsgu, dqkv, dgates]
    dx0, dg1 = _matmul_nt_normbwd(pieces, w["w_in"], l, sv["x"], w["mix_norm_g"][l][None], dx1, name=f"in_proj_bwd_{tag}")
    g["mix_norm_g"] = jnp.sum(dg1, axis=0)
    col0 = 0
    for k, pc in enumerate(pieces):
        width = pc.shape[1]
        big["w_in"] = _matmul_tn(sv["h"], pc, l, layers, big.get("w_in"), name=f"dw_in{k}_{tag}", col0=col0, n_total=n_in,
                                 tn=math.gcd(col0, width) if col0 else width)
        col0 += width
    return dx0, g


def kernel(x, mix_norm_g, w_in, b_gate, conv_w, conv_b, sgu_ln_g, sgu_ln_b, sgu_w, sgu_b, q_norm_g, k_norm_g, w_branch_out, w_o, ffn_norm_g, w_gate_up, w_down, loss_target, m_mix_norm_g, m_w_in, m_b_gate, m_conv_w, m_conv_b, m_sgu_ln_g, m_sgu_ln_b, m_sgu_w, m_sgu_b, m_q_norm_g, m_k_norm_g, m_w_branch_out, m_w_o, m_ffn_norm_g, m_w_gate_up, m_w_down, v_mix_norm_g, v_w_in, v_b_gate, v_conv_w, v_conv_b, v_sgu_ln_g, v_sgu_ln_b, v_sgu_w, v_sgu_b, v_q_norm_g, v_k_norm_g, v_w_branch_out, v_w_o, v_ffn_norm_g, v_w_gate_up, v_w_down):
    given = dict(locals())
    params = {n: given[n] for n in ORDER}
    moms = {n: (given["m_" + n], given["v_" + n]) for n in ORDER}
    layers = mix_norm_g.shape[0]
    assert layers == 2, "the exchanges split the work of a chip's two cores by layer"
    xs = x[0]
    target = loss_target[0]
    chip = 2 * lax.axis_index("x") + lax.axis_index("y")
    core = lax.axis_index("c")

    c_idx = core.reshape(1).astype(jnp.int32)
    j_idx = chip.reshape(1).astype(jnp.int32)
    width = conv_w.shape[2]
    placed = [_place_shard(params[n], BIG_AXIS[n], j_idx, MXU_DTYPE, name=f"place_{n}") for n in BIG]
    placed.append(lax.dynamic_update_slice_in_dim(jnp.zeros((layers, CONV_K, width * N_CHIPS), F32), conv_w, chip * width, 2))
    gathered = _gather_weights(placed, [BIG_AXIS[n] for n in BIG] + [1], name="gather_weights")
    w = dict(zip(BIG + ("conv_w",), gathered))
    for n in ("mix_norm_g", "b_gate", "conv_b", "sgu_ln_g", "sgu_ln_b", "ffn_norm_g"):
        w[n] = params[n]
    groups = sgu_w.shape[1]
    tril = jnp.tril(jnp.ones((CHUNK, CHUNK), dtype=bool))
    w["tril"] = tril
    w["wtril"] = _mx(jnp.where(tril, sgu_w, 0.0))
    w["wtril_t"] = w["wtril"].transpose(0, 1, 3, 2)
    w["bias_full"] = jnp.repeat(sgu_b.transpose(0, 2, 1), CHUNK, axis=2)
    w["qg"] = jnp.tile(q_norm_g, (1, HEADS))[:, None, :]
    w["kg"] = jnp.tile(k_norm_g, (1, HEADS))[:, None, :]
    lane = jnp.arange(HEADS * HEAD_DIM) // HEAD_DIM
    w["headavg"] = _mx(jnp.where(lane[:, None] == lane[None, :], 1.0 / HEAD_DIM, 0.0))
    pos = jnp.arange(ATT_BLOCK)
    w["later"] = _mx(jnp.where(pos[None, :] > pos[:, None], 1.0, 0.0))
    w["earlier"] = _mx(jnp.where(pos[None, :] < pos[:, None], 1.0, 0.0))

    saved = []
    act = xs
    for l in range(layers):
        act, sv = _layer_forward(act, w, l)
        saved.append(sv)
    loss_part, dact = _loss_head(act, target, name="loss_head")
    loss = lax.psum(jnp.sum(loss_part), ("x", "y", "c"))
    grads = [None] * layers
    big = {}
    for l in reversed(range(layers)):
        dact, grads[l] = _layer_backward(dact, w, saved[l], l, big)
    grad_x = dact[None]
    local = {n: jnp.stack([grads[l][n] for l in range(layers)]) for n in SMALL}
    for n in BIG:
        local[n] = big[n].reshape(w[n].shape)

    got = _pair_exchange([local[n] for n in BIG], name="grads_pair_exchange")
    pair = [_pair_sum(local[n], o, c_idx, name=f"pair_sum_{n}") for n, o in zip(BIG, got)]
    landed = _chip_scatter([wire for _, wire in pair], [BIG_AXIS[n] for n in BIG], name="grads_chip_scatter")
    final = [_chip_sum(pr, ld, BIG_AXIS[n], j_idx, c_idx, name=f"chip_sum_{n}") for n, (pr, _), ld in zip(BIG, pair, landed)]
    full = dict(zip(BIG, _pair_share(final, name="grads_pair_share")))

    summed = _all_reduce_small([local[n] for n in SMALL], name="grads_all_reduce_small")
    for n, gsum in zip(SMALL, summed):
        full[n] = gsum
    full["conv_w"] = lax.dynamic_slice_in_dim(full["conv_w"], chip * width, width, axis=2)

    out = {}
    for n in BIG:
        out[n] = _adamw(params[n], full[n], *moms[n], name=f"adamw_{n}")
    small = _adamw_small([params[n] for n in SMALL], [full[n] for n in SMALL], [moms[n][0] for n in SMALL],
                         [moms[n][1] for n in SMALL], name="adamw_small")
    for n, triple in zip(SMALL, small):
        out[n] = triple
    return (loss, grad_x, *[full[n] for n in ORDER], *[out[n][0] for n in ORDER], *[out[n][1] for n in ORDER],
            *[out[n][2] for n in ORDER])
```

```python
import functools
import math
from typing import Callable, NamedTuple

import jax
import jax.numpy as jnp
from jax import lax
from jax.experimental import pallas as pl
from jax.experimental.pallas import tpu as pltpu

F32 = jnp.float32
MXU_DTYPE = jnp.bfloat16
WIRE_DTYPE = jnp.bfloat16
ACT_DTYPE = jnp.bfloat16
HALO = 16

EPS = 1e-6
CONV_K = 3
CHUNK = 128
HEADS = 8
HEAD_DIM = 64
ATT_BLOCK = 128
EXP_UNDERFLOW = -88.0

ADAM_LR = 0.001
ADAM_B1 = 0.9
ADAM_B2 = 0.999
ADAM_EPS = 1e-08
ADAM_WD = 0.01
ADAM_STEP = 10

VMEM_LIMIT = 56 * 1024 * 1024
MESH = pl.DeviceIdType.MESH
N_CHIPS = 4
N_DEV = 8
ANY = pl.BlockSpec(memory_space=pl.ANY)


def _params(**kw):
    return pltpu.CompilerParams(vmem_limit_bytes=VMEM_LIMIT, **kw)


class _Rider(NamedTuple):
    arrays: tuple
    sems: tuple
    start: Callable
    finish: Callable


def _call(body, operands, *, name, grid, in_specs, out_specs, out_shape, scratch_shapes=(), rider=None):
    if rider is None:
        outs = pl.pallas_call(body, name=name, grid=grid, in_specs=in_specs, out_specs=out_specs, out_shape=out_shape,
                              scratch_shapes=list(scratch_shapes), compiler_params=_params())(*operands)
        return tuple(outs), ()
    n_in, n_out, n_scr, k = len(in_specs), len(out_specs), len(scratch_shapes), len(rider.arrays)

    def riding(*refs):
        ins, refs = refs[:n_in], refs[n_in + k:]
        outs, carried, refs = refs[:n_out], refs[n_out:n_out + k], refs[n_out + k:]
        scratch, sems = refs[:n_scr], refs[n_scr:]
        first = functools.reduce(jnp.logical_and, [pl.program_id(ax) == 0 for ax in range(len(grid))])
        last = functools.reduce(jnp.logical_and, [pl.program_id(ax) == grid[ax] - 1 for ax in range(len(grid))])

        @pl.when(first)
        def _():
            rider.start(carried, sems)

        body(*ins, *outs, *scratch)

        @pl.when(last)
        def _():
            rider.finish(carried, sems)

    res = pl.pallas_call(
        riding, name=name, grid=grid, in_specs=[*in_specs, *[ANY] * k], out_specs=[*out_specs, *[ANY] * k],
        out_shape=[*out_shape, *[jax.ShapeDtypeStruct(a.shape, a.dtype) for a in rider.arrays]],
        input_output_aliases={n_in + j: n_out + j for j in range(k)},
        scratch_shapes=[*scratch_shapes, *[pltpu.SemaphoreType.DMA(shape) for shape in rider.sems]],
        compiler_params=_params())(*operands, *rider.arrays)
    return tuple(res[:n_out]), tuple(res[n_out:])


def _ride_alone(riders, *, name):
    arrays = riders[0].arrays
    k = len(arrays)
    counts = [len(r.sems) for r in riders]

    def body(*refs):
        carried, sems = refs[k:2 * k], refs[2 * k:]
        for r, count in zip(riders, counts):
            r.start(carried, sems[:count])
            r.finish(carried, sems[:count])
            sems = sems[count:]

    return pl.pallas_call(
        body, name=name, in_specs=[ANY] * k, out_specs=[ANY] * k,
        out_shape=[jax.ShapeDtypeStruct(a.shape, a.dtype) for a in arrays],
        input_output_aliases={j: j for j in range(k)},
        scratch_shapes=[pltpu.SemaphoreType.DMA(shape) for r in riders for shape in r.sems])(*arrays)


def _mx(v):
    return v.astype(MXU_DTYPE)


def _dot(a, b):
    return lax.dot_general(a, b, (((1,), (0,)), ((), ())), preferred_element_type=F32)


def _dot_nt(a, b):
    return lax.dot_general(a, b, (((1,), (1,)), ((), ())), preferred_element_type=F32)


def _dot_tn(a, b):
    return lax.dot_general(a, b, (((0,), (0,)), ((), ())), preferred_element_type=F32)


def _dot_split(const, v):
    hi = _mx(v)
    lo = _mx(v - hi.astype(F32))
    return _dot(const, hi) + _dot(const, lo)


def _dot_split_r(v, const):
    hi = _mx(v)
    lo = _mx(v - hi.astype(F32))
    return _dot(hi, const) + _dot(lo, const)


def _sigmoid(x):
    return 1.0 / (1.0 + jnp.exp(-x))


_INV_SQRT2 = 1.0 / math.sqrt(2.0)
_INV_SQRT2PI = 1.0 / math.sqrt(2.0 * math.pi)


def _gelu(x):
    return 0.5 * x * (1.0 + lax.erf(x * _INV_SQRT2))


def _gelu_grad(x):
    return 0.5 * (1.0 + lax.erf(x * _INV_SQRT2)) + x * jnp.exp(-0.5 * x * x) * _INV_SQRT2PI


def _row_fold(v):
    m, n = v.shape
    return jnp.sum(v.reshape(m // 8, 8, n), axis=0)


def _tile(m, pref):
    t = min(m, pref)
    while m % t:
        t //= 2
    return t


def _resident(block_shape, index_map):
    return pl.BlockSpec(block_shape, index_map, pipeline_mode=pl.Buffered(1))


def _norm_matmul(x, g, w, l, *, name, tm=512, tn=None, rider=None):
    s, d = x.shape
    n = w.shape[2]
    tm = _tile(s, tm)
    tn = tn or n
    assert n % tn == 0

    def body(x_ref, g_ref, w_ref, p_ref, h_ref):
        xf = x_ref[...]
        r = lax.rsqrt(jnp.mean(xf * xf, axis=-1, keepdims=True) + EPS)
        h = _mx(xf * r * g_ref[...])
        h_ref[...] = h
        for j in range(n // tn):
            cols = slice(j * tn, (j + 1) * tn)
            p_ref[:, cols] = _dot(h, w_ref[:, cols]).astype(ACT_DTYPE)

    return _call(
        body, (x, g, w),
        name=name,
        grid=(s // tm,),
        in_specs=[
            pl.BlockSpec((tm, d), lambda i: (i, 0)),
            pl.BlockSpec((1, d), lambda i: (0, 0)),
            _resident((None, d, n), lambda i: (l, 0, 0)),
        ],
        out_specs=[
            pl.BlockSpec((tm, n), lambda i: (i, 0)),
            pl.BlockSpec((tm, d), lambda i: (i, 0)),
        ],
        out_shape=[jax.ShapeDtypeStruct((s, n), ACT_DTYPE), jax.ShapeDtypeStruct((s, d), MXU_DTYPE)],
        rider=rider,
    )


def _conv_taps(u, prev):
    row = lax.broadcasted_iota(jnp.int32, u.shape, 0)
    last, before = prev[HALO - 1:HALO, :], prev[HALO - 2:HALO - 1, :]
    um1 = jnp.where(row == 0, last, pltpu.roll(u, 1, 0))
    um2 = pltpu.roll(u, 2, 0)
    um2 = jnp.where(row == 0, before, jnp.where(row == 1, last, um2))
    return um1, um2


def _f32(ref):
    return ref[...].astype(F32)


def _conv_fwd(p, conv_w, conv_b, *, name, tm=512):
    s = p.shape[0]
    c = conv_w.shape[1]
    tm = _tile(s, tm)
    hb = tm // HALO

    def body(bg_ref, cg_ref, xa_ref, cgp_ref, xap_ref, w_ref, b_ref, y_ref):
        first = pl.program_id(0) == 0
        u = _f32(cg_ref) * _f32(xa_ref)
        prev = jnp.where(first, 0.0, _f32(cgp_ref) * _f32(xap_ref))
        um1, um2 = _conv_taps(u, prev)
        w = w_ref[...]
        y = b_ref[...] + w[0:1, :] * um2 + w[1:2, :] * um1 + w[2:3, :] * u
        y_ref[...] = _mx(_f32(bg_ref) * y)

    halo = lambda col: pl.BlockSpec((HALO, c), lambda i: (jnp.maximum(i * hb - 1, 0), col))
    return pl.pallas_call(
        body,
        name=name,
        grid=(s // tm,),
        in_specs=[
            pl.BlockSpec((tm, c), lambda i: (i, 0)),
            pl.BlockSpec((tm, c), lambda i: (i, 1)),
            pl.BlockSpec((tm, c), lambda i: (i, 2)),
            halo(1),
            halo(2),
            pl.BlockSpec((CONV_K, c), lambda i: (0, 0)),
            pl.BlockSpec((1, c), lambda i: (0, 0)),
        ],
        out_specs=pl.BlockSpec((tm, c), lambda i: (i, 0)),
        out_shape=jax.ShapeDtypeStruct((s, c), MXU_DTYPE),
        compiler_params=_params(),
    )(p, p, p, p, p, conv_w, conv_b)


def _layernorm_stats(x):
    mu = jnp.mean(x, axis=-1, keepdims=True)
    xc = x - mu
    r = lax.rsqrt(jnp.mean(xc * xc, axis=-1, keepdims=True) + EPS)
    return xc * r, r


def _sgu_fwd(p, ln_g, ln_b, wtril, bias_full, *, col0, name, tm=512):
    s = p.shape[0]
    c = ln_g.shape[1]
    groups = c // CHUNK
    tm = _tile(s, tm)

    def body(u_ref, v_ref, g_ref, b_ref, w_ref, bias_ref, y_ref):
        vn, _ = _layernorm_stats(_gelu(_f32(v_ref)))
        vb = _mx(vn * g_ref[...] + b_ref[...])
        for n in range(tm // CHUNK):
            rows = slice(n * CHUNK, (n + 1) * CHUNK)
            for gi in range(groups):
                cols = slice(gi * CHUNK, (gi + 1) * CHUNK)
                mixed = _dot(w_ref[gi], vb[rows, cols]) + bias_ref[:, cols]
                y_ref[rows, cols] = _mx(_gelu(u_ref[rows, cols].astype(F32)) * mixed)

    return pl.pallas_call(
        body,
        name=name,
        grid=(s // tm,),
        in_specs=[
            pl.BlockSpec((tm, c), lambda i: (i, col0)),
            pl.BlockSpec((tm, c), lambda i: (i, col0 + 1)),
            pl.BlockSpec((1, c), lambda i: (0, 0)),
            pl.BlockSpec((1, c), lambda i: (0, 0)),
            pl.BlockSpec((groups, CHUNK, CHUNK), lambda i: (0, 0, 0)),
            pl.BlockSpec((CHUNK, c), lambda i: (0, 0)),
        ],
        out_specs=pl.BlockSpec((tm, c), lambda i: (i, 0)),
        out_shape=jax.ShapeDtypeStruct((s, c), MXU_DTYPE),
        compiler_params=_params(),
    )(p, p, ln_g, ln_b, wtril, bias_full)


def _head_mean(v, headavg):
    return _dot_split_r(v, headavg)


def _qkv_prep(p, qg, kg, headavg, *, col0, name, tm=512):
    s = p.shape[0]
    c = qg.shape[1]
    t = ATT_BLOCK
    tm = _tile(s, tm)
    assert tm % t == 0

    def body(q_ref, k_ref, v_ref, qg_ref, kg_ref, avg_ref, qc_ref, kcb_ref, vcb_ref):
        normed = []
        for src, g_ref in ((q_ref, qg_ref), (k_ref, kg_ref)):
            xf = _f32(src)
            r = lax.rsqrt(_head_mean(xf * xf, avg_ref[...]) + EPS)
            normed.append(xf * r * g_ref[...])
        q_t, k_t, v_t = (_mx(a.T) for a in (*normed, _f32(v_ref)))
        for hh in range(HEADS):
            rows = slice(hh * HEAD_DIM, (hh + 1) * HEAD_DIM)
            qc_ref[hh] = q_t[rows, :]
            for b in range(tm // t):
                kcb_ref[hh, b] = k_t[rows, b * t:(b + 1) * t]
                vcb_ref[hh, b] = v_t[rows, b * t:(b + 1) * t]

    blk = lambda col: pl.BlockSpec((tm, c), lambda i: (i, col))
    vec = pl.BlockSpec((1, c), lambda i: (0, 0))
    blocked = pl.BlockSpec((HEADS, tm // t, HEAD_DIM, t), lambda i: (0, i, 0, 0))
    return pl.pallas_call(
        body,
        name=name,
        grid=(s // tm,),
        in_specs=[blk(col0), blk(col0 + 1), blk(col0 + 2), vec, vec, pl.BlockSpec((c, c), lambda i: (0, 0))],
        out_specs=[pl.BlockSpec((HEADS, HEAD_DIM, tm), lambda i: (0, 0, i)), blocked, blocked],
        out_shape=[jax.ShapeDtypeStruct((HEADS, HEAD_DIM, s), MXU_DTYPE)]
        + [jax.ShapeDtypeStruct((HEADS, s // t, HEAD_DIM, t), MXU_DTYPE)] * 2,
        compiler_params=_params(),
    )(p, p, p, qg, kg, headavg)


def _att_mask(t, key0, qry0):
    kpos = key0 + lax.broadcasted_iota(jnp.int32, (t, t), 0)
    qpos = qry0 + lax.broadcasted_iota(jnp.int32, (t, t), 1)
    return kpos < qpos


def _att_blocks(k_blks, q_ts, scale, mask):
    zs = [(_dot_tn(k, q) if k.shape[0] == q.shape[0] else _dot(k, q)) * scale for k, q in zip(k_blks, q_ts)]
    es = [jnp.exp(-jnp.abs(z)) for z in zs]
    lbs = [jnp.minimum(z, 0.0) - jnp.log1p(e) for z, e in zip(zs, es)]
    l1ms = [jnp.where(mask, lb - z, 0.0) for lb, z in zip(lbs, zs)]
    return zs, es, lbs, l1ms


def _dot_split_each(const, vs):
    his = [_mx(v) for v in vs]
    los = [_mx(v - hi.astype(F32)) for v, hi in zip(vs, his)]
    tops = [_dot(const, hi) for hi in his]
    return [top + _dot(const, lo) for top, lo in zip(tops, los)]


def _max_over(runs):
    m = runs[0]
    for r in runs[1:]:
        m = jnp.maximum(m, r)
    return jnp.max(m)


NOT_SEEN = -1e30


def _row_put(ref, g, j, row):
    j8 = pl.multiple_of((j // 8) * 8, 8)
    blk = ref[g, pl.ds(j8, 8), :]
    sub = lax.broadcasted_iota(jnp.int32, blk.shape, 0)
    ref[g, pl.ds(j8, 8), :] = jnp.where(sub == j - j8, row, blk)


def _row_get(ref, g, j):
    j8 = pl.multiple_of((j // 8) * 8, 8)
    blk = ref[g, pl.ds(j8, 8), :]
    sub = lax.broadcasted_iota(jnp.int32, blk.shape, 0)
    return jnp.sum(jnp.where(sub == j - j8, blk, 0.0), axis=0, keepdims=True)


def _att_fwd(k_hm, qt_hm, vt_hm, later, *, name, hg=8, rider=None):
    h, nb, d, t = k_hm.shape
    s = nb * t
    nbp = -(-nb // 8) * 8
    scale = 1.0 / math.sqrt(d)
    assert h % hg == 0

    def body(k_ref, qt_ref, vt_ref, later_ref, o_ref, runs_ref):
        i = pl.program_id(1)
        q_ts = [qt_ref[g] for g in range(hg)]
        runs_ref[...] = jnp.full(runs_ref.shape, NOT_SEEN, F32)

        def cond(carry):
            j, _, _, rmax = carry
            return jnp.logical_and(j >= 0, rmax > EXP_UNDERFLOW)

        def step(carry):
            j, runs, accs, _ = carry
            mask = _att_mask(t, j * t, i * t)
            heads = range(hg)
            for g in heads:
                _row_put(runs_ref, g, j, runs[g])
            _, _, lbs, l1ms = _att_blocks([k_ref[g, j] for g in heads], q_ts, scale, mask)
            afters = _dot_split_each(later_ref[...], l1ms)
            weights = [_mx(jnp.where(mask, jnp.exp(lbs[g] + afters[g] + runs[g]), 0.0)) for g in heads]
            new_accs = [accs[g] + _dot(vt_ref[g, j], weights[g]) for g in heads]
            new_runs = [runs[g] + jnp.sum(l1ms[g], axis=0, keepdims=True) for g in heads]
            return j - 1, tuple(new_runs), tuple(new_accs), _max_over(new_runs)

        init = (i, tuple(jnp.zeros((1, t), F32) for _ in range(hg)), tuple(jnp.zeros((d, t), F32) for _ in range(hg)),
                jnp.float32(0.0))
        _, _, accs, _ = lax.while_loop(cond, step, init)
        for g in range(hg):
            o_ref[g] = _mx(accs[g])

    return _call(
        body, (k_hm, qt_hm, vt_hm, later),
        name=name,
        grid=(h // hg, nb),
        in_specs=[
            pl.BlockSpec((hg, nb, d, t), lambda hh, i: (hh, 0, 0, 0)),
            pl.BlockSpec((hg, d, t), lambda hh, i: (hh, 0, i)),
            pl.BlockSpec((hg, nb, d, t), lambda hh, i: (hh, 0, 0, 0)),
            pl.BlockSpec((t, t), lambda hh, i: (0, 0)),
        ],
        out_specs=[pl.BlockSpec((hg, d, t), lambda hh, i: (hh, 0, i)),
                   pl.BlockSpec((hg, None, nbp, t), lambda hh, i: (hh, i, 0, 0))],
        out_shape=[jax.ShapeDtypeStruct((h, d, s), MXU_DTYPE), jax.ShapeDtypeStruct((h, nb, nbp, t), F32)],
        rider=rider,
    )


def _merge_fwd(ys, p, b_gate, w_bo, w_o, l, x, *, gate_col0, name, tm=512):
    s, d = x.shape
    _, nbr, c, _ = w_bo.shape
    tm = _tile(s, tm)

    def body(ya_ref, yb_ref, yc_ref, g0_ref, g1_ref, g2_ref, bg_ref, wbo_ref, wo_ref, x_ref, x1_ref, m_ref):
        merged = jnp.zeros((tm, d), F32)
        for i, (y_ref, g_ref) in enumerate(((ya_ref, g0_ref), (yb_ref, g1_ref), (yc_ref, g2_ref))):
            gate = _sigmoid(_f32(g_ref) + bg_ref[:, i * d:(i + 1) * d])
            merged = merged + gate * _dot(y_ref[...], wbo_ref[i])
        mb = _mx(merged)
        m_ref[...] = mb
        x1_ref[...] = x_ref[...] + _dot(mb, wo_ref[...])

    yblk = pl.BlockSpec((tm, c), lambda i: (i, 0))
    gblk = lambda k: pl.BlockSpec((tm, d), lambda i: (i, gate_col0 + k))
    xblk = pl.BlockSpec((tm, d), lambda i: (i, 0))
    return pl.pallas_call(
        body,
        name=name,
        grid=(s // tm,),
        in_specs=[
            yblk, yblk, yblk, gblk(0), gblk(1), gblk(2),
            pl.BlockSpec((1, nbr * d), lambda i: (0, 0)),
            _resident((None, nbr, c, d), lambda i: (l, 0, 0, 0)),
            _resident((None, d, d), lambda i: (l, 0, 0)),
            xblk,
        ],
        out_specs=[xblk, xblk],
        out_shape=[jax.ShapeDtypeStruct((s, d), F32), jax.ShapeDtypeStruct((s, d), MXU_DTYPE)],
        compiler_params=_params(),
    )(*ys, p, p, p, b_gate, w_bo, w_o, x)


def _ffn_down(gu, w_down, l, x, *, name, tm=512):
    s, d = x.shape
    f = w_down.shape[1]
    tm = _tile(s, tm)

    def body(g_ref, u_ref, w_ref, x_ref, o_ref):
        g = _f32(g_ref)
        act = _mx(g * _sigmoid(g) * _f32(u_ref))
        o_ref[...] = x_ref[...] + _dot(act, w_ref[...])

    return pl.pallas_call(
        body,
        name=name,
        grid=(s // tm,),
        in_specs=[
            pl.BlockSpec((tm, f), lambda i: (i, 0)),
            pl.BlockSpec((tm, f), lambda i: (i, 1)),
            _resident((None, f, d), lambda i: (l, 0, 0)),
            pl.BlockSpec((tm, d), lambda i: (i, 0)),
        ],
        out_specs=pl.BlockSpec((tm, d), lambda i: (i, 0)),
        out_shape=jax.ShapeDtypeStruct((s, d), F32),
        compiler_params=_params(),
    )(gu, gu, w_down, x)


def _loss_head(y, target, *, name, tm=512):
    s, d = y.shape
    tm = _tile(s, tm)

    def body(y_ref, t_ref, l_ref, dy_ref):
        @pl.when(pl.program_id(0) == 0)
        def _():
            l_ref[...] = jnp.zeros_like(l_ref)

        err = y_ref[...] - t_ref[...]
        dy_ref[...] = err * (1.0 / d)
        sq = _row_fold(err * err)
        part = sq[:, 0:128]
        for k in range(1, d // 128):
            part = part + sq[:, k * 128:(k + 1) * 128]
        l_ref[...] += part * (0.5 / d)

    blk = pl.BlockSpec((tm, d), lambda i: (i, 0))
    return pl.pallas_call(
        body,
        name=name,
        grid=(s // tm,),
        in_specs=[blk, blk],
        out_specs=[pl.BlockSpec((8, 128), lambda i: (0, 0)), blk],
        out_shape=[jax.ShapeDtypeStruct((8, 128), F32), jax.ShapeDtypeStruct((s, d), F32)],
        compiler_params=_params(),
    )(y, target)


def _matmul_tn(a, b, slot, n_slots, into, *, name, t1=1024, tn=None, tm=1024, col0=0, n_total=None):
    m, k1 = a.shape
    n = b.shape[1]
    n_total = n_total or n
    t1 = _tile(k1, t1)
    tn = tn or n
    tm = _tile(m, tm)
    steps = m // tm
    assert n % tn == 0 and col0 % tn == 0
    cb0 = col0 // tn

    def body(a_ref, b_ref, *refs):
        o_ref = refs[-1]

        @pl.when(pl.program_id(2) == 0)
        def _():
            o_ref[...] = jnp.zeros_like(o_ref)

        o_ref[...] += _dot_tn(a_ref[...], b_ref[...])

    return pl.pallas_call(
        body,
        name=name,
        grid=(k1 // t1, n // tn, steps),
        in_specs=[
            pl.BlockSpec((tm, t1), lambda i, j, k: (k, i)),
            pl.BlockSpec((tm, tn), lambda i, j, k: (k, j)),
        ] + ([] if into is None else [ANY]),
        out_specs=pl.BlockSpec((None, t1, tn), lambda i, j, k: (slot, i, cb0 + j)),
        out_shape=jax.ShapeDtypeStruct((n_slots, k1, n_total), F32),
        input_output_aliases={} if into is None else {2: 0},
        compiler_params=_params(),
    )(a, b, *([] if into is None else [into]))


def _matmul_nt_normbwd(pieces, w, l, x, g, dres, *, name, tm=512, tk=1024):
    s, d = x.shape
    k = w.shape[2]
    tm = _tile(s, tm)
    n = len(pieces)
    assert sum(pc.shape[1] for pc in pieces) == k

    def body(*refs):
        piece_refs = refs[:n]
        w_ref, x_ref, g_ref, dres_ref, dx_ref, dg_ref = refs[n:]

        @pl.when(pl.program_id(0) == 0)
        def _():
            dg_ref[...] = jnp.zeros_like(dg_ref)

        dh = jnp.zeros((tm, d), F32)
        off = 0
        for pc, pc_ref in zip(pieces, piece_refs):
            width = pc.shape[1]
            step = _tile(width, tk)
            for c0 in range(0, width, step):
                dh = dh + _dot_nt(pc_ref[:, c0:c0 + step], w_ref[:, off + c0:off + c0 + step])
            off += width
        xf = x_ref[...]
        r = lax.rsqrt(jnp.mean(xf * xf, axis=-1, keepdims=True) + EPS)
        y = xf * r
        dy = dh * g_ref[...]
        dx_ref[...] = dres_ref[...] + r * (dy - y * jnp.mean(dy * y, axis=-1, keepdims=True))
        dg_ref[...] += _row_fold(dh * y)

    xblk = pl.BlockSpec((tm, d), lambda i: (i, 0))
    return pl.pallas_call(
        body,
        name=name,
        grid=(s // tm,),
        in_specs=[pl.BlockSpec((tm, pc.shape[1]), lambda i: (i, 0)) for pc in pieces] + [
            _resident((None, d, k), lambda i: (l, 0, 0)),
            xblk,
            pl.BlockSpec((1, d), lambda i: (0, 0)),
            xblk,
        ],
        out_specs=[xblk, pl.BlockSpec((8, d), lambda i: (0, 0))],
        out_shape=[jax.ShapeDtypeStruct((s, d), F32), jax.ShapeDtypeStruct((8, d), F32)],
        compiler_params=_params(),
    )(*pieces, w, x, g, dres)


def _ffn_bwd(dx, gu, w_down, l, *, name, tm=512):
    s, d = dx.shape
    f = w_down.shape[1]
    tm = _tile(s, tm)

    def body(dx_ref, g_ref, u_ref, w_ref, dgu_ref, act_ref, dxb_ref):
        dxb = _mx(dx_ref[...])
        dxb_ref[...] = dxb
        dact = _dot_nt(dxb, w_ref[...])
        g = _f32(g_ref)
        u = _f32(u_ref)
        sg = _sigmoid(g)
        silu = g * sg
        act_ref[...] = _mx(silu * u)
        dgu_ref[:, 0:f] = _mx(dact * u * (sg * (1.0 + g * (1.0 - sg))))
        dgu_ref[:, f:2 * f] = _mx(dact * silu)

    fblk = lambda col: pl.BlockSpec((tm, f), lambda i: (i, col))
    dblk = pl.BlockSpec((tm, d), lambda i: (i, 0))
    return pl.pallas_call(
        body,
        name=name,
        grid=(s // tm,),
        in_specs=[dblk, fblk(0), fblk(1), _resident((None, f, d), lambda i: (l, 0, 0))],
        out_specs=[pl.BlockSpec((tm, 2 * f), lambda i: (i, 0)), fblk(0), dblk],
        out_shape=[
            jax.ShapeDtypeStruct((s, 2 * f), MXU_DTYPE),
            jax.ShapeDtypeStruct((s, f), MXU_DTYPE),
            jax.ShapeDtypeStruct((s, d), MXU_DTYPE),
        ],
        compiler_params=_params(),
    )(dx, gu, gu, w_down)


def _merge_bwd(dx, ys, p, b_gate, w_bo, w_o, l, *, gate_col0, name, tm=512):
    s, d = dx.shape
    _, nbr, c, _ = w_bo.shape
    tm = _tile(s, tm)

    def body(dx_ref, ya_ref, yb_ref, yc_ref, g0_ref, g1_ref, g2_ref, bg_ref, wbo_ref, wo_ref,
             dgate_ref, dya_ref, dyb_ref, dyc_ref, dd0_ref, dd1_ref, dd2_ref, dxb_ref, dbg_ref):
        @pl.when(pl.program_id(0) == 0)
        def _():
            dbg_ref[...] = jnp.zeros_like(dbg_ref)

        dxb = _mx(dx_ref[...])
        dxb_ref[...] = dxb
        dmerged = _dot_nt(dxb, wo_ref[...])
        branches = ((ya_ref, g0_ref, dya_ref, dd0_ref), (yb_ref, g1_ref, dyb_ref, dd1_ref), (yc_ref, g2_ref, dyc_ref, dd2_ref))
        for i, (y_ref, g_ref, dy_ref, dd_ref) in enumerate(branches):
            cols = slice(i * d, (i + 1) * d)
            gate = _sigmoid(_f32(g_ref) + bg_ref[:, cols])
            yd = _dot(y_ref[...], wbo_ref[i])
            dyd = _mx(dmerged * gate)
            dd_ref[...] = dyd
            dy_ref[...] = _dot_nt(dyd, wbo_ref[i])
            dpre = dmerged * yd * gate * (1.0 - gate)
            dgate_ref[:, cols] = _mx(dpre)
            dbg_ref[:, cols] += _row_fold(dpre)

    yblk = pl.BlockSpec((tm, c), lambda i: (i, 0))
    gblk = lambda k: pl.BlockSpec((tm, d), lambda i: (i, gate_col0 + k))
    dblk = pl.BlockSpec((tm, d), lambda i: (i, 0))
    return pl.pallas_call(
        body,
        name=name,
        grid=(s // tm,),
        in_specs=[
            dblk, yblk, yblk, yblk, gblk(0), gblk(1), gblk(2),
            pl.BlockSpec((1, nbr * d), lambda i: (0, 0)),
            _resident((None, nbr, c, d), lambda i: (l, 0, 0, 0)),
            _resident((None, d, d), lambda i: (l, 0, 0)),
        ],
        out_specs=[pl.BlockSpec((tm, nbr * d), lambda i: (i, 0)), yblk, yblk, yblk, dblk, dblk, dblk, dblk,
                   pl.BlockSpec((8, nbr * d), lambda i: (0, 0))],
        out_shape=[jax.ShapeDtypeStruct((s, nbr * d), MXU_DTYPE)] + [jax.ShapeDtypeStruct((s, c), F32)] * 3
        + [jax.ShapeDtypeStruct((s, d), MXU_DTYPE)] * 4 + [jax.ShapeDtypeStruct((8, nbr * d), F32)],
        compiler_params=_params(),
    )(dx, *ys, p, p, p, b_gate, w_bo, w_o)


def _conv_bwd(p, dya, conv_w, conv_b, *, name, tm=512):
    s = p.shape[0]
    c = conv_w.shape[1]
    tm = _tile(s, tm)
    hb = tm // HALO
    last = s // tm - 1

    def body(bg_ref, cg_ref, xa_ref, cgp_ref, xap_ref, dy_ref, dyn_ref, bgn_ref, w_ref, b_ref,
             dp_ref, dw_ref):
        i = pl.program_id(0)

        @pl.when(i == 0)
        def _():
            dw_ref[...] = jnp.zeros_like(dw_ref)

        cg = _f32(cg_ref)
        xa = _f32(xa_ref)
        u = cg * xa
        prev = jnp.where(i == 0, 0.0, _f32(cgp_ref) * _f32(xap_ref))
        um1, um2 = _conv_taps(u, prev)
        w = w_ref[...]
        y = b_ref[...] + w[0:1, :] * um2 + w[1:2, :] * um1 + w[2:3, :] * u
        dya = dy_ref[...]
        dp_ref[:, 0:c] = _mx(dya * y)
        dyv = dya * _f32(bg_ref)
        nxt = jnp.where(i == last, 0.0, dyn_ref[...] * _f32(bgn_ref))
        row = lax.broadcasted_iota(jnp.int32, dyv.shape, 0)
        dp1 = jnp.where(row == tm - 1, nxt[0:1, :], pltpu.roll(dyv, tm - 1, 0))
        dp2 = pltpu.roll(dyv, tm - 2, 0)
        dp2 = jnp.where(row == tm - 2, nxt[0:1, :], jnp.where(row == tm - 1, nxt[1:2, :], dp2))
        du = w[2:3, :] * dyv + w[1:2, :] * dp1 + w[0:1, :] * dp2
        dp_ref[:, c:2 * c] = _mx(du * xa)
        dp_ref[:, 2 * c:3 * c] = _mx(du * cg)
        dw_ref[0] += _row_fold(dyv * um2)
        dw_ref[1] += _row_fold(dyv * um1)
        dw_ref[2] += _row_fold(dyv * u)
        dw_ref[3] += _row_fold(dyv)

    blk = lambda col: pl.BlockSpec((tm, c), lambda i: (i, col))
    halo = lambda col: pl.BlockSpec((HALO, c), lambda i: (jnp.maximum(i * hb - 1, 0), col))
    nhalo = lambda col: pl.BlockSpec((HALO, c), lambda i: (jnp.minimum((i + 1) * hb, s // HALO - 1), col))
    return pl.pallas_call(
        body,
        name=name,
        grid=(s // tm,),
        in_specs=[blk(0), blk(1), blk(2), halo(1), halo(2), blk(0), nhalo(0), nhalo(0),
                  pl.BlockSpec((CONV_K, c), lambda i: (0, 0)), pl.BlockSpec((1, c), lambda i: (0, 0))],
        out_specs=[pl.BlockSpec((tm, 3 * c), lambda i: (i, 0)), pl.BlockSpec((4, 8, c), lambda i: (0, 0, 0))],
        out_shape=[jax.ShapeDtypeStruct((s, 3 * c), MXU_DTYPE), jax.ShapeDtypeStruct((4, 8, c), F32)],
        compiler_params=_params(),
    )(p, p, p, p, p, dya, dya, p, conv_w, conv_b)


def _sgu_bwd(p, dyb, ln_g, ln_b, wtril, wtril_t, bias_full, *, col0, name, tm=512):
    s = p.shape[0]
    c = ln_g.shape[1]
    groups = c // CHUNK
    tm = _tile(s, tm)

    def body(u_ref, v_ref, dy_ref, g_ref, b_ref, w_ref, wt_ref, bias_ref, duv_ref, dln_ref, dw_ref, dbias_ref, dvn_ref):
        @pl.when(pl.program_id(0) == 0)
        def _():
            dln_ref[...] = jnp.zeros_like(dln_ref)
            dw_ref[...] = jnp.zeros_like(dw_ref)
            dbias_ref[...] = jnp.zeros_like(dbias_ref)

        sv = _f32(v_ref)
        xhat, r = _layernorm_stats(_gelu(sv))
        vb = _mx(xhat * g_ref[...] + b_ref[...])
        for n in range(tm // CHUNK):
            rows = slice(n * CHUNK, (n + 1) * CHUNK)
            for gi in range(groups):
                cols = slice(gi * CHUNK, (gi + 1) * CHUNK)
                su = u_ref[rows, cols].astype(F32)
                dy = dy_ref[rows, cols]
                vblk = vb[rows, cols]
                mixed = _dot(w_ref[gi], vblk) + bias_ref[:, cols]
                duv_ref[rows, cols] = _mx(dy * mixed * _gelu_grad(su))
                dmixed = dy * _gelu(su)
                dmb = _mx(dmixed)
                dvn_ref[rows, cols] = _dot(wt_ref[gi], dmb)
                dw_ref[gi] += _dot_nt(dmb, vblk)
                dbias_ref[:, cols] += dmixed
        dvn = dvn_ref[...]
        dln_ref[0] += _row_fold(dvn * xhat)
        dln_ref[1] += _row_fold(dvn)
        dxh = dvn * g_ref[...]
        dgv = r * (dxh - jnp.mean(dxh, axis=-1, keepdims=True) - xhat * jnp.mean(dxh * xhat, axis=-1, keepdims=True))
        duv_ref[:, c:2 * c] = _mx(dgv * _gelu_grad(sv))

    blk = lambda col: pl.BlockSpec((tm, c), lambda i: (i, col))
    vec = pl.BlockSpec((1, c), lambda i: (0, 0))
    wspec = pl.BlockSpec((groups, CHUNK, CHUNK), lambda i: (0, 0, 0))
    return pl.pallas_call(
        body,
        name=name,
        grid=(s // tm,),
        in_specs=[blk(col0), blk(col0 + 1), blk(0), vec, vec, wspec, wspec, pl.BlockSpec((CHUNK, c), lambda i: (0, 0))],
        out_specs=[pl.BlockSpec((tm, 2 * c), lambda i: (i, 0)), pl.BlockSpec((2, 8, c), lambda i: (0, 0, 0)), wspec,
                   pl.BlockSpec((CHUNK, c), lambda i: (0, 0))],
        out_shape=[jax.ShapeDtypeStruct((s, 2 * c), MXU_DTYPE),
                   jax.ShapeDtypeStruct((2, 8, c), F32), jax.ShapeDtypeStruct((groups, CHUNK, CHUNK), F32),
                   jax.ShapeDtypeStruct((CHUNK, c), F32)],
        scratch_shapes=[pltpu.VMEM((tm, c), F32)],
        compiler_params=_params(),
    )(p, p, dyb, ln_g, ln_b, wtril, wtril_t, bias_full)


def _att_bwd(kt_hm, vt_hm, qt_hm, dot_hm, runs, later, earlier, *, name, hg=4):
    h, nb, d, t = kt_hm.shape
    s = nb * t
    nbp = runs.shape[2]
    scale = 1.0 / math.sqrt(d)
    assert h % hg == 0

    def body(kt_ref, vt_ref, qt_ref, dot_ref, runs_ref, later_ref, earlier_ref, dqt_ref, dkt_ref, dvt_ref):
        i = pl.program_id(1)

        @pl.when(i == 0)
        def _():
            dkt_ref[...] = jnp.zeros_like(dkt_ref)
            dvt_ref[...] = jnp.zeros_like(dvt_ref)

        q_ts = [qt_ref[g] for g in range(hg)]
        do_ts = [dot_ref[g] for g in range(hg)]

        best = runs_ref[0]
        for g in range(1, hg):
            best = jnp.maximum(best, runs_ref[g])
        row = lax.broadcasted_iota(jnp.int32, (nbp, 1), 0)
        counts = jnp.logical_and(jnp.max(best, axis=1, keepdims=True) > EXP_UNDERFLOW, row < i)
        seen = jnp.sum(counts.astype(jnp.int32))
        zeros_row = tuple(jnp.zeros((1, t), F32) for _ in range(hg))

        def up(j, carry):
            gsums, dqts = carry
            mask = _att_mask(t, j * t, i * t)
            heads = range(hg)
            zs, es, lbs, l1ms = _att_blocks([kt_ref[g, j] for g in heads], q_ts, scale, mask)
            afters = _dot_split_each(later_ref[...], l1ms)
            das = [_dot_tn(vt_ref[g, j], do_ts[g]) for g in heads]
            weights = [jnp.where(mask, jnp.exp(lbs[g] + afters[g] + _row_get(runs_ref, g, j)), 0.0) for g in heads]
            grs = [das[g] * weights[g] for g in heads]
            gbefores = _dot_split_each(earlier_ref[...], grs)
            dzs = []
            for g in heads:
                inv = 1.0 / (1.0 + es[g])
                pos = zs[g] >= 0.0
                beta = jnp.where(pos, inv, es[g] * inv)
                omb = jnp.where(pos, es[g] * inv, inv)
                dzs.append(_mx(jnp.where(mask, grs[g] * omb - (gbefores[g] + gsums[g]) * beta, 0.0) * scale))
            new_dqts = [dqts[g] + _dot(kt_ref[g, j], dzs[g]) for g in heads]
            for g in heads:
                dkt_ref[g, j] += _dot_nt(q_ts[g], dzs[g])
            for g in heads:
                dvt_ref[g, j] += _dot_nt(do_ts[g], _mx(weights[g]))
            new_gsums = [gsums[g] + jnp.sum(grs[g], axis=0, keepdims=True) for g in heads]
            return tuple(new_gsums), tuple(new_dqts)

        _, dqts = lax.fori_loop(i - seen, i + 1, up, (zeros_row, tuple(jnp.zeros((d, t), F32) for _ in range(hg))))
        for g in range(hg):
            dqt_ref[g] = dqts[g]

    whole = pl.BlockSpec((hg, nb, d, t), lambda hh, i: (hh, 0, 0, 0))
    cols = pl.BlockSpec((hg, d, t), lambda hh, i: (hh, 0, i))
    tri = pl.BlockSpec((t, t), lambda hh, i: (0, 0))
    return pl.pallas_call(
        body,
        name=name,
        grid=(h // hg, nb),
        in_specs=[whole, whole, cols, cols, pl.BlockSpec((hg, None, nbp, t), lambda hh, i: (hh, i, 0, 0)), tri, tri],
        out_specs=[cols, whole, whole],
        out_shape=[jax.ShapeDtypeStruct((h, d, s), F32), jax.ShapeDtypeStruct((h, nb, d, t), F32),
                   jax.ShapeDtypeStruct((h, nb, d, t), F32)],
        compiler_params=_params(),
    )(kt_hm, vt_hm, qt_hm, dot_hm, runs, later, earlier)


def _qkv_bwd(p, dqt, dkt, dvt, qg, kg, headavg, *, col0, name, tm=512):
    s = p.shape[0]
    c = qg.shape[1]
    tm = _tile(s, tm)

    t = ATT_BLOCK
    assert tm % t == 0

    def rows_of(blocked_ref):
        parts = []
        for b in range(tm // t):
            cols = jnp.concatenate([blocked_ref[hh, b] for hh in range(HEADS)], axis=0)
            parts.append(cols.T)
        return jnp.concatenate(parts, axis=0)

    def body(q_ref, k_ref, dqt_ref, dkt_ref, dvt_ref, qg_ref, kg_ref, avg_ref, dqkv_ref, dg_ref):
        @pl.when(pl.program_id(0) == 0)
        def _():
            dg_ref[...] = jnp.zeros_like(dg_ref)

        dqn = jnp.concatenate([dqt_ref[hh] for hh in range(HEADS)], axis=0).T
        for n, (src, dn, g_ref) in enumerate(((q_ref, dqn, qg_ref), (k_ref, rows_of(dkt_ref), kg_ref))):
            xf = _f32(src)
            r = lax.rsqrt(_head_mean(xf * xf, avg_ref[...]) + EPS)
            y = xf * r
            dy = dn * g_ref[...]
            dqkv_ref[:, n * c:(n + 1) * c] = _mx(r * (dy - y * _head_mean(dy * y, avg_ref[...])))
            dg_ref[n] += _row_fold(dn * y)
        dqkv_ref[:, 2 * c:3 * c] = _mx(rows_of(dvt_ref))

    blk = lambda col: pl.BlockSpec((tm, c), lambda i: (i, col))
    vec = pl.BlockSpec((1, c), lambda i: (0, 0))
    blocked = pl.BlockSpec((HEADS, tm // t, HEAD_DIM, t), lambda i: (0, i, 0, 0))
    return pl.pallas_call(
        body,
        name=name,
        grid=(s // tm,),
        in_specs=[blk(col0), blk(col0 + 1), pl.BlockSpec((HEADS, HEAD_DIM, tm), lambda i: (0, 0, i)), blocked, blocked,
                  vec, vec, pl.BlockSpec((c, c), lambda i: (0, 0))],
        out_specs=[pl.BlockSpec((tm, 3 * c), lambda i: (i, 0)), pl.BlockSpec((2, 8, c), lambda i: (0, 0, 0))],
        out_shape=[jax.ShapeDtypeStruct((s, 3 * c), MXU_DTYPE), jax.ShapeDtypeStruct((2, 8, c), F32)],
        compiler_params=_params(),
    )(p, p, dqt, dkt, dvt, qg, kg, headavg)


def _place():
    x, y, c = lax.axis_index("x"), lax.axis_index("y"), lax.axis_index("c")
    chips = [(1 - x, y), (x, 1 - y), (1 - x, 1 - y)]
    return x, y, c, chips


def _shard_of(ref, axis, chip, width):
    idx = [slice(None)] * len(ref.shape)
    idx[axis] = pl.ds(chip * width, width)
    return ref.at[tuple(idx)]


def _place_shard(w, layer, axis, chip_idx, dtype, *, name):
    rows, cols = _as_rows(w.shape[1:])
    tr = _row_tile(rows, cols, 2)
    if axis == len(w.shape) - 2:
        out_shape = (1, rows, cols * N_CHIPS)
        out_spec = pl.BlockSpec((None, tr, cols), lambda i, j_ref: (0, i, j_ref[0]))
    else:
        assert axis == 0
        per = rows // tr
        out_shape = (1, rows * N_CHIPS, cols)
        out_spec = pl.BlockSpec((None, tr, cols), lambda i, j_ref: (0, j_ref[0] * per + i, 0))
    full = [1, *w.shape[1:]]
    full[1 + axis] *= N_CHIPS

    def body(j_ref, w_ref, o_ref):
        o_ref[...] = w_ref[...].astype(dtype)

    out = pl.pallas_call(
        body,
        name=name,
        grid_spec=pltpu.PrefetchScalarGridSpec(
            num_scalar_prefetch=1,
            grid=(rows // tr,),
            in_specs=[pl.BlockSpec((None, tr, cols), lambda i, j_ref: (layer, i, 0))],
            out_specs=out_spec,
        ),
        out_shape=jax.ShapeDtypeStruct(out_shape, dtype),
        compiler_params=_params(),
    )(chip_idx, w.reshape(w.shape[0], rows, cols))
    return out.reshape(full)


def _gather_riders(placed, axes, layer):
    n = len(placed)
    widths = [pa.shape[1 + ax] // N_CHIPS for pa, ax in zip(placed, axes)]

    def copies(refs, sems):
        x, y, c, chips = _place()

        def block(a, chip):
            return _shard_of(refs[a].at[0], axes[a], chip, widths[a])

        def over_ici(a, k, chip):
            return pltpu.make_async_remote_copy(
                src_ref=block(a, chip), dst_ref=block(a, chip), send_sem=sems[0].at[a, k], recv_sem=sems[1].at[a, k],
                device_id=(*chips[k], c), device_id_type=MESH)

        def to_sibling(a, k):
            cx, cy = chips[k]
            return pltpu.make_async_remote_copy(
                src_ref=block(a, 2 * cx + cy), dst_ref=block(a, 2 * cx + cy), send_sem=sems[0].at[a, k],
                recv_sem=sems[1].at[a, k], device_id=(x, y, 1 - c), device_id_type=MESH)

        return 2 * x + y, c, chips, over_ici, to_sibling

    def ici_start(refs, sems):
        me, c, _, over_ici, _ = copies(refs, sems)

        @pl.when(c == layer)
        def _():
            for a in range(n):
                for k in range(3):
                    over_ici(a, k, me).start()

    def ici_finish(refs, sems):
        me, c, chips, over_ici, _ = copies(refs, sems)

        @pl.when(c == layer)
        def _():
            for a in range(n):
                for k in range(3):
                    cx, cy = chips[k]
                    over_ici(a, k, 2 * cx + cy).wait_recv()
            for a in range(n):
                for k in range(3):
                    over_ici(a, k, me).wait_send()

    def d2d_start(refs, sems):
        _, c, _, _, to_sibling = copies(refs, sems)

        @pl.when(c == layer)
        def _():
            for a in range(n):
                for k in range(3):
                    to_sibling(a, k).start()

    def d2d_finish(refs, sems):
        _, c, _, _, to_sibling = copies(refs, sems)

        @pl.when(c == layer)
        def _():
            for a in range(n):
                for k in range(3):
                    to_sibling(a, k).wait_send()

        @pl.when(c != layer)
        def _():
            for a in range(n):
                for k in range(3):
                    to_sibling(a, k).wait_recv()

    arrays, sems = tuple(placed), ((n, 3), (n, 3))
    return _Rider(arrays, sems, ici_start, ici_finish), _Rider(arrays, sems, d2d_start, d2d_finish)


def _pair_exchange(grads, *, name):
    n = len(grads)

    def body(*refs):
        ins, outs = refs[:n], refs[n:2 * n]
        send_sem, recv_sem = refs[2 * n:]
        x, y, c, _ = _place()
        copies = [
            pltpu.make_async_remote_copy(
                src_ref=ins[a].at[1 - c], dst_ref=outs[a], send_sem=send_sem.at[a], recv_sem=recv_sem.at[a],
                device_id=(x, y, 1 - c), device_id_type=MESH)
            for a in range(n)
        ]
        for cp in copies:
            cp.start()
        for cp in copies:
            cp.wait()

    return pl.pallas_call(
        body,
        name=name,
        in_specs=[ANY] * n,
        out_specs=[ANY] * n,
        out_shape=[jax.ShapeDtypeStruct(g.shape[1:], g.dtype) for g in grads],
        scratch_shapes=[pltpu.SemaphoreType.DMA((n,)), pltpu.SemaphoreType.DMA((n,))],
    )(*grads)


def _chip_scatter(sums, axes, *, name):
    n = len(sums)
    widths = [sm.shape[ax] // N_CHIPS for sm, ax in zip(sums, axes)]
    out_shapes = []
    for sm, ax, w in zip(sums, axes, widths):
        shp = list(sm.shape)
        shp[ax] = w
        out_shapes.append(jax.ShapeDtypeStruct((3, *shp), sm.dtype))

    def body(*refs):
        ins, outs = refs[:n], refs[n:2 * n]
        send_sem, recv_sem = refs[2 * n:]
        x, y, c, chips = _place()
        copies = []
        for a in range(n):
            for k in range(3):
                cx, cy = chips[k]
                copies.append(pltpu.make_async_remote_copy(
                    src_ref=_shard_of(ins[a], axes[a], 2 * cx + cy, widths[a]), dst_ref=outs[a].at[k],
                    send_sem=send_sem.at[a, k], recv_sem=recv_sem.at[a, k], device_id=(cx, cy, c), device_id_type=MESH))
        for cp in copies:
            cp.start()
        for cp in copies:
            cp.wait()

    return pl.pallas_call(
        body,
        name=name,
        in_specs=[ANY] * n,
        out_specs=[ANY] * n,
        out_shape=out_shapes,
        scratch_shapes=[pltpu.SemaphoreType.DMA((n, 3)), pltpu.SemaphoreType.DMA((n, 3))],
    )(*sums)


def _pair_share(finals, *, name):
    n = len(finals)

    def body(*refs):
        outs = refs[n:2 * n]
        send_sem, recv_sem = refs[2 * n:]
        x, y, c, _ = _place()
        for a in range(n):
            pltpu.make_async_remote_copy(
                src_ref=outs[a].at[c], dst_ref=outs[a].at[c], send_sem=send_sem.at[a], recv_sem=recv_sem.at[a],
                device_id=(x, y, 1 - c), device_id_type=MESH).start()
        for a in range(n):
            pltpu.make_async_remote_copy(
                src_ref=outs[a].at[c], dst_ref=outs[a].at[1 - c], send_sem=send_sem.at[a], recv_sem=recv_sem.at[a],
                device_id=(x, y, 1 - c), device_id_type=MESH).wait()

    return pl.pallas_call(
        body,
        name=name,
        in_specs=[ANY] * n,
        out_specs=[ANY] * n,
        out_shape=[jax.ShapeDtypeStruct(f.shape, f.dtype) for f in finals],
        input_output_aliases={a: a for a in range(n)},
        scratch_shapes=[pltpu.SemaphoreType.DMA((n,))] * 2,
    )(*finals)


def _small_view(shape):
    size = math.prod(shape)
    return (size // 128, 128) if size % 1024 == 0 else (shape[0], size // shape[0])


def _all_reduce_small(parts, *, name):
    n = len(parts)
    views = [_small_view(a.shape) for a in parts]

    def body(*refs):
        ins, outs, slots = refs[:n], refs[n:2 * n], refs[2 * n:3 * n]
        send_sem, recv_sem = refs[3 * n:]
        x, y, c, _ = _place()
        me = 4 * x + 2 * y + c
        copies = []
        for a in range(n):
            slots[a][me] = ins[a][...]
            for k in range(1, N_DEV):
                peer = (x ^ (k >> 2), y ^ ((k >> 1) & 1), c ^ (k & 1))
                copies.append(pltpu.make_async_remote_copy(
                    src_ref=ins[a], dst_ref=slots[a].at[me], send_sem=send_sem.at[a, k - 1],
                    recv_sem=recv_sem.at[a, k - 1], device_id=peer, device_id_type=MESH))
        for cp in copies:
            cp.start()
        for a in range(n):
            for k in range(1, N_DEV):
                pltpu.make_async_remote_copy(
                    src_ref=ins[a], dst_ref=slots[a].at[me ^ k], send_sem=send_sem.at[a, k - 1],
                    recv_sem=recv_sem.at[a, k - 1], device_id=(x, y, c), device_id_type=MESH).wait()
        for a in range(n):
            total = slots[a][0]
            for dev in range(1, N_DEV):
                total = total + slots[a][dev]
            outs[a][...] = total

    vmem = pl.BlockSpec(memory_space=pltpu.VMEM)
    outs = pl.pallas_call(
        body,
        name=name,
        in_specs=[vmem] * n,
        out_specs=[vmem] * n,
        out_shape=[jax.ShapeDtypeStruct(view, F32) for view in views],
        scratch_shapes=[pltpu.VMEM((N_DEV, *view), F32) for view in views]
        + [pltpu.SemaphoreType.DMA((n, N_DEV - 1)), pltpu.SemaphoreType.DMA((n, N_DEV - 1))],
        compiler_params=_params(),
    )(*[a.reshape(view) for a, view in zip(parts, views)])
    return [o.reshape(a.shape) for o, a in zip(outs, parts)]


def _as_rows(shape):
    cols = shape[-1]
    return math.prod(shape[:-1]), cols


ELEMENTWISE_VMEM = 24 * 1024 * 1024


def _row_tile(rows, cols, n_arrays, sublanes=8):
    cap = ELEMENTWISE_VMEM // (n_arrays * 2 * 4 * cols)
    best = None
    for t in range(sublanes, min(rows, cap) + 1, sublanes):
        if rows % t == 0:
            best = t
    assert best is not None, (rows, cols)
    return best


def _pair_sum(g, other, c_idx, *, name):
    rows, cols = _as_rows(other.shape)
    tr = _row_tile(rows, cols, 4, sublanes=16)
    g2 = g.reshape(2, rows, cols)

    def body(c_ref, g_ref, o_ref, out_ref, wire_ref):
        total = g_ref[...] + o_ref[...]
        out_ref[...] = total
        wire_ref[...] = total.astype(WIRE_DTYPE)

    blk = pl.BlockSpec((tr, cols), lambda i, c_ref: (i, 0))
    out, wire = pl.pallas_call(
        body,
        name=name,
        grid_spec=pltpu.PrefetchScalarGridSpec(
            num_scalar_prefetch=1,
            grid=(rows // tr,),
            in_specs=[pl.BlockSpec((None, tr, cols), lambda i, c_ref: (c_ref[0], i, 0)), blk],
            out_specs=[blk, blk],
        ),
        out_shape=[jax.ShapeDtypeStruct((rows, cols), F32), jax.ShapeDtypeStruct((rows, cols), WIRE_DTYPE)],
        compiler_params=_params(),
    )(c_idx, g2, other.reshape(rows, cols))
    return out.reshape(other.shape), wire.reshape(other.shape)


def _chip_sum(mine, got, axis, chip_idx, c_idx, *, name):
    shard_shape = got.shape[1:]
    rows, cols = _as_rows(shard_shape)
    tr = _row_tile(rows, cols, 5, sublanes=16)
    if axis == len(mine.shape) - 1:
        m2 = mine.reshape(rows, cols * N_CHIPS)
        mine_spec = pl.BlockSpec((tr, cols), lambda i, j_ref, c_ref: (i, j_ref[0]))
    else:
        assert axis == 0
        m2 = mine.reshape(N_CHIPS, rows, cols)
        mine_spec = pl.BlockSpec((None, tr, cols), lambda i, j_ref, c_ref: (j_ref[0], i, 0))

    def body(j_ref, c_ref, m_ref, got_ref, out_ref):
        out_ref[...] = ((m_ref[...] + got_ref[0].astype(F32)) + got_ref[1].astype(F32)) + got_ref[2].astype(F32)

    out = pl.pallas_call(
        body,
        name=name,
        grid_spec=pltpu.PrefetchScalarGridSpec(
            num_scalar_prefetch=2,
            grid=(rows // tr,),
            in_specs=[mine_spec, pl.BlockSpec((3, tr, cols), lambda i, j_ref, c_ref: (0, i, 0))],
            out_specs=pl.BlockSpec((None, tr, cols), lambda i, j_ref, c_ref: (c_ref[0], i, 0)),
        ),
        out_shape=jax.ShapeDtypeStruct((2, rows, cols), F32),
        compiler_params=_params(),
    )(chip_idx, c_idx, m2, got.reshape(3, rows, cols))
    return out.reshape((2, *shard_shape))


def _adamw_update(w_ref, g_ref, m_ref, v_ref, d_ref, nm_ref, nv_ref):
    c1 = 1.0 / (1.0 - ADAM_B1 ** ADAM_STEP)
    c2 = 1.0 / (1.0 - ADAM_B2 ** ADAM_STEP)
    gg = g_ref[...]
    nm = ADAM_B1 * m_ref[...] + (1.0 - ADAM_B1) * gg
    nv = ADAM_B2 * v_ref[...] + (1.0 - ADAM_B2) * (gg * gg)
    nm_ref[...] = nm
    nv_ref[...] = nv
    d_ref[...] = -ADAM_LR * ((nm * c1) / (jnp.sqrt(nv * c2) + ADAM_EPS) + ADAM_WD * w_ref[...])


def _adamw_small(ws, gs, ms, vs, *, name):
    n = len(ws)
    views = [_small_view(a.shape) for a in ws]

    def body(*refs):
        w_r, g_r, m_r, v_r = (refs[k * n:(k + 1) * n] for k in range(4))
        d_r, nm_r, nv_r = (refs[(4 + k) * n:(5 + k) * n] for k in range(3))
        for a in range(n):
            _adamw_update(w_r[a], g_r[a], m_r[a], v_r[a], d_r[a], nm_r[a], nv_r[a])

    vmem = pl.BlockSpec(memory_space=pltpu.VMEM)
    flat = lambda arrs: [a.reshape(view) for a, view in zip(arrs, views)]
    outs = pl.pallas_call(
        body,
        name=name,
        in_specs=[vmem] * (4 * n),
        out_specs=[vmem] * (3 * n),
        out_shape=[jax.ShapeDtypeStruct(view, F32) for view in views] * 3,
        compiler_params=_params(),
    )(*flat(ws), *flat(gs), *flat(ms), *flat(vs))
    return [tuple(outs[k * n + a].reshape(ws[a].shape) for k in range(3)) for a in range(n)]


def _adamw(w, g, m, v, *, name):
    shape = w.shape
    rows, cols = _as_rows(shape)
    tr = _row_tile(rows, cols, 7)
    body = functools.partial(_adamw_update)

    blk = pl.BlockSpec((tr, cols), lambda i: (i, 0))
    flat = lambda a: a.reshape(rows, cols)
    outs = pl.pallas_call(
        body,
        name=name,
        grid=(rows // tr,),
        in_specs=[blk] * 4,
        out_specs=[blk] * 3,
        out_shape=[jax.ShapeDtypeStruct((rows, cols), F32)] * 3,
        compiler_params=_params(),
    )(flat(w), flat(g), flat(m), flat(v))
    return tuple(o.reshape(shape) for o in outs)


def _cols(a):
    return a.reshape(a.shape[0], HEADS, HEAD_DIM).transpose(1, 2, 0)


def _from_cols(a):
    h, d, s = a.shape
    return a.transpose(2, 0, 1).reshape(s, h * d)


BIG = ("w_in", "w_branch_out", "w_o", "w_gate_up", "w_down")
BIG_AXIS = {"w_in": 1, "w_branch_out": 2, "w_o": 0, "w_gate_up": 1, "w_down": 0}
SMALL = ("mix_norm_g", "b_gate", "conv_w", "conv_b", "sgu_ln_g", "sgu_ln_b", "sgu_w", "sgu_b", "q_norm_g", "k_norm_g",
         "ffn_norm_g")
ORDER = ("mix_norm_g", "w_in", "b_gate", "conv_w", "conv_b", "sgu_ln_g", "sgu_ln_b", "sgu_w", "sgu_b", "q_norm_g",
         "k_norm_g", "w_branch_out", "w_o", "ffn_norm_g", "w_gate_up", "w_down")


def _layer_forward(x, w, wl, l, coming):
    n_in = wl["w_in"].shape[2]
    gate_col0 = (n_in - 3 * x.shape[1]) // x.shape[1]
    tag = f"l{l}"
    over_ici = _gather_riders(*coming)[0] if coming else None
    (p, h), arrived = _norm_matmul(x, w["mix_norm_g"][l][None], wl["w_in"], 0, name=f"in_proj_{tag}", tn=n_in // 7,
                                   rider=over_ici)
    ya = _conv_fwd(p, wl["conv_w"][0], w["conv_b"][l][None], name=f"conv_{tag}")
    yb = _sgu_fwd(p, w["sgu_ln_g"][l][None], w["sgu_ln_b"][l][None], w["wtril"][l], w["bias_full"][l], col0=3,
                  name=f"sgu_{tag}")
    q_c, k_cb, v_cb = _qkv_prep(p, w["qg"][l], w["kg"][l], w["headavg"], col0=5, name=f"qkv_{tag}")
    to_sibling = _gather_riders(list(arrived), *coming[1:])[1] if coming else None
    (out_t, runs), arrived = _att_fwd(k_cb, q_c, v_cb, w["later"], name=f"att_{tag}", rider=to_sibling)
    yc = _from_cols(out_t)
    x1, merged = _merge_fwd((ya, yb, yc), p, w["b_gate"][l][None], wl["w_branch_out"], wl["w_o"], 0, x,
                            gate_col0=gate_col0, name=f"merge_{tag}")
    (gu, h2), _ = _norm_matmul(x1, w["ffn_norm_g"][l][None], wl["w_gate_up"], 0, name=f"gate_up_{tag}",
                               tn=wl["w_gate_up"].shape[2] // 4)
    x2 = _ffn_down(gu, wl["w_down"], 0, x1, name=f"down_{tag}")
    saved = dict(x=x, p=p, h=h, ya=ya, yb=yb, yc=yc, merged=merged, x1=x1, gu=gu, h2=h2,
                 att=(k_cb, v_cb, q_c, runs), gate_col0=gate_col0)
    return x2, saved, arrived


def _layer_backward(dx2, w, wl, sv, l, big):
    layers = w["conv_b"].shape[0]
    tag = f"l{l}"
    c = w["conv_b"].shape[1]
    n_in = wl["w_in"].shape[2]
    n_ff = wl["w_gate_up"].shape[2]
    g = {}
    dgu, act, dx2b = _ffn_bwd(dx2, sv["gu"], wl["w_down"], 0, name=f"down_bwd_{tag}")
    big["w_down"] = _matmul_tn(act, dx2b, l, layers, big.get("w_down"), name=f"dw_down_{tag}", t1=act.shape[1] // 2)
    dx1, dg2 = _matmul_nt_normbwd([dgu], wl["w_gate_up"], 0, sv["x1"], w["ffn_norm_g"][l][None], dx2,
                                  name=f"gate_up_bwd_{tag}")
    g["ffn_norm_g"] = jnp.sum(dg2, axis=0)
    big["w_gate_up"] = _matmul_tn(sv["h2"], dgu, l, layers, big.get("w_gate_up"), name=f"dw_gate_up_{tag}", tn=n_ff // 4)
    ys = (sv["ya"], sv["yb"], sv["yc"])
    (dgates, dya, dyb, dyc, dd0, dd1, dd2, dx1b, dbg) = _merge_bwd(
        dx1, ys, sv["p"], w["b_gate"][l][None], wl["w_branch_out"], wl["w_o"], 0, gate_col0=sv["gate_col0"],
        name=f"merge_bwd_{tag}")
    g["b_gate"] = jnp.sum(dbg, axis=0)
    big["w_o"] = _matmul_tn(sv["merged"], dx1b, l, layers, big.get("w_o"), name=f"dw_o_{tag}")
    for i, (y, dd) in enumerate(zip(ys, (dd0, dd1, dd2))):
        big["w_branch_out"] = _matmul_tn(y, dd, len(ys) * l + i, len(ys) * layers, big.get("w_branch_out"),
                                         name=f"dw_bo{i}_{tag}")
    dconv, dwc = _conv_bwd(sv["p"], dya, wl["conv_w"][0], w["conv_b"][l][None], name=f"conv_bwd_{tag}")
    dwc = jnp.sum(dwc, axis=1)
    g["conv_w"] = dwc[0:CONV_K]
    g["conv_b"] = dwc[CONV_K]
    dsgu, dln, dws, dbias = _sgu_bwd(sv["p"], dyb, w["sgu_ln_g"][l][None], w["sgu_ln_b"][l][None], w["wtril"][l],
                                     w["wtril_t"][l], w["bias_full"][l], col0=3, name=f"sgu_bwd_{tag}")
    dln = jnp.sum(dln, axis=1)
    g["sgu_ln_g"], g["sgu_ln_b"] = dln[0], dln[1]
    g["sgu_w"] = jnp.where(w["tril"], dws, 0.0)
    g["sgu_b"] = jnp.sum(dbias.reshape(CHUNK, c // CHUNK, CHUNK), axis=2).T
    k_cb, v_cb, q_c, runs = sv["att"]
    dqt, dkt, dvt = _att_bwd(k_cb, v_cb, q_c, _cols(_mx(dyc)), runs, w["later"], w["earlier"], name=f"att_bwd_{tag}")
    dqkv, dqkg = _qkv_bwd(sv["p"], dqt, dkt, dvt, w["qg"][l], w["kg"][l], w["headavg"], col0=5, name=f"qkv_bwd_{tag}")
    dqkg = jnp.sum(dqkg.reshape(2, 8 * HEADS, HEAD_DIM), axis=1)
    g["q_norm_g"], g["k_norm_g"] = dqkg[0], dqkg[1]
    pieces = [dconv, dsgu, dqkv, dgates]
    dx0, dg1 = _matmul_nt_normbwd(pieces, wl["w_in"], 0, sv["x"], w["mix_norm_g"][l][None], dx1, name=f"in_proj_bwd_{tag}")
    g["mix_norm_g"] = jnp.sum(dg1, axis=0)
    col0 = 0
    for k, pc in enumerate(pieces):
        width = pc.shape[1]
        big["w_in"] = _matmul_tn(sv["h"], pc, l, layers, big.get("w_in"), name=f"dw_in{k}_{tag}", col0=col0, n_total=n_in,
                                 tn=math.gcd(col0, width) if col0 else width)
        col0 += width
    return dx0, g


def kernel(x, mix_norm_g, w_in, b_gate, conv_w, conv_b, sgu_ln_g, sgu_ln_b, sgu_w, sgu_b, q_norm_g, k_norm_g, w_branch_out, w_o, ffn_norm_g, w_gate_up, w_down, loss_target, m_mix_norm_g, m_w_in, m_b_gate, m_conv_w, m_conv_b, m_sgu_ln_g, m_sgu_ln_b, m_sgu_w, m_sgu_b, m_q_norm_g, m_k_norm_g, m_w_branch_out, m_w_o, m_ffn_norm_g, m_w_gate_up, m_w_down, v_mix_norm_g, v_w_in, v_b_gate, v_conv_w, v_conv_b, v_sgu_ln_g, v_sgu_ln_b, v_sgu_w, v_sgu_b, v_q_norm_g, v_k_norm_g, v_w_branch_out, v_w_o, v_ffn_norm_g, v_w_gate_up, v_w_down):
    given = dict(locals())
    params = {n: given[n] for n in ORDER}
    moms = {n: (given["m_" + n], given["v_" + n]) for n in ORDER}
    layers = mix_norm_g.shape[0]
    assert layers == 2, "the exchanges split the work of a chip's two cores by layer"
    xs = x[0]
    target = loss_target[0]
    chip = 2 * lax.axis_index("x") + lax.axis_index("y")
    core = lax.axis_index("c")

    c_idx = core.reshape(1).astype(jnp.int32)
    j_idx = chip.reshape(1).astype(jnp.int32)
    width = conv_w.shape[2]
    gathered = BIG + ("conv_w",)
    axes = [BIG_AXIS[n] for n in BIG] + [1]

    def placed(layer):
        arrays = [_place_shard(params[n], layer, BIG_AXIS[n], j_idx, MXU_DTYPE, name=f"place_{n}_l{layer}") for n in BIG]
        mine = lax.dynamic_slice_in_dim(conv_w, layer, 1, axis=0)
        arrays.append(lax.dynamic_update_slice_in_dim(jnp.zeros((1, CONV_K, width * N_CHIPS), F32), mine, chip * width, 2))
        return arrays

    weights = [dict(zip(gathered, _ride_alone(_gather_riders(placed(0), axes, 0), name="gather_weights_l0"))), None]
    w = {n: params[n] for n in ("mix_norm_g", "b_gate", "conv_b", "sgu_ln_g", "sgu_ln_b", "ffn_norm_g")}
    groups = sgu_w.shape[1]
    tril = jnp.tril(jnp.ones((CHUNK, CHUNK), dtype=bool))
    w["tril"] = tril
    w["wtril"] = _mx(jnp.where(tril, sgu_w, 0.0))
    w["wtril_t"] = w["wtril"].transpose(0, 1, 3, 2)
    w["bias_full"] = jnp.repeat(sgu_b.transpose(0, 2, 1), CHUNK, axis=2)
    w["qg"] = jnp.tile(q_norm_g, (1, HEADS))[:, None, :]
    w["kg"] = jnp.tile(k_norm_g, (1, HEADS))[:, None, :]
    lane = jnp.arange(HEADS * HEAD_DIM) // HEAD_DIM
    w["headavg"] = _mx(jnp.where(lane[:, None] == lane[None, :], 1.0 / HEAD_DIM, 0.0))
    pos = jnp.arange(ATT_BLOCK)
    w["later"] = _mx(jnp.where(pos[None, :] > pos[:, None], 1.0, 0.0))
    w["earlier"] = _mx(jnp.where(pos[None, :] < pos[:, None], 1.0, 0.0))

    saved = []
    act = xs
    for l in range(layers):
        coming = (placed(l + 1), axes, l + 1) if l + 1 < layers else None
        act, sv, arrived = _layer_forward(act, w, weights[l], l, coming)
        saved.append(sv)
        if coming:
            weights[l + 1] = dict(zip(gathered, arrived))
    loss_part, dact = _loss_head(act, target, name="loss_head")
    loss = lax.psum(jnp.sum(loss_part), ("x", "y", "c"))
    grads = [None] * layers
    big = {}
    for l in reversed(range(layers)):
        dact, grads[l] = _layer_backward(dact, w, weights[l], saved[l], l, big)
    grad_x = dact[None]
    local = {n: jnp.stack([grads[l][n] for l in range(layers)]) for n in SMALL}
    for n in BIG:
        local[n] = big[n].reshape((layers, *weights[0][n].shape[1:]))

    got = _pair_exchange([local[n] for n in BIG], name="grads_pair_exchange")
    pair = [_pair_sum(local[n], o, c_idx, name=f"pair_sum_{n}") for n, o in zip(BIG, got)]
    landed = _chip_scatter([wire for _, wire in pair], [BIG_AXIS[n] for n in BIG], name="grads_chip_scatter")
    final = [_chip_sum(pr, ld, BIG_AXIS[n], j_idx, c_idx, name=f"chip_sum_{n}") for n, (pr, _), ld in zip(BIG, pair, landed)]
    full = dict(zip(BIG, _pair_share(final, name="grads_pair_share")))

    summed = _all_reduce_small([local[n] for n in SMALL], name="grads_all_reduce_small")
    for n, gsum in zip(SMALL, summed):
        full[n] = gsum
    full["conv_w"] = lax.dynamic_slice_in_dim(full["conv_w"], chip * width, width, axis=2)

    out = {}
    for n in BIG:
        out[n] = _adamw(params[n], full[n], *moms[n], name=f"adamw_{n}")
    small = _adamw_small([params[n] for n in SMALL], [full[n] for n in SMALL], [moms[n][0] for n in SMALL],
                         [moms[n][1] for n in SMALL], name="adamw_small")
    for n, triple in zip(SMALL, small):
        out[n] = triple
    return (loss, grad_x, *[full[n] for n in ORDER], *[out[n][0] for n in ORDER], *[out[n][1] for n in ORDER],
            *[out[n][2] for n in ORDER])
```

```python
import functools
import math
from typing import Callable, NamedTuple

import jax
import jax.numpy as jnp
from jax import lax
from jax.experimental import pallas as pl
from jax.experimental.pallas import tpu as pltpu

F32 = jnp.float32
MXU_DTYPE = jnp.bfloat16
WIRE_DTYPE = jnp.bfloat16
ACT_DTYPE = jnp.bfloat16
HALO = 16

EPS = 1e-6
CONV_K = 3
CHUNK = 128
HEADS = 8
HEAD_DIM = 64
ATT_BLOCK = 128
EXP_UNDERFLOW = -88.0

ADAM_LR = 0.001
ADAM_B1 = 0.9
ADAM_B2 = 0.999
ADAM_EPS = 1e-08
ADAM_WD = 0.01
ADAM_STEP = 10

VMEM_LIMIT = 56 * 1024 * 1024
MESH = pl.DeviceIdType.MESH
N_CHIPS = 4
N_DEV = 8
ANY = pl.BlockSpec(memory_space=pl.ANY)


def _params(**kw):
    return pltpu.CompilerParams(vmem_limit_bytes=VMEM_LIMIT, **kw)


class _Rider(NamedTuple):
    arrays: tuple
    sems: tuple
    start: Callable
    finish: Callable


def _call(body, operands, *, name, grid, in_specs, out_specs, out_shape, scratch_shapes=(), rider=None):
    if rider is None:
        outs = pl.pallas_call(body, name=name, grid=grid, in_specs=in_specs, out_specs=out_specs, out_shape=out_shape,
                              scratch_shapes=list(scratch_shapes), compiler_params=_params())(*operands)
        return tuple(outs), ()
    n_in, n_out, n_scr, k = len(in_specs), len(out_specs), len(scratch_shapes), len(rider.arrays)

    def riding(*refs):
        ins, refs = refs[:n_in], refs[n_in + k:]
        outs, carried, refs = refs[:n_out], refs[n_out:n_out + k], refs[n_out + k:]
        scratch, sems = refs[:n_scr], refs[n_scr:]
        first = functools.reduce(jnp.logical_and, [pl.program_id(ax) == 0 for ax in range(len(grid))])
        last = functools.reduce(jnp.logical_and, [pl.program_id(ax) == grid[ax] - 1 for ax in range(len(grid))])

        @pl.when(first)
        def _():
            rider.start(carried, sems)

        body(*ins, *outs, *scratch)

        @pl.when(last)
        def _():
            rider.finish(carried, sems)

    res = pl.pallas_call(
        riding, name=name, grid=grid, in_specs=[*in_specs, *[ANY] * k], out_specs=[*out_specs, *[ANY] * k],
        out_shape=[*out_shape, *[jax.ShapeDtypeStruct(a.shape, a.dtype) for a in rider.arrays]],
        input_output_aliases={n_in + j: n_out + j for j in range(k)},
        scratch_shapes=[*scratch_shapes, *[pltpu.SemaphoreType.DMA(shape) for shape in rider.sems]],
        compiler_params=_params())(*operands, *rider.arrays)
    return tuple(res[:n_out]), tuple(res[n_out:])


def _ride_alone(riders, *, name):
    arrays = riders[0].arrays
    k = len(arrays)
    counts = [len(r.sems) for r in riders]

    def body(*refs):
        carried, sems = refs[k:2 * k], refs[2 * k:]
        for r, count in zip(riders, counts):
            r.start(carried, sems[:count])
            r.finish(carried, sems[:count])
            sems = sems[count:]

    return pl.pallas_call(
        body, name=name, in_specs=[ANY] * k, out_specs=[ANY] * k,
        out_shape=[jax.ShapeDtypeStruct(a.shape, a.dtype) for a in arrays],
        input_output_aliases={j: j for j in range(k)},
        scratch_shapes=[pltpu.SemaphoreType.DMA(shape) for r in riders for shape in r.sems])(*arrays)


def _mx(v):
    return v.astype(MXU_DTYPE)


def _dot(a, b):
    return lax.dot_general(a, b, (((1,), (0,)), ((), ())), preferred_element_type=F32)


def _dot_nt(a, b):
    return lax.dot_general(a, b, (((1,), (1,)), ((), ())), preferred_element_type=F32)


def _dot_tn(a, b):
    return lax.dot_general(a, b, (((0,), (0,)), ((), ())), preferred_element_type=F32)


def _dot_split(const, v):
    hi = _mx(v)
    lo = _mx(v - hi.astype(F32))
    return _dot(const, hi) + _dot(const, lo)


def _dot_split_r(v, const):
    hi = _mx(v)
    lo = _mx(v - hi.astype(F32))
    return _dot(hi, const) + _dot(lo, const)


def _sigmoid(x):
    return 1.0 / (1.0 + jnp.exp(-x))


_INV_SQRT2 = 1.0 / math.sqrt(2.0)
_INV_SQRT2PI = 1.0 / math.sqrt(2.0 * math.pi)


def _gelu(x):
    return 0.5 * x * (1.0 + lax.erf(x * _INV_SQRT2))


def _gelu_grad(x):
    return 0.5 * (1.0 + lax.erf(x * _INV_SQRT2)) + x * jnp.exp(-0.5 * x * x) * _INV_SQRT2PI


def _row_fold(v):
    m, n = v.shape
    return jnp.sum(v.reshape(m // 8, 8, n), axis=0)


def _tile(m, pref):
    t = min(m, pref)
    while m % t:
        t //= 2
    return t


def _resident(block_shape, index_map):
    return pl.BlockSpec(block_shape, index_map, pipeline_mode=pl.Buffered(1))


def _norm_matmul(x, g, w, l, *, name, tm=512, tn=None, rider=None):
    s, d = x.shape
    n = w.shape[2]
    tm = _tile(s, tm)
    tn = tn or n
    assert n % tn == 0

    def body(x_ref, g_ref, w_ref, p_ref, h_ref):
        xf = x_ref[...]
        r = lax.rsqrt(jnp.mean(xf * xf, axis=-1, keepdims=True) + EPS)
        h = _mx(xf * r * g_ref[...])
        h_ref[...] = h
        for j in range(n // tn):
            cols = slice(j * tn, (j + 1) * tn)
            p_ref[:, cols] = _dot(h, w_ref[:, cols]).astype(ACT_DTYPE)

    return _call(
        body, (x, g, w),
        name=name,
        grid=(s // tm,),
        in_specs=[
            pl.BlockSpec((tm, d), lambda i: (i, 0)),
            pl.BlockSpec((1, d), lambda i: (0, 0)),
            _resident((None, d, n), lambda i: (l, 0, 0)),
        ],
        out_specs=[
            pl.BlockSpec((tm, n), lambda i: (i, 0)),
            pl.BlockSpec((tm, d), lambda i: (i, 0)),
        ],
        out_shape=[jax.ShapeDtypeStruct((s, n), ACT_DTYPE), jax.ShapeDtypeStruct((s, d), MXU_DTYPE)],
        rider=rider,
    )


def _conv_taps(u, prev):
    row = lax.broadcasted_iota(jnp.int32, u.shape, 0)
    last, before = prev[HALO - 1:HALO, :], prev[HALO - 2:HALO - 1, :]
    um1 = jnp.where(row == 0, last, pltpu.roll(u, 1, 0))
    um2 = pltpu.roll(u, 2, 0)
    um2 = jnp.where(row == 0, before, jnp.where(row == 1, last, um2))
    return um1, um2


def _f32(ref):
    return ref[...].astype(F32)


def _conv_fwd(p, conv_w, conv_b, *, name, tm=512):
    s = p.shape[0]
    c = conv_w.shape[1]
    tm = _tile(s, tm)
    hb = tm // HALO

    def body(bg_ref, cg_ref, xa_ref, cgp_ref, xap_ref, w_ref, b_ref, y_ref):
        first = pl.program_id(0) == 0
        u = _f32(cg_ref) * _f32(xa_ref)
        prev = jnp.where(first, 0.0, _f32(cgp_ref) * _f32(xap_ref))
        um1, um2 = _conv_taps(u, prev)
        w = w_ref[...]
        y = b_ref[...] + w[0:1, :] * um2 + w[1:2, :] * um1 + w[2:3, :] * u
        y_ref[...] = _mx(_f32(bg_ref) * y)

    halo = lambda col: pl.BlockSpec((HALO, c), lambda i: (jnp.maximum(i * hb - 1, 0), col))
    return pl.pallas_call(
        body,
        name=name,
        grid=(s // tm,),
        in_specs=[
            pl.BlockSpec((tm, c), lambda i: (i, 0)),
            pl.BlockSpec((tm, c), lambda i: (i, 1)),
            pl.BlockSpec((tm, c), lambda i: (i, 2)),
            halo(1),
            halo(2),
            pl.BlockSpec((CONV_K, c), lambda i: (0, 0)),
            pl.BlockSpec((1, c), lambda i: (0, 0)),
        ],
        out_specs=pl.BlockSpec((tm, c), lambda i: (i, 0)),
        out_shape=jax.ShapeDtypeStruct((s, c), MXU_DTYPE),
        compiler_params=_params(),
    )(p, p, p, p, p, conv_w, conv_b)


def _layernorm_stats(x):
    mu = jnp.mean(x, axis=-1, keepdims=True)
    xc = x - mu
    r = lax.rsqrt(jnp.mean(xc * xc, axis=-1, keepdims=True) + EPS)
    return xc * r, r


def _sgu_fwd(p, ln_g, ln_b, wtril, bias_full, *, col0, name, tm=512):
    s = p.shape[0]
    c = ln_g.shape[1]
    groups = c // CHUNK
    tm = _tile(s, tm)

    def body(u_ref, v_ref, g_ref, b_ref, w_ref, bias_ref, y_ref):
        vn, _ = _layernorm_stats(_gelu(_f32(v_ref)))
        vb = _mx(vn * g_ref[...] + b_ref[...])
        for n in range(tm // CHUNK):
            rows = slice(n * CHUNK, (n + 1) * CHUNK)
            for gi in range(groups):
                cols = slice(gi * CHUNK, (gi + 1) * CHUNK)
                mixed = _dot(w_ref[gi], vb[rows, cols]) + bias_ref[:, cols]
                y_ref[rows, cols] = _mx(_gelu(u_ref[rows, cols].astype(F32)) * mixed)

    return pl.pallas_call(
        body,
        name=name,
        grid=(s // tm,),
        in_specs=[
            pl.BlockSpec((tm, c), lambda i: (i, col0)),
            pl.BlockSpec((tm, c), lambda i: (i, col0 + 1)),
            pl.BlockSpec((1, c), lambda i: (0, 0)),
            pl.BlockSpec((1, c), lambda i: (0, 0)),
            pl.BlockSpec((groups, CHUNK, CHUNK), lambda i: (0, 0, 0)),
            pl.BlockSpec((CHUNK, c), lambda i: (0, 0)),
        ],
        out_specs=pl.BlockSpec((tm, c), lambda i: (i, 0)),
        out_shape=jax.ShapeDtypeStruct((s, c), MXU_DTYPE),
        compiler_params=_params(),
    )(p, p, ln_g, ln_b, wtril, bias_full)


def _head_mean(v, headavg):
    return _dot_split_r(v, headavg)


def _qkv_prep(p, qg, kg, headavg, *, col0, name, tm=512):
    s = p.shape[0]
    c = qg.shape[1]
    t = ATT_BLOCK
    tm = _tile(s, tm)
    assert tm % t == 0

    def body(q_ref, k_ref, v_ref, qg_ref, kg_ref, avg_ref, qc_ref, kcb_ref, vcb_ref):
        normed = []
        for src, g_ref in ((q_ref, qg_ref), (k_ref, kg_ref)):
            xf = _f32(src)
            r = lax.rsqrt(_head_mean(xf * xf, avg_ref[...]) + EPS)
            normed.append(xf * r * g_ref[...])
        q_t, k_t, v_t = (_mx(a.T) for a in (*normed, _f32(v_ref)))
        for hh in range(HEADS):
            rows = slice(hh * HEAD_DIM, (hh + 1) * HEAD_DIM)
            qc_ref[hh] = q_t[rows, :]
            for b in range(tm // t):
                kcb_ref[hh, b] = k_t[rows, b * t:(b + 1) * t]
                vcb_ref[hh, b] = v_t[rows, b * t:(b + 1) * t]

    blk = lambda col: pl.BlockSpec((tm, c), lambda i: (i, col))
    vec = pl.BlockSpec((1, c), lambda i: (0, 0))
    blocked = pl.BlockSpec((HEADS, tm // t, HEAD_DIM, t), lambda i: (0, i, 0, 0))
    return pl.pallas_call(
        body,
        name=name,
        grid=(s // tm,),
        in_specs=[blk(col0), blk(col0 + 1), blk(col0 + 2), vec, vec, pl.BlockSpec((c, c), lambda i: (0, 0))],
        out_specs=[pl.BlockSpec((HEADS, HEAD_DIM, tm), lambda i: (0, 0, i)), blocked, blocked],
        out_shape=[jax.ShapeDtypeStruct((HEADS, HEAD_DIM, s), MXU_DTYPE)]
        + [jax.ShapeDtypeStruct((HEADS, s // t, HEAD_DIM, t), MXU_DTYPE)] * 2,
        compiler_params=_params(),
    )(p, p, p, qg, kg, headavg)


def _att_mask(t, key0, qry0):
    kpos = key0 + lax.broadcasted_iota(jnp.int32, (t, t), 0)
    qpos = qry0 + lax.broadcasted_iota(jnp.int32, (t, t), 1)
    return kpos < qpos


def _att_blocks(k_blks, q_ts, scale, mask):
    zs = [(_dot_tn(k, q) if k.shape[0] == q.shape[0] else _dot(k, q)) * scale for k, q in zip(k_blks, q_ts)]
    es = [jnp.exp(-jnp.abs(z)) for z in zs]
    lbs = [jnp.minimum(z, 0.0) - jnp.log1p(e) for z, e in zip(zs, es)]
    l1ms = [jnp.where(mask, lb - z, 0.0) for lb, z in zip(lbs, zs)]
    return zs, es, lbs, l1ms


def _dot_split_each(const, vs):
    his = [_mx(v) for v in vs]
    los = [_mx(v - hi.astype(F32)) for v, hi in zip(vs, his)]
    tops = [_dot(const, hi) for hi in his]
    return [top + _dot(const, lo) for top, lo in zip(tops, los)]


def _max_over(runs):
    m = runs[0]
    for r in runs[1:]:
        m = jnp.maximum(m, r)
    return jnp.max(m)


NOT_SEEN = -1e30


def _row_put(ref, g, j, row):
    j8 = pl.multiple_of((j // 8) * 8, 8)
    blk = ref[g, pl.ds(j8, 8), :]
    sub = lax.broadcasted_iota(jnp.int32, blk.shape, 0)
    ref[g, pl.ds(j8, 8), :] = jnp.where(sub == j - j8, row, blk)


def _row_get(ref, g, j):
    j8 = pl.multiple_of((j // 8) * 8, 8)
    blk = ref[g, pl.ds(j8, 8), :]
    sub = lax.broadcasted_iota(jnp.int32, blk.shape, 0)
    return jnp.sum(jnp.where(sub == j - j8, blk, 0.0), axis=0, keepdims=True)


def _att_fwd(k_hm, qt_hm, vt_hm, later, *, name, hg=8, rider=None):
    h, nb, d, t = k_hm.shape
    s = nb * t
    nbp = -(-nb // 8) * 8
    scale = 1.0 / math.sqrt(d)
    assert h % hg == 0

    def body(k_ref, qt_ref, vt_ref, later_ref, o_ref, runs_ref):
        i = pl.program_id(1)
        q_ts = [qt_ref[g] for g in range(hg)]
        runs_ref[...] = jnp.full(runs_ref.shape, NOT_SEEN, F32)

        def cond(carry):
            j, _, _, rmax = carry
            return jnp.logical_and(j >= 0, rmax > EXP_UNDERFLOW)

        def step(carry):
            j, runs, accs, _ = carry
            mask = _att_mask(t, j * t, i * t)
            heads = range(hg)
            for g in heads:
                _row_put(runs_ref, g, j, runs[g])
            _, _, lbs, l1ms = _att_blocks([k_ref[g, j] for g in heads], q_ts, scale, mask)
            afters = _dot_split_each(later_ref[...], l1ms)
            weights = [_mx(jnp.where(mask, jnp.exp(lbs[g] + afters[g] + runs[g]), 0.0)) for g in heads]
            new_accs = [accs[g] + _dot(vt_ref[g, j], weights[g]) for g in heads]
            new_runs = [runs[g] + jnp.sum(l1ms[g], axis=0, keepdims=True) for g in heads]
            return j - 1, tuple(new_runs), tuple(new_accs), _max_over(new_runs)

        init = (i, tuple(jnp.zeros((1, t), F32) for _ in range(hg)), tuple(jnp.zeros((d, t), F32) for _ in range(hg)),
                jnp.float32(0.0))
        _, _, accs, _ = lax.while_loop(cond, step, init)
        for g in range(hg):
            o_ref[g] = _mx(accs[g])

    return _call(
        body, (k_hm, qt_hm, vt_hm, later),
        name=name,
        grid=(h // hg, nb),
        in_specs=[
            pl.BlockSpec((hg, nb, d, t), lambda hh, i: (hh, 0, 0, 0)),
            pl.BlockSpec((hg, d, t), lambda hh, i: (hh, 0, i)),
            pl.BlockSpec((hg, nb, d, t), lambda hh, i: (hh, 0, 0, 0)),
            pl.BlockSpec((t, t), lambda hh, i: (0, 0)),
        ],
        out_specs=[pl.BlockSpec((hg, d, t), lambda hh, i: (hh, 0, i)),
                   pl.BlockSpec((hg, None, nbp, t), lambda hh, i: (hh, i, 0, 0))],
        out_shape=[jax.ShapeDtypeStruct((h, d, s), MXU_DTYPE), jax.ShapeDtypeStruct((h, nb, nbp, t), F32)],
        rider=rider,
    )


def _merge_fwd(ys, p, b_gate, w_bo, w_o, l, x, *, gate_col0, name, tm=512):
    s, d = x.shape
    _, nbr, c, _ = w_bo.shape
    tm = _tile(s, tm)

    def body(ya_ref, yb_ref, yc_ref, g0_ref, g1_ref, g2_ref, bg_ref, wbo_ref, wo_ref, x_ref, x1_ref, m_ref):
        merged = jnp.zeros((tm, d), F32)
        for i, (y_ref, g_ref) in enumerate(((ya_ref, g0_ref), (yb_ref, g1_ref), (yc_ref, g2_ref))):
            gate = _sigmoid(_f32(g_ref) + bg_ref[:, i * d:(i + 1) * d])
            merged = merged + gate * _dot(y_ref[...], wbo_ref[i])
        mb = _mx(merged)
        m_ref[...] = mb
        x1_ref[...] = x_ref[...] + _dot(mb, wo_ref[...])

    yblk = pl.BlockSpec((tm, c), lambda i: (i, 0))
    gblk = lambda k: pl.BlockSpec((tm, d), lambda i: (i, gate_col0 + k))
    xblk = pl.BlockSpec((tm, d), lambda i: (i, 0))
    return pl.pallas_call(
        body,
        name=name,
        grid=(s // tm,),
        in_specs=[
            yblk, yblk, yblk, gblk(0), gblk(1), gblk(2),
            pl.BlockSpec((1, nbr * d), lambda i: (0, 0)),
            _resident((None, nbr, c, d), lambda i: (l, 0, 0, 0)),
            _resident((None, d, d), lambda i: (l, 0, 0)),
            xblk,
        ],
        out_specs=[xblk, xblk],
        out_shape=[jax.ShapeDtypeStruct((s, d), F32), jax.ShapeDtypeStruct((s, d), MXU_DTYPE)],
        compiler_params=_params(),
    )(*ys, p, p, p, b_gate, w_bo, w_o, x)


def _ffn_down(gu, w_down, l, x, *, name, tm=512):
    s, d = x.shape
    f = w_down.shape[1]
    tm = _tile(s, tm)

    def body(g_ref, u_ref, w_ref, x_ref, o_ref):
        g = _f32(g_ref)
        act = _mx(g * _sigmoid(g) * _f32(u_ref))
        o_ref[...] = x_ref[...] + _dot(act, w_ref[...])

    return pl.pallas_call(
        body,
        name=name,
        grid=(s // tm,),
        in_specs=[
            pl.BlockSpec((tm, f), lambda i: (i, 0)),
            pl.BlockSpec((tm, f), lambda i: (i, 1)),
            _resident((None, f, d), lambda i: (l, 0, 0)),
            pl.BlockSpec((tm, d), lambda i: (i, 0)),
        ],
        out_specs=pl.BlockSpec((tm, d), lambda i: (i, 0)),
        out_shape=jax.ShapeDtypeStruct((s, d), F32),
        compiler_params=_params(),
    )(gu, gu, w_down, x)


def _loss_head(y, target, *, name, tm=512):
    s, d = y.shape
    tm = _tile(s, tm)

    def body(y_ref, t_ref, l_ref, dy_ref):
        @pl.when(pl.program_id(0) == 0)
        def _():
            l_ref[...] = jnp.zeros_like(l_ref)

        err = y_ref[...] - t_ref[...]
        dy_ref[...] = err * (1.0 / d)
        sq = _row_fold(err * err)
        part = sq[:, 0:128]
        for k in range(1, d // 128):
            part = part + sq[:, k * 128:(k + 1) * 128]
        l_ref[...] += part * (0.5 / d)

    blk = pl.BlockSpec((tm, d), lambda i: (i, 0))
    return pl.pallas_call(
        body,
        name=name,
        grid=(s // tm,),
        in_specs=[blk, blk],
        out_specs=[pl.BlockSpec((8, 128), lambda i: (0, 0)), blk],
        out_shape=[jax.ShapeDtypeStruct((8, 128), F32), jax.ShapeDtypeStruct((s, d), F32)],
        compiler_params=_params(),
    )(y, target)


def _matmul_tn(a, b, slot, n_slots, into, *, name, t1=1024, tn=None, tm=1024, col0=0, n_total=None):
    m, k1 = a.shape
    n = b.shape[1]
    n_total = n_total or n
    t1 = _tile(k1, t1)
    tn = tn or n
    tm = _tile(m, tm)
    steps = m // tm
    assert n % tn == 0 and col0 % tn == 0
    cb0 = col0 // tn

    def body(a_ref, b_ref, *refs):
        o_ref = refs[-1]

        @pl.when(pl.program_id(2) == 0)
        def _():
            o_ref[...] = jnp.zeros_like(o_ref)

        o_ref[...] += _dot_tn(a_ref[...], b_ref[...])

    return pl.pallas_call(
        body,
        name=name,
        grid=(k1 // t1, n // tn, steps),
        in_specs=[
            pl.BlockSpec((tm, t1), lambda i, j, k: (k, i)),
            pl.BlockSpec((tm, tn), lambda i, j, k: (k, j)),
        ] + ([] if into is None else [ANY]),
        out_specs=pl.BlockSpec((None, t1, tn), lambda i, j, k: (slot, i, cb0 + j)),
        out_shape=jax.ShapeDtypeStruct((n_slots, k1, n_total), F32),
        input_output_aliases={} if into is None else {2: 0},
        compiler_params=_params(),
    )(a, b, *([] if into is None else [into]))


def _matmul_nt_normbwd(pieces, w, l, x, g, dres, *, name, tm=512, tk=1024, rider=None):
    s, d = x.shape
    k = w.shape[2]
    tm = _tile(s, tm)
    n = len(pieces)
    assert sum(pc.shape[1] for pc in pieces) == k

    def body(*refs):
        piece_refs = refs[:n]
        w_ref, x_ref, g_ref, dres_ref, dx_ref, dg_ref = refs[n:]

        @pl.when(pl.program_id(0) == 0)
        def _():
            dg_ref[...] = jnp.zeros_like(dg_ref)

        dh = jnp.zeros((tm, d), F32)
        off = 0
        for pc, pc_ref in zip(pieces, piece_refs):
            width = pc.shape[1]
            step = _tile(width, tk)
            for c0 in range(0, width, step):
                dh = dh + _dot_nt(pc_ref[:, c0:c0 + step], w_ref[:, off + c0:off + c0 + step])
            off += width
        xf = x_ref[...]
        r = lax.rsqrt(jnp.mean(xf * xf, axis=-1, keepdims=True) + EPS)
        y = xf * r
        dy = dh * g_ref[...]
        dx_ref[...] = dres_ref[...] + r * (dy - y * jnp.mean(dy * y, axis=-1, keepdims=True))
        dg_ref[...] += _row_fold(dh * y)

    xblk = pl.BlockSpec((tm, d), lambda i: (i, 0))
    return _call(
        body, (*pieces, w, x, g, dres),
        name=name,
        grid=(s // tm,),
        in_specs=[pl.BlockSpec((tm, pc.shape[1]), lambda i: (i, 0)) for pc in pieces] + [
            _resident((None, d, k), lambda i: (l, 0, 0)),
            xblk,
            pl.BlockSpec((1, d), lambda i: (0, 0)),
            xblk,
        ],
        out_specs=[xblk, pl.BlockSpec((8, d), lambda i: (0, 0))],
        out_shape=[jax.ShapeDtypeStruct((s, d), F32), jax.ShapeDtypeStruct((8, d), F32)],
        rider=rider,
    )


def _ffn_bwd(dx, gu, w_down, l, *, name, tm=512, rider=None):
    s, d = dx.shape
    f = w_down.shape[1]
    tm = _tile(s, tm)

    def body(dx_ref, g_ref, u_ref, w_ref, dgu_ref, act_ref, dxb_ref):
        dxb = _mx(dx_ref[...])
        dxb_ref[...] = dxb
        dact = _dot_nt(dxb, w_ref[...])
        g = _f32(g_ref)
        u = _f32(u_ref)
        sg = _sigmoid(g)
        silu = g * sg
        act_ref[...] = _mx(silu * u)
        dgu_ref[:, 0:f] = _mx(dact * u * (sg * (1.0 + g * (1.0 - sg))))
        dgu_ref[:, f:2 * f] = _mx(dact * silu)

    fblk = lambda col: pl.BlockSpec((tm, f), lambda i: (i, col))
    dblk = pl.BlockSpec((tm, d), lambda i: (i, 0))
    return _call(
        body, (dx, gu, gu, w_down),
        name=name,
        grid=(s // tm,),
        in_specs=[dblk, fblk(0), fblk(1), _resident((None, f, d), lambda i: (l, 0, 0))],
        out_specs=[pl.BlockSpec((tm, 2 * f), lambda i: (i, 0)), fblk(0), dblk],
        out_shape=[
            jax.ShapeDtypeStruct((s, 2 * f), MXU_DTYPE),
            jax.ShapeDtypeStruct((s, f), MXU_DTYPE),
            jax.ShapeDtypeStruct((s, d), MXU_DTYPE),
        ],
        rider=rider,
    )


def _merge_bwd(dx, ys, p, b_gate, w_bo, w_o, l, *, gate_col0, name, tm=512):
    s, d = dx.shape
    _, nbr, c, _ = w_bo.shape
    tm = _tile(s, tm)

    def body(dx_ref, ya_ref, yb_ref, yc_ref, g0_ref, g1_ref, g2_ref, bg_ref, wbo_ref, wo_ref,
             dgate_ref, dya_ref, dyb_ref, dyc_ref, dd0_ref, dd1_ref, dd2_ref, dxb_ref, dbg_ref):
        @pl.when(pl.program_id(0) == 0)
        def _():
            dbg_ref[...] = jnp.zeros_like(dbg_ref)

        dxb = _mx(dx_ref[...])
        dxb_ref[...] = dxb
        dmerged = _dot_nt(dxb, wo_ref[...])
        branches = ((ya_ref, g0_ref, dya_ref, dd0_ref), (yb_ref, g1_ref, dyb_ref, dd1_ref), (yc_ref, g2_ref, dyc_ref, dd2_ref))
        for i, (y_ref, g_ref, dy_ref, dd_ref) in enumerate(branches):
            cols = slice(i * d, (i + 1) * d)
            gate = _sigmoid(_f32(g_ref) + bg_ref[:, cols])
            yd = _dot(y_ref[...], wbo_ref[i])
            dyd = _mx(dmerged * gate)
            dd_ref[...] = dyd
            dy_ref[...] = _dot_nt(dyd, wbo_ref[i])
            dpre = dmerged * yd * gate * (1.0 - gate)
            dgate_ref[:, cols] = _mx(dpre)
            dbg_ref[:, cols] += _row_fold(dpre)

    yblk = pl.BlockSpec((tm, c), lambda i: (i, 0))
    gblk = lambda k: pl.BlockSpec((tm, d), lambda i: (i, gate_col0 + k))
    dblk = pl.BlockSpec((tm, d), lambda i: (i, 0))
    return pl.pallas_call(
        body,
        name=name,
        grid=(s // tm,),
        in_specs=[
            dblk, yblk, yblk, yblk, gblk(0), gblk(1), gblk(2),
            pl.BlockSpec((1, nbr * d), lambda i: (0, 0)),
            _resident((None, nbr, c, d), lambda i: (l, 0, 0, 0)),
            _resident((None, d, d), lambda i: (l, 0, 0)),
        ],
        out_specs=[pl.BlockSpec((tm, nbr * d), lambda i: (i, 0)), yblk, yblk, yblk, dblk, dblk, dblk, dblk,
                   pl.BlockSpec((8, nbr * d), lambda i: (0, 0))],
        out_shape=[jax.ShapeDtypeStruct((s, nbr * d), MXU_DTYPE)] + [jax.ShapeDtypeStruct((s, c), F32)] * 3
        + [jax.ShapeDtypeStruct((s, d), MXU_DTYPE)] * 4 + [jax.ShapeDtypeStruct((8, nbr * d), F32)],
        compiler_params=_params(),
    )(dx, *ys, p, p, p, b_gate, w_bo, w_o)


def _conv_bwd(p, dya, conv_w, conv_b, *, name, tm=512):
    s = p.shape[0]
    c = conv_w.shape[1]
    tm = _tile(s, tm)
    hb = tm // HALO
    last = s // tm - 1

    def body(bg_ref, cg_ref, xa_ref, cgp_ref, xap_ref, dy_ref, dyn_ref, bgn_ref, w_ref, b_ref,
             dp_ref, dw_ref):
        i = pl.program_id(0)

        @pl.when(i == 0)
        def _():
            dw_ref[...] = jnp.zeros_like(dw_ref)

        cg = _f32(cg_ref)
        xa = _f32(xa_ref)
        u = cg * xa
        prev = jnp.where(i == 0, 0.0, _f32(cgp_ref) * _f32(xap_ref))
        um1, um2 = _conv_taps(u, prev)
        w = w_ref[...]
        y = b_ref[...] + w[0:1, :] * um2 + w[1:2, :] * um1 + w[2:3, :] * u
        dya = dy_ref[...]
        dp_ref[:, 0:c] = _mx(dya * y)
        dyv = dya * _f32(bg_ref)
        nxt = jnp.where(i == last, 0.0, dyn_ref[...] * _f32(bgn_ref))
        row = lax.broadcasted_iota(jnp.int32, dyv.shape, 0)
        dp1 = jnp.where(row == tm - 1, nxt[0:1, :], pltpu.roll(dyv, tm - 1, 0))
        dp2 = pltpu.roll(dyv, tm - 2, 0)
        dp2 = jnp.where(row == tm - 2, nxt[0:1, :], jnp.where(row == tm - 1, nxt[1:2, :], dp2))
        du = w[2:3, :] * dyv + w[1:2, :] * dp1 + w[0:1, :] * dp2
        dp_ref[:, c:2 * c] = _mx(du * xa)
        dp_ref[:, 2 * c:3 * c] = _mx(du * cg)
        dw_ref[0] += _row_fold(dyv * um2)
        dw_ref[1] += _row_fold(dyv * um1)
        dw_ref[2] += _row_fold(dyv * u)
        dw_ref[3] += _row_fold(dyv)

    blk = lambda col: pl.BlockSpec((tm, c), lambda i: (i, col))
    halo = lambda col: pl.BlockSpec((HALO, c), lambda i: (jnp.maximum(i * hb - 1, 0), col))
    nhalo = lambda col: pl.BlockSpec((HALO, c), lambda i: (jnp.minimum((i + 1) * hb, s // HALO - 1), col))
    return pl.pallas_call(
        body,
        name=name,
        grid=(s // tm,),
        in_specs=[blk(0), blk(1), blk(2), halo(1), halo(2), blk(0), nhalo(0), nhalo(0),
                  pl.BlockSpec((CONV_K, c), lambda i: (0, 0)), pl.BlockSpec((1, c), lambda i: (0, 0))],
        out_specs=[pl.BlockSpec((tm, 3 * c), lambda i: (i, 0)), pl.BlockSpec((4, 8, c), lambda i: (0, 0, 0))],
        out_shape=[jax.ShapeDtypeStruct((s, 3 * c), MXU_DTYPE), jax.ShapeDtypeStruct((4, 8, c), F32)],
        compiler_params=_params(),
    )(p, p, p, p, p, dya, dya, p, conv_w, conv_b)


def _sgu_bwd(p, dyb, ln_g, ln_b, wtril, wtril_t, bias_full, *, col0, name, tm=512):
    s = p.shape[0]
    c = ln_g.shape[1]
    groups = c // CHUNK
    tm = _tile(s, tm)

    def body(u_ref, v_ref, dy_ref, g_ref, b_ref, w_ref, wt_ref, bias_ref, duv_ref, dln_ref, dw_ref, dbias_ref, dvn_ref):
        @pl.when(pl.program_id(0) == 0)
        def _():
            dln_ref[...] = jnp.zeros_like(dln_ref)
            dw_ref[...] = jnp.zeros_like(dw_ref)
            dbias_ref[...] = jnp.zeros_like(dbias_ref)

        sv = _f32(v_ref)
        xhat, r = _layernorm_stats(_gelu(sv))
        vb = _mx(xhat * g_ref[...] + b_ref[...])
        for n in range(tm // CHUNK):
            rows = slice(n * CHUNK, (n + 1) * CHUNK)
            for gi in range(groups):
                cols = slice(gi * CHUNK, (gi + 1) * CHUNK)
                su = u_ref[rows, cols].astype(F32)
                dy = dy_ref[rows, cols]
                vblk = vb[rows, cols]
                mixed = _dot(w_ref[gi], vblk) + bias_ref[:, cols]
                duv_ref[rows, cols] = _mx(dy * mixed * _gelu_grad(su))
                dmixed = dy * _gelu(su)
                dmb = _mx(dmixed)
                dvn_ref[rows, cols] = _dot(wt_ref[gi], dmb)
                dw_ref[gi] += _dot_nt(dmb, vblk)
                dbias_ref[:, cols] += dmixed
        dvn = dvn_ref[...]
        dln_ref[0] += _row_fold(dvn * xhat)
        dln_ref[1] += _row_fold(dvn)
        dxh = dvn * g_ref[...]
        dgv = r * (dxh - jnp.mean(dxh, axis=-1, keepdims=True) - xhat * jnp.mean(dxh * xhat, axis=-1, keepdims=True))
        duv_ref[:, c:2 * c] = _mx(dgv * _gelu_grad(sv))

    blk = lambda col: pl.BlockSpec((tm, c), lambda i: (i, col))
    vec = pl.BlockSpec((1, c), lambda i: (0, 0))
    wspec = pl.BlockSpec((groups, CHUNK, CHUNK), lambda i: (0, 0, 0))
    return pl.pallas_call(
        body,
        name=name,
        grid=(s // tm,),
        in_specs=[blk(col0), blk(col0 + 1), blk(0), vec, vec, wspec, wspec, pl.BlockSpec((CHUNK, c), lambda i: (0, 0))],
        out_specs=[pl.BlockSpec((tm, 2 * c), lambda i: (i, 0)), pl.BlockSpec((2, 8, c), lambda i: (0, 0, 0)), wspec,
                   pl.BlockSpec((CHUNK, c), lambda i: (0, 0))],
        out_shape=[jax.ShapeDtypeStruct((s, 2 * c), MXU_DTYPE),
                   jax.ShapeDtypeStruct((2, 8, c), F32), jax.ShapeDtypeStruct((groups, CHUNK, CHUNK), F32),
                   jax.ShapeDtypeStruct((CHUNK, c), F32)],
        scratch_shapes=[pltpu.VMEM((tm, c), F32)],
        compiler_params=_params(),
    )(p, p, dyb, ln_g, ln_b, wtril, wtril_t, bias_full)


def _att_bwd(kt_hm, vt_hm, qt_hm, dot_hm, runs, later, earlier, *, name, hg=4, rider=None):
    h, nb, d, t = kt_hm.shape
    s = nb * t
    nbp = runs.shape[2]
    scale = 1.0 / math.sqrt(d)
    assert h % hg == 0

    def body(kt_ref, vt_ref, qt_ref, dot_ref, runs_ref, later_ref, earlier_ref, dqt_ref, dkt_ref, dvt_ref):
        i = pl.program_id(1)

        @pl.when(i == 0)
        def _():
            dkt_ref[...] = jnp.zeros_like(dkt_ref)
            dvt_ref[...] = jnp.zeros_like(dvt_ref)

        q_ts = [qt_ref[g] for g in range(hg)]
        do_ts = [dot_ref[g] for g in range(hg)]

        best = runs_ref[0]
        for g in range(1, hg):
            best = jnp.maximum(best, runs_ref[g])
        row = lax.broadcasted_iota(jnp.int32, (nbp, 1), 0)
        counts = jnp.logical_and(jnp.max(best, axis=1, keepdims=True) > EXP_UNDERFLOW, row < i)
        seen = jnp.sum(counts.astype(jnp.int32))
        zeros_row = tuple(jnp.zeros((1, t), F32) for _ in range(hg))

        def up(j, carry):
            gsums, dqts = carry
            mask = _att_mask(t, j * t, i * t)
            heads = range(hg)
            zs, es, lbs, l1ms = _att_blocks([kt_ref[g, j] for g in heads], q_ts, scale, mask)
            afters = _dot_split_each(later_ref[...], l1ms)
            das = [_dot_tn(vt_ref[g, j], do_ts[g]) for g in heads]
            weights = [jnp.where(mask, jnp.exp(lbs[g] + afters[g] + _row_get(runs_ref, g, j)), 0.0) for g in heads]
            grs = [das[g] * weights[g] for g in heads]
            gbefores = _dot_split_each(earlier_ref[...], grs)
            dzs = []
            for g in heads:
                inv = 1.0 / (1.0 + es[g])
                pos = zs[g] >= 0.0
                beta = jnp.where(pos, inv, es[g] * inv)
                omb = jnp.where(pos, es[g] * inv, inv)
                dzs.append(_mx(jnp.where(mask, grs[g] * omb - (gbefores[g] + gsums[g]) * beta, 0.0) * scale))
            new_dqts = [dqts[g] + _dot(kt_ref[g, j], dzs[g]) for g in heads]
            for g in heads:
                dkt_ref[g, j] += _dot_nt(q_ts[g], dzs[g])
            for g in heads:
                dvt_ref[g, j] += _dot_nt(do_ts[g], _mx(weights[g]))
            new_gsums = [gsums[g] + jnp.sum(grs[g], axis=0, keepdims=True) for g in heads]
            return tuple(new_gsums), tuple(new_dqts)

        _, dqts = lax.fori_loop(i - seen, i + 1, up, (zeros_row, tuple(jnp.zeros((d, t), F32) for _ in range(hg))))
        for g in range(hg):
            dqt_ref[g] = dqts[g]

    whole = pl.BlockSpec((hg, nb, d, t), lambda hh, i: (hh, 0, 0, 0))
    cols = pl.BlockSpec((hg, d, t), lambda hh, i: (hh, 0, i))
    tri = pl.BlockSpec((t, t), lambda hh, i: (0, 0))
    return _call(
        body, (kt_hm, vt_hm, qt_hm, dot_hm, runs, later, earlier),
        name=name,
        grid=(h // hg, nb),
        in_specs=[whole, whole, cols, cols, pl.BlockSpec((hg, None, nbp, t), lambda hh, i: (hh, i, 0, 0)), tri, tri],
        out_specs=[cols, whole, whole],
        out_shape=[jax.ShapeDtypeStruct((h, d, s), F32), jax.ShapeDtypeStruct((h, nb, d, t), F32),
                   jax.ShapeDtypeStruct((h, nb, d, t), F32)],
        rider=rider,
    )


def _qkv_bwd(p, dqt, dkt, dvt, qg, kg, headavg, *, col0, name, tm=512):
    s = p.shape[0]
    c = qg.shape[1]
    tm = _tile(s, tm)

    t = ATT_BLOCK
    assert tm % t == 0

    def rows_of(blocked_ref):
        parts = []
        for b in range(tm // t):
            cols = jnp.concatenate([blocked_ref[hh, b] for hh in range(HEADS)], axis=0)
            parts.append(cols.T)
        return jnp.concatenate(parts, axis=0)

    def body(q_ref, k_ref, dqt_ref, dkt_ref, dvt_ref, qg_ref, kg_ref, avg_ref, dqkv_ref, dg_ref):
        @pl.when(pl.program_id(0) == 0)
        def _():
            dg_ref[...] = jnp.zeros_like(dg_ref)

        dqn = jnp.concatenate([dqt_ref[hh] for hh in range(HEADS)], axis=0).T
        for n, (src, dn, g_ref) in enumerate(((q_ref, dqn, qg_ref), (k_ref, rows_of(dkt_ref), kg_ref))):
            xf = _f32(src)
            r = lax.rsqrt(_head_mean(xf * xf, avg_ref[...]) + EPS)
            y = xf * r
            dy = dn * g_ref[...]
            dqkv_ref[:, n * c:(n + 1) * c] = _mx(r * (dy - y * _head_mean(dy * y, avg_ref[...])))
            dg_ref[n] += _row_fold(dn * y)
        dqkv_ref[:, 2 * c:3 * c] = _mx(rows_of(dvt_ref))

    blk = lambda col: pl.BlockSpec((tm, c), lambda i: (i, col))
    vec = pl.BlockSpec((1, c), lambda i: (0, 0))
    blocked = pl.BlockSpec((HEADS, tm // t, HEAD_DIM, t), lambda i: (0, i, 0, 0))
    return pl.pallas_call(
        body,
        name=name,
        grid=(s // tm,),
        in_specs=[blk(col0), blk(col0 + 1), pl.BlockSpec((HEADS, HEAD_DIM, tm), lambda i: (0, 0, i)), blocked, blocked,
                  vec, vec, pl.BlockSpec((c, c), lambda i: (0, 0))],
        out_specs=[pl.BlockSpec((tm, 3 * c), lambda i: (i, 0)), pl.BlockSpec((2, 8, c), lambda i: (0, 0, 0))],
        out_shape=[jax.ShapeDtypeStruct((s, 3 * c), MXU_DTYPE), jax.ShapeDtypeStruct((2, 8, c), F32)],
        compiler_params=_params(),
    )(p, p, dqt, dkt, dvt, qg, kg, headavg)


def _place():
    x, y, c = lax.axis_index("x"), lax.axis_index("y"), lax.axis_index("c")
    chips = [(1 - x, y), (x, 1 - y), (1 - x, 1 - y)]
    return x, y, c, chips


def _shard_of(ref, axis, chip, width):
    idx = [slice(None)] * len(ref.shape)
    idx[axis] = pl.ds(chip * width, width)
    return ref.at[tuple(idx)]


def _place_shard(w, layer, axis, chip_idx, dtype, *, name):
    rows, cols = _as_rows(w.shape[1:])
    tr = _row_tile(rows, cols, 2)
    if axis == len(w.shape) - 2:
        out_shape = (1, rows, cols * N_CHIPS)
        out_spec = pl.BlockSpec((None, tr, cols), lambda i, j_ref: (0, i, j_ref[0]))
    else:
        assert axis == 0
        per = rows // tr
        out_shape = (1, rows * N_CHIPS, cols)
        out_spec = pl.BlockSpec((None, tr, cols), lambda i, j_ref: (0, j_ref[0] * per + i, 0))
    full = [1, *w.shape[1:]]
    full[1 + axis] *= N_CHIPS

    def body(j_ref, w_ref, o_ref):
        o_ref[...] = w_ref[...].astype(dtype)

    out = pl.pallas_call(
        body,
        name=name,
        grid_spec=pltpu.PrefetchScalarGridSpec(
            num_scalar_prefetch=1,
            grid=(rows // tr,),
            in_specs=[pl.BlockSpec((None, tr, cols), lambda i, j_ref: (layer, i, 0))],
            out_specs=out_spec,
        ),
        out_shape=jax.ShapeDtypeStruct(out_shape, dtype),
        compiler_params=_params(),
    )(chip_idx, w.reshape(w.shape[0], rows, cols))
    return out.reshape(full)


def _gather_riders(placed, axes, layer):
    n = len(placed)
    widths = [pa.shape[1 + ax] // N_CHIPS for pa, ax in zip(placed, axes)]

    def copies(refs, sems):
        x, y, c, chips = _place()

        def block(a, chip):
            return _shard_of(refs[a].at[0], axes[a], chip, widths[a])

        def over_ici(a, k, chip):
            return pltpu.make_async_remote_copy(
                src_ref=block(a, chip), dst_ref=block(a, chip), send_sem=sems[0].at[a, k], recv_sem=sems[1].at[a, k],
                device_id=(*chips[k], c), device_id_type=MESH)

        def to_sibling(a, k):
            cx, cy = chips[k]
            return pltpu.make_async_remote_copy(
                src_ref=block(a, 2 * cx + cy), dst_ref=block(a, 2 * cx + cy), send_sem=sems[0].at[a, k],
                recv_sem=sems[1].at[a, k], device_id=(x, y, 1 - c), device_id_type=MESH)

        return 2 * x + y, c, chips, over_ici, to_sibling

    def ici_start(refs, sems):
        me, c, _, over_ici, _ = copies(refs, sems)

        @pl.when(c == layer)
        def _():
            for a in range(n):
                for k in range(3):
                    over_ici(a, k, me).start()

    def ici_finish(refs, sems):
        me, c, chips, over_ici, _ = copies(refs, sems)

        @pl.when(c == layer)
        def _():
            for a in range(n):
                for k in range(3):
                    cx, cy = chips[k]
                    over_ici(a, k, 2 * cx + cy).wait_recv()
            for a in range(n):
                for k in range(3):
                    over_ici(a, k, me).wait_send()

    def d2d_start(refs, sems):
        _, c, _, _, to_sibling = copies(refs, sems)

        @pl.when(c == layer)
        def _():
            for a in range(n):
                for k in range(3):
                    to_sibling(a, k).start()

    def d2d_finish(refs, sems):
        _, c, _, _, to_sibling = copies(refs, sems)

        @pl.when(c == layer)
        def _():
            for a in range(n):
                for k in range(3):
                    to_sibling(a, k).wait_send()

        @pl.when(c != layer)
        def _():
            for a in range(n):
                for k in range(3):
                    to_sibling(a, k).wait_recv()

    arrays, sems = tuple(placed), ((n, 3), (n, 3))
    return _Rider(arrays, sems, ici_start, ici_finish), _Rider(arrays, sems, d2d_start, d2d_finish)


def _exchange_rider(grads, others, layer):
    n = len(grads)

    def copy(refs, sems, a):
        x, y, c, _ = _place()
        return pltpu.make_async_remote_copy(
            src_ref=refs[a], dst_ref=refs[n + a], send_sem=sems[0].at[a], recv_sem=sems[1].at[a],
            device_id=(x, y, 1 - c), device_id_type=MESH)

    def start(refs, sems):
        @pl.when(lax.axis_index("c") != layer)
        def _():
            for a in range(n):
                copy(refs, sems, a).start()

    def finish(refs, sems):
        @pl.when(lax.axis_index("c") != layer)
        def _():
            for a in range(n):
                copy(refs, sems, a).wait_send()

        @pl.when(lax.axis_index("c") == layer)
        def _():
            for a in range(n):
                copy(refs, sems, a).wait_recv()

    return _Rider((*grads, *others), ((n,), (n,)), start, finish)


def _scatter_rider(wires, landeds, axes, layer):
    n = len(wires)
    widths = [wr.shape[ax] // N_CHIPS for wr, ax in zip(wires, axes)]

    def copies(refs, sems):
        _, _, c, chips = _place()
        return [pltpu.make_async_remote_copy(
            src_ref=_shard_of(refs[a], axes[a], 2 * cx + cy, widths[a]), dst_ref=refs[n + a].at[k],
            send_sem=sems[0].at[a, k], recv_sem=sems[1].at[a, k], device_id=(cx, cy, c), device_id_type=MESH)
            for a in range(n) for k, (cx, cy) in enumerate(chips)]

    def start(refs, sems):
        @pl.when(lax.axis_index("c") == layer)
        def _():
            for cp in copies(refs, sems):
                cp.start()

    def finish(refs, sems):
        @pl.when(lax.axis_index("c") == layer)
        def _():
            for cp in copies(refs, sems):
                cp.wait()

    return _Rider((*wires, *landeds), ((n, 3), (n, 3)), start, finish)


def _share_rider(finals, layer):
    n = len(finals)

    def copy(refs, sems, a):
        x, y, c, _ = _place()
        return pltpu.make_async_remote_copy(
            src_ref=refs[a].at[layer], dst_ref=refs[a].at[layer], send_sem=sems[0].at[a], recv_sem=sems[1].at[a],
            device_id=(x, y, 1 - c), device_id_type=MESH)

    def start(refs, sems):
        @pl.when(lax.axis_index("c") == layer)
        def _():
            for a in range(n):
                copy(refs, sems, a).start()

    def finish(refs, sems):
        @pl.when(lax.axis_index("c") == layer)
        def _():
            for a in range(n):
                copy(refs, sems, a).wait_send()

        @pl.when(lax.axis_index("c") != layer)
        def _():
            for a in range(n):
                copy(refs, sems, a).wait_recv()

    return _Rider(tuple(finals), ((n,), (n,)), start, finish)


def _small_view(shape):
    size = math.prod(shape)
    return (size // 128, 128) if size % 1024 == 0 else (shape[0], size // shape[0])


def _all_reduce_small(parts, *, name):
    n = len(parts)
    views = [_small_view(a.shape) for a in parts]

    def body(*refs):
        ins, outs, slots = refs[:n], refs[n:2 * n], refs[2 * n:3 * n]
        send_sem, recv_sem = refs[3 * n:]
        x, y, c, _ = _place()
        me = 4 * x + 2 * y + c
        copies = []
        for a in range(n):
            slots[a][me] = ins[a][...]
            for k in range(1, N_DEV):
                peer = (x ^ (k >> 2), y ^ ((k >> 1) & 1), c ^ (k & 1))
                copies.append(pltpu.make_async_remote_copy(
                    src_ref=ins[a], dst_ref=slots[a].at[me], send_sem=send_sem.at[a, k - 1],
                    recv_sem=recv_sem.at[a, k - 1], device_id=peer, device_id_type=MESH))
        for cp in copies:
            cp.start()
        for a in range(n):
            for k in range(1, N_DEV):
                pltpu.make_async_remote_copy(
                    src_ref=ins[a], dst_ref=slots[a].at[me ^ k], send_sem=send_sem.at[a, k - 1],
                    recv_sem=recv_sem.at[a, k - 1], device_id=(x, y, c), device_id_type=MESH).wait()
        for a in range(n):
            total = slots[a][0]
            for dev in range(1, N_DEV):
                total = total + slots[a][dev]
            outs[a][...] = total

    vmem = pl.BlockSpec(memory_space=pltpu.VMEM)
    outs = pl.pallas_call(
        body,
        name=name,
        in_specs=[vmem] * n,
        out_specs=[vmem] * n,
        out_shape=[jax.ShapeDtypeStruct(view, F32) for view in views],
        scratch_shapes=[pltpu.VMEM((N_DEV, *view), F32) for view in views]
        + [pltpu.SemaphoreType.DMA((n, N_DEV - 1)), pltpu.SemaphoreType.DMA((n, N_DEV - 1))],
        compiler_params=_params(),
    )(*[a.reshape(view) for a, view in zip(parts, views)])
    return [o.reshape(a.shape) for o, a in zip(outs, parts)]


def _as_rows(shape):
    cols = shape[-1]
    return math.prod(shape[:-1]), cols


ELEMENTWISE_VMEM = 24 * 1024 * 1024


def _row_tile(rows, cols, n_arrays, sublanes=8):
    cap = ELEMENTWISE_VMEM // (n_arrays * 2 * 4 * cols)
    best = None
    for t in range(sublanes, min(rows, cap) + 1, sublanes):
        if rows % t == 0:
            best = t
    assert best is not None, (rows, cols)
    return best


def _pair_sum(g, other, *, name):
    rows, cols = _as_rows(g.shape)
    tr = _row_tile(rows, cols, 4, sublanes=16)

    def body(g_ref, o_ref, out_ref, wire_ref):
        total = g_ref[...] + o_ref[...]
        out_ref[...] = total
        wire_ref[...] = total.astype(WIRE_DTYPE)

    blk = pl.BlockSpec((tr, cols), lambda i: (i, 0))
    out, wire = pl.pallas_call(
        body,
        name=name,
        grid=(rows // tr,),
        in_specs=[blk, blk],
        out_specs=[blk, blk],
        out_shape=[jax.ShapeDtypeStruct((rows, cols), F32), jax.ShapeDtypeStruct((rows, cols), WIRE_DTYPE)],
        compiler_params=_params(),
    )(g.reshape(rows, cols), other.reshape(rows, cols))
    return out.reshape(g.shape), wire.reshape(g.shape)


def _chip_sum(mine, got, axis, chip_idx, layer, layers, into, *, name):
    shard_shape = got.shape[1:]
    rows, cols = _as_rows(shard_shape)
    tr = _row_tile(rows, cols, 5, sublanes=16)
    if axis == len(mine.shape) - 1:
        m2 = mine.reshape(rows, cols * N_CHIPS)
        mine_spec = pl.BlockSpec((tr, cols), lambda i, j_ref: (i, j_ref[0]))
    else:
        assert axis == 0
        m2 = mine.reshape(N_CHIPS, rows, cols)
        mine_spec = pl.BlockSpec((None, tr, cols), lambda i, j_ref: (j_ref[0], i, 0))

    def body(j_ref, m_ref, got_ref, *refs):
        refs[-1][...] = ((m_ref[...] + got_ref[0].astype(F32)) + got_ref[1].astype(F32)) + got_ref[2].astype(F32)

    out = pl.pallas_call(
        body,
        name=name,
        grid_spec=pltpu.PrefetchScalarGridSpec(
            num_scalar_prefetch=1,
            grid=(rows // tr,),
            in_specs=[mine_spec, pl.BlockSpec((3, tr, cols), lambda i, j_ref: (0, i, 0))] + ([] if into is None else [ANY]),
            out_specs=pl.BlockSpec((None, tr, cols), lambda i, j_ref: (layer, i, 0)),
        ),
        out_shape=jax.ShapeDtypeStruct((layers, rows, cols), F32),
        input_output_aliases={} if into is None else {3: 0},
        compiler_params=_params(),
    )(chip_idx, m2, got.reshape(3, rows, cols), *([] if into is None else [into.reshape(layers, rows, cols)]))
    return out.reshape((layers, *shard_shape))


class _ReduceChain:
    def __init__(self, grads, layer, layers, chip_idx, finals):
        self.grads, self.layer, self.layers, self.chip_idx, self.finals = list(grads), layer, layers, chip_idx, finals
        self.axes = [BIG_AXIS[n] for n in BIG]
        self.pairs = None

    def exchange(self):
        return _exchange_rider(self.grads, [lax.empty(g.shape, F32) for g in self.grads], self.layer)

    def exchanged(self, arrived):
        n = len(self.grads)
        self.pairs = [_pair_sum(g, o, name=f"pair_sum_{name}_l{self.layer}")
                      for name, g, o in zip(BIG, arrived[:n], arrived[n:])]

    def scatter(self):
        landeds = []
        for (_, wire), ax in zip(self.pairs, self.axes):
            shard = list(wire.shape)
            shard[ax] //= N_CHIPS
            landeds.append(lax.empty((3, *shard), WIRE_DTYPE))
        return _scatter_rider([wire for _, wire in self.pairs], landeds, self.axes, self.layer)

    def scattered(self, arrived):
        n = len(self.grads)
        into = self.finals or [None] * n
        self.finals = [_chip_sum(pair, landed, ax, self.chip_idx, self.layer, self.layers, old,
                                 name=f"chip_sum_{name}_l{self.layer}")
                       for name, (pair, _), landed, ax, old in zip(BIG, self.pairs, arrived[n:], self.axes, into)]

    def share(self):
        return _share_rider(self.finals, self.layer)

    def shared(self, arrived):
        self.finals = list(arrived)

    def run_alone(self):
        tag = f"l{self.layer}"
        self.exchanged(_ride_alone([self.exchange()], name=f"grads_pair_exchange_{tag}"))
        self.scattered(_ride_alone([self.scatter()], name=f"grads_chip_scatter_{tag}"))
        self.shared(_ride_alone([self.share()], name=f"grads_pair_share_{tag}"))


def _adamw_update(w_ref, g_ref, m_ref, v_ref, d_ref, nm_ref, nv_ref):
    c1 = 1.0 / (1.0 - ADAM_B1 ** ADAM_STEP)
    c2 = 1.0 / (1.0 - ADAM_B2 ** ADAM_STEP)
    gg = g_ref[...]
    nm = ADAM_B1 * m_ref[...] + (1.0 - ADAM_B1) * gg
    nv = ADAM_B2 * v_ref[...] + (1.0 - ADAM_B2) * (gg * gg)
    nm_ref[...] = nm
    nv_ref[...] = nv
    d_ref[...] = -ADAM_LR * ((nm * c1) / (jnp.sqrt(nv * c2) + ADAM_EPS) + ADAM_WD * w_ref[...])


def _adamw_small(ws, gs, ms, vs, *, name):
    n = len(ws)
    views = [_small_view(a.shape) for a in ws]

    def body(*refs):
        w_r, g_r, m_r, v_r = (refs[k * n:(k + 1) * n] for k in range(4))
        d_r, nm_r, nv_r = (refs[(4 + k) * n:(5 + k) * n] for k in range(3))
        for a in range(n):
            _adamw_update(w_r[a], g_r[a], m_r[a], v_r[a], d_r[a], nm_r[a], nv_r[a])

    vmem = pl.BlockSpec(memory_space=pltpu.VMEM)
    flat = lambda arrs: [a.reshape(view) for a, view in zip(arrs, views)]
    outs = pl.pallas_call(
        body,
        name=name,
        in_specs=[vmem] * (4 * n),
        out_specs=[vmem] * (3 * n),
        out_shape=[jax.ShapeDtypeStruct(view, F32) for view in views] * 3,
        compiler_params=_params(),
    )(*flat(ws), *flat(gs), *flat(ms), *flat(vs))
    return [tuple(outs[k * n + a].reshape(ws[a].shape) for k in range(3)) for a in range(n)]


def _adamw(w, g, m, v, *, name):
    shape = w.shape
    rows, cols = _as_rows(shape)
    tr = _row_tile(rows, cols, 7)
    body = functools.partial(_adamw_update)

    blk = pl.BlockSpec((tr, cols), lambda i: (i, 0))
    flat = lambda a: a.reshape(rows, cols)
    outs = pl.pallas_call(
        body,
        name=name,
        grid=(rows // tr,),
        in_specs=[blk] * 4,
        out_specs=[blk] * 3,
        out_shape=[jax.ShapeDtypeStruct((rows, cols), F32)] * 3,
        compiler_params=_params(),
    )(flat(w), flat(g), flat(m), flat(v))
    return tuple(o.reshape(shape) for o in outs)


def _cols(a):
    return a.reshape(a.shape[0], HEADS, HEAD_DIM).transpose(1, 2, 0)


def _from_cols(a):
    h, d, s = a.shape
    return a.transpose(2, 0, 1).reshape(s, h * d)


BIG = ("w_in", "w_branch_out", "w_o", "w_gate_up", "w_down")
BIG_AXIS = {"w_in": 1, "w_branch_out": 2, "w_o": 0, "w_gate_up": 1, "w_down": 0}
SMALL = ("mix_norm_g", "b_gate", "conv_w", "conv_b", "sgu_ln_g", "sgu_ln_b", "sgu_w", "sgu_b", "q_norm_g", "k_norm_g",
         "ffn_norm_g")
ORDER = ("mix_norm_g", "w_in", "b_gate", "conv_w", "conv_b", "sgu_ln_g", "sgu_ln_b", "sgu_w", "sgu_b", "q_norm_g",
         "k_norm_g", "w_branch_out", "w_o", "ffn_norm_g", "w_gate_up", "w_down")


def _layer_forward(x, w, wl, l, coming):
    n_in = wl["w_in"].shape[2]
    gate_col0 = (n_in - 3 * x.shape[1]) // x.shape[1]
    tag = f"l{l}"
    over_ici = _gather_riders(*coming)[0] if coming else None
    (p, h), arrived = _norm_matmul(x, w["mix_norm_g"][l][None], wl["w_in"], 0, name=f"in_proj_{tag}", tn=n_in // 7,
                                   rider=over_ici)
    ya = _conv_fwd(p, wl["conv_w"][0], w["conv_b"][l][None], name=f"conv_{tag}")
    yb = _sgu_fwd(p, w["sgu_ln_g"][l][None], w["sgu_ln_b"][l][None], w["wtril"][l], w["bias_full"][l], col0=3,
                  name=f"sgu_{tag}")
    q_c, k_cb, v_cb = _qkv_prep(p, w["qg"][l], w["kg"][l], w["headavg"], col0=5, name=f"qkv_{tag}")
    to_sibling = _gather_riders(list(arrived), *coming[1:])[1] if coming else None
    (out_t, runs), arrived = _att_fwd(k_cb, q_c, v_cb, w["later"], name=f"att_{tag}", rider=to_sibling)
    yc = _from_cols(out_t)
    x1, merged = _merge_fwd((ya, yb, yc), p, w["b_gate"][l][None], wl["w_branch_out"], wl["w_o"], 0, x,
                            gate_col0=gate_col0, name=f"merge_{tag}")
    (gu, h2), _ = _norm_matmul(x1, w["ffn_norm_g"][l][None], wl["w_gate_up"], 0, name=f"gate_up_{tag}",
                               tn=wl["w_gate_up"].shape[2] // 4)
    x2 = _ffn_down(gu, wl["w_down"], 0, x1, name=f"down_{tag}")
    saved = dict(x=x, p=p, h=h, ya=ya, yb=yb, yc=yc, merged=merged, x1=x1, gu=gu, h2=h2,
                 att=(k_cb, v_cb, q_c, runs), gate_col0=gate_col0)
    return x2, saved, arrived


def _layer_backward(dx2, w, wl, sv, l, chain):
    tag = f"l{l}"
    c = w["conv_b"].shape[1]
    n_in = wl["w_in"].shape[2]
    n_ff = wl["w_gate_up"].shape[2]
    g, big = {}, {}
    (dgu, act, dx2b), arrived = _ffn_bwd(dx2, sv["gu"], wl["w_down"], 0, name=f"down_bwd_{tag}",
                                         rider=chain.exchange() if chain else None)
    if chain:
        chain.exchanged(arrived)
    big["w_down"] = _matmul_tn(act, dx2b, 0, 1, None, name=f"dw_down_{tag}", t1=act.shape[1] // 2)
    (dx1, dg2), _ = _matmul_nt_normbwd([dgu], wl["w_gate_up"], 0, sv["x1"], w["ffn_norm_g"][l][None], dx2,
                                       name=f"gate_up_bwd_{tag}")
    g["ffn_norm_g"] = jnp.sum(dg2, axis=0)
    big["w_gate_up"] = _matmul_tn(sv["h2"], dgu, 0, 1, None, name=f"dw_gate_up_{tag}", tn=n_ff // 4)
    ys = (sv["ya"], sv["yb"], sv["yc"])
    (dgates, dya, dyb, dyc, dd0, dd1, dd2, dx1b, dbg) = _merge_bwd(
        dx1, ys, sv["p"], w["b_gate"][l][None], wl["w_branch_out"], wl["w_o"], 0, gate_col0=sv["gate_col0"],
        name=f"merge_bwd_{tag}")
    g["b_gate"] = jnp.sum(dbg, axis=0)
    big["w_o"] = _matmul_tn(sv["merged"], dx1b, 0, 1, None, name=f"dw_o_{tag}")
    for i, (y, dd) in enumerate(zip(ys, (dd0, dd1, dd2))):
        big["w_branch_out"] = _matmul_tn(y, dd, i, len(ys), big.get("w_branch_out"), name=f"dw_bo{i}_{tag}")
    dconv, dwc = _conv_bwd(sv["p"], dya, wl["conv_w"][0], w["conv_b"][l][None], name=f"conv_bwd_{tag}")
    dwc = jnp.sum(dwc, axis=1)
    g["conv_w"] = dwc[0:CONV_K]
    g["conv_b"] = dwc[CONV_K]
    dsgu, dln, dws, dbias = _sgu_bwd(sv["p"], dyb, w["sgu_ln_g"][l][None], w["sgu_ln_b"][l][None], w["wtril"][l],
                                     w["wtril_t"][l], w["bias_full"][l], col0=3, name=f"sgu_bwd_{tag}")
    dln = jnp.sum(dln, axis=1)
    g["sgu_ln_g"], g["sgu_ln_b"] = dln[0], dln[1]
    g["sgu_w"] = jnp.where(w["tril"], dws, 0.0)
    g["sgu_b"] = jnp.sum(dbias.reshape(CHUNK, c // CHUNK, CHUNK), axis=2).T
    k_cb, v_cb, q_c, runs = sv["att"]
    (dqt, dkt, dvt), arrived = _att_bwd(k_cb, v_cb, q_c, _cols(_mx(dyc)), runs, w["later"], w["earlier"],
                                        name=f"att_bwd_{tag}", rider=chain.scatter() if chain else None)
    if chain:
        chain.scattered(arrived)
    dqkv, dqkg = _qkv_bwd(sv["p"], dqt, dkt, dvt, w["qg"][l], w["kg"][l], w["headavg"], col0=5, name=f"qkv_bwd_{tag}")
    dqkg = jnp.sum(dqkg.reshape(2, 8 * HEADS, HEAD_DIM), axis=1)
    g["q_norm_g"], g["k_norm_g"] = dqkg[0], dqkg[1]
    pieces = [dconv, dsgu, dqkv, dgates]
    (dx0, dg1), arrived = _matmul_nt_normbwd(pieces, wl["w_in"], 0, sv["x"], w["mix_norm_g"][l][None], dx1,
                                             name=f"in_proj_bwd_{tag}", rider=chain.share() if chain else None)
    if chain:
        chain.shared(arrived)
    g["mix_norm_g"] = jnp.sum(dg1, axis=0)
    col0 = 0
    for k, pc in enumerate(pieces):
        width = pc.shape[1]
        big["w_in"] = _matmul_tn(sv["h"], pc, 0, 1, big.get("w_in"), name=f"dw_in{k}_{tag}", col0=col0, n_total=n_in,
                                 tn=math.gcd(col0, width) if col0 else width)
        col0 += width
    return dx0, g, {n: a.reshape(wl[n].shape[1:]) for n, a in big.items()}


def kernel(x, mix_norm_g, w_in, b_gate, conv_w, conv_b, sgu_ln_g, sgu_ln_b, sgu_w, sgu_b, q_norm_g, k_norm_g, w_branch_out, w_o, ffn_norm_g, w_gate_up, w_down, loss_target, m_mix_norm_g, m_w_in, m_b_gate, m_conv_w, m_conv_b, m_sgu_ln_g, m_sgu_ln_b, m_sgu_w, m_sgu_b, m_q_norm_g, m_k_norm_g, m_w_branch_out, m_w_o, m_ffn_norm_g, m_w_gate_up, m_w_down, v_mix_norm_g, v_w_in, v_b_gate, v_conv_w, v_conv_b, v_sgu_ln_g, v_sgu_ln_b, v_sgu_w, v_sgu_b, v_q_norm_g, v_k_norm_g, v_w_branch_out, v_w_o, v_ffn_norm_g, v_w_gate_up, v_w_down):
    given = dict(locals())
    params = {n: given[n] for n in ORDER}
    moms = {n: (given["m_" + n], given["v_" + n]) for n in ORDER}
    layers = mix_norm_g.shape[0]
    assert layers == 2, "the exchanges split the work of a chip's two cores by layer"
    xs = x[0]
    target = loss_target[0]
    chip = 2 * lax.axis_index("x") + lax.axis_index("y")
    core = lax.axis_index("c")

    c_idx = core.reshape(1).astype(jnp.int32)
    j_idx = chip.reshape(1).astype(jnp.int32)
    width = conv_w.shape[2]
    gathered = BIG + ("conv_w",)
    axes = [BIG_AXIS[n] for n in BIG] + [1]

    def placed(layer):
        arrays = [_place_shard(params[n], layer, BIG_AXIS[n], j_idx, MXU_DTYPE, name=f"place_{n}_l{layer}") for n in BIG]
        mine = lax.dynamic_slice_in_dim(conv_w, layer, 1, axis=0)
        arrays.append(lax.dynamic_update_slice_in_dim(jnp.zeros((1, CONV_K, width * N_CHIPS), F32), mine, chip * width, 2))
        return arrays

    weights = [dict(zip(gathered, _ride_alone(_gather_riders(placed(0), axes, 0), name="gather_weights_l0"))), None]
    w = {n: params[n] for n in ("mix_norm_g", "b_gate", "conv_b", "sgu_ln_g", "sgu_ln_b", "ffn_norm_g")}
    groups = sgu_w.shape[1]
    tril = jnp.tril(jnp.ones((CHUNK, CHUNK), dtype=bool))
    w["tril"] = tril
    w["wtril"] = _mx(jnp.where(tril, sgu_w, 0.0))
    w["wtril_t"] = w["wtril"].transpose(0, 1, 3, 2)
    w["bias_full"] = jnp.repeat(sgu_b.transpose(0, 2, 1), CHUNK, axis=2)
    w["qg"] = jnp.tile(q_norm_g, (1, HEADS))[:, None, :]
    w["kg"] = jnp.tile(k_norm_g, (1, HEADS))[:, None, :]
    lane = jnp.arange(HEADS * HEAD_DIM) // HEAD_DIM
    w["headavg"] = _mx(jnp.where(lane[:, None] == lane[None, :], 1.0 / HEAD_DIM, 0.0))
    pos = jnp.arange(ATT_BLOCK)
    w["later"] = _mx(jnp.where(pos[None, :] > pos[:, None], 1.0, 0.0))
    w["earlier"] = _mx(jnp.where(pos[None, :] < pos[:, None], 1.0, 0.0))

    saved = []
    act = xs
    for l in range(layers):
        coming = (placed(l + 1), axes, l + 1) if l + 1 < layers else None
        act, sv, arrived = _layer_forward(act, w, weights[l], l, coming)
        saved.append(sv)
        if coming:
            weights[l + 1] = dict(zip(gathered, arrived))
    loss_part, dact = _loss_head(act, target, name="loss_head")
    loss = lax.psum(jnp.sum(loss_part), ("x", "y", "c"))
    grads = [None] * layers
    chain = None
    for l in reversed(range(layers)):
        dact, grads[l], big = _layer_backward(dact, w, weights[l], saved[l], l, chain)
        chain = _ReduceChain([big[n] for n in BIG], l, layers, j_idx, chain.finals if chain else None)
    chain.run_alone()
    grad_x = dact[None]
    local = {n: jnp.stack([grads[l][n] for l in range(layers)]) for n in SMALL}
    full = dict(zip(BIG, chain.finals))

    summed = _all_reduce_small([local[n] for n in SMALL], name="grads_all_reduce_small")
    for n, gsum in zip(SMALL, summed):
        full[n] = gsum
    full["conv_w"] = lax.dynamic_slice_in_dim(full["conv_w"], chip * width, width, axis=2)

    out = {}
    for n in BIG:
        out[n] = _adamw(params[n], full[n], *moms[n], name=f"adamw_{n}")
    small = _adamw_small([params[n] for n in SMALL], [full[n] for n in SMALL], [moms[n][0] for n in SMALL],
                         [moms[n][1] for n in SMALL], name="adamw_small")
    for n, triple in zip(SMALL, small):
        out[n] = triple
    return (loss, grad_x, *[full[n] for n in ORDER], *[out[n][0] for n in ORDER], *[out[n][1] for n in ORDER],
            *[out[n][2] for n in ORDER])
```

```python
import functools
import math
from typing import Callable, NamedTuple

import jax
import jax.numpy as jnp
from jax import lax
from jax.experimental import pallas as pl
from jax.experimental.pallas import tpu as pltpu

F32 = jnp.float32
MXU_DTYPE = jnp.bfloat16
WIRE_DTYPE = jnp.bfloat16
ACT_DTYPE = jnp.bfloat16
HALO = 16

EPS = 1e-6
CONV_K = 3
CHUNK = 128
HEADS = 8
HEAD_DIM = 64
ATT_BLOCK = 128
EXP_UNDERFLOW = -88.0

ADAM_LR = 0.001
ADAM_B1 = 0.9
ADAM_B2 = 0.999
ADAM_EPS = 1e-08
ADAM_WD = 0.01
ADAM_STEP = 10

VMEM_LIMIT = 56 * 1024 * 1024
MESH = pl.DeviceIdType.MESH
N_CHIPS = 4
N_DEV = 8
ANY = pl.BlockSpec(memory_space=pl.ANY)


def _params(**kw):
    return pltpu.CompilerParams(vmem_limit_bytes=VMEM_LIMIT, **kw)


class _Rider(NamedTuple):
    arrays: tuple
    sems: tuple
    start: Callable
    finish: Callable


def _call(body, operands, *, name, grid, in_specs, out_specs, out_shape, scratch_shapes=(), rider=None):
    if rider is None:
        outs = pl.pallas_call(body, name=name, grid=grid, in_specs=in_specs, out_specs=out_specs, out_shape=out_shape,
                              scratch_shapes=list(scratch_shapes), compiler_params=_params())(*operands)
        return tuple(outs), ()
    n_in, n_out, n_scr, k = len(in_specs), len(out_specs), len(scratch_shapes), len(rider.arrays)

    def riding(*refs):
        ins, refs = refs[:n_in], refs[n_in + k:]
        outs, carried, refs = refs[:n_out], refs[n_out:n_out + k], refs[n_out + k:]
        scratch, sems = refs[:n_scr], refs[n_scr:]
        first = functools.reduce(jnp.logical_and, [pl.program_id(ax) == 0 for ax in range(len(grid))])
        last = functools.reduce(jnp.logical_and, [pl.program_id(ax) == grid[ax] - 1 for ax in range(len(grid))])

        @pl.when(first)
        def _():
            rider.start(carried, sems)

        body(*ins, *outs, *scratch)

        @pl.when(last)
        def _():
            rider.finish(carried, sems)

    res = pl.pallas_call(
        riding, name=name, grid=grid, in_specs=[*in_specs, *[ANY] * k], out_specs=[*out_specs, *[ANY] * k],
        out_shape=[*out_shape, *[jax.ShapeDtypeStruct(a.shape, a.dtype) for a in rider.arrays]],
        input_output_aliases={n_in + j: n_out + j for j in range(k)},
        scratch_shapes=[*scratch_shapes, *[pltpu.SemaphoreType.DMA(shape) for shape in rider.sems]],
        compiler_params=_params())(*operands, *rider.arrays)
    return tuple(res[:n_out]), tuple(res[n_out:])


def _ride_alone(riders, *, name):
    arrays = riders[0].arrays
    k = len(arrays)
    counts = [len(r.sems) for r in riders]

    def body(*refs):
        carried, sems = refs[k:2 * k], refs[2 * k:]
        for r, count in zip(riders, counts):
            r.start(carried, sems[:count])
            r.finish(carried, sems[:count])
            sems = sems[count:]

    return pl.pallas_call(
        body, name=name, in_specs=[ANY] * k, out_specs=[ANY] * k,
        out_shape=[jax.ShapeDtypeStruct(a.shape, a.dtype) for a in arrays],
        input_output_aliases={j: j for j in range(k)},
        scratch_shapes=[pltpu.SemaphoreType.DMA(shape) for r in riders for shape in r.sems])(*arrays)


def _mx(v):
    return v.astype(MXU_DTYPE)


def _dot(a, b):
    return lax.dot_general(a, b, (((1,), (0,)), ((), ())), preferred_element_type=F32)


def _dot_nt(a, b):
    return lax.dot_general(a, b, (((1,), (1,)), ((), ())), preferred_element_type=F32)


def _dot_tn(a, b):
    return lax.dot_general(a, b, (((0,), (0,)), ((), ())), preferred_element_type=F32)


def _dot_split(const, v):
    hi = _mx(v)
    lo = _mx(v - hi.astype(F32))
    return _dot(const, hi) + _dot(const, lo)


def _dot_split_r(v, const):
    hi = _mx(v)
    lo = _mx(v - hi.astype(F32))
    return _dot(hi, const) + _dot(lo, const)


def _sigmoid(x):
    return 1.0 / (1.0 + jnp.exp(-x))


_INV_SQRT2 = 1.0 / math.sqrt(2.0)
_INV_SQRT2PI = 1.0 / math.sqrt(2.0 * math.pi)


def _gelu(x):
    return 0.5 * x * (1.0 + lax.erf(x * _INV_SQRT2))


def _gelu_grad(x):
    return 0.5 * (1.0 + lax.erf(x * _INV_SQRT2)) + x * jnp.exp(-0.5 * x * x) * _INV_SQRT2PI


def _row_fold(v):
    m, n = v.shape
    return jnp.sum(v.reshape(m // 8, 8, n), axis=0)


def _tile(m, pref):
    t = min(m, pref)
    while m % t:
        t //= 2
    return t


def _resident(block_shape, index_map):
    return pl.BlockSpec(block_shape, index_map, pipeline_mode=pl.Buffered(1))


def _norm_matmul(x, g, w, l, *, name, tm=512, tn=None, rider=None):
    s, d = x.shape
    n = w.shape[2]
    tm = _tile(s, tm)
    tn = tn or n
    assert n % tn == 0

    def body(x_ref, g_ref, w_ref, p_ref, h_ref):
        xf = x_ref[...]
        r = lax.rsqrt(jnp.mean(xf * xf, axis=-1, keepdims=True) + EPS)
        h = _mx(xf * r * g_ref[...])
        h_ref[...] = h
        for j in range(n // tn):
            cols = slice(j * tn, (j + 1) * tn)
            p_ref[:, cols] = _dot(h, w_ref[:, cols]).astype(ACT_DTYPE)

    return _call(
        body, (x, g, w),
        name=name,
        grid=(s // tm,),
        in_specs=[
            pl.BlockSpec((tm, d), lambda i: (i, 0)),
            pl.BlockSpec((1, d), lambda i: (0, 0)),
            _resident((None, d, n), lambda i: (l, 0, 0)),
        ],
        out_specs=[
            pl.BlockSpec((tm, n), lambda i: (i, 0)),
            pl.BlockSpec((tm, d), lambda i: (i, 0)),
        ],
        out_shape=[jax.ShapeDtypeStruct((s, n), ACT_DTYPE), jax.ShapeDtypeStruct((s, d), MXU_DTYPE)],
        rider=rider,
    )


def _conv_taps(u, prev):
    row = lax.broadcasted_iota(jnp.int32, u.shape, 0)
    last, before = prev[HALO - 1:HALO, :], prev[HALO - 2:HALO - 1, :]
    um1 = jnp.where(row == 0, last, pltpu.roll(u, 1, 0))
    um2 = pltpu.roll(u, 2, 0)
    um2 = jnp.where(row == 0, before, jnp.where(row == 1, last, um2))
    return um1, um2


def _f32(ref):
    return ref[...].astype(F32)


def _conv_fwd(p, conv_w, conv_b, *, name, tm=512):
    s = p.shape[0]
    c = conv_w.shape[1]
    tm = _tile(s, tm)
    hb = tm // HALO

    def body(bg_ref, cg_ref, xa_ref, cgp_ref, xap_ref, w_ref, b_ref, y_ref):
        first = pl.program_id(0) == 0
        u = _f32(cg_ref) * _f32(xa_ref)
        prev = jnp.where(first, 0.0, _f32(cgp_ref) * _f32(xap_ref))
        um1, um2 = _conv_taps(u, prev)
        w = w_ref[...]
        y = b_ref[...] + w[0:1, :] * um2 + w[1:2, :] * um1 + w[2:3, :] * u
        y_ref[...] = _mx(_f32(bg_ref) * y)

    halo = lambda col: pl.BlockSpec((HALO, c), lambda i: (jnp.maximum(i * hb - 1, 0), col))
    return pl.pallas_call(
        body,
        name=name,
        grid=(s // tm,),
        in_specs=[
            pl.BlockSpec((tm, c), lambda i: (i, 0)),
            pl.BlockSpec((tm, c), lambda i: (i, 1)),
            pl.BlockSpec((tm, c), lambda i: (i, 2)),
            halo(1),
            halo(2),
            pl.BlockSpec((CONV_K, c), lambda i: (0, 0)),
            pl.BlockSpec((1, c), lambda i: (0, 0)),
        ],
        out_specs=pl.BlockSpec((tm, c), lambda i: (i, 0)),
        out_shape=jax.ShapeDtypeStruct((s, c), MXU_DTYPE),
        compiler_params=_params(),
    )(p, p, p, p, p, conv_w, conv_b)


def _layernorm_stats(x):
    mu = jnp.mean(x, axis=-1, keepdims=True)
    xc = x - mu
    r = lax.rsqrt(jnp.mean(xc * xc, axis=-1, keepdims=True) + EPS)
    return xc * r, r


def _sgu_fwd(p, ln_g, ln_b, wtril, bias_full, *, col0, name, tm=512):
    s = p.shape[0]
    c = ln_g.shape[1]
    groups = c // CHUNK
    tm = _tile(s, tm)

    def body(u_ref, v_ref, g_ref, b_ref, w_ref, bias_ref, y_ref):
        vn, _ = _layernorm_stats(_gelu(_f32(v_ref)))
        vb = _mx(vn * g_ref[...] + b_ref[...])
        for n in range(tm // CHUNK):
            rows = slice(n * CHUNK, (n + 1) * CHUNK)
            for gi in range(groups):
                cols = slice(gi * CHUNK, (gi + 1) * CHUNK)
                mixed = _dot(w_ref[gi], vb[rows, cols]) + bias_ref[:, cols]
                y_ref[rows, cols] = _mx(_gelu(u_ref[rows, cols].astype(F32)) * mixed)

    return pl.pallas_call(
        body,
        name=name,
        grid=(s // tm,),
        in_specs=[
            pl.BlockSpec((tm, c), lambda i: (i, col0)),
            pl.BlockSpec((tm, c), lambda i: (i, col0 + 1)),
            pl.BlockSpec((1, c), lambda i: (0, 0)),
            pl.BlockSpec((1, c), lambda i: (0, 0)),
            pl.BlockSpec((groups, CHUNK, CHUNK), lambda i: (0, 0, 0)),
            pl.BlockSpec((CHUNK, c), lambda i: (0, 0)),
        ],
        out_specs=pl.BlockSpec((tm, c), lambda i: (i, 0)),
        out_shape=jax.ShapeDtypeStruct((s, c), MXU_DTYPE),
        compiler_params=_params(),
    )(p, p, ln_g, ln_b, wtril, bias_full)


def _head_mean(v, headavg):
    return _dot_split_r(v, headavg)


def _qkv_prep(p, qg, kg, headavg, *, col0, name, tm=512):
    s = p.shape[0]
    c = qg.shape[1]
    t = ATT_BLOCK
    tm = _tile(s, tm)
    assert tm % t == 0

    def body(q_ref, k_ref, v_ref, qg_ref, kg_ref, avg_ref, qc_ref, kcb_ref, vcb_ref):
        normed = []
        for src, g_ref in ((q_ref, qg_ref), (k_ref, kg_ref)):
            xf = _f32(src)
            r = lax.rsqrt(_head_mean(xf * xf, avg_ref[...]) + EPS)
            normed.append(xf * r * g_ref[...])
        q_t, k_t, v_t = (_mx(a.T) for a in (*normed, _f32(v_ref)))
        for hh in range(HEADS):
            rows = slice(hh * HEAD_DIM, (hh + 1) * HEAD_DIM)
            qc_ref[hh] = q_t[rows, :]
            for b in range(tm // t):
                kcb_ref[hh, b] = k_t[rows, b * t:(b + 1) * t]
                vcb_ref[hh, b] = v_t[rows, b * t:(b + 1) * t]

    blk = lambda col: pl.BlockSpec((tm, c), lambda i: (i, col))
    vec = pl.BlockSpec((1, c), lambda i: (0, 0))
    blocked = pl.BlockSpec((HEADS, tm // t, HEAD_DIM, t), lambda i: (0, i, 0, 0))
    return pl.pallas_call(
        body,
        name=name,
        grid=(s // tm,),
        in_specs=[blk(col0), blk(col0 + 1), blk(col0 + 2), vec, vec, pl.BlockSpec((c, c), lambda i: (0, 0))],
        out_specs=[pl.BlockSpec((HEADS, HEAD_DIM, tm), lambda i: (0, 0, i)), blocked, blocked],
        out_shape=[jax.ShapeDtypeStruct((HEADS, HEAD_DIM, s), MXU_DTYPE)]
        + [jax.ShapeDtypeStruct((HEADS, s // t, HEAD_DIM, t), MXU_DTYPE)] * 2,
        compiler_params=_params(),
    )(p, p, p, qg, kg, headavg)


def _att_mask(t, key0, qry0):
    kpos = key0 + lax.broadcasted_iota(jnp.int32, (t, t), 0)
    qpos = qry0 + lax.broadcasted_iota(jnp.int32, (t, t), 1)
    return kpos < qpos


def _att_blocks(k_blks, q_ts, scale, mask):
    zs = [(_dot_tn(k, q) if k.shape[0] == q.shape[0] else _dot(k, q)) * scale for k, q in zip(k_blks, q_ts)]
    es = [jnp.exp(-jnp.abs(z)) for z in zs]
    lbs = [jnp.minimum(z, 0.0) - jnp.log1p(e) for z, e in zip(zs, es)]
    l1ms = [jnp.where(mask, lb - z, 0.0) for lb, z in zip(lbs, zs)]
    return zs, es, lbs, l1ms


def _dot_split_each(const, vs):
    his = [_mx(v) for v in vs]
    los = [_mx(v - hi.astype(F32)) for v, hi in zip(vs, his)]
    tops = [_dot(const, hi) for hi in his]
    return [top + _dot(const, lo) for top, lo in zip(tops, los)]


def _max_over(runs):
    m = runs[0]
    for r in runs[1:]:
        m = jnp.maximum(m, r)
    return jnp.max(m)


NOT_SEEN = -1e30


def _row_put(ref, g, j, row):
    j8 = pl.multiple_of((j // 8) * 8, 8)
    blk = ref[g, pl.ds(j8, 8), :]
    sub = lax.broadcasted_iota(jnp.int32, blk.shape, 0)
    ref[g, pl.ds(j8, 8), :] = jnp.where(sub == j - j8, row, blk)


def _row_get(ref, g, j):
    j8 = pl.multiple_of((j // 8) * 8, 8)
    blk = ref[g, pl.ds(j8, 8), :]
    sub = lax.broadcasted_iota(jnp.int32, blk.shape, 0)
    return jnp.sum(jnp.where(sub == j - j8, blk, 0.0), axis=0, keepdims=True)


def _att_fwd(k_hm, qt_hm, vt_hm, later, *, name, hg=8, rider=None):
    h, nb, d, t = k_hm.shape
    s = nb * t
    nbp = -(-nb // 8) * 8
    scale = 1.0 / math.sqrt(d)
    assert h % hg == 0

    def body(k_ref, qt_ref, vt_ref, later_ref, o_ref, runs_ref):
        i = pl.program_id(1)
        q_ts = [qt_ref[g] for g in range(hg)]
        runs_ref[...] = jnp.full(runs_ref.shape, NOT_SEEN, F32)

        def cond(carry):
            j, _, _, rmax = carry
            return jnp.logical_and(j >= 0, rmax > EXP_UNDERFLOW)

        def step(carry):
            j, runs, accs, _ = carry
            mask = _att_mask(t, j * t, i * t)
            heads = range(hg)
            for g in heads:
                _row_put(runs_ref, g, j, runs[g])
            _, _, lbs, l1ms = _att_blocks([k_ref[g, j] for g in heads], q_ts, scale, mask)
            afters = _dot_split_each(later_ref[...], l1ms)
            weights = [_mx(jnp.where(mask, jnp.exp(lbs[g] + afters[g] + runs[g]), 0.0)) for g in heads]
            new_accs = [accs[g] + _dot(vt_ref[g, j], weights[g]) for g in heads]
            new_runs = [runs[g] + jnp.sum(l1ms[g], axis=0, keepdims=True) for g in heads]
            return j - 1, tuple(new_runs), tuple(new_accs), _max_over(new_runs)

        init = (i, tuple(jnp.zeros((1, t), F32) for _ in range(hg)), tuple(jnp.zeros((d, t), F32) for _ in range(hg)),
                jnp.float32(0.0))
        _, _, accs, _ = lax.while_loop(cond, step, init)
        for g in range(hg):
            o_ref[g] = _mx(accs[g])

    return _call(
        body, (k_hm, qt_hm, vt_hm, later),
        name=name,
        grid=(h // hg, nb),
        in_specs=[
            pl.BlockSpec((hg, nb, d, t), lambda hh, i: (hh, 0, 0, 0)),
            pl.BlockSpec((hg, d, t), lambda hh, i: (hh, 0, i)),
            pl.BlockSpec((hg, nb, d, t), lambda hh, i: (hh, 0, 0, 0)),
            pl.BlockSpec((t, t), lambda hh, i: (0, 0)),
        ],
        out_specs=[pl.BlockSpec((hg, d, t), lambda hh, i: (hh, 0, i)),
                   pl.BlockSpec((hg, None, nbp, t), lambda hh, i: (hh, i, 0, 0))],
        out_shape=[jax.ShapeDtypeStruct((h, d, s), MXU_DTYPE), jax.ShapeDtypeStruct((h, nb, nbp, t), F32)],
        rider=rider,
    )


def _merge_fwd(ys, p, b_gate, w_bo, w_o, l, x, *, gate_col0, name, tm=512):
    s, d = x.shape
    _, nbr, c, _ = w_bo.shape
    tm = _tile(s, tm)

    def body(ya_ref, yb_ref, yc_ref, g0_ref, g1_ref, g2_ref, bg_ref, wbo_ref, wo_ref, x_ref, x1_ref, m_ref):
        merged = jnp.zeros((tm, d), F32)
        for i, (y_ref, g_ref) in enumerate(((ya_ref, g0_ref), (yb_ref, g1_ref), (yc_ref, g2_ref))):
            gate = _sigmoid(_f32(g_ref) + bg_ref[:, i * d:(i + 1) * d])
            merged = merged + gate * _dot(y_ref[...], wbo_ref[i])
        mb = _mx(merged)
        m_ref[...] = mb
        x1_ref[...] = x_ref[...] + _dot(mb, wo_ref[...])

    yblk = pl.BlockSpec((tm, c), lambda i: (i, 0))
    gblk = lambda k: pl.BlockSpec((tm, d), lambda i: (i, gate_col0 + k))
    xblk = pl.BlockSpec((tm, d), lambda i: (i, 0))
    return pl.pallas_call(
        body,
        name=name,
        grid=(s // tm,),
        in_specs=[
            yblk, yblk, yblk, gblk(0), gblk(1), gblk(2),
            pl.BlockSpec((1, nbr * d), lambda i: (0, 0)),
            _resident((None, nbr, c, d), lambda i: (l, 0, 0, 0)),
            _resident((None, d, d), lambda i: (l, 0, 0)),
            xblk,
        ],
        out_specs=[xblk, xblk],
        out_shape=[jax.ShapeDtypeStruct((s, d), F32), jax.ShapeDtypeStruct((s, d), MXU_DTYPE)],
        compiler_params=_params(),
    )(*ys, p, p, p, b_gate, w_bo, w_o, x)


def _ffn_down(gu, w_down, l, x, *, name, tm=512):
    s, d = x.shape
    f = w_down.shape[1]
    tm = _tile(s, tm)

    def body(g_ref, u_ref, w_ref, x_ref, o_ref):
        g = _f32(g_ref)
        act = _mx(g * _sigmoid(g) * _f32(u_ref))
        o_ref[...] = x_ref[...] + _dot(act, w_ref[...])

    return pl.pallas_call(
        body,
        name=name,
        grid=(s // tm,),
        in_specs=[
            pl.BlockSpec((tm, f), lambda i: (i, 0)),
            pl.BlockSpec((tm, f), lambda i: (i, 1)),
            _resident((None, f, d), lambda i: (l, 0, 0)),
            pl.BlockSpec((tm, d), lambda i: (i, 0)),
        ],
        out_specs=pl.BlockSpec((tm, d), lambda i: (i, 0)),
        out_shape=jax.ShapeDtypeStruct((s, d), F32),
        compiler_params=_params(),
    )(gu, gu, w_down, x)


def _loss_head(y, target, *, name, tm=512):
    s, d = y.shape
    tm = _tile(s, tm)

    def body(y_ref, t_ref, l_ref, dy_ref):
        @pl.when(pl.program_id(0) == 0)
        def _():
            l_ref[...] = jnp.zeros_like(l_ref)

        err = y_ref[...] - t_ref[...]
        dy_ref[...] = err * (1.0 / d)
        sq = _row_fold(err * err)
        part = sq[:, 0:128]
        for k in range(1, d // 128):
            part = part + sq[:, k * 128:(k + 1) * 128]
        l_ref[...] += part * (0.5 / d)

    blk = pl.BlockSpec((tm, d), lambda i: (i, 0))
    return pl.pallas_call(
        body,
        name=name,
        grid=(s // tm,),
        in_specs=[blk, blk],
        out_specs=[pl.BlockSpec((8, 128), lambda i: (0, 0)), blk],
        out_shape=[jax.ShapeDtypeStruct((8, 128), F32), jax.ShapeDtypeStruct((s, d), F32)],
        compiler_params=_params(),
    )(y, target)


def _matmul_tn(a, b, slot, n_slots, into, *, name, t1=1024, tn=None, tm=1024, col0=0, n_total=None):
    m, k1 = a.shape
    n = b.shape[1]
    n_total = n_total or n
    t1 = _tile(k1, t1)
    tn = tn or n
    tm = _tile(m, tm)
    steps = m // tm
    assert n % tn == 0 and col0 % tn == 0
    cb0 = col0 // tn

    def body(a_ref, b_ref, *refs):
        o_ref = refs[-1]

        @pl.when(pl.program_id(2) == 0)
        def _():
            o_ref[...] = jnp.zeros_like(o_ref)

        o_ref[...] += _dot_tn(a_ref[...], b_ref[...])

    return pl.pallas_call(
        body,
        name=name,
        grid=(k1 // t1, n // tn, steps),
        in_specs=[
            pl.BlockSpec((tm, t1), lambda i, j, k: (k, i)),
            pl.BlockSpec((tm, tn), lambda i, j, k: (k, j)),
        ] + ([] if into is None else [ANY]),
        out_specs=pl.BlockSpec((None, t1, tn), lambda i, j, k: (slot, i, cb0 + j)),
        out_shape=jax.ShapeDtypeStruct((n_slots, k1, n_total), F32),
        input_output_aliases={} if into is None else {2: 0},
        compiler_params=_params(),
    )(a, b, *([] if into is None else [into]))


def _matmul_nt_normbwd(pieces, w, l, x, g, dres, *, name, tm=512, tk=1024, rider=None):
    s, d = x.shape
    k = w.shape[2]
    tm = _tile(s, tm)
    n = len(pieces)
    assert sum(pc.shape[1] for pc in pieces) == k

    def body(*refs):
        piece_refs = refs[:n]
        w_ref, x_ref, g_ref, dres_ref, dx_ref, dg_ref = refs[n:]

        @pl.when(pl.program_id(0) == 0)
        def _():
            dg_ref[...] = jnp.zeros_like(dg_ref)

        dh = jnp.zeros((tm, d), F32)
        off = 0
        for pc, pc_ref in zip(pieces, piece_refs):
            width = pc.shape[1]
            step = _tile(width, tk)
            for c0 in range(0, width, step):
                dh = dh + _dot_nt(pc_ref[:, c0:c0 + step], w_ref[:, off + c0:off + c0 + step])
            off += width
        xf = x_ref[...]
        r = lax.rsqrt(jnp.mean(xf * xf, axis=-1, keepdims=True) + EPS)
        y = xf * r
        dy = dh * g_ref[...]
        dx_ref[...] = dres_ref[...] + r * (dy - y * jnp.mean(dy * y, axis=-1, keepdims=True))
        dg_ref[...] += _row_fold(dh * y)

    xblk = pl.BlockSpec((tm, d), lambda i: (i, 0))
    return _call(
        body, (*pieces, w, x, g, dres),
        name=name,
        grid=(s // tm,),
        in_specs=[pl.BlockSpec((tm, pc.shape[1]), lambda i: (i, 0)) for pc in pieces] + [
            _resident((None, d, k), lambda i: (l, 0, 0)),
            xblk,
            pl.BlockSpec((1, d), lambda i: (0, 0)),
            xblk,
        ],
        out_specs=[xblk, pl.BlockSpec((8, d), lambda i: (0, 0))],
        out_shape=[jax.ShapeDtypeStruct((s, d), F32), jax.ShapeDtypeStruct((8, d), F32)],
        rider=rider,
    )


def _ffn_bwd(dx, gu, w_down, l, *, name, tm=512, rider=None):
    s, d = dx.shape
    f = w_down.shape[1]
    tm = _tile(s, tm)

    def body(dx_ref, g_ref, u_ref, w_ref, dgu_ref, act_ref, dxb_ref):
        dxb = _mx(dx_ref[...])
        dxb_ref[...] = dxb
        dact = _dot_nt(dxb, w_ref[...])
        g = _f32(g_ref)
        u = _f32(u_ref)
        sg = _sigmoid(g)
        silu = g * sg
        act_ref[...] = _mx(silu * u)
        dgu_ref[:, 0:f] = _mx(dact * u * (sg * (1.0 + g * (1.0 - sg))))
        dgu_ref[:, f:2 * f] = _mx(dact * silu)

    fblk = lambda col: pl.BlockSpec((tm, f), lambda i: (i, col))
    dblk = pl.BlockSpec((tm, d), lambda i: (i, 0))
    return _call(
        body, (dx, gu, gu, w_down),
        name=name,
        grid=(s // tm,),
        in_specs=[dblk, fblk(0), fblk(1), _resident((None, f, d), lambda i: (l, 0, 0))],
        out_specs=[pl.BlockSpec((tm, 2 * f), lambda i: (i, 0)), fblk(0), dblk],
        out_shape=[
            jax.ShapeDtypeStruct((s, 2 * f), MXU_DTYPE),
            jax.ShapeDtypeStruct((s, f), MXU_DTYPE),
            jax.ShapeDtypeStruct((s, d), MXU_DTYPE),
        ],
        rider=rider,
    )


def _merge_bwd(dx, ys, p, b_gate, w_bo, w_o, l, *, gate_col0, name, tm=512):
    s, d = dx.shape
    _, nbr, c, _ = w_bo.shape
    tm = _tile(s, tm)

    def body(dx_ref, ya_ref, yb_ref, yc_ref, g0_ref, g1_ref, g2_ref, bg_ref, wbo_ref, wo_ref,
             dgate_ref, dya_ref, dyb_ref, dyc_ref, dd0_ref, dd1_ref, dd2_ref, dxb_ref, dbg_ref):
        @pl.when(pl.program_id(0) == 0)
        def _():
            dbg_ref[...] = jnp.zeros_like(dbg_ref)

        dxb = _mx(dx_ref[...])
        dxb_ref[...] = dxb
        dmerged = _dot_nt(dxb, wo_ref[...])
        branches = ((ya_ref, g0_ref, dya_ref, dd0_ref), (yb_ref, g1_ref, dyb_ref, dd1_ref), (yc_ref, g2_ref, dyc_ref, dd2_ref))
        for i, (y_ref, g_ref, dy_ref, dd_ref) in enumerate(branches):
            cols = slice(i * d, (i + 1) * d)
            gate = _sigmoid(_f32(g_ref) + bg_ref[:, cols])
            yd = _dot(y_ref[...], wbo_ref[i])
            dyd = _mx(dmerged * gate)
            dd_ref[...] = dyd
            dy_ref[...] = _dot_nt(dyd, wbo_ref[i])
            dpre = dmerged * yd * gate * (1.0 - gate)
            dgate_ref[:, cols] = _mx(dpre)
            dbg_ref[:, cols] += _row_fold(dpre)

    yblk = pl.BlockSpec((tm, c), lambda i: (i, 0))
    gblk = lambda k: pl.BlockSpec((tm, d), lambda i: (i, gate_col0 + k))
    dblk = pl.BlockSpec((tm, d), lambda i: (i, 0))
    return pl.pallas_call(
        body,
        name=name,
        grid=(s // tm,),
        in_specs=[
            dblk, yblk, yblk, yblk, gblk(0), gblk(1), gblk(2),
            pl.BlockSpec((1, nbr * d), lambda i: (0, 0)),
            _resident((None, nbr, c, d), lambda i: (l, 0, 0, 0)),
            _resident((None, d, d), lambda i: (l, 0, 0)),
        ],
        out_specs=[pl.BlockSpec((tm, nbr * d), lambda i: (i, 0)), yblk, yblk, yblk, dblk, dblk, dblk, dblk,
                   pl.BlockSpec((8, nbr * d), lambda i: (0, 0))],
        out_shape=[jax.ShapeDtypeStruct((s, nbr * d), MXU_DTYPE)] + [jax.ShapeDtypeStruct((s, c), F32)] * 3
        + [jax.ShapeDtypeStruct((s, d), MXU_DTYPE)] * 4 + [jax.ShapeDtypeStruct((8, nbr * d), F32)],
        compiler_params=_params(),
    )(dx, *ys, p, p, p, b_gate, w_bo, w_o)


def _conv_bwd(p, dya, conv_w, conv_b, *, name, tm=512):
    s = p.shape[0]
    c = conv_w.shape[1]
    tm = _tile(s, tm)
    hb = tm // HALO
    last = s // tm - 1

    def body(bg_ref, cg_ref, xa_ref, cgp_ref, xap_ref, dy_ref, dyn_ref, bgn_ref, w_ref, b_ref,
             dp_ref, dw_ref):
        i = pl.program_id(0)

        @pl.when(i == 0)
        def _():
            dw_ref[...] = jnp.zeros_like(dw_ref)

        cg = _f32(cg_ref)
        xa = _f32(xa_ref)
        u = cg * xa
        prev = jnp.where(i == 0, 0.0, _f32(cgp_ref) * _f32(xap_ref))
        um1, um2 = _conv_taps(u, prev)
        w = w_ref[...]
        y = b_ref[...] + w[0:1, :] * um2 + w[1:2, :] * um1 + w[2:3, :] * u
        dya = dy_ref[...]
        dp_ref[:, 0:c] = _mx(dya * y)
        dyv = dya * _f32(bg_ref)
        nxt = jnp.where(i == last, 0.0, dyn_ref[...] * _f32(bgn_ref))
        row = lax.broadcasted_iota(jnp.int32, dyv.shape, 0)
        dp1 = jnp.where(row == tm - 1, nxt[0:1, :], pltpu.roll(dyv, tm - 1, 0))
        dp2 = pltpu.roll(dyv, tm - 2, 0)
        dp2 = jnp.where(row == tm - 2, nxt[0:1, :], jnp.where(row == tm - 1, nxt[1:2, :], dp2))
        du = w[2:3, :] * dyv + w[1:2, :] * dp1 + w[0:1, :] * dp2
        dp_ref[:, c:2 * c] = _mx(du * xa)
        dp_ref[:, 2 * c:3 * c] = _mx(du * cg)
        dw_ref[0] += _row_fold(dyv * um2)
        dw_ref[1] += _row_fold(dyv * um1)
        dw_ref[2] += _row_fold(dyv * u)
        dw_ref[3] += _row_fold(dyv)

    blk = lambda col: pl.BlockSpec((tm, c), lambda i: (i, col))
    halo = lambda col: pl.BlockSpec((HALO, c), lambda i: (jnp.maximum(i * hb - 1, 0), col))
    nhalo = lambda col: pl.BlockSpec((HALO, c), lambda i: (jnp.minimum((i + 1) * hb, s // HALO - 1), col))
    return pl.pallas_call(
        body,
        name=name,
        grid=(s // tm,),
        in_specs=[blk(0), blk(1), blk(2), halo(1), halo(2), blk(0), nhalo(0), nhalo(0),
                  pl.BlockSpec((CONV_K, c), lambda i: (0, 0)), pl.BlockSpec((1, c), lambda i: (0, 0))],
        out_specs=[pl.BlockSpec((tm, 3 * c), lambda i: (i, 0)), pl.BlockSpec((4, 8, c), lambda i: (0, 0, 0))],
        out_shape=[jax.ShapeDtypeStruct((s, 3 * c), MXU_DTYPE), jax.ShapeDtypeStruct((4, 8, c), F32)],
        compiler_params=_params(),
    )(p, p, p, p, p, dya, dya, p, conv_w, conv_b)


def _sgu_bwd(p, dyb, ln_g, ln_b, wtril, wtril_t, bias_full, *, col0, name, tm=512):
    s = p.shape[0]
    c = ln_g.shape[1]
    groups = c // CHUNK
    tm = _tile(s, tm)

    def body(u_ref, v_ref, dy_ref, g_ref, b_ref, w_ref, wt_ref, bias_ref, duv_ref, dln_ref, dw_ref, dbias_ref, dvn_ref):
        @pl.when(pl.program_id(0) == 0)
        def _():
            dln_ref[...] = jnp.zeros_like(dln_ref)
            dw_ref[...] = jnp.zeros_like(dw_ref)
            dbias_ref[...] = jnp.zeros_like(dbias_ref)

        sv = _f32(v_ref)
        xhat, r = _layernorm_stats(_gelu(sv))
        vb = _mx(xhat * g_ref[...] + b_ref[...])
        for n in range(tm // CHUNK):
            rows = slice(n * CHUNK, (n + 1) * CHUNK)
            for gi in range(groups):
                cols = slice(gi * CHUNK, (gi + 1) * CHUNK)
                su = u_ref[rows, cols].astype(F32)
                dy = dy_ref[rows, cols]
                vblk = vb[rows, cols]
                mixed = _dot(w_ref[gi], vblk) + bias_ref[:, cols]
                duv_ref[rows, cols] = _mx(dy * mixed * _gelu_grad(su))
                dmixed = dy * _gelu(su)
                dmb = _mx(dmixed)
                dvn_ref[rows, cols] = _dot(wt_ref[gi], dmb)
                dw_ref[gi] += _dot_nt(dmb, vblk)
                dbias_ref[:, cols] += dmixed
        dvn = dvn_ref[...]
        dln_ref[0] += _row_fold(dvn * xhat)
        dln_ref[1] += _row_fold(dvn)
        dxh = dvn * g_ref[...]
        dgv = r * (dxh - jnp.mean(dxh, axis=-1, keepdims=True) - xhat * jnp.mean(dxh * xhat, axis=-1, keepdims=True))
        duv_ref[:, c:2 * c] = _mx(dgv * _gelu_grad(sv))

    blk = lambda col: pl.BlockSpec((tm, c), lambda i: (i, col))
    vec = pl.BlockSpec((1, c), lambda i: (0, 0))
    wspec = pl.BlockSpec((groups, CHUNK, CHUNK), lambda i: (0, 0, 0))
    return pl.pallas_call(
        body,
        name=name,
        grid=(s // tm,),
        in_specs=[blk(col0), blk(col0 + 1), blk(0), vec, vec, wspec, wspec, pl.BlockSpec((CHUNK, c), lambda i: (0, 0))],
        out_specs=[pl.BlockSpec((tm, 2 * c), lambda i: (i, 0)), pl.BlockSpec((2, 8, c), lambda i: (0, 0, 0)), wspec,
                   pl.BlockSpec((CHUNK, c), lambda i: (0, 0))],
        out_shape=[jax.ShapeDtypeStruct((s, 2 * c), MXU_DTYPE),
                   jax.ShapeDtypeStruct((2, 8, c), F32), jax.ShapeDtypeStruct((groups, CHUNK, CHUNK), F32),
                   jax.ShapeDtypeStruct((CHUNK, c), F32)],
        scratch_shapes=[pltpu.VMEM((tm, c), F32)],
        compiler_params=_params(),
    )(p, p, dyb, ln_g, ln_b, wtril, wtril_t, bias_full)


def _att_bwd(kt_hm, vt_hm, qt_hm, dot_hm, runs, later, earlier, *, name, hg=4, rider=None):
    h, nb, d, t = kt_hm.shape
    s = nb * t
    nbp = runs.shape[2]
    scale = 1.0 / math.sqrt(d)
    assert h % hg == 0

    def body(kt_ref, vt_ref, qt_ref, dot_ref, runs_ref, later_ref, earlier_ref, dqt_ref, dkt_ref, dvt_ref):
        i = pl.program_id(1)

        @pl.when(i == 0)
        def _():
            dkt_ref[...] = jnp.zeros_like(dkt_ref)
            dvt_ref[...] = jnp.zeros_like(dvt_ref)

        q_ts = [qt_ref[g] for g in range(hg)]
        do_ts = [dot_ref[g] for g in range(hg)]

        best = runs_ref[0]
        for g in range(1, hg):
            best = jnp.maximum(best, runs_ref[g])
        row = lax.broadcasted_iota(jnp.int32, (nbp, 1), 0)
        counts = jnp.logical_and(jnp.max(best, axis=1, keepdims=True) > EXP_UNDERFLOW, row < i)
        seen = jnp.sum(counts.astype(jnp.int32))
        zeros_row = tuple(jnp.zeros((1, t), F32) for _ in range(hg))

        def up(j, carry):
            gsums, dqts = carry
            mask = _att_mask(t, j * t, i * t)
            heads = range(hg)
            zs, es, lbs, l1ms = _att_blocks([kt_ref[g, j] for g in heads], q_ts, scale, mask)
            afters = _dot_split_each(later_ref[...], l1ms)
            das = [_dot_tn(vt_ref[g, j], do_ts[g]) for g in heads]
            weights = [jnp.where(mask, jnp.exp(lbs[g] + afters[g] + _row_get(runs_ref, g, j)), 0.0) for g in heads]
            grs = [das[g] * weights[g] for g in heads]
            gbefores = _dot_split_each(earlier_ref[...], grs)
            dzs = []
            for g in heads:
                inv = 1.0 / (1.0 + es[g])
                pos = zs[g] >= 0.0
                beta = jnp.where(pos, inv, es[g] * inv)
                omb = jnp.where(pos, es[g] * inv, inv)
                dzs.append(_mx(jnp.where(mask, grs[g] * omb - (gbefores[g] + gsums[g]) * beta, 0.0) * scale))
            new_dqts = [dqts[g] + _dot(kt_ref[g, j], dzs[g]) for g in heads]
            for g in heads:
                dkt_ref[g, j] += _dot_nt(q_ts[g], dzs[g])
            for g in heads:
                dvt_ref[g, j] += _dot_nt(do_ts[g], _mx(weights[g]))
            new_gsums = [gsums[g] + jnp.sum(grs[g], axis=0, keepdims=True) for g in heads]
            return tuple(new_gsums), tuple(new_dqts)

        _, dqts = lax.fori_loop(i - seen, i + 1, up, (zeros_row, tuple(jnp.zeros((d, t), F32) for _ in range(hg))))
        for g in range(hg):
            dqt_ref[g] = dqts[g]

    whole = pl.BlockSpec((hg, nb, d, t), lambda hh, i: (hh, 0, 0, 0))
    cols = pl.BlockSpec((hg, d, t), lambda hh, i: (hh, 0, i))
    tri = pl.BlockSpec((t, t), lambda hh, i: (0, 0))
    return _call(
        body, (kt_hm, vt_hm, qt_hm, dot_hm, runs, later, earlier),
        name=name,
        grid=(h // hg, nb),
        in_specs=[whole, whole, cols, cols, pl.BlockSpec((hg, None, nbp, t), lambda hh, i: (hh, i, 0, 0)), tri, tri],
        out_specs=[cols, whole, whole],
        out_shape=[jax.ShapeDtypeStruct((h, d, s), F32), jax.ShapeDtypeStruct((h, nb, d, t), F32),
                   jax.ShapeDtypeStruct((h, nb, d, t), F32)],
        rider=rider,
    )


def _qkv_bwd(p, dqt, dkt, dvt, qg, kg, headavg, *, col0, name, tm=512):
    s = p.shape[0]
    c = qg.shape[1]
    tm = _tile(s, tm)

    t = ATT_BLOCK
    assert tm % t == 0

    def rows_of(blocked_ref):
        parts = []
        for b in range(tm // t):
            cols = jnp.concatenate([blocked_ref[hh, b] for hh in range(HEADS)], axis=0)
            parts.append(cols.T)
        return jnp.concatenate(parts, axis=0)

    def body(q_ref, k_ref, dqt_ref, dkt_ref, dvt_ref, qg_ref, kg_ref, avg_ref, dqkv_ref, dg_ref):
        @pl.when(pl.program_id(0) == 0)
        def _():
            dg_ref[...] = jnp.zeros_like(dg_ref)

        dqn = jnp.concatenate([dqt_ref[hh] for hh in range(HEADS)], axis=0).T
        for n, (src, dn, g_ref) in enumerate(((q_ref, dqn, qg_ref), (k_ref, rows_of(dkt_ref), kg_ref))):
            xf = _f32(src)
            r = lax.rsqrt(_head_mean(xf * xf, avg_ref[...]) + EPS)
            y = xf * r
            dy = dn * g_ref[...]
            dqkv_ref[:, n * c:(n + 1) * c] = _mx(r * (dy - y * _head_mean(dy * y, avg_ref[...])))
            dg_ref[n] += _row_fold(dn * y)
        dqkv_ref[:, 2 * c:3 * c] = _mx(rows_of(dvt_ref))

    blk = lambda col: pl.BlockSpec((tm, c), lambda i: (i, col))
    vec = pl.BlockSpec((1, c), lambda i: (0, 0))
    blocked = pl.BlockSpec((HEADS, tm // t, HEAD_DIM, t), lambda i: (0, i, 0, 0))
    return pl.pallas_call(
        body,
        name=name,
        grid=(s // tm,),
        in_specs=[blk(col0), blk(col0 + 1), pl.BlockSpec((HEADS, HEAD_DIM, tm), lambda i: (0, 0, i)), blocked, blocked,
                  vec, vec, pl.BlockSpec((c, c), lambda i: (0, 0))],
        out_specs=[pl.BlockSpec((tm, 3 * c), lambda i: (i, 0)), pl.BlockSpec((2, 8, c), lambda i: (0, 0, 0))],
        out_shape=[jax.ShapeDtypeStruct((s, 3 * c), MXU_DTYPE), jax.ShapeDtypeStruct((2, 8, c), F32)],
        compiler_params=_params(),
    )(p, p, dqt, dkt, dvt, qg, kg, headavg)


def _place():
    x, y, c = lax.axis_index("x"), lax.axis_index("y"), lax.axis_index("c")
    chips = [(1 - x, y), (x, 1 - y), (1 - x, 1 - y)]
    return x, y, c, chips


def _shard_of(ref, axis, chip, width):
    idx = [slice(None)] * len(ref.shape)
    idx[axis] = pl.ds(chip * width, width)
    return ref.at[tuple(idx)]


def _place_shard(w, layer, axis, chip_idx, dtype, *, name):
    rows, cols = _as_rows(w.shape[1:])
    tr = _row_tile(rows, cols, 2)
    if axis == len(w.shape) - 2:
        out_shape = (1, rows, cols * N_CHIPS)
        out_spec = pl.BlockSpec((None, tr, cols), lambda i, j_ref: (0, i, j_ref[0]))
    else:
        assert axis == 0
        per = rows // tr
        out_shape = (1, rows * N_CHIPS, cols)
        out_spec = pl.BlockSpec((None, tr, cols), lambda i, j_ref: (0, j_ref[0] * per + i, 0))
    full = [1, *w.shape[1:]]
    full[1 + axis] *= N_CHIPS

    def body(j_ref, w_ref, o_ref):
        o_ref[...] = w_ref[...].astype(dtype)

    out = pl.pallas_call(
        body,
        name=name,
        grid_spec=pltpu.PrefetchScalarGridSpec(
            num_scalar_prefetch=1,
            grid=(rows // tr,),
            in_specs=[pl.BlockSpec((None, tr, cols), lambda i, j_ref: (layer, i, 0))],
            out_specs=out_spec,
        ),
        out_shape=jax.ShapeDtypeStruct(out_shape, dtype),
        compiler_params=_params(),
    )(chip_idx, w.reshape(w.shape[0], rows, cols))
    return out.reshape(full)


def _gather_riders(placed, axes, layer):
    n = len(placed)
    widths = [pa.shape[1 + ax] // N_CHIPS for pa, ax in zip(placed, axes)]

    def copies(refs, sems):
        x, y, c, chips = _place()

        def block(a, chip):
            return _shard_of(refs[a].at[0], axes[a], chip, widths[a])

        def over_ici(a, k, chip):
            return pltpu.make_async_remote_copy(
                src_ref=block(a, chip), dst_ref=block(a, chip), send_sem=sems[0].at[a, k], recv_sem=sems[1].at[a, k],
                device_id=(*chips[k], c), device_id_type=MESH)

        def to_sibling(a, k):
            cx, cy = chips[k]
            return pltpu.make_async_remote_copy(
                src_ref=block(a, 2 * cx + cy), dst_ref=block(a, 2 * cx + cy), send_sem=sems[0].at[a, k],
                recv_sem=sems[1].at[a, k], device_id=(x, y, 1 - c), device_id_type=MESH)

        return 2 * x + y, c, chips, over_ici, to_sibling

    def ici_start(refs, sems):
        me, c, _, over_ici, _ = copies(refs, sems)

        @pl.when(c == layer)
        def _():
            for a in range(n):
                for k in range(3):
                    over_ici(a, k, me).start()

    def ici_finish(refs, sems):
        me, c, chips, over_ici, _ = copies(refs, sems)

        @pl.when(c == layer)
        def _():
            for a in range(n):
                for k in range(3):
                    cx, cy = chips[k]
                    over_ici(a, k, 2 * cx + cy).wait_recv()
            for a in range(n):
                for k in range(3):
                    over_ici(a, k, me).wait_send()

    def d2d_start(refs, sems):
        _, c, _, _, to_sibling = copies(refs, sems)

        @pl.when(c == layer)
        def _():
            for a in range(n):
                for k in range(3):
                    to_sibling(a, k).start()

    def d2d_finish(refs, sems):
        _, c, _, _, to_sibling = copies(refs, sems)

        @pl.when(c == layer)
        def _():
            for a in range(n):
                for k in range(3):
                    to_sibling(a, k).wait_send()

        @pl.when(c != layer)
        def _():
            for a in range(n):
                for k in range(3):
                    to_sibling(a, k).wait_recv()

    def both_start(refs, sems):
        ici_start(refs, sems[:2])

    def both_finish(refs, sems):
        me, c, chips, over_ici, _ = copies(refs, sems[:2])
        to_sibling = copies(refs, sems[2:])[4]

        @pl.when(c == layer)
        def _():
            for a in range(n):
                for k in range(3):
                    cx, cy = chips[k]
                    over_ici(a, k, 2 * cx + cy).wait_recv()
                    to_sibling(a, k).start()
            for a in range(n):
                for k in range(3):
                    over_ici(a, k, me).wait_send()
                    to_sibling(a, k).wait_send()

        @pl.when(c != layer)
        def _():
            for a in range(n):
                for k in range(3):
                    to_sibling(a, k).wait_recv()

    arrays, sems = tuple(placed), ((n, 3), (n, 3))
    return (_Rider(arrays, sems, ici_start, ici_finish), _Rider(arrays, sems, d2d_start, d2d_finish),
            _Rider(arrays, sems * 2, both_start, both_finish))


def _join_riders(*riders):
    def each(which):
        def run(refs, sems):
            for r in riders:
                getattr(r, which)(refs[:len(r.arrays)], sems[:len(r.sems)])
                refs, sems = refs[len(r.arrays):], sems[len(r.sems):]
        return run

    return _Rider(sum((r.arrays for r in riders), ()), sum((r.sems for r in riders), ()), each("start"), each("finish"))


def _exchange_rider(grads, others, layer):
    n = len(grads)

    def copy(refs, sems, a):
        x, y, c, _ = _place()
        return pltpu.make_async_remote_copy(
            src_ref=refs[a], dst_ref=refs[n + a], send_sem=sems[0].at[a], recv_sem=sems[1].at[a],
            device_id=(x, y, 1 - c), device_id_type=MESH)

    def start(refs, sems):
        @pl.when(lax.axis_index("c") != layer)
        def _():
            for a in range(n):
                copy(refs, sems, a).start()

    def finish(refs, sems):
        @pl.when(lax.axis_index("c") != layer)
        def _():
            for a in range(n):
                copy(refs, sems, a).wait_send()

        @pl.when(lax.axis_index("c") == layer)
        def _():
            for a in range(n):
                copy(refs, sems, a).wait_recv()

    return _Rider((*grads, *others), ((n,), (n,)), start, finish)


def _scatter_rider(wires, landeds, axes, layer):
    n = len(wires)
    widths = [wr.shape[ax] // N_CHIPS for wr, ax in zip(wires, axes)]

    def copies(refs, sems):
        _, _, c, chips = _place()
        return [pltpu.make_async_remote_copy(
            src_ref=_shard_of(refs[a], axes[a], 2 * cx + cy, widths[a]), dst_ref=refs[n + a].at[k],
            send_sem=sems[0].at[a, k], recv_sem=sems[1].at[a, k], device_id=(cx, cy, c), device_id_type=MESH)
            for a in range(n) for k, (cx, cy) in enumerate(chips)]

    def start(refs, sems):
        @pl.when(lax.axis_index("c") == layer)
        def _():
            for cp in copies(refs, sems):
                cp.start()

    def finish(refs, sems):
        @pl.when(lax.axis_index("c") == layer)
        def _():
            for cp in copies(refs, sems):
                cp.wait()

    return _Rider((*wires, *landeds), ((n, 3), (n, 3)), start, finish)


def _share_rider(finals, layer):
    n = len(finals)

    def copy(refs, sems, a):
        x, y, c, _ = _place()
        return pltpu.make_async_remote_copy(
            src_ref=refs[a].at[layer], dst_ref=refs[a].at[layer], send_sem=sems[0].at[a], recv_sem=sems[1].at[a],
            device_id=(x, y, 1 - c), device_id_type=MESH)

    def start(refs, sems):
        @pl.when(lax.axis_index("c") == layer)
        def _():
            for a in range(n):
                copy(refs, sems, a).start()

    def finish(refs, sems):
        @pl.when(lax.axis_index("c") == layer)
        def _():
            for a in range(n):
                copy(refs, sems, a).wait_send()

        @pl.when(lax.axis_index("c") != layer)
        def _():
            for a in range(n):
                copy(refs, sems, a).wait_recv()

    return _Rider(tuple(finals), ((n,), (n,)), start, finish)


def _small_view(shape):
    size = math.prod(shape)
    return (size // 128, 128) if size % 1024 == 0 else (shape[0], size // shape[0])


def _all_reduce_small(parts, *, name):
    n = len(parts)
    views = [_small_view(a.shape) for a in parts]

    def body(*refs):
        ins, outs, slots = refs[:n], refs[n:2 * n], refs[2 * n:3 * n]
        send_sem, recv_sem = refs[3 * n:]
        x, y, c, _ = _place()
        me = 4 * x + 2 * y + c
        copies = []
        for a in range(n):
            slots[a][me] = ins[a][...]
            for k in range(1, N_DEV):
                peer = (x ^ (k >> 2), y ^ ((k >> 1) & 1), c ^ (k & 1))
                copies.append(pltpu.make_async_remote_copy(
                    src_ref=ins[a], dst_ref=slots[a].at[me], send_sem=send_sem.at[a, k - 1],
                    recv_sem=recv_sem.at[a, k - 1], device_id=peer, device_id_type=MESH))
        for cp in copies:
            cp.start()
        for a in range(n):
            for k in range(1, N_DEV):
                pltpu.make_async_remote_copy(
                    src_ref=ins[a], dst_ref=slots[a].at[me ^ k], send_sem=send_sem.at[a, k - 1],
                    recv_sem=recv_sem.at[a, k - 1], device_id=(x, y, c), device_id_type=MESH).wait()
        for a in range(n):
            total = slots[a][0]
            for dev in range(1, N_DEV):
                total = total + slots[a][dev]
            outs[a][...] = total

    vmem = pl.BlockSpec(memory_space=pltpu.VMEM)
    outs = pl.pallas_call(
        body,
        name=name,
        in_specs=[vmem] * n,
        out_specs=[vmem] * n,
        out_shape=[jax.ShapeDtypeStruct(view, F32) for view in views],
        scratch_shapes=[pltpu.VMEM((N_DEV, *view), F32) for view in views]
        + [pltpu.SemaphoreType.DMA((n, N_DEV - 1)), pltpu.SemaphoreType.DMA((n, N_DEV - 1))],
        compiler_params=_params(),
    )(*[a.reshape(view) for a, view in zip(parts, views)])
    return [o.reshape(a.shape) for o, a in zip(outs, parts)]


def _as_rows(shape):
    cols = shape[-1]
    return math.prod(shape[:-1]), cols


ELEMENTWISE_VMEM = 24 * 1024 * 1024


def _row_tile(rows, cols, n_arrays, sublanes=8):
    cap = ELEMENTWISE_VMEM // (n_arrays * 2 * 4 * cols)
    best = None
    for t in range(sublanes, min(rows, cap) + 1, sublanes):
        if rows % t == 0:
            best = t
    assert best is not None, (rows, cols)
    return best


def _pair_sum(g, other, active, *, name):
    rows, cols = _as_rows(g.shape)
    tr = _row_tile(rows, cols, 4, sublanes=16)

    def body(a_ref, g_ref, o_ref, out_ref, wire_ref):
        @pl.when(a_ref[0] == 1)
        def _():
            total = g_ref[...] + o_ref[...]
            out_ref[...] = total
            wire_ref[...] = total.astype(WIRE_DTYPE)

    blk = pl.BlockSpec((tr, cols), lambda i, a_ref: (i * a_ref[0], 0))
    out, wire = pl.pallas_call(
        body,
        name=name,
        grid_spec=pltpu.PrefetchScalarGridSpec(
            num_scalar_prefetch=1, grid=(rows // tr,), in_specs=[blk, blk], out_specs=[blk, blk]),
        out_shape=[jax.ShapeDtypeStruct((rows, cols), F32), jax.ShapeDtypeStruct((rows, cols), WIRE_DTYPE)],
        compiler_params=_params(),
    )(active, g.reshape(rows, cols), other.reshape(rows, cols))
    return out.reshape(g.shape), wire.reshape(g.shape)


def _chip_sum(mine, got, axis, chip_idx, active, layer, layers, into, *, name):
    shard_shape = got.shape[1:]
    rows, cols = _as_rows(shard_shape)
    tr = _row_tile(rows, cols, 5, sublanes=16)
    if axis == len(mine.shape) - 1:
        m2 = mine.reshape(rows, cols * N_CHIPS)
        mine_spec = pl.BlockSpec((tr, cols), lambda i, j_ref, a_ref: (i * a_ref[0], j_ref[0]))
    else:
        assert axis == 0
        m2 = mine.reshape(N_CHIPS, rows, cols)
        mine_spec = pl.BlockSpec((None, tr, cols), lambda i, j_ref, a_ref: (j_ref[0], i * a_ref[0], 0))

    def body(j_ref, a_ref, m_ref, got_ref, *refs):
        @pl.when(a_ref[0] == 1)
        def _():
            refs[-1][...] = ((m_ref[...] + got_ref[0].astype(F32)) + got_ref[1].astype(F32)) + got_ref[2].astype(F32)

    out = pl.pallas_call(
        body,
        name=name,
        grid_spec=pltpu.PrefetchScalarGridSpec(
            num_scalar_prefetch=2,
            grid=(rows // tr,),
            in_specs=[mine_spec, pl.BlockSpec((3, tr, cols), lambda i, j_ref, a_ref: (0, i * a_ref[0], 0))]
            + ([] if into is None else [ANY]),
            out_specs=pl.BlockSpec((None, tr, cols), lambda i, j_ref, a_ref: (layer, i * a_ref[0], 0)),
        ),
        out_shape=jax.ShapeDtypeStruct((layers, rows, cols), F32),
        input_output_aliases={} if into is None else {4: 0},
        compiler_params=_params(),
    )(chip_idx, active, m2, got.reshape(3, rows, cols), *([] if into is None else [into.reshape(layers, rows, cols)]))
    return out.reshape((layers, *shard_shape))


class _ReduceChain:
    def __init__(self, grads, layer, layers, chip_idx, core_idx, finals):
        self.grads, self.layer, self.layers, self.chip_idx, self.finals = list(grads), layer, layers, chip_idx, finals
        self.active = (core_idx == layer).astype(jnp.int32)
        self.axes = [BIG_AXIS[n] for n in BIG]
        self.pairs = None

    def exchange(self):
        return _exchange_rider(self.grads, [lax.empty(g.shape, F32) for g in self.grads], self.layer)

    def exchanged(self, arrived):
        n = len(self.grads)
        self.pairs = [_pair_sum(g, o, self.active, name=f"pair_sum_{name}_l{self.layer}")
                      for name, g, o in zip(BIG, arrived[:n], arrived[n:])]

    def scatter(self):
        landeds = []
        for (_, wire), ax in zip(self.pairs, self.axes):
            shard = list(wire.shape)
            shard[ax] //= N_CHIPS
            landeds.append(lax.empty((3, *shard), WIRE_DTYPE))
        return _scatter_rider([wire for _, wire in self.pairs], landeds, self.axes, self.layer)

    def scattered(self, arrived):
        n = len(self.grads)
        into = self.finals or [None] * n
        self.finals = [_chip_sum(pair, landed, ax, self.chip_idx, self.active, self.layer, self.layers, old,
                                 name=f"chip_sum_{name}_l{self.layer}")
                       for name, (pair, _), landed, ax, old in zip(BIG, self.pairs, arrived[n:], self.axes, into)]

    def share(self):
        return _share_rider(self.finals, self.layer)

    def shared(self, arrived):
        self.finals = list(arrived)

    def run_alone(self):
        tag = f"l{self.layer}"
        self.exchanged(_ride_alone([self.exchange()], name=f"grads_pair_exchange_{tag}"))
        self.scattered(_ride_alone([self.scatter()], name=f"grads_chip_scatter_{tag}"))
        self.shared(_ride_alone([self.share()], name=f"grads_pair_share_{tag}"))


def _adamw_update(w_ref, g_ref, m_ref, v_ref, d_ref, nm_ref, nv_ref):
    c1 = 1.0 / (1.0 - ADAM_B1 ** ADAM_STEP)
    c2 = 1.0 / (1.0 - ADAM_B2 ** ADAM_STEP)
    gg = g_ref[...]
    nm = ADAM_B1 * m_ref[...] + (1.0 - ADAM_B1) * gg
    nv = ADAM_B2 * v_ref[...] + (1.0 - ADAM_B2) * (gg * gg)
    nm_ref[...] = nm
    nv_ref[...] = nv
    d_ref[...] = -ADAM_LR * ((nm * c1) / (jnp.sqrt(nv * c2) + ADAM_EPS) + ADAM_WD * w_ref[...])


def _adamw_small(ws, gs, ms, vs, *, name):
    n = len(ws)
    views = [_small_view(a.shape) for a in ws]

    def body(*refs):
        w_r, g_r, m_r, v_r = (refs[k * n:(k + 1) * n] for k in range(4))
        d_r, nm_r, nv_r = (refs[(4 + k) * n:(5 + k) * n] for k in range(3))
        for a in range(n):
            _adamw_update(w_r[a], g_r[a], m_r[a], v_r[a], d_r[a], nm_r[a], nv_r[a])

    vmem = pl.BlockSpec(memory_space=pltpu.VMEM)
    flat = lambda arrs: [a.reshape(view) for a, view in zip(arrs, views)]
    outs = pl.pallas_call(
        body,
        name=name,
        in_specs=[vmem] * (4 * n),
        out_specs=[vmem] * (3 * n),
        out_shape=[jax.ShapeDtypeStruct(view, F32) for view in views] * 3,
        compiler_params=_params(),
    )(*flat(ws), *flat(gs), *flat(ms), *flat(vs))
    return [tuple(outs[k * n + a].reshape(ws[a].shape) for k in range(3)) for a in range(n)]


def _adamw(w, g, m, v, *, name):
    shape = w.shape
    rows, cols = _as_rows(shape)
    tr = _row_tile(rows, cols, 7)
    body = functools.partial(_adamw_update)

    blk = pl.BlockSpec((tr, cols), lambda i: (i, 0))
    flat = lambda a: a.reshape(rows, cols)
    outs = pl.pallas_call(
        body,
        name=name,
        grid=(rows // tr,),
        in_specs=[blk] * 4,
        out_specs=[blk] * 3,
        out_shape=[jax.ShapeDtypeStruct((rows, cols), F32)] * 3,
        compiler_params=_params(),
    )(flat(w), flat(g), flat(m), flat(v))
    return tuple(o.reshape(shape) for o in outs)


def _cols(a):
    return a.reshape(a.shape[0], HEADS, HEAD_DIM).transpose(1, 2, 0)


def _from_cols(a):
    h, d, s = a.shape
    return a.transpose(2, 0, 1).reshape(s, h * d)


BIG = ("w_in", "w_branch_out", "w_o", "w_gate_up", "w_down")
BIG_AXIS = {"w_in": 1, "w_branch_out": 2, "w_o": 0, "w_gate_up": 1, "w_down": 0}
SMALL = ("mix_norm_g", "b_gate", "conv_w", "conv_b", "sgu_ln_g", "sgu_ln_b", "sgu_w", "sgu_b", "q_norm_g", "k_norm_g",
         "ffn_norm_g")
ORDER = ("mix_norm_g", "w_in", "b_gate", "conv_w", "conv_b", "sgu_ln_g", "sgu_ln_b", "sgu_w", "sgu_b", "q_norm_g",
         "k_norm_g", "w_branch_out", "w_o", "ffn_norm_g", "w_gate_up", "w_down")


def _layer_forward(x, w, wl, l, coming):
    n_in = wl["w_in"].shape[2]
    gate_col0 = (n_in - 3 * x.shape[1]) // x.shape[1]
    tag = f"l{l}"
    rider = None
    if coming:
        placed, axes, nxt = coming
        half = len(placed) // 2
        first, second = (placed[:half], axes[:half], nxt), (placed[half:], axes[half:], nxt)
        rider = _gather_riders(*first)[0]
    (p, h), arrived = _norm_matmul(x, w["mix_norm_g"][l][None], wl["w_in"], 0, name=f"in_proj_{tag}", tn=n_in // 7,
                                   rider=rider)
    ya = _conv_fwd(p, wl["conv_w"][0], w["conv_b"][l][None], name=f"conv_{tag}")
    yb = _sgu_fwd(p, w["sgu_ln_g"][l][None], w["sgu_ln_b"][l][None], w["wtril"][l], w["bias_full"][l], col0=3,
                  name=f"sgu_{tag}")
    q_c, k_cb, v_cb = _qkv_prep(p, w["qg"][l], w["kg"][l], w["headavg"], col0=5, name=f"qkv_{tag}")
    if coming:
        rider = _join_riders(_gather_riders(list(arrived), *first[1:])[1], _gather_riders(*second)[0])
    (out_t, runs), arrived = _att_fwd(k_cb, q_c, v_cb, w["later"], name=f"att_{tag}", rider=rider)
    yc = _from_cols(out_t)
    x1, merged = _merge_fwd((ya, yb, yc), p, w["b_gate"][l][None], wl["w_branch_out"], wl["w_o"], 0, x,
                            gate_col0=gate_col0, name=f"merge_{tag}")
    if coming:
        here = arrived[:half]
        rider = _gather_riders(list(arrived[half:]), *second[1:])[1]
    (gu, h2), arrived = _norm_matmul(x1, w["ffn_norm_g"][l][None], wl["w_gate_up"], 0, name=f"gate_up_{tag}",
                                     tn=wl["w_gate_up"].shape[2] // 4, rider=rider)
    if coming:
        arrived = (*here, *arrived)
    x2 = _ffn_down(gu, wl["w_down"], 0, x1, name=f"down_{tag}")
    saved = dict(x=x, p=p, h=h, ya=ya, yb=yb, yc=yc, merged=merged, x1=x1, gu=gu, h2=h2,
                 att=(k_cb, v_cb, q_c, runs), gate_col0=gate_col0)
    return x2, saved, arrived


def _layer_backward(dx2, w, wl, sv, l, chain):
    tag = f"l{l}"
    c = w["conv_b"].shape[1]
    n_in = wl["w_in"].shape[2]
    n_ff = wl["w_gate_up"].shape[2]
    g, big = {}, {}
    (dgu, act, dx2b), arrived = _ffn_bwd(dx2, sv["gu"], wl["w_down"], 0, name=f"down_bwd_{tag}",
                                         rider=chain.exchange() if chain else None)
    if chain:
        chain.exchanged(arrived)
    big["w_down"] = _matmul_tn(act, dx2b, 0, 1, None, name=f"dw_down_{tag}", t1=act.shape[1] // 2)
    (dx1, dg2), _ = _matmul_nt_normbwd([dgu], wl["w_gate_up"], 0, sv["x1"], w["ffn_norm_g"][l][None], dx2,
                                       name=f"gate_up_bwd_{tag}")
    g["ffn_norm_g"] = jnp.sum(dg2, axis=0)
    big["w_gate_up"] = _matmul_tn(sv["h2"], dgu, 0, 1, None, name=f"dw_gate_up_{tag}", tn=n_ff // 4)
    ys = (sv["ya"], sv["yb"], sv["yc"])
    (dgates, dya, dyb, dyc, dd0, dd1, dd2, dx1b, dbg) = _merge_bwd(
        dx1, ys, sv["p"], w["b_gate"][l][None], wl["w_branch_out"], wl["w_o"], 0, gate_col0=sv["gate_col0"],
        name=f"merge_bwd_{tag}")
    g["b_gate"] = jnp.sum(dbg, axis=0)
    big["w_o"] = _matmul_tn(sv["merged"], dx1b, 0, 1, None, name=f"dw_o_{tag}")
    for i, (y, dd) in enumerate(zip(ys, (dd0, dd1, dd2))):
        big["w_branch_out"] = _matmul_tn(y, dd, i, len(ys), big.get("w_branch_out"), name=f"dw_bo{i}_{tag}")
    dconv, dwc = _conv_bwd(sv["p"], dya, wl["conv_w"][0], w["conv_b"][l][None], name=f"conv_bwd_{tag}")
    dwc = jnp.sum(dwc, axis=1)
    g["conv_w"] = dwc[0:CONV_K]
    g["conv_b"] = dwc[CONV_K]
    dsgu, dln, dws, dbias = _sgu_bwd(sv["p"], dyb, w["sgu_ln_g"][l][None], w["sgu_ln_b"][l][None], w["wtril"][l],
                                     w["wtril_t"][l], w["bias_full"][l], col0=3, name=f"sgu_bwd_{tag}")
    dln = jnp.sum(dln, axis=1)
    g["sgu_ln_g"], g["sgu_ln_b"] = dln[0], dln[1]
    g["sgu_w"] = jnp.where(w["tril"], dws, 0.0)
    g["sgu_b"] = jnp.sum(dbias.reshape(CHUNK, c // CHUNK, CHUNK), axis=2).T
    k_cb, v_cb, q_c, runs = sv["att"]
    (dqt, dkt, dvt), arrived = _att_bwd(k_cb, v_cb, q_c, _cols(_mx(dyc)), runs, w["later"], w["earlier"],
                                        name=f"att_bwd_{tag}", rider=chain.scatter() if chain else None)
    if chain:
        chain.scattered(arrived)
    dqkv, dqkg = _qkv_bwd(sv["p"], dqt, dkt, dvt, w["qg"][l], w["kg"][l], w["headavg"], col0=5, name=f"qkv_bwd_{tag}")
    dqkg = jnp.sum(dqkg.reshape(2, 8 * HEADS, HEAD_DIM), axis=1)
    g["q_norm_g"], g["k_norm_g"] = dqkg[0], dqkg[1]
    pieces = [dconv, dsgu, dqkv, dgates]
    (dx0, dg1), arrived = _matmul_nt_normbwd(pieces, wl["w_in"], 0, sv["x"], w["mix_norm_g"][l][None], dx1,
                                             name=f"in_proj_bwd_{tag}", rider=chain.share() if chain else None)
    if chain:
        chain.shared(arrived)
    g["mix_norm_g"] = jnp.sum(dg1, axis=0)
    col0 = 0
    for k, pc in enumerate(pieces):
        width = pc.shape[1]
        big["w_in"] = _matmul_tn(sv["h"], pc, 0, 1, big.get("w_in"), name=f"dw_in{k}_{tag}", col0=col0, n_total=n_in,
                                 tn=math.gcd(col0, width) if col0 else width)
        col0 += width
    return dx0, g, {n: a.reshape(wl[n].shape[1:]) for n, a in big.items()}


def kernel(x, mix_norm_g, w_in, b_gate, conv_w, conv_b, sgu_ln_g, sgu_ln_b, sgu_w, sgu_b, q_norm_g, k_norm_g, w_branch_out, w_o, ffn_norm_g, w_gate_up, w_down, loss_target, m_mix_norm_g, m_w_in, m_b_gate, m_conv_w, m_conv_b, m_sgu_ln_g, m_sgu_ln_b, m_sgu_w, m_sgu_b, m_q_norm_g, m_k_norm_g, m_w_branch_out, m_w_o, m_ffn_norm_g, m_w_gate_up, m_w_down, v_mix_norm_g, v_w_in, v_b_gate, v_conv_w, v_conv_b, v_sgu_ln_g, v_sgu_ln_b, v_sgu_w, v_sgu_b, v_q_norm_g, v_k_norm_g, v_w_branch_out, v_w_o, v_ffn_norm_g, v_w_gate_up, v_w_down):
    given = dict(locals())
    params = {n: given[n] for n in ORDER}
    moms = {n: (given["m_" + n], given["v_" + n]) for n in ORDER}
    layers = mix_norm_g.shape[0]
    assert layers == 2, "the exchanges split the work of a chip's two cores by layer"
    xs = x[0]
    target = loss_target[0]
    chip = 2 * lax.axis_index("x") + lax.axis_index("y")
    core = lax.axis_index("c")

    c_idx = core.reshape(1).astype(jnp.int32)
    j_idx = chip.reshape(1).astype(jnp.int32)
    width = conv_w.shape[2]
    gathered = BIG + ("conv_w",)
    axes = [BIG_AXIS[n] for n in BIG] + [1]

    def placed(layer):
        arrays = [_place_shard(params[n], layer, BIG_AXIS[n], j_idx, MXU_DTYPE, name=f"place_{n}_l{layer}") for n in BIG]
        mine = lax.dynamic_slice_in_dim(conv_w, layer, 1, axis=0)
        arrays.append(lax.dynamic_update_slice_in_dim(jnp.zeros((1, CONV_K, width * N_CHIPS), F32), mine, chip * width, 2))
        return arrays

    weights = [dict(zip(gathered, _ride_alone(_gather_riders(placed(0), axes, 0)[2:], name="gather_weights_l0"))), None]
    w = {n: params[n] for n in ("mix_norm_g", "b_gate", "conv_b", "sgu_ln_g", "sgu_ln_b", "ffn_norm_g")}
    groups = sgu_w.shape[1]
    tril = jnp.tril(jnp.ones((CHUNK, CHUNK), dtype=bool))
    w["tril"] = tril
    w["wtril"] = _mx(jnp.where(tril, sgu_w, 0.0))
    w["wtril_t"] = w["wtril"].transpose(0, 1, 3, 2)
    w["bias_full"] = jnp.repeat(sgu_b.transpose(0, 2, 1), CHUNK, axis=2)
    w["qg"] = jnp.tile(q_norm_g, (1, HEADS))[:, None, :]
    w["kg"] = jnp.tile(k_norm_g, (1, HEADS))[:, None, :]
    lane = jnp.arange(HEADS * HEAD_DIM) // HEAD_DIM
    w["headavg"] = _mx(jnp.where(lane[:, None] == lane[None, :], 1.0 / HEAD_DIM, 0.0))
    pos = jnp.arange(ATT_BLOCK)
    w["later"] = _mx(jnp.where(pos[None, :] > pos[:, None], 1.0, 0.0))
    w["earlier"] = _mx(jnp.where(pos[None, :] < pos[:, None], 1.0, 0.0))

    saved = []
    act = xs
    for l in range(layers):
        coming = (placed(l + 1), axes, l + 1) if l + 1 < layers else None
        act, sv, arrived = _layer_forward(act, w, weights[l], l, coming)
        saved.append(sv)
        if coming:
            weights[l + 1] = dict(zip(gathered, arrived))
    loss_part, dact = _loss_head(act, target, name="loss_head")
    loss = lax.psum(jnp.sum(loss_part), ("x", "y", "c"))
    grads = [None] * layers
    chain = None
    for l in reversed(range(layers)):
        dact, grads[l], big = _layer_backward(dact, w, weights[l], saved[l], l, chain)
        chain = _ReduceChain([big[n] for n in BIG], l, layers, j_idx, c_idx, chain.finals if chain else None)
    chain.run_alone()
    grad_x = dact[None]
    local = {n: jnp.stack([grads[l][n] for l in range(layers)]) for n in SMALL}
    full = dict(zip(BIG, chain.finals))

    summed = _all_reduce_small([local[n] for n in SMALL], name="grads_all_reduce_small")
    for n, gsum in zip(SMALL, summed):
        full[n] = gsum
    full["conv_w"] = lax.dynamic_slice_in_dim(full["conv_w"], chip * width, width, axis=2)

    out = {}
    for n in BIG:
        out[n] = _adamw(params[n], full[n], *moms[n], name=f"adamw_{n}")
    small = _adamw_small([params[n] for n in SMALL], [full[n] for n in SMALL], [moms[n][0] for n in SMALL],
                         [moms[n][1] for n in SMALL], name="adamw_small")
    for n, triple in zip(SMALL, small):
        out[n] = triple
    return (loss, grad_x, *[full[n] for n in ORDER], *[out[n][0] for n in ORDER], *[out[n][1] for n in ORDER],
            *[out[n][2] for n in ORDER])
```

```python
import functools
import math
from typing import Callable, NamedTuple

import jax
import jax.numpy as jnp
from jax import lax
from jax.experimental import pallas as pl
from jax.experimental.pallas import tpu as pltpu

F32 = jnp.float32
MXU_DTYPE = jnp.bfloat16
WIRE_DTYPE = jnp.bfloat16
ACT_DTYPE = jnp.bfloat16
HALO = 16

EPS = 1e-6
CONV_K = 3
CHUNK = 128
HEADS = 8
HEAD_DIM = 64
ATT_BLOCK = 128
EXP_UNDERFLOW = -88.0

ADAM_LR = 0.001
ADAM_B1 = 0.9
ADAM_B2 = 0.999
ADAM_EPS = 1e-08
ADAM_WD = 0.01
ADAM_STEP = 10

VMEM_LIMIT = 56 * 1024 * 1024
MESH = pl.DeviceIdType.MESH
N_CHIPS = 4
N_DEV = 8
ANY = pl.BlockSpec(memory_space=pl.ANY)


def _params(**kw):
    return pltpu.CompilerParams(vmem_limit_bytes=VMEM_LIMIT, **kw)


class _Rider(NamedTuple):
    arrays: tuple
    sems: tuple
    start: Callable
    finish: Callable


def _call(body, operands, *, name, grid, in_specs, out_specs, out_shape, scratch_shapes=(), rider=None):
    if rider is None:
        outs = pl.pallas_call(body, name=name, grid=grid, in_specs=in_specs, out_specs=out_specs, out_shape=out_shape,
                              scratch_shapes=list(scratch_shapes), compiler_params=_params())(*operands)
        return tuple(outs), ()
    n_in, n_out, n_scr, k = len(in_specs), len(out_specs), len(scratch_shapes), len(rider.arrays)

    def riding(*refs):
        ins, refs = refs[:n_in], refs[n_in + k:]
        outs, carried, refs = refs[:n_out], refs[n_out:n_out + k], refs[n_out + k:]
        scratch, sems = refs[:n_scr], refs[n_scr:]
        first = functools.reduce(jnp.logical_and, [pl.program_id(ax) == 0 for ax in range(len(grid))])
        last = functools.reduce(jnp.logical_and, [pl.program_id(ax) == grid[ax] - 1 for ax in range(len(grid))])

        @pl.when(first)
        def _():
            rider.start(carried, sems)

        body(*ins, *outs, *scratch)

        @pl.when(last)
        def _():
            rider.finish(carried, sems)

    res = pl.pallas_call(
        riding, name=name, grid=grid, in_specs=[*in_specs, *[ANY] * k], out_specs=[*out_specs, *[ANY] * k],
        out_shape=[*out_shape, *[jax.ShapeDtypeStruct(a.shape, a.dtype) for a in rider.arrays]],
        input_output_aliases={n_in + j: n_out + j for j in range(k)},
        scratch_shapes=[*scratch_shapes, *[pltpu.SemaphoreType.DMA(shape) for shape in rider.sems]],
        compiler_params=_params())(*operands, *rider.arrays)
    return tuple(res[:n_out]), tuple(res[n_out:])


def _ride_alone(riders, *, name):
    arrays = riders[0].arrays
    k = len(arrays)
    counts = [len(r.sems) for r in riders]

    def body(*refs):
        carried, sems = refs[k:2 * k], refs[2 * k:]
        for r, count in zip(riders, counts):
            r.start(carried, sems[:count])
            r.finish(carried, sems[:count])
            sems = sems[count:]

    return pl.pallas_call(
        body, name=name, in_specs=[ANY] * k, out_specs=[ANY] * k,
        out_shape=[jax.ShapeDtypeStruct(a.shape, a.dtype) for a in arrays],
        input_output_aliases={j: j for j in range(k)},
        scratch_shapes=[pltpu.SemaphoreType.DMA(shape) for r in riders for shape in r.sems])(*arrays)


def _mx(v):
    return v.astype(MXU_DTYPE)


def _dot(a, b):
    return lax.dot_general(a, b, (((1,), (0,)), ((), ())), preferred_element_type=F32)


def _dot_nt(a, b):
    return lax.dot_general(a, b, (((1,), (1,)), ((), ())), preferred_element_type=F32)


def _dot_tn(a, b):
    return lax.dot_general(a, b, (((0,), (0,)), ((), ())), preferred_element_type=F32)


def _dot_split(const, v):
    hi = _mx(v)
    lo = _mx(v - hi.astype(F32))
    return _dot(const, hi) + _dot(const, lo)


def _dot_split_r(v, const):
    hi = _mx(v)
    lo = _mx(v - hi.astype(F32))
    return _dot(hi, const) + _dot(lo, const)


def _sigmoid(x):
    return 1.0 / (1.0 + jnp.exp(-x))


_INV_SQRT2 = 1.0 / math.sqrt(2.0)
_INV_SQRT2PI = 1.0 / math.sqrt(2.0 * math.pi)


def _gelu(x):
    return 0.5 * x * (1.0 + lax.erf(x * _INV_SQRT2))


def _gelu_grad(x):
    return 0.5 * (1.0 + lax.erf(x * _INV_SQRT2)) + x * jnp.exp(-0.5 * x * x) * _INV_SQRT2PI


def _row_fold(v):
    m, n = v.shape
    return jnp.sum(v.reshape(m // 8, 8, n), axis=0)


def _tile(m, pref):
    t = min(m, pref)
    while m % t:
        t //= 2
    return t


def _resident(block_shape, index_map):
    return pl.BlockSpec(block_shape, index_map, pipeline_mode=pl.Buffered(1))


def _norm_matmul(x, g, w, l, *, name, tm=512, tn=None, rider=None):
    s, d = x.shape
    n = w.shape[2]
    tm = _tile(s, tm)
    tn = tn or n
    assert n % tn == 0

    def body(x_ref, g_ref, w_ref, p_ref, h_ref):
        xf = x_ref[...]
        r = lax.rsqrt(jnp.mean(xf * xf, axis=-1, keepdims=True) + EPS)
        h = _mx(xf * r * g_ref[...])
        h_ref[...] = h
        for j in range(n // tn):
            cols = slice(j * tn, (j + 1) * tn)
            p_ref[:, cols] = _dot(h, w_ref[:, cols]).astype(ACT_DTYPE)

    return _call(
        body, (x, g, w),
        name=name,
        grid=(s // tm,),
        in_specs=[
            pl.BlockSpec((tm, d), lambda i: (i, 0)),
            pl.BlockSpec((1, d), lambda i: (0, 0)),
            _resident((None, d, n), lambda i: (l, 0, 0)),
        ],
        out_specs=[
            pl.BlockSpec((tm, n), lambda i: (i, 0)),
            pl.BlockSpec((tm, d), lambda i: (i, 0)),
        ],
        out_shape=[jax.ShapeDtypeStruct((s, n), ACT_DTYPE), jax.ShapeDtypeStruct((s, d), MXU_DTYPE)],
        rider=rider,
    )


def _conv_taps(u, prev):
    row = lax.broadcasted_iota(jnp.int32, u.shape, 0)
    last, before = prev[HALO - 1:HALO, :], prev[HALO - 2:HALO - 1, :]
    um1 = jnp.where(row == 0, last, pltpu.roll(u, 1, 0))
    um2 = pltpu.roll(u, 2, 0)
    um2 = jnp.where(row == 0, before, jnp.where(row == 1, last, um2))
    return um1, um2


def _f32(ref):
    return ref[...].astype(F32)


def _conv_fwd(p, conv_w, conv_b, *, name, tm=512):
    s = p.shape[0]
    c = conv_w.shape[1]
    tm = _tile(s, tm)
    hb = tm // HALO

    def body(bg_ref, cg_ref, xa_ref, cgp_ref, xap_ref, w_ref, b_ref, y_ref):
        first = pl.program_id(0) == 0
        u = _f32(cg_ref) * _f32(xa_ref)
        prev = jnp.where(first, 0.0, _f32(cgp_ref) * _f32(xap_ref))
        um1, um2 = _conv_taps(u, prev)
        w = w_ref[...]
        y = b_ref[...] + w[0:1, :] * um2 + w[1:2, :] * um1 + w[2:3, :] * u
        y_ref[...] = _mx(_f32(bg_ref) * y)

    halo = lambda col: pl.BlockSpec((HALO, c), lambda i: (jnp.maximum(i * hb - 1, 0), col))
    return pl.pallas_call(
        body,
        name=name,
        grid=(s // tm,),
        in_specs=[
            pl.BlockSpec((tm, c), lambda i: (i, 0)),
            pl.BlockSpec((tm, c), lambda i: (i, 1)),
            pl.BlockSpec((tm, c), lambda i: (i, 2)),
            halo(1),
            halo(2),
            pl.BlockSpec((CONV_K, c), lambda i: (0, 0)),
            pl.BlockSpec((1, c), lambda i: (0, 0)),
        ],
        out_specs=pl.BlockSpec((tm, c), lambda i: (i, 0)),
        out_shape=jax.ShapeDtypeStruct((s, c), MXU_DTYPE),
        compiler_params=_params(),
    )(p, p, p, p, p, conv_w, conv_b)


def _layernorm_stats(x):
    mu = jnp.mean(x, axis=-1, keepdims=True)
    xc = x - mu
    r = lax.rsqrt(jnp.mean(xc * xc, axis=-1, keepdims=True) + EPS)
    return xc * r, r


def _sgu_fwd(p, ln_g, ln_b, wtril, bias_full, *, col0, name, tm=512):
    s = p.shape[0]
    c = ln_g.shape[1]
    groups = c // CHUNK
    tm = _tile(s, tm)

    def body(u_ref, v_ref, g_ref, b_ref, w_ref, bias_ref, y_ref):
        vn, _ = _layernorm_stats(_gelu(_f32(v_ref)))
        vb = _mx(vn * g_ref[...] + b_ref[...])
        for n in range(tm // CHUNK):
            rows = slice(n * CHUNK, (n + 1) * CHUNK)
            for gi in range(groups):
                cols = slice(gi * CHUNK, (gi + 1) * CHUNK)
                mixed = _dot(w_ref[gi], vb[rows, cols]) + bias_ref[:, cols]
                y_ref[rows, cols] = _mx(_gelu(u_ref[rows, cols].astype(F32)) * mixed)

    return pl.pallas_call(
        body,
        name=name,
        grid=(s // tm,),
        in_specs=[
            pl.BlockSpec((tm, c), lambda i: (i, col0)),
            pl.BlockSpec((tm, c), lambda i: (i, col0 + 1)),
            pl.BlockSpec((1, c), lambda i: (0, 0)),
            pl.BlockSpec((1, c), lambda i: (0, 0)),
            pl.BlockSpec((groups, CHUNK, CHUNK), lambda i: (0, 0, 0)),
            pl.BlockSpec((CHUNK, c), lambda i: (0, 0)),
        ],
        out_specs=pl.BlockSpec((tm, c), lambda i: (i, 0)),
        out_shape=jax.ShapeDtypeStruct((s, c), MXU_DTYPE),
        compiler_params=_params(),
    )(p, p, ln_g, ln_b, wtril, bias_full)


def _head_mean(v, headavg):
    return _dot_split_r(v, headavg)


def _qkv_prep(p, qg, kg, headavg, *, col0, name, tm=512):
    s = p.shape[0]
    c = qg.shape[1]
    t = ATT_BLOCK
    tm = _tile(s, tm)
    assert tm % t == 0

    def body(q_ref, k_ref, v_ref, qg_ref, kg_ref, avg_ref, qc_ref, kcb_ref, vcb_ref):
        normed = []
        for src, g_ref in ((q_ref, qg_ref), (k_ref, kg_ref)):
            xf = _f32(src)
            r = lax.rsqrt(_head_mean(xf * xf, avg_ref[...]) + EPS)
            normed.append(xf * r * g_ref[...])
        q_t, k_t, v_t = (_mx(a.T) for a in (*normed, _f32(v_ref)))
        for hh in range(HEADS):
            rows = slice(hh * HEAD_DIM, (hh + 1) * HEAD_DIM)
            qc_ref[hh] = q_t[rows, :]
            for b in range(tm // t):
                kcb_ref[hh, b] = k_t[rows, b * t:(b + 1) * t]
                vcb_ref[hh, b] = v_t[rows, b * t:(b + 1) * t]

    blk = lambda col: pl.BlockSpec((tm, c), lambda i: (i, col))
    vec = pl.BlockSpec((1, c), lambda i: (0, 0))
    blocked = pl.BlockSpec((HEADS, tm // t, HEAD_DIM, t), lambda i: (0, i, 0, 0))
    return pl.pallas_call(
        body,
        name=name,
        grid=(s // tm,),
        in_specs=[blk(col0), blk(col0 + 1), blk(col0 + 2), vec, vec, pl.BlockSpec((c, c), lambda i: (0, 0))],
        out_specs=[pl.BlockSpec((HEADS, HEAD_DIM, tm), lambda i: (0, 0, i)), blocked, blocked],
        out_shape=[jax.ShapeDtypeStruct((HEADS, HEAD_DIM, s), MXU_DTYPE)]
        + [jax.ShapeDtypeStruct((HEADS, s // t, HEAD_DIM, t), MXU_DTYPE)] * 2,
        compiler_params=_params(),
    )(p, p, p, qg, kg, headavg)


def _att_mask(t, key0, qry0):
    kpos = key0 + lax.broadcasted_iota(jnp.int32, (t, t), 0)
    qpos = qry0 + lax.broadcasted_iota(jnp.int32, (t, t), 1)
    return kpos < qpos


def _att_blocks(k_blks, q_ts, scale, mask):
    zs = [(_dot_tn(k, q) if k.shape[0] == q.shape[0] else _dot(k, q)) * scale for k, q in zip(k_blks, q_ts)]
    es = [jnp.exp(-jnp.abs(z)) for z in zs]
    lbs = [jnp.minimum(z, 0.0) - jnp.log1p(e) for z, e in zip(zs, es)]
    l1ms = [jnp.where(mask, lb - z, 0.0) for lb, z in zip(lbs, zs)]
    return zs, es, lbs, l1ms


def _dot_split_each(const, vs):
    his = [_mx(v) for v in vs]
    los = [_mx(v - hi.astype(F32)) for v, hi in zip(vs, his)]
    tops = [_dot(const, hi) for hi in his]
    return [top + _dot(const, lo) for top, lo in zip(tops, los)]


def _max_over(runs):
    m = runs[0]
    for r in runs[1:]:
        m = jnp.maximum(m, r)
    return jnp.max(m)


NOT_SEEN = -1e30


def _row_put(ref, g, j, row):
    j8 = pl.multiple_of((j // 8) * 8, 8)
    blk = ref[g, pl.ds(j8, 8), :]
    sub = lax.broadcasted_iota(jnp.int32, blk.shape, 0)
    ref[g, pl.ds(j8, 8), :] = jnp.where(sub == j - j8, row, blk)


def _row_get(ref, g, j):
    j8 = pl.multiple_of((j // 8) * 8, 8)
    blk = ref[g, pl.ds(j8, 8), :]
    sub = lax.broadcasted_iota(jnp.int32, blk.shape, 0)
    return jnp.sum(jnp.where(sub == j - j8, blk, 0.0), axis=0, keepdims=True)


def _att_fwd(k_hm, qt_hm, vt_hm, later, *, name, hg=8, rider=None):
    h, nb, d, t = k_hm.shape
    s = nb * t
    nbp = -(-nb // 8) * 8
    scale = 1.0 / math.sqrt(d)
    assert h % hg == 0

    def body(k_ref, qt_ref, vt_ref, later_ref, o_ref, runs_ref):
        i = pl.program_id(1)
        q_ts = [qt_ref[g] for g in range(hg)]
        runs_ref[...] = jnp.full(runs_ref.shape, NOT_SEEN, F32)

        def cond(carry):
            j, _, _, rmax = carry
            return jnp.logical_and(j >= 0, rmax > EXP_UNDERFLOW)

        def step(carry):
            j, runs, accs, _ = carry
            mask = _att_mask(t, j * t, i * t)
            heads = range(hg)
            for g in heads:
                _row_put(runs_ref, g, j, runs[g])
            _, _, lbs, l1ms = _att_blocks([k_ref[g, j] for g in heads], q_ts, scale, mask)
            afters = _dot_split_each(later_ref[...], l1ms)
            weights = [_mx(jnp.where(mask, jnp.exp(lbs[g] + afters[g] + runs[g]), 0.0)) for g in heads]
            new_accs = [accs[g] + _dot(vt_ref[g, j], weights[g]) for g in heads]
            new_runs = [runs[g] + jnp.sum(l1ms[g], axis=0, keepdims=True) for g in heads]
            return j - 1, tuple(new_runs), tuple(new_accs), _max_over(new_runs)

        init = (i, tuple(jnp.zeros((1, t), F32) for _ in range(hg)), tuple(jnp.zeros((d, t), F32) for _ in range(hg)),
                jnp.float32(0.0))
        _, _, accs, _ = lax.while_loop(cond, step, init)
        for g in range(hg):
            o_ref[g] = _mx(accs[g])

    return _call(
        body, (k_hm, qt_hm, vt_hm, later),
        name=name,
        grid=(h // hg, nb),
        in_specs=[
            pl.BlockSpec((hg, nb, d, t), lambda hh, i: (hh, 0, 0, 0)),
            pl.BlockSpec((hg, d, t), lambda hh, i: (hh, 0, i)),
            pl.BlockSpec((hg, nb, d, t), lambda hh, i: (hh, 0, 0, 0)),
            pl.BlockSpec((t, t), lambda hh, i: (0, 0)),
        ],
        out_specs=[pl.BlockSpec((hg, d, t), lambda hh, i: (hh, 0, i)),
                   pl.BlockSpec((hg, None, nbp, t), lambda hh, i: (hh, i, 0, 0))],
        out_shape=[jax.ShapeDtypeStruct((h, d, s), MXU_DTYPE), jax.ShapeDtypeStruct((h, nb, nbp, t), F32)],
        rider=rider,
    )


def _merge_fwd(ys, p, b_gate, w_bo, w_o, l, x, *, gate_col0, name, tm=512):
    s, d = x.shape
    _, nbr, c, _ = w_bo.shape
    tm = _tile(s, tm)

    def body(ya_ref, yb_ref, yc_ref, g0_ref, g1_ref, g2_ref, bg_ref, wbo_ref, wo_ref, x_ref, x1_ref, m_ref):
        merged = jnp.zeros((tm, d), F32)
        for i, (y_ref, g_ref) in enumerate(((ya_ref, g0_ref), (yb_ref, g1_ref), (yc_ref, g2_ref))):
            gate = _sigmoid(_f32(g_ref) + bg_ref[:, i * d:(i + 1) * d])
            merged = merged + gate * _dot(y_ref[...], wbo_ref[i])
        mb = _mx(merged)
        m_ref[...] = mb
        x1_ref[...] = x_ref[...] + _dot(mb, wo_ref[...])

    yblk = pl.BlockSpec((tm, c), lambda i: (i, 0))
    gblk = lambda k: pl.BlockSpec((tm, d), lambda i: (i, gate_col0 + k))
    xblk = pl.BlockSpec((tm, d), lambda i: (i, 0))
    return pl.pallas_call(
        body,
        name=name,
        grid=(s // tm,),
        in_specs=[
            yblk, yblk, yblk, gblk(0), gblk(1), gblk(2),
            pl.BlockSpec((1, nbr * d), lambda i: (0, 0)),
            _resident((None, nbr, c, d), lambda i: (l, 0, 0, 0)),
            _resident((None, d, d), lambda i: (l, 0, 0)),
            xblk,
        ],
        out_specs=[xblk, xblk],
        out_shape=[jax.ShapeDtypeStruct((s, d), F32), jax.ShapeDtypeStruct((s, d), MXU_DTYPE)],
        compiler_params=_params(),
    )(*ys, p, p, p, b_gate, w_bo, w_o, x)


def _ffn_down(gu, w_down, l, x, *, name, tm=512):
    s, d = x.shape
    f = w_down.shape[1]
    tm = _tile(s, tm)

    def body(g_ref, u_ref, w_ref, x_ref, o_ref):
        g = _f32(g_ref)
        act = _mx(g * _sigmoid(g) * _f32(u_ref))
        o_ref[...] = x_ref[...] + _dot(act, w_ref[...])

    return pl.pallas_call(
        body,
        name=name,
        grid=(s // tm,),
        in_specs=[
            pl.BlockSpec((tm, f), lambda i: (i, 0)),
            pl.BlockSpec((tm, f), lambda i: (i, 1)),
            _resident((None, f, d), lambda i: (l, 0, 0)),
            pl.BlockSpec((tm, d), lambda i: (i, 0)),
        ],
        out_specs=pl.BlockSpec((tm, d), lambda i: (i, 0)),
        out_shape=jax.ShapeDtypeStruct((s, d), F32),
        compiler_params=_params(),
    )(gu, gu, w_down, x)


def _loss_head(y, target, *, name, tm=512):
    s, d = y.shape
    tm = _tile(s, tm)

    def body(y_ref, t_ref, l_ref, dy_ref):
        @pl.when(pl.program_id(0) == 0)
        def _():
            l_ref[...] = jnp.zeros_like(l_ref)

        err = y_ref[...] - t_ref[...]
        dy_ref[...] = err * (1.0 / d)
        sq = _row_fold(err * err)
        part = sq[:, 0:128]
        for k in range(1, d // 128):
            part = part + sq[:, k * 128:(k + 1) * 128]
        l_ref[...] += part * (0.5 / d)

    blk = pl.BlockSpec((tm, d), lambda i: (i, 0))
    return pl.pallas_call(
        body,
        name=name,
        grid=(s // tm,),
        in_specs=[blk, blk],
        out_specs=[pl.BlockSpec((8, 128), lambda i: (0, 0)), blk],
        out_shape=[jax.ShapeDtypeStruct((8, 128), F32), jax.ShapeDtypeStruct((s, d), F32)],
        compiler_params=_params(),
    )(y, target)


def _matmul_tn(a, b, slot, n_slots, into, *, name, t1=1024, tn=None, tm=1024, col0=0, n_total=None):
    m, k1 = a.shape
    n = b.shape[1]
    n_total = n_total or n
    t1 = _tile(k1, t1)
    tn = tn or n
    tm = _tile(m, tm)
    steps = m // tm
    assert n % tn == 0 and col0 % tn == 0
    cb0 = col0 // tn

    def body(a_ref, b_ref, *refs):
        o_ref = refs[-1]

        @pl.when(pl.program_id(2) == 0)
        def _():
            o_ref[...] = jnp.zeros_like(o_ref)

        o_ref[...] += _dot_tn(a_ref[...], b_ref[...])

    return pl.pallas_call(
        body,
        name=name,
        grid=(k1 // t1, n // tn, steps),
        in_specs=[
            pl.BlockSpec((tm, t1), lambda i, j, k: (k, i)),
            pl.BlockSpec((tm, tn), lambda i, j, k: (k, j)),
        ] + ([] if into is None else [ANY]),
        out_specs=pl.BlockSpec((None, t1, tn), lambda i, j, k: (slot, i, cb0 + j)),
        out_shape=jax.ShapeDtypeStruct((n_slots, k1, n_total), F32),
        input_output_aliases={} if into is None else {2: 0},
        compiler_params=_params(),
    )(a, b, *([] if into is None else [into]))


def _matmul_nt_normbwd(pieces, w, l, x, g, dres, *, name, tm=512, tk=1024, rider=None):
    s, d = x.shape
    k = w.shape[2]
    tm = _tile(s, tm)
    n = len(pieces)
    assert sum(pc.shape[1] for pc in pieces) == k

    def body(*refs):
        piece_refs = refs[:n]
        w_ref, x_ref, g_ref, dres_ref, dx_ref, dg_ref = refs[n:]

        @pl.when(pl.program_id(0) == 0)
        def _():
            dg_ref[...] = jnp.zeros_like(dg_ref)

        dh = jnp.zeros((tm, d), F32)
        off = 0
        for pc, pc_ref in zip(pieces, piece_refs):
            width = pc.shape[1]
            step = _tile(width, tk)
            for c0 in range(0, width, step):
                dh = dh + _dot_nt(pc_ref[:, c0:c0 + step], w_ref[:, off + c0:off + c0 + step])
            off += width
        xf = x_ref[...]
        r = lax.rsqrt(jnp.mean(xf * xf, axis=-1, keepdims=True) + EPS)
        y = xf * r
        dy = dh * g_ref[...]
        dx_ref[...] = dres_ref[...] + r * (dy - y * jnp.mean(dy * y, axis=-1, keepdims=True))
        dg_ref[...] += _row_fold(dh * y)

    xblk = pl.BlockSpec((tm, d), lambda i: (i, 0))
    return _call(
        body, (*pieces, w, x, g, dres),
        name=name,
        grid=(s // tm,),
        in_specs=[pl.BlockSpec((tm, pc.shape[1]), lambda i: (i, 0)) for pc in pieces] + [
            _resident((None, d, k), lambda i: (l, 0, 0)),
            xblk,
            pl.BlockSpec((1, d), lambda i: (0, 0)),
            xblk,
        ],
        out_specs=[xblk, pl.BlockSpec((8, d), lambda i: (0, 0))],
        out_shape=[jax.ShapeDtypeStruct((s, d), F32), jax.ShapeDtypeStruct((8, d), F32)],
        rider=rider,
    )


def _ffn_bwd(dx, gu, w_down, l, *, name, tm=512, rider=None):
    s, d = dx.shape
    f = w_down.shape[1]
    tm = _tile(s, tm)

    def body(dx_ref, g_ref, u_ref, w_ref, dgu_ref, act_ref, dxb_ref):
        dxb = _mx(dx_ref[...])
        dxb_ref[...] = dxb
        dact = _dot_nt(dxb, w_ref[...])
        g = _f32(g_ref)
        u = _f32(u_ref)
        sg = _sigmoid(g)
        silu = g * sg
        act_ref[...] = _mx(silu * u)
        dgu_ref[:, 0:f] = _mx(dact * u * (sg * (1.0 + g * (1.0 - sg))))
        dgu_ref[:, f:2 * f] = _mx(dact * silu)

    fblk = lambda col: pl.BlockSpec((tm, f), lambda i: (i, col))
    dblk = pl.BlockSpec((tm, d), lambda i: (i, 0))
    return _call(
        body, (dx, gu, gu, w_down),
        name=name,
        grid=(s // tm,),
        in_specs=[dblk, fblk(0), fblk(1), _resident((None, f, d), lambda i: (l, 0, 0))],
        out_specs=[pl.BlockSpec((tm, 2 * f), lambda i: (i, 0)), fblk(0), dblk],
        out_shape=[
            jax.ShapeDtypeStruct((s, 2 * f), MXU_DTYPE),
            jax.ShapeDtypeStruct((s, f), MXU_DTYPE),
            jax.ShapeDtypeStruct((s, d), MXU_DTYPE),
        ],
        rider=rider,
    )


def _merge_bwd(dx, ys, p, b_gate, w_bo, w_o, l, *, gate_col0, name, tm=512):
    s, d = dx.shape
    _, nbr, c, _ = w_bo.shape
    tm = _tile(s, tm)

    def body(dx_ref, ya_ref, yb_ref, yc_ref, g0_ref, g1_ref, g2_ref, bg_ref, wbo_ref, wo_ref,
             dgate_ref, dya_ref, dyb_ref, dyc_ref, dd0_ref, dd1_ref, dd2_ref, dxb_ref, dbg_ref):
        @pl.when(pl.program_id(0) == 0)
        def _():
            dbg_ref[...] = jnp.zeros_like(dbg_ref)

        dxb = _mx(dx_ref[...])
        dxb_ref[...] = dxb
        dmerged = _dot_nt(dxb, wo_ref[...])
        branches = ((ya_ref, g0_ref, dya_ref, dd0_ref), (yb_ref, g1_ref, dyb_ref, dd1_ref), (yc_ref, g2_ref, dyc_ref, dd2_ref))
        for i, (y_ref, g_ref, dy_ref, dd_ref) in enumerate(branches):
            cols = slice(i * d, (i + 1) * d)
            gate = _sigmoid(_f32(g_ref) + bg_ref[:, cols])
            yd = _dot(y_ref[...], wbo_ref[i])
            dyd = _mx(dmerged * gate)
            dd_ref[...] = dyd
            dy_ref[...] = _dot_nt(dyd, wbo_ref[i])
            dpre = dmerged * yd * gate * (1.0 - gate)
            dgate_ref[:, cols] = _mx(dpre)
            dbg_ref[:, cols] += _row_fold(dpre)

    yblk = pl.BlockSpec((tm, c), lambda i: (i, 0))
    gblk = lambda k: pl.BlockSpec((tm, d), lambda i: (i, gate_col0 + k))
    dblk = pl.BlockSpec((tm, d), lambda i: (i, 0))
    return pl.pallas_call(
        body,
        name=name,
        grid=(s // tm,),
        in_specs=[
            dblk, yblk, yblk, yblk, gblk(0), gblk(1), gblk(2),
            pl.BlockSpec((1, nbr * d), lambda i: (0, 0)),
            _resident((None, nbr, c, d), lambda i: (l, 0, 0, 0)),
            _resident((None, d, d), lambda i: (l, 0, 0)),
        ],
        out_specs=[pl.BlockSpec((tm, nbr * d), lambda i: (i, 0)), yblk, yblk, yblk, dblk, dblk, dblk, dblk,
                   pl.BlockSpec((8, nbr * d), lambda i: (0, 0))],
        out_shape=[jax.ShapeDtypeStruct((s, nbr * d), MXU_DTYPE)] + [jax.ShapeDtypeStruct((s, c), F32)] * 3
        + [jax.ShapeDtypeStruct((s, d), MXU_DTYPE)] * 4 + [jax.ShapeDtypeStruct((8, nbr * d), F32)],
        compiler_params=_params(),
    )(dx, *ys, p, p, p, b_gate, w_bo, w_o)


def _conv_bwd(p, dya, conv_w, conv_b, *, name, tm=512):
    s = p.shape[0]
    c = conv_w.shape[1]
    tm = _tile(s, tm)
    hb = tm // HALO
    last = s // tm - 1

    def body(bg_ref, cg_ref, xa_ref, cgp_ref, xap_ref, dy_ref, dyn_ref, bgn_ref, w_ref, b_ref,
             dp_ref, dw_ref):
        i = pl.program_id(0)

        @pl.when(i == 0)
        def _():
            dw_ref[...] = jnp.zeros_like(dw_ref)

        cg = _f32(cg_ref)
        xa = _f32(xa_ref)
        u = cg * xa
        prev = jnp.where(i == 0, 0.0, _f32(cgp_ref) * _f32(xap_ref))
        um1, um2 = _conv_taps(u, prev)
        w = w_ref[...]
        y = b_ref[...] + w[0:1, :] * um2 + w[1:2, :] * um1 + w[2:3, :] * u
        dya = dy_ref[...]
        dp_ref[:, 0:c] = _mx(dya * y)
        dyv = dya * _f32(bg_ref)
        nxt = jnp.where(i == last, 0.0, dyn_ref[...] * _f32(bgn_ref))
        row = lax.broadcasted_iota(jnp.int32, dyv.shape, 0)
        dp1 = jnp.where(row == tm - 1, nxt[0:1, :], pltpu.roll(dyv, tm - 1, 0))
        dp2 = pltpu.roll(dyv, tm - 2, 0)
        dp2 = jnp.where(row == tm - 2, nxt[0:1, :], jnp.where(row == tm - 1, nxt[1:2, :], dp2))
        du = w[2:3, :] * dyv + w[1:2, :] * dp1 + w[0:1, :] * dp2
        dp_ref[:, c:2 * c] = _mx(du * xa)
        dp_ref[:, 2 * c:3 * c] = _mx(du * cg)
        dw_ref[0] += _row_fold(dyv * um2)
        dw_ref[1] += _row_fold(dyv * um1)
        dw_ref[2] += _row_fold(dyv * u)
        dw_ref[3] += _row_fold(dyv)

    blk = lambda col: pl.BlockSpec((tm, c), lambda i: (i, col))
    halo = lambda col: pl.BlockSpec((HALO, c), lambda i: (jnp.maximum(i * hb - 1, 0), col))
    nhalo = lambda col: pl.BlockSpec((HALO, c), lambda i: (jnp.minimum((i + 1) * hb, s // HALO - 1), col))
    return pl.pallas_call(
        body,
        name=name,
        grid=(s // tm,),
        in_specs=[blk(0), blk(1), blk(2), halo(1), halo(2), blk(0), nhalo(0), nhalo(0),
                  pl.BlockSpec((CONV_K, c), lambda i: (0, 0)), pl.BlockSpec((1, c), lambda i: (0, 0))],
        out_specs=[pl.BlockSpec((tm, 3 * c), lambda i: (i, 0)), pl.BlockSpec((4, 8, c), lambda i: (0, 0, 0))],
        out_shape=[jax.ShapeDtypeStruct((s, 3 * c), MXU_DTYPE), jax.ShapeDtypeStruct((4, 8, c), F32)],
        compiler_params=_params(),
    )(p, p, p, p, p, dya, dya, p, conv_w, conv_b)


def _sgu_bwd(p, dyb, ln_g, ln_b, wtril, wtril_t, bias_full, *, col0, name, tm=512):
    s = p.shape[0]
    c = ln_g.shape[1]
    groups = c // CHUNK
    tm = _tile(s, tm)

    def body(u_ref, v_ref, dy_ref, g_ref, b_ref, w_ref, wt_ref, bias_ref, duv_ref, dln_ref, dw_ref, dbias_ref, dvn_ref):
        @pl.when(pl.program_id(0) == 0)
        def _():
            dln_ref[...] = jnp.zeros_like(dln_ref)
            dw_ref[...] = jnp.zeros_like(dw_ref)
            dbias_ref[...] = jnp.zeros_like(dbias_ref)

        sv = _f32(v_ref)
        xhat, r = _layernorm_stats(_gelu(sv))
        vb = _mx(xhat * g_ref[...] + b_ref[...])
        for n in range(tm // CHUNK):
            rows = slice(n * CHUNK, (n + 1) * CHUNK)
            for gi in range(groups):
                cols = slice(gi * CHUNK, (gi + 1) * CHUNK)
                su = u_ref[rows, cols].astype(F32)
                dy = dy_ref[rows, cols]
                vblk = vb[rows, cols]
                mixed = _dot(w_ref[gi], vblk) + bias_ref[:, cols]
                duv_ref[rows, cols] = _mx(dy * mixed * _gelu_grad(su))
                dmixed = dy * _gelu(su)
                dmb = _mx(dmixed)
                dvn_ref[rows, cols] = _dot(wt_ref[gi], dmb)
                dw_ref[gi] += _dot_nt(dmb, vblk)
                dbias_ref[:, cols] += dmixed
        dvn = dvn_ref[...]
        dln_ref[0] += _row_fold(dvn * xhat)
        dln_ref[1] += _row_fold(dvn)
        dxh = dvn * g_ref[...]
        dgv = r * (dxh - jnp.mean(dxh, axis=-1, keepdims=True) - xhat * jnp.mean(dxh * xhat, axis=-1, keepdims=True))
        duv_ref[:, c:2 * c] = _mx(dgv * _gelu_grad(sv))

    blk = lambda col: pl.BlockSpec((tm, c), lambda i: (i, col))
    vec = pl.BlockSpec((1, c), lambda i: (0, 0))
    wspec = pl.BlockSpec((groups, CHUNK, CHUNK), lambda i: (0, 0, 0))
    return pl.pallas_call(
        body,
        name=name,
        grid=(s // tm,),
        in_specs=[blk(col0), blk(col0 + 1), blk(0), vec, vec, wspec, wspec, pl.BlockSpec((CHUNK, c), lambda i: (0, 0))],
        out_specs=[pl.BlockSpec((tm, 2 * c), lambda i: (i, 0)), pl.BlockSpec((2, 8, c), lambda i: (0, 0, 0)), wspec,
                   pl.BlockSpec((CHUNK, c), lambda i: (0, 0))],
        out_shape=[jax.ShapeDtypeStruct((s, 2 * c), MXU_DTYPE),
                   jax.ShapeDtypeStruct((2, 8, c), F32), jax.ShapeDtypeStruct((groups, CHUNK, CHUNK), F32),
                   jax.ShapeDtypeStruct((CHUNK, c), F32)],
        scratch_shapes=[pltpu.VMEM((tm, c), F32)],
        compiler_params=_params(),
    )(p, p, dyb, ln_g, ln_b, wtril, wtril_t, bias_full)


def _att_bwd(kt_hm, vt_hm, qt_hm, dot_hm, runs, later, earlier, *, name, hg=4, rider=None):
    h, nb, d, t = kt_hm.shape
    s = nb * t
    nbp = runs.shape[2]
    scale = 1.0 / math.sqrt(d)
    assert h % hg == 0

    def body(kt_ref, vt_ref, qt_ref, dot_ref, runs_ref, later_ref, earlier_ref, dqt_ref, dkt_ref, dvt_ref):
        i = pl.program_id(1)

        @pl.when(i == 0)
        def _():
            dkt_ref[...] = jnp.zeros_like(dkt_ref)
            dvt_ref[...] = jnp.zeros_like(dvt_ref)

        q_ts = [qt_ref[g] for g in range(hg)]
        do_ts = [dot_ref[g] for g in range(hg)]

        best = runs_ref[0]
        for g in range(1, hg):
            best = jnp.maximum(best, runs_ref[g])
        row = lax.broadcasted_iota(jnp.int32, (nbp, 1), 0)
        counts = jnp.logical_and(jnp.max(best, axis=1, keepdims=True) > EXP_UNDERFLOW, row < i)
        seen = jnp.sum(counts.astype(jnp.int32))
        zeros_row = tuple(jnp.zeros((1, t), F32) for _ in range(hg))

        def up(j, carry):
            gsums, dqts = carry
            mask = _att_mask(t, j * t, i * t)
            heads = range(hg)
            zs, es, lbs, l1ms = _att_blocks([kt_ref[g, j] for g in heads], q_ts, scale, mask)
            afters = _dot_split_each(later_ref[...], l1ms)
            das = [_dot_tn(vt_ref[g, j], do_ts[g]) for g in heads]
            weights = [jnp.where(mask, jnp.exp(lbs[g] + afters[g] + _row_get(runs_ref, g, j)), 0.0) for g in heads]
            grs = [das[g] * weights[g] for g in heads]
            gbefores = _dot_split_each(earlier_ref[...], grs)
            dzs = []
            for g in heads:
                inv = 1.0 / (1.0 + es[g])
                pos = zs[g] >= 0.0
                beta = jnp.where(pos, inv, es[g] * inv)
                omb = jnp.where(pos, es[g] * inv, inv)
                dzs.append(_mx(jnp.where(mask, grs[g] * omb - (gbefores[g] + gsums[g]) * beta, 0.0) * scale))
            new_dqts = [dqts[g] + _dot(kt_ref[g, j], dzs[g]) for g in heads]
            for g in heads:
                dkt_ref[g, j] += _dot_nt(q_ts[g], dzs[g])
            for g in heads:
                dvt_ref[g, j] += _dot_nt(do_ts[g], _mx(weights[g]))
            new_gsums = [gsums[g] + jnp.sum(grs[g], axis=0, keepdims=True) for g in heads]
            return tuple(new_gsums), tuple(new_dqts)

        _, dqts = lax.fori_loop(i - seen, i + 1, up, (zeros_row, tuple(jnp.zeros((d, t), F32) for _ in range(hg))))
        for g in range(hg):
            dqt_ref[g] = dqts[g]

    whole = pl.BlockSpec((hg, nb, d, t), lambda hh, i: (hh, 0, 0, 0))
    cols = pl.BlockSpec((hg, d, t), lambda hh, i: (hh, 0, i))
    tri = pl.BlockSpec((t, t), lambda hh, i: (0, 0))
    return _call(
        body, (kt_hm, vt_hm, qt_hm, dot_hm, runs, later, earlier),
        name=name,
        grid=(h // hg, nb),
        in_specs=[whole, whole, cols, cols, pl.BlockSpec((hg, None, nbp, t), lambda hh, i: (hh, i, 0, 0)), tri, tri],
        out_specs=[cols, whole, whole],
        out_shape=[jax.ShapeDtypeStruct((h, d, s), F32), jax.ShapeDtypeStruct((h, nb, d, t), F32),
                   jax.ShapeDtypeStruct((h, nb, d, t), F32)],
        rider=rider,
    )


def _qkv_bwd(p, dqt, dkt, dvt, qg, kg, headavg, *, col0, name, tm=512):
    s = p.shape[0]
    c = qg.shape[1]
    tm = _tile(s, tm)

    t = ATT_BLOCK
    assert tm % t == 0

    def rows_of(blocked_ref):
        parts = []
        for b in range(tm // t):
            cols = jnp.concatenate([blocked_ref[hh, b] for hh in range(HEADS)], axis=0)
            parts.append(cols.T)
        return jnp.concatenate(parts, axis=0)

    def body(q_ref, k_ref, dqt_ref, dkt_ref, dvt_ref, qg_ref, kg_ref, avg_ref, dqkv_ref, dg_ref):
        @pl.when(pl.program_id(0) == 0)
        def _():
            dg_ref[...] = jnp.zeros_like(dg_ref)

        dqn = jnp.concatenate([dqt_ref[hh] for hh in range(HEADS)], axis=0).T
        for n, (src, dn, g_ref) in enumerate(((q_ref, dqn, qg_ref), (k_ref, rows_of(dkt_ref), kg_ref))):
            xf = _f32(src)
            r = lax.rsqrt(_head_mean(xf * xf, avg_ref[...]) + EPS)
            y = xf * r
            dy = dn * g_ref[...]
            dqkv_ref[:, n * c:(n + 1) * c] = _mx(r * (dy - y * _head_mean(dy * y, avg_ref[...])))
            dg_ref[n] += _row_fold(dn * y)
        dqkv_ref[:, 2 * c:3 * c] = _mx(rows_of(dvt_ref))

    blk = lambda col: pl.BlockSpec((tm, c), lambda i: (i, col))
    vec = pl.BlockSpec((1, c), lambda i: (0, 0))
    blocked = pl.BlockSpec((HEADS, tm // t, HEAD_DIM, t), lambda i: (0, i, 0, 0))
    return pl.pallas_call(
        body,
        name=name,
        grid=(s // tm,),
        in_specs=[blk(col0), blk(col0 + 1), pl.BlockSpec((HEADS, HEAD_DIM, tm), lambda i: (0, 0, i)), blocked, blocked,
                  vec, vec, pl.BlockSpec((c, c), lambda i: (0, 0))],
        out_specs=[pl.BlockSpec((tm, 3 * c), lambda i: (i, 0)), pl.BlockSpec((2, 8, c), lambda i: (0, 0, 0))],
        out_shape=[jax.ShapeDtypeStruct((s, 3 * c), MXU_DTYPE), jax.ShapeDtypeStruct((2, 8, c), F32)],
        compiler_params=_params(),
    )(p, p, dqt, dkt, dvt, qg, kg, headavg)


def _place():
    x, y, c = lax.axis_index("x"), lax.axis_index("y"), lax.axis_index("c")
    chips = [(1 - x, y), (x, 1 - y), (1 - x, 1 - y)]
    return x, y, c, chips


def _shard_of(ref, axis, chip, width):
    idx = [slice(None)] * len(ref.shape)
    idx[axis] = pl.ds(chip * width, width)
    return ref.at[tuple(idx)]


def _place_shard(w, layer, axis, chip_idx, dtype, *, name):
    rows, cols = _as_rows(w.shape[1:])
    tr = _row_tile(rows, cols, 2)
    if axis == len(w.shape) - 2:
        out_shape = (1, rows, cols * N_CHIPS)
        out_spec = pl.BlockSpec((None, tr, cols), lambda i, j_ref: (0, i, j_ref[0]))
    else:
        assert axis == 0
        per = rows // tr
        out_shape = (1, rows * N_CHIPS, cols)
        out_spec = pl.BlockSpec((None, tr, cols), lambda i, j_ref: (0, j_ref[0] * per + i, 0))
    full = [1, *w.shape[1:]]
    full[1 + axis] *= N_CHIPS

    def body(j_ref, w_ref, o_ref):
        o_ref[...] = w_ref[...].astype(dtype)

    out = pl.pallas_call(
        body,
        name=name,
        grid_spec=pltpu.PrefetchScalarGridSpec(
            num_scalar_prefetch=1,
            grid=(rows // tr,),
            in_specs=[pl.BlockSpec((None, tr, cols), lambda i, j_ref: (layer, i, 0))],
            out_specs=out_spec,
        ),
        out_shape=jax.ShapeDtypeStruct(out_shape, dtype),
        compiler_params=_params(),
    )(chip_idx, w.reshape(w.shape[0], rows, cols))
    return out.reshape(full)


def _gather_riders(placed, axes, layer):
    n = len(placed)
    widths = [pa.shape[1 + ax] // N_CHIPS for pa, ax in zip(placed, axes)]

    def copies(refs, sems):
        x, y, c, chips = _place()

        def block(a, chip):
            return _shard_of(refs[a].at[0], axes[a], chip, widths[a])

        def over_ici(a, k, chip):
            return pltpu.make_async_remote_copy(
                src_ref=block(a, chip), dst_ref=block(a, chip), send_sem=sems[0].at[a, k], recv_sem=sems[1].at[a, k],
                device_id=(*chips[k], c), device_id_type=MESH)

        def to_sibling(a, k):
            cx, cy = chips[k]
            return pltpu.make_async_remote_copy(
                src_ref=block(a, 2 * cx + cy), dst_ref=block(a, 2 * cx + cy), send_sem=sems[0].at[a, k],
                recv_sem=sems[1].at[a, k], device_id=(x, y, 1 - c), device_id_type=MESH)

        return 2 * x + y, c, chips, over_ici, to_sibling

    def ici_start(refs, sems):
        me, c, _, over_ici, _ = copies(refs, sems)

        @pl.when(c == layer)
        def _():
            for a in range(n):
                for k in range(3):
                    over_ici(a, k, me).start()

    def ici_finish(refs, sems):
        me, c, chips, over_ici, _ = copies(refs, sems)

        @pl.when(c == layer)
        def _():
            for a in range(n):
                for k in range(3):
                    cx, cy = chips[k]
                    over_ici(a, k, 2 * cx + cy).wait_recv()
            for a in range(n):
                for k in range(3):
                    over_ici(a, k, me).wait_send()

    def d2d_start(refs, sems):
        _, c, _, _, to_sibling = copies(refs, sems)

        @pl.when(c == layer)
        def _():
            for a in range(n):
                for k in range(3):
                    to_sibling(a, k).start()

    def d2d_finish(refs, sems):
        _, c, _, _, to_sibling = copies(refs, sems)

        @pl.when(c == layer)
        def _():
            for a in range(n):
                for k in range(3):
                    to_sibling(a, k).wait_send()

        @pl.when(c != layer)
        def _():
            for a in range(n):
                for k in range(3):
                    to_sibling(a, k).wait_recv()

    def both_start(refs, sems):
        ici_start(refs, sems[:2])

    def both_finish(refs, sems):
        me, c, chips, over_ici, _ = copies(refs, sems[:2])
        to_sibling = copies(refs, sems[2:])[4]

        @pl.when(c == layer)
        def _():
            for a in range(n):
                for k in range(3):
                    cx, cy = chips[k]
                    over_ici(a, k, 2 * cx + cy).wait_recv()
                    to_sibling(a, k).start()
            for a in range(n):
                for k in range(3):
                    over_ici(a, k, me).wait_send()
                    to_sibling(a, k).wait_send()

        @pl.when(c != layer)
        def _():
            for a in range(n):
                for k in range(3):
                    to_sibling(a, k).wait_recv()

    arrays, sems = tuple(placed), ((n, 3), (n, 3))
    return (_Rider(arrays, sems, ici_start, ici_finish), _Rider(arrays, sems, d2d_start, d2d_finish),
            _Rider(arrays, sems * 2, both_start, both_finish))


def _join_riders(*riders):
    def each(which):
        def run(refs, sems):
            for r in riders:
                getattr(r, which)(refs[:len(r.arrays)], sems[:len(r.sems)])
                refs, sems = refs[len(r.arrays):], sems[len(r.sems):]
        return run

    return _Rider(sum((r.arrays for r in riders), ()), sum((r.sems for r in riders), ()), each("start"), each("finish"))


def _exchange_rider(grads, others, layer):
    n = len(grads)

    def copy(refs, sems, a):
        x, y, c, _ = _place()
        return pltpu.make_async_remote_copy(
            src_ref=refs[a], dst_ref=refs[n + a], send_sem=sems[0].at[a], recv_sem=sems[1].at[a],
            device_id=(x, y, 1 - c), device_id_type=MESH)

    def start(refs, sems):
        @pl.when(lax.axis_index("c") != layer)
        def _():
            for a in range(n):
                copy(refs, sems, a).start()

    def finish(refs, sems):
        @pl.when(lax.axis_index("c") != layer)
        def _():
            for a in range(n):
                copy(refs, sems, a).wait_send()

        @pl.when(lax.axis_index("c") == layer)
        def _():
            for a in range(n):
                copy(refs, sems, a).wait_recv()

    return _Rider((*grads, *others), ((n,), (n,)), start, finish)


def _scatter_rider(wires, landeds, axes, layer):
    n = len(wires)
    widths = [wr.shape[ax] // N_CHIPS for wr, ax in zip(wires, axes)]

    def copies(refs, sems):
        _, _, c, chips = _place()
        return [pltpu.make_async_remote_copy(
            src_ref=_shard_of(refs[a], axes[a], 2 * cx + cy, widths[a]), dst_ref=refs[n + a].at[k],
            send_sem=sems[0].at[a, k], recv_sem=sems[1].at[a, k], device_id=(cx, cy, c), device_id_type=MESH)
            for a in range(n) for k, (cx, cy) in enumerate(chips)]

    def start(refs, sems):
        @pl.when(lax.axis_index("c") == layer)
        def _():
            for cp in copies(refs, sems):
                cp.start()

    def finish(refs, sems):
        @pl.when(lax.axis_index("c") == layer)
        def _():
            for cp in copies(refs, sems):
                cp.wait()

    return _Rider((*wires, *landeds), ((n, 3), (n, 3)), start, finish)


def _share_rider(finals, layer):
    n = len(finals)

    def copy(refs, sems, a):
        x, y, c, _ = _place()
        return pltpu.make_async_remote_copy(
            src_ref=refs[a].at[layer], dst_ref=refs[a].at[layer], send_sem=sems[0].at[a], recv_sem=sems[1].at[a],
            device_id=(x, y, 1 - c), device_id_type=MESH)

    def start(refs, sems):
        @pl.when(lax.axis_index("c") == layer)
        def _():
            for a in range(n):
                copy(refs, sems, a).start()

    def finish(refs, sems):
        @pl.when(lax.axis_index("c") == layer)
        def _():
            for a in range(n):
                copy(refs, sems, a).wait_send()

        @pl.when(lax.axis_index("c") != layer)
        def _():
            for a in range(n):
                copy(refs, sems, a).wait_recv()

    return _Rider(tuple(finals), ((n,), (n,)), start, finish)


def _small_view(shape):
    size = math.prod(shape)
    return (size // 128, 128) if size % 1024 == 0 else (shape[0], size // shape[0])


def _all_reduce_small(parts, *, name):
    n = len(parts)
    views = [_small_view(a.shape) for a in parts]

    def body(*refs):
        ins, outs, slots = refs[:n], refs[n:2 * n], refs[2 * n:3 * n]
        send_sem, recv_sem = refs[3 * n:]
        x, y, c, _ = _place()
        me = 4 * x + 2 * y + c
        copies = []
        for a in range(n):
            slots[a][me] = ins[a][...]
            for k in range(1, N_DEV):
                peer = (x ^ (k >> 2), y ^ ((k >> 1) & 1), c ^ (k & 1))
                copies.append(pltpu.make_async_remote_copy(
                    src_ref=ins[a], dst_ref=slots[a].at[me], send_sem=send_sem.at[a, k - 1],
                    recv_sem=recv_sem.at[a, k - 1], device_id=peer, device_id_type=MESH))
        for cp in copies:
            cp.start()
        for a in range(n):
            for k in range(1, N_DEV):
                pltpu.make_async_remote_copy(
                    src_ref=ins[a], dst_ref=slots[a].at[me ^ k], send_sem=send_sem.at[a, k - 1],
                    recv_sem=recv_sem.at[a, k - 1], device_id=(x, y, c), device_id_type=MESH).wait()
        for a in range(n):
            total = slots[a][0]
            for dev in range(1, N_DEV):
                total = total + slots[a][dev]
            outs[a][...] = total

    vmem = pl.BlockSpec(memory_space=pltpu.VMEM)
    outs = pl.pallas_call(
        body,
        name=name,
        in_specs=[vmem] * n,
        out_specs=[vmem] * n,
        out_shape=[jax.ShapeDtypeStruct(view, F32) for view in views],
        scratch_shapes=[pltpu.VMEM((N_DEV, *view), F32) for view in views]
        + [pltpu.SemaphoreType.DMA((n, N_DEV - 1)), pltpu.SemaphoreType.DMA((n, N_DEV - 1))],
        compiler_params=_params(),
    )(*[a.reshape(view) for a, view in zip(parts, views)])
    return [o.reshape(a.shape) for o, a in zip(outs, parts)]


def _as_rows(shape):
    cols = shape[-1]
    return math.prod(shape[:-1]), cols


ELEMENTWISE_VMEM = 24 * 1024 * 1024


def _row_tile(rows, cols, n_arrays, sublanes=8):
    cap = ELEMENTWISE_VMEM // (n_arrays * 2 * 4 * cols)
    best = None
    for t in range(sublanes, min(rows, cap) + 1, sublanes):
        if rows % t == 0:
            best = t
    assert best is not None, (rows, cols)
    return best


def _pair_sum(g, other, active, *, name):
    rows, cols = _as_rows(g.shape)
    tr = _row_tile(rows, cols, 4, sublanes=16)

    def body(a_ref, g_ref, o_ref, out_ref, wire_ref):
        @pl.when(a_ref[0] == 1)
        def _():
            total = g_ref[...] + o_ref[...]
            out_ref[...] = total
            wire_ref[...] = total.astype(WIRE_DTYPE)

    blk = pl.BlockSpec((tr, cols), lambda i, a_ref: (i * a_ref[0], 0))
    out, wire = pl.pallas_call(
        body,
        name=name,
        grid_spec=pltpu.PrefetchScalarGridSpec(
            num_scalar_prefetch=1, grid=(rows // tr,), in_specs=[blk, blk], out_specs=[blk, blk]),
        out_shape=[jax.ShapeDtypeStruct((rows, cols), F32), jax.ShapeDtypeStruct((rows, cols), WIRE_DTYPE)],
        compiler_params=_params(),
    )(active, g.reshape(rows, cols), other.reshape(rows, cols))
    return out.reshape(g.shape), wire.reshape(g.shape)


def _chip_sum(mine, got, axis, chip_idx, active, layer, layers, into, *, name):
    shard_shape = got.shape[1:]
    rows, cols = _as_rows(shard_shape)
    tr = _row_tile(rows, cols, 5, sublanes=16)
    if axis == len(mine.shape) - 1:
        m2 = mine.reshape(rows, cols * N_CHIPS)
        mine_spec = pl.BlockSpec((tr, cols), lambda i, j_ref, a_ref: (i * a_ref[0], j_ref[0]))
    else:
        assert axis == 0
        m2 = mine.reshape(N_CHIPS, rows, cols)
        mine_spec = pl.BlockSpec((None, tr, cols), lambda i, j_ref, a_ref: (j_ref[0], i * a_ref[0], 0))

    def body(j_ref, a_ref, m_ref, got_ref, *refs):
        @pl.when(a_ref[0] == 1)
        def _():
            refs[-1][...] = ((m_ref[...] + got_ref[0].astype(F32)) + got_ref[1].astype(F32)) + got_ref[2].astype(F32)

    out = pl.pallas_call(
        body,
        name=name,
        grid_spec=pltpu.PrefetchScalarGridSpec(
            num_scalar_prefetch=2,
            grid=(rows // tr,),
            in_specs=[mine_spec, pl.BlockSpec((3, tr, cols), lambda i, j_ref, a_ref: (0, i * a_ref[0], 0))]
            + ([] if into is None else [ANY]),
            out_specs=pl.BlockSpec((None, tr, cols), lambda i, j_ref, a_ref: (layer, i * a_ref[0], 0)),
        ),
        out_shape=jax.ShapeDtypeStruct((layers, rows, cols), F32),
        input_output_aliases={} if into is None else {4: 0},
        compiler_params=_params(),
    )(chip_idx, active, m2, got.reshape(3, rows, cols), *([] if into is None else [into.reshape(layers, rows, cols)]))
    return out.reshape((layers, *shard_shape))


class _ReduceChain:
    def __init__(self, grads, layer, layers, chip_idx, core_idx, finals):
        self.grads, self.layer, self.layers, self.chip_idx, self.finals = list(grads), layer, layers, chip_idx, finals
        self.active = (core_idx == layer).astype(jnp.int32)
        self.axes = [BIG_AXIS[n] for n in BIG]
        self.pairs = None

    def exchange(self):
        return _exchange_rider(self.grads, [lax.empty(g.shape, F32) for g in self.grads], self.layer)

    def exchanged(self, arrived):
        n = len(self.grads)
        self.pairs = [_pair_sum(g, o, self.active, name=f"pair_sum_{name}_l{self.layer}")
                      for name, g, o in zip(BIG, arrived[:n], arrived[n:])]

    def scatter(self):
        landeds = []
        for (_, wire), ax in zip(self.pairs, self.axes):
            shard = list(wire.shape)
            shard[ax] //= N_CHIPS
            landeds.append(lax.empty((3, *shard), WIRE_DTYPE))
        return _scatter_rider([wire for _, wire in self.pairs], landeds, self.axes, self.layer)

    def scattered(self, arrived):
        n = len(self.grads)
        into = self.finals or [None] * n
        self.finals = [_chip_sum(pair, landed, ax, self.chip_idx, self.active, self.layer, self.layers, old,
                                 name=f"chip_sum_{name}_l{self.layer}")
                       for name, (pair, _), landed, ax, old in zip(BIG, self.pairs, arrived[n:], self.axes, into)]

    def share(self):
        return _share_rider(self.finals, self.layer)

    def shared(self, arrived):
        self.finals = list(arrived)

    def run_alone(self):
        tag = f"l{self.layer}"
        self.exchanged(_ride_alone([self.exchange()], name=f"grads_pair_exchange_{tag}"))
        self.scattered(_ride_alone([self.scatter()], name=f"grads_chip_scatter_{tag}"))
        self.shared(_ride_alone([self.share()], name=f"grads_pair_share_{tag}"))


def _adamw_update(w_ref, g_ref, m_ref, v_ref, d_ref, nm_ref, nv_ref):
    c1 = 1.0 / (1.0 - ADAM_B1 ** ADAM_STEP)
    c2 = 1.0 / (1.0 - ADAM_B2 ** ADAM_STEP)
    gg = g_ref[...]
    nm = ADAM_B1 * m_ref[...] + (1.0 - ADAM_B1) * gg
    nv = ADAM_B2 * v_ref[...] + (1.0 - ADAM_B2) * (gg * gg)
    nm_ref[...] = nm
    nv_ref[...] = nv
    d_ref[...] = -ADAM_LR * ((nm * c1) / (jnp.sqrt(nv * c2) + ADAM_EPS) + ADAM_WD * w_ref[...])


def _adamw_small(ws, gs, ms, vs, *, name):
    n = len(ws)
    views = [_small_view(a.shape) for a in ws]

    def body(*refs):
        w_r, g_r, m_r, v_r = (refs[k * n:(k + 1) * n] for k in range(4))
        d_r, nm_r, nv_r = (refs[(4 + k) * n:(5 + k) * n] for k in range(3))
        for a in range(n):
            _adamw_update(w_r[a], g_r[a], m_r[a], v_r[a], d_r[a], nm_r[a], nv_r[a])

    vmem = pl.BlockSpec(memory_space=pltpu.VMEM)
    flat = lambda arrs: [a.reshape(view) for a, view in zip(arrs, views)]
    outs = pl.pallas_call(
        body,
        name=name,
        in_specs=[vmem] * (4 * n),
        out_specs=[vmem] * (3 * n),
        out_shape=[jax.ShapeDtypeStruct(view, F32) for view in views] * 3,
        compiler_params=_params(),
    )(*flat(ws), *flat(gs), *flat(ms), *flat(vs))
    return [tuple(outs[k * n + a].reshape(ws[a].shape) for k in range(3)) for a in range(n)]


def _adamw(w, g, m, v, *, name):
    shape = w.shape
    rows, cols = _as_rows(shape)
    tr = _row_tile(rows, cols, 7)
    body = functools.partial(_adamw_update)

    blk = pl.BlockSpec((tr, cols), lambda i: (i, 0))
    flat = lambda a: a.reshape(rows, cols)
    outs = pl.pallas_call(
        body,
        name=name,
        grid=(rows // tr,),
        in_specs=[blk] * 4,
        out_specs=[blk] * 3,
        out_shape=[jax.ShapeDtypeStruct((rows, cols), F32)] * 3,
        compiler_params=_params(),
    )(flat(w), flat(g), flat(m), flat(v))
    return tuple(o.reshape(shape) for o in outs)


def _cols(a):
    return a.reshape(a.shape[0], HEADS, HEAD_DIM).transpose(1, 2, 0)


def _from_cols(a):
    h, d, s = a.shape
    return a.transpose(2, 0, 1).reshape(s, h * d)


BIG = ("w_in", "w_branch_out", "w_o", "w_gate_up", "w_down")
BIG_AXIS = {"w_in": 1, "w_branch_out": 2, "w_o": 0, "w_gate_up": 1, "w_down": 0}
SMALL = ("mix_norm_g", "b_gate", "conv_w", "conv_b", "sgu_ln_g", "sgu_ln_b", "sgu_w", "sgu_b", "q_norm_g", "k_norm_g",
         "ffn_norm_g")
ORDER = ("mix_norm_g", "w_in", "b_gate", "conv_w", "conv_b", "sgu_ln_g", "sgu_ln_b", "sgu_w", "sgu_b", "q_norm_g",
         "k_norm_g", "w_branch_out", "w_o", "ffn_norm_g", "w_gate_up", "w_down")


def _layer_forward(x, w, wl, l, coming):
    n_in = wl["w_in"].shape[2]
    gate_col0 = (n_in - 3 * x.shape[1]) // x.shape[1]
    tag = f"l{l}"
    rider = None
    if coming:
        placed, axes, nxt = coming
        half = len(placed) // 2
        first, second = (placed[:half], axes[:half], nxt), (placed[half:], axes[half:], nxt)
        rider = _gather_riders(*first)[0]
    (p, h), arrived = _norm_matmul(x, w["mix_norm_g"][l][None], wl["w_in"], 0, name=f"in_proj_{tag}", tn=n_in // 7,
                                   rider=rider)
    ya = _conv_fwd(p, wl["conv_w"][0], w["conv_b"][l][None], name=f"conv_{tag}")
    yb = _sgu_fwd(p, w["sgu_ln_g"][l][None], w["sgu_ln_b"][l][None], w["wtril"][l], w["bias_full"][l], col0=3,
                  name=f"sgu_{tag}")
    q_c, k_cb, v_cb = _qkv_prep(p, w["qg"][l], w["kg"][l], w["headavg"], col0=5, name=f"qkv_{tag}")
    if coming:
        rider = _join_riders(_gather_riders(list(arrived), *first[1:])[1], _gather_riders(*second)[0])
    (out_t, runs), arrived = _att_fwd(k_cb, q_c, v_cb, w["later"], name=f"att_{tag}", rider=rider)
    yc = _from_cols(out_t)
    x1, merged = _merge_fwd((ya, yb, yc), p, w["b_gate"][l][None], wl["w_branch_out"], wl["w_o"], 0, x,
                            gate_col0=gate_col0, name=f"merge_{tag}")
    if coming:
        here = arrived[:half]
        rider = _gather_riders(list(arrived[half:]), *second[1:])[1]
    (gu, h2), arrived = _norm_matmul(x1, w["ffn_norm_g"][l][None], wl["w_gate_up"], 0, name=f"gate_up_{tag}",
                                     tn=wl["w_gate_up"].shape[2] // 4, rider=rider)
    if coming:
        arrived = (*here, *arrived)
    x2 = _ffn_down(gu, wl["w_down"], 0, x1, name=f"down_{tag}")
    saved = dict(x=x, p=p, h=h, ya=ya, yb=yb, yc=yc, merged=merged, x1=x1, gu=gu, h2=h2,
                 att=(k_cb, v_cb, q_c, runs), gate_col0=gate_col0)
    return x2, saved, arrived


def _layer_backward(dx2, w, wl, sv, l, chain):
    tag = f"l{l}"
    c = w["conv_b"].shape[1]
    n_in = wl["w_in"].shape[2]
    n_ff = wl["w_gate_up"].shape[2]
    g, big = {}, {}
    (dgu, act, dx2b), arrived = _ffn_bwd(dx2, sv["gu"], wl["w_down"], 0, name=f"down_bwd_{tag}",
                                         rider=chain.exchange() if chain else None)
    if chain:
        chain.exchanged(arrived)
    big["w_down"] = _matmul_tn(act, dx2b, 0, 1, None, name=f"dw_down_{tag}", t1=act.shape[1] // 2)
    (dx1, dg2), _ = _matmul_nt_normbwd([dgu], wl["w_gate_up"], 0, sv["x1"], w["ffn_norm_g"][l][None], dx2,
                                       name=f"gate_up_bwd_{tag}")
    g["ffn_norm_g"] = jnp.sum(dg2, axis=0)
    big["w_gate_up"] = _matmul_tn(sv["h2"], dgu, 0, 1, None, name=f"dw_gate_up_{tag}", t1=512)
    ys = (sv["ya"], sv["yb"], sv["yc"])
    (dgates, dya, dyb, dyc, dd0, dd1, dd2, dx1b, dbg) = _merge_bwd(
        dx1, ys, sv["p"], w["b_gate"][l][None], wl["w_branch_out"], wl["w_o"], 0, gate_col0=sv["gate_col0"],
        name=f"merge_bwd_{tag}")
    g["b_gate"] = jnp.sum(dbg, axis=0)
    big["w_o"] = _matmul_tn(sv["merged"], dx1b, 0, 1, None, name=f"dw_o_{tag}")
    for i, (y, dd) in enumerate(zip(ys, (dd0, dd1, dd2))):
        big["w_branch_out"] = _matmul_tn(y, dd, i, len(ys), big.get("w_branch_out"), name=f"dw_bo{i}_{tag}")
    dconv, dwc = _conv_bwd(sv["p"], dya, wl["conv_w"][0], w["conv_b"][l][None], name=f"conv_bwd_{tag}")
    dwc = jnp.sum(dwc, axis=1)
    g["conv_w"] = dwc[0:CONV_K]
    g["conv_b"] = dwc[CONV_K]
    dsgu, dln, dws, dbias = _sgu_bwd(sv["p"], dyb, w["sgu_ln_g"][l][None], w["sgu_ln_b"][l][None], w["wtril"][l],
                                     w["wtril_t"][l], w["bias_full"][l], col0=3, name=f"sgu_bwd_{tag}")
    dln = jnp.sum(dln, axis=1)
    g["sgu_ln_g"], g["sgu_ln_b"] = dln[0], dln[1]
    g["sgu_w"] = jnp.where(w["tril"], dws, 0.0)
    g["sgu_b"] = jnp.sum(dbias.reshape(CHUNK, c // CHUNK, CHUNK), axis=2).T
    k_cb, v_cb, q_c, runs = sv["att"]
    (dqt, dkt, dvt), arrived = _att_bwd(k_cb, v_cb, q_c, _cols(_mx(dyc)), runs, w["later"], w["earlier"],
                                        name=f"att_bwd_{tag}", rider=chain.scatter() if chain else None)
    if chain:
        chain.scattered(arrived)
    dqkv, dqkg = _qkv_bwd(sv["p"], dqt, dkt, dvt, w["qg"][l], w["kg"][l], w["headavg"], col0=5, name=f"qkv_bwd_{tag}")
    dqkg = jnp.sum(dqkg.reshape(2, 8 * HEADS, HEAD_DIM), axis=1)
    g["q_norm_g"], g["k_norm_g"] = dqkg[0], dqkg[1]
    pieces = [dconv, dsgu, dqkv, dgates]
    (dx0, dg1), arrived = _matmul_nt_normbwd(pieces, wl["w_in"], 0, sv["x"], w["mix_norm_g"][l][None], dx1,
                                             name=f"in_proj_bwd_{tag}", rider=chain.share() if chain else None)
    if chain:
        chain.shared(arrived)
    g["mix_norm_g"] = jnp.sum(dg1, axis=0)
    col0 = 0
    for k, pc in enumerate(pieces):
        width = pc.shape[1]
        big["w_in"] = _matmul_tn(sv["h"], pc, 0, 1, big.get("w_in"), name=f"dw_in{k}_{tag}", col0=col0, n_total=n_in,
                                 tn=math.gcd(col0, width) if col0 else width)
        col0 += width
    return dx0, g, {n: a.reshape(wl[n].shape[1:]) for n, a in big.items()}


def kernel(x, mix_norm_g, w_in, b_gate, conv_w, conv_b, sgu_ln_g, sgu_ln_b, sgu_w, sgu_b, q_norm_g, k_norm_g, w_branch_out, w_o, ffn_norm_g, w_gate_up, w_down, loss_target, m_mix_norm_g, m_w_in, m_b_gate, m_conv_w, m_conv_b, m_sgu_ln_g, m_sgu_ln_b, m_sgu_w, m_sgu_b, m_q_norm_g, m_k_norm_g, m_w_branch_out, m_w_o, m_ffn_norm_g, m_w_gate_up, m_w_down, v_mix_norm_g, v_w_in, v_b_gate, v_conv_w, v_conv_b, v_sgu_ln_g, v_sgu_ln_b, v_sgu_w, v_sgu_b, v_q_norm_g, v_k_norm_g, v_w_branch_out, v_w_o, v_ffn_norm_g, v_w_gate_up, v_w_down):
    given = dict(locals())
    params = {n: given[n] for n in ORDER}
    moms = {n: (given["m_" + n], given["v_" + n]) for n in ORDER}
    layers = mix_norm_g.shape[0]
    assert layers == 2, "the exchanges split the work of a chip's two cores by layer"
    xs = x[0]
    target = loss_target[0]
    chip = 2 * lax.axis_index("x") + lax.axis_index("y")
    core = lax.axis_index("c")

    c_idx = core.reshape(1).astype(jnp.int32)
    j_idx = chip.reshape(1).astype(jnp.int32)
    width = conv_w.shape[2]
    gathered = BIG + ("conv_w",)
    axes = [BIG_AXIS[n] for n in BIG] + [1]

    def placed(layer):
        arrays = [_place_shard(params[n], layer, BIG_AXIS[n], j_idx, MXU_DTYPE, name=f"place_{n}_l{layer}") for n in BIG]
        mine = lax.dynamic_slice_in_dim(conv_w, layer, 1, axis=0)
        arrays.append(lax.dynamic_update_slice_in_dim(jnp.zeros((1, CONV_K, width * N_CHIPS), F32), mine, chip * width, 2))
        return arrays

    weights = [dict(zip(gathered, _ride_alone(_gather_riders(placed(0), axes, 0)[2:], name="gather_weights_l0"))), None]
    w = {n: params[n] for n in ("mix_norm_g", "b_gate", "conv_b", "sgu_ln_g", "sgu_ln_b", "ffn_norm_g")}
    groups = sgu_w.shape[1]
    tril = jnp.tril(jnp.ones((CHUNK, CHUNK), dtype=bool))
    w["tril"] = tril
    w["wtril"] = _mx(jnp.where(tril, sgu_w, 0.0))
    w["wtril_t"] = w["wtril"].transpose(0, 1, 3, 2)
    w["bias_full"] = jnp.repeat(sgu_b.transpose(0, 2, 1), CHUNK, axis=2)
    w["qg"] = jnp.tile(q_norm_g, (1, HEADS))[:, None, :]
    w["kg"] = jnp.tile(k_norm_g, (1, HEADS))[:, None, :]
    lane = jnp.arange(HEADS * HEAD_DIM) // HEAD_DIM
    w["headavg"] = _mx(jnp.where(lane[:, None] == lane[None, :], 1.0 / HEAD_DIM, 0.0))
    pos = jnp.arange(ATT_BLOCK)
    w["later"] = _mx(jnp.where(pos[None, :] > pos[:, None], 1.0, 0.0))
    w["earlier"] = _mx(jnp.where(pos[None, :] < pos[:, None], 1.0, 0.0))

    saved = []
    act = xs
    for l in range(layers):
        coming = (placed(l + 1), axes, l + 1) if l + 1 < layers else None
        act, sv, arrived = _layer_forward(act, w, weights[l], l, coming)
        saved.append(sv)
        if coming:
            weights[l + 1] = dict(zip(gathered, arrived))
    loss_part, dact = _loss_head(act, target, name="loss_head")
    loss = lax.psum(jnp.sum(loss_part), ("x", "y", "c"))
    grads = [None] * layers
    chain = None
    for l in reversed(range(layers)):
        dact, grads[l], big = _layer_backward(dact, w, weights[l], saved[l], l, chain)
        chain = _ReduceChain([big[n] for n in BIG], l, layers, j_idx, c_idx, chain.finals if chain else None)
    chain.run_alone()
    grad_x = dact[None]
    local = {n: jnp.stack([grads[l][n] for l in range(layers)]) for n in SMALL}
    full = dict(zip(BIG, chain.finals))

    summed = _all_reduce_small([local[n] for n in SMALL], name="grads_all_reduce_small")
    for n, gsum in zip(SMALL, summed):
        full[n] = gsum
    full["conv_w"] = lax.dynamic_slice_in_dim(full["conv_w"], chip * width, width, axis=2)

    out = {}
    for n in BIG:
        out[n] = _adamw(params[n], full[n], *moms[n], name=f"adamw_{n}")
    small = _adamw_small([params[n] for n in SMALL], [full[n] for n in SMALL], [moms[n][0] for n in SMALL],
                         [moms[n][1] for n in SMALL], name="adamw_small")
    for n, triple in zip(SMALL, small):
        out[n] = triple
    return (loss, grad_x, *[full[n] for n in ORDER], *[out[n][0] for n in ORDER], *[out[n][1] for n in ORDER],
            *[out[n][2] for n in ORDER])
```

```python
import functools
import math
from typing import Callable, NamedTuple

import jax
import jax.numpy as jnp
from jax import lax
from jax.experimental import pallas as pl
from jax.experimental.pallas import tpu as pltpu

F32 = jnp.float32
MXU_DTYPE = jnp.bfloat16
WIRE_DTYPE = jnp.bfloat16
ACT_DTYPE = jnp.bfloat16
HALO = 16

EPS = 1e-6
CONV_K = 3
CHUNK = 128
HEADS = 8
HEAD_DIM = 64
ATT_BLOCK = 128
EXP_UNDERFLOW = -88.0

ADAM_LR = 0.001
ADAM_B1 = 0.9
ADAM_B2 = 0.999
ADAM_EPS = 1e-08
ADAM_WD = 0.01
ADAM_STEP = 10

VMEM_LIMIT = 56 * 1024 * 1024
MESH = pl.DeviceIdType.MESH
N_CHIPS = 4
N_DEV = 8
ANY = pl.BlockSpec(memory_space=pl.ANY)


def _params(**kw):
    return pltpu.CompilerParams(vmem_limit_bytes=VMEM_LIMIT, **kw)


class _Rider(NamedTuple):
    arrays: tuple
    sems: tuple
    start: Callable
    finish: Callable


def _call(body, operands, *, name, grid, in_specs, out_specs, out_shape, scratch_shapes=(), rider=None):
    if rider is None:
        outs = pl.pallas_call(body, name=name, grid=grid, in_specs=in_specs, out_specs=out_specs, out_shape=out_shape,
                              scratch_shapes=list(scratch_shapes), compiler_params=_params())(*operands)
        return tuple(outs), ()
    n_in, n_out, n_scr, k = len(in_specs), len(out_specs), len(scratch_shapes), len(rider.arrays)

    def riding(*refs):
        ins, refs = refs[:n_in], refs[n_in + k:]
        outs, carried, refs = refs[:n_out], refs[n_out:n_out + k], refs[n_out + k:]
        scratch, sems = refs[:n_scr], refs[n_scr:]
        first = functools.reduce(jnp.logical_and, [pl.program_id(ax) == 0 for ax in range(len(grid))])
        last = functools.reduce(jnp.logical_and, [pl.program_id(ax) == grid[ax] - 1 for ax in range(len(grid))])

        @pl.when(first)
        def _():
            rider.start(carried, sems)

        body(*ins, *outs, *scratch)

        @pl.when(last)
        def _():
            rider.finish(carried, sems)

    res = pl.pallas_call(
        riding, name=name, grid=grid, in_specs=[*in_specs, *[ANY] * k], out_specs=[*out_specs, *[ANY] * k],
        out_shape=[*out_shape, *[jax.ShapeDtypeStruct(a.shape, a.dtype) for a in rider.arrays]],
        input_output_aliases={n_in + j: n_out + j for j in range(k)},
        scratch_shapes=[*scratch_shapes, *[pltpu.SemaphoreType.DMA(shape) for shape in rider.sems]],
        compiler_params=_params())(*operands, *rider.arrays)
    return tuple(res[:n_out]), tuple(res[n_out:])


def _ride_alone(riders, *, name):
    arrays = riders[0].arrays
    k = len(arrays)
    counts = [len(r.sems) for r in riders]

    def body(*refs):
        carried, sems = refs[k:2 * k], refs[2 * k:]
        for r, count in zip(riders, counts):
            r.start(carried, sems[:count])
            r.finish(carried, sems[:count])
            sems = sems[count:]

    return pl.pallas_call(
        body, name=name, in_specs=[ANY] * k, out_specs=[ANY] * k,
        out_shape=[jax.ShapeDtypeStruct(a.shape, a.dtype) for a in arrays],
        input_output_aliases={j: j for j in range(k)},
        scratch_shapes=[pltpu.SemaphoreType.DMA(shape) for r in riders for shape in r.sems])(*arrays)


def _mx(v):
    return v.astype(MXU_DTYPE)


def _dot(a, b):
    return lax.dot_general(a, b, (((1,), (0,)), ((), ())), preferred_element_type=F32)


def _dot_nt(a, b):
    return lax.dot_general(a, b, (((1,), (1,)), ((), ())), preferred_element_type=F32)


def _dot_tn(a, b):
    return lax.dot_general(a, b, (((0,), (0,)), ((), ())), preferred_element_type=F32)


def _dot_split_r(v, const):
    hi = _mx(v)
    lo = _mx(v - hi.astype(F32))
    return _dot(hi, const) + _dot(lo, const)


def _sigmoid(x):
    return 1.0 / (1.0 + jnp.exp(-x))


_INV_SQRT2 = 1.0 / math.sqrt(2.0)
_INV_SQRT2PI = 1.0 / math.sqrt(2.0 * math.pi)


def _gelu(x):
    return 0.5 * x * (1.0 + lax.erf(x * _INV_SQRT2))


def _gelu_grad(x):
    return 0.5 * (1.0 + lax.erf(x * _INV_SQRT2)) + x * jnp.exp(-0.5 * x * x) * _INV_SQRT2PI


def _row_fold(v):
    m, n = v.shape
    return jnp.sum(v.reshape(m // 8, 8, n), axis=0)


def _tile(m, pref):
    t = min(m, pref)
    while m % t:
        t //= 2
    return t


def _resident(block_shape, index_map):
    return pl.BlockSpec(block_shape, index_map, pipeline_mode=pl.Buffered(1))


def _norm_matmul(x, g, w, l, *, name, tm=512, tn=None, rider=None):
    s, d = x.shape
    n = w.shape[2]
    tm = _tile(s, tm)
    tn = tn or n
    assert n % tn == 0

    def body(x_ref, g_ref, w_ref, p_ref, h_ref):
        xf = x_ref[...]
        r = lax.rsqrt(jnp.mean(xf * xf, axis=-1, keepdims=True) + EPS)
        h = _mx(xf * r * g_ref[...])
        h_ref[...] = h
        for j in range(n // tn):
            cols = slice(j * tn, (j + 1) * tn)
            p_ref[:, cols] = _dot(h, w_ref[:, cols]).astype(ACT_DTYPE)

    return _call(
        body, (x, g, w),
        name=name,
        grid=(s // tm,),
        in_specs=[
            pl.BlockSpec((tm, d), lambda i: (i, 0)),
            pl.BlockSpec((1, d), lambda i: (0, 0)),
            _resident((None, d, n), lambda i: (l, 0, 0)),
        ],
        out_specs=[
            pl.BlockSpec((tm, n), lambda i: (i, 0)),
            pl.BlockSpec((tm, d), lambda i: (i, 0)),
        ],
        out_shape=[jax.ShapeDtypeStruct((s, n), ACT_DTYPE), jax.ShapeDtypeStruct((s, d), MXU_DTYPE)],
        rider=rider,
    )


def _conv_taps(u, prev):
    row = lax.broadcasted_iota(jnp.int32, u.shape, 0)
    last, before = prev[HALO - 1:HALO, :], prev[HALO - 2:HALO - 1, :]
    um1 = jnp.where(row == 0, last, pltpu.roll(u, 1, 0))
    um2 = pltpu.roll(u, 2, 0)
    um2 = jnp.where(row == 0, before, jnp.where(row == 1, last, um2))
    return um1, um2


def _f32(ref):
    return ref[...].astype(F32)


def _conv_fwd(p, conv_w, conv_b, *, name, tm=512):
    s = p.shape[0]
    c = conv_w.shape[1]
    tm = _tile(s, tm)
    hb = tm // HALO

    def body(bg_ref, cg_ref, xa_ref, cgp_ref, xap_ref, w_ref, b_ref, y_ref):
        first = pl.program_id(0) == 0
        u = _f32(cg_ref) * _f32(xa_ref)
        prev = jnp.where(first, 0.0, _f32(cgp_ref) * _f32(xap_ref))
        um1, um2 = _conv_taps(u, prev)
        w = w_ref[...]
        y = b_ref[...] + w[0:1, :] * um2 + w[1:2, :] * um1 + w[2:3, :] * u
        y_ref[...] = _mx(_f32(bg_ref) * y)

    halo = lambda col: pl.BlockSpec((HALO, c), lambda i: (jnp.maximum(i * hb - 1, 0), col))
    return pl.pallas_call(
        body,
        name=name,
        grid=(s // tm,),
        in_specs=[
            pl.BlockSpec((tm, c), lambda i: (i, 0)),
            pl.BlockSpec((tm, c), lambda i: (i, 1)),
            pl.BlockSpec((tm, c), lambda i: (i, 2)),
            halo(1),
            halo(2),
            pl.BlockSpec((CONV_K, c), lambda i: (0, 0)),
            pl.BlockSpec((1, c), lambda i: (0, 0)),
        ],
        out_specs=pl.BlockSpec((tm, c), lambda i: (i, 0)),
        out_shape=jax.ShapeDtypeStruct((s, c), MXU_DTYPE),
        compiler_params=_params(),
    )(p, p, p, p, p, conv_w, conv_b)


def _layernorm_stats(x):
    mu = jnp.mean(x, axis=-1, keepdims=True)
    xc = x - mu
    r = lax.rsqrt(jnp.mean(xc * xc, axis=-1, keepdims=True) + EPS)
    return xc * r, r


def _sgu_fwd(p, ln_g, ln_b, wtril, bias_full, *, col0, name, tm=512):
    s = p.shape[0]
    c = ln_g.shape[1]
    groups = c // CHUNK
    tm = _tile(s, tm)

    def body(u_ref, v_ref, g_ref, b_ref, w_ref, bias_ref, y_ref):
        vn, _ = _layernorm_stats(_gelu(_f32(v_ref)))
        vb = _mx(vn * g_ref[...] + b_ref[...])
        for n in range(tm // CHUNK):
            rows = slice(n * CHUNK, (n + 1) * CHUNK)
            for gi in range(groups):
                cols = slice(gi * CHUNK, (gi + 1) * CHUNK)
                mixed = _dot(w_ref[gi], vb[rows, cols]) + bias_ref[:, cols]
                y_ref[rows, cols] = _mx(_gelu(u_ref[rows, cols].astype(F32)) * mixed)

    return pl.pallas_call(
        body,
        name=name,
        grid=(s // tm,),
        in_specs=[
            pl.BlockSpec((tm, c), lambda i: (i, col0)),
            pl.BlockSpec((tm, c), lambda i: (i, col0 + 1)),
            pl.BlockSpec((1, c), lambda i: (0, 0)),
            pl.BlockSpec((1, c), lambda i: (0, 0)),
            pl.BlockSpec((groups, CHUNK, CHUNK), lambda i: (0, 0, 0)),
            pl.BlockSpec((CHUNK, c), lambda i: (0, 0)),
        ],
        out_specs=pl.BlockSpec((tm, c), lambda i: (i, 0)),
        out_shape=jax.ShapeDtypeStruct((s, c), MXU_DTYPE),
        compiler_params=_params(),
    )(p, p, ln_g, ln_b, wtril, bias_full)


def _head_mean(v, headavg):
    return _dot_split_r(v, headavg)


def _qkv_prep(p, qg, kg, headavg, *, col0, name, tm=512):
    s = p.shape[0]
    c = qg.shape[1]
    t = ATT_BLOCK
    tm = _tile(s, tm)
    assert tm % t == 0

    def body(q_ref, k_ref, v_ref, qg_ref, kg_ref, avg_ref, qc_ref, kcb_ref, vcb_ref):
        normed = []
        for src, g_ref in ((q_ref, qg_ref), (k_ref, kg_ref)):
            xf = _f32(src)
            r = lax.rsqrt(_head_mean(xf * xf, avg_ref[...]) + EPS)
            normed.append(xf * r * g_ref[...])
        q_t, k_t, v_t = (_mx(a.T) for a in (*normed, _f32(v_ref)))
        for hh in range(HEADS):
            rows = slice(hh * HEAD_DIM, (hh + 1) * HEAD_DIM)
            qc_ref[hh] = q_t[rows, :]
            for b in range(tm // t):
                kcb_ref[hh, b] = k_t[rows, b * t:(b + 1) * t]
                vcb_ref[hh, b] = v_t[rows, b * t:(b + 1) * t]

    blk = lambda col: pl.BlockSpec((tm, c), lambda i: (i, col))
    vec = pl.BlockSpec((1, c), lambda i: (0, 0))
    blocked = pl.BlockSpec((HEADS, tm // t, HEAD_DIM, t), lambda i: (0, i, 0, 0))
    return pl.pallas_call(
        body,
        name=name,
        grid=(s // tm,),
        in_specs=[blk(col0), blk(col0 + 1), blk(col0 + 2), vec, vec, pl.BlockSpec((c, c), lambda i: (0, 0))],
        out_specs=[pl.BlockSpec((HEADS, HEAD_DIM, tm), lambda i: (0, 0, i)), blocked, blocked],
        out_shape=[jax.ShapeDtypeStruct((HEADS, HEAD_DIM, s), MXU_DTYPE)]
        + [jax.ShapeDtypeStruct((HEADS, s // t, HEAD_DIM, t), MXU_DTYPE)] * 2,
        compiler_params=_params(),
    )(p, p, p, qg, kg, headavg)


def _att_mask(t, key0, qry0):
    kpos = key0 + lax.broadcasted_iota(jnp.int32, (t, t), 0)
    qpos = qry0 + lax.broadcasted_iota(jnp.int32, (t, t), 1)
    return kpos < qpos


def _att_blocks(k_blks, q_ts, scale, mask):
    zs = [(_dot_tn(k, q) if k.shape[0] == q.shape[0] else _dot(k, q)) * scale for k, q in zip(k_blks, q_ts)]
    es = [jnp.exp(-jnp.abs(z)) for z in zs]
    lbs = [jnp.minimum(z, 0.0) - jnp.log1p(e) for z, e in zip(zs, es)]
    l1ms = [jnp.where(mask, lb - z, 0.0) for lb, z in zip(lbs, zs)]
    return zs, es, lbs, l1ms


def _dot_split_each(const, vs):
    his = [_mx(v) for v in vs]
    los = [_mx(v - hi.astype(F32)) for v, hi in zip(vs, his)]
    tops = [_dot(const, hi) for hi in his]
    return [top + _dot(const, lo) for top, lo in zip(tops, los)]


def _max_over(runs):
    m = runs[0]
    for r in runs[1:]:
        m = jnp.maximum(m, r)
    return jnp.max(m)


NOT_SEEN = -1e30


def _row_put(ref, g, j, row):
    j8 = pl.multiple_of((j // 8) * 8, 8)
    blk = ref[g, pl.ds(j8, 8), :]
    sub = lax.broadcasted_iota(jnp.int32, blk.shape, 0)
    ref[g, pl.ds(j8, 8), :] = jnp.where(sub == j - j8, row, blk)


def _row_get(ref, g, j):
    j8 = pl.multiple_of((j // 8) * 8, 8)
    blk = ref[g, pl.ds(j8, 8), :]
    sub = lax.broadcasted_iota(jnp.int32, blk.shape, 0)
    return jnp.sum(jnp.where(sub == j - j8, blk, 0.0), axis=0, keepdims=True)


def _att_fwd(k_hm, qt_hm, vt_hm, later, *, name, hg=8, rider=None):
    h, nb, d, t = k_hm.shape
    s = nb * t
    nbp = -(-nb // 8) * 8
    scale = 1.0 / math.sqrt(d)
    assert h % hg == 0

    def body(k_ref, qt_ref, vt_ref, later_ref, o_ref, runs_ref):
        i = pl.program_id(1)
        q_ts = [qt_ref[g] for g in range(hg)]
        runs_ref[...] = jnp.full(runs_ref.shape, NOT_SEEN, F32)

        def cond(carry):
            j, _, _, rmax = carry
            return jnp.logical_and(j >= 0, rmax > EXP_UNDERFLOW)

        def step(carry):
            j, runs, accs, _ = carry
            mask = _att_mask(t, j * t, i * t)
            heads = range(hg)
            for g in heads:
                _row_put(runs_ref, g, j, runs[g])
            _, _, lbs, l1ms = _att_blocks([k_ref[g, j] for g in heads], q_ts, scale, mask)
            afters = _dot_split_each(later_ref[...], l1ms)
            weights = [_mx(jnp.where(mask, jnp.exp(lbs[g] + afters[g] + runs[g]), 0.0)) for g in heads]
            new_accs = [accs[g] + _dot(vt_ref[g, j], weights[g]) for g in heads]
            new_runs = [runs[g] + jnp.sum(l1ms[g], axis=0, keepdims=True) for g in heads]
            return j - 1, tuple(new_runs), tuple(new_accs), _max_over(new_runs)

        init = (i, tuple(jnp.zeros((1, t), F32) for _ in range(hg)), tuple(jnp.zeros((d, t), F32) for _ in range(hg)),
                jnp.float32(0.0))
        _, _, accs, _ = lax.while_loop(cond, step, init)
        for g in range(hg):
            o_ref[g] = _mx(accs[g])

    return _call(
        body, (k_hm, qt_hm, vt_hm, later),
        name=name,
        grid=(h // hg, nb),
        in_specs=[
            pl.BlockSpec((hg, nb, d, t), lambda hh, i: (hh, 0, 0, 0)),
            pl.BlockSpec((hg, d, t), lambda hh, i: (hh, 0, i)),
            pl.BlockSpec((hg, nb, d, t), lambda hh, i: (hh, 0, 0, 0)),
            pl.BlockSpec((t, t), lambda hh, i: (0, 0)),
        ],
        out_specs=[pl.BlockSpec((hg, d, t), lambda hh, i: (hh, 0, i)),
                   pl.BlockSpec((hg, None, nbp, t), lambda hh, i: (hh, i, 0, 0))],
        out_shape=[jax.ShapeDtypeStruct((h, d, s), MXU_DTYPE), jax.ShapeDtypeStruct((h, nb, nbp, t), F32)],
        rider=rider,
    )


def _heads_to_rows(ref):
    return _mx(jnp.concatenate([ref[hh].astype(F32) for hh in range(HEADS)], axis=0).T)


def _merge_fwd(ys, p, b_gate, w_bo, w_o, l, x, *, gate_col0, name, tm=512):
    s, d = x.shape
    _, nbr, c, _ = w_bo.shape
    tm = _tile(s, tm)

    def body(ya_ref, yb_ref, yct_ref, g0_ref, g1_ref, g2_ref, bg_ref, wbo_ref, wo_ref, x_ref, x1_ref, m_ref):
        merged = jnp.zeros((tm, d), F32)
        branches = ((ya_ref[...], g0_ref), (yb_ref[...], g1_ref), (_heads_to_rows(yct_ref), g2_ref))
        for i, (y, g_ref) in enumerate(branches):
            gate = _sigmoid(_f32(g_ref) + bg_ref[:, i * d:(i + 1) * d])
            merged = merged + gate * _dot(y, wbo_ref[i])
        mb = _mx(merged)
        m_ref[...] = mb
        x1_ref[...] = x_ref[...] + _dot(mb, wo_ref[...])

    yblk = pl.BlockSpec((tm, c), lambda i: (i, 0))
    tblk = pl.BlockSpec((HEADS, HEAD_DIM, tm), lambda i: (0, 0, i))
    gblk = lambda k: pl.BlockSpec((tm, d), lambda i: (i, gate_col0 + k))
    xblk = pl.BlockSpec((tm, d), lambda i: (i, 0))
    return pl.pallas_call(
        body,
        name=name,
        grid=(s // tm,),
        in_specs=[
            yblk, yblk, tblk, gblk(0), gblk(1), gblk(2),
            pl.BlockSpec((1, nbr * d), lambda i: (0, 0)),
            _resident((None, nbr, c, d), lambda i: (l, 0, 0, 0)),
            _resident((None, d, d), lambda i: (l, 0, 0)),
            xblk,
        ],
        out_specs=[xblk, xblk],
        out_shape=[jax.ShapeDtypeStruct((s, d), F32), jax.ShapeDtypeStruct((s, d), MXU_DTYPE)],
        compiler_params=_params(),
    )(*ys, p, p, p, b_gate, w_bo, w_o, x)


def _ffn_down(gu, w_down, l, x, *, name, tm=512):
    s, d = x.shape
    f = w_down.shape[1]
    tm = _tile(s, tm)

    def body(g_ref, u_ref, w_ref, x_ref, o_ref):
        g = _f32(g_ref)
        act = _mx(g * _sigmoid(g) * _f32(u_ref))
        o_ref[...] = x_ref[...] + _dot(act, w_ref[...])

    return pl.pallas_call(
        body,
        name=name,
        grid=(s // tm,),
        in_specs=[
            pl.BlockSpec((tm, f), lambda i: (i, 0)),
            pl.BlockSpec((tm, f), lambda i: (i, 1)),
            _resident((None, f, d), lambda i: (l, 0, 0)),
            pl.BlockSpec((tm, d), lambda i: (i, 0)),
        ],
        out_specs=pl.BlockSpec((tm, d), lambda i: (i, 0)),
        out_shape=jax.ShapeDtypeStruct((s, d), F32),
        compiler_params=_params(),
    )(gu, gu, w_down, x)


def _loss_head(y, target, *, name, tm=512):
    s, d = y.shape
    tm = _tile(s, tm)

    def body(y_ref, t_ref, l_ref, dy_ref):
        @pl.when(pl.program_id(0) == 0)
        def _():
            l_ref[...] = jnp.zeros_like(l_ref)

        err = y_ref[...] - t_ref[...]
        dy_ref[...] = err * (1.0 / d)
        sq = _row_fold(err * err)
        part = sq[:, 0:128]
        for k in range(1, d // 128):
            part = part + sq[:, k * 128:(k + 1) * 128]
        l_ref[...] += part * (0.5 / d)

    blk = pl.BlockSpec((tm, d), lambda i: (i, 0))
    return pl.pallas_call(
        body,
        name=name,
        grid=(s // tm,),
        in_specs=[blk, blk],
        out_specs=[pl.BlockSpec((8, 128), lambda i: (0, 0)), blk],
        out_shape=[jax.ShapeDtypeStruct((8, 128), F32), jax.ShapeDtypeStruct((s, d), F32)],
        compiler_params=_params(),
    )(y, target)


def _matmul_tn(a, b, slot, n_slots, into, *, name, t1=1024, tn=None, tm=1024, col0=0, n_total=None, a_transposed=False):
    m, k1 = a.shape[::-1] if a_transposed else a.shape
    n = b.shape[1]
    n_total = n_total or n
    t1 = _tile(k1, t1)
    tn = tn or n
    tm = _tile(m, tm)
    steps = m // tm
    assert n % tn == 0 and col0 % tn == 0
    cb0 = col0 // tn

    def body(a_ref, b_ref, *refs):
        o_ref = refs[-1]

        @pl.when(pl.program_id(2) == 0)
        def _():
            o_ref[...] = jnp.zeros_like(o_ref)

        o_ref[...] += (_dot if a_transposed else _dot_tn)(a_ref[...], b_ref[...])

    return pl.pallas_call(
        body,
        name=name,
        grid=(k1 // t1, n // tn, steps),
        in_specs=[
            pl.BlockSpec((t1, tm), lambda i, j, k: (i, k)) if a_transposed else pl.BlockSpec((tm, t1), lambda i, j, k: (k, i)),
            pl.BlockSpec((tm, tn), lambda i, j, k: (k, j)),
        ] + ([] if into is None else [ANY]),
        out_specs=pl.BlockSpec((None, t1, tn), lambda i, j, k: (slot, i, cb0 + j)),
        out_shape=jax.ShapeDtypeStruct((n_slots, k1, n_total), F32),
        input_output_aliases={} if into is None else {2: 0},
        compiler_params=_params(),
    )(a, b, *([] if into is None else [into]))


def _matmul_nt_normbwd(pieces, w, l, x, g, dres, *, name, tm=512, tk=1024, rider=None):
    s, d = x.shape
    k = w.shape[2]
    tm = _tile(s, tm)
    n = len(pieces)
    assert sum(pc.shape[1] for pc in pieces) == k

    def body(*refs):
        piece_refs = refs[:n]
        w_ref, x_ref, g_ref, dres_ref, dx_ref, dg_ref = refs[n:]

        @pl.when(pl.program_id(0) == 0)
        def _():
            dg_ref[...] = jnp.zeros_like(dg_ref)

        dh = jnp.zeros((tm, d), F32)
        off = 0
        for pc, pc_ref in zip(pieces, piece_refs):
            width = pc.shape[1]
            step = _tile(width, tk)
            for c0 in range(0, width, step):
                dh = dh + _dot_nt(pc_ref[:, c0:c0 + step], w_ref[:, off + c0:off + c0 + step])
            off += width
        xf = x_ref[...]
        r = lax.rsqrt(jnp.mean(xf * xf, axis=-1, keepdims=True) + EPS)
        y = xf * r
        dy = dh * g_ref[...]
        dx_ref[...] = dres_ref[...] + r * (dy - y * jnp.mean(dy * y, axis=-1, keepdims=True))
        dg_ref[...] += _row_fold(dh * y)

    xblk = pl.BlockSpec((tm, d), lambda i: (i, 0))
    return _call(
        body, (*pieces, w, x, g, dres),
        name=name,
        grid=(s // tm,),
        in_specs=[pl.BlockSpec((tm, pc.shape[1]), lambda i: (i, 0)) for pc in pieces] + [
            _resident((None, d, k), lambda i: (l, 0, 0)),
            xblk,
            pl.BlockSpec((1, d), lambda i: (0, 0)),
            xblk,
        ],
        out_specs=[xblk, pl.BlockSpec((8, d), lambda i: (0, 0))],
        out_shape=[jax.ShapeDtypeStruct((s, d), F32), jax.ShapeDtypeStruct((8, d), F32)],
        rider=rider,
    )


def _ffn_bwd(dx, gu, w_down, l, *, name, tm=512, rider=None):
    s, d = dx.shape
    f = w_down.shape[1]
    tm = _tile(s, tm)

    def body(dx_ref, g_ref, u_ref, w_ref, dgu_ref, act_ref, dxb_ref):
        dxb = _mx(dx_ref[...])
        dxb_ref[...] = dxb
        dact = _dot_nt(dxb, w_ref[...])
        g = _f32(g_ref)
        u = _f32(u_ref)
        sg = _sigmoid(g)
        silu = g * sg
        act_ref[...] = _mx(silu * u)
        dgu_ref[:, 0:f] = _mx(dact * u * (sg * (1.0 + g * (1.0 - sg))))
        dgu_ref[:, f:2 * f] = _mx(dact * silu)

    fblk = lambda col: pl.BlockSpec((tm, f), lambda i: (i, col))
    dblk = pl.BlockSpec((tm, d), lambda i: (i, 0))
    return _call(
        body, (dx, gu, gu, w_down),
        name=name,
        grid=(s // tm,),
        in_specs=[dblk, fblk(0), fblk(1), _resident((None, f, d), lambda i: (l, 0, 0))],
        out_specs=[pl.BlockSpec((tm, 2 * f), lambda i: (i, 0)), fblk(0), dblk],
        out_shape=[
            jax.ShapeDtypeStruct((s, 2 * f), MXU_DTYPE),
            jax.ShapeDtypeStruct((s, f), MXU_DTYPE),
            jax.ShapeDtypeStruct((s, d), MXU_DTYPE),
        ],
        rider=rider,
    )


def _merge_bwd(dx, ys, p, b_gate, w_bo, w_o, l, *, gate_col0, name, tm=512):
    s, d = dx.shape
    _, nbr, c, _ = w_bo.shape
    tm = _tile(s, tm)

    def body(dx_ref, ya_ref, yb_ref, yct_ref, g0_ref, g1_ref, g2_ref, bg_ref, wbo_ref, wo_ref,
             dgate_ref, dya_ref, dyb_ref, dyct_ref, dd0_ref, dd1_ref, dd2_ref, dxb_ref, dbg_ref):
        @pl.when(pl.program_id(0) == 0)
        def _():
            dbg_ref[...] = jnp.zeros_like(dbg_ref)

        dxb = _mx(dx_ref[...])
        dxb_ref[...] = dxb
        dmerged = _dot_nt(dxb, wo_ref[...])
        branches = ((ya_ref[...], g0_ref, dd0_ref), (yb_ref[...], g1_ref, dd1_ref), (_heads_to_rows(yct_ref), g2_ref, dd2_ref))
        dys = []
        for i, (y, g_ref, dd_ref) in enumerate(branches):
            cols = slice(i * d, (i + 1) * d)
            gate = _sigmoid(_f32(g_ref) + bg_ref[:, cols])
            yd = _dot(y, wbo_ref[i])
            dyd = _mx(dmerged * gate)
            dd_ref[...] = dyd
            dys.append(_dot_nt(dyd, wbo_ref[i]))
            dpre = dmerged * yd * gate * (1.0 - gate)
            dgate_ref[:, cols] = _mx(dpre)
            dbg_ref[:, cols] += _row_fold(dpre)
        dya_ref[...] = dys[0]
        dyb_ref[...] = dys[1]
        dyc_t = _mx(dys[2].T)
        for hh in range(HEADS):
            dyct_ref[hh] = dyc_t[hh * HEAD_DIM:(hh + 1) * HEAD_DIM, :]

    yblk = pl.BlockSpec((tm, c), lambda i: (i, 0))
    tblk = pl.BlockSpec((HEADS, HEAD_DIM, tm), lambda i: (0, 0, i))
    gblk = lambda k: pl.BlockSpec((tm, d), lambda i: (i, gate_col0 + k))
    dblk = pl.BlockSpec((tm, d), lambda i: (i, 0))
    return pl.pallas_call(
        body,
        name=name,
        grid=(s // tm,),
        in_specs=[
            dblk, yblk, yblk, tblk, gblk(0), gblk(1), gblk(2),
            pl.BlockSpec((1, nbr * d), lambda i: (0, 0)),
            _resident((None, nbr, c, d), lambda i: (l, 0, 0, 0)),
            _resident((None, d, d), lambda i: (l, 0, 0)),
        ],
        out_specs=[pl.BlockSpec((tm, nbr * d), lambda i: (i, 0)), yblk, yblk, tblk, dblk, dblk, dblk, dblk,
                   pl.BlockSpec((8, nbr * d), lambda i: (0, 0))],
        out_shape=[jax.ShapeDtypeStruct((s, nbr * d), MXU_DTYPE)] + [jax.ShapeDtypeStruct((s, c), F32)] * 2
        + [jax.ShapeDtypeStruct((HEADS, HEAD_DIM, s), MXU_DTYPE)]
        + [jax.ShapeDtypeStruct((s, d), MXU_DTYPE)] * 4 + [jax.ShapeDtypeStruct((8, nbr * d), F32)],
        compiler_params=_params(),
    )(dx, *ys, p, p, p, b_gate, w_bo, w_o)


def _conv_bwd(p, dya, conv_w, conv_b, *, name, tm=512):
    s = p.shape[0]
    c = conv_w.shape[1]
    tm = _tile(s, tm)
    hb = tm // HALO
    last = s // tm - 1

    def body(bg_ref, cg_ref, xa_ref, cgp_ref, xap_ref, dy_ref, dyn_ref, bgn_ref, w_ref, b_ref,
             dp_ref, dw_ref):
        i = pl.program_id(0)

        @pl.when(i == 0)
        def _():
            dw_ref[...] = jnp.zeros_like(dw_ref)

        cg = _f32(cg_ref)
        xa = _f32(xa_ref)
        u = cg * xa
        prev = jnp.where(i == 0, 0.0, _f32(cgp_ref) * _f32(xap_ref))
        um1, um2 = _conv_taps(u, prev)
        w = w_ref[...]
        y = b_ref[...] + w[0:1, :] * um2 + w[1:2, :] * um1 + w[2:3, :] * u
        dya = dy_ref[...]
        dp_ref[:, 0:c] = _mx(dya * y)
        dyv = dya * _f32(bg_ref)
        nxt = jnp.where(i == last, 0.0, dyn_ref[...] * _f32(bgn_ref))
        row = lax.broadcasted_iota(jnp.int32, dyv.shape, 0)
        dp1 = jnp.where(row == tm - 1, nxt[0:1, :], pltpu.roll(dyv, tm - 1, 0))
        dp2 = pltpu.roll(dyv, tm - 2, 0)
        dp2 = jnp.where(row == tm - 2, nxt[0:1, :], jnp.where(row == tm - 1, nxt[1:2, :], dp2))
        du = w[2:3, :] * dyv + w[1:2, :] * dp1 + w[0:1, :] * dp2
        dp_ref[:, c:2 * c] = _mx(du * xa)
        dp_ref[:, 2 * c:3 * c] = _mx(du * cg)
        dw_ref[0] += _row_fold(dyv * um2)
        dw_ref[1] += _row_fold(dyv * um1)
        dw_ref[2] += _row_fold(dyv * u)
        dw_ref[3] += _row_fold(dyv)

    blk = lambda col: pl.BlockSpec((tm, c), lambda i: (i, col))
    halo = lambda col: pl.BlockSpec((HALO, c), lambda i: (jnp.maximum(i * hb - 1, 0), col))
    nhalo = lambda col: pl.BlockSpec((HALO, c), lambda i: (jnp.minimum((i + 1) * hb, s // HALO - 1), col))
    return pl.pallas_call(
        body,
        name=name,
        grid=(s // tm,),
        in_specs=[blk(0), blk(1), blk(2), halo(1), halo(2), blk(0), nhalo(0), nhalo(0),
                  pl.BlockSpec((CONV_K, c), lambda i: (0, 0)), pl.BlockSpec((1, c), lambda i: (0, 0))],
        out_specs=[pl.BlockSpec((tm, 3 * c), lambda i: (i, 0)), pl.BlockSpec((4, 8, c), lambda i: (0, 0, 0))],
        out_shape=[jax.ShapeDtypeStruct((s, 3 * c), MXU_DTYPE), jax.ShapeDtypeStruct((4, 8, c), F32)],
        compiler_params=_params(),
    )(p, p, p, p, p, dya, dya, p, conv_w, conv_b)


def _sgu_bwd(p, dyb, ln_g, ln_b, wtril, wtril_t, bias_full, *, col0, name, tm=512):
    s = p.shape[0]
    c = ln_g.shape[1]
    groups = c // CHUNK
    tm = _tile(s, tm)

    def body(u_ref, v_ref, dy_ref, g_ref, b_ref, w_ref, wt_ref, bias_ref, duv_ref, dln_ref, dw_ref, dbias_ref, dvn_ref):
        @pl.when(pl.program_id(0) == 0)
        def _():
            dln_ref[...] = jnp.zeros_like(dln_ref)
            dw_ref[...] = jnp.zeros_like(dw_ref)
            dbias_ref[...] = jnp.zeros_like(dbias_ref)

        sv = _f32(v_ref)
        xhat, r = _layernorm_stats(_gelu(sv))
        vb = _mx(xhat * g_ref[...] + b_ref[...])
        for n in range(tm // CHUNK):
            rows = slice(n * CHUNK, (n + 1) * CHUNK)
            for gi in range(groups):
                cols = slice(gi * CHUNK, (gi + 1) * CHUNK)
                su = u_ref[rows, cols].astype(F32)
                dy = dy_ref[rows, cols]
                vblk = vb[rows, cols]
                mixed = _dot(w_ref[gi], vblk) + bias_ref[:, cols]
                duv_ref[rows, cols] = _mx(dy * mixed * _gelu_grad(su))
                dmixed = dy * _gelu(su)
                dmb = _mx(dmixed)
                dvn_ref[rows, cols] = _dot(wt_ref[gi], dmb)
                dw_ref[gi] += _dot_nt(dmb, vblk)
                dbias_ref[:, cols] += dmixed
        dvn = dvn_ref[...]
        dln_ref[0] += _row_fold(dvn * xhat)
        dln_ref[1] += _row_fold(dvn)
        dxh = dvn * g_ref[...]
        dgv = r * (dxh - jnp.mean(dxh, axis=-1, keepdims=True) - xhat * jnp.mean(dxh * xhat, axis=-1, keepdims=True))
        duv_ref[:, c:2 * c] = _mx(dgv * _gelu_grad(sv))

    blk = lambda col: pl.BlockSpec((tm, c), lambda i: (i, col))
    vec = pl.BlockSpec((1, c), lambda i: (0, 0))
    wspec = pl.BlockSpec((groups, CHUNK, CHUNK), lambda i: (0, 0, 0))
    return pl.pallas_call(
        body,
        name=name,
        grid=(s // tm,),
        in_specs=[blk(col0), blk(col0 + 1), blk(0), vec, vec, wspec, wspec, pl.BlockSpec((CHUNK, c), lambda i: (0, 0))],
        out_specs=[pl.BlockSpec((tm, 2 * c), lambda i: (i, 0)), pl.BlockSpec((2, 8, c), lambda i: (0, 0, 0)), wspec,
                   pl.BlockSpec((CHUNK, c), lambda i: (0, 0))],
        out_shape=[jax.ShapeDtypeStruct((s, 2 * c), MXU_DTYPE),
                   jax.ShapeDtypeStruct((2, 8, c), F32), jax.ShapeDtypeStruct((groups, CHUNK, CHUNK), F32),
                   jax.ShapeDtypeStruct((CHUNK, c), F32)],
        scratch_shapes=[pltpu.VMEM((tm, c), F32)],
        compiler_params=_params(),
    )(p, p, dyb, ln_g, ln_b, wtril, wtril_t, bias_full)


def _att_bwd(kt_hm, vt_hm, qt_hm, dot_hm, runs, later, earlier, *, name, hg=4, rider=None):
    h, nb, d, t = kt_hm.shape
    s = nb * t
    nbp = runs.shape[2]
    scale = 1.0 / math.sqrt(d)
    assert h % hg == 0

    def body(kt_ref, vt_ref, qt_ref, dot_ref, runs_ref, later_ref, earlier_ref, dqt_ref, dkt_ref, dvt_ref):
        i = pl.program_id(1)

        @pl.when(i == 0)
        def _():
            dkt_ref[...] = jnp.zeros_like(dkt_ref)
            dvt_ref[...] = jnp.zeros_like(dvt_ref)

        q_ts = [qt_ref[g] for g in range(hg)]
        do_ts = [dot_ref[g] for g in range(hg)]

        best = runs_ref[0]
        for g in range(1, hg):
            best = jnp.maximum(best, runs_ref[g])
        row = lax.broadcasted_iota(jnp.int32, (nbp, 1), 0)
        counts = jnp.logical_and(jnp.max(best, axis=1, keepdims=True) > EXP_UNDERFLOW, row < i)
        seen = jnp.sum(counts.astype(jnp.int32))
        zeros_row = tuple(jnp.zeros((1, t), F32) for _ in range(hg))

        def up(j, carry):
            gsums, dqts = carry
            mask = _att_mask(t, j * t, i * t)
            heads = range(hg)
            zs, es, lbs, l1ms = _att_blocks([kt_ref[g, j] for g in heads], q_ts, scale, mask)
            afters = _dot_split_each(later_ref[...], l1ms)
            das = [_dot_tn(vt_ref[g, j], do_ts[g]) for g in heads]
            weights = [jnp.where(mask, jnp.exp(lbs[g] + afters[g] + _row_get(runs_ref, g, j)), 0.0) for g in heads]
            grs = [das[g] * weights[g] for g in heads]
            gbefores = _dot_split_each(earlier_ref[...], grs)
            dzs = []
            for g in heads:
                inv = 1.0 / (1.0 + es[g])
                pos = zs[g] >= 0.0
                beta = jnp.where(pos, inv, es[g] * inv)
                omb = jnp.where(pos, es[g] * inv, inv)
                dzs.append(_mx(jnp.where(mask, grs[g] * omb - (gbefores[g] + gsums[g]) * beta, 0.0) * scale))
            new_dqts = [dqts[g] + _dot(kt_ref[g, j], dzs[g]) for g in heads]
            for g in heads:
                dkt_ref[g, j] += _dot_nt(q_ts[g], dzs[g])
            for g in heads:
                dvt_ref[g, j] += _dot_nt(do_ts[g], _mx(weights[g]))
            new_gsums = [gsums[g] + jnp.sum(grs[g], axis=0, keepdims=True) for g in heads]
            return tuple(new_gsums), tuple(new_dqts)

        _, dqts = lax.fori_loop(i - seen, i + 1, up, (zeros_row, tuple(jnp.zeros((d, t), F32) for _ in range(hg))))
        for g in range(hg):
            dqt_ref[g] = dqts[g]

    whole = pl.BlockSpec((hg, nb, d, t), lambda hh, i: (hh, 0, 0, 0))
    cols = pl.BlockSpec((hg, d, t), lambda hh, i: (hh, 0, i))
    tri = pl.BlockSpec((t, t), lambda hh, i: (0, 0))
    return _call(
        body, (kt_hm, vt_hm, qt_hm, dot_hm, runs, later, earlier),
        name=name,
        grid=(h // hg, nb),
        in_specs=[whole, whole, cols, cols, pl.BlockSpec((hg, None, nbp, t), lambda hh, i: (hh, i, 0, 0)), tri, tri],
        out_specs=[cols, whole, whole],
        out_shape=[jax.ShapeDtypeStruct((h, d, s), F32), jax.ShapeDtypeStruct((h, nb, d, t), F32),
                   jax.ShapeDtypeStruct((h, nb, d, t), F32)],
        rider=rider,
    )


def _qkv_bwd(p, dqt, dkt, dvt, qg, kg, headavg, *, col0, name, tm=512):
    s = p.shape[0]
    c = qg.shape[1]
    tm = _tile(s, tm)

    t = ATT_BLOCK
    assert tm % t == 0

    def rows_of(blocked_ref):
        parts = []
        for b in range(tm // t):
            cols = jnp.concatenate([blocked_ref[hh, b] for hh in range(HEADS)], axis=0)
            parts.append(cols.T)
        return jnp.concatenate(parts, axis=0)

    def body(q_ref, k_ref, dqt_ref, dkt_ref, dvt_ref, qg_ref, kg_ref, avg_ref, dqkv_ref, dg_ref):
        @pl.when(pl.program_id(0) == 0)
        def _():
            dg_ref[...] = jnp.zeros_like(dg_ref)

        dqn = jnp.concatenate([dqt_ref[hh] for hh in range(HEADS)], axis=0).T
        for n, (src, dn, g_ref) in enumerate(((q_ref, dqn, qg_ref), (k_ref, rows_of(dkt_ref), kg_ref))):
            xf = _f32(src)
            r = lax.rsqrt(_head_mean(xf * xf, avg_ref[...]) + EPS)
            y = xf * r
            dy = dn * g_ref[...]
            dqkv_ref[:, n * c:(n + 1) * c] = _mx(r * (dy - y * _head_mean(dy * y, avg_ref[...])))
            dg_ref[n] += _row_fold(dn * y)
        dqkv_ref[:, 2 * c:3 * c] = _mx(rows_of(dvt_ref))

    blk = lambda col: pl.BlockSpec((tm, c), lambda i: (i, col))
    vec = pl.BlockSpec((1, c), lambda i: (0, 0))
    blocked = pl.BlockSpec((HEADS, tm // t, HEAD_DIM, t), lambda i: (0, i, 0, 0))
    return pl.pallas_call(
        body,
        name=name,
        grid=(s // tm,),
        in_specs=[blk(col0), blk(col0 + 1), pl.BlockSpec((HEADS, HEAD_DIM, tm), lambda i: (0, 0, i)), blocked, blocked,
                  vec, vec, pl.BlockSpec((c, c), lambda i: (0, 0))],
        out_specs=[pl.BlockSpec((tm, 3 * c), lambda i: (i, 0)), pl.BlockSpec((2, 8, c), lambda i: (0, 0, 0))],
        out_shape=[jax.ShapeDtypeStruct((s, 3 * c), MXU_DTYPE), jax.ShapeDtypeStruct((2, 8, c), F32)],
        compiler_params=_params(),
    )(p, p, dqt, dkt, dvt, qg, kg, headavg)


def _place():
    x, y, c = lax.axis_index("x"), lax.axis_index("y"), lax.axis_index("c")
    chips = [(1 - x, y), (x, 1 - y), (1 - x, 1 - y)]
    return x, y, c, chips


def _shard_of(ref, axis, chip, width):
    idx = [slice(None)] * len(ref.shape)
    idx[axis] = pl.ds(chip * width, width)
    return ref.at[tuple(idx)]


def _place_shard(w, layer, axis, chip_idx, dtype, *, name):
    rows, cols = _as_rows(w.shape[1:])
    tr = _row_tile(rows, cols, 2)
    if axis == len(w.shape) - 2:
        out_shape = (1, rows, cols * N_CHIPS)
        out_spec = pl.BlockSpec((None, tr, cols), lambda i, j_ref: (0, i, j_ref[0]))
    else:
        assert axis == 0
        per = rows // tr
        out_shape = (1, rows * N_CHIPS, cols)
        out_spec = pl.BlockSpec((None, tr, cols), lambda i, j_ref: (0, j_ref[0] * per + i, 0))
    full = [1, *w.shape[1:]]
    full[1 + axis] *= N_CHIPS

    def body(j_ref, w_ref, o_ref):
        o_ref[...] = w_ref[...].astype(dtype)

    out = pl.pallas_call(
        body,
        name=name,
        grid_spec=pltpu.PrefetchScalarGridSpec(
            num_scalar_prefetch=1,
            grid=(rows // tr,),
            in_specs=[pl.BlockSpec((None, tr, cols), lambda i, j_ref: (layer, i, 0))],
            out_specs=out_spec,
        ),
        out_shape=jax.ShapeDtypeStruct(out_shape, dtype),
        compiler_params=_params(),
    )(chip_idx, w.reshape(w.shape[0], rows, cols))
    return out.reshape(full)


def _gather_riders(placed, axes, layer):
    n = len(placed)
    widths = [pa.shape[1 + ax] // N_CHIPS for pa, ax in zip(placed, axes)]

    def copies(refs, sems):
        x, y, c, chips = _place()

        def block(a, chip):
            return _shard_of(refs[a].at[0], axes[a], chip, widths[a])

        def over_ici(a, k, chip):
            return pltpu.make_async_remote_copy(
                src_ref=block(a, chip), dst_ref=block(a, chip), send_sem=sems[0].at[a, k], recv_sem=sems[1].at[a, k],
                device_id=(*chips[k], c), device_id_type=MESH)

        def to_sibling(a, k):
            cx, cy = chips[k]
            return pltpu.make_async_remote_copy(
                src_ref=block(a, 2 * cx + cy), dst_ref=block(a, 2 * cx + cy), send_sem=sems[0].at[a, k],
                recv_sem=sems[1].at[a, k], device_id=(x, y, 1 - c), device_id_type=MESH)

        return 2 * x + y, c, chips, over_ici, to_sibling

    def ici_start(refs, sems):
        me, c, _, over_ici, _ = copies(refs, sems)

        @pl.when(c == layer)
        def _():
            for a in range(n):
                for k in range(3):
                    over_ici(a, k, me).start()

    def ici_finish(refs, sems):
        me, c, chips, over_ici, _ = copies(refs, sems)

        @pl.when(c == layer)
        def _():
            for a in range(n):
                for k in range(3):
                    cx, cy = chips[k]
                    over_ici(a, k, 2 * cx + cy).wait_recv()
            for a in range(n):
                for k in range(3):
                    over_ici(a, k, me).wait_send()

    def d2d_start(refs, sems):
        _, c, _, _, to_sibling = copies(refs, sems)

        @pl.when(c == layer)
        def _():
            for a in range(n):
                for k in range(3):
                    to_sibling(a, k).start()

    def d2d_finish(refs, sems):
        _, c, _, _, to_sibling = copies(refs, sems)

        @pl.when(c == layer)
        def _():
            for a in range(n):
                for k in range(3):
                    to_sibling(a, k).wait_send()

        @pl.when(c != layer)
        def _():
            for a in range(n):
                for k in range(3):
                    to_sibling(a, k).wait_recv()

    def both_start(refs, sems):
        ici_start(refs, sems[:2])

    def both_finish(refs, sems):
        me, c, chips, over_ici, _ = copies(refs, sems[:2])
        to_sibling = copies(refs, sems[2:])[4]

        @pl.when(c == layer)
        def _():
            for a in range(n):
                for k in range(3):
                    cx, cy = chips[k]
                    over_ici(a, k, 2 * cx + cy).wait_recv()
                    to_sibling(a, k).start()
            for a in range(n):
                for k in range(3):
                    over_ici(a, k, me).wait_send()
                    to_sibling(a, k).wait_send()

        @pl.when(c != layer)
        def _():
            for a in range(n):
                for k in range(3):
                    to_sibling(a, k).wait_recv()

    arrays, sems = tuple(placed), ((n, 3), (n, 3))
    return (_Rider(arrays, sems, ici_start, ici_finish), _Rider(arrays, sems, d2d_start, d2d_finish),
            _Rider(arrays, sems * 2, both_start, both_finish))


def _join_riders(*riders):
    def each(which):
        def run(refs, sems):
            for r in riders:
                getattr(r, which)(refs[:len(r.arrays)], sems[:len(r.sems)])
                refs, sems = refs[len(r.arrays):], sems[len(r.sems):]
        return run

    return _Rider(sum((r.arrays for r in riders), ()), sum((r.sems for r in riders), ()), each("start"), each("finish"))


def _exchange_rider(grads, others, layer):
    n = len(grads)

    def copy(refs, sems, a):
        x, y, c, _ = _place()
        return pltpu.make_async_remote_copy(
            src_ref=refs[a], dst_ref=refs[n + a], send_sem=sems[0].at[a], recv_sem=sems[1].at[a],
            device_id=(x, y, 1 - c), device_id_type=MESH)

    def start(refs, sems):
        @pl.when(lax.axis_index("c") != layer)
        def _():
            for a in range(n):
                copy(refs, sems, a).start()

    def finish(refs, sems):
        @pl.when(lax.axis_index("c") != layer)
        def _():
            for a in range(n):
                copy(refs, sems, a).wait_send()

        @pl.when(lax.axis_index("c") == layer)
        def _():
            for a in range(n):
                copy(refs, sems, a).wait_recv()

    return _Rider((*grads, *others), ((n,), (n,)), start, finish)


def _scatter_rider(wires, landeds, axes, layer):
    n = len(wires)
    widths = [wr.shape[ax] // N_CHIPS for wr, ax in zip(wires, axes)]

    def copies(refs, sems):
        _, _, c, chips = _place()
        return [pltpu.make_async_remote_copy(
            src_ref=_shard_of(refs[a], axes[a], 2 * cx + cy, widths[a]), dst_ref=refs[n + a].at[k],
            send_sem=sems[0].at[a, k], recv_sem=sems[1].at[a, k], device_id=(cx, cy, c), device_id_type=MESH)
            for a in range(n) for k, (cx, cy) in enumerate(chips)]

    def start(refs, sems):
        @pl.when(lax.axis_index("c") == layer)
        def _():
            for cp in copies(refs, sems):
                cp.start()

    def finish(refs, sems):
        @pl.when(lax.axis_index("c") == layer)
        def _():
            for cp in copies(refs, sems):
                cp.wait()

    return _Rider((*wires, *landeds), ((n, 3), (n, 3)), start, finish)


def _share_rider(finals, layer):
    n = len(finals)

    def copy(refs, sems, a):
        x, y, c, _ = _place()
        return pltpu.make_async_remote_copy(
            src_ref=refs[a].at[layer], dst_ref=refs[a].at[layer], send_sem=sems[0].at[a], recv_sem=sems[1].at[a],
            device_id=(x, y, 1 - c), device_id_type=MESH)

    def start(refs, sems):
        @pl.when(lax.axis_index("c") == layer)
        def _():
            for a in range(n):
                copy(refs, sems, a).start()

    def finish(refs, sems):
        @pl.when(lax.axis_index("c") == layer)
        def _():
            for a in range(n):
                copy(refs, sems, a).wait_send()

        @pl.when(lax.axis_index("c") != layer)
        def _():
            for a in range(n):
                copy(refs, sems, a).wait_recv()

    return _Rider(tuple(finals), ((n,), (n,)), start, finish)


def _small_view(shape):
    size = math.prod(shape)
    return (size // 128, 128) if size % 1024 == 0 else (shape[0], size // shape[0])


def _all_reduce_small(parts, *, name):
    n = len(parts)
    views = [_small_view(a.shape) for a in parts]

    def body(*refs):
        ins, outs, slots = refs[:n], refs[n:2 * n], refs[2 * n:3 * n]
        send_sem, recv_sem = refs[3 * n:]
        x, y, c, _ = _place()
        me = 4 * x + 2 * y + c
        copies = []
        for a in range(n):
            slots[a][me] = ins[a][...]
            for k in range(1, N_DEV):
                peer = (x ^ (k >> 2), y ^ ((k >> 1) & 1), c ^ (k & 1))
                copies.append(pltpu.make_async_remote_copy(
                    src_ref=ins[a], dst_ref=slots[a].at[me], send_sem=send_sem.at[a, k - 1],
                    recv_sem=recv_sem.at[a, k - 1], device_id=peer, device_id_type=MESH))
        for cp in copies:
            cp.start()
        for a in range(n):
            for k in range(1, N_DEV):
                pltpu.make_async_remote_copy(
                    src_ref=ins[a], dst_ref=slots[a].at[me ^ k], send_sem=send_sem.at[a, k - 1],
                    recv_sem=recv_sem.at[a, k - 1], device_id=(x, y, c), device_id_type=MESH).wait()
        for a in range(n):
            total = slots[a][0]
            for dev in range(1, N_DEV):
                total = total + slots[a][dev]
            outs[a][...] = total

    vmem = pl.BlockSpec(memory_space=pltpu.VMEM)
    outs = pl.pallas_call(
        body,
        name=name,
        in_specs=[vmem] * n,
        out_specs=[vmem] * n,
        out_shape=[jax.ShapeDtypeStruct(view, F32) for view in views],
        scratch_shapes=[pltpu.VMEM((N_DEV, *view), F32) for view in views]
        + [pltpu.SemaphoreType.DMA((n, N_DEV - 1)), pltpu.SemaphoreType.DMA((n, N_DEV - 1))],
        compiler_params=_params(),
    )(*[a.reshape(view) for a, view in zip(parts, views)])
    return [o.reshape(a.shape) for o, a in zip(outs, parts)]


def _as_rows(shape):
    cols = shape[-1]
    return math.prod(shape[:-1]), cols


ELEMENTWISE_VMEM = 24 * 1024 * 1024


def _row_tile(rows, cols, n_arrays, sublanes=8):
    cap = ELEMENTWISE_VMEM // (n_arrays * 2 * 4 * cols)
    best = None
    for t in range(sublanes, min(rows, cap) + 1, sublanes):
        if rows % t == 0:
            best = t
    assert best is not None, (rows, cols)
    return best


def _pair_sum(g, other, active, *, name):
    rows, cols = _as_rows(g.shape)
    tr = _row_tile(rows, cols, 4, sublanes=16)

    def body(a_ref, g_ref, o_ref, out_ref, wire_ref):
        @pl.when(a_ref[0] == 1)
        def _():
            total = g_ref[...] + o_ref[...]
            out_ref[...] = total
            wire_ref[...] = total.astype(WIRE_DTYPE)

    blk = pl.BlockSpec((tr, cols), lambda i, a_ref: (i * a_ref[0], 0))
    out, wire = pl.pallas_call(
        body,
        name=name,
        grid_spec=pltpu.PrefetchScalarGridSpec(
            num_scalar_prefetch=1, grid=(rows // tr,), in_specs=[blk, blk], out_specs=[blk, blk]),
        out_shape=[jax.ShapeDtypeStruct((rows, cols), F32), jax.ShapeDtypeStruct((rows, cols), WIRE_DTYPE)],
        compiler_params=_params(),
    )(active, g.reshape(rows, cols), other.reshape(rows, cols))
    return out.reshape(g.shape), wire.reshape(g.shape)


def _chip_sum(mine, got, axis, chip_idx, active, layer, layers, into, *, name):
    shard_shape = got.shape[1:]
    rows, cols = _as_rows(shard_shape)
    tr = _row_tile(rows, cols, 5, sublanes=16)
    if axis == len(mine.shape) - 1:
        m2 = mine.reshape(rows, cols * N_CHIPS)
        mine_spec = pl.BlockSpec((tr, cols), lambda i, j_ref, a_ref: (i * a_ref[0], j_ref[0]))
    else:
        assert axis == 0
        m2 = mine.reshape(N_CHIPS, rows, cols)
        mine_spec = pl.BlockSpec((None, tr, cols), lambda i, j_ref, a_ref: (j_ref[0], i * a_ref[0], 0))

    def body(j_ref, a_ref, m_ref, got_ref, *refs):
        @pl.when(a_ref[0] == 1)
        def _():
            refs[-1][...] = ((m_ref[...] + got_ref[0].astype(F32)) + got_ref[1].astype(F32)) + got_ref[2].astype(F32)

    out = pl.pallas_call(
        body,
        name=name,
        grid_spec=pltpu.PrefetchScalarGridSpec(
            num_scalar_prefetch=2,
            grid=(rows // tr,),
            in_specs=[mine_spec, pl.BlockSpec((3, tr, cols), lambda i, j_ref, a_ref: (0, i * a_ref[0], 0))]
            + ([] if into is None else [ANY]),
            out_specs=pl.BlockSpec((None, tr, cols), lambda i, j_ref, a_ref: (layer, i * a_ref[0], 0)),
        ),
        out_shape=jax.ShapeDtypeStruct((layers, rows, cols), F32),
        input_output_aliases={} if into is None else {4: 0},
        compiler_params=_params(),
    )(chip_idx, active, m2, got.reshape(3, rows, cols), *([] if into is None else [into.reshape(layers, rows, cols)]))
    return out.reshape((layers, *shard_shape))


class _ReduceChain:
    def __init__(self, grads, layer, layers, chip_idx, core_idx, finals):
        self.grads, self.layer, self.layers, self.chip_idx, self.finals = list(grads), layer, layers, chip_idx, finals
        self.active = (core_idx == layer).astype(jnp.int32)
        self.axes = [BIG_AXIS[n] for n in BIG]
        self.pairs = None

    def exchange(self):
        return _exchange_rider(self.grads, [lax.empty(g.shape, F32) for g in self.grads], self.layer)

    def exchanged(self, arrived):
        n = len(self.grads)
        self.pairs = [_pair_sum(g, o, self.active, name=f"pair_sum_{name}_l{self.layer}")
                      for name, g, o in zip(BIG, arrived[:n], arrived[n:])]

    def scatter(self):
        landeds = []
        for (_, wire), ax in zip(self.pairs, self.axes):
            shard = list(wire.shape)
            shard[ax] //= N_CHIPS
            landeds.append(lax.empty((3, *shard), WIRE_DTYPE))
        return _scatter_rider([wire for _, wire in self.pairs], landeds, self.axes, self.layer)

    def scattered(self, arrived):
        n = len(self.grads)
        into = self.finals or [None] * n
        self.finals = [_chip_sum(pair, landed, ax, self.chip_idx, self.active, self.layer, self.layers, old,
                                 name=f"chip_sum_{name}_l{self.layer}")
                       for name, (pair, _), landed, ax, old in zip(BIG, self.pairs, arrived[n:], self.axes, into)]

    def share(self):
        return _share_rider(self.finals, self.layer)

    def shared(self, arrived):
        self.finals = list(arrived)

    def run_alone(self):
        tag = f"l{self.layer}"
        self.exchanged(_ride_alone([self.exchange()], name=f"grads_pair_exchange_{tag}"))
        self.scattered(_ride_alone([self.scatter()], name=f"grads_chip_scatter_{tag}"))
        self.shared(_ride_alone([self.share()], name=f"grads_pair_share_{tag}"))


def _adamw_update(w_ref, g_ref, m_ref, v_ref, d_ref, nm_ref, nv_ref):
    c1 = 1.0 / (1.0 - ADAM_B1 ** ADAM_STEP)
    c2 = 1.0 / (1.0 - ADAM_B2 ** ADAM_STEP)
    gg = g_ref[...]
    nm = ADAM_B1 * m_ref[...] + (1.0 - ADAM_B1) * gg
    nv = ADAM_B2 * v_ref[...] + (1.0 - ADAM_B2) * (gg * gg)
    nm_ref[...] = nm
    nv_ref[...] = nv
    d_ref[...] = -ADAM_LR * ((nm * c1) / (jnp.sqrt(nv * c2) + ADAM_EPS) + ADAM_WD * w_ref[...])


def _adamw_small(ws, gs, ms, vs, *, name):
    n = len(ws)
    views = [_small_view(a.shape) for a in ws]

    def body(*refs):
        w_r, g_r, m_r, v_r = (refs[k * n:(k + 1) * n] for k in range(4))
        d_r, nm_r, nv_r = (refs[(4 + k) * n:(5 + k) * n] for k in range(3))
        for a in range(n):
            _adamw_update(w_r[a], g_r[a], m_r[a], v_r[a], d_r[a], nm_r[a], nv_r[a])

    vmem = pl.BlockSpec(memory_space=pltpu.VMEM)
    flat = lambda arrs: [a.reshape(view) for a, view in zip(arrs, views)]
    outs = pl.pallas_call(
        body,
        name=name,
        in_specs=[vmem] * (4 * n),
        out_specs=[vmem] * (3 * n),
        out_shape=[jax.ShapeDtypeStruct(view, F32) for view in views] * 3,
        compiler_params=_params(),
    )(*flat(ws), *flat(gs), *flat(ms), *flat(vs))
    return [tuple(outs[k * n + a].reshape(ws[a].shape) for k in range(3)) for a in range(n)]


def _adamw(w, g, m, v, *, name):
    shape = w.shape
    rows, cols = _as_rows(shape)
    tr = _row_tile(rows, cols, 7)
    body = functools.partial(_adamw_update)

    blk = pl.BlockSpec((tr, cols), lambda i: (i, 0))
    flat = lambda a: a.reshape(rows, cols)
    outs = pl.pallas_call(
        body,
        name=name,
        grid=(rows // tr,),
        in_specs=[blk] * 4,
        out_specs=[blk] * 3,
        out_shape=[jax.ShapeDtypeStruct((rows, cols), F32)] * 3,
        compiler_params=_params(),
    )(flat(w), flat(g), flat(m), flat(v))
    return tuple(o.reshape(shape) for o in outs)


BIG = ("w_in", "w_branch_out", "w_o", "w_gate_up", "w_down")
BIG_AXIS = {"w_in": 1, "w_branch_out": 2, "w_o": 0, "w_gate_up": 1, "w_down": 0}
SMALL = ("mix_norm_g", "b_gate", "conv_w", "conv_b", "sgu_ln_g", "sgu_ln_b", "sgu_w", "sgu_b", "q_norm_g", "k_norm_g",
         "ffn_norm_g")
ORDER = ("mix_norm_g", "w_in", "b_gate", "conv_w", "conv_b", "sgu_ln_g", "sgu_ln_b", "sgu_w", "sgu_b", "q_norm_g",
         "k_norm_g", "w_branch_out", "w_o", "ffn_norm_g", "w_gate_up", "w_down")


def _layer_forward(x, w, wl, l, coming):
    n_in = wl["w_in"].shape[2]
    gate_col0 = (n_in - 3 * x.shape[1]) // x.shape[1]
    tag = f"l{l}"
    rider = None
    if coming:
        placed, axes, nxt = coming
        half = len(placed) // 2
        first, second = (placed[:half], axes[:half], nxt), (placed[half:], axes[half:], nxt)
        rider = _gather_riders(*first)[0]
    (p, h), arrived = _norm_matmul(x, w["mix_norm_g"][l][None], wl["w_in"], 0, name=f"in_proj_{tag}", tn=n_in // 7,
                                   rider=rider)
    ya = _conv_fwd(p, wl["conv_w"][0], w["conv_b"][l][None], name=f"conv_{tag}")
    yb = _sgu_fwd(p, w["sgu_ln_g"][l][None], w["sgu_ln_b"][l][None], w["wtril"][l], w["bias_full"][l], col0=3,
                  name=f"sgu_{tag}")
    q_c, k_cb, v_cb = _qkv_prep(p, w["qg"][l], w["kg"][l], w["headavg"], col0=5, name=f"qkv_{tag}")
    if coming:
        rider = _join_riders(_gather_riders(list(arrived), *first[1:])[1], _gather_riders(*second)[0])
    (out_t, runs), arrived = _att_fwd(k_cb, q_c, v_cb, w["later"], name=f"att_{tag}", rider=rider)
    x1, merged = _merge_fwd((ya, yb, out_t), p, w["b_gate"][l][None], wl["w_branch_out"], wl["w_o"], 0, x,
                            gate_col0=gate_col0, name=f"merge_{tag}")
    if coming:
        here = arrived[:half]
        rider = _gather_riders(list(arrived[half:]), *second[1:])[1]
    (gu, h2), arrived = _norm_matmul(x1, w["ffn_norm_g"][l][None], wl["w_gate_up"], 0, name=f"gate_up_{tag}",
                                     tn=wl["w_gate_up"].shape[2] // 4, rider=rider)
    if coming:
        arrived = (*here, *arrived)
    x2 = _ffn_down(gu, wl["w_down"], 0, x1, name=f"down_{tag}")
    saved = dict(x=x, p=p, h=h, ya=ya, yb=yb, yct=out_t, merged=merged, x1=x1, gu=gu, h2=h2,
                 att=(k_cb, v_cb, q_c, runs), gate_col0=gate_col0)
    return x2, saved, arrived


def _layer_backward(dx2, w, wl, sv, l, chain):
    tag = f"l{l}"
    c = w["conv_b"].shape[1]
    n_in = wl["w_in"].shape[2]
    n_ff = wl["w_gate_up"].shape[2]
    g, big = {}, {}
    (dgu, act, dx2b), arrived = _ffn_bwd(dx2, sv["gu"], wl["w_down"], 0, name=f"down_bwd_{tag}",
                                         rider=chain.exchange() if chain else None)
    if chain:
        chain.exchanged(arrived)
    big["w_down"] = _matmul_tn(act, dx2b, 0, 1, None, name=f"dw_down_{tag}", t1=act.shape[1] // 2)
    (dx1, dg2), _ = _matmul_nt_normbwd([dgu], wl["w_gate_up"], 0, sv["x1"], w["ffn_norm_g"][l][None], dx2,
                                       name=f"gate_up_bwd_{tag}")
    g["ffn_norm_g"] = jnp.sum(dg2, axis=0)
    big["w_gate_up"] = _matmul_tn(sv["h2"], dgu, 0, 1, None, name=f"dw_gate_up_{tag}", t1=512)
    ys = (sv["ya"], sv["yb"], sv["yct"])
    (dgates, dya, dyb, dyct, dd0, dd1, dd2, dx1b, dbg) = _merge_bwd(
        dx1, ys, sv["p"], w["b_gate"][l][None], wl["w_branch_out"], wl["w_o"], 0, gate_col0=sv["gate_col0"],
        name=f"merge_bwd_{tag}")
    g["b_gate"] = jnp.sum(dbg, axis=0)
    big["w_o"] = _matmul_tn(sv["merged"], dx1b, 0, 1, None, name=f"dw_o_{tag}")
    for i, (y, dd) in enumerate(zip(ys, (dd0, dd1, dd2))):
        heads_first = y.ndim == 3
        big["w_branch_out"] = _matmul_tn(y.reshape(c, -1) if heads_first else y, dd, i, len(ys), big.get("w_branch_out"),
                                         name=f"dw_bo{i}_{tag}", a_transposed=heads_first)
    dconv, dwc = _conv_bwd(sv["p"], dya, wl["conv_w"][0], w["conv_b"][l][None], name=f"conv_bwd_{tag}")
    dwc = jnp.sum(dwc, axis=1)
    g["conv_w"] = dwc[0:CONV_K]
    g["conv_b"] = dwc[CONV_K]
    dsgu, dln, dws, dbias = _sgu_bwd(sv["p"], dyb, w["sgu_ln_g"][l][None], w["sgu_ln_b"][l][None], w["wtril"][l],
                                     w["wtril_t"][l], w["bias_full"][l], col0=3, name=f"sgu_bwd_{tag}")
    dln = jnp.sum(dln, axis=1)
    g["sgu_ln_g"], g["sgu_ln_b"] = dln[0], dln[1]
    g["sgu_w"] = jnp.where(w["tril"], dws, 0.0)
    g["sgu_b"] = jnp.sum(dbias.reshape(CHUNK, c // CHUNK, CHUNK), axis=2).T
    k_cb, v_cb, q_c, runs = sv["att"]
    (dqt, dkt, dvt), arrived = _att_bwd(k_cb, v_cb, q_c, dyct, runs, w["later"], w["earlier"],
                                        name=f"att_bwd_{tag}", rider=chain.scatter() if chain else None)
    if chain:
        chain.scattered(arrived)
    dqkv, dqkg = _qkv_bwd(sv["p"], dqt, dkt, dvt, w["qg"][l], w["kg"][l], w["headavg"], col0=5, name=f"qkv_bwd_{tag}")
    dqkg = jnp.sum(dqkg.reshape(2, 8 * HEADS, HEAD_DIM), axis=1)
    g["q_norm_g"], g["k_norm_g"] = dqkg[0], dqkg[1]
    pieces = [dconv, dsgu, dqkv, dgates]
    (dx0, dg1), arrived = _matmul_nt_normbwd(pieces, wl["w_in"], 0, sv["x"], w["mix_norm_g"][l][None], dx1,
                                             name=f"in_proj_bwd_{tag}", rider=chain.share() if chain else None)
    if chain:
        chain.shared(arrived)
    g["mix_norm_g"] = jnp.sum(dg1, axis=0)
    col0 = 0
    for k, pc in enumerate(pieces):
        width = pc.shape[1]
        big["w_in"] = _matmul_tn(sv["h"], pc, 0, 1, big.get("w_in"), name=f"dw_in{k}_{tag}", col0=col0, n_total=n_in,
                                 tn=math.gcd(col0, width) if col0 else width)
        col0 += width
    return dx0, g, {n: a.reshape(wl[n].shape[1:]) for n, a in big.items()}


def kernel(x, mix_norm_g, w_in, b_gate, conv_w, conv_b, sgu_ln_g, sgu_ln_b, sgu_w, sgu_b, q_norm_g, k_norm_g, w_branch_out, w_o, ffn_norm_g, w_gate_up, w_down, loss_target, m_mix_norm_g, m_w_in, m_b_gate, m_conv_w, m_conv_b, m_sgu_ln_g, m_sgu_ln_b, m_sgu_w, m_sgu_b, m_q_norm_g, m_k_norm_g, m_w_branch_out, m_w_o, m_ffn_norm_g, m_w_gate_up, m_w_down, v_mix_norm_g, v_w_in, v_b_gate, v_conv_w, v_conv_b, v_sgu_ln_g, v_sgu_ln_b, v_sgu_w, v_sgu_b, v_q_norm_g, v_k_norm_g, v_w_branch_out, v_w_o, v_ffn_norm_g, v_w_gate_up, v_w_down):
    given = dict(locals())
    params = {n: given[n] for n in ORDER}
    moms = {n: (given["m_" + n], given["v_" + n]) for n in ORDER}
    layers = mix_norm_g.shape[0]
    assert layers == 2, "the exchanges split the work of a chip's two cores by layer"
    xs = x[0]
    target = loss_target[0]
    chip = 2 * lax.axis_index("x") + lax.axis_index("y")
    core = lax.axis_index("c")

    c_idx = core.reshape(1).astype(jnp.int32)
    j_idx = chip.reshape(1).astype(jnp.int32)
    width = conv_w.shape[2]
    gathered = BIG + ("conv_w",)
    axes = [BIG_AXIS[n] for n in BIG] + [1]

    def placed(layer):
        arrays = [_place_shard(params[n], layer, BIG_AXIS[n], j_idx, MXU_DTYPE, name=f"place_{n}_l{layer}") for n in BIG]
        mine = lax.dynamic_slice_in_dim(conv_w, layer, 1, axis=0)
        arrays.append(lax.dynamic_update_slice_in_dim(jnp.zeros((1, CONV_K, width * N_CHIPS), F32), mine, chip * width, 2))
        return arrays

    weights = [dict(zip(gathered, _ride_alone(_gather_riders(placed(0), axes, 0)[2:], name="gather_weights_l0"))), None]
    w = {n: params[n] for n in ("mix_norm_g", "b_gate", "conv_b", "sgu_ln_g", "sgu_ln_b", "ffn_norm_g")}
    tril = jnp.tril(jnp.ones((CHUNK, CHUNK), dtype=bool))
    w["tril"] = tril
    w["wtril"] = _mx(jnp.where(tril, sgu_w, 0.0))
    w["wtril_t"] = w["wtril"].transpose(0, 1, 3, 2)
    w["bias_full"] = jnp.repeat(sgu_b.transpose(0, 2, 1), CHUNK, axis=2)
    w["qg"] = jnp.tile(q_norm_g, (1, HEADS))[:, None, :]
    w["kg"] = jnp.tile(k_norm_g, (1, HEADS))[:, None, :]
    lane = jnp.arange(HEADS * HEAD_DIM) // HEAD_DIM
    w["headavg"] = _mx(jnp.where(lane[:, None] == lane[None, :], 1.0 / HEAD_DIM, 0.0))
    pos = jnp.arange(ATT_BLOCK)
    w["later"] = _mx(jnp.where(pos[None, :] > pos[:, None], 1.0, 0.0))
    w["earlier"] = _mx(jnp.where(pos[None, :] < pos[:, None], 1.0, 0.0))

    saved = []
    act = xs
    for l in range(layers):
        coming = (placed(l + 1), axes, l + 1) if l + 1 < layers else None
        act, sv, arrived = _layer_forward(act, w, weights[l], l, coming)
        saved.append(sv)
        if coming:
            weights[l + 1] = dict(zip(gathered, arrived))
    loss_part, dact = _loss_head(act, target, name="loss_head")
    loss = lax.psum(jnp.sum(loss_part), ("x", "y", "c"))
    grads = [None] * layers
    chain = None
    for l in reversed(range(layers)):
        dact, grads[l], big = _layer_backward(dact, w, weights[l], saved[l], l, chain)
        chain = _ReduceChain([big[n] for n in BIG], l, layers, j_idx, c_idx, chain.finals if chain else None)
    chain.run_alone()
    grad_x = dact[None]
    local = {n: jnp.stack([grads[l][n] for l in range(layers)]) for n in SMALL}
    full = dict(zip(BIG, chain.finals))

    summed = _all_reduce_small([local[n] for n in SMALL], name="grads_all_reduce_small")
    for n, gsum in zip(SMALL, summed):
        full[n] = gsum
    full["conv_w"] = lax.dynamic_slice_in_dim(full["conv_w"], chip * width, width, axis=2)

    out = {}
    for n in BIG:
        out[n] = _adamw(params[n], full[n], *moms[n], name=f"adamw_{n}")
    small = _adamw_small([params[n] for n in SMALL], [full[n] for n in SMALL], [moms[n][0] for n in SMALL],
                         [moms[n][1] for n in SMALL], name="adamw_small")
    for n, triple in zip(SMALL, small):
        out[n] = triple
    return (loss, grad_x, *[full[n] for n in ORDER], *[out[n][0] for n in ORDER], *[out[n][1] for n in ORDER],
            *[out[n][2] for n in ORDER])
```

```python
import functools
import math
from typing import Callable, NamedTuple

import jax
import jax.numpy as jnp
from jax import lax
from jax.experimental import pallas as pl
from jax.experimental.pallas import tpu as pltpu

F32 = jnp.float32
MXU_DTYPE = jnp.bfloat16
WIRE_DTYPE = jnp.bfloat16
ACT_DTYPE = jnp.bfloat16
HALO = 16

EPS = 1e-6
CONV_K = 3
CHUNK = 128
HEADS = 8
HEAD_DIM = 64
ATT_BLOCK = 128
EXP_UNDERFLOW = -88.0

ADAM_LR = 0.001
ADAM_B1 = 0.9
ADAM_B2 = 0.999
ADAM_EPS = 1e-08
ADAM_WD = 0.01
ADAM_STEP = 10

VMEM_LIMIT = 56 * 1024 * 1024
MESH = pl.DeviceIdType.MESH
N_CHIPS = 4
N_DEV = 8
ANY = pl.BlockSpec(memory_space=pl.ANY)


def _params(**kw):
    return pltpu.CompilerParams(vmem_limit_bytes=VMEM_LIMIT, **kw)


class _Rider(NamedTuple):
    arrays: tuple
    sems: tuple
    start: Callable
    finish: Callable


def _call(body, operands, *, name, grid, in_specs, out_specs, out_shape, scratch_shapes=(), rider=None):
    if rider is None:
        outs = pl.pallas_call(body, name=name, grid=grid, in_specs=in_specs, out_specs=out_specs, out_shape=out_shape,
                              scratch_shapes=list(scratch_shapes), compiler_params=_params())(*operands)
        return tuple(outs), ()
    n_in, n_out, n_scr, k = len(in_specs), len(out_specs), len(scratch_shapes), len(rider.arrays)

    def riding(*refs):
        ins, refs = refs[:n_in], refs[n_in + k:]
        outs, carried, refs = refs[:n_out], refs[n_out:n_out + k], refs[n_out + k:]
        scratch, sems = refs[:n_scr], refs[n_scr:]
        first = functools.reduce(jnp.logical_and, [pl.program_id(ax) == 0 for ax in range(len(grid))])
        last = functools.reduce(jnp.logical_and, [pl.program_id(ax) == grid[ax] - 1 for ax in range(len(grid))])

        @pl.when(first)
        def _():
            rider.start(carried, sems)

        body(*ins, *outs, *scratch)

        @pl.when(last)
        def _():
            rider.finish(carried, sems)

    res = pl.pallas_call(
        riding, name=name, grid=grid, in_specs=[*in_specs, *[ANY] * k], out_specs=[*out_specs, *[ANY] * k],
        out_shape=[*out_shape, *[jax.ShapeDtypeStruct(a.shape, a.dtype) for a in rider.arrays]],
        input_output_aliases={n_in + j: n_out + j for j in range(k)},
        scratch_shapes=[*scratch_shapes, *[pltpu.SemaphoreType.DMA(shape) for shape in rider.sems]],
        compiler_params=_params())(*operands, *rider.arrays)
    return tuple(res[:n_out]), tuple(res[n_out:])


def _ride_alone(riders, *, name):
    arrays = riders[0].arrays
    k = len(arrays)
    counts = [len(r.sems) for r in riders]

    def body(*refs):
        carried, sems = refs[k:2 * k], refs[2 * k:]
        for r, count in zip(riders, counts):
            r.start(carried, sems[:count])
            r.finish(carried, sems[:count])
            sems = sems[count:]

    return pl.pallas_call(
        body, name=name, in_specs=[ANY] * k, out_specs=[ANY] * k,
        out_shape=[jax.ShapeDtypeStruct(a.shape, a.dtype) for a in arrays],
        input_output_aliases={j: j for j in range(k)},
        scratch_shapes=[pltpu.SemaphoreType.DMA(shape) for r in riders for shape in r.sems])(*arrays)


def _mx(v):
    return v.astype(MXU_DTYPE)


def _dot(a, b):
    return lax.dot_general(a, b, (((1,), (0,)), ((), ())), preferred_element_type=F32)


def _dot_nt(a, b):
    return lax.dot_general(a, b, (((1,), (1,)), ((), ())), preferred_element_type=F32)


def _dot_tn(a, b):
    return lax.dot_general(a, b, (((0,), (0,)), ((), ())), preferred_element_type=F32)


def _dot_split_r(v, const):
    hi = _mx(v)
    lo = _mx(v - hi.astype(F32))
    return _dot(hi, const) + _dot(lo, const)


def _sigmoid(x):
    return 1.0 / (1.0 + jnp.exp(-x))


_INV_SQRT2 = 1.0 / math.sqrt(2.0)
_INV_SQRT2PI = 1.0 / math.sqrt(2.0 * math.pi)


def _gelu(x):
    return 0.5 * x * (1.0 + lax.erf(x * _INV_SQRT2))


def _gelu_grad(x):
    return 0.5 * (1.0 + lax.erf(x * _INV_SQRT2)) + x * jnp.exp(-0.5 * x * x) * _INV_SQRT2PI


def _row_fold(v):
    m, n = v.shape
    return jnp.sum(v.reshape(m // 8, 8, n), axis=0)


def _tile(m, pref):
    t = min(m, pref)
    while m % t:
        t //= 2
    return t


def _resident(block_shape, index_map):
    return pl.BlockSpec(block_shape, index_map, pipeline_mode=pl.Buffered(1))


def _norm_matmul(x, g, w, l, *, name, tm=512, tn=None, rider=None):
    s, d = x.shape
    n = w.shape[2]
    tm = _tile(s, tm)
    tn = tn or n
    assert n % tn == 0

    def body(x_ref, g_ref, w_ref, p_ref, h_ref):
        xf = x_ref[...]
        r = lax.rsqrt(jnp.mean(xf * xf, axis=-1, keepdims=True) + EPS)
        h = _mx(xf * r * g_ref[...])
        h_ref[...] = h
        for j in range(n // tn):
            cols = slice(j * tn, (j + 1) * tn)
            p_ref[:, cols] = _dot(h, w_ref[:, cols]).astype(ACT_DTYPE)

    return _call(
        body, (x, g, w),
        name=name,
        grid=(s // tm,),
        in_specs=[
            pl.BlockSpec((tm, d), lambda i: (i, 0)),
            pl.BlockSpec((1, d), lambda i: (0, 0)),
            _resident((None, d, n), lambda i: (l, 0, 0)),
        ],
        out_specs=[
            pl.BlockSpec((tm, n), lambda i: (i, 0)),
            pl.BlockSpec((tm, d), lambda i: (i, 0)),
        ],
        out_shape=[jax.ShapeDtypeStruct((s, n), ACT_DTYPE), jax.ShapeDtypeStruct((s, d), MXU_DTYPE)],
        rider=rider,
    )


def _conv_taps(u, prev):
    row = lax.broadcasted_iota(jnp.int32, u.shape, 0)
    last, before = prev[HALO - 1:HALO, :], prev[HALO - 2:HALO - 1, :]
    um1 = jnp.where(row == 0, last, pltpu.roll(u, 1, 0))
    um2 = pltpu.roll(u, 2, 0)
    um2 = jnp.where(row == 0, before, jnp.where(row == 1, last, um2))
    return um1, um2


def _f32(ref):
    return ref[...].astype(F32)


def _conv_fwd(p, conv_w, conv_b, *, name, tm=512):
    s = p.shape[0]
    c = conv_w.shape[1]
    tm = _tile(s, tm)
    hb = tm // HALO

    def body(bg_ref, cg_ref, xa_ref, cgp_ref, xap_ref, w_ref, b_ref, y_ref):
        first = pl.program_id(0) == 0
        u = _f32(cg_ref) * _f32(xa_ref)
        prev = jnp.where(first, 0.0, _f32(cgp_ref) * _f32(xap_ref))
        um1, um2 = _conv_taps(u, prev)
        w = w_ref[...]
        y = b_ref[...] + w[0:1, :] * um2 + w[1:2, :] * um1 + w[2:3, :] * u
        y_ref[...] = _mx(_f32(bg_ref) * y)

    halo = lambda col: pl.BlockSpec((HALO, c), lambda i: (jnp.maximum(i * hb - 1, 0), col))
    return pl.pallas_call(
        body,
        name=name,
        grid=(s // tm,),
        in_specs=[
            pl.BlockSpec((tm, c), lambda i: (i, 0)),
            pl.BlockSpec((tm, c), lambda i: (i, 1)),
            pl.BlockSpec((tm, c), lambda i: (i, 2)),
            halo(1),
            halo(2),
            pl.BlockSpec((CONV_K, c), lambda i: (0, 0)),
            pl.BlockSpec((1, c), lambda i: (0, 0)),
        ],
        out_specs=pl.BlockSpec((tm, c), lambda i: (i, 0)),
        out_shape=jax.ShapeDtypeStruct((s, c), MXU_DTYPE),
        compiler_params=_params(),
    )(p, p, p, p, p, conv_w, conv_b)


def _layernorm_stats(x):
    mu = jnp.mean(x, axis=-1, keepdims=True)
    xc = x - mu
    r = lax.rsqrt(jnp.mean(xc * xc, axis=-1, keepdims=True) + EPS)
    return xc * r, r


def _sgu_fwd(p, ln_g, ln_b, wtril, bias_full, *, col0, name, tm=512):
    s = p.shape[0]
    c = ln_g.shape[1]
    groups = c // CHUNK
    tm = _tile(s, tm)

    def body(u_ref, v_ref, g_ref, b_ref, w_ref, bias_ref, y_ref):
        vn, _ = _layernorm_stats(_gelu(_f32(v_ref)))
        vb = _mx(vn * g_ref[...] + b_ref[...])
        for n in range(tm // CHUNK):
            rows = slice(n * CHUNK, (n + 1) * CHUNK)
            for gi in range(groups):
                cols = slice(gi * CHUNK, (gi + 1) * CHUNK)
                mixed = _dot(w_ref[gi], vb[rows, cols]) + bias_ref[:, cols]
                y_ref[rows, cols] = _mx(_gelu(u_ref[rows, cols].astype(F32)) * mixed)

    return pl.pallas_call(
        body,
        name=name,
        grid=(s // tm,),
        in_specs=[
            pl.BlockSpec((tm, c), lambda i: (i, col0)),
            pl.BlockSpec((tm, c), lambda i: (i, col0 + 1)),
            pl.BlockSpec((1, c), lambda i: (0, 0)),
            pl.BlockSpec((1, c), lambda i: (0, 0)),
            pl.BlockSpec((groups, CHUNK, CHUNK), lambda i: (0, 0, 0)),
            pl.BlockSpec((CHUNK, c), lambda i: (0, 0)),
        ],
        out_specs=pl.BlockSpec((tm, c), lambda i: (i, 0)),
        out_shape=jax.ShapeDtypeStruct((s, c), MXU_DTYPE),
        compiler_params=_params(),
    )(p, p, ln_g, ln_b, wtril, bias_full)


def _head_mean(v, headavg):
    return _dot_split_r(v, headavg)


def _qkv_prep(p, qg, kg, headavg, *, col0, name, tm=512):
    s = p.shape[0]
    c = qg.shape[1]
    t = ATT_BLOCK
    tm = _tile(s, tm)
    assert tm % t == 0

    def body(q_ref, k_ref, v_ref, qg_ref, kg_ref, avg_ref, qc_ref, kcb_ref, vcb_ref):
        normed = []
        for src, g_ref in ((q_ref, qg_ref), (k_ref, kg_ref)):
            xf = _f32(src)
            r = lax.rsqrt(_head_mean(xf * xf, avg_ref[...]) + EPS)
            normed.append(xf * r * g_ref[...])
        q_t, k_t, v_t = (_mx(a.T) for a in (*normed, _f32(v_ref)))
        for hh in range(HEADS):
            rows = slice(hh * HEAD_DIM, (hh + 1) * HEAD_DIM)
            qc_ref[hh] = q_t[rows, :]
            for b in range(tm // t):
                kcb_ref[hh, b] = k_t[rows, b * t:(b + 1) * t]
                vcb_ref[hh, b] = v_t[rows, b * t:(b + 1) * t]

    blk = lambda col: pl.BlockSpec((tm, c), lambda i: (i, col))
    vec = pl.BlockSpec((1, c), lambda i: (0, 0))
    blocked = pl.BlockSpec((HEADS, tm // t, HEAD_DIM, t), lambda i: (0, i, 0, 0))
    return pl.pallas_call(
        body,
        name=name,
        grid=(s // tm,),
        in_specs=[blk(col0), blk(col0 + 1), blk(col0 + 2), vec, vec, pl.BlockSpec((c, c), lambda i: (0, 0))],
        out_specs=[pl.BlockSpec((HEADS, HEAD_DIM, tm), lambda i: (0, 0, i)), blocked, blocked],
        out_shape=[jax.ShapeDtypeStruct((HEADS, HEAD_DIM, s), MXU_DTYPE)]
        + [jax.ShapeDtypeStruct((HEADS, s // t, HEAD_DIM, t), MXU_DTYPE)] * 2,
        compiler_params=_params(),
    )(p, p, p, qg, kg, headavg)


def _att_mask(t, key0, qry0):
    kpos = key0 + lax.broadcasted_iota(jnp.int32, (t, t), 0)
    qpos = qry0 + lax.broadcasted_iota(jnp.int32, (t, t), 1)
    return kpos < qpos


def _att_blocks(k_blks, q_ts, scale, mask):
    zs = [(_dot_tn(k, q) if k.shape[0] == q.shape[0] else _dot(k, q)) * scale for k, q in zip(k_blks, q_ts)]
    es = [jnp.exp(-jnp.abs(z)) for z in zs]
    lbs = [jnp.minimum(z, 0.0) - jnp.log1p(e) for z, e in zip(zs, es)]
    l1ms = [jnp.where(mask, lb - z, 0.0) for lb, z in zip(lbs, zs)]
    return zs, es, lbs, l1ms


def _dot_split_each(const, vs):
    his = [_mx(v) for v in vs]
    los = [_mx(v - hi.astype(F32)) for v, hi in zip(vs, his)]
    tops = [_dot(const, hi) for hi in his]
    return [top + _dot(const, lo) for top, lo in zip(tops, los)]


def _max_over(runs):
    m = runs[0]
    for r in runs[1:]:
        m = jnp.maximum(m, r)
    return jnp.max(m)


NOT_SEEN = -1e30


def _row_put(ref, g, j, row):
    j8 = pl.multiple_of((j // 8) * 8, 8)
    blk = ref[g, pl.ds(j8, 8), :]
    sub = lax.broadcasted_iota(jnp.int32, blk.shape, 0)
    ref[g, pl.ds(j8, 8), :] = jnp.where(sub == j - j8, row, blk)


def _row_get(ref, g, j):
    j8 = pl.multiple_of((j // 8) * 8, 8)
    blk = ref[g, pl.ds(j8, 8), :]
    sub = lax.broadcasted_iota(jnp.int32, blk.shape, 0)
    return jnp.sum(jnp.where(sub == j - j8, blk, 0.0), axis=0, keepdims=True)


def _att_fwd(k_hm, qt_hm, vt_hm, later, *, name, hg=8, rider=None):
    h, nb, d, t = k_hm.shape
    s = nb * t
    nbp = -(-nb // 8) * 8
    scale = 1.0 / math.sqrt(d)
    assert h % hg == 0

    def body(k_ref, qt_ref, vt_ref, later_ref, o_ref, runs_ref):
        i = pl.program_id(1)
        q_ts = [qt_ref[g] for g in range(hg)]
        runs_ref[...] = jnp.full(runs_ref.shape, NOT_SEEN, F32)

        def cond(carry):
            j, _, _, rmax = carry
            return jnp.logical_and(j >= 0, rmax > EXP_UNDERFLOW)

        def step(carry):
            j, runs, accs, _ = carry
            mask = _att_mask(t, j * t, i * t)
            heads = range(hg)
            for g in heads:
                _row_put(runs_ref, g, j, runs[g])
            _, _, lbs, l1ms = _att_blocks([k_ref[g, j] for g in heads], q_ts, scale, mask)
            afters = _dot_split_each(later_ref[...], l1ms)
            weights = [_mx(jnp.where(mask, jnp.exp(lbs[g] + afters[g] + runs[g]), 0.0)) for g in heads]
            new_accs = [accs[g] + _dot(vt_ref[g, j], weights[g]) for g in heads]
            new_runs = [runs[g] + jnp.sum(l1ms[g], axis=0, keepdims=True) for g in heads]
            return j - 1, tuple(new_runs), tuple(new_accs), _max_over(new_runs)

        init = (i, tuple(jnp.zeros((1, t), F32) for _ in range(hg)), tuple(jnp.zeros((d, t), F32) for _ in range(hg)),
                jnp.float32(0.0))
        _, _, accs, _ = lax.while_loop(cond, step, init)
        for g in range(hg):
            o_ref[g] = _mx(accs[g])

    return _call(
        body, (k_hm, qt_hm, vt_hm, later),
        name=name,
        grid=(h // hg, nb),
        in_specs=[
            pl.BlockSpec((hg, nb, d, t), lambda hh, i: (hh, 0, 0, 0)),
            pl.BlockSpec((hg, d, t), lambda hh, i: (hh, 0, i)),
            pl.BlockSpec((hg, nb, d, t), lambda hh, i: (hh, 0, 0, 0)),
            pl.BlockSpec((t, t), lambda hh, i: (0, 0)),
        ],
        out_specs=[pl.BlockSpec((hg, d, t), lambda hh, i: (hh, 0, i)),
                   pl.BlockSpec((hg, None, nbp, t), lambda hh, i: (hh, i, 0, 0))],
        out_shape=[jax.ShapeDtypeStruct((h, d, s), MXU_DTYPE), jax.ShapeDtypeStruct((h, nb, nbp, t), F32)],
        rider=rider,
    )


def _heads_to_rows(ref):
    return _mx(jnp.concatenate([ref[hh].astype(F32) for hh in range(HEADS)], axis=0).T)


def _merge_fwd(ys, p, b_gate, w_bo, w_o, l, x, *, gate_col0, name, tm=512):
    s, d = x.shape
    _, nbr, c, _ = w_bo.shape
    tm = _tile(s, tm)

    def body(ya_ref, yb_ref, yct_ref, g0_ref, g1_ref, g2_ref, bg_ref, wbo_ref, wo_ref, x_ref, x1_ref, m_ref):
        merged = jnp.zeros((tm, d), F32)
        branches = ((ya_ref[...], g0_ref), (yb_ref[...], g1_ref), (_heads_to_rows(yct_ref), g2_ref))
        for i, (y, g_ref) in enumerate(branches):
            gate = _sigmoid(_f32(g_ref) + bg_ref[:, i * d:(i + 1) * d])
            merged = merged + gate * _dot(y, wbo_ref[i])
        mb = _mx(merged)
        m_ref[...] = mb
        x1_ref[...] = x_ref[...] + _dot(mb, wo_ref[...])

    yblk = pl.BlockSpec((tm, c), lambda i: (i, 0))
    tblk = pl.BlockSpec((HEADS, HEAD_DIM, tm), lambda i: (0, 0, i))
    gblk = lambda k: pl.BlockSpec((tm, d), lambda i: (i, gate_col0 + k))
    xblk = pl.BlockSpec((tm, d), lambda i: (i, 0))
    return pl.pallas_call(
        body,
        name=name,
        grid=(s // tm,),
        in_specs=[
            yblk, yblk, tblk, gblk(0), gblk(1), gblk(2),
            pl.BlockSpec((1, nbr * d), lambda i: (0, 0)),
            _resident((None, nbr, c, d), lambda i: (l, 0, 0, 0)),
            _resident((None, d, d), lambda i: (l, 0, 0)),
            xblk,
        ],
        out_specs=[xblk, xblk],
        out_shape=[jax.ShapeDtypeStruct((s, d), F32), jax.ShapeDtypeStruct((s, d), MXU_DTYPE)],
        compiler_params=_params(),
    )(*ys, p, p, p, b_gate, w_bo, w_o, x)


def _ffn_down(gu, w_down, l, x, *, name, tm=512):
    s, d = x.shape
    f = w_down.shape[1]
    tm = _tile(s, tm)

    def body(g_ref, u_ref, w_ref, x_ref, o_ref):
        g = _f32(g_ref)
        act = _mx(g * _sigmoid(g) * _f32(u_ref))
        o_ref[...] = x_ref[...] + _dot(act, w_ref[...])

    return pl.pallas_call(
        body,
        name=name,
        grid=(s // tm,),
        in_specs=[
            pl.BlockSpec((tm, f), lambda i: (i, 0)),
            pl.BlockSpec((tm, f), lambda i: (i, 1)),
            _resident((None, f, d), lambda i: (l, 0, 0)),
            pl.BlockSpec((tm, d), lambda i: (i, 0)),
        ],
        out_specs=pl.BlockSpec((tm, d), lambda i: (i, 0)),
        out_shape=jax.ShapeDtypeStruct((s, d), F32),
        compiler_params=_params(),
    )(gu, gu, w_down, x)


def _loss_head(y, target, *, name, tm=512):
    s, d = y.shape
    tm = _tile(s, tm)

    def body(y_ref, t_ref, l_ref, dy_ref):
        @pl.when(pl.program_id(0) == 0)
        def _():
            l_ref[...] = jnp.zeros_like(l_ref)

        err = y_ref[...] - t_ref[...]
        dy_ref[...] = err * (1.0 / d)
        sq = _row_fold(err * err)
        part = sq[:, 0:128]
        for k in range(1, d // 128):
            part = part + sq[:, k * 128:(k + 1) * 128]
        l_ref[...] += part * (0.5 / d)

    blk = pl.BlockSpec((tm, d), lambda i: (i, 0))
    return pl.pallas_call(
        body,
        name=name,
        grid=(s // tm,),
        in_specs=[blk, blk],
        out_specs=[pl.BlockSpec((8, 128), lambda i: (0, 0)), blk],
        out_shape=[jax.ShapeDtypeStruct((8, 128), F32), jax.ShapeDtypeStruct((s, d), F32)],
        compiler_params=_params(),
    )(y, target)


def _matmul_tn(a, b, slot, n_slots, into, *, name, t1=1024, tn=None, tm=1024, col0=0, n_total=None, a_transposed=False):
    m, k1 = a.shape[::-1] if a_transposed else a.shape
    n = b.shape[1]
    n_total = n_total or n
    t1 = _tile(k1, t1)
    tn = tn or n
    tm = _tile(m, tm)
    steps = m // tm
    assert n % tn == 0 and col0 % tn == 0
    cb0 = col0 // tn

    def body(a_ref, b_ref, *refs):
        o_ref = refs[-1]

        @pl.when(pl.program_id(2) == 0)
        def _():
            o_ref[...] = jnp.zeros_like(o_ref)

        o_ref[...] += (_dot if a_transposed else _dot_tn)(a_ref[...], b_ref[...])

    return pl.pallas_call(
        body,
        name=name,
        grid=(k1 // t1, n // tn, steps),
        in_specs=[
            pl.BlockSpec((t1, tm), lambda i, j, k: (i, k)) if a_transposed else pl.BlockSpec((tm, t1), lambda i, j, k: (k, i)),
            pl.BlockSpec((tm, tn), lambda i, j, k: (k, j)),
        ] + ([] if into is None else [ANY]),
        out_specs=pl.BlockSpec((None, t1, tn), lambda i, j, k: (slot, i, cb0 + j)),
        out_shape=jax.ShapeDtypeStruct((n_slots, k1, n_total), F32),
        input_output_aliases={} if into is None else {2: 0},
        compiler_params=_params(),
    )(a, b, *([] if into is None else [into]))


def _matmul_nt_normbwd(pieces, w, l, x, g, dres, *, name, tm=512, tk=1024, rider=None):
    s, d = x.shape
    k = w.shape[2]
    tm = _tile(s, tm)
    n = len(pieces)
    assert sum(pc.shape[1] for pc in pieces) == k

    def body(*refs):
        piece_refs = refs[:n]
        w_ref, x_ref, g_ref, dres_ref, dx_ref, dg_ref = refs[n:]

        @pl.when(pl.program_id(0) == 0)
        def _():
            dg_ref[...] = jnp.zeros_like(dg_ref)

        dh = jnp.zeros((tm, d), F32)
        off = 0
        for pc, pc_ref in zip(pieces, piece_refs):
            width = pc.shape[1]
            step = _tile(width, tk)
            for c0 in range(0, width, step):
                dh = dh + _dot_nt(pc_ref[:, c0:c0 + step], w_ref[:, off + c0:off + c0 + step])
            off += width
        xf = x_ref[...]
        r = lax.rsqrt(jnp.mean(xf * xf, axis=-1, keepdims=True) + EPS)
        y = xf * r
        dy = dh * g_ref[...]
        dx_ref[...] = dres_ref[...] + r * (dy - y * jnp.mean(dy * y, axis=-1, keepdims=True))
        dg_ref[...] += _row_fold(dh * y)

    xblk = pl.BlockSpec((tm, d), lambda i: (i, 0))
    return _call(
        body, (*pieces, w, x, g, dres),
        name=name,
        grid=(s // tm,),
        in_specs=[pl.BlockSpec((tm, pc.shape[1]), lambda i: (i, 0)) for pc in pieces] + [
            _resident((None, d, k), lambda i: (l, 0, 0)),
            xblk,
            pl.BlockSpec((1, d), lambda i: (0, 0)),
            xblk,
        ],
        out_specs=[xblk, pl.BlockSpec((8, d), lambda i: (0, 0))],
        out_shape=[jax.ShapeDtypeStruct((s, d), F32), jax.ShapeDtypeStruct((8, d), F32)],
        rider=rider,
    )


def _ffn_bwd(dx, gu, w_down, l, *, name, tm=512, rider=None):
    s, d = dx.shape
    f = w_down.shape[1]
    tm = _tile(s, tm)

    def body(dx_ref, g_ref, u_ref, w_ref, dgu_ref, act_ref, dxb_ref):
        dxb = _mx(dx_ref[...])
        dxb_ref[...] = dxb
        dact = _dot_nt(dxb, w_ref[...])
        g = _f32(g_ref)
        u = _f32(u_ref)
        sg = _sigmoid(g)
        silu = g * sg
        act_ref[...] = _mx(silu * u)
        dgu_ref[:, 0:f] = _mx(dact * u * (sg * (1.0 + g * (1.0 - sg))))
        dgu_ref[:, f:2 * f] = _mx(dact * silu)

    fblk = lambda col: pl.BlockSpec((tm, f), lambda i: (i, col))
    dblk = pl.BlockSpec((tm, d), lambda i: (i, 0))
    return _call(
        body, (dx, gu, gu, w_down),
        name=name,
        grid=(s // tm,),
        in_specs=[dblk, fblk(0), fblk(1), _resident((None, f, d), lambda i: (l, 0, 0))],
        out_specs=[pl.BlockSpec((tm, 2 * f), lambda i: (i, 0)), fblk(0), dblk],
        out_shape=[
            jax.ShapeDtypeStruct((s, 2 * f), MXU_DTYPE),
            jax.ShapeDtypeStruct((s, f), MXU_DTYPE),
            jax.ShapeDtypeStruct((s, d), MXU_DTYPE),
        ],
        rider=rider,
    )


def _merge_bwd(dx, ys, p, b_gate, w_bo, w_o, l, *, gate_col0, name, tm=512):
    s, d = dx.shape
    _, nbr, c, _ = w_bo.shape
    tm = _tile(s, tm)

    def body(dx_ref, ya_ref, yb_ref, yct_ref, g0_ref, g1_ref, g2_ref, bg_ref, wbo_ref, wo_ref,
             dgate_ref, dya_ref, dyb_ref, dyct_ref, dd0_ref, dd1_ref, dd2_ref, dxb_ref, dbg_ref):
        @pl.when(pl.program_id(0) == 0)
        def _():
            dbg_ref[...] = jnp.zeros_like(dbg_ref)

        dxb = _mx(dx_ref[...])
        dxb_ref[...] = dxb
        dmerged = _dot_nt(dxb, wo_ref[...])
        branches = ((ya_ref[...], g0_ref, dd0_ref), (yb_ref[...], g1_ref, dd1_ref), (_heads_to_rows(yct_ref), g2_ref, dd2_ref))
        dys = []
        for i, (y, g_ref, dd_ref) in enumerate(branches):
            cols = slice(i * d, (i + 1) * d)
            gate = _sigmoid(_f32(g_ref) + bg_ref[:, cols])
            yd = _dot(y, wbo_ref[i])
            dyd = _mx(dmerged * gate)
            dd_ref[...] = dyd
            dys.append(_dot_nt(dyd, wbo_ref[i]))
            dpre = dmerged * yd * gate * (1.0 - gate)
            dgate_ref[:, cols] = _mx(dpre)
            dbg_ref[:, cols] += _row_fold(dpre)
        dya_ref[...] = dys[0]
        dyb_ref[...] = dys[1]
        dyc_t = _mx(dys[2].T)
        for hh in range(HEADS):
            dyct_ref[hh] = dyc_t[hh * HEAD_DIM:(hh + 1) * HEAD_DIM, :]

    yblk = pl.BlockSpec((tm, c), lambda i: (i, 0))
    tblk = pl.BlockSpec((HEADS, HEAD_DIM, tm), lambda i: (0, 0, i))
    gblk = lambda k: pl.BlockSpec((tm, d), lambda i: (i, gate_col0 + k))
    dblk = pl.BlockSpec((tm, d), lambda i: (i, 0))
    return pl.pallas_call(
        body,
        name=name,
        grid=(s // tm,),
        in_specs=[
            dblk, yblk, yblk, tblk, gblk(0), gblk(1), gblk(2),
            pl.BlockSpec((1, nbr * d), lambda i: (0, 0)),
            _resident((None, nbr, c, d), lambda i: (l, 0, 0, 0)),
            _resident((None, d, d), lambda i: (l, 0, 0)),
        ],
        out_specs=[pl.BlockSpec((tm, nbr * d), lambda i: (i, 0)), yblk, yblk, tblk, dblk, dblk, dblk, dblk,
                   pl.BlockSpec((8, nbr * d), lambda i: (0, 0))],
        out_shape=[jax.ShapeDtypeStruct((s, nbr * d), MXU_DTYPE)] + [jax.ShapeDtypeStruct((s, c), F32)] * 2
        + [jax.ShapeDtypeStruct((HEADS, HEAD_DIM, s), MXU_DTYPE)]
        + [jax.ShapeDtypeStruct((s, d), MXU_DTYPE)] * 4 + [jax.ShapeDtypeStruct((8, nbr * d), F32)],
        compiler_params=_params(),
    )(dx, *ys, p, p, p, b_gate, w_bo, w_o)


def _conv_bwd(p, dya, conv_w, conv_b, *, name, tm=512):
    s = p.shape[0]
    c = conv_w.shape[1]
    tm = _tile(s, tm)
    hb = tm // HALO
    last = s // tm - 1

    def body(bg_ref, cg_ref, xa_ref, cgp_ref, xap_ref, dy_ref, dyn_ref, bgn_ref, w_ref, b_ref,
             dp_ref, dw_ref):
        i = pl.program_id(0)

        @pl.when(i == 0)
        def _():
            dw_ref[...] = jnp.zeros_like(dw_ref)

        cg = _f32(cg_ref)
        xa = _f32(xa_ref)
        u = cg * xa
        prev = jnp.where(i == 0, 0.0, _f32(cgp_ref) * _f32(xap_ref))
        um1, um2 = _conv_taps(u, prev)
        w = w_ref[...]
        y = b_ref[...] + w[0:1, :] * um2 + w[1:2, :] * um1 + w[2:3, :] * u
        dya = dy_ref[...]
        dp_ref[:, 0:c] = _mx(dya * y)
        dyv = dya * _f32(bg_ref)
        nxt = jnp.where(i == last, 0.0, dyn_ref[...] * _f32(bgn_ref))
        row = lax.broadcasted_iota(jnp.int32, dyv.shape, 0)
        dp1 = jnp.where(row == tm - 1, nxt[0:1, :], pltpu.roll(dyv, tm - 1, 0))
        dp2 = pltpu.roll(dyv, tm - 2, 0)
        dp2 = jnp.where(row == tm - 2, nxt[0:1, :], jnp.where(row == tm - 1, nxt[1:2, :], dp2))
        du = w[2:3, :] * dyv + w[1:2, :] * dp1 + w[0:1, :] * dp2
        dp_ref[:, c:2 * c] = _mx(du * xa)
        dp_ref[:, 2 * c:3 * c] = _mx(du * cg)
        dw_ref[0] += _row_fold(dyv * um2)
        dw_ref[1] += _row_fold(dyv * um1)
        dw_ref[2] += _row_fold(dyv * u)
        dw_ref[3] += _row_fold(dyv)

    blk = lambda col: pl.BlockSpec((tm, c), lambda i: (i, col))
    halo = lambda col: pl.BlockSpec((HALO, c), lambda i: (jnp.maximum(i * hb - 1, 0), col))
    nhalo = lambda col: pl.BlockSpec((HALO, c), lambda i: (jnp.minimum((i + 1) * hb, s // HALO - 1), col))
    return pl.pallas_call(
        body,
        name=name,
        grid=(s // tm,),
        in_specs=[blk(0), blk(1), blk(2), halo(1), halo(2), blk(0), nhalo(0), nhalo(0),
                  pl.BlockSpec((CONV_K, c), lambda i: (0, 0)), pl.BlockSpec((1, c), lambda i: (0, 0))],
        out_specs=[pl.BlockSpec((tm, 3 * c), lambda i: (i, 0)), pl.BlockSpec((4, 8, c), lambda i: (0, 0, 0))],
        out_shape=[jax.ShapeDtypeStruct((s, 3 * c), MXU_DTYPE), jax.ShapeDtypeStruct((4, 8, c), F32)],
        compiler_params=_params(),
    )(p, p, p, p, p, dya, dya, p, conv_w, conv_b)


def _sgu_bwd(p, dyb, ln_g, ln_b, wtril, wtril_t, bias_full, *, col0, name, tm=512):
    s = p.shape[0]
    c = ln_g.shape[1]
    groups = c // CHUNK
    tm = _tile(s, tm)

    def body(u_ref, v_ref, dy_ref, g_ref, b_ref, w_ref, wt_ref, bias_ref, duv_ref, dln_ref, dw_ref, dbias_ref, dvn_ref):
        @pl.when(pl.program_id(0) == 0)
        def _():
            dln_ref[...] = jnp.zeros_like(dln_ref)
            dw_ref[...] = jnp.zeros_like(dw_ref)
            dbias_ref[...] = jnp.zeros_like(dbias_ref)

        sv = _f32(v_ref)
        xhat, r = _layernorm_stats(_gelu(sv))
        vb = _mx(xhat * g_ref[...] + b_ref[...])
        for n in range(tm // CHUNK):
            rows = slice(n * CHUNK, (n + 1) * CHUNK)
            for gi in range(groups):
                cols = slice(gi * CHUNK, (gi + 1) * CHUNK)
                su = u_ref[rows, cols].astype(F32)
                dy = dy_ref[rows, cols]
                vblk = vb[rows, cols]
                mixed = _dot(w_ref[gi], vblk) + bias_ref[:, cols]
                duv_ref[rows, cols] = _mx(dy * mixed * _gelu_grad(su))
                dmixed = dy * _gelu(su)
                dmb = _mx(dmixed)
                dvn_ref[rows, cols] = _dot(wt_ref[gi], dmb)
                dw_ref[gi] += _dot_nt(dmb, vblk)
                dbias_ref[:, cols] += dmixed
        dvn = dvn_ref[...]
        dln_ref[0] += _row_fold(dvn * xhat)
        dln_ref[1] += _row_fold(dvn)
        dxh = dvn * g_ref[...]
        dgv = r * (dxh - jnp.mean(dxh, axis=-1, keepdims=True) - xhat * jnp.mean(dxh * xhat, axis=-1, keepdims=True))
        duv_ref[:, c:2 * c] = _mx(dgv * _gelu_grad(sv))

    blk = lambda col: pl.BlockSpec((tm, c), lambda i: (i, col))
    vec = pl.BlockSpec((1, c), lambda i: (0, 0))
    wspec = pl.BlockSpec((groups, CHUNK, CHUNK), lambda i: (0, 0, 0))
    return pl.pallas_call(
        body,
        name=name,
        grid=(s // tm,),
        in_specs=[blk(col0), blk(col0 + 1), blk(0), vec, vec, wspec, wspec, pl.BlockSpec((CHUNK, c), lambda i: (0, 0))],
        out_specs=[pl.BlockSpec((tm, 2 * c), lambda i: (i, 0)), pl.BlockSpec((2, 8, c), lambda i: (0, 0, 0)), wspec,
                   pl.BlockSpec((CHUNK, c), lambda i: (0, 0))],
        out_shape=[jax.ShapeDtypeStruct((s, 2 * c), MXU_DTYPE),
                   jax.ShapeDtypeStruct((2, 8, c), F32), jax.ShapeDtypeStruct((groups, CHUNK, CHUNK), F32),
                   jax.ShapeDtypeStruct((CHUNK, c), F32)],
        scratch_shapes=[pltpu.VMEM((tm, c), F32)],
        compiler_params=_params(),
    )(p, p, dyb, ln_g, ln_b, wtril, wtril_t, bias_full)


def _att_bwd(kt_hm, vt_hm, qt_hm, dot_hm, runs, later, earlier, *, name, hg=4, rider=None):
    h, nb, d, t = kt_hm.shape
    s = nb * t
    nbp = runs.shape[2]
    scale = 1.0 / math.sqrt(d)
    assert h % hg == 0

    def body(kt_ref, vt_ref, qt_ref, dot_ref, runs_ref, later_ref, earlier_ref, dqt_ref, dkt_ref, dvt_ref):
        i = pl.program_id(1)

        @pl.when(i == 0)
        def _():
            dkt_ref[...] = jnp.zeros_like(dkt_ref)
            dvt_ref[...] = jnp.zeros_like(dvt_ref)

        q_ts = [qt_ref[g] for g in range(hg)]
        do_ts = [dot_ref[g] for g in range(hg)]

        best = runs_ref[0]
        for g in range(1, hg):
            best = jnp.maximum(best, runs_ref[g])
        row = lax.broadcasted_iota(jnp.int32, (nbp, 1), 0)
        counts = jnp.logical_and(jnp.max(best, axis=1, keepdims=True) > EXP_UNDERFLOW, row < i)
        seen = jnp.sum(counts.astype(jnp.int32))
        zeros_row = tuple(jnp.zeros((1, t), F32) for _ in range(hg))

        def up(j, carry):
            gsums, dqts = carry
            mask = _att_mask(t, j * t, i * t)
            heads = range(hg)
            zs, es, lbs, l1ms = _att_blocks([kt_ref[g, j] for g in heads], q_ts, scale, mask)
            afters = _dot_split_each(later_ref[...], l1ms)
            das = [_dot_tn(vt_ref[g, j], do_ts[g]) for g in heads]
            weights = [jnp.where(mask, jnp.exp(lbs[g] + afters[g] + _row_get(runs_ref, g, j)), 0.0) for g in heads]
            grs = [das[g] * weights[g] for g in heads]
            gbefores = _dot_split_each(earlier_ref[...], grs)
            dzs = []
            for g in heads:
                inv = 1.0 / (1.0 + es[g])
                pos = zs[g] >= 0.0
                beta = jnp.where(pos, inv, es[g] * inv)
                omb = jnp.where(pos, es[g] * inv, inv)
                dzs.append(_mx(jnp.where(mask, grs[g] * omb - (gbefores[g] + gsums[g]) * beta, 0.0) * scale))
            new_dqts = [dqts[g] + _dot(kt_ref[g, j], dzs[g]) for g in heads]
            for g in heads:
                dkt_ref[g, j] += _dot_nt(q_ts[g], dzs[g])
            for g in heads:
                dvt_ref[g, j] += _dot_nt(do_ts[g], _mx(weights[g]))
            new_gsums = [gsums[g] + jnp.sum(grs[g], axis=0, keepdims=True) for g in heads]
            return tuple(new_gsums), tuple(new_dqts)

        _, dqts = lax.fori_loop(i - seen, i + 1, up, (zeros_row, tuple(jnp.zeros((d, t), F32) for _ in range(hg))))
        for g in range(hg):
            dqt_ref[g] = dqts[g]

    whole = pl.BlockSpec((hg, nb, d, t), lambda hh, i: (hh, 0, 0, 0))
    cols = pl.BlockSpec((hg, d, t), lambda hh, i: (hh, 0, i))
    tri = pl.BlockSpec((t, t), lambda hh, i: (0, 0))
    return _call(
        body, (kt_hm, vt_hm, qt_hm, dot_hm, runs, later, earlier),
        name=name,
        grid=(h // hg, nb),
        in_specs=[whole, whole, cols, cols, pl.BlockSpec((hg, None, nbp, t), lambda hh, i: (hh, i, 0, 0)), tri, tri],
        out_specs=[cols, whole, whole],
        out_shape=[jax.ShapeDtypeStruct((h, d, s), F32), jax.ShapeDtypeStruct((h, nb, d, t), F32),
                   jax.ShapeDtypeStruct((h, nb, d, t), F32)],
        rider=rider,
    )


def _qkv_bwd(p, dqt, dkt, dvt, qg, kg, headavg, *, col0, name, tm=512):
    s = p.shape[0]
    c = qg.shape[1]
    tm = _tile(s, tm)

    t = ATT_BLOCK
    assert tm % t == 0

    def rows_of(blocked_ref):
        parts = []
        for b in range(tm // t):
            cols = jnp.concatenate([blocked_ref[hh, b] for hh in range(HEADS)], axis=0)
            parts.append(cols.T)
        return jnp.concatenate(parts, axis=0)

    def body(q_ref, k_ref, dqt_ref, dkt_ref, dvt_ref, qg_ref, kg_ref, avg_ref, dqkv_ref, dg_ref):
        @pl.when(pl.program_id(0) == 0)
        def _():
            dg_ref[...] = jnp.zeros_like(dg_ref)

        dqn = jnp.concatenate([dqt_ref[hh] for hh in range(HEADS)], axis=0).T
        for n, (src, dn, g_ref) in enumerate(((q_ref, dqn, qg_ref), (k_ref, rows_of(dkt_ref), kg_ref))):
            xf = _f32(src)
            r = lax.rsqrt(_head_mean(xf * xf, avg_ref[...]) + EPS)
            y = xf * r
            dy = dn * g_ref[...]
            dqkv_ref[:, n * c:(n + 1) * c] = _mx(r * (dy - y * _head_mean(dy * y, avg_ref[...])))
            dg_ref[n] += _row_fold(dn * y)
        dqkv_ref[:, 2 * c:3 * c] = _mx(rows_of(dvt_ref))

    blk = lambda col: pl.BlockSpec((tm, c), lambda i: (i, col))
    vec = pl.BlockSpec((1, c), lambda i: (0, 0))
    blocked = pl.BlockSpec((HEADS, tm // t, HEAD_DIM, t), lambda i: (0, i, 0, 0))
    return pl.pallas_call(
        body,
        name=name,
        grid=(s // tm,),
        in_specs=[blk(col0), blk(col0 + 1), pl.BlockSpec((HEADS, HEAD_DIM, tm), lambda i: (0, 0, i)), blocked, blocked,
                  vec, vec, pl.BlockSpec((c, c), lambda i: (0, 0))],
        out_specs=[pl.BlockSpec((tm, 3 * c), lambda i: (i, 0)), pl.BlockSpec((2, 8, c), lambda i: (0, 0, 0))],
        out_shape=[jax.ShapeDtypeStruct((s, 3 * c), MXU_DTYPE), jax.ShapeDtypeStruct((2, 8, c), F32)],
        compiler_params=_params(),
    )(p, p, dqt, dkt, dvt, qg, kg, headavg)


def _place():
    x, y, c = lax.axis_index("x"), lax.axis_index("y"), lax.axis_index("c")
    chips = [(1 - x, y), (x, 1 - y), (1 - x, 1 - y)]
    return x, y, c, chips


def _shard_of(ref, axis, chip, width):
    idx = [slice(None)] * len(ref.shape)
    idx[axis] = pl.ds(chip * width, width)
    return ref.at[tuple(idx)]


def _place_shard(w, layer, axis, chip_idx, dtype, *, name):
    rows, cols = _as_rows(w.shape[1:])
    tr = _row_tile(rows, cols, 2)
    if axis == len(w.shape) - 2:
        out_shape = (1, rows, cols * N_CHIPS)
        out_spec = pl.BlockSpec((None, tr, cols), lambda i, j_ref: (0, i, j_ref[0]))
    else:
        assert axis == 0
        per = rows // tr
        out_shape = (1, rows * N_CHIPS, cols)
        out_spec = pl.BlockSpec((None, tr, cols), lambda i, j_ref: (0, j_ref[0] * per + i, 0))
    full = [1, *w.shape[1:]]
    full[1 + axis] *= N_CHIPS

    def body(j_ref, w_ref, o_ref):
        o_ref[...] = w_ref[...].astype(dtype)

    out = pl.pallas_call(
        body,
        name=name,
        grid_spec=pltpu.PrefetchScalarGridSpec(
            num_scalar_prefetch=1,
            grid=(rows // tr,),
            in_specs=[pl.BlockSpec((None, tr, cols), lambda i, j_ref: (layer, i, 0))],
            out_specs=out_spec,
        ),
        out_shape=jax.ShapeDtypeStruct(out_shape, dtype),
        compiler_params=_params(),
    )(chip_idx, w.reshape(w.shape[0], rows, cols))
    return out.reshape(full)


def _gather_riders(placed, axes, layer):
    n = len(placed)
    widths = [pa.shape[1 + ax] // N_CHIPS for pa, ax in zip(placed, axes)]

    def copies(refs, sems):
        x, y, c, chips = _place()

        def block(a, chip):
            return _shard_of(refs[a].at[0], axes[a], chip, widths[a])

        def over_ici(a, k, chip):
            return pltpu.make_async_remote_copy(
                src_ref=block(a, chip), dst_ref=block(a, chip), send_sem=sems[0].at[a, k], recv_sem=sems[1].at[a, k],
                device_id=(*chips[k], c), device_id_type=MESH)

        def to_sibling(a, k):
            cx, cy = chips[k]
            return pltpu.make_async_remote_copy(
                src_ref=block(a, 2 * cx + cy), dst_ref=block(a, 2 * cx + cy), send_sem=sems[0].at[a, k],
                recv_sem=sems[1].at[a, k], device_id=(x, y, 1 - c), device_id_type=MESH)

        return 2 * x + y, c, chips, over_ici, to_sibling

    def ici_start(refs, sems):
        me, c, _, over_ici, _ = copies(refs, sems)

        @pl.when(c == layer)
        def _():
            for a in range(n):
                for k in range(3):
                    over_ici(a, k, me).start()

    def ici_finish(refs, sems):
        me, c, chips, over_ici, _ = copies(refs, sems)

        @pl.when(c == layer)
        def _():
            for a in range(n):
                for k in range(3):
                    cx, cy = chips[k]
                    over_ici(a, k, 2 * cx + cy).wait_recv()
            for a in range(n):
                for k in range(3):
                    over_ici(a, k, me).wait_send()

    def d2d_start(refs, sems):
        _, c, _, _, to_sibling = copies(refs, sems)

        @pl.when(c == layer)
        def _():
            for a in range(n):
                for k in range(3):
                    to_sibling(a, k).start()

    def d2d_finish(refs, sems):
        _, c, _, _, to_sibling = copies(refs, sems)

        @pl.when(c == layer)
        def _():
            for a in range(n):
                for k in range(3):
                    to_sibling(a, k).wait_send()

        @pl.when(c != layer)
        def _():
            for a in range(n):
                for k in range(3):
                    to_sibling(a, k).wait_recv()

    def both_start(refs, sems):
        ici_start(refs, sems[:2])

    def both_finish(refs, sems):
        me, c, chips, over_ici, _ = copies(refs, sems[:2])
        to_sibling = copies(refs, sems[2:])[4]

        @pl.when(c == layer)
        def _():
            for a in range(n):
                for k in range(3):
                    cx, cy = chips[k]
                    over_ici(a, k, 2 * cx + cy).wait_recv()
                    to_sibling(a, k).start()
            for a in range(n):
                for k in range(3):
                    over_ici(a, k, me).wait_send()
                    to_sibling(a, k).wait_send()

        @pl.when(c != layer)
        def _():
            for a in range(n):
                for k in range(3):
                    to_sibling(a, k).wait_recv()

    arrays, sems = tuple(placed), ((n, 3), (n, 3))
    return (_Rider(arrays, sems, ici_start, ici_finish), _Rider(arrays, sems, d2d_start, d2d_finish),
            _Rider(arrays, sems * 2, both_start, both_finish))


def _join_riders(*riders):
    def each(which):
        def run(refs, sems):
            for r in riders:
                getattr(r, which)(refs[:len(r.arrays)], sems[:len(r.sems)])
                refs, sems = refs[len(r.arrays):], sems[len(r.sems):]
        return run

    return _Rider(sum((r.arrays for r in riders), ()), sum((r.sems for r in riders), ()), each("start"), each("finish"))


def _exchange_rider(grads, others, layer):
    n = len(grads)

    def copy(refs, sems, a):
        x, y, c, _ = _place()
        return pltpu.make_async_remote_copy(
            src_ref=refs[a], dst_ref=refs[n + a], send_sem=sems[0].at[a], recv_sem=sems[1].at[a],
            device_id=(x, y, 1 - c), device_id_type=MESH)

    def start(refs, sems):
        @pl.when(lax.axis_index("c") != layer)
        def _():
            for a in range(n):
                copy(refs, sems, a).start()

    def finish(refs, sems):
        @pl.when(lax.axis_index("c") != layer)
        def _():
            for a in range(n):
                copy(refs, sems, a).wait_send()

        @pl.when(lax.axis_index("c") == layer)
        def _():
            for a in range(n):
                copy(refs, sems, a).wait_recv()

    return _Rider((*grads, *others), ((n,), (n,)), start, finish)


def _scatter_rider(wires, landeds, axes, layer):
    n = len(wires)
    widths = [wr.shape[ax] // N_CHIPS for wr, ax in zip(wires, axes)]

    def copies(refs, sems):
        _, _, c, chips = _place()
        return [pltpu.make_async_remote_copy(
            src_ref=_shard_of(refs[a], axes[a], 2 * cx + cy, widths[a]), dst_ref=refs[n + a].at[k],
            send_sem=sems[0].at[a, k], recv_sem=sems[1].at[a, k], device_id=(cx, cy, c), device_id_type=MESH)
            for a in range(n) for k, (cx, cy) in enumerate(chips)]

    def start(refs, sems):
        @pl.when(lax.axis_index("c") == layer)
        def _():
            for cp in copies(refs, sems):
                cp.start()

    def finish(refs, sems):
        @pl.when(lax.axis_index("c") == layer)
        def _():
            for cp in copies(refs, sems):
                cp.wait()

    return _Rider((*wires, *landeds), ((n, 3), (n, 3)), start, finish)


def _share_rider(finals, layer):
    n = len(finals)

    def copy(refs, sems, a):
        x, y, c, _ = _place()
        return pltpu.make_async_remote_copy(
            src_ref=refs[a].at[layer], dst_ref=refs[a].at[layer], send_sem=sems[0].at[a], recv_sem=sems[1].at[a],
            device_id=(x, y, 1 - c), device_id_type=MESH)

    def start(refs, sems):
        @pl.when(lax.axis_index("c") == layer)
        def _():
            for a in range(n):
                copy(refs, sems, a).start()

    def finish(refs, sems):
        @pl.when(lax.axis_index("c") == layer)
        def _():
            for a in range(n):
                copy(refs, sems, a).wait_send()

        @pl.when(lax.axis_index("c") != layer)
        def _():
            for a in range(n):
                copy(refs, sems, a).wait_recv()

    return _Rider(tuple(finals), ((n,), (n,)), start, finish)


def _small_view(shape):
    size = math.prod(shape)
    return (size // 128, 128) if size % 1024 == 0 else (shape[0], size // shape[0])


def _all_reduce_small(parts, *, name):
    n = len(parts)
    views = [_small_view(a.shape) for a in parts]

    def body(*refs):
        ins, outs, slots = refs[:n], refs[n:2 * n], refs[2 * n:3 * n]
        send_sem, recv_sem = refs[3 * n:]
        x, y, c, _ = _place()
        me = 4 * x + 2 * y + c
        copies = []
        for a in range(n):
            slots[a][me] = ins[a][...]
            for k in range(1, N_DEV):
                peer = (x ^ (k >> 2), y ^ ((k >> 1) & 1), c ^ (k & 1))
                copies.append(pltpu.make_async_remote_copy(
                    src_ref=ins[a], dst_ref=slots[a].at[me], send_sem=send_sem.at[a, k - 1],
                    recv_sem=recv_sem.at[a, k - 1], device_id=peer, device_id_type=MESH))
        for cp in copies:
            cp.start()
        for a in range(n):
            for k in range(1, N_DEV):
                pltpu.make_async_remote_copy(
                    src_ref=ins[a], dst_ref=slots[a].at[me ^ k], send_sem=send_sem.at[a, k - 1],
                    recv_sem=recv_sem.at[a, k - 1], device_id=(x, y, c), device_id_type=MESH).wait()
        for a in range(n):
            total = slots[a][0]
            for dev in range(1, N_DEV):
                total = total + slots[a][dev]
            outs[a][...] = total

    vmem = pl.BlockSpec(memory_space=pltpu.VMEM)
    outs = pl.pallas_call(
        body,
        name=name,
        in_specs=[vmem] * n,
        out_specs=[vmem] * n,
        out_shape=[jax.ShapeDtypeStruct(view, F32) for view in views],
        scratch_shapes=[pltpu.VMEM((N_DEV, *view), F32) for view in views]
        + [pltpu.SemaphoreType.DMA((n, N_DEV - 1)), pltpu.SemaphoreType.DMA((n, N_DEV - 1))],
        compiler_params=_params(),
    )(*[a.reshape(view) for a, view in zip(parts, views)])
    return [o.reshape(a.shape) for o, a in zip(outs, parts)]


def _as_rows(shape):
    cols = shape[-1]
    return math.prod(shape[:-1]), cols


ELEMENTWISE_VMEM = 24 * 1024 * 1024


def _row_tile(rows, cols, n_arrays, sublanes=8):
    cap = ELEMENTWISE_VMEM // (n_arrays * 2 * 4 * cols)
    best = None
    for t in range(sublanes, min(rows, cap) + 1, sublanes):
        if rows % t == 0:
            best = t
    assert best is not None, (rows, cols)
    return best


def _pair_sum(g, other, active, *, name):
    rows, cols = _as_rows(g.shape)
    tr = _row_tile(rows, cols, 4, sublanes=16)

    def body(a_ref, g_ref, o_ref, out_ref, wire_ref):
        @pl.when(a_ref[0] == 1)
        def _():
            total = g_ref[...] + o_ref[...]
            out_ref[...] = total
            wire_ref[...] = total.astype(WIRE_DTYPE)

    blk = pl.BlockSpec((tr, cols), lambda i, a_ref: (i * a_ref[0], 0))
    out, wire = pl.pallas_call(
        body,
        name=name,
        grid_spec=pltpu.PrefetchScalarGridSpec(
            num_scalar_prefetch=1, grid=(rows // tr,), in_specs=[blk, blk], out_specs=[blk, blk]),
        out_shape=[jax.ShapeDtypeStruct((rows, cols), F32), jax.ShapeDtypeStruct((rows, cols), WIRE_DTYPE)],
        compiler_params=_params(),
    )(active, g.reshape(rows, cols), other.reshape(rows, cols))
    return out.reshape(g.shape), wire.reshape(g.shape)


def _chip_sum(mine, got, axis, chip_idx, active, layer, layers, into, *, name):
    shard_shape = got.shape[1:]
    rows, cols = _as_rows(shard_shape)
    tr = _row_tile(rows, cols, 5, sublanes=16)
    if axis == len(mine.shape) - 1:
        m2 = mine.reshape(rows, cols * N_CHIPS)
        mine_spec = pl.BlockSpec((tr, cols), lambda i, j_ref, a_ref: (i * a_ref[0], j_ref[0]))
    else:
        assert axis == 0
        m2 = mine.reshape(N_CHIPS, rows, cols)
        mine_spec = pl.BlockSpec((None, tr, cols), lambda i, j_ref, a_ref: (j_ref[0], i * a_ref[0], 0))

    def body(j_ref, a_ref, m_ref, got_ref, *refs):
        @pl.when(a_ref[0] == 1)
        def _():
            refs[-1][...] = ((m_ref[...] + got_ref[0].astype(F32)) + got_ref[1].astype(F32)) + got_ref[2].astype(F32)

    out = pl.pallas_call(
        body,
        name=name,
        grid_spec=pltpu.PrefetchScalarGridSpec(
            num_scalar_prefetch=2,
            grid=(rows // tr,),
            in_specs=[mine_spec, pl.BlockSpec((3, tr, cols), lambda i, j_ref, a_ref: (0, i * a_ref[0], 0))]
            + ([] if into is None else [ANY]),
            out_specs=pl.BlockSpec((None, tr, cols), lambda i, j_ref, a_ref: (layer, i * a_ref[0], 0)),
        ),
        out_shape=jax.ShapeDtypeStruct((layers, rows, cols), F32),
        input_output_aliases={} if into is None else {4: 0},
        compiler_params=_params(),
    )(chip_idx, active, m2, got.reshape(3, rows, cols), *([] if into is None else [into.reshape(layers, rows, cols)]))
    return out.reshape((layers, *shard_shape))


class _ReduceChain:
    def __init__(self, grads, layer, layers, chip_idx, core_idx, finals):
        self.grads, self.layer, self.layers, self.chip_idx, self.finals = list(grads), layer, layers, chip_idx, finals
        self.active = (core_idx == layer).astype(jnp.int32)
        self.axes = [BIG_AXIS[n] for n in BIG]
        self.pairs = None

    def exchange(self):
        return _exchange_rider(self.grads, [lax.empty(g.shape, F32) for g in self.grads], self.layer)

    def exchanged(self, arrived):
        n = len(self.grads)
        self.pairs = [_pair_sum(g, o, self.active, name=f"pair_sum_{name}_l{self.layer}")
                      for name, g, o in zip(BIG, arrived[:n], arrived[n:])]

    def scatter(self):
        landeds = []
        for (_, wire), ax in zip(self.pairs, self.axes):
            shard = list(wire.shape)
            shard[ax] //= N_CHIPS
            landeds.append(lax.empty((3, *shard), WIRE_DTYPE))
        return _scatter_rider([wire for _, wire in self.pairs], landeds, self.axes, self.layer)

    def scattered(self, arrived):
        n = len(self.grads)
        into = self.finals or [None] * n
        self.finals = [_chip_sum(pair, landed, ax, self.chip_idx, self.active, self.layer, self.layers, old,
                                 name=f"chip_sum_{name}_l{self.layer}")
                       for name, (pair, _), landed, ax, old in zip(BIG, self.pairs, arrived[n:], self.axes, into)]

    def share(self):
        return _share_rider(self.finals, self.layer)

    def shared(self, arrived):
        self.finals = list(arrived)


def _adamw_update(w_ref, g_ref, m_ref, v_ref, d_ref, nm_ref, nv_ref):
    c1 = 1.0 / (1.0 - ADAM_B1 ** ADAM_STEP)
    c2 = 1.0 / (1.0 - ADAM_B2 ** ADAM_STEP)
    gg = g_ref[...]
    nm = ADAM_B1 * m_ref[...] + (1.0 - ADAM_B1) * gg
    nv = ADAM_B2 * v_ref[...] + (1.0 - ADAM_B2) * (gg * gg)
    nm_ref[...] = nm
    nv_ref[...] = nv
    d_ref[...] = -ADAM_LR * ((nm * c1) / (jnp.sqrt(nv * c2) + ADAM_EPS) + ADAM_WD * w_ref[...])


def _adamw_small(ws, gs, ms, vs, *, name):
    n = len(ws)
    views = [_small_view(a.shape) for a in ws]

    def body(*refs):
        w_r, g_r, m_r, v_r = (refs[k * n:(k + 1) * n] for k in range(4))
        d_r, nm_r, nv_r = (refs[(4 + k) * n:(5 + k) * n] for k in range(3))
        for a in range(n):
            _adamw_update(w_r[a], g_r[a], m_r[a], v_r[a], d_r[a], nm_r[a], nv_r[a])

    vmem = pl.BlockSpec(memory_space=pltpu.VMEM)
    flat = lambda arrs: [a.reshape(view) for a, view in zip(arrs, views)]
    outs = pl.pallas_call(
        body,
        name=name,
        in_specs=[vmem] * (4 * n),
        out_specs=[vmem] * (3 * n),
        out_shape=[jax.ShapeDtypeStruct(view, F32) for view in views] * 3,
        compiler_params=_params(),
    )(*flat(ws), *flat(gs), *flat(ms), *flat(vs))
    return [tuple(outs[k * n + a].reshape(ws[a].shape) for k in range(3)) for a in range(n)]


def _adamw(w, g, m, v, *, name):
    shape = w.shape
    rows, cols = _as_rows(shape)
    tr = _row_tile(rows, cols, 7)
    body = functools.partial(_adamw_update)

    blk = pl.BlockSpec((tr, cols), lambda i: (i, 0))
    flat = lambda a: a.reshape(rows, cols)
    outs = pl.pallas_call(
        body,
        name=name,
        grid=(rows // tr,),
        in_specs=[blk] * 4,
        out_specs=[blk] * 3,
        out_shape=[jax.ShapeDtypeStruct((rows, cols), F32)] * 3,
        compiler_params=_params(),
    )(flat(w), flat(g), flat(m), flat(v))
    return tuple(o.reshape(shape) for o in outs)


BIG = ("w_in", "w_branch_out", "w_o", "w_gate_up", "w_down")
BIG_AXIS = {"w_in": 1, "w_branch_out": 2, "w_o": 0, "w_gate_up": 1, "w_down": 0}
SMALL = ("mix_norm_g", "b_gate", "conv_w", "conv_b", "sgu_ln_g", "sgu_ln_b", "sgu_w", "sgu_b", "q_norm_g", "k_norm_g",
         "ffn_norm_g")
ORDER = ("mix_norm_g", "w_in", "b_gate", "conv_w", "conv_b", "sgu_ln_g", "sgu_ln_b", "sgu_w", "sgu_b", "q_norm_g",
         "k_norm_g", "w_branch_out", "w_o", "ffn_norm_g", "w_gate_up", "w_down")


def _layer_forward(x, w, wl, l, coming):
    n_in = wl["w_in"].shape[2]
    gate_col0 = (n_in - 3 * x.shape[1]) // x.shape[1]
    tag = f"l{l}"
    rider = None
    if coming:
        placed, axes, nxt = coming
        half = len(placed) // 2
        first, second = (placed[:half], axes[:half], nxt), (placed[half:], axes[half:], nxt)
        rider = _gather_riders(*first)[0]
    (p, h), arrived = _norm_matmul(x, w["mix_norm_g"][l][None], wl["w_in"], 0, name=f"in_proj_{tag}", tn=n_in // 7,
                                   rider=rider)
    ya = _conv_fwd(p, wl["conv_w"][0], w["conv_b"][l][None], name=f"conv_{tag}")
    yb = _sgu_fwd(p, w["sgu_ln_g"][l][None], w["sgu_ln_b"][l][None], w["wtril"][l], w["bias_full"][l], col0=3,
                  name=f"sgu_{tag}")
    q_c, k_cb, v_cb = _qkv_prep(p, w["qg"][l], w["kg"][l], w["headavg"], col0=5, name=f"qkv_{tag}")
    if coming:
        rider = _join_riders(_gather_riders(list(arrived), *first[1:])[1], _gather_riders(*second)[0])
    (out_t, runs), arrived = _att_fwd(k_cb, q_c, v_cb, w["later"], name=f"att_{tag}", rider=rider)
    x1, merged = _merge_fwd((ya, yb, out_t), p, w["b_gate"][l][None], wl["w_branch_out"], wl["w_o"], 0, x,
                            gate_col0=gate_col0, name=f"merge_{tag}")
    if coming:
        here = arrived[:half]
        rider = _gather_riders(list(arrived[half:]), *second[1:])[1]
    (gu, h2), arrived = _norm_matmul(x1, w["ffn_norm_g"][l][None], wl["w_gate_up"], 0, name=f"gate_up_{tag}",
                                     tn=wl["w_gate_up"].shape[2] // 4, rider=rider)
    if coming:
        arrived = (*here, *arrived)
    x2 = _ffn_down(gu, wl["w_down"], 0, x1, name=f"down_{tag}")
    saved = dict(x=x, p=p, h=h, ya=ya, yb=yb, yct=out_t, merged=merged, x1=x1, gu=gu, h2=h2,
                 att=(k_cb, v_cb, q_c, runs), gate_col0=gate_col0)
    return x2, saved, arrived


def _layer_backward(dx2, w, wl, sv, l, chain, make_chain, last):
    tag = f"l{l}"
    c = w["conv_b"].shape[1]
    n_in = wl["w_in"].shape[2]
    n_ff = wl["w_gate_up"].shape[2]
    g, big = {}, {}
    (dgu, act, dx2b), arrived = _ffn_bwd(dx2, sv["gu"], wl["w_down"], 0, name=f"down_bwd_{tag}",
                                         rider=chain.exchange() if chain else None)
    if chain:
        chain.exchanged(arrived)
    big["w_down"] = _matmul_tn(act, dx2b, 0, 1, None, name=f"dw_down_{tag}", t1=act.shape[1] // 2)
    (dx1, dg2), _ = _matmul_nt_normbwd([dgu], wl["w_gate_up"], 0, sv["x1"], w["ffn_norm_g"][l][None], dx2,
                                       name=f"gate_up_bwd_{tag}")
    g["ffn_norm_g"] = jnp.sum(dg2, axis=0)
    big["w_gate_up"] = _matmul_tn(sv["h2"], dgu, 0, 1, None, name=f"dw_gate_up_{tag}", t1=512)
    ys = (sv["ya"], sv["yb"], sv["yct"])
    (dgates, dya, dyb, dyct, dd0, dd1, dd2, dx1b, dbg) = _merge_bwd(
        dx1, ys, sv["p"], w["b_gate"][l][None], wl["w_branch_out"], wl["w_o"], 0, gate_col0=sv["gate_col0"],
        name=f"merge_bwd_{tag}")
    g["b_gate"] = jnp.sum(dbg, axis=0)
    big["w_o"] = _matmul_tn(sv["merged"], dx1b, 0, 1, None, name=f"dw_o_{tag}")
    for i, (y, dd) in enumerate(zip(ys, (dd0, dd1, dd2))):
        heads_first = y.ndim == 3
        big["w_branch_out"] = _matmul_tn(y.reshape(c, -1) if heads_first else y, dd, i, len(ys), big.get("w_branch_out"),
                                         name=f"dw_bo{i}_{tag}", a_transposed=heads_first)
    dconv, dwc = _conv_bwd(sv["p"], dya, wl["conv_w"][0], w["conv_b"][l][None], name=f"conv_bwd_{tag}")
    dwc = jnp.sum(dwc, axis=1)
    g["conv_w"] = dwc[0:CONV_K]
    g["conv_b"] = dwc[CONV_K]
    dsgu, dln, dws, dbias = _sgu_bwd(sv["p"], dyb, w["sgu_ln_g"][l][None], w["sgu_ln_b"][l][None], w["wtril"][l],
                                     w["wtril_t"][l], w["bias_full"][l], col0=3, name=f"sgu_bwd_{tag}")
    dln = jnp.sum(dln, axis=1)
    g["sgu_ln_g"], g["sgu_ln_b"] = dln[0], dln[1]
    g["sgu_w"] = jnp.where(w["tril"], dws, 0.0)
    g["sgu_b"] = jnp.sum(dbias.reshape(CHUNK, c // CHUNK, CHUNK), axis=2).T
    k_cb, v_cb, q_c, runs = sv["att"]
    (dqt, dkt, dvt), arrived = _att_bwd(k_cb, v_cb, q_c, dyct, runs, w["later"], w["earlier"],
                                        name=f"att_bwd_{tag}", rider=chain.scatter() if chain else None)
    if chain:
        chain.scattered(arrived)
    dqkv, dqkg = _qkv_bwd(sv["p"], dqt, dkt, dvt, w["qg"][l], w["kg"][l], w["headavg"], col0=5, name=f"qkv_bwd_{tag}")
    dqkg = jnp.sum(dqkg.reshape(2, 8 * HEADS, HEAD_DIM), axis=1)
    g["q_norm_g"], g["k_norm_g"] = dqkg[0], dqkg[1]
    pieces = [dconv, dsgu, dqkv, dgates]
    col0 = 0
    for k, pc in enumerate(pieces):
        width = pc.shape[1]
        big["w_in"] = _matmul_tn(sv["h"], pc, 0, 1, big.get("w_in"), name=f"dw_in{k}_{tag}", col0=col0, n_total=n_in,
                                 tn=math.gcd(col0, width) if col0 else width)
        col0 += width
    big = {n: a.reshape(wl[n].shape[1:]) for n, a in big.items()}
    own = make_chain([big[n] for n in BIG], l)
    riders = [chain.share()] if chain else []
    if last:
        own.exchanged(_ride_alone([own.exchange()], name=f"grads_pair_exchange_{tag}"))
        riders.append(own.scatter())
    (dx0, dg1), arrived = _matmul_nt_normbwd(pieces, wl["w_in"], 0, sv["x"], w["mix_norm_g"][l][None], dx1,
                                             name=f"in_proj_bwd_{tag}", rider=_join_riders(*riders) if riders else None)
    if chain:
        count = len(chain.finals)
        chain.shared(arrived[:count])
        arrived = arrived[count:]
        own.finals = chain.finals
    if last:
        own.scattered(arrived)
        own.shared(_ride_alone([own.share()], name=f"grads_pair_share_{tag}"))
    g["mix_norm_g"] = jnp.sum(dg1, axis=0)
    return dx0, g, own


def kernel(x, mix_norm_g, w_in, b_gate, conv_w, conv_b, sgu_ln_g, sgu_ln_b, sgu_w, sgu_b, q_norm_g, k_norm_g, w_branch_out, w_o, ffn_norm_g, w_gate_up, w_down, loss_target, m_mix_norm_g, m_w_in, m_b_gate, m_conv_w, m_conv_b, m_sgu_ln_g, m_sgu_ln_b, m_sgu_w, m_sgu_b, m_q_norm_g, m_k_norm_g, m_w_branch_out, m_w_o, m_ffn_norm_g, m_w_gate_up, m_w_down, v_mix_norm_g, v_w_in, v_b_gate, v_conv_w, v_conv_b, v_sgu_ln_g, v_sgu_ln_b, v_sgu_w, v_sgu_b, v_q_norm_g, v_k_norm_g, v_w_branch_out, v_w_o, v_ffn_norm_g, v_w_gate_up, v_w_down):
    given = dict(locals())
    params = {n: given[n] for n in ORDER}
    moms = {n: (given["m_" + n], given["v_" + n]) for n in ORDER}
    layers = mix_norm_g.shape[0]
    assert layers == 2, "the exchanges split the work of a chip's two cores by layer"
    xs = x[0]
    target = loss_target[0]
    chip = 2 * lax.axis_index("x") + lax.axis_index("y")
    core = lax.axis_index("c")

    c_idx = core.reshape(1).astype(jnp.int32)
    j_idx = chip.reshape(1).astype(jnp.int32)
    width = conv_w.shape[2]
    gathered = BIG + ("conv_w",)
    axes = [BIG_AXIS[n] for n in BIG] + [1]

    def placed(layer):
        arrays = [_place_shard(params[n], layer, BIG_AXIS[n], j_idx, MXU_DTYPE, name=f"place_{n}_l{layer}") for n in BIG]
        mine = lax.dynamic_slice_in_dim(conv_w, layer, 1, axis=0)
        arrays.append(lax.dynamic_update_slice_in_dim(jnp.zeros((1, CONV_K, width * N_CHIPS), F32), mine, chip * width, 2))
        return arrays

    weights = [dict(zip(gathered, _ride_alone(_gather_riders(placed(0), axes, 0)[2:], name="gather_weights_l0"))), None]
    w = {n: params[n] for n in ("mix_norm_g", "b_gate", "conv_b", "sgu_ln_g", "sgu_ln_b", "ffn_norm_g")}
    tril = jnp.tril(jnp.ones((CHUNK, CHUNK), dtype=bool))
    w["tril"] = tril
    w["wtril"] = _mx(jnp.where(tril, sgu_w, 0.0))
    w["wtril_t"] = w["wtril"].transpose(0, 1, 3, 2)
    w["bias_full"] = jnp.repeat(sgu_b.transpose(0, 2, 1), CHUNK, axis=2)
    w["qg"] = jnp.tile(q_norm_g, (1, HEADS))[:, None, :]
    w["kg"] = jnp.tile(k_norm_g, (1, HEADS))[:, None, :]
    lane = jnp.arange(HEADS * HEAD_DIM) // HEAD_DIM
    w["headavg"] = _mx(jnp.where(lane[:, None] == lane[None, :], 1.0 / HEAD_DIM, 0.0))
    pos = jnp.arange(ATT_BLOCK)
    w["later"] = _mx(jnp.where(pos[None, :] > pos[:, None], 1.0, 0.0))
    w["earlier"] = _mx(jnp.where(pos[None, :] < pos[:, None], 1.0, 0.0))

    saved = []
    act = xs
    for l in range(layers):
        coming = (placed(l + 1), axes, l + 1) if l + 1 < layers else None
        act, sv, arrived = _layer_forward(act, w, weights[l], l, coming)
        saved.append(sv)
        if coming:
            weights[l + 1] = dict(zip(gathered, arrived))
    loss_part, dact = _loss_head(act, target, name="loss_head")
    loss = lax.psum(jnp.sum(loss_part), ("x", "y", "c"))
    grads = [None] * layers
    chain = None
    make_chain = lambda large, layer: _ReduceChain(large, layer, layers, j_idx, c_idx, None)
    for l in reversed(range(layers)):
        dact, grads[l], chain = _layer_backward(dact, w, weights[l], saved[l], l, chain, make_chain, l == 0)
    grad_x = dact[None]
    local = {n: jnp.stack([grads[l][n] for l in range(layers)]) for n in SMALL}
    full = dict(zip(BIG, chain.finals))

    summed = _all_reduce_small([local[n] for n in SMALL], name="grads_all_reduce_small")
    for n, gsum in zip(SMALL, summed):
        full[n] = gsum
    full["conv_w"] = lax.dynamic_slice_in_dim(full["conv_w"], chip * width, width, axis=2)

    out = {}
    for n in BIG:
        out[n] = _adamw(params[n], full[n], *moms[n], name=f"adamw_{n}")
    small = _adamw_small([params[n] for n in SMALL], [full[n] for n in SMALL], [moms[n][0] for n in SMALL],
                         [moms[n][1] for n in SMALL], name="adamw_small")
    for n, triple in zip(SMALL, small):
        out[n] = triple
    return (loss, grad_x, *[full[n] for n in ORDER], *[out[n][0] for n in ORDER], *[out[n][1] for n in ORDER],
            *[out[n][2] for n in ORDER])
```

```python
import functools
import math
from typing import Callable, NamedTuple

import jax
import jax.numpy as jnp
from jax import lax
from jax.experimental import pallas as pl
from jax.experimental.pallas import tpu as pltpu

F32 = jnp.float32
MXU_DTYPE = jnp.bfloat16
WIRE_DTYPE = jnp.bfloat16
ACT_DTYPE = jnp.bfloat16
HALO = 16

EPS = 1e-6
CONV_K = 3
CHUNK = 128
HEADS = 8
HEAD_DIM = 64
ATT_BLOCK = 128
EXP_UNDERFLOW = -88.0

ADAM_LR = 0.001
ADAM_B1 = 0.9
ADAM_B2 = 0.999
ADAM_EPS = 1e-08
ADAM_WD = 0.01
ADAM_STEP = 10

VMEM_LIMIT = 56 * 1024 * 1024
MESH = pl.DeviceIdType.MESH
N_CHIPS = 4
N_DEV = 8
ANY = pl.BlockSpec(memory_space=pl.ANY)


def _params(**kw):
    return pltpu.CompilerParams(vmem_limit_bytes=VMEM_LIMIT, **kw)


class _Rider(NamedTuple):
    arrays: tuple
    sems: tuple
    start: Callable
    finish: Callable


def _call(body, operands, *, name, grid, in_specs, out_specs, out_shape, scratch_shapes=(), rider=None):
    if rider is None:
        outs = pl.pallas_call(body, name=name, grid=grid, in_specs=in_specs, out_specs=out_specs, out_shape=out_shape,
                              scratch_shapes=list(scratch_shapes), compiler_params=_params())(*operands)
        return tuple(outs), ()
    n_in, n_out, n_scr, k = len(in_specs), len(out_specs), len(scratch_shapes), len(rider.arrays)

    def riding(*refs):
        ins, refs = refs[:n_in], refs[n_in + k:]
        outs, carried, refs = refs[:n_out], refs[n_out:n_out + k], refs[n_out + k:]
        scratch, sems = refs[:n_scr], refs[n_scr:]
        first = functools.reduce(jnp.logical_and, [pl.program_id(ax) == 0 for ax in range(len(grid))])
        last = functools.reduce(jnp.logical_and, [pl.program_id(ax) == grid[ax] - 1 for ax in range(len(grid))])

        @pl.when(first)
        def _():
            rider.start(carried, sems)

        body(*ins, *outs, *scratch)

        @pl.when(last)
        def _():
            rider.finish(carried, sems)

    res = pl.pallas_call(
        riding, name=name, grid=grid, in_specs=[*in_specs, *[ANY] * k], out_specs=[*out_specs, *[ANY] * k],
        out_shape=[*out_shape, *[jax.ShapeDtypeStruct(a.shape, a.dtype) for a in rider.arrays]],
        input_output_aliases={n_in + j: n_out + j for j in range(k)},
        scratch_shapes=[*scratch_shapes, *[pltpu.SemaphoreType.DMA(shape) for shape in rider.sems]],
        compiler_params=_params())(*operands, *rider.arrays)
    return tuple(res[:n_out]), tuple(res[n_out:])


def _ride_alone(riders, *, name):
    arrays = riders[0].arrays
    k = len(arrays)
    counts = [len(r.sems) for r in riders]

    def body(*refs):
        carried, sems = refs[k:2 * k], refs[2 * k:]
        for r, count in zip(riders, counts):
            r.start(carried, sems[:count])
            r.finish(carried, sems[:count])
            sems = sems[count:]

    return pl.pallas_call(
        body, name=name, in_specs=[ANY] * k, out_specs=[ANY] * k,
        out_shape=[jax.ShapeDtypeStruct(a.shape, a.dtype) for a in arrays],
        input_output_aliases={j: j for j in range(k)},
        scratch_shapes=[pltpu.SemaphoreType.DMA(shape) for r in riders for shape in r.sems])(*arrays)


def _mx(v):
    return v.astype(MXU_DTYPE)


def _dot(a, b):
    return lax.dot_general(a, b, (((1,), (0,)), ((), ())), preferred_element_type=F32)


def _dot_nt(a, b):
    return lax.dot_general(a, b, (((1,), (1,)), ((), ())), preferred_element_type=F32)


def _dot_tn(a, b):
    return lax.dot_general(a, b, (((0,), (0,)), ((), ())), preferred_element_type=F32)


def _dot_split_r(v, const):
    hi = _mx(v)
    lo = _mx(v - hi.astype(F32))
    return _dot(hi, const) + _dot(lo, const)


def _sigmoid(x):
    return 1.0 / (1.0 + jnp.exp(-x))


_INV_SQRT2 = 1.0 / math.sqrt(2.0)
_INV_SQRT2PI = 1.0 / math.sqrt(2.0 * math.pi)


def _gelu(x):
    return 0.5 * x * (1.0 + lax.erf(x * _INV_SQRT2))


def _gelu_grad(x):
    return 0.5 * (1.0 + lax.erf(x * _INV_SQRT2)) + x * jnp.exp(-0.5 * x * x) * _INV_SQRT2PI


def _row_fold(v):
    m, n = v.shape
    return jnp.sum(v.reshape(m // 8, 8, n), axis=0)


def _tile(m, pref):
    t = min(m, pref)
    while m % t:
        t //= 2
    return t


def _resident(block_shape, index_map):
    return pl.BlockSpec(block_shape, index_map, pipeline_mode=pl.Buffered(1))


def _norm_matmul(x, g, w, l, *, name, tm=512, tn=None, rider=None):
    s, d = x.shape
    n = w.shape[2]
    tm = _tile(s, tm)
    tn = tn or n
    assert n % tn == 0

    def body(x_ref, g_ref, w_ref, p_ref, h_ref):
        xf = x_ref[...]
        r = lax.rsqrt(jnp.mean(xf * xf, axis=-1, keepdims=True) + EPS)
        h = _mx(xf * r * g_ref[...])
        h_ref[...] = h
        for j in range(n // tn):
            cols = slice(j * tn, (j + 1) * tn)
            p_ref[:, cols] = _dot(h, w_ref[:, cols]).astype(ACT_DTYPE)

    return _call(
        body, (x, g, w),
        name=name,
        grid=(s // tm,),
        in_specs=[
            pl.BlockSpec((tm, d), lambda i: (i, 0)),
            pl.BlockSpec((1, d), lambda i: (0, 0)),
            _resident((None, d, n), lambda i: (l, 0, 0)),
        ],
        out_specs=[
            pl.BlockSpec((tm, n), lambda i: (i, 0)),
            pl.BlockSpec((tm, d), lambda i: (i, 0)),
        ],
        out_shape=[jax.ShapeDtypeStruct((s, n), ACT_DTYPE), jax.ShapeDtypeStruct((s, d), MXU_DTYPE)],
        rider=rider,
    )


def _conv_taps(u, prev):
    row = lax.broadcasted_iota(jnp.int32, u.shape, 0)
    last, before = prev[HALO - 1:HALO, :], prev[HALO - 2:HALO - 1, :]
    um1 = jnp.where(row == 0, last, pltpu.roll(u, 1, 0))
    um2 = pltpu.roll(u, 2, 0)
    um2 = jnp.where(row == 0, before, jnp.where(row == 1, last, um2))
    return um1, um2


def _f32(ref):
    return ref[...].astype(F32)


def _conv_fwd(p, conv_w, conv_b, *, name, tm=512):
    s = p.shape[0]
    c = conv_w.shape[1]
    tm = _tile(s, tm)
    hb = tm // HALO

    def body(bg_ref, cg_ref, xa_ref, cgp_ref, xap_ref, w_ref, b_ref, y_ref):
        first = pl.program_id(0) == 0
        u = _f32(cg_ref) * _f32(xa_ref)
        prev = jnp.where(first, 0.0, _f32(cgp_ref) * _f32(xap_ref))
        um1, um2 = _conv_taps(u, prev)
        w = w_ref[...]
        y = b_ref[...] + w[0:1, :] * um2 + w[1:2, :] * um1 + w[2:3, :] * u
        y_ref[...] = _mx(_f32(bg_ref) * y)

    halo = lambda col: pl.BlockSpec((HALO, c), lambda i: (jnp.maximum(i * hb - 1, 0), col))
    return pl.pallas_call(
        body,
        name=name,
        grid=(s // tm,),
        in_specs=[
            pl.BlockSpec((tm, c), lambda i: (i, 0)),
            pl.BlockSpec((tm, c), lambda i: (i, 1)),
            pl.BlockSpec((tm, c), lambda i: (i, 2)),
            halo(1),
            halo(2),
            pl.BlockSpec((CONV_K, c), lambda i: (0, 0)),
            pl.BlockSpec((1, c), lambda i: (0, 0)),
        ],
        out_specs=pl.BlockSpec((tm, c), lambda i: (i, 0)),
        out_shape=jax.ShapeDtypeStruct((s, c), MXU_DTYPE),
        compiler_params=_params(),
    )(p, p, p, p, p, conv_w, conv_b)


def _layernorm_stats(x):
    mu = jnp.mean(x, axis=-1, keepdims=True)
    xc = x - mu
    r = lax.rsqrt(jnp.mean(xc * xc, axis=-1, keepdims=True) + EPS)
    return xc * r, r


def _sgu_fwd(p, ln_g, ln_b, wtril, bias_full, *, col0, name, tm=512):
    s = p.shape[0]
    c = ln_g.shape[1]
    groups = c // CHUNK
    tm = _tile(s, tm)

    def body(u_ref, v_ref, g_ref, b_ref, w_ref, bias_ref, y_ref):
        vn, _ = _layernorm_stats(_gelu(_f32(v_ref)))
        vb = _mx(vn * g_ref[...] + b_ref[...])
        for n in range(tm // CHUNK):
            rows = slice(n * CHUNK, (n + 1) * CHUNK)
            for gi in range(groups):
                cols = slice(gi * CHUNK, (gi + 1) * CHUNK)
                mixed = _dot(w_ref[gi], vb[rows, cols]) + bias_ref[:, cols]
                y_ref[rows, cols] = _mx(_gelu(u_ref[rows, cols].astype(F32)) * mixed)

    return pl.pallas_call(
        body,
        name=name,
        grid=(s // tm,),
        in_specs=[
            pl.BlockSpec((tm, c), lambda i: (i, col0)),
            pl.BlockSpec((tm, c), lambda i: (i, col0 + 1)),
            pl.BlockSpec((1, c), lambda i: (0, 0)),
            pl.BlockSpec((1, c), lambda i: (0, 0)),
            pl.BlockSpec((groups, CHUNK, CHUNK), lambda i: (0, 0, 0)),
            pl.BlockSpec((CHUNK, c), lambda i: (0, 0)),
        ],
        out_specs=pl.BlockSpec((tm, c), lambda i: (i, 0)),
        out_shape=jax.ShapeDtypeStruct((s, c), MXU_DTYPE),
        compiler_params=_params(),
    )(p, p, ln_g, ln_b, wtril, bias_full)


def _head_mean(v, headavg):
    return _dot_split_r(v, headavg)


def _qkv_prep(p, qg, kg, headavg, *, col0, name, tm=512):
    s = p.shape[0]
    c = qg.shape[1]
    t = ATT_BLOCK
    tm = _tile(s, tm)
    assert tm % t == 0

    def body(q_ref, k_ref, v_ref, qg_ref, kg_ref, avg_ref, qc_ref, kcb_ref, vcb_ref):
        normed = []
        for src, g_ref in ((q_ref, qg_ref), (k_ref, kg_ref)):
            xf = _f32(src)
            r = lax.rsqrt(_head_mean(xf * xf, avg_ref[...]) + EPS)
            normed.append(xf * r * g_ref[...])
        q_t, k_t, v_t = (_mx(a.T) for a in (*normed, _f32(v_ref)))
        for hh in range(HEADS):
            rows = slice(hh * HEAD_DIM, (hh + 1) * HEAD_DIM)
            qc_ref[hh] = q_t[rows, :]
            for b in range(tm // t):
                kcb_ref[hh, b] = k_t[rows, b * t:(b + 1) * t]
                vcb_ref[hh, b] = v_t[rows, b * t:(b + 1) * t]

    blk = lambda col: pl.BlockSpec((tm, c), lambda i: (i, col))
    vec = pl.BlockSpec((1, c), lambda i: (0, 0))
    blocked = pl.BlockSpec((HEADS, tm // t, HEAD_DIM, t), lambda i: (0, i, 0, 0))
    return pl.pallas_call(
        body,
        name=name,
        grid=(s // tm,),
        in_specs=[blk(col0), blk(col0 + 1), blk(col0 + 2), vec, vec, pl.BlockSpec((c, c), lambda i: (0, 0))],
        out_specs=[pl.BlockSpec((HEADS, HEAD_DIM, tm), lambda i: (0, 0, i)), blocked, blocked],
        out_shape=[jax.ShapeDtypeStruct((HEADS, HEAD_DIM, s), MXU_DTYPE)]
        + [jax.ShapeDtypeStruct((HEADS, s // t, HEAD_DIM, t), MXU_DTYPE)] * 2,
        compiler_params=_params(),
    )(p, p, p, qg, kg, headavg)


def _att_mask(t, key0, qry0):
    kpos = key0 + lax.broadcasted_iota(jnp.int32, (t, t), 0)
    qpos = qry0 + lax.broadcasted_iota(jnp.int32, (t, t), 1)
    return kpos < qpos


def _att_blocks(k_blks, q_ts, scale, mask):
    zs = [(_dot_tn(k, q) if k.shape[0] == q.shape[0] else _dot(k, q)) * scale for k, q in zip(k_blks, q_ts)]
    es = [jnp.exp(-jnp.abs(z)) for z in zs]
    lbs = [jnp.minimum(z, 0.0) - jnp.log1p(e) for z, e in zip(zs, es)]
    l1ms = [jnp.where(mask, lb - z, 0.0) for lb, z in zip(lbs, zs)]
    return zs, es, lbs, l1ms


def _dot_split_each(const, vs):
    his = [_mx(v) for v in vs]
    los = [_mx(v - hi.astype(F32)) for v, hi in zip(vs, his)]
    tops = [_dot(const, hi) for hi in his]
    return [top + _dot(const, lo) for top, lo in zip(tops, los)]


def _max_over(runs):
    m = runs[0]
    for r in runs[1:]:
        m = jnp.maximum(m, r)
    return jnp.max(m)


NOT_SEEN = -1e30


def _row_put(ref, g, j, row):
    j8 = pl.multiple_of((j // 8) * 8, 8)
    blk = ref[g, pl.ds(j8, 8), :]
    sub = lax.broadcasted_iota(jnp.int32, blk.shape, 0)
    ref[g, pl.ds(j8, 8), :] = jnp.where(sub == j - j8, row, blk)


def _row_get(ref, g, j):
    j8 = pl.multiple_of((j // 8) * 8, 8)
    blk = ref[g, pl.ds(j8, 8), :]
    sub = lax.broadcasted_iota(jnp.int32, blk.shape, 0)
    return jnp.sum(jnp.where(sub == j - j8, blk, 0.0), axis=0, keepdims=True)


def _att_fwd(k_hm, qt_hm, vt_hm, later, *, name, hg=8, rider=None):
    h, nb, d, t = k_hm.shape
    s = nb * t
    nbp = -(-nb // 8) * 8
    scale = 1.0 / math.sqrt(d)
    assert h % hg == 0

    def body(k_ref, qt_ref, vt_ref, later_ref, o_ref, runs_ref):
        i = pl.program_id(1)
        q_ts = [qt_ref[g] for g in range(hg)]
        runs_ref[...] = jnp.full(runs_ref.shape, NOT_SEEN, F32)

        def cond(carry):
            j, _, _, rmax = carry
            return jnp.logical_and(j >= 0, rmax > EXP_UNDERFLOW)

        def step(carry):
            j, runs, accs, _ = carry
            mask = _att_mask(t, j * t, i * t)
            heads = range(hg)
            for g in heads:
                _row_put(runs_ref, g, j, runs[g])
            _, _, lbs, l1ms = _att_blocks([k_ref[g, j] for g in heads], q_ts, scale, mask)
            afters = _dot_split_each(later_ref[...], l1ms)
            weights = [_mx(jnp.where(mask, jnp.exp(lbs[g] + afters[g] + runs[g]), 0.0)) for g in heads]
            new_accs = [accs[g] + _dot(vt_ref[g, j], weights[g]) for g in heads]
            new_runs = [runs[g] + jnp.sum(l1ms[g], axis=0, keepdims=True) for g in heads]
            return j - 1, tuple(new_runs), tuple(new_accs), _max_over(new_runs)

        init = (i, tuple(jnp.zeros((1, t), F32) for _ in range(hg)), tuple(jnp.zeros((d, t), F32) for _ in range(hg)),
                jnp.float32(0.0))
        _, _, accs, _ = lax.while_loop(cond, step, init)
        for g in range(hg):
            o_ref[g] = _mx(accs[g])

    return _call(
        body, (k_hm, qt_hm, vt_hm, later),
        name=name,
        grid=(h // hg, nb),
        in_specs=[
            pl.BlockSpec((hg, nb, d, t), lambda hh, i: (hh, 0, 0, 0)),
            pl.BlockSpec((hg, d, t), lambda hh, i: (hh, 0, i)),
            pl.BlockSpec((hg, nb, d, t), lambda hh, i: (hh, 0, 0, 0)),
            pl.BlockSpec((t, t), lambda hh, i: (0, 0)),
        ],
        out_specs=[pl.BlockSpec((hg, d, t), lambda hh, i: (hh, 0, i)),
                   pl.BlockSpec((hg, None, nbp, t), lambda hh, i: (hh, i, 0, 0))],
        out_shape=[jax.ShapeDtypeStruct((h, d, s), MXU_DTYPE), jax.ShapeDtypeStruct((h, nb, nbp, t), F32)],
        rider=rider,
    )


def _heads_to_rows(ref):
    return _mx(jnp.concatenate([ref[hh].astype(F32) for hh in range(HEADS)], axis=0).T)


def _merge_fwd(ys, p, b_gate, w_bo, w_o, l, x, *, gate_col0, name, tm=512):
    s, d = x.shape
    _, nbr, c, _ = w_bo.shape
    tm = _tile(s, tm)

    def body(ya_ref, yb_ref, yct_ref, g0_ref, g1_ref, g2_ref, bg_ref, wbo_ref, wo_ref, x_ref, x1_ref, m_ref):
        merged = jnp.zeros((tm, d), F32)
        branches = ((ya_ref[...], g0_ref), (yb_ref[...], g1_ref), (_heads_to_rows(yct_ref), g2_ref))
        for i, (y, g_ref) in enumerate(branches):
            gate = _sigmoid(_f32(g_ref) + bg_ref[:, i * d:(i + 1) * d])
            merged = merged + gate * _dot(y, wbo_ref[i])
        mb = _mx(merged)
        m_ref[...] = mb
        x1_ref[...] = x_ref[...] + _dot(mb, wo_ref[...])

    yblk = pl.BlockSpec((tm, c), lambda i: (i, 0))
    tblk = pl.BlockSpec((HEADS, HEAD_DIM, tm), lambda i: (0, 0, i))
    gblk = lambda k: pl.BlockSpec((tm, d), lambda i: (i, gate_col0 + k))
    xblk = pl.BlockSpec((tm, d), lambda i: (i, 0))
    return pl.pallas_call(
        body,
        name=name,
        grid=(s // tm,),
        in_specs=[
            yblk, yblk, tblk, gblk(0), gblk(1), gblk(2),
            pl.BlockSpec((1, nbr * d), lambda i: (0, 0)),
            _resident((None, nbr, c, d), lambda i: (l, 0, 0, 0)),
            _resident((None, d, d), lambda i: (l, 0, 0)),
            xblk,
        ],
        out_specs=[xblk, xblk],
        out_shape=[jax.ShapeDtypeStruct((s, d), F32), jax.ShapeDtypeStruct((s, d), MXU_DTYPE)],
        compiler_params=_params(),
    )(*ys, p, p, p, b_gate, w_bo, w_o, x)


def _ffn_down(gu, w_down, l, x, *, name, tm=512):
    s, d = x.shape
    f = w_down.shape[1]
    tm = _tile(s, tm)

    def body(g_ref, u_ref, w_ref, x_ref, o_ref):
        g = _f32(g_ref)
        act = _mx(g * _sigmoid(g) * _f32(u_ref))
        o_ref[...] = x_ref[...] + _dot(act, w_ref[...])

    return pl.pallas_call(
        body,
        name=name,
        grid=(s // tm,),
        in_specs=[
            pl.BlockSpec((tm, f), lambda i: (i, 0)),
            pl.BlockSpec((tm, f), lambda i: (i, 1)),
            _resident((None, f, d), lambda i: (l, 0, 0)),
            pl.BlockSpec((tm, d), lambda i: (i, 0)),
        ],
        out_specs=pl.BlockSpec((tm, d), lambda i: (i, 0)),
        out_shape=jax.ShapeDtypeStruct((s, d), F32),
        compiler_params=_params(),
    )(gu, gu, w_down, x)


def _loss_head(y, target, *, name, tm=512):
    s, d = y.shape
    tm = _tile(s, tm)

    def body(y_ref, t_ref, l_ref, dy_ref):
        @pl.when(pl.program_id(0) == 0)
        def _():
            l_ref[...] = jnp.zeros_like(l_ref)

        err = y_ref[...] - t_ref[...]
        dy_ref[...] = err * (1.0 / d)
        sq = _row_fold(err * err)
        part = sq[:, 0:128]
        for k in range(1, d // 128):
            part = part + sq[:, k * 128:(k + 1) * 128]
        l_ref[...] += part * (0.5 / d)

    blk = pl.BlockSpec((tm, d), lambda i: (i, 0))
    return pl.pallas_call(
        body,
        name=name,
        grid=(s // tm,),
        in_specs=[blk, blk],
        out_specs=[pl.BlockSpec((8, 128), lambda i: (0, 0)), blk],
        out_shape=[jax.ShapeDtypeStruct((8, 128), F32), jax.ShapeDtypeStruct((s, d), F32)],
        compiler_params=_params(),
    )(y, target)


def _matmul_tn(a, b, slot, n_slots, into, *, name, t1=1024, tn=None, tm=1024, col0=0, n_total=None, a_transposed=False):
    m, k1 = a.shape[::-1] if a_transposed else a.shape
    n = b.shape[1]
    n_total = n_total or n
    t1 = _tile(k1, t1)
    tn = tn or n
    tm = _tile(m, tm)
    steps = m // tm
    assert n % tn == 0 and col0 % tn == 0
    cb0 = col0 // tn

    def body(a_ref, b_ref, *refs):
        o_ref = refs[-1]

        @pl.when(pl.program_id(2) == 0)
        def _():
            o_ref[...] = jnp.zeros_like(o_ref)

        o_ref[...] += (_dot if a_transposed else _dot_tn)(a_ref[...], b_ref[...])

    return pl.pallas_call(
        body,
        name=name,
        grid=(k1 // t1, n // tn, steps),
        in_specs=[
            pl.BlockSpec((t1, tm), lambda i, j, k: (i, k)) if a_transposed else pl.BlockSpec((tm, t1), lambda i, j, k: (k, i)),
            pl.BlockSpec((tm, tn), lambda i, j, k: (k, j)),
        ] + ([] if into is None else [ANY]),
        out_specs=pl.BlockSpec((None, t1, tn), lambda i, j, k: (slot, i, cb0 + j)),
        out_shape=jax.ShapeDtypeStruct((n_slots, k1, n_total), F32),
        input_output_aliases={} if into is None else {2: 0},
        compiler_params=_params(),
    )(a, b, *([] if into is None else [into]))


def _matmul_nt_normbwd(pieces, w, l, x, g, dres, *, name, tm=512, tk=1024, rider=None):
    s, d = x.shape
    k = w.shape[2]
    tm = _tile(s, tm)
    n = len(pieces)
    assert sum(pc.shape[1] for pc in pieces) == k

    def body(*refs):
        piece_refs = refs[:n]
        w_ref, x_ref, g_ref, dres_ref, dx_ref, dg_ref = refs[n:]

        @pl.when(pl.program_id(0) == 0)
        def _():
            dg_ref[...] = jnp.zeros_like(dg_ref)

        dh = jnp.zeros((tm, d), F32)
        off = 0
        for pc, pc_ref in zip(pieces, piece_refs):
            width = pc.shape[1]
            step = _tile(width, tk)
            for c0 in range(0, width, step):
                dh = dh + _dot_nt(pc_ref[:, c0:c0 + step], w_ref[:, off + c0:off + c0 + step])
            off += width
        xf = x_ref[...]
        r = lax.rsqrt(jnp.mean(xf * xf, axis=-1, keepdims=True) + EPS)
        y = xf * r
        dy = dh * g_ref[...]
        dx_ref[...] = dres_ref[...] + r * (dy - y * jnp.mean(dy * y, axis=-1, keepdims=True))
        dg_ref[...] += _row_fold(dh * y)

    xblk = pl.BlockSpec((tm, d), lambda i: (i, 0))
    return _call(
        body, (*pieces, w, x, g, dres),
        name=name,
        grid=(s // tm,),
        in_specs=[pl.BlockSpec((tm, pc.shape[1]), lambda i: (i, 0)) for pc in pieces] + [
            _resident((None, d, k), lambda i: (l, 0, 0)),
            xblk,
            pl.BlockSpec((1, d), lambda i: (0, 0)),
            xblk,
        ],
        out_specs=[xblk, pl.BlockSpec((8, d), lambda i: (0, 0))],
        out_shape=[jax.ShapeDtypeStruct((s, d), F32), jax.ShapeDtypeStruct((8, d), F32)],
        rider=rider,
    )


def _ffn_bwd(dx, gu, w_down, l, *, name, tm=512, rider=None):
    s, d = dx.shape
    f = w_down.shape[1]
    tm = _tile(s, tm)

    def body(dx_ref, g_ref, u_ref, w_ref, dgu_ref, act_ref, dxb_ref):
        dxb = _mx(dx_ref[...])
        dxb_ref[...] = dxb
        dact = _dot_nt(dxb, w_ref[...])
        g = _f32(g_ref)
        u = _f32(u_ref)
        sg = _sigmoid(g)
        silu = g * sg
        act_ref[...] = _mx(silu * u)
        dgu_ref[:, 0:f] = _mx(dact * u * (sg * (1.0 + g * (1.0 - sg))))
        dgu_ref[:, f:2 * f] = _mx(dact * silu)

    fblk = lambda col: pl.BlockSpec((tm, f), lambda i: (i, col))
    dblk = pl.BlockSpec((tm, d), lambda i: (i, 0))
    return _call(
        body, (dx, gu, gu, w_down),
        name=name,
        grid=(s // tm,),
        in_specs=[dblk, fblk(0), fblk(1), _resident((None, f, d), lambda i: (l, 0, 0))],
        out_specs=[pl.BlockSpec((tm, 2 * f), lambda i: (i, 0)), fblk(0), dblk],
        out_shape=[
            jax.ShapeDtypeStruct((s, 2 * f), MXU_DTYPE),
            jax.ShapeDtypeStruct((s, f), MXU_DTYPE),
            jax.ShapeDtypeStruct((s, d), MXU_DTYPE),
        ],
        rider=rider,
    )


def _merge_bwd(dx, ys, p, b_gate, w_bo, w_o, l, *, gate_col0, name, tm=512):
    s, d = dx.shape
    _, nbr, c, _ = w_bo.shape
    tm = _tile(s, tm)

    def body(dx_ref, ya_ref, yb_ref, yct_ref, g0_ref, g1_ref, g2_ref, bg_ref, wbo_ref, wo_ref,
             dgate_ref, dya_ref, dyb_ref, dyct_ref, dd0_ref, dd1_ref, dd2_ref, dxb_ref, dbg_ref):
        @pl.when(pl.program_id(0) == 0)
        def _():
            dbg_ref[...] = jnp.zeros_like(dbg_ref)

        dxb = _mx(dx_ref[...])
        dxb_ref[...] = dxb
        dmerged = _dot_nt(dxb, wo_ref[...])
        branches = ((ya_ref[...], g0_ref, dd0_ref), (yb_ref[...], g1_ref, dd1_ref), (_heads_to_rows(yct_ref), g2_ref, dd2_ref))
        dys = []
        for i, (y, g_ref, dd_ref) in enumerate(branches):
            cols = slice(i * d, (i + 1) * d)
            gate = _sigmoid(_f32(g_ref) + bg_ref[:, cols])
            yd = _dot(y, wbo_ref[i])
            dyd = _mx(dmerged * gate)
            dd_ref[...] = dyd
            dys.append(_dot_nt(dyd, wbo_ref[i]))
            dpre = dmerged * yd * gate * (1.0 - gate)
            dgate_ref[:, cols] = _mx(dpre)
            dbg_ref[:, cols] += _row_fold(dpre)
        dya_ref[...] = dys[0]
        dyb_ref[...] = dys[1]
        dyc_t = _mx(dys[2].T)
        for hh in range(HEADS):
            dyct_ref[hh] = dyc_t[hh * HEAD_DIM:(hh + 1) * HEAD_DIM, :]

    yblk = pl.BlockSpec((tm, c), lambda i: (i, 0))
    tblk = pl.BlockSpec((HEADS, HEAD_DIM, tm), lambda i: (0, 0, i))
    gblk = lambda k: pl.BlockSpec((tm, d), lambda i: (i, gate_col0 + k))
    dblk = pl.BlockSpec((tm, d), lambda i: (i, 0))
    return pl.pallas_call(
        body,
        name=name,
        grid=(s // tm,),
        in_specs=[
            dblk, yblk, yblk, tblk, gblk(0), gblk(1), gblk(2),
            pl.BlockSpec((1, nbr * d), lambda i: (0, 0)),
            _resident((None, nbr, c, d), lambda i: (l, 0, 0, 0)),
            _resident((None, d, d), lambda i: (l, 0, 0)),
        ],
        out_specs=[pl.BlockSpec((tm, nbr * d), lambda i: (i, 0)), yblk, yblk, tblk, dblk, dblk, dblk, dblk,
                   pl.BlockSpec((8, nbr * d), lambda i: (0, 0))],
        out_shape=[jax.ShapeDtypeStruct((s, nbr * d), MXU_DTYPE)] + [jax.ShapeDtypeStruct((s, c), F32)] * 2
        + [jax.ShapeDtypeStruct((HEADS, HEAD_DIM, s), MXU_DTYPE)]
        + [jax.ShapeDtypeStruct((s, d), MXU_DTYPE)] * 4 + [jax.ShapeDtypeStruct((8, nbr * d), F32)],
        compiler_params=_params(),
    )(dx, *ys, p, p, p, b_gate, w_bo, w_o)


def _conv_bwd(p, dya, conv_w, conv_b, *, name, tm=512):
    s = p.shape[0]
    c = conv_w.shape[1]
    tm = _tile(s, tm)
    hb = tm // HALO
    last = s // tm - 1

    def body(bg_ref, cg_ref, xa_ref, cgp_ref, xap_ref, dy_ref, dyn_ref, bgn_ref, w_ref, b_ref,
             dp_ref, dw_ref):
        i = pl.program_id(0)

        @pl.when(i == 0)
        def _():
            dw_ref[...] = jnp.zeros_like(dw_ref)

        cg = _f32(cg_ref)
        xa = _f32(xa_ref)
        u = cg * xa
        prev = jnp.where(i == 0, 0.0, _f32(cgp_ref) * _f32(xap_ref))
        um1, um2 = _conv_taps(u, prev)
        w = w_ref[...]
        y = b_ref[...] + w[0:1, :] * um2 + w[1:2, :] * um1 + w[2:3, :] * u
        dya = dy_ref[...]
        dp_ref[:, 0:c] = _mx(dya * y)
        dyv = dya * _f32(bg_ref)
        nxt = jnp.where(i == last, 0.0, dyn_ref[...] * _f32(bgn_ref))
        row = lax.broadcasted_iota(jnp.int32, dyv.shape, 0)
        dp1 = jnp.where(row == tm - 1, nxt[0:1, :], pltpu.roll(dyv, tm - 1, 0))
        dp2 = pltpu.roll(dyv, tm - 2, 0)
        dp2 = jnp.where(row == tm - 2, nxt[0:1, :], jnp.where(row == tm - 1, nxt[1:2, :], dp2))
        du = w[2:3, :] * dyv + w[1:2, :] * dp1 + w[0:1, :] * dp2
        dp_ref[:, c:2 * c] = _mx(du * xa)
        dp_ref[:, 2 * c:3 * c] = _mx(du * cg)
        dw_ref[0] += _row_fold(dyv * um2)
        dw_ref[1] += _row_fold(dyv * um1)
        dw_ref[2] += _row_fold(dyv * u)
        dw_ref[3] += _row_fold(dyv)

    blk = lambda col: pl.BlockSpec((tm, c), lambda i: (i, col))
    halo = lambda col: pl.BlockSpec((HALO, c), lambda i: (jnp.maximum(i * hb - 1, 0), col))
    nhalo = lambda col: pl.BlockSpec((HALO, c), lambda i: (jnp.minimum((i + 1) * hb, s // HALO - 1), col))
    return pl.pallas_call(
        body,
        name=name,
        grid=(s // tm,),
        in_specs=[blk(0), blk(1), blk(2), halo(1), halo(2), blk(0), nhalo(0), nhalo(0),
                  pl.BlockSpec((CONV_K, c), lambda i: (0, 0)), pl.BlockSpec((1, c), lambda i: (0, 0))],
        out_specs=[pl.BlockSpec((tm, 3 * c), lambda i: (i, 0)), pl.BlockSpec((4, 8, c), lambda i: (0, 0, 0))],
        out_shape=[jax.ShapeDtypeStruct((s, 3 * c), MXU_DTYPE), jax.ShapeDtypeStruct((4, 8, c), F32)],
        compiler_params=_params(),
    )(p, p, p, p, p, dya, dya, p, conv_w, conv_b)


def _sgu_bwd(p, dyb, ln_g, ln_b, wtril, wtril_t, bias_full, *, col0, name, tm=512):
    s = p.shape[0]
    c = ln_g.shape[1]
    groups = c // CHUNK
    tm = _tile(s, tm)

    def body(u_ref, v_ref, dy_ref, g_ref, b_ref, w_ref, wt_ref, bias_ref, duv_ref, dln_ref, dw_ref, dbias_ref, dvn_ref):
        @pl.when(pl.program_id(0) == 0)
        def _():
            dln_ref[...] = jnp.zeros_like(dln_ref)
            dw_ref[...] = jnp.zeros_like(dw_ref)
            dbias_ref[...] = jnp.zeros_like(dbias_ref)

        sv = _f32(v_ref)
        xhat, r = _layernorm_stats(_gelu(sv))
        vb = _mx(xhat * g_ref[...] + b_ref[...])
        for n in range(tm // CHUNK):
            rows = slice(n * CHUNK, (n + 1) * CHUNK)
            for gi in range(groups):
                cols = slice(gi * CHUNK, (gi + 1) * CHUNK)
                su = u_ref[rows, cols].astype(F32)
                dy = dy_ref[rows, cols]
                vblk = vb[rows, cols]
                mixed = _dot(w_ref[gi], vblk) + bias_ref[:, cols]
                duv_ref[rows, cols] = _mx(dy * mixed * _gelu_grad(su))
                dmixed = dy * _gelu(su)
                dmb = _mx(dmixed)
                dvn_ref[rows, cols] = _dot(wt_ref[gi], dmb)
                dw_ref[gi] += _dot_nt(dmb, vblk)
                dbias_ref[:, cols] += dmixed
        dvn = dvn_ref[...]
        dln_ref[0] += _row_fold(dvn * xhat)
        dln_ref[1] += _row_fold(dvn)
        dxh = dvn * g_ref[...]
        dgv = r * (dxh - jnp.mean(dxh, axis=-1, keepdims=True) - xhat * jnp.mean(dxh * xhat, axis=-1, keepdims=True))
        duv_ref[:, c:2 * c] = _mx(dgv * _gelu_grad(sv))

    blk = lambda col: pl.BlockSpec((tm, c), lambda i: (i, col))
    vec = pl.BlockSpec((1, c), lambda i: (0, 0))
    wspec = pl.BlockSpec((groups, CHUNK, CHUNK), lambda i: (0, 0, 0))
    return pl.pallas_call(
        body,
        name=name,
        grid=(s // tm,),
        in_specs=[blk(col0), blk(col0 + 1), blk(0), vec, vec, wspec, wspec, pl.BlockSpec((CHUNK, c), lambda i: (0, 0))],
        out_specs=[pl.BlockSpec((tm, 2 * c), lambda i: (i, 0)), pl.BlockSpec((2, 8, c), lambda i: (0, 0, 0)), wspec,
                   pl.BlockSpec((CHUNK, c), lambda i: (0, 0))],
        out_shape=[jax.ShapeDtypeStruct((s, 2 * c), MXU_DTYPE),
                   jax.ShapeDtypeStruct((2, 8, c), F32), jax.ShapeDtypeStruct((groups, CHUNK, CHUNK), F32),
                   jax.ShapeDtypeStruct((CHUNK, c), F32)],
        scratch_shapes=[pltpu.VMEM((tm, c), F32)],
        compiler_params=_params(),
    )(p, p, dyb, ln_g, ln_b, wtril, wtril_t, bias_full)


def _att_bwd(kt_hm, vt_hm, qt_hm, dot_hm, runs, later, earlier, *, name, hg=4, rider=None):
    h, nb, d, t = kt_hm.shape
    s = nb * t
    nbp = runs.shape[2]
    scale = 1.0 / math.sqrt(d)
    assert h % hg == 0

    def body(kt_ref, vt_ref, qt_ref, dot_ref, runs_ref, later_ref, earlier_ref, dqt_ref, dkt_ref, dvt_ref):
        i = pl.program_id(1)

        @pl.when(i == 0)
        def _():
            dkt_ref[...] = jnp.zeros_like(dkt_ref)
            dvt_ref[...] = jnp.zeros_like(dvt_ref)

        q_ts = [qt_ref[g] for g in range(hg)]
        do_ts = [dot_ref[g] for g in range(hg)]

        best = runs_ref[0]
        for g in range(1, hg):
            best = jnp.maximum(best, runs_ref[g])
        row = lax.broadcasted_iota(jnp.int32, (nbp, 1), 0)
        counts = jnp.logical_and(jnp.max(best, axis=1, keepdims=True) > EXP_UNDERFLOW, row < i)
        seen = jnp.sum(counts.astype(jnp.int32))
        zeros_row = tuple(jnp.zeros((1, t), F32) for _ in range(hg))

        def up(j, carry):
            gsums, dqts = carry
            mask = _att_mask(t, j * t, i * t)
            heads = range(hg)
            zs, es, lbs, l1ms = _att_blocks([kt_ref[g, j] for g in heads], q_ts, scale, mask)
            afters = _dot_split_each(later_ref[...], l1ms)
            das = [_dot_tn(vt_ref[g, j], do_ts[g]) for g in heads]
            weights = [jnp.where(mask, jnp.exp(lbs[g] + afters[g] + _row_get(runs_ref, g, j)), 0.0) for g in heads]
            grs = [das[g] * weights[g] for g in heads]
            gbefores = _dot_split_each(earlier_ref[...], grs)
            dzs = []
            for g in heads:
                inv = 1.0 / (1.0 + es[g])
                pos = zs[g] >= 0.0
                beta = jnp.where(pos, inv, es[g] * inv)
                omb = jnp.where(pos, es[g] * inv, inv)
                dzs.append(_mx(jnp.where(mask, grs[g] * omb - (gbefores[g] + gsums[g]) * beta, 0.0) * scale))
            new_dqts = [dqts[g] + _dot(kt_ref[g, j], dzs[g]) for g in heads]
            for g in heads:
                dkt_ref[g, j] += _dot_nt(q_ts[g], dzs[g])
            for g in heads:
                dvt_ref[g, j] += _dot_nt(do_ts[g], _mx(weights[g]))
            new_gsums = [gsums[g] + jnp.sum(grs[g], axis=0, keepdims=True) for g in heads]
            return tuple(new_gsums), tuple(new_dqts)

        _, dqts = lax.fori_loop(i - seen, i + 1, up, (zeros_row, tuple(jnp.zeros((d, t), F32) for _ in range(hg))))
        for g in range(hg):
            dqt_ref[g] = dqts[g]

    whole = pl.BlockSpec((hg, nb, d, t), lambda hh, i: (hh, 0, 0, 0))
    cols = pl.BlockSpec((hg, d, t), lambda hh, i: (hh, 0, i))
    tri = pl.BlockSpec((t, t), lambda hh, i: (0, 0))
    return _call(
        body, (kt_hm, vt_hm, qt_hm, dot_hm, runs, later, earlier),
        name=name,
        grid=(h // hg, nb),
        in_specs=[whole, whole, cols, cols, pl.BlockSpec((hg, None, nbp, t), lambda hh, i: (hh, i, 0, 0)), tri, tri],
        out_specs=[cols, whole, whole],
        out_shape=[jax.ShapeDtypeStruct((h, d, s), F32), jax.ShapeDtypeStruct((h, nb, d, t), F32),
                   jax.ShapeDtypeStruct((h, nb, d, t), F32)],
        rider=rider,
    )


def _qkv_bwd(p, dqt, dkt, dvt, qg, kg, headavg, *, col0, name, tm=512):
    s = p.shape[0]
    c = qg.shape[1]
    tm = _tile(s, tm)

    t = ATT_BLOCK
    assert tm % t == 0

    def rows_of(blocked_ref):
        parts = []
        for b in range(tm // t):
            cols = jnp.concatenate([blocked_ref[hh, b] for hh in range(HEADS)], axis=0)
            parts.append(cols.T)
        return jnp.concatenate(parts, axis=0)

    def body(q_ref, k_ref, dqt_ref, dkt_ref, dvt_ref, qg_ref, kg_ref, avg_ref, dqkv_ref, dg_ref):
        @pl.when(pl.program_id(0) == 0)
        def _():
            dg_ref[...] = jnp.zeros_like(dg_ref)

        dqn = jnp.concatenate([dqt_ref[hh] for hh in range(HEADS)], axis=0).T
        for n, (src, dn, g_ref) in enumerate(((q_ref, dqn, qg_ref), (k_ref, rows_of(dkt_ref), kg_ref))):
            xf = _f32(src)
            r = lax.rsqrt(_head_mean(xf * xf, avg_ref[...]) + EPS)
            y = xf * r
            dy = dn * g_ref[...]
            dqkv_ref[:, n * c:(n + 1) * c] = _mx(r * (dy - y * _head_mean(dy * y, avg_ref[...])))
            dg_ref[n] += _row_fold(dn * y)
        dqkv_ref[:, 2 * c:3 * c] = _mx(rows_of(dvt_ref))

    blk = lambda col: pl.BlockSpec((tm, c), lambda i: (i, col))
    vec = pl.BlockSpec((1, c), lambda i: (0, 0))
    blocked = pl.BlockSpec((HEADS, tm // t, HEAD_DIM, t), lambda i: (0, i, 0, 0))
    return pl.pallas_call(
        body,
        name=name,
        grid=(s // tm,),
        in_specs=[blk(col0), blk(col0 + 1), pl.BlockSpec((HEADS, HEAD_DIM, tm), lambda i: (0, 0, i)), blocked, blocked,
                  vec, vec, pl.BlockSpec((c, c), lambda i: (0, 0))],
        out_specs=[pl.BlockSpec((tm, 3 * c), lambda i: (i, 0)), pl.BlockSpec((2, 8, c), lambda i: (0, 0, 0))],
        out_shape=[jax.ShapeDtypeStruct((s, 3 * c), MXU_DTYPE), jax.ShapeDtypeStruct((2, 8, c), F32)],
        compiler_params=_params(),
    )(p, p, dqt, dkt, dvt, qg, kg, headavg)


def _place():
    x, y, c = lax.axis_index("x"), lax.axis_index("y"), lax.axis_index("c")
    chips = [(1 - x, y), (x, 1 - y), (1 - x, 1 - y)]
    return x, y, c, chips


def _shard_of(ref, axis, chip, width):
    idx = [slice(None)] * len(ref.shape)
    idx[axis] = pl.ds(chip * width, width)
    return ref.at[tuple(idx)]


def _place_shard(w, layer, axis, chip_idx, dtype, *, name):
    rows, cols = _as_rows(w.shape[1:])
    tr = _row_tile(rows, cols, 2)
    if axis == len(w.shape) - 2:
        out_shape = (1, rows, cols * N_CHIPS)
        out_spec = pl.BlockSpec((None, tr, cols), lambda i, j_ref: (0, i, j_ref[0]))
    else:
        assert axis == 0
        per = rows // tr
        out_shape = (1, rows * N_CHIPS, cols)
        out_spec = pl.BlockSpec((None, tr, cols), lambda i, j_ref: (0, j_ref[0] * per + i, 0))
    full = [1, *w.shape[1:]]
    full[1 + axis] *= N_CHIPS

    def body(j_ref, w_ref, o_ref):
        o_ref[...] = w_ref[...].astype(dtype)

    out = pl.pallas_call(
        body,
        name=name,
        grid_spec=pltpu.PrefetchScalarGridSpec(
            num_scalar_prefetch=1,
            grid=(rows // tr,),
            in_specs=[pl.BlockSpec((None, tr, cols), lambda i, j_ref: (layer, i, 0))],
            out_specs=out_spec,
        ),
        out_shape=jax.ShapeDtypeStruct(out_shape, dtype),
        compiler_params=_params(),
    )(chip_idx, w.reshape(w.shape[0], rows, cols))
    return out.reshape(full)


def _gather_riders(placed, axes, layer):
    n = len(placed)
    widths = [pa.shape[1 + ax] // N_CHIPS for pa, ax in zip(placed, axes)]

    def copies(refs, sems):
        x, y, c, chips = _place()

        def block(a, chip):
            return _shard_of(refs[a].at[0], axes[a], chip, widths[a])

        def over_ici(a, k, chip):
            return pltpu.make_async_remote_copy(
                src_ref=block(a, chip), dst_ref=block(a, chip), send_sem=sems[0].at[a, k], recv_sem=sems[1].at[a, k],
                device_id=(*chips[k], c), device_id_type=MESH)

        def to_sibling(a, k):
            cx, cy = chips[k]
            return pltpu.make_async_remote_copy(
                src_ref=block(a, 2 * cx + cy), dst_ref=block(a, 2 * cx + cy), send_sem=sems[0].at[a, k],
                recv_sem=sems[1].at[a, k], device_id=(x, y, 1 - c), device_id_type=MESH)

        return 2 * x + y, c, chips, over_ici, to_sibling

    def ici_start(refs, sems):
        me, c, _, over_ici, _ = copies(refs, sems)

        @pl.when(c == layer)
        def _():
            for a in range(n):
                for k in range(3):
                    over_ici(a, k, me).start()

    def ici_finish(refs, sems):
        me, c, chips, over_ici, _ = copies(refs, sems)

        @pl.when(c == layer)
        def _():
            for a in range(n):
                for k in range(3):
                    cx, cy = chips[k]
                    over_ici(a, k, 2 * cx + cy).wait_recv()
            for a in range(n):
                for k in range(3):
                    over_ici(a, k, me).wait_send()

    def d2d_start(refs, sems):
        _, c, _, _, to_sibling = copies(refs, sems)

        @pl.when(c == layer)
        def _():
            for a in range(n):
                for k in range(3):
                    to_sibling(a, k).start()

    def d2d_finish(refs, sems):
        _, c, _, _, to_sibling = copies(refs, sems)

        @pl.when(c == layer)
        def _():
            for a in range(n):
                for k in range(3):
                    to_sibling(a, k).wait_send()

        @pl.when(c != layer)
        def _():
            for a in range(n):
                for k in range(3):
                    to_sibling(a, k).wait_recv()

    def both_start(refs, sems):
        ici_start(refs, sems[:2])

    def both_finish(refs, sems):
        me, c, chips, over_ici, _ = copies(refs, sems[:2])
        to_sibling = copies(refs, sems[2:])[4]

        @pl.when(c == layer)
        def _():
            for a in range(n):
                for k in range(3):
                    cx, cy = chips[k]
                    over_ici(a, k, 2 * cx + cy).wait_recv()
                    to_sibling(a, k).start()
            for a in range(n):
                for k in range(3):
                    over_ici(a, k, me).wait_send()
                    to_sibling(a, k).wait_send()

        @pl.when(c != layer)
        def _():
            for a in range(n):
                for k in range(3):
                    to_sibling(a, k).wait_recv()

    arrays, sems = tuple(placed), ((n, 3), (n, 3))
    return (_Rider(arrays, sems, ici_start, ici_finish), _Rider(arrays, sems, d2d_start, d2d_finish),
            _Rider(arrays, sems * 2, both_start, both_finish))


def _join_riders(*riders):
    def each(which):
        def run(refs, sems):
            for r in riders:
                getattr(r, which)(refs[:len(r.arrays)], sems[:len(r.sems)])
                refs, sems = refs[len(r.arrays):], sems[len(r.sems):]
        return run

    return _Rider(sum((r.arrays for r in riders), ()), sum((r.sems for r in riders), ()), each("start"), each("finish"))


def _exchange_rider(grads, others, layer):
    n = len(grads)

    def copy(refs, sems, a):
        x, y, c, _ = _place()
        return pltpu.make_async_remote_copy(
            src_ref=refs[a], dst_ref=refs[n + a], send_sem=sems[0].at[a], recv_sem=sems[1].at[a],
            device_id=(x, y, 1 - c), device_id_type=MESH)

    def start(refs, sems):
        @pl.when(lax.axis_index("c") != layer)
        def _():
            for a in range(n):
                copy(refs, sems, a).start()

    def finish(refs, sems):
        @pl.when(lax.axis_index("c") != layer)
        def _():
            for a in range(n):
                copy(refs, sems, a).wait_send()

        @pl.when(lax.axis_index("c") == layer)
        def _():
            for a in range(n):
                copy(refs, sems, a).wait_recv()

    return _Rider((*grads, *others), ((n,), (n,)), start, finish)


def _scatter_rider(wires, landeds, axes, layer):
    n = len(wires)
    widths = [wr.shape[ax] // N_CHIPS for wr, ax in zip(wires, axes)]

    def copies(refs, sems):
        _, _, c, chips = _place()
        return [pltpu.make_async_remote_copy(
            src_ref=_shard_of(refs[a], axes[a], 2 * cx + cy, widths[a]), dst_ref=refs[n + a].at[k],
            send_sem=sems[0].at[a, k], recv_sem=sems[1].at[a, k], device_id=(cx, cy, c), device_id_type=MESH)
            for a in range(n) for k, (cx, cy) in enumerate(chips)]

    def start(refs, sems):
        @pl.when(lax.axis_index("c") == layer)
        def _():
            for cp in copies(refs, sems):
                cp.start()

    def finish(refs, sems):
        @pl.when(lax.axis_index("c") == layer)
        def _():
            for cp in copies(refs, sems):
                cp.wait()

    return _Rider((*wires, *landeds), ((n, 3), (n, 3)), start, finish)


def _share_rider(finals, layer):
    n = len(finals)

    def copy(refs, sems, a):
        x, y, c, _ = _place()
        return pltpu.make_async_remote_copy(
            src_ref=refs[a].at[layer], dst_ref=refs[a].at[layer], send_sem=sems[0].at[a], recv_sem=sems[1].at[a],
            device_id=(x, y, 1 - c), device_id_type=MESH)

    def start(refs, sems):
        @pl.when(lax.axis_index("c") == layer)
        def _():
            for a in range(n):
                copy(refs, sems, a).start()

    def finish(refs, sems):
        @pl.when(lax.axis_index("c") == layer)
        def _():
            for a in range(n):
                copy(refs, sems, a).wait_send()

        @pl.when(lax.axis_index("c") != layer)
        def _():
            for a in range(n):
                copy(refs, sems, a).wait_recv()

    return _Rider(tuple(finals), ((n,), (n,)), start, finish)


def _small_view(shape):
    size = math.prod(shape)
    return (size // 128, 128) if size % 1024 == 0 else (shape[0], size // shape[0])


def _all_reduce_small(parts, *, name):
    n = len(parts)
    views = [_small_view(a.shape) for a in parts]

    def body(*refs):
        ins, outs, slots = refs[:n], refs[n:2 * n], refs[2 * n:3 * n]
        send_sem, recv_sem = refs[3 * n:]
        x, y, c, _ = _place()
        me = 4 * x + 2 * y + c
        copies = []
        for a in range(n):
            slots[a][me] = ins[a][...]
            for k in range(1, N_DEV):
                peer = (x ^ (k >> 2), y ^ ((k >> 1) & 1), c ^ (k & 1))
                copies.append(pltpu.make_async_remote_copy(
                    src_ref=ins[a], dst_ref=slots[a].at[me], send_sem=send_sem.at[a, k - 1],
                    recv_sem=recv_sem.at[a, k - 1], device_id=peer, device_id_type=MESH))
        for cp in copies:
            cp.start()
        for a in range(n):
            for k in range(1, N_DEV):
                pltpu.make_async_remote_copy(
                    src_ref=ins[a], dst_ref=slots[a].at[me ^ k], send_sem=send_sem.at[a, k - 1],
                    recv_sem=recv_sem.at[a, k - 1], device_id=(x, y, c), device_id_type=MESH).wait()
        for a in range(n):
            total = slots[a][0]
            for dev in range(1, N_DEV):
                total = total + slots[a][dev]
            outs[a][...] = total

    vmem = pl.BlockSpec(memory_space=pltpu.VMEM)
    outs = pl.pallas_call(
        body,
        name=name,
        in_specs=[vmem] * n,
        out_specs=[vmem] * n,
        out_shape=[jax.ShapeDtypeStruct(view, F32) for view in views],
        scratch_shapes=[pltpu.VMEM((N_DEV, *view), F32) for view in views]
        + [pltpu.SemaphoreType.DMA((n, N_DEV - 1)), pltpu.SemaphoreType.DMA((n, N_DEV - 1))],
        compiler_params=_params(),
    )(*[a.reshape(view) for a, view in zip(parts, views)])
    return [o.reshape(a.shape) for o, a in zip(outs, parts)]


def _as_rows(shape):
    cols = shape[-1]
    return math.prod(shape[:-1]), cols


ELEMENTWISE_VMEM = 24 * 1024 * 1024


def _row_tile(rows, cols, n_arrays, sublanes=8):
    cap = ELEMENTWISE_VMEM // (n_arrays * 2 * 4 * cols)
    best = None
    for t in range(sublanes, min(rows, cap) + 1, sublanes):
        if rows % t == 0:
            best = t
    assert best is not None, (rows, cols)
    return best


def _pair_sum(g, other, active, *, name):
    rows, cols = _as_rows(g.shape)
    tr = _row_tile(rows, cols, 4, sublanes=16)

    def body(a_ref, g_ref, o_ref, out_ref, wire_ref):
        @pl.when(a_ref[0] == 1)
        def _():
            total = g_ref[...] + o_ref[...]
            out_ref[...] = total
            wire_ref[...] = total.astype(WIRE_DTYPE)

    blk = pl.BlockSpec((tr, cols), lambda i, a_ref: (i * a_ref[0], 0))
    out, wire = pl.pallas_call(
        body,
        name=name,
        grid_spec=pltpu.PrefetchScalarGridSpec(
            num_scalar_prefetch=1, grid=(rows // tr,), in_specs=[blk, blk], out_specs=[blk, blk]),
        out_shape=[jax.ShapeDtypeStruct((rows, cols), F32), jax.ShapeDtypeStruct((rows, cols), WIRE_DTYPE)],
        compiler_params=_params(),
    )(active, g.reshape(rows, cols), other.reshape(rows, cols))
    return out.reshape(g.shape), wire.reshape(g.shape)


def _chip_sum(mine, got, axis, chip_idx, active, layer, layers, into, *, name):
    shard_shape = got.shape[1:]
    rows, cols = _as_rows(shard_shape)
    tr = _row_tile(rows, cols, 5, sublanes=16)
    if axis == len(mine.shape) - 1:
        m2 = mine.reshape(rows, cols * N_CHIPS)
        mine_spec = pl.BlockSpec((tr, cols), lambda i, j_ref, a_ref: (i * a_ref[0], j_ref[0]))
    else:
        assert axis == 0
        m2 = mine.reshape(N_CHIPS, rows, cols)
        mine_spec = pl.BlockSpec((None, tr, cols), lambda i, j_ref, a_ref: (j_ref[0], i * a_ref[0], 0))

    def body(j_ref, a_ref, m_ref, got_ref, *refs):
        @pl.when(a_ref[0] == 1)
        def _():
            refs[-1][...] = ((m_ref[...] + got_ref[0].astype(F32)) + got_ref[1].astype(F32)) + got_ref[2].astype(F32)

    out = pl.pallas_call(
        body,
        name=name,
        grid_spec=pltpu.PrefetchScalarGridSpec(
            num_scalar_prefetch=2,
            grid=(rows // tr,),
            in_specs=[mine_spec, pl.BlockSpec((3, tr, cols), lambda i, j_ref, a_ref: (0, i * a_ref[0], 0))]
            + ([] if into is None else [ANY]),
            out_specs=pl.BlockSpec((None, tr, cols), lambda i, j_ref, a_ref: (layer, i * a_ref[0], 0)),
        ),
        out_shape=jax.ShapeDtypeStruct((layers, rows, cols), F32),
        input_output_aliases={} if into is None else {4: 0},
        compiler_params=_params(),
    )(chip_idx, active, m2, got.reshape(3, rows, cols), *([] if into is None else [into.reshape(layers, rows, cols)]))
    return out.reshape((layers, *shard_shape))


class _ReduceChain:
    def __init__(self, grads, layer, layers, chip_idx, core_idx, finals):
        self.grads, self.layer, self.layers, self.chip_idx, self.finals = list(grads), layer, layers, chip_idx, finals
        self.active = (core_idx == layer).astype(jnp.int32)
        self.axes = [BIG_AXIS[n] for n in BIG]
        self.pairs = None

    def exchange(self):
        return _exchange_rider(self.grads, [lax.empty(g.shape, F32) for g in self.grads], self.layer)

    def exchanged(self, arrived):
        n = len(self.grads)
        self.pairs = [_pair_sum(g, o, self.active, name=f"pair_sum_{name}_l{self.layer}")
                      for name, g, o in zip(BIG, arrived[:n], arrived[n:])]

    def scatter(self):
        landeds = []
        for (_, wire), ax in zip(self.pairs, self.axes):
            shard = list(wire.shape)
            shard[ax] //= N_CHIPS
            landeds.append(lax.empty((3, *shard), WIRE_DTYPE))
        return _scatter_rider([wire for _, wire in self.pairs], landeds, self.axes, self.layer)

    def scattered(self, arrived):
        n = len(self.grads)
        into = self.finals or [None] * n
        self.finals = [_chip_sum(pair, landed, ax, self.chip_idx, self.active, self.layer, self.layers, old,
                                 name=f"chip_sum_{name}_l{self.layer}")
                       for name, (pair, _), landed, ax, old in zip(BIG, self.pairs, arrived[n:], self.axes, into)]

    def share(self):
        return _share_rider(self.finals, self.layer)

    def shared(self, arrived):
        self.finals = list(arrived)


def _adamw_update(w_ref, g_ref, m_ref, v_ref, d_ref, nm_ref, nv_ref):
    c1 = 1.0 / (1.0 - ADAM_B1 ** ADAM_STEP)
    c2 = 1.0 / (1.0 - ADAM_B2 ** ADAM_STEP)
    gg = g_ref[...]
    nm = ADAM_B1 * m_ref[...] + (1.0 - ADAM_B1) * gg
    nv = ADAM_B2 * v_ref[...] + (1.0 - ADAM_B2) * (gg * gg)
    nm_ref[...] = nm
    nv_ref[...] = nv
    d_ref[...] = -ADAM_LR * ((nm * c1) / (jnp.sqrt(nv * c2) + ADAM_EPS) + ADAM_WD * w_ref[...])


def _adamw_small(ws, gs, ms, vs, *, name):
    n = len(ws)
    views = [_small_view(a.shape) for a in ws]

    def body(*refs):
        w_r, g_r, m_r, v_r = (refs[k * n:(k + 1) * n] for k in range(4))
        d_r, nm_r, nv_r = (refs[(4 + k) * n:(5 + k) * n] for k in range(3))
        for a in range(n):
            _adamw_update(w_r[a], g_r[a], m_r[a], v_r[a], d_r[a], nm_r[a], nv_r[a])

    vmem = pl.BlockSpec(memory_space=pltpu.VMEM)
    flat = lambda arrs: [a.reshape(view) for a, view in zip(arrs, views)]
    outs = pl.pallas_call(
        body,
        name=name,
        in_specs=[vmem] * (4 * n),
        out_specs=[vmem] * (3 * n),
        out_shape=[jax.ShapeDtypeStruct(view, F32) for view in views] * 3,
        compiler_params=_params(),
    )(*flat(ws), *flat(gs), *flat(ms), *flat(vs))
    return [tuple(outs[k * n + a].reshape(ws[a].shape) for k in range(3)) for a in range(n)]


def _adamw(w, g, m, v, *, name):
    shape = w.shape
    rows, cols = _as_rows(shape)
    tr = _row_tile(rows, cols, 7)
    body = functools.partial(_adamw_update)

    blk = pl.BlockSpec((tr, cols), lambda i: (i, 0))
    flat = lambda a: a.reshape(rows, cols)
    outs = pl.pallas_call(
        body,
        name=name,
        grid=(rows // tr,),
        in_specs=[blk] * 4,
        out_specs=[blk] * 3,
        out_shape=[jax.ShapeDtypeStruct((rows, cols), F32)] * 3,
        compiler_params=_params(),
    )(flat(w), flat(g), flat(m), flat(v))
    return tuple(o.reshape(shape) for o in outs)


BIG = ("w_in", "w_branch_out", "w_o", "w_gate_up", "w_down")
BIG_AXIS = {"w_in": 1, "w_branch_out": 2, "w_o": 0, "w_gate_up": 1, "w_down": 0}
SMALL = ("mix_norm_g", "b_gate", "conv_w", "conv_b", "sgu_ln_g", "sgu_ln_b", "sgu_w", "sgu_b", "q_norm_g", "k_norm_g",
         "ffn_norm_g")
ORDER = ("mix_norm_g", "w_in", "b_gate", "conv_w", "conv_b", "sgu_ln_g", "sgu_ln_b", "sgu_w", "sgu_b", "q_norm_g",
         "k_norm_g", "w_branch_out", "w_o", "ffn_norm_g", "w_gate_up", "w_down")


def _layer_forward(x, w, wl, l, coming):
    n_in = wl["w_in"].shape[2]
    gate_col0 = (n_in - 3 * x.shape[1]) // x.shape[1]
    tag = f"l{l}"
    rider = None
    if coming:
        placed, axes, nxt = coming
        half = len(placed) // 2
        first, second = (placed[:half], axes[:half], nxt), (placed[half:], axes[half:], nxt)
        rider = _gather_riders(*first)[0]
    (p, h), arrived = _norm_matmul(x, w["mix_norm_g"][l][None], wl["w_in"], 0, name=f"in_proj_{tag}", tn=n_in // 4,
                                   rider=rider)
    ya = _conv_fwd(p, wl["conv_w"][0], w["conv_b"][l][None], name=f"conv_{tag}")
    yb = _sgu_fwd(p, w["sgu_ln_g"][l][None], w["sgu_ln_b"][l][None], w["wtril"][l], w["bias_full"][l], col0=3,
                  name=f"sgu_{tag}")
    q_c, k_cb, v_cb = _qkv_prep(p, w["qg"][l], w["kg"][l], w["headavg"], col0=5, name=f"qkv_{tag}")
    if coming:
        rider = _join_riders(_gather_riders(list(arrived), *first[1:])[1], _gather_riders(*second)[0])
    (out_t, runs), arrived = _att_fwd(k_cb, q_c, v_cb, w["later"], name=f"att_{tag}", rider=rider)
    x1, merged = _merge_fwd((ya, yb, out_t), p, w["b_gate"][l][None], wl["w_branch_out"], wl["w_o"], 0, x,
                            gate_col0=gate_col0, name=f"merge_{tag}")
    if coming:
        here = arrived[:half]
        rider = _gather_riders(list(arrived[half:]), *second[1:])[1]
    (gu, h2), arrived = _norm_matmul(x1, w["ffn_norm_g"][l][None], wl["w_gate_up"], 0, name=f"gate_up_{tag}",
                                     tn=wl["w_gate_up"].shape[2] // 4, rider=rider)
    if coming:
        arrived = (*here, *arrived)
    x2 = _ffn_down(gu, wl["w_down"], 0, x1, name=f"down_{tag}")
    saved = dict(x=x, p=p, h=h, ya=ya, yb=yb, yct=out_t, merged=merged, x1=x1, gu=gu, h2=h2,
                 att=(k_cb, v_cb, q_c, runs), gate_col0=gate_col0)
    return x2, saved, arrived


def _layer_backward(dx2, w, wl, sv, l, chain, make_chain, last):
    tag = f"l{l}"
    c = w["conv_b"].shape[1]
    n_in = wl["w_in"].shape[2]
    n_ff = wl["w_gate_up"].shape[2]
    g, big = {}, {}
    (dgu, act, dx2b), arrived = _ffn_bwd(dx2, sv["gu"], wl["w_down"], 0, name=f"down_bwd_{tag}",
                                         rider=chain.exchange() if chain else None)
    if chain:
        chain.exchanged(arrived)
    big["w_down"] = _matmul_tn(act, dx2b, 0, 1, None, name=f"dw_down_{tag}", t1=act.shape[1] // 2)
    (dx1, dg2), _ = _matmul_nt_normbwd([dgu], wl["w_gate_up"], 0, sv["x1"], w["ffn_norm_g"][l][None], dx2,
                                       name=f"gate_up_bwd_{tag}")
    g["ffn_norm_g"] = jnp.sum(dg2, axis=0)
    big["w_gate_up"] = _matmul_tn(sv["h2"], dgu, 0, 1, None, name=f"dw_gate_up_{tag}", t1=512)
    ys = (sv["ya"], sv["yb"], sv["yct"])
    (dgates, dya, dyb, dyct, dd0, dd1, dd2, dx1b, dbg) = _merge_bwd(
        dx1, ys, sv["p"], w["b_gate"][l][None], wl["w_branch_out"], wl["w_o"], 0, gate_col0=sv["gate_col0"],
        name=f"merge_bwd_{tag}")
    g["b_gate"] = jnp.sum(dbg, axis=0)
    big["w_o"] = _matmul_tn(sv["merged"], dx1b, 0, 1, None, name=f"dw_o_{tag}")
    for i, (y, dd) in enumerate(zip(ys, (dd0, dd1, dd2))):
        heads_first = y.ndim == 3
        big["w_branch_out"] = _matmul_tn(y.reshape(c, -1) if heads_first else y, dd, i, len(ys), big.get("w_branch_out"),
                                         name=f"dw_bo{i}_{tag}", a_transposed=heads_first)
    dconv, dwc = _conv_bwd(sv["p"], dya, wl["conv_w"][0], w["conv_b"][l][None], name=f"conv_bwd_{tag}")
    dwc = jnp.sum(dwc, axis=1)
    g["conv_w"] = dwc[0:CONV_K]
    g["conv_b"] = dwc[CONV_K]
    dsgu, dln, dws, dbias = _sgu_bwd(sv["p"], dyb, w["sgu_ln_g"][l][None], w["sgu_ln_b"][l][None], w["wtril"][l],
                                     w["wtril_t"][l], w["bias_full"][l], col0=3, name=f"sgu_bwd_{tag}")
    dln = jnp.sum(dln, axis=1)
    g["sgu_ln_g"], g["sgu_ln_b"] = dln[0], dln[1]
    g["sgu_w"] = jnp.where(w["tril"], dws, 0.0)
    g["sgu_b"] = jnp.sum(dbias.reshape(CHUNK, c // CHUNK, CHUNK), axis=2).T
    k_cb, v_cb, q_c, runs = sv["att"]
    (dqt, dkt, dvt), arrived = _att_bwd(k_cb, v_cb, q_c, dyct, runs, w["later"], w["earlier"],
                                        name=f"att_bwd_{tag}", rider=chain.scatter() if chain else None)
    if chain:
        chain.scattered(arrived)
    dqkv, dqkg = _qkv_bwd(sv["p"], dqt, dkt, dvt, w["qg"][l], w["kg"][l], w["headavg"], col0=5, name=f"qkv_bwd_{tag}")
    dqkg = jnp.sum(dqkg.reshape(2, 8 * HEADS, HEAD_DIM), axis=1)
    g["q_norm_g"], g["k_norm_g"] = dqkg[0], dqkg[1]
    pieces = [dconv, dsgu, dqkv, dgates]
    col0 = 0
    for k, pc in enumerate(pieces):
        width = pc.shape[1]
        big["w_in"] = _matmul_tn(sv["h"], pc, 0, 1, big.get("w_in"), name=f"dw_in{k}_{tag}", col0=col0, n_total=n_in,
                                 tn=math.gcd(col0, width) if col0 else width)
        col0 += width
    big = {n: a.reshape(wl[n].shape[1:]) for n, a in big.items()}
    own = make_chain([big[n] for n in BIG], l)
    riders = [chain.share()] if chain else []
    if last:
        own.exchanged(_ride_alone([own.exchange()], name=f"grads_pair_exchange_{tag}"))
        riders.append(own.scatter())
    (dx0, dg1), arrived = _matmul_nt_normbwd(pieces, wl["w_in"], 0, sv["x"], w["mix_norm_g"][l][None], dx1,
                                             name=f"in_proj_bwd_{tag}", rider=_join_riders(*riders) if riders else None)
    if chain:
        count = len(chain.finals)
        chain.shared(arrived[:count])
        arrived = arrived[count:]
        own.finals = chain.finals
    if last:
        own.scattered(arrived)
        own.shared(_ride_alone([own.share()], name=f"grads_pair_share_{tag}"))
    g["mix_norm_g"] = jnp.sum(dg1, axis=0)
    return dx0, g, own


def kernel(x, mix_norm_g, w_in, b_gate, conv_w, conv_b, sgu_ln_g, sgu_ln_b, sgu_w, sgu_b, q_norm_g, k_norm_g, w_branch_out, w_o, ffn_norm_g, w_gate_up, w_down, loss_target, m_mix_norm_g, m_w_in, m_b_gate, m_conv_w, m_conv_b, m_sgu_ln_g, m_sgu_ln_b, m_sgu_w, m_sgu_b, m_q_norm_g, m_k_norm_g, m_w_branch_out, m_w_o, m_ffn_norm_g, m_w_gate_up, m_w_down, v_mix_norm_g, v_w_in, v_b_gate, v_conv_w, v_conv_b, v_sgu_ln_g, v_sgu_ln_b, v_sgu_w, v_sgu_b, v_q_norm_g, v_k_norm_g, v_w_branch_out, v_w_o, v_ffn_norm_g, v_w_gate_up, v_w_down):
    given = dict(locals())
    params = {n: given[n] for n in ORDER}
    moms = {n: (given["m_" + n], given["v_" + n]) for n in ORDER}
    layers = mix_norm_g.shape[0]
    assert layers == 2, "the exchanges split the work of a chip's two cores by layer"
    xs = x[0]
    target = loss_target[0]
    chip = 2 * lax.axis_index("x") + lax.axis_index("y")
    core = lax.axis_index("c")

    c_idx = core.reshape(1).astype(jnp.int32)
    j_idx = chip.reshape(1).astype(jnp.int32)
    width = conv_w.shape[2]
    gathered = BIG + ("conv_w",)
    axes = [BIG_AXIS[n] for n in BIG] + [1]

    def placed(layer):
        arrays = [_place_shard(params[n], layer, BIG_AXIS[n], j_idx, MXU_DTYPE, name=f"place_{n}_l{layer}") for n in BIG]
        mine = lax.dynamic_slice_in_dim(conv_w, layer, 1, axis=0)
        arrays.append(lax.dynamic_update_slice_in_dim(jnp.zeros((1, CONV_K, width * N_CHIPS), F32), mine, chip * width, 2))
        return arrays

    weights = [dict(zip(gathered, _ride_alone(_gather_riders(placed(0), axes, 0)[2:], name="gather_weights_l0"))), None]
    w = {n: params[n] for n in ("mix_norm_g", "b_gate", "conv_b", "sgu_ln_g", "sgu_ln_b", "ffn_norm_g")}
    tril = jnp.tril(jnp.ones((CHUNK, CHUNK), dtype=bool))
    w["tril"] = tril
    w["wtril"] = _mx(jnp.where(tril, sgu_w, 0.0))
    w["wtril_t"] = w["wtril"].transpose(0, 1, 3, 2)
    w["bias_full"] = jnp.repeat(sgu_b.transpose(0, 2, 1), CHUNK, axis=2)
    w["qg"] = jnp.tile(q_norm_g, (1, HEADS))[:, None, :]
    w["kg"] = jnp.tile(k_norm_g, (1, HEADS))[:, None, :]
    lane = jnp.arange(HEADS * HEAD_DIM) // HEAD_DIM
    w["headavg"] = _mx(jnp.where(lane[:, None] == lane[None, :], 1.0 / HEAD_DIM, 0.0))
    pos = jnp.arange(ATT_BLOCK)
    w["later"] = _mx(jnp.where(pos[None, :] > pos[:, None], 1.0, 0.0))
    w["earlier"] = _mx(jnp.where(pos[None, :] < pos[:, None], 1.0, 0.0))

    saved = []
    act = xs
    for l in range(layers):
        coming = (placed(l + 1), axes, l + 1) if l + 1 < layers else None
        act, sv, arrived = _layer_forward(act, w, weights[l], l, coming)
        saved.append(sv)
        if coming:
            weights[l + 1] = dict(zip(gathered, arrived))
    loss_part, dact = _loss_head(act, target, name="loss_head")
    loss = lax.psum(jnp.sum(loss_part), ("x", "y", "c"))
    grads = [None] * layers
    chain = None
    make_chain = lambda large, layer: _ReduceChain(large, layer, layers, j_idx, c_idx, None)
    for l in reversed(range(layers)):
        dact, grads[l], chain = _layer_backward(dact, w, weights[l], saved[l], l, chain, make_chain, l == 0)
    grad_x = dact[None]
    local = {n: jnp.stack([grads[l][n] for l in range(layers)]) for n in SMALL}
    full = dict(zip(BIG, chain.finals))

    summed = _all_reduce_small([local[n] for n in SMALL], name="grads_all_reduce_small")
    for n, gsum in zip(SMALL, summed):
        full[n] = gsum
    full["conv_w"] = lax.dynamic_slice_in_dim(full["conv_w"], chip * width, width, axis=2)

    out = {}
    for n in BIG:
        out[n] = _adamw(params[n], full[n], *moms[n], name=f"adamw_{n}")
    small = _adamw_small([params[n] for n in SMALL], [full[n] for n in SMALL], [moms[n][0] for n in SMALL],
                         [moms[n][1] for n in SMALL], name="adamw_small")
    for n, triple in zip(SMALL, small):
        out[n] = triple
    return (loss, grad_x, *[full[n] for n in ORDER], *[out[n][0] for n in ORDER], *[out[n][1] for n in ORDER],
            *[out[n][2] for n in ORDER])
```

```python
import functools
import math
from typing import Callable, NamedTuple

import jax
import jax.numpy as jnp
from jax import lax
from jax.experimental import pallas as pl
from jax.experimental.pallas import tpu as pltpu

F32 = jnp.float32
MXU_DTYPE = jnp.bfloat16
WIRE_DTYPE = jnp.bfloat16
ACT_DTYPE = jnp.bfloat16
HALO = 16

EPS = 1e-6
CONV_K = 3
CHUNK = 128
HEADS = 8
HEAD_DIM = 64
ATT_BLOCK = 128
EXP_UNDERFLOW = -88.0

ADAM_LR = 0.001
ADAM_B1 = 0.9
ADAM_B2 = 0.999
ADAM_EPS = 1e-08
ADAM_WD = 0.01
ADAM_STEP = 10

VMEM_LIMIT = 56 * 1024 * 1024
MESH = pl.DeviceIdType.MESH
N_CHIPS = 4
N_DEV = 8
ANY = pl.BlockSpec(memory_space=pl.ANY)


def _params(**kw):
    return pltpu.CompilerParams(vmem_limit_bytes=VMEM_LIMIT, **kw)


class _Rider(NamedTuple):
    arrays: tuple
    sems: tuple
    start: Callable
    finish: Callable


def _call(body, operands, *, name, grid, in_specs, out_specs, out_shape, scratch_shapes=(), rider=None):
    if rider is None:
        outs = pl.pallas_call(body, name=name, grid=grid, in_specs=in_specs, out_specs=out_specs, out_shape=out_shape,
                              scratch_shapes=list(scratch_shapes), compiler_params=_params())(*operands)
        return tuple(outs), ()
    n_in, n_out, n_scr, k = len(in_specs), len(out_specs), len(scratch_shapes), len(rider.arrays)

    def riding(*refs):
        ins, refs = refs[:n_in], refs[n_in + k:]
        outs, carried, refs = refs[:n_out], refs[n_out:n_out + k], refs[n_out + k:]
        scratch, sems = refs[:n_scr], refs[n_scr:]
        first = functools.reduce(jnp.logical_and, [pl.program_id(ax) == 0 for ax in range(len(grid))])
        last = functools.reduce(jnp.logical_and, [pl.program_id(ax) == grid[ax] - 1 for ax in range(len(grid))])

        @pl.when(first)
        def _():
            rider.start(carried, sems)

        body(*ins, *outs, *scratch)

        @pl.when(last)
        def _():
            rider.finish(carried, sems)

    res = pl.pallas_call(
        riding, name=name, grid=grid, in_specs=[*in_specs, *[ANY] * k], out_specs=[*out_specs, *[ANY] * k],
        out_shape=[*out_shape, *[jax.ShapeDtypeStruct(a.shape, a.dtype) for a in rider.arrays]],
        input_output_aliases={n_in + j: n_out + j for j in range(k)},
        scratch_shapes=[*scratch_shapes, *[pltpu.SemaphoreType.DMA(shape) for shape in rider.sems]],
        compiler_params=_params())(*operands, *rider.arrays)
    return tuple(res[:n_out]), tuple(res[n_out:])


def _ride_alone(riders, *, name):
    arrays = riders[0].arrays
    k = len(arrays)
    counts = [len(r.sems) for r in riders]

    def body(*refs):
        carried, sems = refs[k:2 * k], refs[2 * k:]
        for r, count in zip(riders, counts):
            r.start(carried, sems[:count])
            r.finish(carried, sems[:count])
            sems = sems[count:]

    return pl.pallas_call(
        body, name=name, in_specs=[ANY] * k, out_specs=[ANY] * k,
        out_shape=[jax.ShapeDtypeStruct(a.shape, a.dtype) for a in arrays],
        input_output_aliases={j: j for j in range(k)},
        scratch_shapes=[pltpu.SemaphoreType.DMA(shape) for r in riders for shape in r.sems])(*arrays)


def _mx(v):
    return v.astype(MXU_DTYPE)


def _dot(a, b):
    return lax.dot_general(a, b, (((1,), (0,)), ((), ())), preferred_element_type=F32)


def _dot_nt(a, b):
    return lax.dot_general(a, b, (((1,), (1,)), ((), ())), preferred_element_type=F32)


def _dot_tn(a, b):
    return lax.dot_general(a, b, (((0,), (0,)), ((), ())), preferred_element_type=F32)


def _dot_split_r(v, const):
    hi = _mx(v)
    lo = _mx(v - hi.astype(F32))
    return _dot(hi, const) + _dot(lo, const)


def _sigmoid(x):
    return 1.0 / (1.0 + jnp.exp(-x))


_INV_SQRT2 = 1.0 / math.sqrt(2.0)
_INV_SQRT2PI = 1.0 / math.sqrt(2.0 * math.pi)


def _gelu(x):
    return 0.5 * x * (1.0 + lax.erf(x * _INV_SQRT2))


def _gelu_grad(x):
    return 0.5 * (1.0 + lax.erf(x * _INV_SQRT2)) + x * jnp.exp(-0.5 * x * x) * _INV_SQRT2PI


def _row_fold(v):
    m, n = v.shape
    return jnp.sum(v.reshape(m // 8, 8, n), axis=0)


def _tile(m, pref):
    t = min(m, pref)
    while m % t:
        t //= 2
    return t


def _resident(block_shape, index_map):
    return pl.BlockSpec(block_shape, index_map, pipeline_mode=pl.Buffered(1))


def _norm_matmul(x, g, w, l, *, name, tm=512, tn=None, rider=None):
    s, d = x.shape
    n = w.shape[2]
    tm = _tile(s, tm)
    tn = tn or n
    assert n % tn == 0

    def body(x_ref, g_ref, w_ref, p_ref, h_ref):
        xf = x_ref[...]
        r = lax.rsqrt(jnp.mean(xf * xf, axis=-1, keepdims=True) + EPS)
        h = _mx(xf * r * g_ref[...])
        h_ref[...] = h
        for j in range(n // tn):
            cols = slice(j * tn, (j + 1) * tn)
            p_ref[:, cols] = _dot(h, w_ref[:, cols]).astype(ACT_DTYPE)

    return _call(
        body, (x, g, w),
        name=name,
        grid=(s // tm,),
        in_specs=[
            pl.BlockSpec((tm, d), lambda i: (i, 0)),
            pl.BlockSpec((1, d), lambda i: (0, 0)),
            _resident((None, d, n), lambda i: (l, 0, 0)),
        ],
        out_specs=[
            pl.BlockSpec((tm, n), lambda i: (i, 0)),
            pl.BlockSpec((tm, d), lambda i: (i, 0)),
        ],
        out_shape=[jax.ShapeDtypeStruct((s, n), ACT_DTYPE), jax.ShapeDtypeStruct((s, d), MXU_DTYPE)],
        rider=rider,
    )


def _conv_taps(u, prev):
    row = lax.broadcasted_iota(jnp.int32, u.shape, 0)
    last, before = prev[HALO - 1:HALO, :], prev[HALO - 2:HALO - 1, :]
    um1 = jnp.where(row == 0, last, pltpu.roll(u, 1, 0))
    um2 = pltpu.roll(u, 2, 0)
    um2 = jnp.where(row == 0, before, jnp.where(row == 1, last, um2))
    return um1, um2


def _f32(ref):
    return ref[...].astype(F32)


def _conv_fwd(p, conv_w, conv_b, *, name, tm=512):
    s = p.shape[0]
    c = conv_w.shape[1]
    tm = _tile(s, tm)
    hb = tm // HALO

    def body(bg_ref, cg_ref, xa_ref, cgp_ref, xap_ref, w_ref, b_ref, y_ref):
        first = pl.program_id(0) == 0
        u = _f32(cg_ref) * _f32(xa_ref)
        prev = jnp.where(first, 0.0, _f32(cgp_ref) * _f32(xap_ref))
        um1, um2 = _conv_taps(u, prev)
        w = w_ref[...]
        y = b_ref[...] + w[0:1, :] * um2 + w[1:2, :] * um1 + w[2:3, :] * u
        y_ref[...] = _mx(_f32(bg_ref) * y)

    halo = lambda col: pl.BlockSpec((HALO, c), lambda i: (jnp.maximum(i * hb - 1, 0), col))
    return pl.pallas_call(
        body,
        name=name,
        grid=(s // tm,),
        in_specs=[
            pl.BlockSpec((tm, c), lambda i: (i, 0)),
            pl.BlockSpec((tm, c), lambda i: (i, 1)),
            pl.BlockSpec((tm, c), lambda i: (i, 2)),
            halo(1),
            halo(2),
            pl.BlockSpec((CONV_K, c), lambda i: (0, 0)),
            pl.BlockSpec((1, c), lambda i: (0, 0)),
        ],
        out_specs=pl.BlockSpec((tm, c), lambda i: (i, 0)),
        out_shape=jax.ShapeDtypeStruct((s, c), MXU_DTYPE),
        compiler_params=_params(),
    )(p, p, p, p, p, conv_w, conv_b)


def _layernorm_stats(x):
    mu = jnp.mean(x, axis=-1, keepdims=True)
    xc = x - mu
    r = lax.rsqrt(jnp.mean(xc * xc, axis=-1, keepdims=True) + EPS)
    return xc * r, r


def _sgu_fwd(p, ln_g, ln_b, wtril, bias_full, *, col0, name, tm=512):
    s = p.shape[0]
    c = ln_g.shape[1]
    groups = c // CHUNK
    tm = _tile(s, tm)

    def body(u_ref, v_ref, g_ref, b_ref, w_ref, bias_ref, y_ref):
        vn, _ = _layernorm_stats(_gelu(_f32(v_ref)))
        vb = _mx(vn * g_ref[...] + b_ref[...])
        for n in range(tm // CHUNK):
            rows = slice(n * CHUNK, (n + 1) * CHUNK)
            for gi in range(groups):
                cols = slice(gi * CHUNK, (gi + 1) * CHUNK)
                mixed = _dot(w_ref[gi], vb[rows, cols]) + bias_ref[:, cols]
                y_ref[rows, cols] = _mx(_gelu(u_ref[rows, cols].astype(F32)) * mixed)

    return pl.pallas_call(
        body,
        name=name,
        grid=(s // tm,),
        in_specs=[
            pl.BlockSpec((tm, c), lambda i: (i, col0)),
            pl.BlockSpec((tm, c), lambda i: (i, col0 + 1)),
            pl.BlockSpec((1, c), lambda i: (0, 0)),
            pl.BlockSpec((1, c), lambda i: (0, 0)),
            pl.BlockSpec((groups, CHUNK, CHUNK), lambda i: (0, 0, 0)),
            pl.BlockSpec((CHUNK, c), lambda i: (0, 0)),
        ],
        out_specs=pl.BlockSpec((tm, c), lambda i: (i, 0)),
        out_shape=jax.ShapeDtypeStruct((s, c), MXU_DTYPE),
        compiler_params=_params(),
    )(p, p, ln_g, ln_b, wtril, bias_full)


def _head_mean(v, headavg):
    return _dot_split_r(v, headavg)


def _qkv_prep(p, qg, kg, headavg, *, col0, name, tm=512):
    s = p.shape[0]
    c = qg.shape[1]
    t = ATT_BLOCK
    tm = _tile(s, tm)
    assert tm % t == 0

    def body(q_ref, k_ref, v_ref, qg_ref, kg_ref, avg_ref, qc_ref, kcb_ref, vcb_ref):
        normed = []
        for src, g_ref in ((q_ref, qg_ref), (k_ref, kg_ref)):
            xf = _f32(src)
            r = lax.rsqrt(_head_mean(xf * xf, avg_ref[...]) + EPS)
            normed.append(xf * r * g_ref[...])
        q_t, k_t, v_t = (_mx(a.T) for a in (*normed, _f32(v_ref)))
        for hh in range(HEADS):
            rows = slice(hh * HEAD_DIM, (hh + 1) * HEAD_DIM)
            qc_ref[hh] = q_t[rows, :]
            for b in range(tm // t):
                kcb_ref[hh, b] = k_t[rows, b * t:(b + 1) * t]
                vcb_ref[hh, b] = v_t[rows, b * t:(b + 1) * t]

    blk = lambda col: pl.BlockSpec((tm, c), lambda i: (i, col))
    vec = pl.BlockSpec((1, c), lambda i: (0, 0))
    blocked = pl.BlockSpec((HEADS, tm // t, HEAD_DIM, t), lambda i: (0, i, 0, 0))
    return pl.pallas_call(
        body,
        name=name,
        grid=(s // tm,),
        in_specs=[blk(col0), blk(col0 + 1), blk(col0 + 2), vec, vec, pl.BlockSpec((c, c), lambda i: (0, 0))],
        out_specs=[pl.BlockSpec((HEADS, HEAD_DIM, tm), lambda i: (0, 0, i)), blocked, blocked],
        out_shape=[jax.ShapeDtypeStruct((HEADS, HEAD_DIM, s), MXU_DTYPE)]
        + [jax.ShapeDtypeStruct((HEADS, s // t, HEAD_DIM, t), MXU_DTYPE)] * 2,
        compiler_params=_params(),
    )(p, p, p, qg, kg, headavg)


def _att_mask(t, key0, qry0):
    kpos = key0 + lax.broadcasted_iota(jnp.int32, (t, t), 0)
    qpos = qry0 + lax.broadcasted_iota(jnp.int32, (t, t), 1)
    return kpos < qpos


def _att_blocks(k_blks, q_ts, scale, mask):
    zs = [(_dot_tn(k, q) if k.shape[0] == q.shape[0] else _dot(k, q)) * scale for k, q in zip(k_blks, q_ts)]
    es = [jnp.exp(-jnp.abs(z)) for z in zs]
    lbs = [jnp.minimum(z, 0.0) - jnp.log1p(e) for z, e in zip(zs, es)]
    l1ms = [jnp.where(mask, lb - z, 0.0) for lb, z in zip(lbs, zs)]
    return zs, es, lbs, l1ms


def _dot_split_each(const, vs):
    his = [_mx(v) for v in vs]
    los = [_mx(v - hi.astype(F32)) for v, hi in zip(vs, his)]
    tops = [_dot(const, hi) for hi in his]
    return [top + _dot(const, lo) for top, lo in zip(tops, los)]


def _max_over(runs):
    m = runs[0]
    for r in runs[1:]:
        m = jnp.maximum(m, r)
    return jnp.max(m)


NOT_SEEN = -1e30


def _row_put(ref, g, j, row):
    j8 = pl.multiple_of((j // 8) * 8, 8)
    blk = ref[g, pl.ds(j8, 8), :]
    sub = lax.broadcasted_iota(jnp.int32, blk.shape, 0)
    ref[g, pl.ds(j8, 8), :] = jnp.where(sub == j - j8, row, blk)


def _row_get(ref, g, j):
    j8 = pl.multiple_of((j // 8) * 8, 8)
    blk = ref[g, pl.ds(j8, 8), :]
    sub = lax.broadcasted_iota(jnp.int32, blk.shape, 0)
    return jnp.sum(jnp.where(sub == j - j8, blk, 0.0), axis=0, keepdims=True)


def _att_fwd(k_hm, qt_hm, vt_hm, later, *, name, hg=8, rider=None):
    h, nb, d, t = k_hm.shape
    s = nb * t
    nbp = -(-nb // 8) * 8
    scale = 1.0 / math.sqrt(d)
    assert h % hg == 0

    def body(k_ref, qt_ref, vt_ref, later_ref, o_ref, runs_ref):
        i = pl.program_id(1)
        q_ts = [qt_ref[g] for g in range(hg)]
        runs_ref[...] = jnp.full(runs_ref.shape, NOT_SEEN, F32)

        def cond(carry):
            j, _, _, rmax = carry
            return jnp.logical_and(j >= 0, rmax > EXP_UNDERFLOW)

        def step(carry):
            j, runs, accs, _ = carry
            mask = _att_mask(t, j * t, i * t)
            heads = range(hg)
            for g in heads:
                _row_put(runs_ref, g, j, runs[g])
            _, _, lbs, l1ms = _att_blocks([k_ref[g, j] for g in heads], q_ts, scale, mask)
            afters = _dot_split_each(later_ref[...], l1ms)
            weights = [_mx(jnp.where(mask, jnp.exp(lbs[g] + afters[g] + runs[g]), 0.0)) for g in heads]
            new_accs = [accs[g] + _dot(vt_ref[g, j], weights[g]) for g in heads]
            new_runs = [runs[g] + jnp.sum(l1ms[g], axis=0, keepdims=True) for g in heads]
            return j - 1, tuple(new_runs), tuple(new_accs), _max_over(new_runs)

        init = (i, tuple(jnp.zeros((1, t), F32) for _ in range(hg)), tuple(jnp.zeros((d, t), F32) for _ in range(hg)),
                jnp.float32(0.0))
        _, _, accs, _ = lax.while_loop(cond, step, init)
        for g in range(hg):
            o_ref[g] = _mx(accs[g])

    return _call(
        body, (k_hm, qt_hm, vt_hm, later),
        name=name,
        grid=(h // hg, nb),
        in_specs=[
            pl.BlockSpec((hg, nb, d, t), lambda hh, i: (hh, 0, 0, 0)),
            pl.BlockSpec((hg, d, t), lambda hh, i: (hh, 0, i)),
            pl.BlockSpec((hg, nb, d, t), lambda hh, i: (hh, 0, 0, 0)),
            pl.BlockSpec((t, t), lambda hh, i: (0, 0)),
        ],
        out_specs=[pl.BlockSpec((hg, d, t), lambda hh, i: (hh, 0, i)),
                   pl.BlockSpec((hg, None, nbp, t), lambda hh, i: (hh, i, 0, 0))],
        out_shape=[jax.ShapeDtypeStruct((h, d, s), MXU_DTYPE), jax.ShapeDtypeStruct((h, nb, nbp, t), F32)],
        rider=rider,
    )


def _heads_to_rows(ref):
    return _mx(jnp.concatenate([ref[hh].astype(F32) for hh in range(HEADS)], axis=0).T)


def _merge_fwd(ys, p, b_gate, w_bo, w_o, l, x, *, gate_col0, name, tm=512):
    s, d = x.shape
    _, nbr, c, _ = w_bo.shape
    tm = _tile(s, tm)

    def body(ya_ref, yb_ref, yct_ref, g0_ref, g1_ref, g2_ref, bg_ref, wbo_ref, wo_ref, x_ref, x1_ref, m_ref):
        merged = jnp.zeros((tm, d), F32)
        branches = ((ya_ref[...], g0_ref), (yb_ref[...], g1_ref), (_heads_to_rows(yct_ref), g2_ref))
        for i, (y, g_ref) in enumerate(branches):
            gate = _sigmoid(_f32(g_ref) + bg_ref[:, i * d:(i + 1) * d])
            merged = merged + gate * _dot(y, wbo_ref[i])
        mb = _mx(merged)
        m_ref[...] = mb
        x1_ref[...] = x_ref[...] + _dot(mb, wo_ref[...])

    yblk = pl.BlockSpec((tm, c), lambda i: (i, 0))
    tblk = pl.BlockSpec((HEADS, HEAD_DIM, tm), lambda i: (0, 0, i))
    gblk = lambda k: pl.BlockSpec((tm, d), lambda i: (i, gate_col0 + k))
    xblk = pl.BlockSpec((tm, d), lambda i: (i, 0))
    return pl.pallas_call(
        body,
        name=name,
        grid=(s // tm,),
        in_specs=[
            yblk, yblk, tblk, gblk(0), gblk(1), gblk(2),
            pl.BlockSpec((1, nbr * d), lambda i: (0, 0)),
            _resident((None, nbr, c, d), lambda i: (l, 0, 0, 0)),
            _resident((None, d, d), lambda i: (l, 0, 0)),
            xblk,
        ],
        out_specs=[xblk, xblk],
        out_shape=[jax.ShapeDtypeStruct((s, d), F32), jax.ShapeDtypeStruct((s, d), MXU_DTYPE)],
        compiler_params=_params(),
    )(*ys, p, p, p, b_gate, w_bo, w_o, x)


def _ffn_down(gu, w_down, l, x, *, name, tm=512):
    s, d = x.shape
    f = w_down.shape[1]
    tm = _tile(s, tm)

    def body(g_ref, u_ref, w_ref, x_ref, o_ref):
        g = _f32(g_ref)
        act = _mx(g * _sigmoid(g) * _f32(u_ref))
        o_ref[...] = x_ref[...] + _dot(act, w_ref[...])

    return pl.pallas_call(
        body,
        name=name,
        grid=(s // tm,),
        in_specs=[
            pl.BlockSpec((tm, f), lambda i: (i, 0)),
            pl.BlockSpec((tm, f), lambda i: (i, 1)),
            _resident((None, f, d), lambda i: (l, 0, 0)),
            pl.BlockSpec((tm, d), lambda i: (i, 0)),
        ],
        out_specs=pl.BlockSpec((tm, d), lambda i: (i, 0)),
        out_shape=jax.ShapeDtypeStruct((s, d), F32),
        compiler_params=_params(),
    )(gu, gu, w_down, x)


def _loss_head(y, target, *, name, tm=512):
    s, d = y.shape
    tm = _tile(s, tm)

    def body(y_ref, t_ref, l_ref, dy_ref):
        @pl.when(pl.program_id(0) == 0)
        def _():
            l_ref[...] = jnp.zeros_like(l_ref)

        err = y_ref[...] - t_ref[...]
        dy_ref[...] = err * (1.0 / d)
        sq = _row_fold(err * err)
        part = sq[:, 0:128]
        for k in range(1, d // 128):
            part = part + sq[:, k * 128:(k + 1) * 128]
        l_ref[...] += part * (0.5 / d)

    blk = pl.BlockSpec((tm, d), lambda i: (i, 0))
    return pl.pallas_call(
        body,
        name=name,
        grid=(s // tm,),
        in_specs=[blk, blk],
        out_specs=[pl.BlockSpec((8, 128), lambda i: (0, 0)), blk],
        out_shape=[jax.ShapeDtypeStruct((8, 128), F32), jax.ShapeDtypeStruct((s, d), F32)],
        compiler_params=_params(),
    )(y, target)


def _matmul_tn(a, b, slot, n_slots, into, *, name, t1=1024, tn=None, tm=1024, col0=0, n_total=None, a_transposed=False):
    m, k1 = a.shape[::-1] if a_transposed else a.shape
    n = b.shape[1]
    n_total = n_total or n
    t1 = _tile(k1, t1)
    tn = tn or n
    tm = _tile(m, tm)
    steps = m // tm
    assert n % tn == 0 and col0 % tn == 0
    cb0 = col0 // tn

    def body(a_ref, b_ref, *refs):
        o_ref = refs[-1]

        @pl.when(pl.program_id(2) == 0)
        def _():
            o_ref[...] = jnp.zeros_like(o_ref)

        o_ref[...] += (_dot if a_transposed else _dot_tn)(a_ref[...], b_ref[...])

    return pl.pallas_call(
        body,
        name=name,
        grid=(k1 // t1, n // tn, steps),
        in_specs=[
            pl.BlockSpec((t1, tm), lambda i, j, k: (i, k)) if a_transposed else pl.BlockSpec((tm, t1), lambda i, j, k: (k, i)),
            pl.BlockSpec((tm, tn), lambda i, j, k: (k, j)),
        ] + ([] if into is None else [ANY]),
        out_specs=pl.BlockSpec((None, t1, tn), lambda i, j, k: (slot, i, cb0 + j)),
        out_shape=jax.ShapeDtypeStruct((n_slots, k1, n_total), F32),
        input_output_aliases={} if into is None else {2: 0},
        compiler_params=_params(),
    )(a, b, *([] if into is None else [into]))


def _matmul_nt_normbwd(pieces, w, l, x, g, dres, *, name, tm=512, tk=1024, rider=None):
    s, d = x.shape
    k = w.shape[2]
    tm = _tile(s, tm)
    n = len(pieces)
    assert sum(pc.shape[1] for pc in pieces) == k

    def body(*refs):
        piece_refs = refs[:n]
        w_ref, x_ref, g_ref, dres_ref, dx_ref, dg_ref = refs[n:]

        @pl.when(pl.program_id(0) == 0)
        def _():
            dg_ref[...] = jnp.zeros_like(dg_ref)

        dh = jnp.zeros((tm, d), F32)
        off = 0
        for pc, pc_ref in zip(pieces, piece_refs):
            width = pc.shape[1]
            step = _tile(width, tk)
            for c0 in range(0, width, step):
                dh = dh + _dot_nt(pc_ref[:, c0:c0 + step], w_ref[:, off + c0:off + c0 + step])
            off += width
        xf = x_ref[...]
        r = lax.rsqrt(jnp.mean(xf * xf, axis=-1, keepdims=True) + EPS)
        y = xf * r
        dy = dh * g_ref[...]
        dx_ref[...] = dres_ref[...] + r * (dy - y * jnp.mean(dy * y, axis=-1, keepdims=True))
        dg_ref[...] += _row_fold(dh * y)

    xblk = pl.BlockSpec((tm, d), lambda i: (i, 0))
    return _call(
        body, (*pieces, w, x, g, dres),
        name=name,
        grid=(s // tm,),
        in_specs=[pl.BlockSpec((tm, pc.shape[1]), lambda i: (i, 0)) for pc in pieces] + [
            _resident((None, d, k), lambda i: (l, 0, 0)),
            xblk,
            pl.BlockSpec((1, d), lambda i: (0, 0)),
            xblk,
        ],
        out_specs=[xblk, pl.BlockSpec((8, d), lambda i: (0, 0))],
        out_shape=[jax.ShapeDtypeStruct((s, d), F32), jax.ShapeDtypeStruct((8, d), F32)],
        rider=rider,
    )


def _ffn_bwd(dx, gu, w_down, l, *, name, tm=512, rider=None):
    s, d = dx.shape
    f = w_down.shape[1]
    tm = _tile(s, tm)

    def body(dx_ref, g_ref, u_ref, w_ref, dgu_ref, act_ref, dxb_ref):
        dxb = _mx(dx_ref[...])
        dxb_ref[...] = dxb
        dact = _dot_nt(dxb, w_ref[...])
        g = _f32(g_ref)
        u = _f32(u_ref)
        sg = _sigmoid(g)
        silu = g * sg
        act_ref[...] = _mx(silu * u)
        dgu_ref[:, 0:f] = _mx(dact * u * (sg * (1.0 + g * (1.0 - sg))))
        dgu_ref[:, f:2 * f] = _mx(dact * silu)

    fblk = lambda col: pl.BlockSpec((tm, f), lambda i: (i, col))
    dblk = pl.BlockSpec((tm, d), lambda i: (i, 0))
    return _call(
        body, (dx, gu, gu, w_down),
        name=name,
        grid=(s // tm,),
        in_specs=[dblk, fblk(0), fblk(1), _resident((None, f, d), lambda i: (l, 0, 0))],
        out_specs=[pl.BlockSpec((tm, 2 * f), lambda i: (i, 0)), fblk(0), dblk],
        out_shape=[
            jax.ShapeDtypeStruct((s, 2 * f), MXU_DTYPE),
            jax.ShapeDtypeStruct((s, f), MXU_DTYPE),
            jax.ShapeDtypeStruct((s, d), MXU_DTYPE),
        ],
        rider=rider,
    )


def _merge_bwd(dx, ys, p, b_gate, w_bo, w_o, l, *, gate_col0, name, tm=512):
    s, d = dx.shape
    _, nbr, c, _ = w_bo.shape
    tm = _tile(s, tm)

    def body(dx_ref, ya_ref, yb_ref, yct_ref, g0_ref, g1_ref, g2_ref, bg_ref, wbo_ref, wo_ref,
             dgate_ref, dya_ref, dyb_ref, dyct_ref, dd0_ref, dd1_ref, dd2_ref, dxb_ref, dbg_ref):
        @pl.when(pl.program_id(0) == 0)
        def _():
            dbg_ref[...] = jnp.zeros_like(dbg_ref)

        dxb = _mx(dx_ref[...])
        dxb_ref[...] = dxb
        dmerged = _dot_nt(dxb, wo_ref[...])
        branches = ((ya_ref[...], g0_ref, dd0_ref), (yb_ref[...], g1_ref, dd1_ref), (_heads_to_rows(yct_ref), g2_ref, dd2_ref))
        dys = []
        for i, (y, g_ref, dd_ref) in enumerate(branches):
            cols = slice(i * d, (i + 1) * d)
            gate = _sigmoid(_f32(g_ref) + bg_ref[:, cols])
            yd = _dot(y, wbo_ref[i])
            dyd = _mx(dmerged * gate)
            dd_ref[...] = dyd
            dys.append(_dot_nt(dyd, wbo_ref[i]))
            dpre = dmerged * yd * gate * (1.0 - gate)
            dgate_ref[:, cols] = _mx(dpre)
            dbg_ref[:, cols] += _row_fold(dpre)
        dya_ref[...] = dys[0]
        dyb_ref[...] = dys[1]
        dyc_t = _mx(dys[2].T)
        for hh in range(HEADS):
            dyct_ref[hh] = dyc_t[hh * HEAD_DIM:(hh + 1) * HEAD_DIM, :]

    yblk = pl.BlockSpec((tm, c), lambda i: (i, 0))
    tblk = pl.BlockSpec((HEADS, HEAD_DIM, tm), lambda i: (0, 0, i))
    gblk = lambda k: pl.BlockSpec((tm, d), lambda i: (i, gate_col0 + k))
    dblk = pl.BlockSpec((tm, d), lambda i: (i, 0))
    return pl.pallas_call(
        body,
        name=name,
        grid=(s // tm,),
        in_specs=[
            dblk, yblk, yblk, tblk, gblk(0), gblk(1), gblk(2),
            pl.BlockSpec((1, nbr * d), lambda i: (0, 0)),
            _resident((None, nbr, c, d), lambda i: (l, 0, 0, 0)),
            _resident((None, d, d), lambda i: (l, 0, 0)),
        ],
        out_specs=[pl.BlockSpec((tm, nbr * d), lambda i: (i, 0)), yblk, yblk, tblk, dblk, dblk, dblk, dblk,
                   pl.BlockSpec((8, nbr * d), lambda i: (0, 0))],
        out_shape=[jax.ShapeDtypeStruct((s, nbr * d), MXU_DTYPE)] + [jax.ShapeDtypeStruct((s, c), F32)] * 2
        + [jax.ShapeDtypeStruct((HEADS, HEAD_DIM, s), MXU_DTYPE)]
        + [jax.ShapeDtypeStruct((s, d), MXU_DTYPE)] * 4 + [jax.ShapeDtypeStruct((8, nbr * d), F32)],
        compiler_params=_params(),
    )(dx, *ys, p, p, p, b_gate, w_bo, w_o)


def _conv_bwd(p, dya, conv_w, conv_b, *, name, tm=512):
    s = p.shape[0]
    c = conv_w.shape[1]
    tm = _tile(s, tm)
    hb = tm // HALO
    last = s // tm - 1

    def body(bg_ref, cg_ref, xa_ref, cgp_ref, xap_ref, dy_ref, dyn_ref, bgn_ref, w_ref, b_ref,
             dp_ref, dw_ref):
        i = pl.program_id(0)

        @pl.when(i == 0)
        def _():
            dw_ref[...] = jnp.zeros_like(dw_ref)

        cg = _f32(cg_ref)
        xa = _f32(xa_ref)
        u = cg * xa
        prev = jnp.where(i == 0, 0.0, _f32(cgp_ref) * _f32(xap_ref))
        um1, um2 = _conv_taps(u, prev)
        w = w_ref[...]
        y = b_ref[...] + w[0:1, :] * um2 + w[1:2, :] * um1 + w[2:3, :] * u
        dya = dy_ref[...]
        dp_ref[:, 0:c] = _mx(dya * y)
        dyv = dya * _f32(bg_ref)
        nxt = jnp.where(i == last, 0.0, dyn_ref[...] * _f32(bgn_ref))
        row = lax.broadcasted_iota(jnp.int32, dyv.shape, 0)
        dp1 = jnp.where(row == tm - 1, nxt[0:1, :], pltpu.roll(dyv, tm - 1, 0))
        dp2 = pltpu.roll(dyv, tm - 2, 0)
        dp2 = jnp.where(row == tm - 2, nxt[0:1, :], jnp.where(row == tm - 1, nxt[1:2, :], dp2))
        du = w[2:3, :] * dyv + w[1:2, :] * dp1 + w[0:1, :] * dp2
        dp_ref[:, c:2 * c] = _mx(du * xa)
        dp_ref[:, 2 * c:3 * c] = _mx(du * cg)
        dw_ref[0] += _row_fold(dyv * um2)
        dw_ref[1] += _row_fold(dyv * um1)
        dw_ref[2] += _row_fold(dyv * u)
        dw_ref[3] += _row_fold(dyv)

    blk = lambda col: pl.BlockSpec((tm, c), lambda i: (i, col))
    halo = lambda col: pl.BlockSpec((HALO, c), lambda i: (jnp.maximum(i * hb - 1, 0), col))
    nhalo = lambda col: pl.BlockSpec((HALO, c), lambda i: (jnp.minimum((i + 1) * hb, s // HALO - 1), col))
    return pl.pallas_call(
        body,
        name=name,
        grid=(s // tm,),
        in_specs=[blk(0), blk(1), blk(2), halo(1), halo(2), blk(0), nhalo(0), nhalo(0),
                  pl.BlockSpec((CONV_K, c), lambda i: (0, 0)), pl.BlockSpec((1, c), lambda i: (0, 0))],
        out_specs=[pl.BlockSpec((tm, 3 * c), lambda i: (i, 0)), pl.BlockSpec((4, 8, c), lambda i: (0, 0, 0))],
        out_shape=[jax.ShapeDtypeStruct((s, 3 * c), MXU_DTYPE), jax.ShapeDtypeStruct((4, 8, c), F32)],
        compiler_params=_params(),
    )(p, p, p, p, p, dya, dya, p, conv_w, conv_b)


def _sgu_bwd(p, dyb, ln_g, ln_b, wtril, wtril_t, bias_full, *, col0, name, tm=512):
    s = p.shape[0]
    c = ln_g.shape[1]
    groups = c // CHUNK
    tm = _tile(s, tm)

    def body(u_ref, v_ref, dy_ref, g_ref, b_ref, w_ref, wt_ref, bias_ref, duv_ref, dln_ref, dw_ref, dbias_ref, dvn_ref):
        @pl.when(pl.program_id(0) == 0)
        def _():
            dln_ref[...] = jnp.zeros_like(dln_ref)
            dw_ref[...] = jnp.zeros_like(dw_ref)
            dbias_ref[...] = jnp.zeros_like(dbias_ref)

        sv = _f32(v_ref)
        xhat, r = _layernorm_stats(_gelu(sv))
        vb = _mx(xhat * g_ref[...] + b_ref[...])
        for n in range(tm // CHUNK):
            rows = slice(n * CHUNK, (n + 1) * CHUNK)
            for gi in range(groups):
                cols = slice(gi * CHUNK, (gi + 1) * CHUNK)
                su = u_ref[rows, cols].astype(F32)
                dy = dy_ref[rows, cols]
                vblk = vb[rows, cols]
                mixed = _dot(w_ref[gi], vblk) + bias_ref[:, cols]
                duv_ref[rows, cols] = _mx(dy * mixed * _gelu_grad(su))
                dmixed = dy * _gelu(su)
                dmb = _mx(dmixed)
                dvn_ref[rows, cols] = _dot(wt_ref[gi], dmb)
                dw_ref[gi] += _dot_nt(dmb, vblk)
                dbias_ref[:, cols] += dmixed
        dvn = dvn_ref[...]
        dln_ref[0] += _row_fold(dvn * xhat)
        dln_ref[1] += _row_fold(dvn)
        dxh = dvn * g_ref[...]
        dgv = r * (dxh - jnp.mean(dxh, axis=-1, keepdims=True) - xhat * jnp.mean(dxh * xhat, axis=-1, keepdims=True))
        duv_ref[:, c:2 * c] = _mx(dgv * _gelu_grad(sv))

    blk = lambda col: pl.BlockSpec((tm, c), lambda i: (i, col))
    vec = pl.BlockSpec((1, c), lambda i: (0, 0))
    wspec = pl.BlockSpec((groups, CHUNK, CHUNK), lambda i: (0, 0, 0))
    return pl.pallas_call(
        body,
        name=name,
        grid=(s // tm,),
        in_specs=[blk(col0), blk(col0 + 1), blk(0), vec, vec, wspec, wspec, pl.BlockSpec((CHUNK, c), lambda i: (0, 0))],
        out_specs=[pl.BlockSpec((tm, 2 * c), lambda i: (i, 0)), pl.BlockSpec((2, 8, c), lambda i: (0, 0, 0)), wspec,
                   pl.BlockSpec((CHUNK, c), lambda i: (0, 0))],
        out_shape=[jax.ShapeDtypeStruct((s, 2 * c), MXU_DTYPE),
                   jax.ShapeDtypeStruct((2, 8, c), F32), jax.ShapeDtypeStruct((groups, CHUNK, CHUNK), F32),
                   jax.ShapeDtypeStruct((CHUNK, c), F32)],
        scratch_shapes=[pltpu.VMEM((tm, c), F32)],
        compiler_params=_params(),
    )(p, p, dyb, ln_g, ln_b, wtril, wtril_t, bias_full)


def _att_bwd(kt_hm, vt_hm, qt_hm, dot_hm, runs, later, earlier, *, name, hg=4, rider=None):
    h, nb, d, t = kt_hm.shape
    s = nb * t
    nbp = runs.shape[2]
    scale = 1.0 / math.sqrt(d)
    assert h % hg == 0

    def body(kt_ref, vt_ref, qt_ref, dot_ref, runs_ref, later_ref, earlier_ref, dqt_ref, dkt_ref, dvt_ref):
        i = pl.program_id(1)

        @pl.when(i == 0)
        def _():
            dkt_ref[...] = jnp.zeros_like(dkt_ref)
            dvt_ref[...] = jnp.zeros_like(dvt_ref)

        q_ts = [qt_ref[g] for g in range(hg)]
        do_ts = [dot_ref[g] for g in range(hg)]

        best = runs_ref[0]
        for g in range(1, hg):
            best = jnp.maximum(best, runs_ref[g])
        row = lax.broadcasted_iota(jnp.int32, (nbp, 1), 0)
        counts = jnp.logical_and(jnp.max(best, axis=1, keepdims=True) > EXP_UNDERFLOW, row < i)
        seen = jnp.sum(counts.astype(jnp.int32))
        zeros_row = tuple(jnp.zeros((1, t), F32) for _ in range(hg))

        def up(j, carry):
            gsums, dqts = carry
            mask = _att_mask(t, j * t, i * t)
            heads = range(hg)
            zs, es, lbs, l1ms = _att_blocks([kt_ref[g, j] for g in heads], q_ts, scale, mask)
            afters = _dot_split_each(later_ref[...], l1ms)
            das = [_dot_tn(vt_ref[g, j], do_ts[g]) for g in heads]
            weights = [jnp.where(mask, jnp.exp(lbs[g] + afters[g] + _row_get(runs_ref, g, j)), 0.0) for g in heads]
            grs = [das[g] * weights[g] for g in heads]
            gbefores = _dot_split_each(earlier_ref[...], grs)
            dzs = []
            for g in heads:
                inv = 1.0 / (1.0 + es[g])
                pos = zs[g] >= 0.0
                beta = jnp.where(pos, inv, es[g] * inv)
                omb = jnp.where(pos, es[g] * inv, inv)
                dzs.append(_mx(jnp.where(mask, grs[g] * omb - (gbefores[g] + gsums[g]) * beta, 0.0) * scale))
            new_dqts = [dqts[g] + _dot(kt_ref[g, j], dzs[g]) for g in heads]
            for g in heads:
                dkt_ref[g, j] += _dot_nt(q_ts[g], dzs[g])
            for g in heads:
                dvt_ref[g, j] += _dot_nt(do_ts[g], _mx(weights[g]))
            new_gsums = [gsums[g] + jnp.sum(grs[g], axis=0, keepdims=True) for g in heads]
            return tuple(new_gsums), tuple(new_dqts)

        _, dqts = lax.fori_loop(i - seen, i + 1, up, (zeros_row, tuple(jnp.zeros((d, t), F32) for _ in range(hg))))
        for g in range(hg):
            dqt_ref[g] = dqts[g]

    whole = pl.BlockSpec((hg, nb, d, t), lambda hh, i: (hh, 0, 0, 0))
    cols = pl.BlockSpec((hg, d, t), lambda hh, i: (hh, 0, i))
    tri = pl.BlockSpec((t, t), lambda hh, i: (0, 0))
    return _call(
        body, (kt_hm, vt_hm, qt_hm, dot_hm, runs, later, earlier),
        name=name,
        grid=(h // hg, nb),
        in_specs=[whole, whole, cols, cols, pl.BlockSpec((hg, None, nbp, t), lambda hh, i: (hh, i, 0, 0)), tri, tri],
        out_specs=[cols, whole, whole],
        out_shape=[jax.ShapeDtypeStruct((h, d, s), F32), jax.ShapeDtypeStruct((h, nb, d, t), F32),
                   jax.ShapeDtypeStruct((h, nb, d, t), F32)],
        rider=rider,
    )


def _qkv_bwd(p, dqt, dkt, dvt, qg, kg, headavg, *, col0, name, tm=512):
    s = p.shape[0]
    c = qg.shape[1]
    tm = _tile(s, tm)

    t = ATT_BLOCK
    assert tm % t == 0

    def rows_of(blocked_ref):
        parts = []
        for b in range(tm // t):
            cols = jnp.concatenate([blocked_ref[hh, b] for hh in range(HEADS)], axis=0)
            parts.append(cols.T)
        return jnp.concatenate(parts, axis=0)

    def body(q_ref, k_ref, dqt_ref, dkt_ref, dvt_ref, qg_ref, kg_ref, avg_ref, dqkv_ref, dg_ref):
        @pl.when(pl.program_id(0) == 0)
        def _():
            dg_ref[...] = jnp.zeros_like(dg_ref)

        dqn = jnp.concatenate([dqt_ref[hh] for hh in range(HEADS)], axis=0).T
        for n, (src, dn, g_ref) in enumerate(((q_ref, dqn, qg_ref), (k_ref, rows_of(dkt_ref), kg_ref))):
            xf = _f32(src)
            r = lax.rsqrt(_head_mean(xf * xf, avg_ref[...]) + EPS)
            y = xf * r
            dy = dn * g_ref[...]
            dqkv_ref[:, n * c:(n + 1) * c] = _mx(r * (dy - y * _head_mean(dy * y, avg_ref[...])))
            dg_ref[n] += _row_fold(dn * y)
        dqkv_ref[:, 2 * c:3 * c] = _mx(rows_of(dvt_ref))

    blk = lambda col: pl.BlockSpec((tm, c), lambda i: (i, col))
    vec = pl.BlockSpec((1, c), lambda i: (0, 0))
    blocked = pl.BlockSpec((HEADS, tm // t, HEAD_DIM, t), lambda i: (0, i, 0, 0))
    return pl.pallas_call(
        body,
        name=name,
        grid=(s // tm,),
        in_specs=[blk(col0), blk(col0 + 1), pl.BlockSpec((HEADS, HEAD_DIM, tm), lambda i: (0, 0, i)), blocked, blocked,
                  vec, vec, pl.BlockSpec((c, c), lambda i: (0, 0))],
        out_specs=[pl.BlockSpec((tm, 3 * c), lambda i: (i, 0)), pl.BlockSpec((2, 8, c), lambda i: (0, 0, 0))],
        out_shape=[jax.ShapeDtypeStruct((s, 3 * c), MXU_DTYPE), jax.ShapeDtypeStruct((2, 8, c), F32)],
        compiler_params=_params(),
    )(p, p, dqt, dkt, dvt, qg, kg, headavg)


def _place():
    x, y, c = lax.axis_index("x"), lax.axis_index("y"), lax.axis_index("c")
    chips = [(1 - x, y), (x, 1 - y), (1 - x, 1 - y)]
    return x, y, c, chips


def _shard_of(ref, axis, chip, width):
    idx = [slice(None)] * len(ref.shape)
    idx[axis] = pl.ds(chip * width, width)
    return ref.at[tuple(idx)]


def _place_shard(w, layer, axis, chip_idx, dtype, *, name):
    rows, cols = _as_rows(w.shape[1:])
    tr = _row_tile(rows, cols, 2)
    if axis == len(w.shape) - 2:
        out_shape = (1, rows, cols * N_CHIPS)
        out_spec = pl.BlockSpec((None, tr, cols), lambda i, j_ref: (0, i, j_ref[0]))
    else:
        assert axis == 0
        per = rows // tr
        out_shape = (1, rows * N_CHIPS, cols)
        out_spec = pl.BlockSpec((None, tr, cols), lambda i, j_ref: (0, j_ref[0] * per + i, 0))
    full = [1, *w.shape[1:]]
    full[1 + axis] *= N_CHIPS

    def body(j_ref, w_ref, o_ref):
        o_ref[...] = w_ref[...].astype(dtype)

    out = pl.pallas_call(
        body,
        name=name,
        grid_spec=pltpu.PrefetchScalarGridSpec(
            num_scalar_prefetch=1,
            grid=(rows // tr,),
            in_specs=[pl.BlockSpec((None, tr, cols), lambda i, j_ref: (layer, i, 0))],
            out_specs=out_spec,
        ),
        out_shape=jax.ShapeDtypeStruct(out_shape, dtype),
        compiler_params=_params(),
    )(chip_idx, w.reshape(w.shape[0], rows, cols))
    return out.reshape(full)


def _gather_riders(placed, axes, layer):
    n = len(placed)
    widths = [pa.shape[1 + ax] // N_CHIPS for pa, ax in zip(placed, axes)]

    def copies(refs, sems):
        x, y, c, chips = _place()

        def block(a, chip):
            return _shard_of(refs[a].at[0], axes[a], chip, widths[a])

        def over_ici(a, k, chip):
            return pltpu.make_async_remote_copy(
                src_ref=block(a, chip), dst_ref=block(a, chip), send_sem=sems[0].at[a, k], recv_sem=sems[1].at[a, k],
                device_id=(*chips[k], c), device_id_type=MESH)

        def to_sibling(a, k):
            cx, cy = chips[k]
            return pltpu.make_async_remote_copy(
                src_ref=block(a, 2 * cx + cy), dst_ref=block(a, 2 * cx + cy), send_sem=sems[0].at[a, k],
                recv_sem=sems[1].at[a, k], device_id=(x, y, 1 - c), device_id_type=MESH)

        return 2 * x + y, c, chips, over_ici, to_sibling

    def ici_start(refs, sems):
        me, c, _, over_ici, _ = copies(refs, sems)

        @pl.when(c == layer)
        def _():
            for a in range(n):
                for k in range(3):
                    over_ici(a, k, me).start()

    def ici_finish(refs, sems):
        me, c, chips, over_ici, _ = copies(refs, sems)

        @pl.when(c == layer)
        def _():
            for a in range(n):
                for k in range(3):
                    cx, cy = chips[k]
                    over_ici(a, k, 2 * cx + cy).wait_recv()
            for a in range(n):
                for k in range(3):
                    over_ici(a, k, me).wait_send()

    def d2d_start(refs, sems):
        _, c, _, _, to_sibling = copies(refs, sems)

        @pl.when(c == layer)
        def _():
            for a in range(n):
                for k in range(3):
                    to_sibling(a, k).start()

    def d2d_finish(refs, sems):
        _, c, _, _, to_sibling = copies(refs, sems)

        @pl.when(c == layer)
        def _():
            for a in range(n):
                for k in range(3):
                    to_sibling(a, k).wait_send()

        @pl.when(c != layer)
        def _():
            for a in range(n):
                for k in range(3):
                    to_sibling(a, k).wait_recv()

    def both_start(refs, sems):
        ici_start(refs, sems[:2])

    def both_finish(refs, sems):
        me, c, chips, over_ici, _ = copies(refs, sems[:2])
        to_sibling = copies(refs, sems[2:])[4]

        @pl.when(c == layer)
        def _():
            for a in range(n):
                for k in range(3):
                    cx, cy = chips[k]
                    over_ici(a, k, 2 * cx + cy).wait_recv()
                    to_sibling(a, k).start()
            for a in range(n):
                for k in range(3):
                    over_ici(a, k, me).wait_send()
                    to_sibling(a, k).wait_send()

        @pl.when(c != layer)
        def _():
            for a in range(n):
                for k in range(3):
                    to_sibling(a, k).wait_recv()

    arrays, sems = tuple(placed), ((n, 3), (n, 3))
    return (_Rider(arrays, sems, ici_start, ici_finish), _Rider(arrays, sems, d2d_start, d2d_finish),
            _Rider(arrays, sems * 2, both_start, both_finish))


def _join_riders(*riders):
    def each(which):
        def run(refs, sems):
            for r in riders:
                getattr(r, which)(refs[:len(r.arrays)], sems[:len(r.sems)])
                refs, sems = refs[len(r.arrays):], sems[len(r.sems):]
        return run

    return _Rider(sum((r.arrays for r in riders), ()), sum((r.sems for r in riders), ()), each("start"), each("finish"))


def _exchange_rider(grads, others, layer):
    n = len(grads)

    def copy(refs, sems, a):
        x, y, c, _ = _place()
        return pltpu.make_async_remote_copy(
            src_ref=refs[a], dst_ref=refs[n + a], send_sem=sems[0].at[a], recv_sem=sems[1].at[a],
            device_id=(x, y, 1 - c), device_id_type=MESH)

    def start(refs, sems):
        @pl.when(lax.axis_index("c") != layer)
        def _():
            for a in range(n):
                copy(refs, sems, a).start()

    def finish(refs, sems):
        @pl.when(lax.axis_index("c") != layer)
        def _():
            for a in range(n):
                copy(refs, sems, a).wait_send()

        @pl.when(lax.axis_index("c") == layer)
        def _():
            for a in range(n):
                copy(refs, sems, a).wait_recv()

    return _Rider((*grads, *others), ((n,), (n,)), start, finish)


def _scatter_rider(wires, landeds, axes, layer):
    n = len(wires)
    widths = [wr.shape[ax] // N_CHIPS for wr, ax in zip(wires, axes)]

    def copies(refs, sems):
        _, _, c, chips = _place()
        return [pltpu.make_async_remote_copy(
            src_ref=_shard_of(refs[a], axes[a], 2 * cx + cy, widths[a]), dst_ref=refs[n + a].at[k],
            send_sem=sems[0].at[a, k], recv_sem=sems[1].at[a, k], device_id=(cx, cy, c), device_id_type=MESH)
            for a in range(n) for k, (cx, cy) in enumerate(chips)]

    def start(refs, sems):
        @pl.when(lax.axis_index("c") == layer)
        def _():
            for cp in copies(refs, sems):
                cp.start()

    def finish(refs, sems):
        @pl.when(lax.axis_index("c") == layer)
        def _():
            for cp in copies(refs, sems):
                cp.wait()

    return _Rider((*wires, *landeds), ((n, 3), (n, 3)), start, finish)


def _share_rider(finals, layer):
    n = len(finals)

    def copy(refs, sems, a):
        x, y, c, _ = _place()
        return pltpu.make_async_remote_copy(
            src_ref=refs[a].at[layer], dst_ref=refs[a].at[layer], send_sem=sems[0].at[a], recv_sem=sems[1].at[a],
            device_id=(x, y, 1 - c), device_id_type=MESH)

    def start(refs, sems):
        @pl.when(lax.axis_index("c") == layer)
        def _():
            for a in range(n):
                copy(refs, sems, a).start()

    def finish(refs, sems):
        @pl.when(lax.axis_index("c") == layer)
        def _():
            for a in range(n):
                copy(refs, sems, a).wait_send()

        @pl.when(lax.axis_index("c") != layer)
        def _():
            for a in range(n):
                copy(refs, sems, a).wait_recv()

    return _Rider(tuple(finals), ((n,), (n,)), start, finish)


def _small_view(shape):
    size = math.prod(shape)
    return (size // 128, 128) if size % 1024 == 0 else (shape[0], size // shape[0])


def _all_reduce_small(parts, *, name):
    n = len(parts)
    views = [_small_view(a.shape) for a in parts]

    def body(*refs):
        ins, outs, slots = refs[:n], refs[n:2 * n], refs[2 * n:3 * n]
        send_sem, recv_sem = refs[3 * n:]
        x, y, c, _ = _place()
        me = 4 * x + 2 * y + c
        copies = []
        for a in range(n):
            slots[a][me] = ins[a][...]
            for k in range(1, N_DEV):
                peer = (x ^ (k >> 2), y ^ ((k >> 1) & 1), c ^ (k & 1))
                copies.append(pltpu.make_async_remote_copy(
                    src_ref=ins[a], dst_ref=slots[a].at[me], send_sem=send_sem.at[a, k - 1],
                    recv_sem=recv_sem.at[a, k - 1], device_id=peer, device_id_type=MESH))
        for cp in copies:
            cp.start()
        for a in range(n):
            for k in range(1, N_DEV):
                pltpu.make_async_remote_copy(
                    src_ref=ins[a], dst_ref=slots[a].at[me ^ k], send_sem=send_sem.at[a, k - 1],
                    recv_sem=recv_sem.at[a, k - 1], device_id=(x, y, c), device_id_type=MESH).wait()
        for a in range(n):
            total = slots[a][0]
            for dev in range(1, N_DEV):
                total = total + slots[a][dev]
            outs[a][...] = total

    vmem = pl.BlockSpec(memory_space=pltpu.VMEM)
    outs = pl.pallas_call(
        body,
        name=name,
        in_specs=[vmem] * n,
        out_specs=[vmem] * n,
        out_shape=[jax.ShapeDtypeStruct(view, F32) for view in views],
        scratch_shapes=[pltpu.VMEM((N_DEV, *view), F32) for view in views]
        + [pltpu.SemaphoreType.DMA((n, N_DEV - 1)), pltpu.SemaphoreType.DMA((n, N_DEV - 1))],
        compiler_params=_params(),
    )(*[a.reshape(view) for a, view in zip(parts, views)])
    return [o.reshape(a.shape) for o, a in zip(outs, parts)]


def _as_rows(shape):
    cols = shape[-1]
    return math.prod(shape[:-1]), cols


ELEMENTWISE_VMEM = 24 * 1024 * 1024


def _row_tile(rows, cols, n_arrays, sublanes=8):
    cap = ELEMENTWISE_VMEM // (n_arrays * 2 * 4 * cols)
    best = None
    for t in range(sublanes, min(rows, cap) + 1, sublanes):
        if rows % t == 0:
            best = t
    assert best is not None, (rows, cols)
    return best


def _pair_sum(g, other, active, *, name):
    rows, cols = _as_rows(g.shape)
    tr = _row_tile(rows, cols, 4, sublanes=16)

    def body(a_ref, g_ref, o_ref, out_ref, wire_ref):
        @pl.when(a_ref[0] == 1)
        def _():
            total = g_ref[...] + o_ref[...]
            out_ref[...] = total
            wire_ref[...] = total.astype(WIRE_DTYPE)

    blk = pl.BlockSpec((tr, cols), lambda i, a_ref: (i * a_ref[0], 0))
    out, wire = pl.pallas_call(
        body,
        name=name,
        grid_spec=pltpu.PrefetchScalarGridSpec(
            num_scalar_prefetch=1, grid=(rows // tr,), in_specs=[blk, blk], out_specs=[blk, blk]),
        out_shape=[jax.ShapeDtypeStruct((rows, cols), F32), jax.ShapeDtypeStruct((rows, cols), WIRE_DTYPE)],
        compiler_params=_params(),
    )(active, g.reshape(rows, cols), other.reshape(rows, cols))
    return out.reshape(g.shape), wire.reshape(g.shape)


def _chip_sum(mine, got, axis, chip_idx, active, layer, layers, into, *, name):
    shard_shape = got.shape[1:]
    rows, cols = _as_rows(shard_shape)
    tr = _row_tile(rows, cols, 5, sublanes=16)
    if axis == len(mine.shape) - 1:
        m2 = mine.reshape(rows, cols * N_CHIPS)
        mine_spec = pl.BlockSpec((tr, cols), lambda i, j_ref, a_ref: (i * a_ref[0], j_ref[0]))
    else:
        assert axis == 0
        m2 = mine.reshape(N_CHIPS, rows, cols)
        mine_spec = pl.BlockSpec((None, tr, cols), lambda i, j_ref, a_ref: (j_ref[0], i * a_ref[0], 0))

    def body(j_ref, a_ref, m_ref, got_ref, *refs):
        @pl.when(a_ref[0] == 1)
        def _():
            refs[-1][...] = ((m_ref[...] + got_ref[0].astype(F32)) + got_ref[1].astype(F32)) + got_ref[2].astype(F32)

    out = pl.pallas_call(
        body,
        name=name,
        grid_spec=pltpu.PrefetchScalarGridSpec(
            num_scalar_prefetch=2,
            grid=(rows // tr,),
            in_specs=[mine_spec, pl.BlockSpec((3, tr, cols), lambda i, j_ref, a_ref: (0, i * a_ref[0], 0))]
            + ([] if into is None else [ANY]),
            out_specs=pl.BlockSpec((None, tr, cols), lambda i, j_ref, a_ref: (layer, i * a_ref[0], 0)),
        ),
        out_shape=jax.ShapeDtypeStruct((layers, rows, cols), F32),
        input_output_aliases={} if into is None else {4: 0},
        compiler_params=_params(),
    )(chip_idx, active, m2, got.reshape(3, rows, cols), *([] if into is None else [into.reshape(layers, rows, cols)]))
    return out.reshape((layers, *shard_shape))


class _ReduceChain:
    def __init__(self, grads, layer, layers, chip_idx, core_idx, finals):
        self.grads, self.layer, self.layers, self.chip_idx, self.finals = list(grads), layer, layers, chip_idx, finals
        self.active = (core_idx == layer).astype(jnp.int32)
        self.axes = [BIG_AXIS[n] for n in BIG]
        self.pairs = None

    def exchange(self):
        return _exchange_rider(self.grads, [lax.empty(g.shape, F32) for g in self.grads], self.layer)

    def exchanged(self, arrived):
        n = len(self.grads)
        self.pairs = [_pair_sum(g, o, self.active, name=f"pair_sum_{name}_l{self.layer}")
                      for name, g, o in zip(BIG, arrived[:n], arrived[n:])]

    def scatter(self):
        landeds = []
        for (_, wire), ax in zip(self.pairs, self.axes):
            shard = list(wire.shape)
            shard[ax] //= N_CHIPS
            landeds.append(lax.empty((3, *shard), WIRE_DTYPE))
        return _scatter_rider([wire for _, wire in self.pairs], landeds, self.axes, self.layer)

    def scattered(self, arrived):
        n = len(self.grads)
        into = self.finals or [None] * n
        self.finals = [_chip_sum(pair, landed, ax, self.chip_idx, self.active, self.layer, self.layers, old,
                                 name=f"chip_sum_{name}_l{self.layer}")
                       for name, (pair, _), landed, ax, old in zip(BIG, self.pairs, arrived[n:], self.axes, into)]

    def share(self):
        return _share_rider(self.finals, self.layer)

    def shared(self, arrived):
        self.finals = list(arrived)


def _adamw_update(w_ref, g_ref, m_ref, v_ref, d_ref, nm_ref, nv_ref):
    c1 = 1.0 / (1.0 - ADAM_B1 ** ADAM_STEP)
    c2 = 1.0 / (1.0 - ADAM_B2 ** ADAM_STEP)
    gg = g_ref[...]
    nm = ADAM_B1 * m_ref[...] + (1.0 - ADAM_B1) * gg
    nv = ADAM_B2 * v_ref[...] + (1.0 - ADAM_B2) * (gg * gg)
    nm_ref[...] = nm
    nv_ref[...] = nv
    d_ref[...] = -ADAM_LR * ((nm * c1) / (jnp.sqrt(nv * c2) + ADAM_EPS) + ADAM_WD * w_ref[...])


def _adamw_small(ws, gs, ms, vs, *, name):
    n = len(ws)
    views = [_small_view(a.shape) for a in ws]

    def body(*refs):
        w_r, g_r, m_r, v_r = (refs[k * n:(k + 1) * n] for k in range(4))
        d_r, nm_r, nv_r = (refs[(4 + k) * n:(5 + k) * n] for k in range(3))
        for a in range(n):
            _adamw_update(w_r[a], g_r[a], m_r[a], v_r[a], d_r[a], nm_r[a], nv_r[a])

    vmem = pl.BlockSpec(memory_space=pltpu.VMEM)
    flat = lambda arrs: [a.reshape(view) for a, view in zip(arrs, views)]
    outs = pl.pallas_call(
        body,
        name=name,
        in_specs=[vmem] * (4 * n),
        out_specs=[vmem] * (3 * n),
        out_shape=[jax.ShapeDtypeStruct(view, F32) for view in views] * 3,
        compiler_params=_params(),
    )(*flat(ws), *flat(gs), *flat(ms), *flat(vs))
    return [tuple(outs[k * n + a].reshape(ws[a].shape) for k in range(3)) for a in range(n)]


def _adamw(w, g, m, v, *, name):
    shape = w.shape
    rows, cols = _as_rows(shape)
    tr = _row_tile(rows, cols, 7)
    body = functools.partial(_adamw_update)

    blk = pl.BlockSpec((tr, cols), lambda i: (i, 0))
    flat = lambda a: a.reshape(rows, cols)
    outs = pl.pallas_call(
        body,
        name=name,
        grid=(rows // tr,),
        in_specs=[blk] * 4,
        out_specs=[blk] * 3,
        out_shape=[jax.ShapeDtypeStruct((rows, cols), F32)] * 3,
        compiler_params=_params(),
    )(flat(w), flat(g), flat(m), flat(v))
    return tuple(o.reshape(shape) for o in outs)


BIG = ("w_in", "w_branch_out", "w_o", "w_gate_up", "w_down")
BIG_AXIS = {"w_in": 1, "w_branch_out": 2, "w_o": 0, "w_gate_up": 1, "w_down": 0}
SMALL = ("mix_norm_g", "b_gate", "conv_w", "conv_b", "sgu_ln_g", "sgu_ln_b", "sgu_w", "sgu_b", "q_norm_g", "k_norm_g",
         "ffn_norm_g")
ORDER = ("mix_norm_g", "w_in", "b_gate", "conv_w", "conv_b", "sgu_ln_g", "sgu_ln_b", "sgu_w", "sgu_b", "q_norm_g",
         "k_norm_g", "w_branch_out", "w_o", "ffn_norm_g", "w_gate_up", "w_down")


def _layer_forward(x, w, wl, l, coming):
    n_in = wl["w_in"].shape[2]
    gate_col0 = (n_in - 3 * x.shape[1]) // x.shape[1]
    tag = f"l{l}"
    rider = None
    if coming:
        placed, axes, nxt = coming
        half = len(placed) // 2
        first, second = (placed[:half], axes[:half], nxt), (placed[half:], axes[half:], nxt)
        rider = _gather_riders(*first)[0]
    (p, h), arrived = _norm_matmul(x, w["mix_norm_g"][l][None], wl["w_in"], 0, name=f"in_proj_{tag}", tn=n_in // 4,
                                   rider=rider)
    ya = _conv_fwd(p, wl["conv_w"][0], w["conv_b"][l][None], name=f"conv_{tag}")
    yb = _sgu_fwd(p, w["sgu_ln_g"][l][None], w["sgu_ln_b"][l][None], w["wtril"][l], w["bias_full"][l], col0=3,
                  name=f"sgu_{tag}")
    q_c, k_cb, v_cb = _qkv_prep(p, w["qg"][l], w["kg"][l], w["headavg"], col0=5, name=f"qkv_{tag}")
    if coming:
        rider = _join_riders(_gather_riders(list(arrived), *first[1:])[1], _gather_riders(*second)[0])
    (out_t, runs), arrived = _att_fwd(k_cb, q_c, v_cb, w["later"], name=f"att_{tag}", rider=rider)
    x1, merged = _merge_fwd((ya, yb, out_t), p, w["b_gate"][l][None], wl["w_branch_out"], wl["w_o"], 0, x,
                            gate_col0=gate_col0, name=f"merge_{tag}")
    if coming:
        here = arrived[:half]
        rider = _gather_riders(list(arrived[half:]), *second[1:])[1]
    (gu, h2), arrived = _norm_matmul(x1, w["ffn_norm_g"][l][None], wl["w_gate_up"], 0, name=f"gate_up_{tag}",
                                     tn=wl["w_gate_up"].shape[2] // 4, rider=rider)
    if coming:
        arrived = (*here, *arrived)
    x2 = _ffn_down(gu, wl["w_down"], 0, x1, name=f"down_{tag}")
    saved = dict(x=x, p=p, h=h, ya=ya, yb=yb, yct=out_t, merged=merged, x1=x1, gu=gu, h2=h2,
                 att=(k_cb, v_cb, q_c, runs), gate_col0=gate_col0)
    return x2, saved, arrived


def _layer_backward(dx2, w, wl, sv, l, chain, make_chain, last):
    tag = f"l{l}"
    c = w["conv_b"].shape[1]
    n_in = wl["w_in"].shape[2]
    n_ff = wl["w_gate_up"].shape[2]
    g, big = {}, {}
    (dgu, act, dx2b), arrived = _ffn_bwd(dx2, sv["gu"], wl["w_down"], 0, name=f"down_bwd_{tag}",
                                         rider=chain.exchange() if chain else None)
    if chain:
        chain.exchanged(arrived)
    big["w_down"] = _matmul_tn(act, dx2b, 0, 1, None, name=f"dw_down_{tag}", t1=act.shape[1] // 2)
    (dx1, dg2), _ = _matmul_nt_normbwd([dgu], wl["w_gate_up"], 0, sv["x1"], w["ffn_norm_g"][l][None], dx2,
                                       name=f"gate_up_bwd_{tag}")
    g["ffn_norm_g"] = jnp.sum(dg2, axis=0)
    big["w_gate_up"] = _matmul_tn(sv["h2"], dgu, 0, 1, None, name=f"dw_gate_up_{tag}", t1=512)
    ys = (sv["ya"], sv["yb"], sv["yct"])
    (dgates, dya, dyb, dyct, dd0, dd1, dd2, dx1b, dbg) = _merge_bwd(
        dx1, ys, sv["p"], w["b_gate"][l][None], wl["w_branch_out"], wl["w_o"], 0, gate_col0=sv["gate_col0"],
        name=f"merge_bwd_{tag}")
    g["b_gate"] = jnp.sum(dbg, axis=0)
    big["w_o"] = _matmul_tn(sv["merged"], dx1b, 0, 1, None, name=f"dw_o_{tag}")
    for i, (y, dd) in enumerate(zip(ys, (dd0, dd1, dd2))):
        heads_first = y.ndim == 3
        big["w_branch_out"] = _matmul_tn(y.reshape(c, -1) if heads_first else y, dd, i, len(ys), big.get("w_branch_out"),
                                         name=f"dw_bo{i}_{tag}", a_transposed=heads_first)
    dconv, dwc = _conv_bwd(sv["p"], dya, wl["conv_w"][0], w["conv_b"][l][None], name=f"conv_bwd_{tag}")
    dwc = jnp.sum(dwc, axis=1)
    g["conv_w"] = dwc[0:CONV_K]
    g["conv_b"] = dwc[CONV_K]
    dsgu, dln, dws, dbias = _sgu_bwd(sv["p"], dyb, w["sgu_ln_g"][l][None], w["sgu_ln_b"][l][None], w["wtril"][l],
                                     w["wtril_t"][l], w["bias_full"][l], col0=3, name=f"sgu_bwd_{tag}")
    dln = jnp.sum(dln, axis=1)
    g["sgu_ln_g"], g["sgu_ln_b"] = dln[0], dln[1]
    g["sgu_w"] = jnp.where(w["tril"], dws, 0.0)
    g["sgu_b"] = jnp.sum(dbias.reshape(CHUNK, c // CHUNK, CHUNK), axis=2).T
    k_cb, v_cb, q_c, runs = sv["att"]
    early = [big[n].reshape(wl[n].shape[1:]) for n in BIG[1:]] if last and chain else []
    riders = ([chain.scatter()] if chain else []) + (
        [_exchange_rider(early, [lax.empty(a.shape, F32) for a in early], l)] if early else [])
    (dqt, dkt, dvt), arrived = _att_bwd(k_cb, v_cb, q_c, dyct, runs, w["later"], w["earlier"],
                                        name=f"att_bwd_{tag}", rider=_join_riders(*riders) if riders else None)
    if chain:
        count = 2 * len(chain.grads)
        chain.scattered(arrived[:count])
        early = arrived[count:]
    dqkv, dqkg = _qkv_bwd(sv["p"], dqt, dkt, dvt, w["qg"][l], w["kg"][l], w["headavg"], col0=5, name=f"qkv_bwd_{tag}")
    dqkg = jnp.sum(dqkg.reshape(2, 8 * HEADS, HEAD_DIM), axis=1)
    g["q_norm_g"], g["k_norm_g"] = dqkg[0], dqkg[1]
    pieces = [dconv, dsgu, dqkv, dgates]
    col0 = 0
    for k, pc in enumerate(pieces):
        width = pc.shape[1]
        big["w_in"] = _matmul_tn(sv["h"], pc, 0, 1, big.get("w_in"), name=f"dw_in{k}_{tag}", col0=col0, n_total=n_in,
                                 tn=math.gcd(col0, width) if col0 else width)
        col0 += width
    big = {n: a.reshape(wl[n].shape[1:]) for n, a in big.items()}
    own = make_chain([big[n] for n in BIG], l)
    riders = [chain.share()] if chain else []
    if last and early:
        rest = len(early) // 2
        first = _ride_alone([_exchange_rider([big[BIG[0]]], [lax.empty(big[BIG[0]].shape, F32)], l)],
                            name=f"grads_pair_exchange_{tag}")
        own.exchanged((first[0], *early[:rest], first[1], *early[rest:]))
        riders.append(own.scatter())
    elif last:
        own.exchanged(_ride_alone([own.exchange()], name=f"grads_pair_exchange_{tag}"))
        riders.append(own.scatter())
    (dx0, dg1), arrived = _matmul_nt_normbwd(pieces, wl["w_in"], 0, sv["x"], w["mix_norm_g"][l][None], dx1,
                                             name=f"in_proj_bwd_{tag}", rider=_join_riders(*riders) if riders else None)
    if chain:
        count = len(chain.finals)
        chain.shared(arrived[:count])
        arrived = arrived[count:]
        own.finals = chain.finals
    if last:
        own.scattered(arrived)
        own.shared(_ride_alone([own.share()], name=f"grads_pair_share_{tag}"))
    g["mix_norm_g"] = jnp.sum(dg1, axis=0)
    return dx0, g, own


def kernel(x, mix_norm_g, w_in, b_gate, conv_w, conv_b, sgu_ln_g, sgu_ln_b, sgu_w, sgu_b, q_norm_g, k_norm_g, w_branch_out, w_o, ffn_norm_g, w_gate_up, w_down, loss_target, m_mix_norm_g, m_w_in, m_b_gate, m_conv_w, m_conv_b, m_sgu_ln_g, m_sgu_ln_b, m_sgu_w, m_sgu_b, m_q_norm_g, m_k_norm_g, m_w_branch_out, m_w_o, m_ffn_norm_g, m_w_gate_up, m_w_down, v_mix_norm_g, v_w_in, v_b_gate, v_conv_w, v_conv_b, v_sgu_ln_g, v_sgu_ln_b, v_sgu_w, v_sgu_b, v_q_norm_g, v_k_norm_g, v_w_branch_out, v_w_o, v_ffn_norm_g, v_w_gate_up, v_w_down):
    given = dict(locals())
    params = {n: given[n] for n in ORDER}
    moms = {n: (given["m_" + n], given["v_" + n]) for n in ORDER}
    layers = mix_norm_g.shape[0]
    assert layers == 2, "the exchanges split the work of a chip's two cores by layer"
    xs = x[0]
    target = loss_target[0]
    chip = 2 * lax.axis_index("x") + lax.axis_index("y")
    core = lax.axis_index("c")

    c_idx = core.reshape(1).astype(jnp.int32)
    j_idx = chip.reshape(1).astype(jnp.int32)
    width = conv_w.shape[2]
    gathered = BIG + ("conv_w",)
    axes = [BIG_AXIS[n] for n in BIG] + [1]

    def placed(layer):
        arrays = [_place_shard(params[n], layer, BIG_AXIS[n], j_idx, MXU_DTYPE, name=f"place_{n}_l{layer}") for n in BIG]
        mine = lax.dynamic_slice_in_dim(conv_w, layer, 1, axis=0)
        arrays.append(lax.dynamic_update_slice_in_dim(jnp.zeros((1, CONV_K, width * N_CHIPS), F32), mine, chip * width, 2))
        return arrays

    weights = [dict(zip(gathered, _ride_alone(_gather_riders(placed(0), axes, 0)[2:], name="gather_weights_l0"))), None]
    w = {n: params[n] for n in ("mix_norm_g", "b_gate", "conv_b", "sgu_ln_g", "sgu_ln_b", "ffn_norm_g")}
    tril = jnp.tril(jnp.ones((CHUNK, CHUNK), dtype=bool))
    w["tril"] = tril
    w["wtril"] = _mx(jnp.where(tril, sgu_w, 0.0))
    w["wtril_t"] = w["wtril"].transpose(0, 1, 3, 2)
    w["bias_full"] = jnp.repeat(sgu_b.transpose(0, 2, 1), CHUNK, axis=2)
    w["qg"] = jnp.tile(q_norm_g, (1, HEADS))[:, None, :]
    w["kg"] = jnp.tile(k_norm_g, (1, HEADS))[:, None, :]
    lane = jnp.arange(HEADS * HEAD_DIM) // HEAD_DIM
    w["headavg"] = _mx(jnp.where(lane[:, None] == lane[None, :], 1.0 / HEAD_DIM, 0.0))
    pos = jnp.arange(ATT_BLOCK)
    w["later"] = _mx(jnp.where(pos[None, :] > pos[:, None], 1.0, 0.0))
    w["earlier"] = _mx(jnp.where(pos[None, :] < pos[:, None], 1.0, 0.0))

    saved = []
    act = xs
    for l in range(layers):
        coming = (placed(l + 1), axes, l + 1) if l + 1 < layers else None
        act, sv, arrived = _layer_forward(act, w, weights[l], l, coming)
        saved.append(sv)
        if coming:
            weights[l + 1] = dict(zip(gathered, arrived))
    loss_part, dact = _loss_head(act, target, name="loss_head")
    loss = lax.psum(jnp.sum(loss_part), ("x", "y", "c"))
    grads = [None] * layers
    chain = None
    make_chain = lambda large, layer: _ReduceChain(large, layer, layers, j_idx, c_idx, None)
    for l in reversed(range(layers)):
        dact, grads[l], chain = _layer_backward(dact, w, weights[l], saved[l], l, chain, make_chain, l == 0)
    grad_x = dact[None]
    local = {n: jnp.stack([grads[l][n] for l in range(layers)]) for n in SMALL}
    full = dict(zip(BIG, chain.finals))

    summed = _all_reduce_small([local[n] for n in SMALL], name="grads_all_reduce_small")
    for n, gsum in zip(SMALL, summed):
        full[n] = gsum
    full["conv_w"] = lax.dynamic_slice_in_dim(full["conv_w"], chip * width, width, axis=2)

    out = {}
    for n in BIG:
        out[n] = _adamw(params[n], full[n], *moms[n], name=f"adamw_{n}")
    small = _adamw_small([params[n] for n in SMALL], [full[n] for n in SMALL], [moms[n][0] for n in SMALL],
                         [moms[n][1] for n in SMALL], name="adamw_small")
    for n, triple in zip(SMALL, small):
        out[n] = triple
    return (loss, grad_x, *[full[n] for n in ORDER], *[out[n][0] for n in ORDER], *[out[n][1] for n in ORDER],
            *[out[n][2] for n in ORDER])
```
